```python
import jax, jax.numpy as jnp
from jax import lax
import numpy as np

D_MODEL = 1024
BATCH = 8
SEQ = 8192
DEPTH = 1

D_MIX = D_MODEL
W_CONV = D_MIX // 2
W_LRU = D_MIX - W_CONV
N_CONV_GROUPS = 8
N_LRU_HEADS = 8
LRU_HEAD_DIM = W_LRU // N_LRU_HEADS
CONV_WIDTH = 31
LRU_CONV_WIDTH = 4
LRU_C = 8.0
D_FF = 2816
FFN_RES_SCALE = 0.5
RMS_EPS = 1e-6
LN_EPS = 1e-5

kernel_name = "macaron_conformer_conv_rglru_hybrid"


def rmsnorm(x, g):
    xf = x.astype(jnp.float32)
    y = xf * lax.rsqrt(jnp.mean(xf * xf, axis=-1, keepdims=True) + RMS_EPS)
    return (y * g.astype(jnp.float32)).astype(x.dtype)


def layernorm(x, g, b):
    xf = x.astype(jnp.float32)
    mu = jnp.mean(xf, axis=-1, keepdims=True)
    xc = xf - mu
    var = jnp.mean(xc * xc, axis=-1, keepdims=True)
    y = xc * lax.rsqrt(var + LN_EPS)
    return (y * g.astype(jnp.float32) + b.astype(jnp.float32)).astype(x.dtype)


def swiglu_ffn(x, w_gate, w_up, w_down):
    return (jax.nn.silu(x @ w_gate) * (x @ w_up)) @ w_down


def causal_depthwise_conv(x, w, b):
    k = w.shape[0]
    c = x.shape[-1]
    out = lax.conv_general_dilated(
        x, w[:, None, :].astype(x.dtype), window_strides=(1,), padding=[(k - 1, 0)],
        dimension_numbers=("NWC", "WIO", "NWC"), feature_group_count=c)
    return out + b.astype(x.dtype)


def block_diag_linear(x, w, b):
    bsz, s, _ = x.shape
    xh = x.reshape(bsz, s, N_LRU_HEADS, LRU_HEAD_DIM)
    y = jnp.einsum("bshi,hij->bshj", xh, w.astype(x.dtype))
    return y.reshape(bsz, s, W_LRU) + b.astype(x.dtype)


def rg_lru(x, w_a, b_a, w_x, b_x, lam):
    xf = x.astype(jnp.float32)
    r = jax.nn.sigmoid(block_diag_linear(x, w_a, b_a).astype(jnp.float32))
    i = jax.nn.sigmoid(block_diag_linear(x, w_x, b_x).astype(jnp.float32))
    log_a = -LRU_C * r * jax.nn.softplus(-lam.astype(jnp.float32))
    a = jnp.exp(log_a)
    mult = jnp.sqrt(-jnp.expm1(2.0 * log_a))
    bterm = mult * (i * xf)

    def combine(lhs, rhs):
        a1, b1 = lhs
        a2, b2 = rhs
        return a1 * a2, a2 * b1 + b2

    _, h = lax.associative_scan(combine, (a, bterm), axis=1)
    return h.astype(x.dtype)


def hybrid_mixer(h, w_in, conv_dw, conv_dw_bias, conv_ln_g, conv_ln_b,
                 lru_conv_w, lru_conv_b, lru_w_a, lru_b_a, lru_w_x, lru_b_x, lru_lambda, w_out):
    z = h @ w_in
    c_val, c_gate, r_x, r_gate = jnp.split(
        z, [W_CONV, 2 * W_CONV, 2 * W_CONV + W_LRU], axis=-1)
    u = c_val * jax.nn.sigmoid(c_gate)
    u = causal_depthwise_conv(u, conv_dw, conv_dw_bias)
    u = jax.nn.silu(layernorm(u, conv_ln_g, conv_ln_b))
    xr = causal_depthwise_conv(r_x, lru_conv_w, lru_conv_b)
    yr = rg_lru(xr, lru_w_a, lru_b_a, lru_w_x, lru_b_x, lru_lambda)
    yr = yr * jax.nn.gelu(r_gate, approximate=True)
    return jnp.concatenate([u, yr], axis=-1) @ w_out


def _fwd_setup_inputs(seed: int = 0) -> dict:
    key = jax.random.key(seed)
    ks = jax.random.split(key, 32)
    f32 = jnp.float32
    nrm = lambda k, shape, scale: jax.random.normal(k, shape, f32) * scale
    gain = lambda k, n: 1.0 + 0.02 * jax.random.normal(k, (n,), f32)
    d_in = 2 * W_CONV + 2 * W_LRU
    u = jax.random.uniform(ks[20], (W_LRU,), f32, 0.9, 0.999)
    a0 = u ** (1.0 / LRU_C)
    lru_lambda = jnp.log(a0) - jnp.log1p(-a0)
    return {
        "x": jax.random.normal(ks[0], (BATCH, SEQ, D_MODEL), f32),
        "ffn1_norm": gain(ks[1], D_MODEL),
        "ffn1_w_gate": nrm(ks[2], (D_MODEL, D_FF), D_MODEL ** -0.5),
        "ffn1_w_up": nrm(ks[3], (D_MODEL, D_FF), D_MODEL ** -0.5),
        "ffn1_w_down": nrm(ks[4], (D_FF, D_MODEL), D_FF ** -0.5),
        "mix_norm": gain(ks[5], D_MODEL),
        "w_in": nrm(ks[6], (D_MODEL, d_in), D_MODEL ** -0.5),
        "conv_dw": nrm(ks[7], (CONV_WIDTH, W_CONV), CONV_WIDTH ** -0.5),
        "conv_dw_bias": nrm(ks[8], (W_CONV,), 0.02),
        "conv_ln_g": gain(ks[9], W_CONV),
        "conv_ln_b": nrm(ks[10], (W_CONV,), 0.02),
        "lru_conv_w": nrm(ks[11], (LRU_CONV_WIDTH, W_LRU), LRU_CONV_WIDTH ** -0.5),
        "lru_conv_b": nrm(ks[12], (W_LRU,), 0.02),
        "lru_w_a": nrm(ks[13], (N_LRU_HEADS, LRU_HEAD_DIM, LRU_HEAD_DIM), LRU_HEAD_DIM ** -0.5),
        "lru_b_a": nrm(ks[14], (W_LRU,), 0.02),
        "lru_w_x": nrm(ks[15], (N_LRU_HEADS, LRU_HEAD_DIM, LRU_HEAD_DIM), LRU_HEAD_DIM ** -0.5),
        "lru_b_x": nrm(ks[16], (W_LRU,), 0.02),
        "lru_lambda": lru_lambda,
        "w_out": nrm(ks[17], (D_MIX, D_MODEL), D_MIX ** -0.5),
        "ffn2_norm": gain(ks[18], D_MODEL),
        "ffn2_w_gate": nrm(ks[19], (D_MODEL, D_FF), D_MODEL ** -0.5),
        "ffn2_w_up": nrm(ks[21], (D_MODEL, D_FF), D_MODEL ** -0.5),
        "ffn2_w_down": nrm(ks[22], (D_FF, D_MODEL), D_FF ** -0.5),
        "final_norm": gain(ks[23], D_MODEL),
    }


def _fwd_reference(x, ffn1_norm, ffn1_w_gate, ffn1_w_up, ffn1_w_down, mix_norm, w_in,
              conv_dw, conv_dw_bias, conv_ln_g, conv_ln_b, lru_conv_w, lru_conv_b,
              lru_w_a, lru_b_a, lru_w_x, lru_b_x, lru_lambda, w_out,
              ffn2_norm, ffn2_w_gate, ffn2_w_up, ffn2_w_down, final_norm):
    for _ in range(DEPTH):
        x = x + FFN_RES_SCALE * swiglu_ffn(rmsnorm(x, ffn1_norm), ffn1_w_gate, ffn1_w_up, ffn1_w_down)
        x = x + hybrid_mixer(rmsnorm(x, mix_norm), w_in, conv_dw, conv_dw_bias, conv_ln_g, conv_ln_b,
                             lru_conv_w, lru_conv_b, lru_w_a, lru_b_a, lru_w_x, lru_b_x,
                             lru_lambda, w_out)
        x = x + FFN_RES_SCALE * swiglu_ffn(rmsnorm(x, ffn2_norm), ffn2_w_gate, ffn2_w_up, ffn2_w_down)
    return rmsnorm(x, final_norm)


import jax as _jax
import jax.numpy as _jnp

TWIN_FORMAT = 'train_step'
FWD_PARAMS = ['x', 'ffn1_norm', 'ffn1_w_gate', 'ffn1_w_up', 'ffn1_w_down', 'mix_norm', 'w_in', 'conv_dw', 'conv_dw_bias', 'conv_ln_g', 'conv_ln_b', 'lru_conv_w', 'lru_conv_b', 'lru_w_a', 'lru_b_a', 'lru_w_x', 'lru_b_x', 'lru_lambda', 'w_out', 'ffn2_norm', 'ffn2_w_gate', 'ffn2_w_up', 'ffn2_w_down', 'final_norm']
TWIN_WEIGHTS = ['ffn1_norm', 'ffn1_w_gate', 'ffn1_w_up', 'ffn1_w_down', 'mix_norm', 'w_in', 'conv_dw', 'conv_dw_bias', 'conv_ln_g', 'conv_ln_b', 'lru_conv_w', 'lru_conv_b', 'lru_w_a', 'lru_b_a', 'lru_w_x', 'lru_b_x', 'lru_lambda', 'w_out', 'ffn2_norm', 'ffn2_w_gate', 'ffn2_w_up', 'ffn2_w_down', 'final_norm']
TWIN_DIFF_INPUT = 'x'
TWIN_INPUTS = ['x', 'ffn1_norm', 'ffn1_w_gate', 'ffn1_w_up', 'ffn1_w_down', 'mix_norm', 'w_in', 'conv_dw', 'conv_dw_bias', 'conv_ln_g', 'conv_ln_b', 'lru_conv_w', 'lru_conv_b', 'lru_w_a', 'lru_b_a', 'lru_w_x', 'lru_b_x', 'lru_lambda', 'w_out', 'ffn2_norm', 'ffn2_w_gate', 'ffn2_w_up', 'ffn2_w_down', 'final_norm', 'loss_target', 'm_ffn1_norm', 'm_ffn1_w_gate', 'm_ffn1_w_up', 'm_ffn1_w_down', 'm_mix_norm', 'm_w_in', 'm_conv_dw', 'm_conv_dw_bias', 'm_conv_ln_g', 'm_conv_ln_b', 'm_lru_conv_w', 'm_lru_conv_b', 'm_lru_w_a', 'm_lru_b_a', 'm_lru_w_x', 'm_lru_b_x', 'm_lru_lambda', 'm_w_out', 'm_ffn2_norm', 'm_ffn2_w_gate', 'm_ffn2_w_up', 'm_ffn2_w_down', 'm_final_norm', 'v_ffn1_norm', 'v_ffn1_w_gate', 'v_ffn1_w_up', 'v_ffn1_w_down', 'v_mix_norm', 'v_w_in', 'v_conv_dw', 'v_conv_dw_bias', 'v_conv_ln_g', 'v_conv_ln_b', 'v_lru_conv_w', 'v_lru_conv_b', 'v_lru_w_a', 'v_lru_b_a', 'v_lru_w_x', 'v_lru_b_x', 'v_lru_lambda', 'v_w_out', 'v_ffn2_norm', 'v_ffn2_w_gate', 'v_ffn2_w_up', 'v_ffn2_w_down', 'v_final_norm']
TWIN_OUTPUTS = ['loss', 'grad_x', 'grad_ffn1_norm', 'grad_ffn1_w_gate', 'grad_ffn1_w_up', 'grad_ffn1_w_down', 'grad_mix_norm', 'grad_w_in', 'grad_conv_dw', 'grad_conv_dw_bias', 'grad_conv_ln_g', 'grad_conv_ln_b', 'grad_lru_conv_w', 'grad_lru_conv_b', 'grad_lru_w_a', 'grad_lru_b_a', 'grad_lru_w_x', 'grad_lru_b_x', 'grad_lru_lambda', 'grad_w_out', 'grad_ffn2_norm', 'grad_ffn2_w_gate', 'grad_ffn2_w_up', 'grad_ffn2_w_down', 'grad_final_norm', 'delta_ffn1_norm', 'delta_ffn1_w_gate', 'delta_ffn1_w_up', 'delta_ffn1_w_down', 'delta_mix_norm', 'delta_w_in', 'delta_conv_dw', 'delta_conv_dw_bias', 'delta_conv_ln_g', 'delta_conv_ln_b', 'delta_lru_conv_w', 'delta_lru_conv_b', 'delta_lru_w_a', 'delta_lru_b_a', 'delta_lru_w_x', 'delta_lru_b_x', 'delta_lru_lambda', 'delta_w_out', 'delta_ffn2_norm', 'delta_ffn2_w_gate', 'delta_ffn2_w_up', 'delta_ffn2_w_down', 'delta_final_norm', 'new_m_ffn1_norm', 'new_m_ffn1_w_gate', 'new_m_ffn1_w_up', 'new_m_ffn1_w_down', 'new_m_mix_norm', 'new_m_w_in', 'new_m_conv_dw', 'new_m_conv_dw_bias', 'new_m_conv_ln_g', 'new_m_conv_ln_b', 'new_m_lru_conv_w', 'new_m_lru_conv_b', 'new_m_lru_w_a', 'new_m_lru_b_a', 'new_m_lru_w_x', 'new_m_lru_b_x', 'new_m_lru_lambda', 'new_m_w_out', 'new_m_ffn2_norm', 'new_m_ffn2_w_gate', 'new_m_ffn2_w_up', 'new_m_ffn2_w_down', 'new_m_final_norm', 'new_v_ffn1_norm', 'new_v_ffn1_w_gate', 'new_v_ffn1_w_up', 'new_v_ffn1_w_down', 'new_v_mix_norm', 'new_v_w_in', 'new_v_conv_dw', 'new_v_conv_dw_bias', 'new_v_conv_ln_g', 'new_v_conv_ln_b', 'new_v_lru_conv_w', 'new_v_lru_conv_b', 'new_v_lru_w_a', 'new_v_lru_b_a', 'new_v_lru_w_x', 'new_v_lru_b_x', 'new_v_lru_lambda', 'new_v_w_out', 'new_v_ffn2_norm', 'new_v_ffn2_w_gate', 'new_v_ffn2_w_up', 'new_v_ffn2_w_down', 'new_v_final_norm']
TWIN_LEAF_KINDS = {'loss': 'loss', 'grad_x': 'grad_x', 'grad_ffn1_norm': 'grad_w', 'grad_ffn1_w_gate': 'grad_w', 'grad_ffn1_w_up': 'grad_w', 'grad_ffn1_w_down': 'grad_w', 'grad_mix_norm': 'grad_w', 'grad_w_in': 'grad_w', 'grad_conv_dw': 'grad_w', 'grad_conv_dw_bias': 'grad_w', 'grad_conv_ln_g': 'grad_w', 'grad_conv_ln_b': 'grad_w', 'grad_lru_conv_w': 'grad_w', 'grad_lru_conv_b': 'grad_w', 'grad_lru_w_a': 'grad_w', 'grad_lru_b_a': 'grad_w', 'grad_lru_w_x': 'grad_w', 'grad_lru_b_x': 'grad_w', 'grad_lru_lambda': 'grad_w', 'grad_w_out': 'grad_w', 'grad_ffn2_norm': 'grad_w', 'grad_ffn2_w_gate': 'grad_w', 'grad_ffn2_w_up': 'grad_w', 'grad_ffn2_w_down': 'grad_w', 'grad_final_norm': 'grad_w', 'delta_ffn1_norm': 'delta_w', 'delta_ffn1_w_gate': 'delta_w', 'delta_ffn1_w_up': 'delta_w', 'delta_ffn1_w_down': 'delta_w', 'delta_mix_norm': 'delta_w', 'delta_w_in': 'delta_w', 'delta_conv_dw': 'delta_w', 'delta_conv_dw_bias': 'delta_w', 'delta_conv_ln_g': 'delta_w', 'delta_conv_ln_b': 'delta_w', 'delta_lru_conv_w': 'delta_w', 'delta_lru_conv_b': 'delta_w', 'delta_lru_w_a': 'delta_w', 'delta_lru_b_a': 'delta_w', 'delta_lru_w_x': 'delta_w', 'delta_lru_b_x': 'delta_w', 'delta_lru_lambda': 'delta_w', 'delta_w_out': 'delta_w', 'delta_ffn2_norm': 'delta_w', 'delta_ffn2_w_gate': 'delta_w', 'delta_ffn2_w_up': 'delta_w', 'delta_ffn2_w_down': 'delta_w', 'delta_final_norm': 'delta_w', 'new_m_ffn1_norm': 'new_m', 'new_m_ffn1_w_gate': 'new_m', 'new_m_ffn1_w_up': 'new_m', 'new_m_ffn1_w_down': 'new_m', 'new_m_mix_norm': 'new_m', 'new_m_w_in': 'new_m', 'new_m_conv_dw': 'new_m', 'new_m_conv_dw_bias': 'new_m', 'new_m_conv_ln_g': 'new_m', 'new_m_conv_ln_b': 'new_m', 'new_m_lru_conv_w': 'new_m', 'new_m_lru_conv_b': 'new_m', 'new_m_lru_w_a': 'new_m', 'new_m_lru_b_a': 'new_m', 'new_m_lru_w_x': 'new_m', 'new_m_lru_b_x': 'new_m', 'new_m_lru_lambda': 'new_m', 'new_m_w_out': 'new_m', 'new_m_ffn2_norm': 'new_m', 'new_m_ffn2_w_gate': 'new_m', 'new_m_ffn2_w_up': 'new_m', 'new_m_ffn2_w_down': 'new_m', 'new_m_final_norm': 'new_m', 'new_v_ffn1_norm': 'new_v', 'new_v_ffn1_w_gate': 'new_v', 'new_v_ffn1_w_up': 'new_v', 'new_v_ffn1_w_down': 'new_v', 'new_v_mix_norm': 'new_v', 'new_v_w_in': 'new_v', 'new_v_conv_dw': 'new_v', 'new_v_conv_dw_bias': 'new_v', 'new_v_conv_ln_g': 'new_v', 'new_v_conv_ln_b': 'new_v', 'new_v_lru_conv_w': 'new_v', 'new_v_lru_conv_b': 'new_v', 'new_v_lru_w_a': 'new_v', 'new_v_lru_b_a': 'new_v', 'new_v_lru_w_x': 'new_v', 'new_v_lru_b_x': 'new_v', 'new_v_lru_lambda': 'new_v', 'new_v_w_out': 'new_v', 'new_v_ffn2_norm': 'new_v', 'new_v_ffn2_w_gate': 'new_v', 'new_v_ffn2_w_up': 'new_v', 'new_v_ffn2_w_down': 'new_v', 'new_v_final_norm': 'new_v'}


def _forward(args):
    return _fwd_reference(*[args[k] for k in FWD_PARAMS])


def _output_shape():
    out = _jax.eval_shape(lambda: _forward(_fwd_setup_inputs(0)))
    return out.shape, out.dtype

N_MICROBATCH = 1
ADAM_LR = 0.001
ADAM_B1 = 0.9
ADAM_B2 = 0.999
ADAM_EPS = 1e-08
ADAM_WD = 0.01
ADAM_STEP = 10
PER_EXAMPLE_BATCH_AXIS = {'x': 0, 'loss_target': 0}
SHARED_INPUTS = []
_WEIGHT_DTYPES = {'ffn1_norm': _jnp.float32, 'ffn1_w_gate': _jnp.float32, 'ffn1_w_up': _jnp.float32, 'ffn1_w_down': _jnp.float32, 'mix_norm': _jnp.float32, 'w_in': _jnp.float32, 'conv_dw': _jnp.float32, 'conv_dw_bias': _jnp.float32, 'conv_ln_g': _jnp.float32, 'conv_ln_b': _jnp.float32, 'lru_conv_w': _jnp.float32, 'lru_conv_b': _jnp.float32, 'lru_w_a': _jnp.float32, 'lru_b_a': _jnp.float32, 'lru_w_x': _jnp.float32, 'lru_b_x': _jnp.float32, 'lru_lambda': _jnp.float32, 'w_out': _jnp.float32, 'ffn2_norm': _jnp.float32, 'ffn2_w_gate': _jnp.float32, 'ffn2_w_up': _jnp.float32, 'ffn2_w_down': _jnp.float32, 'final_norm': _jnp.float32}
MOMENT_SCALE = {'ffn1_norm': 1.247071e-01, 'ffn1_w_gate': 4.813794e-02, 'ffn1_w_up': 4.649311e-02, 'ffn1_w_down': 7.693321e-02, 'mix_norm': 1.613035e-01, 'w_in': 1.013854e-01, 'conv_dw': 1.391150e-01, 'conv_dw_bias': 2.979136e-01, 'conv_ln_g': 1.973687e-01, 'conv_ln_b': 2.001870e-01, 'lru_conv_w': 1.217728e-01, 'lru_conv_b': 1.528440e+00, 'lru_w_a': 4.320167e-02, 'lru_b_a': 2.354335e-02, 'lru_w_x': 7.677005e-02, 'lru_b_x': 3.705931e-02, 'lru_lambda': 4.241841e-02, 'w_out': 1.211505e-01, 'ffn2_norm': 9.576240e-02, 'ffn2_w_gate': 3.889965e-02, 'ffn2_w_up': 3.778531e-02, 'ffn2_w_down': 6.255123e-02, 'final_norm': 6.386987e+01}


def _to_microbatches(a, axis):
    t = _jnp.moveaxis(a, axis, 0)
    t = t.reshape((N_MICROBATCH, t.shape[0] // N_MICROBATCH) + t.shape[1:])
    return _jnp.moveaxis(t, 1, axis + 1)


def setup_inputs(seed: int = 0) -> dict:
    inp = _fwd_setup_inputs(seed)
    key = _jax.random.fold_in(_jax.random.key(seed), 7919)
    shape, _ = _output_shape()
    out = dict(inp)
    out["loss_target"] = _jax.random.normal(_jax.random.fold_in(key, 0), shape, _jnp.float32)
    for i, name in enumerate(TWIN_WEIGHTS):
        w = inp[name].astype(_jnp.float32)
        if MOMENT_SCALE is None:
            s = _jnp.sqrt(_jnp.mean(_jnp.square(w)) + 1e-30)
        else:
            s = MOMENT_SCALE[name]
        km, kv = _jax.random.split(_jax.random.fold_in(key, i + 1))
        out[name] = w
        out["m_" + name] = s * _jax.random.normal(km, w.shape, _jnp.float32)
        out["v_" + name] = (s * s) * _jax.random.uniform(kv, w.shape, _jnp.float32, 0.5, 1.5)
    if N_MICROBATCH > 1:
        for name, axis in PER_EXAMPLE_BATCH_AXIS.items():
            out[name] = _to_microbatches(out[name], axis)
    return {'x': out['x'], 'ffn1_norm': out['ffn1_norm'], 'ffn1_w_gate': out['ffn1_w_gate'], 'ffn1_w_up': out['ffn1_w_up'], 'ffn1_w_down': out['ffn1_w_down'], 'mix_norm': out['mix_norm'], 'w_in': out['w_in'], 'conv_dw': out['conv_dw'], 'conv_dw_bias': out['conv_dw_bias'], 'conv_ln_g': out['conv_ln_g'], 'conv_ln_b': out['conv_ln_b'], 'lru_conv_w': out['lru_conv_w'], 'lru_conv_b': out['lru_conv_b'], 'lru_w_a': out['lru_w_a'], 'lru_b_a': out['lru_b_a'], 'lru_w_x': out['lru_w_x'], 'lru_b_x': out['lru_b_x'], 'lru_lambda': out['lru_lambda'], 'w_out': out['w_out'], 'ffn2_norm': out['ffn2_norm'], 'ffn2_w_gate': out['ffn2_w_gate'], 'ffn2_w_up': out['ffn2_w_up'], 'ffn2_w_down': out['ffn2_w_down'], 'final_norm': out['final_norm'], 'loss_target': out['loss_target'], 'm_ffn1_norm': out['m_ffn1_norm'], 'm_ffn1_w_gate': out['m_ffn1_w_gate'], 'm_ffn1_w_up': out['m_ffn1_w_up'], 'm_ffn1_w_down': out['m_ffn1_w_down'], 'm_mix_norm': out['m_mix_norm'], 'm_w_in': out['m_w_in'], 'm_conv_dw': out['m_conv_dw'], 'm_conv_dw_bias': out['m_conv_dw_bias'], 'm_conv_ln_g': out['m_conv_ln_g'], 'm_conv_ln_b': out['m_conv_ln_b'], 'm_lru_conv_w': out['m_lru_conv_w'], 'm_lru_conv_b': out['m_lru_conv_b'], 'm_lru_w_a': out['m_lru_w_a'], 'm_lru_b_a': out['m_lru_b_a'], 'm_lru_w_x': out['m_lru_w_x'], 'm_lru_b_x': out['m_lru_b_x'], 'm_lru_lambda': out['m_lru_lambda'], 'm_w_out': out['m_w_out'], 'm_ffn2_norm': out['m_ffn2_norm'], 'm_ffn2_w_gate': out['m_ffn2_w_gate'], 'm_ffn2_w_up': out['m_ffn2_w_up'], 'm_ffn2_w_down': out['m_ffn2_w_down'], 'm_final_norm': out['m_final_norm'], 'v_ffn1_norm': out['v_ffn1_norm'], 'v_ffn1_w_gate': out['v_ffn1_w_gate'], 'v_ffn1_w_up': out['v_ffn1_w_up'], 'v_ffn1_w_down': out['v_ffn1_w_down'], 'v_mix_norm': out['v_mix_norm'], 'v_w_in': out['v_w_in'], 'v_conv_dw': out['v_conv_dw'], 'v_conv_dw_bias': out['v_conv_dw_bias'], 'v_conv_ln_g': out['v_conv_ln_g'], 'v_conv_ln_b': out['v_conv_ln_b'], 'v_lru_conv_w': out['v_lru_conv_w'], 'v_lru_conv_b': out['v_lru_conv_b'], 'v_lru_w_a': out['v_lru_w_a'], 'v_lru_b_a': out['v_lru_b_a'], 'v_lru_w_x': out['v_lru_w_x'], 'v_lru_b_x': out['v_lru_b_x'], 'v_lru_lambda': out['v_lru_lambda'], 'v_w_out': out['v_w_out'], 'v_ffn2_norm': out['v_ffn2_norm'], 'v_ffn2_w_gate': out['v_ffn2_w_gate'], 'v_ffn2_w_up': out['v_ffn2_w_up'], 'v_ffn2_w_down': out['v_ffn2_w_down'], 'v_final_norm': out['v_final_norm']}


def _loss(weights, diff, rest, loss_target):
    with _jax.named_scope("forward"):
        args = {**rest, TWIN_DIFF_INPUT: diff, **{k: w.astype(_WEIGHT_DTYPES[k]) for k, w in weights.items()}}
        y = _forward(args)
    with _jax.named_scope("loss_head"):
        err = _jnp.square(y.astype(_jnp.float32) - loss_target)
        return 0.5 * _jnp.sum(_jnp.mean(err, axis=-1)) if err.ndim else 0.5 * err


def _adamw(w, g, m, v):
    m = ADAM_B1 * m + (1.0 - ADAM_B1) * g
    v = ADAM_B2 * v + (1.0 - ADAM_B2) * _jnp.square(g)
    m_hat = m / (1.0 - ADAM_B1 ** ADAM_STEP)
    v_hat = v / (1.0 - ADAM_B2 ** ADAM_STEP)
    delta = -ADAM_LR * (m_hat / (_jnp.sqrt(v_hat) + ADAM_EPS) + ADAM_WD * w)
    return delta, m, v


def reference(x, ffn1_norm, ffn1_w_gate, ffn1_w_up, ffn1_w_down, mix_norm, w_in, conv_dw, conv_dw_bias, conv_ln_g, conv_ln_b, lru_conv_w, lru_conv_b, lru_w_a, lru_b_a, lru_w_x, lru_b_x, lru_lambda, w_out, ffn2_norm, ffn2_w_gate, ffn2_w_up, ffn2_w_down, final_norm, loss_target, m_ffn1_norm, m_ffn1_w_gate, m_ffn1_w_up, m_ffn1_w_down, m_mix_norm, m_w_in, m_conv_dw, m_conv_dw_bias, m_conv_ln_g, m_conv_ln_b, m_lru_conv_w, m_lru_conv_b, m_lru_w_a, m_lru_b_a, m_lru_w_x, m_lru_b_x, m_lru_lambda, m_w_out, m_ffn2_norm, m_ffn2_w_gate, m_ffn2_w_up, m_ffn2_w_down, m_final_norm, v_ffn1_norm, v_ffn1_w_gate, v_ffn1_w_up, v_ffn1_w_down, v_mix_norm, v_w_in, v_conv_dw, v_conv_dw_bias, v_conv_ln_g, v_conv_ln_b, v_lru_conv_w, v_lru_conv_b, v_lru_w_a, v_lru_b_a, v_lru_w_x, v_lru_b_x, v_lru_lambda, v_w_out, v_ffn2_norm, v_ffn2_w_gate, v_ffn2_w_up, v_ffn2_w_down, v_final_norm):
    given = dict(x=x, ffn1_norm=ffn1_norm, ffn1_w_gate=ffn1_w_gate, ffn1_w_up=ffn1_w_up, ffn1_w_down=ffn1_w_down, mix_norm=mix_norm, w_in=w_in, conv_dw=conv_dw, conv_dw_bias=conv_dw_bias, conv_ln_g=conv_ln_g, conv_ln_b=conv_ln_b, lru_conv_w=lru_conv_w, lru_conv_b=lru_conv_b, lru_w_a=lru_w_a, lru_b_a=lru_b_a, lru_w_x=lru_w_x, lru_b_x=lru_b_x, lru_lambda=lru_lambda, w_out=w_out, ffn2_norm=ffn2_norm, ffn2_w_gate=ffn2_w_gate, ffn2_w_up=ffn2_w_up, ffn2_w_down=ffn2_w_down, final_norm=final_norm, loss_target=loss_target, m_ffn1_norm=m_ffn1_norm, m_ffn1_w_gate=m_ffn1_w_gate, m_ffn1_w_up=m_ffn1_w_up, m_ffn1_w_down=m_ffn1_w_down, m_mix_norm=m_mix_norm, m_w_in=m_w_in, m_conv_dw=m_conv_dw, m_conv_dw_bias=m_conv_dw_bias, m_conv_ln_g=m_conv_ln_g, m_conv_ln_b=m_conv_ln_b, m_lru_conv_w=m_lru_conv_w, m_lru_conv_b=m_lru_conv_b, m_lru_w_a=m_lru_w_a, m_lru_b_a=m_lru_b_a, m_lru_w_x=m_lru_w_x, m_lru_b_x=m_lru_b_x, m_lru_lambda=m_lru_lambda, m_w_out=m_w_out, m_ffn2_norm=m_ffn2_norm, m_ffn2_w_gate=m_ffn2_w_gate, m_ffn2_w_up=m_ffn2_w_up, m_ffn2_w_down=m_ffn2_w_down, m_final_norm=m_final_norm, v_ffn1_norm=v_ffn1_norm, v_ffn1_w_gate=v_ffn1_w_gate, v_ffn1_w_up=v_ffn1_w_up, v_ffn1_w_down=v_ffn1_w_down, v_mix_norm=v_mix_norm, v_w_in=v_w_in, v_conv_dw=v_conv_dw, v_conv_dw_bias=v_conv_dw_bias, v_conv_ln_g=v_conv_ln_g, v_conv_ln_b=v_conv_ln_b, v_lru_conv_w=v_lru_conv_w, v_lru_conv_b=v_lru_conv_b, v_lru_w_a=v_lru_w_a, v_lru_b_a=v_lru_b_a, v_lru_w_x=v_lru_w_x, v_lru_b_x=v_lru_b_x, v_lru_lambda=v_lru_lambda, v_w_out=v_w_out, v_ffn2_norm=v_ffn2_norm, v_ffn2_w_gate=v_ffn2_w_gate, v_ffn2_w_up=v_ffn2_w_up, v_ffn2_w_down=v_ffn2_w_down, v_final_norm=v_final_norm)
    weights = {n: given[n] for n in TWIN_WEIGHTS}
    shared = {n: given[n] for n in SHARED_INPUTS}
    per_example = {n: given[n] for n in ['x']}
    grad_fn = _jax.value_and_grad(_loss, argnums=(0, 1))

    def one_microbatch(ex, loss_target):
        ex = dict(ex)
        diff = ex.pop(TWIN_DIFF_INPUT)
        return grad_fn(weights, diff, {**shared, **ex}, loss_target)

    if N_MICROBATCH == 1:
        loss, (grad_w, grad_x) = one_microbatch(per_example, given["loss_target"])
    else:
        def body(carry, xs):
            loss_sum, grad_sum = carry
            l_k, (gw_k, gx_k) = one_microbatch(xs[0], xs[1])
            with _jax.named_scope("update"):
                return (loss_sum + l_k, _jax.tree.map(_jnp.add, grad_sum, gw_k)), gx_k

        init = (_jnp.zeros((), _jnp.float32), _jax.tree.map(_jnp.zeros_like, weights))
        (loss, grad_w), grad_x = _jax.lax.scan(body, init, (per_example, given["loss_target"]))
    with _jax.named_scope("update"):
        delta_w, new_m, new_v = {}, {}, {}
        for n in TWIN_WEIGHTS:
            delta_w[n], new_m[n], new_v[n] = _adamw(weights[n], grad_w[n], given["m_" + n], given["v_" + n])
    return (loss, grad_x, *[grad_w[n] for n in TWIN_WEIGHTS], *[delta_w[n] for n in TWIN_WEIGHTS],
            *[new_m[n] for n in TWIN_WEIGHTS], *[new_v[n] for n in TWIN_WEIGHTS])
```

```python
import functools
import math

import jax
import jax.numpy as jnp
from jax import lax
from jax.experimental import pallas as pl
from jax.experimental.pallas import tpu as pltpu

F32 = jnp.float32
BF16 = jnp.bfloat16
MESH = pl.DeviceIdType.MESH

RMS_EPS = 1e-6
LN_EPS = 1e-5
LRU_C = 8.0
FFN_RES_SCALE = 0.5
ADAM_LR = 0.001
ADAM_B1 = 0.9
ADAM_B2 = 0.999
ADAM_EPS = 1e-08
ADAM_WD = 0.01
ADAM_STEP = 10

LANES = 128
SUBLANES = 8
CONV_HALO = 32
LRU_HALO = 8
ROW_CHUNK = 64
VMEM_LIMIT = 56 * 1024 * 1024
N_CHIPS = 4
N_DEV = 8
TOK_TILE = 1024
BWD_TILE = 512
CONV_TILE = 512
LRU_TILE = 1024


def _dot(a, b):
    return jnp.dot(a, b, preferred_element_type=F32)


def _dot_nt(a, b):
    return lax.dot_general(a, b, (((1,), (1,)), ((), ())), preferred_element_type=F32)


def _dot_tn(a, b):
    return lax.dot_general(a, b, (((0,), (0,)), ((), ())), preferred_element_type=F32)


def _tile(n, pref, mult=SUBLANES):
    for t in range(min(pref, n), 0, -1):
        if n % t == 0 and t % mult == 0:
            return t
    return n


def _params(*sem):
    return pltpu.CompilerParams(dimension_semantics=sem, vmem_limit_bytes=VMEM_LIMIT)


def _rms_stats(x):
    r = lax.rsqrt(jnp.mean(x * x, axis=-1, keepdims=True) + RMS_EPS)
    return x * r, r


def _rms_bwd(dh, xh, r, g):
    dxh = dh * g
    return r * (dxh - xh * jnp.mean(dxh * xh, axis=-1, keepdims=True))


def _colsum(v):
    return jnp.sum(v, axis=0, keepdims=True)


def _ffn_fwd(x, g, wcol, wrow, ig, iu, idn, name):
    T, D = x.shape
    ns, fs = wcol.shape[1], wcol.shape[3]
    tm = _tile(T, TOK_TILE)

    def body(x_ref, g_ref, wg_ref, wu_ref, wd_ref, y_ref, hb_ref, acc_ref):
        j = pl.program_id(1)

        @pl.when(j == 0)
        def _():
            xh, _ = _rms_stats(x_ref[...])
            hb_ref[...] = (xh * g_ref[...]).astype(BF16)
            acc_ref[...] = jnp.zeros_like(acc_ref)

        hb = hb_ref[...]
        a = _dot(hb, wg_ref[...])
        b = _dot(hb, wu_ref[...])
        p = (a * jax.nn.sigmoid(a) * b).astype(BF16)
        acc_ref[...] += _dot(p, wd_ref[...])

        @pl.when(j == ns - 1)
        def _():
            y_ref[...] = x_ref[...] + FFN_RES_SCALE * acc_ref[...]

    return pl.pallas_call(
        body, grid=(T // tm, ns),
        in_specs=[pl.BlockSpec((tm, D), lambda i, j: (i, 0)),
                  pl.BlockSpec((1, D), lambda i, j: (0, 0)),
                  pl.BlockSpec((None, None, D, fs), lambda i, j: (ig, j, 0, 0)),
                  pl.BlockSpec((None, None, D, fs), lambda i, j: (iu, j, 0, 0)),
                  pl.BlockSpec((None, None, fs, D), lambda i, j: (idn, j, 0, 0))],
        out_specs=pl.BlockSpec((tm, D), lambda i, j: (i, 0)),
        out_shape=jax.ShapeDtypeStruct((T, D), F32),
        scratch_shapes=[pltpu.VMEM((tm, D), BF16), pltpu.VMEM((tm, D), F32)],
        compiler_params=_params("parallel", "arbitrary"), name=name)(x, g, wcol, wcol, wrow)


def _ffn_bwd_tok(dy, x, g, wcol, wrow, ig, iu, idn, name):
    T, D = x.shape
    ns, fs = wcol.shape[1], wcol.shape[3]
    tm = _tile(T, BWD_TILE)

    def body(dy_ref, x_ref, g_ref, wg_ref, wu_ref, wd_ref,
             dx_ref, da_ref, db_ref, p_ref, hb_ref, dyh_ref, dg_ref, dh_ref):
        i, j = pl.program_id(0), pl.program_id(1)

        @pl.when((i == 0) & (j == 0))
        def _():
            dg_ref[...] = jnp.zeros_like(dg_ref)

        @pl.when(j == 0)
        def _():
            xh, _ = _rms_stats(x_ref[...])
            hb_ref[...] = (xh * g_ref[...]).astype(BF16)
            dyh_ref[...] = (FFN_RES_SCALE * dy_ref[...]).astype(BF16)
            dh_ref[...] = jnp.zeros_like(dh_ref)

        hb = hb_ref[...]
        wg, wu = wg_ref[...], wu_ref[...]
        a = _dot(hb, wg)
        b = _dot(hb, wu)
        dp = _dot_nt(dyh_ref[...], wd_ref[...])
        s = jax.nn.sigmoid(a)
        sl = a * s
        da = (dp * b * (s * (1.0 + a * (1.0 - s)))).astype(BF16)
        db = (dp * sl).astype(BF16)
        da_ref[...] = da
        db_ref[...] = db
        p_ref[...] = (sl * b).astype(BF16)
        dh_ref[...] += _dot_nt(da, wg) + _dot_nt(db, wu)

        @pl.when(j == ns - 1)
        def _():
            xh, r = _rms_stats(x_ref[...])
            dh = dh_ref[...]
            gv = g_ref[...]
            dx_ref[...] = dy_ref[...] + _rms_bwd(dh, xh, r, gv)
            dg_ref[...] += _colsum(dh * xh)

    tok = pl.BlockSpec((tm, D), lambda i, j: (i, 0))
    mid = pl.BlockSpec((None, tm, fs), lambda i, j: (j, i, 0))
    vec = pl.BlockSpec((1, D), lambda i, j: (0, 0))
    return pl.pallas_call(
        body, grid=(T // tm, ns),
        in_specs=[tok, tok, vec,
                  pl.BlockSpec((None, None, D, fs), lambda i, j: (ig, j, 0, 0)),
                  pl.BlockSpec((None, None, D, fs), lambda i, j: (iu, j, 0, 0)),
                  pl.BlockSpec((None, None, fs, D), lambda i, j: (idn, j, 0, 0))],
        out_specs=[tok, mid, mid, mid, tok, tok, vec],
        out_shape=[jax.ShapeDtypeStruct((T, D), F32),
                   jax.ShapeDtypeStruct((ns, T, fs), BF16), jax.ShapeDtypeStruct((ns, T, fs), BF16),
                   jax.ShapeDtypeStruct((ns, T, fs), BF16),
                   jax.ShapeDtypeStruct((T, D), BF16), jax.ShapeDtypeStruct((T, D), BF16),
                   jax.ShapeDtypeStruct((1, D), F32)],
        scratch_shapes=[pltpu.VMEM((tm, D), F32)],
        compiler_params=_params("arbitrary", "arbitrary"), name=name)(dy, x, g, wcol, wcol, wrow)


def _ffn_wgrad(hb, dyh, da, db, p, name):
    T, D = hb.shape
    ns, _, fs = da.shape
    tm = _tile(T, TOK_TILE)

    def body(hb_ref, dyh_ref, da_ref, db_ref, p_ref, dwg_ref, dwu_ref, dwd_ref):
        @pl.when(pl.program_id(1) == 0)
        def _():
            dwg_ref[...] = jnp.zeros_like(dwg_ref)
            dwu_ref[...] = jnp.zeros_like(dwu_ref)
            dwd_ref[...] = jnp.zeros_like(dwd_ref)

        hbv = hb_ref[...]
        dwg_ref[...] += _dot_tn(hbv, da_ref[...])
        dwu_ref[...] += _dot_tn(hbv, db_ref[...])
        dwd_ref[...] += _dot_tn(p_ref[...], dyh_ref[...])

    tok = pl.BlockSpec((tm, D), lambda j, i: (i, 0))
    mid = pl.BlockSpec((None, tm, fs), lambda j, i: (j, i, 0))
    return pl.pallas_call(
        body, grid=(ns, T // tm),
        in_specs=[tok, tok, mid, mid, mid],
        out_specs=[pl.BlockSpec((None, D, fs), lambda j, i: (j, 0, 0)),
                   pl.BlockSpec((None, D, fs), lambda j, i: (j, 0, 0)),
                   pl.BlockSpec((None, fs, D), lambda j, i: (j, 0, 0))],
        out_shape=[jax.ShapeDtypeStruct((ns, D, fs), F32), jax.ShapeDtypeStruct((ns, D, fs), F32),
                   jax.ShapeDtypeStruct((ns, fs, D), F32)],
        compiler_params=_params("parallel", "arbitrary"), name=name)(hb, dyh, da, db, p)


def _mix_in_fwd(x, g, win):
    T, D = x.shape
    ns, ws = win.shape[0], win.shape[2]
    tm = _tile(T, TOK_TILE)

    def body(x_ref, g_ref, w_ref, z_ref):
        xh, _ = _rms_stats(x_ref[...])
        hb = (xh * g_ref[...]).astype(BF16)
        for j in range(ns):
            z_ref[:, pl.ds(j * ws, ws)] = _dot(hb, w_ref[j])

    return pl.pallas_call(
        body, grid=(T // tm,),
        in_specs=[pl.BlockSpec((tm, D), lambda i: (i, 0)), pl.BlockSpec((1, D), lambda i: (0, 0)),
                  pl.BlockSpec((ns, D, ws), lambda i: (0, 0, 0), pipeline_mode=pl.Buffered(1))],
        out_specs=pl.BlockSpec((tm, ns * ws), lambda i: (i, 0)),
        out_shape=jax.ShapeDtypeStruct((T, ns * ws), F32),
        compiler_params=_params("parallel"), name="mix_in_fwd")(x, g, win)


def _tap_sum(buf, w_ref, ntaps, first_row, r0, rows, flip):
    acc = None
    for k in range(ntaps):
        off = (ntaps - 1 - k) if flip else k
        t = buf[pl.ds(first_row + r0 + off, rows), :] * w_ref[pl.ds(k, 1), :]
        acc = t if acc is None else acc + t
    return acc


def _conv_fwd(z, w, bias, lng, lnb):
    T = z.shape[0]
    K, C = w.shape
    tm = _tile(T, CONV_TILE, ROW_CHUNK)
    rc = min(ROW_CHUNK, tm)

    def body(cv_ref, cg_ref, w_ref, b_ref, g_ref, bb_ref, u_ref, u1_ref, buf):
        @pl.when(pl.program_id(0) == 0)
        def _():
            buf[pl.ds(0, CONV_HALO), :] = jnp.zeros((CONV_HALO, C), F32)

        buf[pl.ds(CONV_HALO, tm), :] = cv_ref[...] * jax.nn.sigmoid(cg_ref[...])
        for r0 in range(0, tm, rc):
            u1 = _tap_sum(buf, w_ref, K, CONV_HALO - (K - 1), r0, rc, False) + b_ref[...]
            u1_ref[pl.ds(r0, rc), :] = u1
            xc = u1 - jnp.mean(u1, axis=-1, keepdims=True)
            xh = xc * lax.rsqrt(jnp.mean(xc * xc, axis=-1, keepdims=True) + LN_EPS)
            u2 = xh * g_ref[...] + bb_ref[...]
            u_ref[pl.ds(r0, rc), :] = (u2 * jax.nn.sigmoid(u2)).astype(BF16)
        buf[pl.ds(0, CONV_HALO), :] = buf[pl.ds(tm, CONV_HALO), :]

    vec = pl.BlockSpec((1, C), lambda i: (0, 0))
    return pl.pallas_call(
        body, grid=(T // tm,),
        in_specs=[pl.BlockSpec((tm, C), lambda i: (i, 0)), pl.BlockSpec((tm, C), lambda i: (i, 1)),
                  pl.BlockSpec((K, C), lambda i: (0, 0)), vec, vec, vec],
        out_specs=[pl.BlockSpec((tm, C), lambda i: (i, 0)), pl.BlockSpec((tm, C), lambda i: (i, 0))],
        out_shape=[jax.ShapeDtypeStruct((T, C), BF16), jax.ShapeDtypeStruct((T, C), F32)],
        scratch_shapes=[pltpu.VMEM((CONV_HALO + tm, C), F32)],
        compiler_params=_params("arbitrary"), name="conv_fwd")(z, z, w, bias, lng, lnb)


def _conv_bwd(dcat, u1, z, w, lng, lnb):
    T = z.shape[0]
    K, C = w.shape
    tm = _tile(T, CONV_TILE, ROW_CHUNK)
    rc = min(ROW_CHUNK, tm)
    nI = T // tm
    hb = tm // CONV_HALO
    srows = ((K + 4 + SUBLANES - 1) // SUBLANES) * SUBLANES

    def body(du_ref, u1_ref, cv_ref, cg_ref, cvp_ref, cgp_ref, w_ref, g_ref, bb_ref,
             dz_ref, st_ref, u0buf, d1buf):
        i = pl.program_id(0)
        ti = nI - 1 - i

        @pl.when(i == 0)
        def _():
            st_ref[...] = jnp.zeros_like(st_ref)
            d1buf[pl.ds(tm, CONV_HALO), :] = jnp.zeros((CONV_HALO, C), F32)

        prev = cvp_ref[...] * jax.nn.sigmoid(cgp_ref[...])
        u0buf[pl.ds(0, CONV_HALO), :] = jnp.where(ti == 0, 0.0, prev)
        u0buf[pl.ds(CONV_HALO, tm), :] = cv_ref[...] * jax.nn.sigmoid(cg_ref[...])

        gv = g_ref[...]
        dbias = jnp.zeros((1, C), F32)
        dgain = jnp.zeros((1, C), F32)
        dlnb = jnp.zeros((1, C), F32)
        for r0 in range(0, tm, rc):
            u1 = u1_ref[pl.ds(r0, rc), :]
            xc = u1 - jnp.mean(u1, axis=-1, keepdims=True)
            rstd = lax.rsqrt(jnp.mean(xc * xc, axis=-1, keepdims=True) + LN_EPS)
            xh = xc * rstd
            u2 = xh * gv + bb_ref[...]
            s = jax.nn.sigmoid(u2)
            du2 = du_ref[pl.ds(r0, rc), :] * (s * (1.0 + u2 * (1.0 - s)))
            dgain = dgain + _colsum(du2 * xh)
            dlnb = dlnb + _colsum(du2)
            dxh = du2 * gv
            du1 = rstd * (dxh - jnp.mean(dxh, axis=-1, keepdims=True)
                          - xh * jnp.mean(dxh * xh, axis=-1, keepdims=True))
            dbias = dbias + _colsum(du1)
            d1buf[pl.ds(r0, rc), :] = du1
        st_ref[pl.ds(K + 1, 1), :] += dbias
        st_ref[pl.ds(K + 2, 1), :] += dgain
        st_ref[pl.ds(K + 3, 1), :] += dlnb

        for k in range(K):
            acc = jnp.zeros((1, C), F32)
            for r0 in range(0, tm, rc):
                acc = acc + _colsum(d1buf[pl.ds(r0, rc), :]
                                    * u0buf[pl.ds(CONV_HALO - (K - 1) + k + r0, rc), :])
            st_ref[pl.ds(k, 1), :] += acc

        for r0 in range(0, tm, rc):
            du0 = _tap_sum(d1buf, w_ref, K, 0, r0, rc, True)
            cv = cv_ref[pl.ds(r0, rc), :]
            sg = jax.nn.sigmoid(cg_ref[pl.ds(r0, rc), :])
            dz_ref[pl.ds(r0, rc), pl.ds(0, C)] = (du0 * sg).astype(BF16)
            dz_ref[pl.ds(r0, rc), pl.ds(C, C)] = (du0 * cv * sg * (1.0 - sg)).astype(BF16)
        d1buf[pl.ds(tm, CONV_HALO), :] = d1buf[pl.ds(0, CONV_HALO), :]

    def rev(col):
        return lambda i: (nI - 1 - i, col)

    def rev_prev(col):
        return lambda i: (jnp.maximum((nI - 1 - i) * hb - 1, 0), col)

    vec = pl.BlockSpec((1, C), lambda i: (0, 0))
    return pl.pallas_call(
        body, grid=(nI,),
        in_specs=[pl.BlockSpec((tm, C), rev(0)), pl.BlockSpec((tm, C), rev(0)),
                  pl.BlockSpec((tm, C), rev(0)), pl.BlockSpec((tm, C), rev(1)),
                  pl.BlockSpec((CONV_HALO, C), rev_prev(0)), pl.BlockSpec((CONV_HALO, C), rev_prev(1)),
                  pl.BlockSpec((K, C), lambda i: (0, 0)), vec, vec],
        out_specs=[pl.BlockSpec((tm, 2 * C), rev(0)), pl.BlockSpec((srows, C), lambda i: (0, 0))],
        out_shape=[jax.ShapeDtypeStruct((T, 2 * C), BF16), jax.ShapeDtypeStruct((srows, C), F32)],
        scratch_shapes=[pltpu.VMEM((CONV_HALO + tm, C), F32), pltpu.VMEM((tm + CONV_HALO, C), F32)],
        compiler_params=_params("arbitrary"), name="conv_bwd")(dcat, u1, z, z, z, z, w, lng, lnb)


def _softplus(v):
    return jnp.maximum(v, 0.0) + jnp.log(1.0 + jnp.exp(-jnp.abs(v)))


def _gelu(v):
    c = math.sqrt(2.0 / math.pi)
    t = jnp.tanh(c * (v + 0.044715 * v * v * v))
    gl = 0.5 * v * (1.0 + t)
    dgl = 0.5 * (1.0 + t) + 0.5 * v * (1.0 - t * t) * c * (1.0 + 3.0 * 0.044715 * v * v)
    return gl, dgl


def _lru_gates(xr, wa, ba, wx, bx, lam):
    xb = xr.astype(BF16)
    r = jax.nn.sigmoid(_dot(xb, wa) + ba)
    ig = jax.nn.sigmoid(_dot(xb, wx) + bx)
    sp = _softplus(-lam)
    log_a = -LRU_C * r * sp
    a = jnp.exp(log_a)
    y = 2.0 * log_a
    series = -(y * (1.0 + y * (0.5 + y * (1.0 / 6.0 + y * (1.0 / 24.0)))))
    mult = jnp.sqrt(jnp.where(y > -0.02, series, 1.0 - jnp.exp(y)))
    return a, mult, r, ig, sp


def _scan_tile(a_s, b_s, carry, seg, reverse):
    def step(n, hp):
        hl, pr = hp
        k = (seg - 1 - n) if reverse else n
        rows = pl.ds(k, SUBLANES, stride=seg)
        av = a_s[rows, :]
        hl = av * hl + b_s[rows, :]
        pr = av * pr
        b_s[rows, :] = hl
        a_s[rows, :] = pr
        return hl, pr

    hl, pr = lax.fori_loop(0, seg, step, (jnp.zeros((SUBLANES, LANES), F32), jnp.ones((SUBLANES, LANES), F32)),
                           unroll=min(8, seg))
    cs = [None] * SUBLANES
    c = carry
    for s in (range(SUBLANES - 1, -1, -1) if reverse else range(SUBLANES)):
        cs[s] = c
        c = hl[s:s + 1, :] + pr[s:s + 1, :] * c
    return cs, c


def _lru_fwd(z, col0, w4, b4, wa, ba, wx, bx, lam):
    T = z.shape[0]
    K4, W = w4.shape
    nC = W // LANES
    tm = _tile(T, LRU_TILE, SUBLANES * SUBLANES)
    seg = tm // SUBLANES
    cx, cg = col0 // LANES, (col0 + W) // LANES

    def body(rx_ref, rg_ref, w4_ref, b4_ref, wa_ref, ba_ref, wx_ref, bx_ref, lam_ref,
             yr_ref, hs_ref, xbuf, a_s, b_s, hc):
        @pl.when(pl.program_id(1) == 0)
        def _():
            xbuf[pl.ds(0, LRU_HALO), :] = jnp.zeros((LRU_HALO, LANES), F32)
            hc[...] = jnp.zeros_like(hc)

        xbuf[pl.ds(LRU_HALO, tm), :] = rx_ref[...]
        xr = _tap_sum(xbuf, w4_ref, K4, LRU_HALO - (K4 - 1), 0, tm, False) + b4_ref[...]
        a, mult, _, ig, _ = _lru_gates(xr, wa_ref[...], ba_ref[...], wx_ref[...], bx_ref[...], lam_ref[...])
        a_s[...] = a
        b_s[...] = mult * ig * xr
        cs, cout = _scan_tile(a_s, b_s, hc[pl.ds(0, 1), :], seg, False)
        hc[pl.ds(0, 1), :] = cout
        for s in range(SUBLANES):
            rows = pl.ds(s * seg, seg)
            h = b_s[rows, :] + a_s[rows, :] * cs[s]
            hs_ref[rows, :] = h
            gl, _ = _gelu(rg_ref[rows, :])
            yr_ref[rows, :] = (h * gl).astype(BF16)
        xbuf[pl.ds(0, LRU_HALO), :] = xbuf[pl.ds(tm, LRU_HALO), :]

    vec = pl.BlockSpec((1, LANES), lambda c, i: (0, c))
    mat = pl.BlockSpec((None, LANES, LANES), lambda c, i: (c, 0, 0))
    return pl.pallas_call(
        body, grid=(nC, T // tm),
        in_specs=[pl.BlockSpec((tm, LANES), lambda c, i: (i, cx + c)),
                  pl.BlockSpec((tm, LANES), lambda c, i: (i, cg + c)),
                  pl.BlockSpec((K4, LANES), lambda c, i: (0, c)), vec, mat, vec, mat, vec, vec],
        out_specs=[pl.BlockSpec((tm, LANES), lambda c, i: (i, c)), pl.BlockSpec((tm, LANES), lambda c, i: (i, c))],
        out_shape=[jax.ShapeDtypeStruct((T, W), BF16), jax.ShapeDtypeStruct((T, W), F32)],
        scratch_shapes=[pltpu.VMEM((LRU_HALO + tm, LANES), F32), pltpu.VMEM((tm, LANES), F32),
                        pltpu.VMEM((tm, LANES), F32), pltpu.VMEM((SUBLANES, LANES), F32)],
        compiler_params=_params("parallel", "arbitrary"), name="lru_fwd")(z, z, w4, b4, wa, ba, wx, bx, lam)


def _lru_bwd(dcat, dcol0, hs, z, col0, w4, b4, wa, ba, wx, bx, lam):
    T = z.shape[0]
    K4, W = w4.shape
    assert K4 + 4 == SUBLANES
    nC = W // LANES
    tm = _tile(T, LRU_TILE, SUBLANES * SUBLANES)
    seg = tm // SUBLANES
    nI = T // tm
    hb = tm // LRU_HALO
    cx, cg, cd = col0 // LANES, (col0 + W) // LANES, dcol0 // LANES

    def body(dyr_ref, hs_ref, hsp_ref, rx_ref, rxp_ref, rg_ref, w4_ref, b4_ref, wa_ref, ba_ref, wx_ref, bx_ref,
             lam_ref, dzx_ref, dzg_ref, st_ref, dwa_ref, dwx_ref, xbuf, hbuf, abuf, a_s, b_s, dbuf, gc, anc):
        i = pl.program_id(1)
        ti = nI - 1 - i

        @pl.when(i == 0)
        def _():
            st_ref[...] = jnp.zeros_like(st_ref)
            dwa_ref[...] = jnp.zeros_like(dwa_ref)
            dwx_ref[...] = jnp.zeros_like(dwx_ref)
            gc[...] = jnp.zeros_like(gc)
            anc[...] = jnp.zeros_like(anc)
            dbuf[pl.ds(tm, LRU_HALO), :] = jnp.zeros((LRU_HALO, LANES), F32)

        xbuf[pl.ds(0, LRU_HALO), :] = jnp.where(ti == 0, 0.0, rxp_ref[...])
        xbuf[pl.ds(LRU_HALO, tm), :] = rx_ref[...]
        hbuf[pl.ds(0, LRU_HALO), :] = jnp.where(ti == 0, 0.0, hsp_ref[...])
        hbuf[pl.ds(LRU_HALO, tm), :] = hs_ref[...]

        wa, wx = wa_ref[...], wx_ref[...]
        lam_v = lam_ref[...]
        xr = _tap_sum(xbuf, w4_ref, K4, LRU_HALO - (K4 - 1), 0, tm, False) + b4_ref[...]
        a, mult, r, ig, sp = _lru_gates(xr, wa, ba_ref[...], wx, bx_ref[...], lam_v)

        dyr = dyr_ref[...]
        gl, dgl = _gelu(rg_ref[...])
        dzg_ref[...] = (dyr * hs_ref[...] * dgl).astype(BF16)

        abuf[pl.ds(0, tm), :] = a
        abuf[pl.ds(tm, LRU_HALO), :] = anc[...]
        a_s[...] = abuf[pl.ds(1, tm), :]
        b_s[...] = dyr * gl
        cs, cout = _scan_tile(a_s, b_s, gc[pl.ds(0, 1), :], seg, True)
        gc[pl.ds(0, 1), :] = cout
        anc[pl.ds(0, 1), :] = a[0:1, :]
        for s in range(SUBLANES):
            rows = pl.ds(s * seg, seg)
            b_s[rows, :] = b_s[rows, :] + a_s[rows, :] * cs[s]
        g = b_s[...]

        d_a = g * hbuf[pl.ds(LRU_HALO - 1, tm), :]
        gx_ = g * xr
        d_log_a = d_a * a - (gx_ * ig) * (a * a / mult)
        dga = (d_log_a * (-LRU_C * sp)) * r * (1.0 - r)
        dgx = (gx_ * mult) * ig * (1.0 - ig)
        dga_b, dgx_b = dga.astype(BF16), dgx.astype(BF16)
        dxr = g * mult * ig + _dot_nt(dga_b, wa) + _dot_nt(dgx_b, wx)
        xb = xr.astype(BF16)
        dwa_ref[...] += _dot_tn(xb, dga_b)
        dwx_ref[...] += _dot_tn(xb, dgx_b)
        st_ref[pl.ds(K4, 1), :] += _colsum(dxr)
        st_ref[pl.ds(K4 + 1, 1), :] += _colsum(dga)
        st_ref[pl.ds(K4 + 2, 1), :] += _colsum(dgx)
        st_ref[pl.ds(K4 + 3, 1), :] += _colsum(d_log_a * (-LRU_C * r)) * (-jax.nn.sigmoid(-lam_v))

        dbuf[pl.ds(0, tm), :] = dxr
        for k in range(K4):
            st_ref[pl.ds(k, 1), :] += _colsum(dxr * xbuf[pl.ds(LRU_HALO - (K4 - 1) + k, tm), :])
        dzx_ref[...] = _tap_sum(dbuf, w4_ref, K4, 0, 0, tm, True).astype(BF16)
        dbuf[pl.ds(tm, LRU_HALO), :] = dbuf[pl.ds(0, LRU_HALO), :]

    def rev(col):
        return lambda c, i: (nI - 1 - i, col + c)

    def rev_prev(col):
        return lambda c, i: (jnp.maximum((nI - 1 - i) * hb - 1, 0), col + c)

    vec = pl.BlockSpec((1, LANES), lambda c, i: (0, c))
    mat = pl.BlockSpec((None, LANES, LANES), lambda c, i: (c, 0, 0))
    big = pltpu.VMEM((tm, LANES), F32)
    halo = pltpu.VMEM((tm + LRU_HALO, LANES), F32)
    return pl.pallas_call(
        body, grid=(nC, nI),
        in_specs=[pl.BlockSpec((tm, LANES), rev(cd)),
                  pl.BlockSpec((tm, LANES), rev(0)), pl.BlockSpec((LRU_HALO, LANES), rev_prev(0)),
                  pl.BlockSpec((tm, LANES), rev(cx)), pl.BlockSpec((LRU_HALO, LANES), rev_prev(cx)),
                  pl.BlockSpec((tm, LANES), rev(cg)),
                  pl.BlockSpec((K4, LANES), lambda c, i: (0, c)), vec, mat, vec, mat, vec, vec],
        out_specs=[pl.BlockSpec((tm, LANES), rev(0)), pl.BlockSpec((tm, LANES), rev(0)),
                   pl.BlockSpec((SUBLANES, LANES), lambda c, i: (0, c)), mat, mat],
        out_shape=[jax.ShapeDtypeStruct((T, W), BF16), jax.ShapeDtypeStruct((T, W), BF16),
                   jax.ShapeDtypeStruct((SUBLANES, W), F32),
                   jax.ShapeDtypeStruct((nC, LANES, LANES), F32), jax.ShapeDtypeStruct((nC, LANES, LANES), F32)],
        scratch_shapes=[halo, halo, halo, big, big, halo,
                        pltpu.VMEM((SUBLANES, LANES), F32), pltpu.VMEM((SUBLANES, LANES), F32)],
        compiler_params=_params("parallel", "arbitrary"), name="lru_bwd")(
            dcat, hs, hs, z, z, z, w4, b4, wa, ba, wx, bx, lam)


def _mix_out_fwd(x, u, yr, wout):
    T, D = x.shape
    C, W = u.shape[1], yr.shape[1]
    tm = _tile(T, TOK_TILE)

    def body(x_ref, u_ref, yr_ref, w_ref, y_ref):
        y_ref[...] = (x_ref[...] + _dot(u_ref[...], w_ref[pl.ds(0, C), :])
                      + _dot(yr_ref[...], w_ref[pl.ds(C, W), :]))

    return pl.pallas_call(
        body, grid=(T // tm,),
        in_specs=[pl.BlockSpec((tm, D), lambda i: (i, 0)), pl.BlockSpec((tm, C), lambda i: (i, 0)),
                  pl.BlockSpec((tm, W), lambda i: (i, 0)),
                  pl.BlockSpec((C + W, D), lambda i: (0, 0), pipeline_mode=pl.Buffered(1))],
        out_specs=pl.BlockSpec((tm, D), lambda i: (i, 0)),
        out_shape=jax.ShapeDtypeStruct((T, D), F32),
        compiler_params=_params("parallel"), name="mix_out_fwd")(x, u, yr, wout)


def _mix_out_bwd(dy, u, yr, wout):
    T, D = dy.shape
    C, W = u.shape[1], yr.shape[1]
    tm = _tile(T, BWD_TILE)

    def body(dy_ref, u_ref, yr_ref, w_ref, dcat_ref, dw_ref):
        @pl.when(pl.program_id(0) == 0)
        def _():
            dw_ref[...] = jnp.zeros_like(dw_ref)

        dyb = dy_ref[...].astype(BF16)
        dcat_ref[...] = _dot_nt(dyb, w_ref[...])
        dw_ref[pl.ds(0, C), :] += _dot_tn(u_ref[...], dyb)
        dw_ref[pl.ds(C, W), :] += _dot_tn(yr_ref[...], dyb)

    return pl.pallas_call(
        body, grid=(T // tm,),
        in_specs=[pl.BlockSpec((tm, D), lambda i: (i, 0)), pl.BlockSpec((tm, C), lambda i: (i, 0)),
                  pl.BlockSpec((tm, W), lambda i: (i, 0)),
                  pl.BlockSpec((C + W, D), lambda i: (0, 0), pipeline_mode=pl.Buffered(1))],
        out_specs=[pl.BlockSpec((tm, C + W), lambda i: (i, 0)), pl.BlockSpec((C + W, D), lambda i: (0, 0))],
        out_shape=[jax.ShapeDtypeStruct((T, C + W), F32), jax.ShapeDtypeStruct((C + W, D), F32)],
        compiler_params=_params("arbitrary"), name="mix_out_bwd")(dy, u, yr, wout)


def _mix_in_bwd(dzc, dzx, dzg, x, dy, g, win):
    T, D = x.shape
    ns, ws = win.shape[0], win.shape[2]
    tm = _tile(T, BWD_TILE)
    parts = []
    for j in range(ns):
        lo = j * ws
        if lo < dzc.shape[1]:
            parts.append((0, lo))
        elif lo < dzc.shape[1] + dzx.shape[1]:
            parts.append((1, lo - dzc.shape[1]))
        else:
            parts.append((2, lo - dzc.shape[1] - dzx.shape[1]))

    def body(dzc_ref, dzx_ref, dzg_ref, x_ref, dy_ref, g_ref, w_ref, dx_ref, dw_ref, dg_ref):
        @pl.when(pl.program_id(0) == 0)
        def _():
            dw_ref[...] = jnp.zeros_like(dw_ref)
            dg_ref[...] = jnp.zeros_like(dg_ref)

        xh, r = _rms_stats(x_ref[...])
        gv = g_ref[...]
        hb = (xh * gv).astype(BF16)
        srcs = (dzc_ref, dzx_ref, dzg_ref)
        dh = jnp.zeros((tm, D), F32)
        for j, (si, off) in enumerate(parts):
            dzj = srcs[si][:, pl.ds(off, ws)]
            dh = dh + _dot_nt(dzj, w_ref[j])
            dw_ref[j] += _dot_tn(hb, dzj)
        dx_ref[...] = dy_ref[...] + _rms_bwd(dh, xh, r, gv)
        dg_ref[...] += _colsum(dh * xh)

    def tok(n):
        return pl.BlockSpec((tm, n), lambda i: (i, 0))

    vec = pl.BlockSpec((1, D), lambda i: (0, 0))
    return pl.pallas_call(
        body, grid=(T // tm,),
        in_specs=[tok(dzc.shape[1]), tok(dzx.shape[1]), tok(dzg.shape[1]), tok(D), tok(D), vec,
                  pl.BlockSpec((ns, D, ws), lambda i: (0, 0, 0), pipeline_mode=pl.Buffered(1))],
        out_specs=[tok(D), pl.BlockSpec((ns, D, ws), lambda i: (0, 0, 0)), vec],
        out_shape=[jax.ShapeDtypeStruct((T, D), F32), jax.ShapeDtypeStruct((ns, D, ws), F32),
                   jax.ShapeDtypeStruct((1, D), F32)],
        compiler_params=_params("arbitrary"), name="mix_in_bwd")(dzc, dzx, dzg, x, dy, g, win)


def _final_loss(x, g, tgt):
    T, D = x.shape
    tm = _tile(T, TOK_TILE)

    def body(x_ref, g_ref, t_ref, dx_ref, loss_ref, dg_ref):
        @pl.when(pl.program_id(0) == 0)
        def _():
            loss_ref[...] = jnp.zeros_like(loss_ref)
            dg_ref[...] = jnp.zeros_like(dg_ref)

        xh, r = _rms_stats(x_ref[...])
        gv = g_ref[...]
        e = xh * gv - t_ref[...]
        loss_ref[...] += 0.5 * jnp.sum(jnp.mean(e * e, axis=-1, keepdims=True))
        dy = e * (1.0 / D)
        dg_ref[...] += _colsum(dy * xh)
        dx_ref[...] = _rms_bwd(dy, xh, r, gv)

    tok = pl.BlockSpec((tm, D), lambda i: (i, 0))
    vec = pl.BlockSpec((1, D), lambda i: (0, 0))
    return pl.pallas_call(
        body, grid=(T // tm,),
        in_specs=[tok, vec, tok],
        out_specs=[tok, pl.BlockSpec((SUBLANES, LANES), lambda i: (0, 0)), vec],
        out_shape=[jax.ShapeDtypeStruct((T, D), F32), jax.ShapeDtypeStruct((SUBLANES, LANES), F32),
                   jax.ShapeDtypeStruct((1, D), F32)],
        compiler_params=_params("arbitrary"), name="final_loss")(x, g, tgt)


def _adamw(w, g, m, v, name):
    R, Cc = w.shape
    tr = _tile(R, max(SUBLANES, (1 << 19) // Cc))
    c1 = 1.0 - ADAM_B1 ** ADAM_STEP
    c2 = 1.0 - ADAM_B2 ** ADAM_STEP

    def body(w_ref, g_ref, m_ref, v_ref, d_ref, nm_ref, nv_ref):
        gv = g_ref[...]
        nm = ADAM_B1 * m_ref[...] + (1.0 - ADAM_B1) * gv
        nv = ADAM_B2 * v_ref[...] + (1.0 - ADAM_B2) * (gv * gv)
        nm_ref[...] = nm
        nv_ref[...] = nv
        d_ref[...] = -ADAM_LR * ((nm / c1) / (jnp.sqrt(nv / c2) + ADAM_EPS) + ADAM_WD * w_ref[...])

    blk = pl.BlockSpec((tr, Cc), lambda i: (i, 0))
    sds = jax.ShapeDtypeStruct((R, Cc), F32)
    return pl.pallas_call(
        body, grid=(R // tr,), in_specs=[blk] * 4, out_specs=[blk] * 3, out_shape=[sds] * 3,
        compiler_params=_params("parallel"), name=name)(w, g, m, v)


def _here():
    return lax.axis_index("x"), lax.axis_index("y"), lax.axis_index("c")


def _chip_at(x, y, m):
    return x ^ (m >> 1), y ^ (m & 1)


ANY = pl.BlockSpec(memory_space=pl.ANY)


def _gather_weights(shards):
    n = len(shards)
    out_shapes = [jax.ShapeDtypeStruct(s.shape[:-2] + (N_CHIPS,) + s.shape[-2:], s.dtype) for s in shards]

    def body(*refs):
        ins, outs = refs[:n], refs[n:2 * n]
        send1, recv1, send2, recv2, lsem = refs[2 * n:]
        x, y, c = _here()
        own = 2 * x + y

        def half(ref, chip, cc):
            rh = ref.shape[-2] // 2
            lead = (slice(None),) * (len(ref.shape) - 3)
            return ref.at[lead + (chip, pl.ds(cc * rh, rh), slice(None))]

        def src_half(ref, cc):
            rh = ref.shape[-2] // 2
            lead = (slice(None),) * (len(ref.shape) - 2)
            return ref.at[lead + (pl.ds(cc * rh, rh), slice(None))]

        local = []
        for k in range(n):
            lead = (slice(None),) * (len(outs[k].shape) - 3)
            cp = pltpu.make_async_copy(ins[k], outs[k].at[lead + (own,)], lsem.at[k])
            cp.start()
            local.append(cp)

        first = []
        for k in range(n):
            for m in (1, 2, 3):
                px, py = _chip_at(x, y, m)
                cp = pltpu.make_async_remote_copy(
                    src_ref=src_half(ins[k], c), dst_ref=half(outs[k], own, c),
                    send_sem=send1.at[k, m - 1], recv_sem=recv1.at[k, m - 1],
                    device_id=(px, py, c), device_id_type=MESH)
                cp.start()
                first.append(cp)

        passed = []
        for k in range(n):
            for m in (1, 2, 3):
                px, py = _chip_at(x, y, m)
                peer = 2 * px + py
                got = half(outs[k], peer, c)
                pltpu.make_async_remote_copy(
                    src_ref=got, dst_ref=got, send_sem=send1.at[k, m - 1], recv_sem=recv1.at[k, m - 1],
                    device_id=(px, py, c), device_id_type=MESH).wait_recv()
                cp = pltpu.make_async_remote_copy(
                    src_ref=got, dst_ref=got, send_sem=send2.at[k, m - 1], recv_sem=recv2.at[k, m - 1],
                    device_id=(x, y, 1 - c), device_id_type=MESH)
                cp.start()
                passed.append(cp)

        for k in range(n):
            for m in (1, 2, 3):
                px, py = _chip_at(x, y, m)
                other = half(outs[k], 2 * px + py, 1 - c)
                pltpu.make_async_remote_copy(
                    src_ref=other, dst_ref=other, send_sem=send2.at[k, m - 1], recv_sem=recv2.at[k, m - 1],
                    device_id=(x, y, 1 - c), device_id_type=MESH).wait_recv()
        for cp in first + passed:
            cp.wait_send()
        for cp in local:
            cp.wait()

    return pl.pallas_call(
        body, in_specs=[ANY] * n, out_specs=[ANY] * n, out_shape=out_shapes,
        scratch_shapes=[pltpu.SemaphoreType.DMA((n, 3)), pltpu.SemaphoreType.DMA((n, 3)),
                        pltpu.SemaphoreType.DMA((n, 3)), pltpu.SemaphoreType.DMA((n, 3)),
                        pltpu.SemaphoreType.DMA((n,))],
        name="gather_weights")(*shards)


def _swap_halves_out(grads):
    n = len(grads)
    out_shapes = [jax.ShapeDtypeStruct((g.shape[0], g.shape[1] // 2, g.shape[2]), g.dtype) for g in grads]

    def body(*refs):
        ins, outs = refs[:n], refs[n:2 * n]
        send, recv = refs[2 * n:]
        x, y, c = _here()
        cps = []
        for k in range(n):
            rh = ins[k].shape[1] // 2
            cp = pltpu.make_async_remote_copy(
                src_ref=ins[k].at[:, pl.ds((1 - c) * rh, rh), :], dst_ref=outs[k],
                send_sem=send.at[k], recv_sem=recv.at[k], device_id=(x, y, 1 - c), device_id_type=MESH)
            cp.start()
            cps.append(cp)
        for cp in cps:
            cp.wait()

    return pl.pallas_call(
        body, in_specs=[ANY] * n, out_specs=[ANY] * n, out_shape=out_shapes,
        scratch_shapes=[pltpu.SemaphoreType.DMA((n,)), pltpu.SemaphoreType.DMA((n,))],
        name="swap_halves")(*grads)


def _add_cast(g, other, cidx, name):
    ns, R, Cc = g.shape
    rh = R // 2
    tr = _tile(rh, max(16, (1 << 18) // Cc), 16)
    nb = rh // tr

    def body(c_ref, g_ref, o_ref, s_ref):
        s_ref[...] = (g_ref[...] + o_ref[...]).astype(BF16)

    return pl.pallas_call(
        body,
        grid_spec=pltpu.PrefetchScalarGridSpec(
            num_scalar_prefetch=1, grid=(ns, nb),
            in_specs=[pl.BlockSpec((None, tr, Cc), lambda k, i, c: (k, c[0] * nb + i, 0)),
                      pl.BlockSpec((None, tr, Cc), lambda k, i, c: (k, i, 0))],
            out_specs=pl.BlockSpec((None, tr, Cc), lambda k, i, c: (k, i, 0))),
        out_shape=jax.ShapeDtypeStruct((ns, rh, Cc), BF16),
        compiler_params=_params("parallel", "parallel"), name=name)(cidx, g, other)


def _exchange_chips(parts):
    n = len(parts)
    out_shapes = [jax.ShapeDtypeStruct(p.shape, p.dtype) for p in parts]

    def body(*refs):
        ins, outs = refs[:n], refs[n:2 * n]
        send, recv, lsem = refs[2 * n:]
        x, y, c = _here()
        own = 2 * x + y
        cps, local = [], []
        for k in range(n):
            cp = pltpu.make_async_copy(ins[k].at[own], outs[k].at[0], lsem.at[k])
            cp.start()
            local.append(cp)
            for m in (1, 2, 3):
                px, py = _chip_at(x, y, m)
                cp = pltpu.make_async_remote_copy(
                    src_ref=ins[k].at[2 * px + py], dst_ref=outs[k].at[m],
                    send_sem=send.at[k, m - 1], recv_sem=recv.at[k, m - 1],
                    device_id=(px, py, c), device_id_type=MESH)
                cp.start()
                cps.append(cp)
        for cp in cps:
            cp.wait()
        for cp in local:
            cp.wait()

    return pl.pallas_call(
        body, in_specs=[ANY] * n, out_specs=[ANY] * n, out_shape=out_shapes,
        scratch_shapes=[pltpu.SemaphoreType.DMA((n, 3)), pltpu.SemaphoreType.DMA((n, 3)),
                        pltpu.SemaphoreType.DMA((n,))],
        name="exchange_chips")(*parts)


def _sum_slots(b, cidx, name):
    ns, rh, Cc = b.shape
    tr = _tile(rh, max(16, (1 << 17) // Cc), 16)
    nb = rh // tr

    def body(c_ref, b_ref, o_ref):
        acc = b_ref[0].astype(F32)
        for m in range(1, ns):
            acc = acc + b_ref[m].astype(F32)
        o_ref[...] = acc

    return pl.pallas_call(
        body,
        grid_spec=pltpu.PrefetchScalarGridSpec(
            num_scalar_prefetch=1, grid=(nb,),
            in_specs=[pl.BlockSpec((ns, tr, Cc), lambda i, c: (0, i, 0))],
            out_specs=pl.BlockSpec((tr, Cc), lambda i, c: (c[0] * nb + i, 0))),
        out_shape=jax.ShapeDtypeStruct((2 * rh, Cc), F32),
        compiler_params=_params("parallel"), name=name)(cidx, b)


def _share_halves(blocks):
    n = len(blocks)

    def body(*refs):
        ins, outs = refs[:n], refs[n:2 * n]
        send, recv = refs[2 * n:]
        x, y, c = _here()
        cps = []
        for k in range(n):
            rh = outs[k].shape[0] // 2
            mine = outs[k].at[pl.ds(c * rh, rh), :]
            cp = pltpu.make_async_remote_copy(
                src_ref=mine, dst_ref=mine, send_sem=send.at[k], recv_sem=recv.at[k],
                device_id=(x, y, 1 - c), device_id_type=MESH)
            cp.start()
            cps.append(cp)
        for cp in cps:
            cp.wait()

    return pl.pallas_call(
        body, in_specs=[ANY] * n, out_specs=[ANY] * n,
        out_shape=[jax.ShapeDtypeStruct(b.shape, b.dtype) for b in blocks],
        input_output_aliases={k: k for k in range(n)},
        scratch_shapes=[pltpu.SemaphoreType.DMA((n,)), pltpu.SemaphoreType.DMA((n,))],
        name="share_halves")(*blocks)


def _gather_small(packed):
    R = packed.shape[0]

    def body(p_ref, o_ref, send, recv, lsem):
        x, y, c = _here()
        me = 4 * x + 2 * y + c
        mine = o_ref.at[me]
        lc = pltpu.make_async_copy(p_ref, mine, lsem)
        lc.start()
        cps = []
        for m in range(1, N_DEV):
            peer = (x ^ (m >> 2), y ^ ((m >> 1) & 1), c ^ (m & 1))
            cp = pltpu.make_async_remote_copy(
                src_ref=p_ref, dst_ref=mine, send_sem=send.at[m - 1], recv_sem=recv.at[m - 1],
                device_id=peer, device_id_type=MESH)
            cp.start()
            cps.append(cp)
        for m in range(1, N_DEV):
            theirs = o_ref.at[me ^ m]
            pltpu.make_async_remote_copy(
                src_ref=p_ref, dst_ref=theirs, send_sem=send.at[m - 1], recv_sem=recv.at[m - 1],
                device_id=(x, y, c), device_id_type=MESH).wait_recv()
        for cp in cps:
            cp.wait_send()
        lc.wait()

    return pl.pallas_call(
        body, in_specs=[ANY], out_specs=ANY,
        out_shape=jax.ShapeDtypeStruct((N_DEV, R, LANES), F32),
        scratch_shapes=[pltpu.SemaphoreType.DMA((N_DEV - 1,)), pltpu.SemaphoreType.DMA((N_DEV - 1,)),
                        pltpu.SemaphoreType.DMA],
        name="gather_small")(packed)


def _sum_devices(slots):
    n, R, _ = slots.shape
    tr = _tile(R, 256)

    def body(s_ref, o_ref):
        acc = s_ref[0]
        for d in range(1, n):
            acc = acc + s_ref[d]
        o_ref[...] = acc

    return pl.pallas_call(
        body, grid=(R // tr,),
        in_specs=[pl.BlockSpec((n, tr, LANES), lambda i: (0, i, 0))],
        out_specs=pl.BlockSpec((tr, LANES), lambda i: (i, 0)),
        out_shape=jax.ShapeDtypeStruct((R, LANES), F32),
        compiler_params=_params("parallel"), name="sum_devices")(slots)


def _pack(arrs):
    rows, parts = [], []
    for a in arrs:
        flat = a.reshape(-1)
        r = -(-flat.shape[0] // (SUBLANES * LANES)) * SUBLANES
        parts.append(jnp.pad(flat, (0, r * LANES - flat.shape[0])).reshape(r, LANES))
        rows.append(r)
    return jnp.concatenate(parts, axis=0), rows


def _unpack(packed, rows, shapes):
    out, r0 = [], 0
    for r, shp in zip(rows, shapes):
        size = math.prod(shp)
        out.append(packed[r0:r0 + r].reshape(-1)[:size].reshape(shp))
        r0 += r
    return out


def _block_diag(w, per):
    H, dh, _ = w.shape
    w4 = w.reshape(H // per, per, dh, dh)
    eye = jnp.eye(per, dtype=w.dtype)
    return (w4[:, :, :, None, :] * eye[None, :, None, :, None]).reshape(H // per, per * dh, per * dh)


def _block_diag_take(d, per):
    n, s, _ = d.shape
    dh = s // per
    d5 = d.reshape(n, per, dh, per, dh)
    return jnp.stack([d5[:, h, :, h, :] for h in range(per)], axis=1).reshape(n * per, dh, dh)


def kernel(x, ffn1_norm, ffn1_w_gate, ffn1_w_up, ffn1_w_down, mix_norm, w_in, conv_dw, conv_dw_bias, conv_ln_g, conv_ln_b, lru_conv_w, lru_conv_b, lru_w_a, lru_b_a, lru_w_x, lru_b_x, lru_lambda, w_out, ffn2_norm, ffn2_w_gate, ffn2_w_up, ffn2_w_down, final_norm, loss_target, m_ffn1_norm, m_ffn1_w_gate, m_ffn1_w_up, m_ffn1_w_down, m_mix_norm, m_w_in, m_conv_dw, m_conv_dw_bias, m_conv_ln_g, m_conv_ln_b, m_lru_conv_w, m_lru_conv_b, m_lru_w_a, m_lru_b_a, m_lru_w_x, m_lru_b_x, m_lru_lambda, m_w_out, m_ffn2_norm, m_ffn2_w_gate, m_ffn2_w_up, m_ffn2_w_down, m_final_norm, v_ffn1_norm, v_ffn1_w_gate, v_ffn1_w_up, v_ffn1_w_down, v_mix_norm, v_w_in, v_conv_dw, v_conv_dw_bias, v_conv_ln_g, v_conv_ln_b, v_lru_conv_w, v_lru_conv_b, v_lru_w_a, v_lru_b_a, v_lru_w_x, v_lru_b_x, v_lru_lambda, v_w_out, v_ffn2_norm, v_ffn2_w_gate, v_ffn2_w_up, v_ffn2_w_down, v_final_norm):
    names = ['ffn1_norm', 'ffn1_w_gate', 'ffn1_w_up', 'ffn1_w_down', 'mix_norm', 'w_in', 'conv_dw', 'conv_dw_bias',
             'conv_ln_g', 'conv_ln_b', 'lru_conv_w', 'lru_conv_b', 'lru_w_a', 'lru_b_a', 'lru_w_x', 'lru_b_x',
             'lru_lambda', 'w_out', 'ffn2_norm', 'ffn2_w_gate', 'ffn2_w_up', 'ffn2_w_down', 'final_norm']
    env = dict(locals())
    W = {n: env[n] for n in names}
    M = {n: env['m_' + n] for n in names}
    V = {n: env['v_' + n] for n in names}

    xi, yi, ci = _here()
    chip = 2 * xi + yi
    cidx = ci.astype(jnp.int32).reshape(1)
    T, D = x.shape[-2], x.shape[-1]
    xs = x.reshape(T, D)
    tgt = loss_target.reshape(T, D)
    K, Cs = conv_dw.shape
    C = conv_dw_bias.shape[0]
    Wl = lru_conv_b.shape[0]
    K4 = lru_conv_w.shape[0]
    heads, dh, _ = lru_w_a.shape
    per = LANES // dh

    def row(v):
        return v.reshape(1, -1)

    colw = jnp.stack([ffn1_w_gate, ffn1_w_up, ffn2_w_gate, ffn2_w_up]).astype(BF16)
    roww = jnp.stack([ffn1_w_down, ffn2_w_down]).astype(BF16)
    kp = -(-K // SUBLANES) * SUBLANES
    taps = jnp.concatenate([conv_dw, jnp.zeros((kp - K, Cs), F32), lru_conv_w,
                            jnp.zeros((2 * SUBLANES - K4, Cs), F32)], axis=0)
    wcol, wrow, win, wout, taps = _gather_weights([colw, roww, w_in.astype(BF16), w_out.astype(BF16), taps])
    wout = wout.reshape(-1, D)
    conv_w_full = taps[:, :K].transpose(1, 0, 2).reshape(K, N_CHIPS * Cs)
    lru_w4_full = taps[:, kp:kp + K4].transpose(1, 0, 2).reshape(K4, N_CHIPS * Cs)

    wa_bd = _block_diag(lru_w_a, per).astype(BF16)
    wx_bd = _block_diag(lru_w_x, per).astype(BF16)

    x1 = _ffn_fwd(xs, row(ffn1_norm), wcol, wrow, 0, 1, 0, "ffn1_fwd")
    z = _mix_in_fwd(x1, row(mix_norm), win)
    u, u1 = _conv_fwd(z, conv_w_full, row(conv_dw_bias), row(conv_ln_g), row(conv_ln_b))
    yr, hs = _lru_fwd(z, 2 * C, lru_w4_full, row(lru_conv_b), wa_bd, row(lru_b_a), wx_bd, row(lru_b_x),
                      row(lru_lambda))
    x2 = _mix_out_fwd(x1, u, yr, wout)
    x3 = _ffn_fwd(x2, row(ffn2_norm), wcol, wrow, 2, 3, 1, "ffn2_fwd")
    dx3, loss_blk, d_final = _final_loss(x3, row(final_norm), tgt)

    dx2, da2, db2, p2, hb2, dyh2, d_ffn2n = _ffn_bwd_tok(dx3, x2, row(ffn2_norm), wcol, wrow, 2, 3, 1, "ffn2_bwd")
    dwg2, dwu2, dwd2 = _ffn_wgrad(hb2, dyh2, da2, db2, p2, "ffn2_wgrad")
    dcat, dwout = _mix_out_bwd(dx2, u, yr, wout)
    dzc, cst = _conv_bwd(dcat, u1, z, conv_w_full, row(conv_ln_g), row(conv_ln_b))
    dzx, dzg, lst, dwa_bd, dwx_bd = _lru_bwd(dcat, C, hs, z, 2 * C, lru_w4_full, row(lru_conv_b), wa_bd,
                                              row(lru_b_a), wx_bd, row(lru_b_x), row(lru_lambda))
    dx1, dwin, d_mixn = _mix_in_bwd(dzc, dzx, dzg, x1, dx2, row(mix_norm), win)
    dx0, da1, db1, p1, hb1, dyh1, d_ffn1n = _ffn_bwd_tok(dx1, xs, row(ffn1_norm), wcol, wrow, 0, 1, 0, "ffn1_bwd")
    dwg1, dwu1, dwd1 = _ffn_wgrad(hb1, dyh1, da1, db1, p1, "ffn1_wgrad")

    big = [dwg1, dwu1, dwd1, dwin, dwout.reshape(N_CHIPS, -1, D), dwg2, dwu2, dwd2]
    big_names = ['ffn1_w_gate', 'ffn1_w_up', 'ffn1_w_down', 'w_in', 'w_out', 'ffn2_w_gate', 'ffn2_w_up',
                 'ffn2_w_down']
    others = _swap_halves_out(big)
    parts = [_add_cast(g, o, cidx, "add_cast_" + n) for g, o, n in zip(big, others, big_names)]
    slots = _exchange_chips(parts)
    halves = [_sum_slots(b, cidx, "sum_slots_" + n) for b, n in zip(slots, big_names)]
    G = dict(zip(big_names, _share_halves(halves)))

    small_names = ['ffn1_norm', 'mix_norm', 'conv_dw', 'conv_dw_bias', 'conv_ln_g', 'conv_ln_b', 'lru_conv_w',
                   'lru_conv_b', 'lru_w_a', 'lru_b_a', 'lru_w_x', 'lru_b_x', 'lru_lambda', 'ffn2_norm',
                   'final_norm']
    small = {
        'ffn1_norm': d_ffn1n, 'mix_norm': d_mixn, 'conv_dw': cst[:K], 'conv_dw_bias': cst[K + 1],
        'conv_ln_g': cst[K + 2], 'conv_ln_b': cst[K + 3], 'lru_conv_w': lst[:K4], 'lru_conv_b': lst[K4],
        'lru_w_a': _block_diag_take(dwa_bd, per), 'lru_b_a': lst[K4 + 1],
        'lru_w_x': _block_diag_take(dwx_bd, per), 'lru_b_x': lst[K4 + 2], 'lru_lambda': lst[K4 + 3],
        'ffn2_norm': d_ffn2n, 'final_norm': d_final,
    }
    full_shapes = [(K, C) if n == 'conv_dw' else (K4, Wl) if n == 'lru_conv_w' else W[n].shape for n in small_names]
    packed, rows = _pack([small[n] for n in small_names])
    summed = _sum_devices(_gather_small(packed))
    for n, gsum in zip(small_names, _unpack(summed, rows, full_shapes)):
        if n == 'conv_dw':
            gsum = lax.dynamic_slice_in_dim(gsum, chip * Cs, Cs, axis=1)
        elif n == 'lru_conv_w':
            gsum = lax.dynamic_slice_in_dim(gsum, chip * lru_conv_w.shape[1], lru_conv_w.shape[1], axis=1)
        G[n] = gsum

    delta, new_m, new_v = {}, {}, {}
    for n in big_names:
        shp = W[n].shape
        g2 = G[n] if G[n].shape == shp else G[n].reshape(shp)
        G[n] = g2
        delta[n], new_m[n], new_v[n] = _adamw(W[n], g2, M[n], V[n], "adamw_" + n)
    pw, prow = _pack([W[n] for n in small_names])
    pg, _ = _pack([G[n] for n in small_names])
    pm, _ = _pack([M[n] for n in small_names])
    pv, _ = _pack([V[n] for n in small_names])
    sd, sm, sv = _adamw(pw, pg, pm, pv, "adamw_small")
    shapes = [W[n].shape for n in small_names]
    for n, a, b, c_ in zip(small_names, _unpack(sd, prow, shapes), _unpack(sm, prow, shapes),
                           _unpack(sv, prow, shapes)):
        delta[n], new_m[n], new_v[n] = a, b, c_

    loss = lax.psum(loss_blk[0, 0], ("x", "y", "c"))
    grad_x = dx0.reshape(x.shape)
    return (loss, grad_x, *[G[n] for n in names], *[delta[n] for n in names],
            *[new_m[n] for n in names], *[new_v[n] for n in names])
```

```python
import functools
import math

import jax
import jax.numpy as jnp
from jax import lax
from jax.experimental import pallas as pl
from jax.experimental.pallas import tpu as pltpu

F32 = jnp.float32
BF16 = jnp.bfloat16
MESH = pl.DeviceIdType.MESH

RMS_EPS = 1e-6
LN_EPS = 1e-5
LRU_C = 8.0
FFN_RES_SCALE = 0.5
ADAM_LR = 0.001
ADAM_B1 = 0.9
ADAM_B2 = 0.999
ADAM_EPS = 1e-08
ADAM_WD = 0.01
ADAM_STEP = 10

LANES = 128
SUBLANES = 8
CONV_HALO = 32
LRU_HALO = 8
ROW_CHUNK = 64
VMEM_LIMIT = 56 * 1024 * 1024
N_CHIPS = 4
N_DEV = 8
TOK_TILE = 1024
BWD_TILE = 512
CONV_TILE = 512
LRU_TILE = 1024


def _dot(a, b):
    return jnp.dot(a, b, preferred_element_type=F32)


def _dot_nt(a, b):
    return lax.dot_general(a, b, (((1,), (1,)), ((), ())), preferred_element_type=F32)


def _dot_tn(a, b):
    return lax.dot_general(a, b, (((0,), (0,)), ((), ())), preferred_element_type=F32)


def _tile(n, pref, mult=SUBLANES):
    for t in range(min(pref, n), 0, -1):
        if n % t == 0 and t % mult == 0:
            return t
    return n


def _params(*sem):
    return pltpu.CompilerParams(dimension_semantics=sem, vmem_limit_bytes=VMEM_LIMIT)


def _rms_stats(x):
    r = lax.rsqrt(jnp.mean(x * x, axis=-1, keepdims=True) + RMS_EPS)
    return x * r, r


def _rms_bwd(dh, xh, r, g):
    dxh = dh * g
    return r * (dxh - xh * jnp.mean(dxh * xh, axis=-1, keepdims=True))


def _colsum(v):
    return jnp.sum(v, axis=0, keepdims=True)


def _ffn_fwd(x, g, wff, name):
    T, D = x.shape
    ns, fs = wff.shape[1], wff.shape[2]
    tm = _tile(T, TOK_TILE)

    def body(x_ref, g_ref, wg_ref, wu_ref, wd_ref, y_ref, a_ref, b_ref, hb_ref, acc_ref):
        j = pl.program_id(1)

        @pl.when(j == 0)
        def _():
            xh, _ = _rms_stats(x_ref[...])
            hb_ref[...] = (xh * g_ref[...]).astype(BF16)
            acc_ref[...] = jnp.zeros_like(acc_ref)

        hb = hb_ref[...]
        a = _dot_nt(hb, wg_ref[...])
        b = _dot_nt(hb, wu_ref[...])
        a_ref[...] = a.astype(BF16)
        b_ref[...] = b.astype(BF16)
        p = (a * jax.nn.sigmoid(a) * b).astype(BF16)
        acc_ref[...] += _dot(p, wd_ref[...])

        @pl.when(j == ns - 1)
        def _():
            y_ref[...] = x_ref[...] + FFN_RES_SCALE * acc_ref[...]

    def wspec(n):
        return pl.BlockSpec((None, None, fs, D), lambda i, j: (n, j, 0, 0))

    mid = pl.BlockSpec((None, tm, fs), lambda i, j: (j, i, 0))
    return pl.pallas_call(
        body, grid=(T // tm, ns),
        in_specs=[pl.BlockSpec((tm, D), lambda i, j: (i, 0)), pl.BlockSpec((1, D), lambda i, j: (0, 0)),
                  wspec(0), wspec(1), wspec(2)],
        out_specs=[pl.BlockSpec((tm, D), lambda i, j: (i, 0)), mid, mid],
        out_shape=[jax.ShapeDtypeStruct((T, D), F32), jax.ShapeDtypeStruct((ns, T, fs), BF16),
                   jax.ShapeDtypeStruct((ns, T, fs), BF16)],
        scratch_shapes=[pltpu.VMEM((tm, D), BF16), pltpu.VMEM((tm, D), F32)],
        compiler_params=_params("parallel", "arbitrary"), name=name)(x, g, wff, wff, wff)


def _ffn_bwd_tok(dy, x, g, a, b, wff, name):
    T, D = x.shape
    ns, fs = wff.shape[1], wff.shape[2]
    tm = _tile(T, BWD_TILE)

    def body(dy_ref, x_ref, g_ref, a_ref, b_ref, wg_ref, wu_ref, wd_ref,
             dx_ref, da_ref, db_ref, p_ref, hb_ref, dyh_ref, dg_ref, dh_ref):
        i, j = pl.program_id(0), pl.program_id(1)

        @pl.when((i == 0) & (j == 0))
        def _():
            dg_ref[...] = jnp.zeros_like(dg_ref)

        @pl.when(j == 0)
        def _():
            xh, _ = _rms_stats(x_ref[...])
            hb_ref[...] = (xh * g_ref[...]).astype(BF16)
            dyh_ref[...] = (FFN_RES_SCALE * dy_ref[...]).astype(BF16)
            dh_ref[...] = jnp.zeros_like(dh_ref)

        av = a_ref[...].astype(F32)
        bv = b_ref[...].astype(F32)
        dp = _dot_nt(dyh_ref[...], wd_ref[...])
        s = jax.nn.sigmoid(av)
        sl = av * s
        da = (dp * bv * (s * (1.0 + av * (1.0 - s)))).astype(BF16)
        db = (dp * sl).astype(BF16)
        da_ref[...] = da
        db_ref[...] = db
        p_ref[...] = (sl * bv).astype(BF16)
        dh_ref[...] += _dot(da, wg_ref[...]) + _dot(db, wu_ref[...])

        @pl.when(j == ns - 1)
        def _():
            xh, r = _rms_stats(x_ref[...])
            dh = dh_ref[...]
            gv = g_ref[...]
            dx_ref[...] = dy_ref[...] + _rms_bwd(dh, xh, r, gv)
            dg_ref[...] += _colsum(dh * xh)

    def wspec(n):
        return pl.BlockSpec((None, None, fs, D), lambda i, j: (n, j, 0, 0))

    tok = pl.BlockSpec((tm, D), lambda i, j: (i, 0))
    mid = pl.BlockSpec((None, tm, fs), lambda i, j: (j, i, 0))
    vec = pl.BlockSpec((1, D), lambda i, j: (0, 0))
    return pl.pallas_call(
        body, grid=(T // tm, ns),
        in_specs=[tok, tok, vec, mid, mid, wspec(0), wspec(1), wspec(2)],
        out_specs=[tok, mid, mid, mid, tok, tok, vec],
        out_shape=[jax.ShapeDtypeStruct((T, D), F32),
                   jax.ShapeDtypeStruct((ns, T, fs), BF16), jax.ShapeDtypeStruct((ns, T, fs), BF16),
                   jax.ShapeDtypeStruct((ns, T, fs), BF16),
                   jax.ShapeDtypeStruct((T, D), BF16), jax.ShapeDtypeStruct((T, D), BF16),
                   jax.ShapeDtypeStruct((1, D), F32)],
        scratch_shapes=[pltpu.VMEM((tm, D), F32)],
        compiler_params=_params("arbitrary", "arbitrary"), name=name)(dy, x, g, a, b, wff, wff, wff)


def _ffn_wgrad(hb, dyh, da, db, p, name):
    T, D = hb.shape
    ns, _, fs = da.shape
    tm = _tile(T, TOK_TILE)

    def body(hb_ref, dyh_ref, da_ref, db_ref, p_ref, dwg_ref, dwu_ref, dwd_ref):
        @pl.when(pl.program_id(1) == 0)
        def _():
            dwg_ref[...] = jnp.zeros_like(dwg_ref)
            dwu_ref[...] = jnp.zeros_like(dwu_ref)
            dwd_ref[...] = jnp.zeros_like(dwd_ref)

        hbv = hb_ref[...]
        dwg_ref[...] += _dot_tn(da_ref[...], hbv)
        dwu_ref[...] += _dot_tn(db_ref[...], hbv)
        dwd_ref[...] += _dot_tn(p_ref[...], dyh_ref[...])

    tok = pl.BlockSpec((tm, D), lambda j, i: (i, 0))
    mid = pl.BlockSpec((None, tm, fs), lambda j, i: (j, i, 0))
    wsp = pl.BlockSpec((None, fs, D), lambda j, i: (j, 0, 0))
    sds = jax.ShapeDtypeStruct((ns, fs, D), F32)
    return pl.pallas_call(
        body, grid=(ns, T // tm),
        in_specs=[tok, tok, mid, mid, mid], out_specs=[wsp, wsp, wsp], out_shape=[sds, sds, sds],
        compiler_params=_params("parallel", "arbitrary"), name=name)(hb, dyh, da, db, p)


def _mix_in_fwd(x, g, win):
    T, D = x.shape
    ns, ws = win.shape[0], win.shape[2]
    tm = _tile(T, TOK_TILE)

    def body(x_ref, g_ref, w_ref, z_ref):
        xh, _ = _rms_stats(x_ref[...])
        hb = (xh * g_ref[...]).astype(BF16)
        for j in range(ns):
            z_ref[:, pl.ds(j * ws, ws)] = _dot(hb, w_ref[j])

    return pl.pallas_call(
        body, grid=(T // tm,),
        in_specs=[pl.BlockSpec((tm, D), lambda i: (i, 0)), pl.BlockSpec((1, D), lambda i: (0, 0)),
                  pl.BlockSpec((ns, D, ws), lambda i: (0, 0, 0), pipeline_mode=pl.Buffered(1))],
        out_specs=pl.BlockSpec((tm, ns * ws), lambda i: (i, 0)),
        out_shape=jax.ShapeDtypeStruct((T, ns * ws), F32),
        compiler_params=_params("parallel"), name="mix_in_fwd")(x, g, win)


def _tap_sum(buf, w_ref, ntaps, first_row, r0, rows, flip):
    acc = None
    for k in range(ntaps):
        off = (ntaps - 1 - k) if flip else k
        t = buf[pl.ds(first_row + r0 + off, rows), :] * w_ref[pl.ds(k, 1), :]
        acc = t if acc is None else acc + t
    return acc


def _conv_fwd(z, w, bias, lng, lnb):
    T = z.shape[0]
    K, C = w.shape
    tm = _tile(T, CONV_TILE, ROW_CHUNK)
    rc = min(ROW_CHUNK, tm)

    def body(cv_ref, cg_ref, w_ref, b_ref, g_ref, bb_ref, u_ref, u1_ref, buf):
        @pl.when(pl.program_id(0) == 0)
        def _():
            buf[pl.ds(0, CONV_HALO), :] = jnp.zeros((CONV_HALO, C), F32)

        buf[pl.ds(CONV_HALO, tm), :] = cv_ref[...] * jax.nn.sigmoid(cg_ref[...])
        for r0 in range(0, tm, rc):
            u1 = _tap_sum(buf, w_ref, K, CONV_HALO - (K - 1), r0, rc, False) + b_ref[...]
            u1_ref[pl.ds(r0, rc), :] = u1
            xc = u1 - jnp.mean(u1, axis=-1, keepdims=True)
            xh = xc * lax.rsqrt(jnp.mean(xc * xc, axis=-1, keepdims=True) + LN_EPS)
            u2 = xh * g_ref[...] + bb_ref[...]
            u_ref[pl.ds(r0, rc), :] = (u2 * jax.nn.sigmoid(u2)).astype(BF16)
        buf[pl.ds(0, CONV_HALO), :] = buf[pl.ds(tm, CONV_HALO), :]

    vec = pl.BlockSpec((1, C), lambda i: (0, 0))
    return pl.pallas_call(
        body, grid=(T // tm,),
        in_specs=[pl.BlockSpec((tm, C), lambda i: (i, 0)), pl.BlockSpec((tm, C), lambda i: (i, 1)),
                  pl.BlockSpec((K, C), lambda i: (0, 0)), vec, vec, vec],
        out_specs=[pl.BlockSpec((tm, C), lambda i: (i, 0)), pl.BlockSpec((tm, C), lambda i: (i, 0))],
        out_shape=[jax.ShapeDtypeStruct((T, C), BF16), jax.ShapeDtypeStruct((T, C), F32)],
        scratch_shapes=[pltpu.VMEM((CONV_HALO + tm, C), F32)],
        compiler_params=_params("arbitrary"), name="conv_fwd")(z, z, w, bias, lng, lnb)


def _conv_bwd(dcat, u1, z, w, lng, lnb):
    T = z.shape[0]
    K, C = w.shape
    tm = _tile(T, CONV_TILE, ROW_CHUNK)
    rc = min(ROW_CHUNK, tm)
    nI = T // tm
    hb = tm // CONV_HALO
    srows = ((K + 4 + SUBLANES - 1) // SUBLANES) * SUBLANES

    def body(du_ref, u1_ref, cv_ref, cg_ref, cvp_ref, cgp_ref, w_ref, g_ref, bb_ref,
             dz_ref, st_ref, u0buf, d1buf):
        i = pl.program_id(0)
        ti = nI - 1 - i

        @pl.when(i == 0)
        def _():
            st_ref[...] = jnp.zeros_like(st_ref)
            d1buf[pl.ds(tm, CONV_HALO), :] = jnp.zeros((CONV_HALO, C), F32)

        prev = cvp_ref[...] * jax.nn.sigmoid(cgp_ref[...])
        u0buf[pl.ds(0, CONV_HALO), :] = jnp.where(ti == 0, 0.0, prev)
        u0buf[pl.ds(CONV_HALO, tm), :] = cv_ref[...] * jax.nn.sigmoid(cg_ref[...])

        gv = g_ref[...]
        dbias = jnp.zeros((1, C), F32)
        dgain = jnp.zeros((1, C), F32)
        dlnb = jnp.zeros((1, C), F32)
        for r0 in range(0, tm, rc):
            u1 = u1_ref[pl.ds(r0, rc), :]
            xc = u1 - jnp.mean(u1, axis=-1, keepdims=True)
            rstd = lax.rsqrt(jnp.mean(xc * xc, axis=-1, keepdims=True) + LN_EPS)
            xh = xc * rstd
            u2 = xh * gv + bb_ref[...]
            s = jax.nn.sigmoid(u2)
            du2 = du_ref[pl.ds(r0, rc), :] * (s * (1.0 + u2 * (1.0 - s)))
            dgain = dgain + _colsum(du2 * xh)
            dlnb = dlnb + _colsum(du2)
            dxh = du2 * gv
            du1 = rstd * (dxh - jnp.mean(dxh, axis=-1, keepdims=True)
                          - xh * jnp.mean(dxh * xh, axis=-1, keepdims=True))
            dbias = dbias + _colsum(du1)
            d1buf[pl.ds(r0, rc), :] = du1
        st_ref[pl.ds(K + 1, 1), :] += dbias
        st_ref[pl.ds(K + 2, 1), :] += dgain
        st_ref[pl.ds(K + 3, 1), :] += dlnb

        for k in range(K):
            acc = jnp.zeros((1, C), F32)
            for r0 in range(0, tm, rc):
                acc = acc + _colsum(d1buf[pl.ds(r0, rc), :]
                                    * u0buf[pl.ds(CONV_HALO - (K - 1) + k + r0, rc), :])
            st_ref[pl.ds(k, 1), :] += acc

        for r0 in range(0, tm, rc):
            du0 = _tap_sum(d1buf, w_ref, K, 0, r0, rc, True)
            cv = cv_ref[pl.ds(r0, rc), :]
            sg = jax.nn.sigmoid(cg_ref[pl.ds(r0, rc), :])
            dz_ref[pl.ds(r0, rc), pl.ds(0, C)] = (du0 * sg).astype(BF16)
            dz_ref[pl.ds(r0, rc), pl.ds(C, C)] = (du0 * cv * sg * (1.0 - sg)).astype(BF16)
        d1buf[pl.ds(tm, CONV_HALO), :] = d1buf[pl.ds(0, CONV_HALO), :]

    def rev(col):
        return lambda i: (nI - 1 - i, col)

    def rev_prev(col):
        return lambda i: (jnp.maximum((nI - 1 - i) * hb - 1, 0), col)

    vec = pl.BlockSpec((1, C), lambda i: (0, 0))
    return pl.pallas_call(
        body, grid=(nI,),
        in_specs=[pl.BlockSpec((tm, C), rev(0)), pl.BlockSpec((tm, C), rev(0)),
                  pl.BlockSpec((tm, C), rev(0)), pl.BlockSpec((tm, C), rev(1)),
                  pl.BlockSpec((CONV_HALO, C), rev_prev(0)), pl.BlockSpec((CONV_HALO, C), rev_prev(1)),
                  pl.BlockSpec((K, C), lambda i: (0, 0)), vec, vec],
        out_specs=[pl.BlockSpec((tm, 2 * C), rev(0)), pl.BlockSpec((srows, C), lambda i: (0, 0))],
        out_shape=[jax.ShapeDtypeStruct((T, 2 * C), BF16), jax.ShapeDtypeStruct((srows, C), F32)],
        scratch_shapes=[pltpu.VMEM((CONV_HALO + tm, C), F32), pltpu.VMEM((tm + CONV_HALO, C), F32)],
        compiler_params=_params("arbitrary"), name="conv_bwd")(dcat, u1, z, z, z, z, w, lng, lnb)


def _softplus(v):
    return jnp.maximum(v, 0.0) + jnp.log(1.0 + jnp.exp(-jnp.abs(v)))


def _gelu(v):
    c = math.sqrt(2.0 / math.pi)
    t = jnp.tanh(c * (v + 0.044715 * v * v * v))
    gl = 0.5 * v * (1.0 + t)
    dgl = 0.5 * (1.0 + t) + 0.5 * v * (1.0 - t * t) * c * (1.0 + 3.0 * 0.044715 * v * v)
    return gl, dgl


def _lru_gates(xr, wa, ba, wx, bx, lam):
    xb = xr.astype(BF16)
    r = jax.nn.sigmoid(_dot(xb, wa) + ba)
    ig = jax.nn.sigmoid(_dot(xb, wx) + bx)
    sp = _softplus(-lam)
    log_a = -LRU_C * r * sp
    a = jnp.exp(log_a)
    y = 2.0 * log_a
    series = -(y * (1.0 + y * (0.5 + y * (1.0 / 6.0 + y * (1.0 / 24.0)))))
    mult = jnp.sqrt(jnp.where(y > -0.02, series, 1.0 - jnp.exp(y)))
    return a, mult, r, ig, sp


def _scan_tile(a_s, b_s, carry, seg, reverse):
    def step(n, hp):
        hl, pr = hp
        k = (seg - 1 - n) if reverse else n
        rows = pl.ds(k, SUBLANES, stride=seg)
        av = a_s[rows, :]
        hl = av * hl + b_s[rows, :]
        pr = av * pr
        b_s[rows, :] = hl
        a_s[rows, :] = pr
        return hl, pr

    hl, pr = lax.fori_loop(0, seg, step, (jnp.zeros((SUBLANES, LANES), F32), jnp.ones((SUBLANES, LANES), F32)),
                           unroll=min(8, seg))
    cs = [None] * SUBLANES
    c = carry
    for s in (range(SUBLANES - 1, -1, -1) if reverse else range(SUBLANES)):
        cs[s] = c
        c = hl[s:s + 1, :] + pr[s:s + 1, :] * c
    return cs, c


def _lru_fwd(z, col0, w4, b4, wa, ba, wx, bx, lam):
    T = z.shape[0]
    K4, W = w4.shape
    nC = W // LANES
    tm = _tile(T, LRU_TILE, SUBLANES * SUBLANES)
    seg = tm // SUBLANES
    cx, cg = col0 // LANES, (col0 + W) // LANES

    def body(rx_ref, rg_ref, w4_ref, b4_ref, wa_ref, ba_ref, wx_ref, bx_ref, lam_ref,
             yr_ref, hs_ref, xbuf, a_s, b_s, hc):
        @pl.when(pl.program_id(1) == 0)
        def _():
            xbuf[pl.ds(0, LRU_HALO), :] = jnp.zeros((LRU_HALO, LANES), F32)
            hc[...] = jnp.zeros_like(hc)

        xbuf[pl.ds(LRU_HALO, tm), :] = rx_ref[...]
        xr = _tap_sum(xbuf, w4_ref, K4, LRU_HALO - (K4 - 1), 0, tm, False) + b4_ref[...]
        a, mult, _, ig, _ = _lru_gates(xr, wa_ref[...], ba_ref[...], wx_ref[...], bx_ref[...], lam_ref[...])
        a_s[...] = a
        b_s[...] = mult * ig * xr
        cs, cout = _scan_tile(a_s, b_s, hc[pl.ds(0, 1), :], seg, False)
        hc[pl.ds(0, 1), :] = cout
        for s in range(SUBLANES):
            rows = pl.ds(s * seg, seg)
            h = b_s[rows, :] + a_s[rows, :] * cs[s]
            hs_ref[rows, :] = h
            gl, _ = _gelu(rg_ref[rows, :])
            yr_ref[rows, :] = (h * gl).astype(BF16)
        xbuf[pl.ds(0, LRU_HALO), :] = xbuf[pl.ds(tm, LRU_HALO), :]

    vec = pl.BlockSpec((1, LANES), lambda c, i: (0, c))
    mat = pl.BlockSpec((None, LANES, LANES), lambda c, i: (c, 0, 0))
    return pl.pallas_call(
        body, grid=(nC, T // tm),
        in_specs=[pl.BlockSpec((tm, LANES), lambda c, i: (i, cx + c)),
                  pl.BlockSpec((tm, LANES), lambda c, i: (i, cg + c)),
                  pl.BlockSpec((K4, LANES), lambda c, i: (0, c)), vec, mat, vec, mat, vec, vec],
        out_specs=[pl.BlockSpec((tm, LANES), lambda c, i: (i, c)), pl.BlockSpec((tm, LANES), lambda c, i: (i, c))],
        out_shape=[jax.ShapeDtypeStruct((T, W), BF16), jax.ShapeDtypeStruct((T, W), F32)],
        scratch_shapes=[pltpu.VMEM((LRU_HALO + tm, LANES), F32), pltpu.VMEM((tm, LANES), F32),
                        pltpu.VMEM((tm, LANES), F32), pltpu.VMEM((SUBLANES, LANES), F32)],
        compiler_params=_params("parallel", "arbitrary"), name="lru_fwd")(z, z, w4, b4, wa, ba, wx, bx, lam)


def _lru_bwd(dcat, dcol0, hs, z, col0, w4, b4, wa, ba, wx, bx, lam):
    T = z.shape[0]
    K4, W = w4.shape
    assert K4 + 4 == SUBLANES
    nC = W // LANES
    tm = _tile(T, LRU_TILE, SUBLANES * SUBLANES)
    seg = tm // SUBLANES
    nI = T // tm
    hb = tm // LRU_HALO
    cx, cg, cd = col0 // LANES, (col0 + W) // LANES, dcol0 // LANES

    def body(dyr_ref, hs_ref, hsp_ref, rx_ref, rxp_ref, rg_ref, w4_ref, b4_ref, wa_ref, ba_ref, wx_ref, bx_ref,
             lam_ref, dzx_ref, dzg_ref, st_ref, dwa_ref, dwx_ref, xbuf, hbuf, abuf, a_s, b_s, dbuf, gc, anc):
        i = pl.program_id(1)
        ti = nI - 1 - i

        @pl.when(i == 0)
        def _():
            st_ref[...] = jnp.zeros_like(st_ref)
            dwa_ref[...] = jnp.zeros_like(dwa_ref)
            dwx_ref[...] = jnp.zeros_like(dwx_ref)
            gc[...] = jnp.zeros_like(gc)
            anc[...] = jnp.zeros_like(anc)
            dbuf[pl.ds(tm, LRU_HALO), :] = jnp.zeros((LRU_HALO, LANES), F32)

        xbuf[pl.ds(0, LRU_HALO), :] = jnp.where(ti == 0, 0.0, rxp_ref[...])
        xbuf[pl.ds(LRU_HALO, tm), :] = rx_ref[...]
        hbuf[pl.ds(0, LRU_HALO), :] = jnp.where(ti == 0, 0.0, hsp_ref[...])
        hbuf[pl.ds(LRU_HALO, tm), :] = hs_ref[...]

        wa, wx = wa_ref[...], wx_ref[...]
        lam_v = lam_ref[...]
        xr = _tap_sum(xbuf, w4_ref, K4, LRU_HALO - (K4 - 1), 0, tm, False) + b4_ref[...]
        a, mult, r, ig, sp = _lru_gates(xr, wa, ba_ref[...], wx, bx_ref[...], lam_v)

        dyr = dyr_ref[...]
        gl, dgl = _gelu(rg_ref[...])
        dzg_ref[...] = (dyr * hs_ref[...] * dgl).astype(BF16)

        abuf[pl.ds(0, tm), :] = a
        abuf[pl.ds(tm, LRU_HALO), :] = anc[...]
        a_s[...] = abuf[pl.ds(1, tm), :]
        b_s[...] = dyr * gl
        cs, cout = _scan_tile(a_s, b_s, gc[pl.ds(0, 1), :], seg, True)
        gc[pl.ds(0, 1), :] = cout
        anc[pl.ds(0, 1), :] = a[0:1, :]
        for s in range(SUBLANES):
            rows = pl.ds(s * seg, seg)
            b_s[rows, :] = b_s[rows, :] + a_s[rows, :] * cs[s]
        g = b_s[...]

        d_a = g * hbuf[pl.ds(LRU_HALO - 1, tm), :]
        gx_ = g * xr
        d_log_a = d_a * a - (gx_ * ig) * (a * a / mult)
        dga = (d_log_a * (-LRU_C * sp)) * r * (1.0 - r)
        dgx = (gx_ * mult) * ig * (1.0 - ig)
        dga_b, dgx_b = dga.astype(BF16), dgx.astype(BF16)
        dxr = g * mult * ig + _dot_nt(dga_b, wa) + _dot_nt(dgx_b, wx)
        xb = xr.astype(BF16)
        dwa_ref[...] += _dot_tn(xb, dga_b)
        dwx_ref[...] += _dot_tn(xb, dgx_b)
        st_ref[pl.ds(K4, 1), :] += _colsum(dxr)
        st_ref[pl.ds(K4 + 1, 1), :] += _colsum(dga)
        st_ref[pl.ds(K4 + 2, 1), :] += _colsum(dgx)
        st_ref[pl.ds(K4 + 3, 1), :] += _colsum(d_log_a * (-LRU_C * r)) * (-jax.nn.sigmoid(-lam_v))

        dbuf[pl.ds(0, tm), :] = dxr
        for k in range(K4):
            st_ref[pl.ds(k, 1), :] += _colsum(dxr * xbuf[pl.ds(LRU_HALO - (K4 - 1) + k, tm), :])
        dzx_ref[...] = _tap_sum(dbuf, w4_ref, K4, 0, 0, tm, True).astype(BF16)
        dbuf[pl.ds(tm, LRU_HALO), :] = dbuf[pl.ds(0, LRU_HALO), :]

    def rev(col):
        return lambda c, i: (nI - 1 - i, col + c)

    def rev_prev(col):
        return lambda c, i: (jnp.maximum((nI - 1 - i) * hb - 1, 0), col + c)

    vec = pl.BlockSpec((1, LANES), lambda c, i: (0, c))
    mat = pl.BlockSpec((None, LANES, LANES), lambda c, i: (c, 0, 0))
    big = pltpu.VMEM((tm, LANES), F32)
    halo = pltpu.VMEM((tm + LRU_HALO, LANES), F32)
    return pl.pallas_call(
        body, grid=(nC, nI),
        in_specs=[pl.BlockSpec((tm, LANES), rev(cd)),
                  pl.BlockSpec((tm, LANES), rev(0)), pl.BlockSpec((LRU_HALO, LANES), rev_prev(0)),
                  pl.BlockSpec((tm, LANES), rev(cx)), pl.BlockSpec((LRU_HALO, LANES), rev_prev(cx)),
                  pl.BlockSpec((tm, LANES), rev(cg)),
                  pl.BlockSpec((K4, LANES), lambda c, i: (0, c)), vec, mat, vec, mat, vec, vec],
        out_specs=[pl.BlockSpec((tm, LANES), rev(0)), pl.BlockSpec((tm, LANES), rev(0)),
                   pl.BlockSpec((SUBLANES, LANES), lambda c, i: (0, c)), mat, mat],
        out_shape=[jax.ShapeDtypeStruct((T, W), BF16), jax.ShapeDtypeStruct((T, W), BF16),
                   jax.ShapeDtypeStruct((SUBLANES, W), F32),
                   jax.ShapeDtypeStruct((nC, LANES, LANES), F32), jax.ShapeDtypeStruct((nC, LANES, LANES), F32)],
        scratch_shapes=[halo, halo, halo, big, big, halo,
                        pltpu.VMEM((SUBLANES, LANES), F32), pltpu.VMEM((SUBLANES, LANES), F32)],
        compiler_params=_params("parallel", "arbitrary"), name="lru_bwd")(
            dcat, hs, hs, z, z, z, w4, b4, wa, ba, wx, bx, lam)


def _mix_out_fwd(x, u, yr, wout):
    T, D = x.shape
    C, W = u.shape[1], yr.shape[1]
    tm = _tile(T, TOK_TILE)

    def body(x_ref, u_ref, yr_ref, w_ref, y_ref):
        y_ref[...] = (x_ref[...] + _dot(u_ref[...], w_ref[pl.ds(0, C), :])
                      + _dot(yr_ref[...], w_ref[pl.ds(C, W), :]))

    return pl.pallas_call(
        body, grid=(T // tm,),
        in_specs=[pl.BlockSpec((tm, D), lambda i: (i, 0)), pl.BlockSpec((tm, C), lambda i: (i, 0)),
                  pl.BlockSpec((tm, W), lambda i: (i, 0)),
                  pl.BlockSpec((C + W, D), lambda i: (0, 0), pipeline_mode=pl.Buffered(1))],
        out_specs=pl.BlockSpec((tm, D), lambda i: (i, 0)),
        out_shape=jax.ShapeDtypeStruct((T, D), F32),
        compiler_params=_params("parallel"), name="mix_out_fwd")(x, u, yr, wout)


def _mix_out_bwd(dy, u, yr, wout):
    T, D = dy.shape
    C, W = u.shape[1], yr.shape[1]
    tm = _tile(T, BWD_TILE)

    def body(dy_ref, u_ref, yr_ref, w_ref, dcat_ref, dw_ref):
        @pl.when(pl.program_id(0) == 0)
        def _():
            dw_ref[...] = jnp.zeros_like(dw_ref)

        dyb = dy_ref[...].astype(BF16)
        dcat_ref[...] = _dot_nt(dyb, w_ref[...])
        dw_ref[pl.ds(0, C), :] += _dot_tn(u_ref[...], dyb)
        dw_ref[pl.ds(C, W), :] += _dot_tn(yr_ref[...], dyb)

    return pl.pallas_call(
        body, grid=(T // tm,),
        in_specs=[pl.BlockSpec((tm, D), lambda i: (i, 0)), pl.BlockSpec((tm, C), lambda i: (i, 0)),
                  pl.BlockSpec((tm, W), lambda i: (i, 0)),
                  pl.BlockSpec((C + W, D), lambda i: (0, 0), pipeline_mode=pl.Buffered(1))],
        out_specs=[pl.BlockSpec((tm, C + W), lambda i: (i, 0)), pl.BlockSpec((C + W, D), lambda i: (0, 0))],
        out_shape=[jax.ShapeDtypeStruct((T, C + W), F32), jax.ShapeDtypeStruct((C + W, D), F32)],
        compiler_params=_params("arbitrary"), name="mix_out_bwd")(dy, u, yr, wout)


def _mix_in_bwd(dzc, dzx, dzg, x, dy, g, win):
    T, D = x.shape
    ns, ws = win.shape[0], win.shape[2]
    tm = _tile(T, BWD_TILE)
    parts = []
    for j in range(ns):
        lo = j * ws
        if lo < dzc.shape[1]:
            parts.append((0, lo))
        elif lo < dzc.shape[1] + dzx.shape[1]:
            parts.append((1, lo - dzc.shape[1]))
        else:
            parts.append((2, lo - dzc.shape[1] - dzx.shape[1]))

    def body(dzc_ref, dzx_ref, dzg_ref, x_ref, dy_ref, g_ref, w_ref, dx_ref, dw_ref, dg_ref):
        @pl.when(pl.program_id(0) == 0)
        def _():
            dw_ref[...] = jnp.zeros_like(dw_ref)
            dg_ref[...] = jnp.zeros_like(dg_ref)

        xh, r = _rms_stats(x_ref[...])
        gv = g_ref[...]
        hb = (xh * gv).astype(BF16)
        srcs = (dzc_ref, dzx_ref, dzg_ref)
        dh = jnp.zeros((tm, D), F32)
        for j, (si, off) in enumerate(parts):
            dzj = srcs[si][:, pl.ds(off, ws)]
            dh = dh + _dot_nt(dzj, w_ref[j])
            dw_ref[j] += _dot_tn(hb, dzj)
        dx_ref[...] = dy_ref[...] + _rms_bwd(dh, xh, r, gv)
        dg_ref[...] += _colsum(dh * xh)

    def tok(n):
        return pl.BlockSpec((tm, n), lambda i: (i, 0))

    vec = pl.BlockSpec((1, D), lambda i: (0, 0))
    return pl.pallas_call(
        body, grid=(T // tm,),
        in_specs=[tok(dzc.shape[1]), tok(dzx.shape[1]), tok(dzg.shape[1]), tok(D), tok(D), vec,
                  pl.BlockSpec((ns, D, ws), lambda i: (0, 0, 0), pipeline_mode=pl.Buffered(1))],
        out_specs=[tok(D), pl.BlockSpec((ns, D, ws), lambda i: (0, 0, 0)), vec],
        out_shape=[jax.ShapeDtypeStruct((T, D), F32), jax.ShapeDtypeStruct((ns, D, ws), F32),
                   jax.ShapeDtypeStruct((1, D), F32)],
        compiler_params=_params("arbitrary"), name="mix_in_bwd")(dzc, dzx, dzg, x, dy, g, win)


def _final_loss(x, g, tgt):
    T, D = x.shape
    tm = _tile(T, TOK_TILE)

    def body(x_ref, g_ref, t_ref, dx_ref, loss_ref, dg_ref):
        @pl.when(pl.program_id(0) == 0)
        def _():
            loss_ref[...] = jnp.zeros_like(loss_ref)
            dg_ref[...] = jnp.zeros_like(dg_ref)

        xh, r = _rms_stats(x_ref[...])
        gv = g_ref[...]
        e = xh * gv - t_ref[...]
        loss_ref[...] += 0.5 * jnp.sum(jnp.mean(e * e, axis=-1, keepdims=True))
        dy = e * (1.0 / D)
        dg_ref[...] += _colsum(dy * xh)
        dx_ref[...] = _rms_bwd(dy, xh, r, gv)

    tok = pl.BlockSpec((tm, D), lambda i: (i, 0))
    vec = pl.BlockSpec((1, D), lambda i: (0, 0))
    return pl.pallas_call(
        body, grid=(T // tm,),
        in_specs=[tok, vec, tok],
        out_specs=[tok, pl.BlockSpec((SUBLANES, LANES), lambda i: (0, 0)), vec],
        out_shape=[jax.ShapeDtypeStruct((T, D), F32), jax.ShapeDtypeStruct((SUBLANES, LANES), F32),
                   jax.ShapeDtypeStruct((1, D), F32)],
        compiler_params=_params("arbitrary"), name="final_loss")(x, g, tgt)


def _adamw(w, g, m, v, name):
    R, Cc = w.shape
    tr = _tile(R, max(SUBLANES, (1 << 19) // Cc))
    c1 = 1.0 - ADAM_B1 ** ADAM_STEP
    c2 = 1.0 - ADAM_B2 ** ADAM_STEP

    def body(w_ref, g_ref, m_ref, v_ref, d_ref, nm_ref, nv_ref):
        gv = g_ref[...]
        nm = ADAM_B1 * m_ref[...] + (1.0 - ADAM_B1) * gv
        nv = ADAM_B2 * v_ref[...] + (1.0 - ADAM_B2) * (gv * gv)
        nm_ref[...] = nm
        nv_ref[...] = nv
        d_ref[...] = -ADAM_LR * ((nm / c1) / (jnp.sqrt(nv / c2) + ADAM_EPS) + ADAM_WD * w_ref[...])

    blk = pl.BlockSpec((tr, Cc), lambda i: (i, 0))
    sds = jax.ShapeDtypeStruct((R, Cc), F32)
    return pl.pallas_call(
        body, grid=(R // tr,), in_specs=[blk] * 4, out_specs=[blk] * 3, out_shape=[sds] * 3,
        compiler_params=_params("parallel"), name=name)(w, g, m, v)


def _here():
    return lax.axis_index("x"), lax.axis_index("y"), lax.axis_index("c")


def _chip_at(x, y, m):
    return x ^ (m >> 1), y ^ (m & 1)


ANY = pl.BlockSpec(memory_space=pl.ANY)


def _gather_weights(shards):
    n = len(shards)
    out_shapes = [jax.ShapeDtypeStruct(s.shape[:-2] + (N_CHIPS,) + s.shape[-2:], s.dtype) for s in shards]

    def body(*refs):
        ins, outs = refs[:n], refs[n:2 * n]
        send1, recv1, send2, recv2, lsem = refs[2 * n:]
        x, y, c = _here()
        own = 2 * x + y

        def half(ref, chip, cc):
            rh = ref.shape[-2] // 2
            lead = (slice(None),) * (len(ref.shape) - 3)
            return ref.at[lead + (chip, pl.ds(cc * rh, rh), slice(None))]

        def src_half(ref, cc):
            rh = ref.shape[-2] // 2
            lead = (slice(None),) * (len(ref.shape) - 2)
            return ref.at[lead + (pl.ds(cc * rh, rh), slice(None))]

        local = []
        for k in range(n):
            lead = (slice(None),) * (len(outs[k].shape) - 3)
            cp = pltpu.make_async_copy(ins[k], outs[k].at[lead + (own,)], lsem.at[k])
            cp.start()
            local.append(cp)

        first = []
        for k in range(n):
            for m in (1, 2, 3):
                px, py = _chip_at(x, y, m)
                cp = pltpu.make_async_remote_copy(
                    src_ref=src_half(ins[k], c), dst_ref=half(outs[k], own, c),
                    send_sem=send1.at[k, m - 1], recv_sem=recv1.at[k, m - 1],
                    device_id=(px, py, c), device_id_type=MESH)
                cp.start()
                first.append(cp)

        passed = []
        for k in range(n):
            for m in (1, 2, 3):
                px, py = _chip_at(x, y, m)
                peer = 2 * px + py
                got = half(outs[k], peer, c)
                pltpu.make_async_remote_copy(
                    src_ref=got, dst_ref=got, send_sem=send1.at[k, m - 1], recv_sem=recv1.at[k, m - 1],
                    device_id=(px, py, c), device_id_type=MESH).wait_recv()
                cp = pltpu.make_async_remote_copy(
                    src_ref=got, dst_ref=got, send_sem=send2.at[k, m - 1], recv_sem=recv2.at[k, m - 1],
                    device_id=(x, y, 1 - c), device_id_type=MESH)
                cp.start()
                passed.append(cp)

        for k in range(n):
            for m in (1, 2, 3):
                px, py = _chip_at(x, y, m)
                other = half(outs[k], 2 * px + py, 1 - c)
                pltpu.make_async_remote_copy(
                    src_ref=other, dst_ref=other, send_sem=send2.at[k, m - 1], recv_sem=recv2.at[k, m - 1],
                    device_id=(x, y, 1 - c), device_id_type=MESH).wait_recv()
        for cp in first + passed:
            cp.wait_send()
        for cp in local:
            cp.wait()

    return pl.pallas_call(
        body, in_specs=[ANY] * n, out_specs=[ANY] * n, out_shape=out_shapes,
        scratch_shapes=[pltpu.SemaphoreType.DMA((n, 3)), pltpu.SemaphoreType.DMA((n, 3)),
                        pltpu.SemaphoreType.DMA((n, 3)), pltpu.SemaphoreType.DMA((n, 3)),
                        pltpu.SemaphoreType.DMA((n,))],
        name="gather_weights")(*shards)


HBM = pl.BlockSpec(memory_space=pltpu.HBM)
SEM = pl.BlockSpec(memory_space=pltpu.SEMAPHORE)
EFFECT = pltpu.SideEffectType.DATAFLOW_SIDE_EFFECTING


def _in_hbm(a):
    return pltpu.with_memory_space_constraint(a, pltpu.HBM)


def _place_own(shards, name):
    n = len(shards)
    out_shapes = [jax.ShapeDtypeStruct(s.shape[:-2] + (N_CHIPS,) + s.shape[-2:], s.dtype) for s in shards]

    def body(*refs):
        ins, outs, lsem = refs[:n], refs[n:2 * n], refs[2 * n]
        x, y, _ = _here()
        own = 2 * x + y
        cps = []
        for k in range(n):
            lead = (slice(None),) * (len(outs[k].shape) - 3)
            cp = pltpu.make_async_copy(ins[k], outs[k].at[lead + (own,)], lsem.at[k])
            cp.start()
            cps.append(cp)
        for cp in cps:
            cp.wait()

    return pl.pallas_call(
        body, in_specs=[ANY] * n, out_specs=[ANY] * n, out_shape=out_shapes,
        scratch_shapes=[pltpu.SemaphoreType.DMA((n,))], name=name)(*shards)


def _gather_copies(shard_refs, land_refs, send, recv):
    x, y, c = _here()
    own = 2 * x + y
    cps = []
    for k in range(len(shard_refs)):
        lead = (slice(None),) * (len(land_refs[k].shape) - 3)
        for m in (1, 2, 3):
            px, py = _chip_at(x, y, m)
            cps.append(pltpu.make_async_remote_copy(
                src_ref=shard_refs[k], dst_ref=land_refs[k].at[lead + (own,)],
                send_sem=send.at[3 * k + m - 1], recv_sem=recv.at[3 * k + m - 1],
                device_id=(px, py, c), device_id_type=MESH))
    return cps


def _gather_start(shards, lands, after, name):
    n = len(shards)

    def body(*refs):
        ins, lz = refs[:n], refs[n:2 * n]
        send, recv = refs[2 * n + 1], refs[2 * n + 2]
        token = refs[-1]
        for cp in _gather_copies(ins, lz, send, recv):
            cp.start()
        token[...] = jnp.zeros_like(token)

    hbm = [pltpu.HBM(a.shape, a.dtype) for a in list(shards) + list(lands)]
    outs = pl.pallas_call(
        body, name=name,
        in_specs=[HBM] * (2 * n) + [ANY],
        out_specs=[SEM, SEM] + [HBM] * (2 * n) + [pl.BlockSpec(memory_space=pltpu.VMEM)],
        out_shape=[pltpu.SemaphoreType.DMA((3 * n,)), pltpu.SemaphoreType.DMA((3 * n,))] + hbm
        + [jax.ShapeDtypeStruct((SUBLANES, LANES), F32)],
        input_output_aliases={k: 2 + k for k in range(2 * n)},
        compiler_params=pltpu.CompilerParams(has_side_effects=EFFECT),
    )(*[_in_hbm(a) for a in shards], *[_in_hbm(a) for a in lands], after)
    return outs[0], outs[1], outs[2:2 + n], outs[2 + n:2 + 2 * n], outs[-1]


def _gather_wait(send, recv, shards, lands, after, name):
    n = len(shards)

    def body(*refs):
        ins, lz = refs[:n], refs[n:2 * n]
        send_r, recv_r = refs[2 * n], refs[2 * n + 1]
        for cp in _gather_copies(ins, lz, send_r, recv_r):
            cp.wait_send()
            cp.wait_recv()

    hbm = [pltpu.HBM(a.shape, a.dtype) for a in list(shards) + list(lands)]
    outs = pl.pallas_call(
        body, name=name,
        in_specs=[HBM] * (2 * n) + [SEM, SEM, ANY],
        out_specs=[HBM] * (2 * n), out_shape=hbm,
        input_output_aliases={k: k for k in range(2 * n)},
        compiler_params=pltpu.CompilerParams(has_side_effects=EFFECT),
    )(*shards, *lands, send, recv, after)
    return outs[n:]


def _exchange_copies(part_refs, slot_refs, send, recv):
    x, y, c = _here()
    cps = []
    for k in range(len(part_refs)):
        for m in (1, 2, 3):
            px, py = _chip_at(x, y, m)
            cps.append(pltpu.make_async_remote_copy(
                src_ref=part_refs[k].at[2 * px + py], dst_ref=slot_refs[k].at[m - 1],
                send_sem=send.at[3 * k + m - 1], recv_sem=recv.at[3 * k + m - 1],
                device_id=(px, py, c), device_id_type=MESH))
    return cps


def _exchange_start(parts):
    n = len(parts)
    lands = [lax.empty((N_CHIPS - 1,) + p.shape[1:], p.dtype) for p in parts]

    def body(*refs):
        ins, lz = refs[:n], refs[n:2 * n]
        send, recv = refs[2 * n], refs[2 * n + 1]
        token = refs[-1]
        for cp in _exchange_copies(ins, lz, send, recv):
            cp.start()
        token[...] = jnp.zeros_like(token)

    hbm = [pltpu.HBM(a.shape, a.dtype) for a in list(parts) + lands]
    outs = pl.pallas_call(
        body, name="exchange_start",
        in_specs=[HBM] * (2 * n),
        out_specs=[SEM, SEM] + [HBM] * (2 * n) + [pl.BlockSpec(memory_space=pltpu.VMEM)],
        out_shape=[pltpu.SemaphoreType.DMA((3 * n,)), pltpu.SemaphoreType.DMA((3 * n,))] + hbm
        + [jax.ShapeDtypeStruct((SUBLANES, LANES), F32)],
        input_output_aliases={k: 2 + k for k in range(2 * n)},
        compiler_params=pltpu.CompilerParams(has_side_effects=EFFECT),
    )(*[_in_hbm(a) for a in parts], *[_in_hbm(a) for a in lands])
    return outs[0], outs[1], outs[2:2 + n], outs[2 + n:2 + 2 * n], outs[-1]


def _exchange_wait(send, recv, parts, lands, after):
    n = len(parts)

    def body(*refs):
        ins, lz = refs[:n], refs[n:2 * n]
        send_r, recv_r = refs[2 * n], refs[2 * n + 1]
        for cp in _exchange_copies(ins, lz, send_r, recv_r):
            cp.wait_send()
            cp.wait_recv()

    hbm = [pltpu.HBM(a.shape, a.dtype) for a in list(parts) + list(lands)]
    outs = pl.pallas_call(
        body, name="exchange_wait",
        in_specs=[HBM] * (2 * n) + [SEM, SEM, ANY],
        out_specs=[HBM] * (2 * n), out_shape=hbm,
        input_output_aliases={k: k for k in range(2 * n)},
        compiler_params=pltpu.CompilerParams(has_side_effects=EFFECT),
    )(*parts, *lands, send, recv, after)
    return outs[:n], outs[n:]


def _swap_halves_out(grads, name):
    n = len(grads)
    out_shapes = [jax.ShapeDtypeStruct((g.shape[0], g.shape[1] // 2, g.shape[2]), g.dtype) for g in grads]

    def body(*refs):
        ins, outs = refs[:n], refs[n:2 * n]
        send, recv = refs[2 * n:]
        x, y, c = _here()
        cps = []
        for k in range(n):
            rh = ins[k].shape[1] // 2
            cp = pltpu.make_async_remote_copy(
                src_ref=ins[k].at[:, pl.ds((1 - c) * rh, rh), :], dst_ref=outs[k],
                send_sem=send.at[k], recv_sem=recv.at[k], device_id=(x, y, 1 - c), device_id_type=MESH)
            cp.start()
            cps.append(cp)
        for cp in cps:
            cp.wait()

    return pl.pallas_call(
        body, in_specs=[ANY] * n, out_specs=[ANY] * n, out_shape=out_shapes,
        scratch_shapes=[pltpu.SemaphoreType.DMA((n,)), pltpu.SemaphoreType.DMA((n,))],
        name=name)(*grads)


def _add_cast(g, other, cidx, name):
    ns, R, Cc = g.shape
    rh = R // 2
    tr = _tile(rh, max(16, (1 << 18) // Cc), 16)
    nb = rh // tr

    def body(c_ref, g_ref, o_ref, s_ref):
        s_ref[...] = (g_ref[...] + o_ref[...]).astype(BF16)

    return pl.pallas_call(
        body,
        grid_spec=pltpu.PrefetchScalarGridSpec(
            num_scalar_prefetch=1, grid=(ns, nb),
            in_specs=[pl.BlockSpec((None, tr, Cc), lambda k, i, c: (k, c[0] * nb + i, 0)),
                      pl.BlockSpec((None, tr, Cc), lambda k, i, c: (k, i, 0))],
            out_specs=pl.BlockSpec((None, tr, Cc), lambda k, i, c: (k, i, 0))),
        out_shape=jax.ShapeDtypeStruct((ns, rh, Cc), BF16),
        compiler_params=_params("parallel", "parallel"), name=name)(cidx, g, other)


def _exchange_chips(parts):
    n = len(parts)
    out_shapes = [jax.ShapeDtypeStruct((N_CHIPS - 1,) + p.shape[1:], p.dtype) for p in parts]

    def body(*refs):
        ins, outs = refs[:n], refs[n:2 * n]
        send, recv = refs[2 * n:]
        cps = _exchange_copies(ins, outs, send, recv)
        for cp in cps:
            cp.start()
        for cp in cps:
            cp.wait()

    return pl.pallas_call(
        body, in_specs=[ANY] * n, out_specs=[ANY] * n, out_shape=out_shapes,
        scratch_shapes=[pltpu.SemaphoreType.DMA((3 * n,)), pltpu.SemaphoreType.DMA((3 * n,))],
        name="exchange_chips")(*parts)


def _sum_slots(part, got, idx, name):
    ns, rh, Cc = got.shape
    tr = _tile(rh, max(16, (1 << 17) // Cc), 16)
    nb = rh // tr

    def body(i_ref, p_ref, b_ref, o_ref):
        acc = p_ref[...].astype(F32)
        for m in range(ns):
            acc = acc + b_ref[m].astype(F32)
        o_ref[...] = acc

    return pl.pallas_call(
        body,
        grid_spec=pltpu.PrefetchScalarGridSpec(
            num_scalar_prefetch=1, grid=(nb,),
            in_specs=[pl.BlockSpec((None, tr, Cc), lambda i, s: (s[1], i, 0)),
                      pl.BlockSpec((ns, tr, Cc), lambda i, s: (0, i, 0))],
            out_specs=pl.BlockSpec((tr, Cc), lambda i, s: (s[0] * nb + i, 0))),
        out_shape=jax.ShapeDtypeStruct((2 * rh, Cc), F32),
        compiler_params=_params("parallel"), name=name)(idx, part, got)


def _share_halves(blocks):
    n = len(blocks)

    def body(*refs):
        ins, outs = refs[:n], refs[n:2 * n]
        send, recv = refs[2 * n:]
        x, y, c = _here()
        cps = []
        for k in range(n):
            rh = outs[k].shape[0] // 2
            mine = outs[k].at[pl.ds(c * rh, rh), :]
            cp = pltpu.make_async_remote_copy(
                src_ref=mine, dst_ref=mine, send_sem=send.at[k], recv_sem=recv.at[k],
                device_id=(x, y, 1 - c), device_id_type=MESH)
            cp.start()
            cps.append(cp)
        for cp in cps:
            cp.wait()

    return pl.pallas_call(
        body, in_specs=[ANY] * n, out_specs=[ANY] * n,
        out_shape=[jax.ShapeDtypeStruct(b.shape, b.dtype) for b in blocks],
        input_output_aliases={k: k for k in range(n)},
        scratch_shapes=[pltpu.SemaphoreType.DMA((n,)), pltpu.SemaphoreType.DMA((n,))],
        name="share_halves")(*blocks)


def _gather_small(packed):
    R = packed.shape[0]

    def body(p_ref, o_ref, send, recv, lsem):
        x, y, c = _here()
        me = 4 * x + 2 * y + c
        mine = o_ref.at[me]
        lc = pltpu.make_async_copy(p_ref, mine, lsem)
        lc.start()
        cps = []
        for m in range(1, N_DEV):
            peer = (x ^ (m >> 2), y ^ ((m >> 1) & 1), c ^ (m & 1))
            cp = pltpu.make_async_remote_copy(
                src_ref=p_ref, dst_ref=mine, send_sem=send.at[m - 1], recv_sem=recv.at[m - 1],
                device_id=peer, device_id_type=MESH)
            cp.start()
            cps.append(cp)
        for m in range(1, N_DEV):
            theirs = o_ref.at[me ^ m]
            pltpu.make_async_remote_copy(
                src_ref=p_ref, dst_ref=theirs, send_sem=send.at[m - 1], recv_sem=recv.at[m - 1],
                device_id=(x, y, c), device_id_type=MESH).wait_recv()
        for cp in cps:
            cp.wait_send()
        lc.wait()

    return pl.pallas_call(
        body, in_specs=[ANY], out_specs=ANY,
        out_shape=jax.ShapeDtypeStruct((N_DEV, R, LANES), F32),
        scratch_shapes=[pltpu.SemaphoreType.DMA((N_DEV - 1,)), pltpu.SemaphoreType.DMA((N_DEV - 1,)),
                        pltpu.SemaphoreType.DMA],
        name="gather_small")(packed)


def _sum_devices(slots):
    n, R, _ = slots.shape
    tr = _tile(R, 256)

    def body(s_ref, o_ref):
        acc = s_ref[0]
        for d in range(1, n):
            acc = acc + s_ref[d]
        o_ref[...] = acc

    return pl.pallas_call(
        body, grid=(R // tr,),
        in_specs=[pl.BlockSpec((n, tr, LANES), lambda i: (0, i, 0))],
        out_specs=pl.BlockSpec((tr, LANES), lambda i: (i, 0)),
        out_shape=jax.ShapeDtypeStruct((R, LANES), F32),
        compiler_params=_params("parallel"), name="sum_devices")(slots)


def _pack(arrs):
    rows, parts = [], []
    for a in arrs:
        flat = a.reshape(-1)
        r = -(-flat.shape[0] // (SUBLANES * LANES)) * SUBLANES
        parts.append(jnp.pad(flat, (0, r * LANES - flat.shape[0])).reshape(r, LANES))
        rows.append(r)
    return jnp.concatenate(parts, axis=0), rows


def _unpack(packed, rows, shapes):
    out, r0 = [], 0
    for r, shp in zip(rows, shapes):
        size = math.prod(shp)
        out.append(packed[r0:r0 + r].reshape(-1)[:size].reshape(shp))
        r0 += r
    return out


def _block_diag(w, per):
    H, dh, _ = w.shape
    w4 = w.reshape(H // per, per, dh, dh)
    eye = jnp.eye(per, dtype=w.dtype)
    return (w4[:, :, :, None, :] * eye[None, :, None, :, None]).reshape(H // per, per * dh, per * dh)


def _block_diag_take(d, per):
    n, s, _ = d.shape
    dh = s // per
    d5 = d.reshape(n, per, dh, per, dh)
    return jnp.stack([d5[:, h, :, h, :] for h in range(per)], axis=1).reshape(n * per, dh, dh)


def kernel(x, ffn1_norm, ffn1_w_gate, ffn1_w_up, ffn1_w_down, mix_norm, w_in, conv_dw, conv_dw_bias, conv_ln_g, conv_ln_b, lru_conv_w, lru_conv_b, lru_w_a, lru_b_a, lru_w_x, lru_b_x, lru_lambda, w_out, ffn2_norm, ffn2_w_gate, ffn2_w_up, ffn2_w_down, final_norm, loss_target, m_ffn1_norm, m_ffn1_w_gate, m_ffn1_w_up, m_ffn1_w_down, m_mix_norm, m_w_in, m_conv_dw, m_conv_dw_bias, m_conv_ln_g, m_conv_ln_b, m_lru_conv_w, m_lru_conv_b, m_lru_w_a, m_lru_b_a, m_lru_w_x, m_lru_b_x, m_lru_lambda, m_w_out, m_ffn2_norm, m_ffn2_w_gate, m_ffn2_w_up, m_ffn2_w_down, m_final_norm, v_ffn1_norm, v_ffn1_w_gate, v_ffn1_w_up, v_ffn1_w_down, v_mix_norm, v_w_in, v_conv_dw, v_conv_dw_bias, v_conv_ln_g, v_conv_ln_b, v_lru_conv_w, v_lru_conv_b, v_lru_w_a, v_lru_b_a, v_lru_w_x, v_lru_b_x, v_lru_lambda, v_w_out, v_ffn2_norm, v_ffn2_w_gate, v_ffn2_w_up, v_ffn2_w_down, v_final_norm):
    names = ['ffn1_norm', 'ffn1_w_gate', 'ffn1_w_up', 'ffn1_w_down', 'mix_norm', 'w_in', 'conv_dw', 'conv_dw_bias',
             'conv_ln_g', 'conv_ln_b', 'lru_conv_w', 'lru_conv_b', 'lru_w_a', 'lru_b_a', 'lru_w_x', 'lru_b_x',
             'lru_lambda', 'w_out', 'ffn2_norm', 'ffn2_w_gate', 'ffn2_w_up', 'ffn2_w_down', 'final_norm']
    env = dict(locals())
    W = {n: env[n] for n in names}
    M = {n: env['m_' + n] for n in names}
    V = {n: env['v_' + n] for n in names}

    xi, yi, ci = _here()
    chip = 2 * xi + yi
    cidx = ci.astype(jnp.int32).reshape(1)
    T, D = x.shape[-2], x.shape[-1]
    xs = x.reshape(T, D)
    tgt = loss_target.reshape(T, D)
    K, Cs = conv_dw.shape
    C = conv_dw_bias.shape[0]
    Wl = lru_conv_b.shape[0]
    K4 = lru_conv_w.shape[0]
    heads, dh, _ = lru_w_a.shape
    per = LANES // dh

    def row(v):
        return v.reshape(1, -1)

    tform = ('ffn1_w_gate', 'ffn1_w_up', 'ffn2_w_gate', 'ffn2_w_up')
    for n in tform:
        W[n], M[n], V[n] = W[n].T, M[n].T, V[n].T
    ff1 = jnp.stack([W['ffn1_w_gate'], W['ffn1_w_up'], ffn1_w_down]).astype(BF16)
    ff2 = jnp.stack([W['ffn2_w_gate'], W['ffn2_w_up'], ffn2_w_down]).astype(BF16)
    kp = -(-K // SUBLANES) * SUBLANES
    taps = jnp.concatenate([conv_dw, jnp.zeros((kp - K, Cs), F32), lru_conv_w,
                            jnp.zeros((2 * SUBLANES - K4, Cs), F32)], axis=0)
    (wff1,) = _gather_weights([ff1])
    mixw = [w_in.astype(BF16), w_out.astype(BF16), taps]
    msend, mrecv, mixw, mlands, mtok = _gather_start(mixw, _place_own(mixw, "place_mix"), wff1, "gather_mix_start")
    fsend, frecv, ff2s, flands, ftok = _gather_start([ff2], _place_own([ff2], "place_ffn2"), mtok, "gather_ffn2_start")
    wa_bd = _block_diag(lru_w_a, per).astype(BF16)
    wx_bd = _block_diag(lru_w_x, per).astype(BF16)

    x1, a1, b1 = _ffn_fwd(xs, row(ffn1_norm) + ftok[0:1, 0:1], wff1, "ffn1_fwd")
    win, wout, taps = _gather_wait(msend, mrecv, mixw, mlands, x1, "gather_mix_wait")
    wout = wout.reshape(-1, D)
    conv_w_full = taps[:, :K].transpose(1, 0, 2).reshape(K, N_CHIPS * Cs)
    lru_w4_full = taps[:, kp:kp + K4].transpose(1, 0, 2).reshape(K4, N_CHIPS * Cs)
    z = _mix_in_fwd(x1, row(mix_norm), win)
    u, u1 = _conv_fwd(z, conv_w_full, row(conv_dw_bias), row(conv_ln_g), row(conv_ln_b))
    yr, hs = _lru_fwd(z, 2 * C, lru_w4_full, row(lru_conv_b), wa_bd, row(lru_b_a), wx_bd, row(lru_b_x),
                      row(lru_lambda))
    x2 = _mix_out_fwd(x1, u, yr, wout)
    (wff2,) = _gather_wait(fsend, frecv, ff2s, flands, x2, "gather_ffn2_wait")
    x3, a2, b2 = _ffn_fwd(x2, row(ffn2_norm), wff2, "ffn2_fwd")
    dx3, loss_blk, d_final = _final_loss(x3, row(final_norm), tgt)

    dx2, da2, db2, p2, hb2, dyh2, d_ffn2n = _ffn_bwd_tok(dx3, x2, row(ffn2_norm), a2, b2, wff2, "ffn2_bwd")
    dwg2, dwu2, dwd2 = _ffn_wgrad(hb2, dyh2, da2, db2, p2, "ffn2_wgrad")
    dcat, dwout = _mix_out_bwd(dx2, u, yr, wout)
    dzc, cst = _conv_bwd(dcat, u1, z, conv_w_full, row(conv_ln_g), row(conv_ln_b))
    dzx, dzg, lst, dwa_bd, dwx_bd = _lru_bwd(dcat, C, hs, z, 2 * C, lru_w4_full, row(lru_conv_b), wa_bd,
                                              row(lru_b_a), wx_bd, row(lru_b_x), row(lru_lambda))
    dx1, dwin, d_mixn = _mix_in_bwd(dzc, dzx, dzg, x1, dx2, row(mix_norm), win)

    idx = jnp.stack([ci, chip]).astype(jnp.int32)
    early_names = ['w_in', 'w_out', 'ffn2_w_gate', 'ffn2_w_up', 'ffn2_w_down']
    early = [dwin, dwout.reshape(N_CHIPS, -1, D), dwg2, dwu2, dwd2]
    e_parts = [_add_cast(g, o, cidx, "add_cast_" + n)
               for g, o, n in zip(early, _swap_halves_out(early, "swap_halves_early"), early_names)]
    esend, erecv, e_parts, e_lands, etok = _exchange_start(e_parts)

    dx0, da1, db1, p1, hb1, dyh1, d_ffn1n = _ffn_bwd_tok(dx1, xs, row(ffn1_norm) + etok[0:1, 0:1], a1, b1, wff1,
                                                         "ffn1_bwd")
    dwg1, dwu1, dwd1 = _ffn_wgrad(hb1, dyh1, da1, db1, p1, "ffn1_wgrad")

    last_names = ['ffn1_w_gate', 'ffn1_w_up', 'ffn1_w_down']
    last = [dwg1, dwu1, dwd1]
    l_parts = [_add_cast(g, o, cidx, "add_cast_" + n)
               for g, o, n in zip(last, _swap_halves_out(last, "swap_halves_last"), last_names)]
    l_slots = _exchange_chips(l_parts)
    e_parts, e_slots = _exchange_wait(esend, erecv, e_parts, e_lands, dwd1)
    big_names = early_names + last_names
    halves = [_sum_slots(p, b, idx, "sum_slots_" + n)
              for p, b, n in zip(list(e_parts) + l_parts, list(e_slots) + list(l_slots), big_names)]
    G = dict(zip(big_names, _share_halves(halves)))

    small_names = ['ffn1_norm', 'mix_norm', 'conv_dw', 'conv_dw_bias', 'conv_ln_g', 'conv_ln_b', 'lru_conv_w',
                   'lru_conv_b', 'lru_w_a', 'lru_b_a', 'lru_w_x', 'lru_b_x', 'lru_lambda', 'ffn2_norm',
                   'final_norm']
    small = {
        'ffn1_norm': d_ffn1n, 'mix_norm': d_mixn, 'conv_dw': cst[:K], 'conv_dw_bias': cst[K + 1],
        'conv_ln_g': cst[K + 2], 'conv_ln_b': cst[K + 3], 'lru_conv_w': lst[:K4], 'lru_conv_b': lst[K4],
        'lru_w_a': _block_diag_take(dwa_bd, per), 'lru_b_a': lst[K4 + 1],
        'lru_w_x': _block_diag_take(dwx_bd, per), 'lru_b_x': lst[K4 + 2], 'lru_lambda': lst[K4 + 3],
        'ffn2_norm': d_ffn2n, 'final_norm': d_final,
    }
    full_shapes = [(K, C) if n == 'conv_dw' else (K4, Wl) if n == 'lru_conv_w' else W[n].shape for n in small_names]
    packed, rows = _pack([small[n] for n in small_names])
    summed = _sum_devices(_gather_small(packed))
    for n, gsum in zip(small_names, _unpack(summed, rows, full_shapes)):
        if n == 'conv_dw':
            gsum = lax.dynamic_slice_in_dim(gsum, chip * Cs, Cs, axis=1)
        elif n == 'lru_conv_w':
            gsum = lax.dynamic_slice_in_dim(gsum, chip * lru_conv_w.shape[1], lru_conv_w.shape[1], axis=1)
        G[n] = gsum

    delta, new_m, new_v = {}, {}, {}
    for n in big_names:
        shp = W[n].shape
        g2 = G[n] if G[n].shape == shp else G[n].reshape(shp)
        G[n] = g2
        delta[n], new_m[n], new_v[n] = _adamw(W[n], g2, M[n], V[n], "adamw_" + n)
    pw, prow = _pack([W[n] for n in small_names])
    pg, _ = _pack([G[n] for n in small_names])
    pm, _ = _pack([M[n] for n in small_names])
    pv, _ = _pack([V[n] for n in small_names])
    sd, sm, sv = _adamw(pw, pg, pm, pv, "adamw_small")
    shapes = [W[n].shape for n in small_names]
    for n, a, b, c_ in zip(small_names, _unpack(sd, prow, shapes), _unpack(sm, prow, shapes),
                           _unpack(sv, prow, shapes)):
        delta[n], new_m[n], new_v[n] = a, b, c_

    loss = lax.psum(loss_blk[0, 0], ("x", "y", "c"))
    grad_x = dx0.reshape(x.shape)
    for n in tform:
        G[n], delta[n], new_m[n], new_v[n] = G[n].T, delta[n].T, new_m[n].T, new_v[n].T
    return (loss, grad_x, *[G[n] for n in names], *[delta[n] for n in names],
            *[new_m[n] for n in names], *[new_v[n] for n in names])
```

```python
import functools
import math

import jax
import jax.numpy as jnp
from jax import lax
from jax.experimental import pallas as pl
from jax.experimental.pallas import tpu as pltpu

F32 = jnp.float32
BF16 = jnp.bfloat16
MESH = pl.DeviceIdType.MESH

RMS_EPS = 1e-6
LN_EPS = 1e-5
LRU_C = 8.0
FFN_RES_SCALE = 0.5
ADAM_LR = 0.001
ADAM_B1 = 0.9
ADAM_B2 = 0.999
ADAM_EPS = 1e-08
ADAM_WD = 0.01
ADAM_STEP = 10

LANES = 128
SUBLANES = 8
CONV_HALO = 32
LRU_HALO = 8
ROW_CHUNK = 64
VMEM_LIMIT = 56 * 1024 * 1024
N_CHIPS = 4
N_DEV = 8
TOK_TILE = 1024
BWD_TILE = 512
CONV_TILE = 512
LRU_TILE = 1024


def _dot(a, b):
    return jnp.dot(a, b, preferred_element_type=F32)


def _dot_nt(a, b):
    return lax.dot_general(a, b, (((1,), (1,)), ((), ())), preferred_element_type=F32)


def _dot_tn(a, b):
    return lax.dot_general(a, b, (((0,), (0,)), ((), ())), preferred_element_type=F32)


def _tile(n, pref, mult=SUBLANES):
    for t in range(min(pref, n), 0, -1):
        if n % t == 0 and t % mult == 0:
            return t
    return n


def _params(*sem):
    return pltpu.CompilerParams(dimension_semantics=sem, vmem_limit_bytes=VMEM_LIMIT)


def _rms_stats(x):
    r = lax.rsqrt(jnp.mean(x * x, axis=-1, keepdims=True) + RMS_EPS)
    return x * r, r


def _rms_bwd(dh, xh, r, g):
    dxh = dh * g
    return r * (dxh - xh * jnp.mean(dxh * xh, axis=-1, keepdims=True))


def _colsum(v):
    return jnp.sum(v, axis=0, keepdims=True)


def _ffn_fwd(x, g, wff, name):
    T, D = x.shape
    ns, fs = wff.shape[1], wff.shape[2]
    tm = _tile(T, TOK_TILE)

    def body(x_ref, g_ref, wg_ref, wu_ref, wd_ref, y_ref, a_ref, b_ref, hb_ref, acc_ref):
        j = pl.program_id(1)

        @pl.when(j == 0)
        def _():
            xh, _ = _rms_stats(x_ref[...])
            hb_ref[...] = (xh * g_ref[...]).astype(BF16)
            acc_ref[...] = jnp.zeros_like(acc_ref)

        hb = hb_ref[...]
        a = _dot_nt(hb, wg_ref[...])
        b = _dot_nt(hb, wu_ref[...])
        a_ref[...] = a.astype(BF16)
        b_ref[...] = b.astype(BF16)
        p = (a * jax.nn.sigmoid(a) * b).astype(BF16)
        acc_ref[...] += _dot(p, wd_ref[...])

        @pl.when(j == ns - 1)
        def _():
            y_ref[...] = x_ref[...] + FFN_RES_SCALE * acc_ref[...]

    def wspec(n):
        return pl.BlockSpec((None, None, fs, D), lambda i, j: (n, j, 0, 0))

    mid = pl.BlockSpec((None, tm, fs), lambda i, j: (j, i, 0))
    return pl.pallas_call(
        body, grid=(T // tm, ns),
        in_specs=[pl.BlockSpec((tm, D), lambda i, j: (i, 0)), pl.BlockSpec((1, D), lambda i, j: (0, 0)),
                  wspec(0), wspec(1), wspec(2)],
        out_specs=[pl.BlockSpec((tm, D), lambda i, j: (i, 0)), mid, mid],
        out_shape=[jax.ShapeDtypeStruct((T, D), F32), jax.ShapeDtypeStruct((ns, T, fs), BF16),
                   jax.ShapeDtypeStruct((ns, T, fs), BF16)],
        scratch_shapes=[pltpu.VMEM((tm, D), BF16), pltpu.VMEM((tm, D), F32)],
        compiler_params=_params("parallel", "arbitrary"), name=name)(x, g, wff, wff, wff)


def _ffn_bwd_tok(dy, x, g, a, b, wff, name):
    T, D = x.shape
    ns, fs = wff.shape[1], wff.shape[2]
    tm = _tile(T, BWD_TILE)

    def body(dy_ref, x_ref, g_ref, a_ref, b_ref, wg_ref, wu_ref, wd_ref,
             dx_ref, da_ref, db_ref, p_ref, hb_ref, dyh_ref, dg_ref, dh_ref):
        i, j = pl.program_id(0), pl.program_id(1)

        @pl.when((i == 0) & (j == 0))
        def _():
            dg_ref[...] = jnp.zeros_like(dg_ref)

        @pl.when(j == 0)
        def _():
            xh, _ = _rms_stats(x_ref[...])
            hb_ref[...] = (xh * g_ref[...]).astype(BF16)
            dyh_ref[...] = (FFN_RES_SCALE * dy_ref[...]).astype(BF16)
            dh_ref[...] = jnp.zeros_like(dh_ref)

        av = a_ref[...].astype(F32)
        bv = b_ref[...].astype(F32)
        dp = _dot_nt(dyh_ref[...], wd_ref[...])
        s = jax.nn.sigmoid(av)
        sl = av * s
        da = (dp * bv * (s * (1.0 + av * (1.0 - s)))).astype(BF16)
        db = (dp * sl).astype(BF16)
        da_ref[...] = da
        db_ref[...] = db
        p_ref[...] = (sl * bv).astype(BF16)
        dh_ref[...] += _dot(da, wg_ref[...]) + _dot(db, wu_ref[...])

        @pl.when(j == ns - 1)
        def _():
            xh, r = _rms_stats(x_ref[...])
            dh = dh_ref[...]
            gv = g_ref[...]
            dx_ref[...] = dy_ref[...] + _rms_bwd(dh, xh, r, gv)
            dg_ref[...] += _colsum(dh * xh)

    def wspec(n):
        return pl.BlockSpec((None, None, fs, D), lambda i, j: (n, j, 0, 0))

    tok = pl.BlockSpec((tm, D), lambda i, j: (i, 0))
    mid = pl.BlockSpec((None, tm, fs), lambda i, j: (j, i, 0))
    vec = pl.BlockSpec((1, D), lambda i, j: (0, 0))
    return pl.pallas_call(
        body, grid=(T // tm, ns),
        in_specs=[tok, tok, vec, mid, mid, wspec(0), wspec(1), wspec(2)],
        out_specs=[tok, mid, mid, mid, tok, tok, vec],
        out_shape=[jax.ShapeDtypeStruct((T, D), F32),
                   jax.ShapeDtypeStruct((ns, T, fs), BF16), jax.ShapeDtypeStruct((ns, T, fs), BF16),
                   jax.ShapeDtypeStruct((ns, T, fs), BF16),
                   jax.ShapeDtypeStruct((T, D), BF16), jax.ShapeDtypeStruct((T, D), BF16),
                   jax.ShapeDtypeStruct((1, D), F32)],
        scratch_shapes=[pltpu.VMEM((tm, D), F32)],
        compiler_params=_params("arbitrary", "arbitrary"), name=name)(dy, x, g, a, b, wff, wff, wff)


def _ffn_wgrad(hb, dyh, da, db, p, after, name):
    T, D = hb.shape
    ns, _, fs = da.shape
    tm = _tile(T, TOK_TILE)

    def body(hb_ref, dyh_ref, da_ref, db_ref, p_ref, after_ref, dwg_ref, dwu_ref, dwd_ref):
        @pl.when(pl.program_id(1) == 0)
        def _():
            dwg_ref[...] = jnp.zeros_like(dwg_ref)
            dwu_ref[...] = jnp.zeros_like(dwu_ref)
            dwd_ref[...] = jnp.zeros_like(dwd_ref)

        hbv = hb_ref[...]
        dwg_ref[...] += _dot_tn(da_ref[...], hbv)
        dwu_ref[...] += _dot_tn(db_ref[...], hbv)
        dwd_ref[...] += _dot_tn(p_ref[...], dyh_ref[...])

    tok = pl.BlockSpec((tm, D), lambda j, i: (i, 0))
    mid = pl.BlockSpec((None, tm, fs), lambda j, i: (j, i, 0))
    wsp = pl.BlockSpec((None, fs, D), lambda j, i: (j, 0, 0))
    sds = jax.ShapeDtypeStruct((ns, fs, D), F32)
    return pl.pallas_call(
        body, grid=(ns, T // tm),
        in_specs=[tok, tok, mid, mid, mid, pl.BlockSpec((SUBLANES, LANES), lambda j, i: (0, 0))],
        out_specs=[wsp, wsp, wsp], out_shape=[sds, sds, sds],
        compiler_params=_params("parallel", "arbitrary"), name=name)(hb, dyh, da, db, p, after)


def _mix_in_fwd(x, g, win):
    T, D = x.shape
    ns, ws = win.shape[0], win.shape[2]
    tm = _tile(T, TOK_TILE)

    def body(x_ref, g_ref, w_ref, z_ref):
        xh, _ = _rms_stats(x_ref[...])
        hb = (xh * g_ref[...]).astype(BF16)
        for j in range(ns):
            z_ref[:, pl.ds(j * ws, ws)] = _dot(hb, w_ref[j])

    return pl.pallas_call(
        body, grid=(T // tm,),
        in_specs=[pl.BlockSpec((tm, D), lambda i: (i, 0)), pl.BlockSpec((1, D), lambda i: (0, 0)),
                  pl.BlockSpec((ns, D, ws), lambda i: (0, 0, 0), pipeline_mode=pl.Buffered(1))],
        out_specs=pl.BlockSpec((tm, ns * ws), lambda i: (i, 0)),
        out_shape=jax.ShapeDtypeStruct((T, ns * ws), F32),
        compiler_params=_params("parallel"), name="mix_in_fwd")(x, g, win)


def _tap_sum(buf, w_ref, ntaps, first_row, r0, rows, flip):
    acc = None
    for k in range(ntaps):
        off = (ntaps - 1 - k) if flip else k
        t = buf[pl.ds(first_row + r0 + off, rows), :] * w_ref[pl.ds(k, 1), :]
        acc = t if acc is None else acc + t
    return acc


def _shift_copies(buf, sh, rows):
    for r in range(1, SUBLANES):
        sh[r - 1, pl.ds(0, rows), :] = buf[pl.ds(r, rows), :]


def _tap_rows(buf, sh, off, r0, rows):
    r = off % SUBLANES
    if r == 0:
        return buf[pl.ds(off + r0, rows), :]
    return sh[r - 1, pl.ds(off - r + r0, rows), :]


def _tap_sum_tiles(buf, sh, w_ref, ntaps, first_row, r0, rows, flip):
    acc = None
    for k in range(ntaps):
        off = first_row + ((ntaps - 1 - k) if flip else k)
        t = _tap_rows(buf, sh, off, r0, rows) * w_ref[pl.ds(k, 1), :]
        acc = t if acc is None else acc + t
    return acc


def _conv_fwd(z, w, bias, lng, lnb):
    T = z.shape[0]
    K, C = w.shape
    tm = _tile(T, CONV_TILE, ROW_CHUNK)
    rc = min(ROW_CHUNK, tm)
    srows = tm + CONV_HALO - SUBLANES

    def body(cv_ref, cg_ref, w_ref, b_ref, g_ref, bb_ref, u_ref, u1_ref, buf, sh):
        @pl.when(pl.program_id(0) == 0)
        def _():
            buf[pl.ds(0, CONV_HALO), :] = jnp.zeros((CONV_HALO, C), F32)

        buf[pl.ds(CONV_HALO, tm), :] = cv_ref[...] * jax.nn.sigmoid(cg_ref[...])
        _shift_copies(buf, sh, srows)
        for r0 in range(0, tm, rc):
            u1 = _tap_sum_tiles(buf, sh, w_ref, K, CONV_HALO - (K - 1), r0, rc, False) + b_ref[...]
            u1_ref[pl.ds(r0, rc), :] = u1
            xc = u1 - jnp.mean(u1, axis=-1, keepdims=True)
            xh = xc * lax.rsqrt(jnp.mean(xc * xc, axis=-1, keepdims=True) + LN_EPS)
            u2 = xh * g_ref[...] + bb_ref[...]
            u_ref[pl.ds(r0, rc), :] = (u2 * jax.nn.sigmoid(u2)).astype(BF16)
        buf[pl.ds(0, CONV_HALO), :] = buf[pl.ds(tm, CONV_HALO), :]

    vec = pl.BlockSpec((1, C), lambda i: (0, 0))
    return pl.pallas_call(
        body, grid=(T // tm,),
        in_specs=[pl.BlockSpec((tm, C), lambda i: (i, 0)), pl.BlockSpec((tm, C), lambda i: (i, 1)),
                  pl.BlockSpec((K, C), lambda i: (0, 0)), vec, vec, vec],
        out_specs=[pl.BlockSpec((tm, C), lambda i: (i, 0)), pl.BlockSpec((tm, C), lambda i: (i, 0))],
        out_shape=[jax.ShapeDtypeStruct((T, C), BF16), jax.ShapeDtypeStruct((T, C), F32)],
        scratch_shapes=[pltpu.VMEM((CONV_HALO + tm, C), F32), pltpu.VMEM((SUBLANES - 1, srows, C), F32)],
        compiler_params=_params("arbitrary"), name="conv_fwd")(z, z, w, bias, lng, lnb)


def _conv_bwd(dcat, u1, z, w, lng, lnb):
    T = z.shape[0]
    K, C = w.shape
    tm = _tile(T, CONV_TILE, ROW_CHUNK)
    rc = min(ROW_CHUNK, tm)
    nI = T // tm
    hb = tm // CONV_HALO
    srows = ((K + 4 + SUBLANES - 1) // SUBLANES) * SUBLANES
    shrows = tm + CONV_HALO - SUBLANES

    def body(du_ref, u1_ref, cv_ref, cg_ref, cvp_ref, cgp_ref, w_ref, g_ref, bb_ref,
             dz_ref, st_ref, u0buf, d1buf, ush, dsh):
        i = pl.program_id(0)
        ti = nI - 1 - i

        @pl.when(i == 0)
        def _():
            st_ref[...] = jnp.zeros_like(st_ref)
            d1buf[pl.ds(tm, CONV_HALO), :] = jnp.zeros((CONV_HALO, C), F32)

        prev = cvp_ref[...] * jax.nn.sigmoid(cgp_ref[...])
        u0buf[pl.ds(0, CONV_HALO), :] = jnp.where(ti == 0, 0.0, prev)
        u0buf[pl.ds(CONV_HALO, tm), :] = cv_ref[...] * jax.nn.sigmoid(cg_ref[...])

        gv = g_ref[...]
        dbias = jnp.zeros((1, C), F32)
        dgain = jnp.zeros((1, C), F32)
        dlnb = jnp.zeros((1, C), F32)
        for r0 in range(0, tm, rc):
            u1 = u1_ref[pl.ds(r0, rc), :]
            xc = u1 - jnp.mean(u1, axis=-1, keepdims=True)
            rstd = lax.rsqrt(jnp.mean(xc * xc, axis=-1, keepdims=True) + LN_EPS)
            xh = xc * rstd
            u2 = xh * gv + bb_ref[...]
            s = jax.nn.sigmoid(u2)
            du2 = du_ref[pl.ds(r0, rc), :] * (s * (1.0 + u2 * (1.0 - s)))
            dgain = dgain + _colsum(du2 * xh)
            dlnb = dlnb + _colsum(du2)
            dxh = du2 * gv
            du1 = rstd * (dxh - jnp.mean(dxh, axis=-1, keepdims=True)
                          - xh * jnp.mean(dxh * xh, axis=-1, keepdims=True))
            dbias = dbias + _colsum(du1)
            d1buf[pl.ds(r0, rc), :] = du1
        st_ref[pl.ds(K + 1, 1), :] += dbias
        st_ref[pl.ds(K + 2, 1), :] += dgain
        st_ref[pl.ds(K + 3, 1), :] += dlnb

        _shift_copies(u0buf, ush, shrows)
        _shift_copies(d1buf, dsh, shrows)
        for k in range(K):
            acc = jnp.zeros((SUBLANES, C), F32)
            for r0 in range(0, tm, rc):
                prod = d1buf[pl.ds(r0, rc), :] * _tap_rows(u0buf, ush, CONV_HALO - (K - 1) + k, r0, rc)
                acc = acc + jnp.sum(prod.reshape(rc // SUBLANES, SUBLANES, C), axis=0)
            st_ref[pl.ds(k, 1), :] += _colsum(acc)

        for r0 in range(0, tm, rc):
            du0 = _tap_sum_tiles(d1buf, dsh, w_ref, K, 0, r0, rc, True)
            cv = cv_ref[pl.ds(r0, rc), :]
            sg = jax.nn.sigmoid(cg_ref[pl.ds(r0, rc), :])
            dz_ref[pl.ds(r0, rc), pl.ds(0, C)] = (du0 * sg).astype(BF16)
            dz_ref[pl.ds(r0, rc), pl.ds(C, C)] = (du0 * cv * sg * (1.0 - sg)).astype(BF16)
        d1buf[pl.ds(tm, CONV_HALO), :] = d1buf[pl.ds(0, CONV_HALO), :]

    def rev(col):
        return lambda i: (nI - 1 - i, col)

    def rev_prev(col):
        return lambda i: (jnp.maximum((nI - 1 - i) * hb - 1, 0), col)

    vec = pl.BlockSpec((1, C), lambda i: (0, 0))
    return pl.pallas_call(
        body, grid=(nI,),
        in_specs=[pl.BlockSpec((tm, C), rev(0)), pl.BlockSpec((tm, C), rev(0)),
                  pl.BlockSpec((tm, C), rev(0)), pl.BlockSpec((tm, C), rev(1)),
                  pl.BlockSpec((CONV_HALO, C), rev_prev(0)), pl.BlockSpec((CONV_HALO, C), rev_prev(1)),
                  pl.BlockSpec((K, C), lambda i: (0, 0)), vec, vec],
        out_specs=[pl.BlockSpec((tm, 2 * C), rev(0)), pl.BlockSpec((srows, C), lambda i: (0, 0))],
        out_shape=[jax.ShapeDtypeStruct((T, 2 * C), BF16), jax.ShapeDtypeStruct((srows, C), F32)],
        scratch_shapes=[pltpu.VMEM((CONV_HALO + tm, C), F32), pltpu.VMEM((tm + CONV_HALO, C), F32),
                        pltpu.VMEM((SUBLANES - 1, shrows, C), F32), pltpu.VMEM((SUBLANES - 1, shrows, C), F32)],
        compiler_params=_params("arbitrary"), name="conv_bwd")(dcat, u1, z, z, z, z, w, lng, lnb)


def _softplus(v):
    return jnp.maximum(v, 0.0) + jnp.log(1.0 + jnp.exp(-jnp.abs(v)))


def _gelu(v):
    c = math.sqrt(2.0 / math.pi)
    t = jnp.tanh(c * (v + 0.044715 * v * v * v))
    gl = 0.5 * v * (1.0 + t)
    dgl = 0.5 * (1.0 + t) + 0.5 * v * (1.0 - t * t) * c * (1.0 + 3.0 * 0.044715 * v * v)
    return gl, dgl


def _lru_gates(xr, wa, ba, wx, bx, lam):
    xb = xr.astype(BF16)
    r = jax.nn.sigmoid(_dot(xb, wa) + ba)
    ig = jax.nn.sigmoid(_dot(xb, wx) + bx)
    sp = _softplus(-lam)
    log_a = -LRU_C * r * sp
    a = jnp.exp(log_a)
    y = 2.0 * log_a
    series = -(y * (1.0 + y * (0.5 + y * (1.0 / 6.0 + y * (1.0 / 24.0)))))
    mult = jnp.sqrt(jnp.where(y > -0.02, series, 1.0 - jnp.exp(y)))
    return a, mult, r, ig, sp


def _scan_tile(a_s, b_s, carry, seg, reverse):
    def step(n, hp):
        hl, pr = hp
        k = (seg - 1 - n) if reverse else n
        rows = pl.ds(k, SUBLANES, stride=seg)
        av = a_s[rows, :]
        hl = av * hl + b_s[rows, :]
        pr = av * pr
        b_s[rows, :] = hl
        a_s[rows, :] = pr
        return hl, pr

    hl, pr = lax.fori_loop(0, seg, step, (jnp.zeros((SUBLANES, LANES), F32), jnp.ones((SUBLANES, LANES), F32)),
                           unroll=min(8, seg))
    cs = [None] * SUBLANES
    c = carry
    for s in (range(SUBLANES - 1, -1, -1) if reverse else range(SUBLANES)):
        cs[s] = c
        c = hl[s:s + 1, :] + pr[s:s + 1, :] * c
    return cs, c


def _lru_fwd(z, col0, w4, b4, wa, ba, wx, bx, lam):
    T = z.shape[0]
    K4, W = w4.shape
    nC = W // LANES
    tm = _tile(T, LRU_TILE, SUBLANES * SUBLANES)
    seg = tm // SUBLANES
    cx, cg = col0 // LANES, (col0 + W) // LANES

    def body(rx_ref, rg_ref, w4_ref, b4_ref, wa_ref, ba_ref, wx_ref, bx_ref, lam_ref,
             yr_ref, hs_ref, xbuf, a_s, b_s, hc):
        @pl.when(pl.program_id(1) == 0)
        def _():
            xbuf[pl.ds(0, LRU_HALO), :] = jnp.zeros((LRU_HALO, LANES), F32)
            hc[...] = jnp.zeros_like(hc)

        xbuf[pl.ds(LRU_HALO, tm), :] = rx_ref[...]
        xr = _tap_sum(xbuf, w4_ref, K4, LRU_HALO - (K4 - 1), 0, tm, False) + b4_ref[...]
        a, mult, _, ig, _ = _lru_gates(xr, wa_ref[...], ba_ref[...], wx_ref[...], bx_ref[...], lam_ref[...])
        a_s[...] = a
        b_s[...] = mult * ig * xr
        cs, cout = _scan_tile(a_s, b_s, hc[pl.ds(0, 1), :], seg, False)
        hc[pl.ds(0, 1), :] = cout
        for s in range(SUBLANES):
            rows = pl.ds(s * seg, seg)
            h = b_s[rows, :] + a_s[rows, :] * cs[s]
            hs_ref[rows, :] = h
            gl, _ = _gelu(rg_ref[rows, :])
            yr_ref[rows, :] = (h * gl).astype(BF16)
        xbuf[pl.ds(0, LRU_HALO), :] = xbuf[pl.ds(tm, LRU_HALO), :]

    vec = pl.BlockSpec((1, LANES), lambda c, i: (0, c))
    mat = pl.BlockSpec((None, LANES, LANES), lambda c, i: (c, 0, 0))
    return pl.pallas_call(
        body, grid=(nC, T // tm),
        in_specs=[pl.BlockSpec((tm, LANES), lambda c, i: (i, cx + c)),
                  pl.BlockSpec((tm, LANES), lambda c, i: (i, cg + c)),
                  pl.BlockSpec((K4, LANES), lambda c, i: (0, c)), vec, mat, vec, mat, vec, vec],
        out_specs=[pl.BlockSpec((tm, LANES), lambda c, i: (i, c)), pl.BlockSpec((tm, LANES), lambda c, i: (i, c))],
        out_shape=[jax.ShapeDtypeStruct((T, W), BF16), jax.ShapeDtypeStruct((T, W), F32)],
        scratch_shapes=[pltpu.VMEM((LRU_HALO + tm, LANES), F32), pltpu.VMEM((tm, LANES), F32),
                        pltpu.VMEM((tm, LANES), F32), pltpu.VMEM((SUBLANES, LANES), F32)],
        compiler_params=_params("parallel", "arbitrary"), name="lru_fwd")(z, z, w4, b4, wa, ba, wx, bx, lam)


def _lru_bwd(dcat, dcol0, hs, z, col0, w4, b4, wa, ba, wx, bx, lam):
    T = z.shape[0]
    K4, W = w4.shape
    assert K4 + 4 == SUBLANES
    nC = W // LANES
    tm = _tile(T, LRU_TILE, SUBLANES * SUBLANES)
    seg = tm // SUBLANES
    nI = T // tm
    hb = tm // LRU_HALO
    cx, cg, cd = col0 // LANES, (col0 + W) // LANES, dcol0 // LANES

    def body(dyr_ref, hs_ref, hsp_ref, rx_ref, rxp_ref, rg_ref, w4_ref, b4_ref, wa_ref, ba_ref, wx_ref, bx_ref,
             lam_ref, dzx_ref, dzg_ref, st_ref, dwa_ref, dwx_ref, xbuf, hbuf, abuf, a_s, b_s, dbuf, gc, anc):
        i = pl.program_id(1)
        ti = nI - 1 - i

        @pl.when(i == 0)
        def _():
            st_ref[...] = jnp.zeros_like(st_ref)
            dwa_ref[...] = jnp.zeros_like(dwa_ref)
            dwx_ref[...] = jnp.zeros_like(dwx_ref)
            gc[...] = jnp.zeros_like(gc)
            anc[...] = jnp.zeros_like(anc)
            dbuf[pl.ds(tm, LRU_HALO), :] = jnp.zeros((LRU_HALO, LANES), F32)

        xbuf[pl.ds(0, LRU_HALO), :] = jnp.where(ti == 0, 0.0, rxp_ref[...])
        xbuf[pl.ds(LRU_HALO, tm), :] = rx_ref[...]
        hbuf[pl.ds(0, LRU_HALO), :] = jnp.where(ti == 0, 0.0, hsp_ref[...])
        hbuf[pl.ds(LRU_HALO, tm), :] = hs_ref[...]

        wa, wx = wa_ref[...], wx_ref[...]
        lam_v = lam_ref[...]
        xr = _tap_sum(xbuf, w4_ref, K4, LRU_HALO - (K4 - 1), 0, tm, False) + b4_ref[...]
        a, mult, r, ig, sp = _lru_gates(xr, wa, ba_ref[...], wx, bx_ref[...], lam_v)

        dyr = dyr_ref[...]
        gl, dgl = _gelu(rg_ref[...])
        dzg_ref[...] = (dyr * hs_ref[...] * dgl).astype(BF16)

        abuf[pl.ds(0, tm), :] = a
        abuf[pl.ds(tm, LRU_HALO), :] = anc[...]
        a_s[...] = abuf[pl.ds(1, tm), :]
        b_s[...] = dyr * gl
        cs, cout = _scan_tile(a_s, b_s, gc[pl.ds(0, 1), :], seg, True)
        gc[pl.ds(0, 1), :] = cout
        anc[pl.ds(0, 1), :] = a[0:1, :]
        for s in range(SUBLANES):
            rows = pl.ds(s * seg, seg)
            b_s[rows, :] = b_s[rows, :] + a_s[rows, :] * cs[s]
        g = b_s[...]

        d_a = g * hbuf[pl.ds(LRU_HALO - 1, tm), :]
        gx_ = g * xr
        d_log_a = d_a * a - (gx_ * ig) * (a * a / mult)
        dga = (d_log_a * (-LRU_C * sp)) * r * (1.0 - r)
        dgx = (gx_ * mult) * ig * (1.0 - ig)
        dga_b, dgx_b = dga.astype(BF16), dgx.astype(BF16)
        dxr = g * mult * ig + _dot_nt(dga_b, wa) + _dot_nt(dgx_b, wx)
        xb = xr.astype(BF16)
        dwa_ref[...] += _dot_tn(xb, dga_b)
        dwx_ref[...] += _dot_tn(xb, dgx_b)
        st_ref[pl.ds(K4, 1), :] += _colsum(dxr)
        st_ref[pl.ds(K4 + 1, 1), :] += _colsum(dga)
        st_ref[pl.ds(K4 + 2, 1), :] += _colsum(dgx)
        st_ref[pl.ds(K4 + 3, 1), :] += _colsum(d_log_a * (-LRU_C * r)) * (-jax.nn.sigmoid(-lam_v))

        dbuf[pl.ds(0, tm), :] = dxr
        for k in range(K4):
            st_ref[pl.ds(k, 1), :] += _colsum(dxr * xbuf[pl.ds(LRU_HALO - (K4 - 1) + k, tm), :])
        dzx_ref[...] = _tap_sum(dbuf, w4_ref, K4, 0, 0, tm, True).astype(BF16)
        dbuf[pl.ds(tm, LRU_HALO), :] = dbuf[pl.ds(0, LRU_HALO), :]

    def rev(col):
        return lambda c, i: (nI - 1 - i, col + c)

    def rev_prev(col):
        return lambda c, i: (jnp.maximum((nI - 1 - i) * hb - 1, 0), col + c)

    vec = pl.BlockSpec((1, LANES), lambda c, i: (0, c))
    mat = pl.BlockSpec((None, LANES, LANES), lambda c, i: (c, 0, 0))
    big = pltpu.VMEM((tm, LANES), F32)
    halo = pltpu.VMEM((tm + LRU_HALO, LANES), F32)
    return pl.pallas_call(
        body, grid=(nC, nI),
        in_specs=[pl.BlockSpec((tm, LANES), rev(cd)),
                  pl.BlockSpec((tm, LANES), rev(0)), pl.BlockSpec((LRU_HALO, LANES), rev_prev(0)),
                  pl.BlockSpec((tm, LANES), rev(cx)), pl.BlockSpec((LRU_HALO, LANES), rev_prev(cx)),
                  pl.BlockSpec((tm, LANES), rev(cg)),
                  pl.BlockSpec((K4, LANES), lambda c, i: (0, c)), vec, mat, vec, mat, vec, vec],
        out_specs=[pl.BlockSpec((tm, LANES), rev(0)), pl.BlockSpec((tm, LANES), rev(0)),
                   pl.BlockSpec((SUBLANES, LANES), lambda c, i: (0, c)), mat, mat],
        out_shape=[jax.ShapeDtypeStruct((T, W), BF16), jax.ShapeDtypeStruct((T, W), BF16),
                   jax.ShapeDtypeStruct((SUBLANES, W), F32),
                   jax.ShapeDtypeStruct((nC, LANES, LANES), F32), jax.ShapeDtypeStruct((nC, LANES, LANES), F32)],
        scratch_shapes=[halo, halo, halo, big, big, halo,
                        pltpu.VMEM((SUBLANES, LANES), F32), pltpu.VMEM((SUBLANES, LANES), F32)],
        compiler_params=_params("parallel", "arbitrary"), name="lru_bwd")(
            dcat, hs, hs, z, z, z, w4, b4, wa, ba, wx, bx, lam)


def _mix_out_fwd(x, u, yr, wout):
    T, D = x.shape
    C, W = u.shape[1], yr.shape[1]
    tm = _tile(T, TOK_TILE)

    def body(x_ref, u_ref, yr_ref, w_ref, y_ref):
        y_ref[...] = (x_ref[...] + _dot(u_ref[...], w_ref[pl.ds(0, C), :])
                      + _dot(yr_ref[...], w_ref[pl.ds(C, W), :]))

    return pl.pallas_call(
        body, grid=(T // tm,),
        in_specs=[pl.BlockSpec((tm, D), lambda i: (i, 0)), pl.BlockSpec((tm, C), lambda i: (i, 0)),
                  pl.BlockSpec((tm, W), lambda i: (i, 0)),
                  pl.BlockSpec((C + W, D), lambda i: (0, 0), pipeline_mode=pl.Buffered(1))],
        out_specs=pl.BlockSpec((tm, D), lambda i: (i, 0)),
        out_shape=jax.ShapeDtypeStruct((T, D), F32),
        compiler_params=_params("parallel"), name="mix_out_fwd")(x, u, yr, wout)


def _mix_out_bwd(dy, u, yr, wout):
    T, D = dy.shape
    C, W = u.shape[1], yr.shape[1]
    tm = _tile(T, BWD_TILE)

    def body(dy_ref, u_ref, yr_ref, w_ref, dcat_ref, dw_ref):
        @pl.when(pl.program_id(0) == 0)
        def _():
            dw_ref[...] = jnp.zeros_like(dw_ref)

        dyb = dy_ref[...].astype(BF16)
        dcat_ref[...] = _dot_nt(dyb, w_ref[...])
        dw_ref[pl.ds(0, C), :] += _dot_tn(u_ref[...], dyb)
        dw_ref[pl.ds(C, W), :] += _dot_tn(yr_ref[...], dyb)

    return pl.pallas_call(
        body, grid=(T // tm,),
        in_specs=[pl.BlockSpec((tm, D), lambda i: (i, 0)), pl.BlockSpec((tm, C), lambda i: (i, 0)),
                  pl.BlockSpec((tm, W), lambda i: (i, 0)),
                  pl.BlockSpec((C + W, D), lambda i: (0, 0), pipeline_mode=pl.Buffered(1))],
        out_specs=[pl.BlockSpec((tm, C + W), lambda i: (i, 0)), pl.BlockSpec((C + W, D), lambda i: (0, 0))],
        out_shape=[jax.ShapeDtypeStruct((T, C + W), F32), jax.ShapeDtypeStruct((C + W, D), F32)],
        compiler_params=_params("arbitrary"), name="mix_out_bwd")(dy, u, yr, wout)


def _mix_in_bwd(dzc, dzx, dzg, x, dy, g, win):
    T, D = x.shape
    ns, ws = win.shape[0], win.shape[2]
    tm = _tile(T, BWD_TILE)
    parts = []
    for j in range(ns):
        lo = j * ws
        if lo < dzc.shape[1]:
            parts.append((0, lo))
        elif lo < dzc.shape[1] + dzx.shape[1]:
            parts.append((1, lo - dzc.shape[1]))
        else:
            parts.append((2, lo - dzc.shape[1] - dzx.shape[1]))

    def body(dzc_ref, dzx_ref, dzg_ref, x_ref, dy_ref, g_ref, w_ref, dx_ref, dw_ref, dg_ref):
        @pl.when(pl.program_id(0) == 0)
        def _():
            dw_ref[...] = jnp.zeros_like(dw_ref)
            dg_ref[...] = jnp.zeros_like(dg_ref)

        xh, r = _rms_stats(x_ref[...])
        gv = g_ref[...]
        hb = (xh * gv).astype(BF16)
        srcs = (dzc_ref, dzx_ref, dzg_ref)
        dh = jnp.zeros((tm, D), F32)
        for j, (si, off) in enumerate(parts):
            dzj = srcs[si][:, pl.ds(off, ws)]
            dh = dh + _dot_nt(dzj, w_ref[j])
            dw_ref[j] += _dot_tn(hb, dzj)
        dx_ref[...] = dy_ref[...] + _rms_bwd(dh, xh, r, gv)
        dg_ref[...] += _colsum(dh * xh)

    def tok(n):
        return pl.BlockSpec((tm, n), lambda i: (i, 0))

    vec = pl.BlockSpec((1, D), lambda i: (0, 0))
    return pl.pallas_call(
        body, grid=(T // tm,),
        in_specs=[tok(dzc.shape[1]), tok(dzx.shape[1]), tok(dzg.shape[1]), tok(D), tok(D), vec,
                  pl.BlockSpec((ns, D, ws), lambda i: (0, 0, 0), pipeline_mode=pl.Buffered(1))],
        out_specs=[tok(D), pl.BlockSpec((ns, D, ws), lambda i: (0, 0, 0)), vec],
        out_shape=[jax.ShapeDtypeStruct((T, D), F32), jax.ShapeDtypeStruct((ns, D, ws), F32),
                   jax.ShapeDtypeStruct((1, D), F32)],
        compiler_params=_params("arbitrary"), name="mix_in_bwd")(dzc, dzx, dzg, x, dy, g, win)


def _final_loss(x, g, tgt):
    T, D = x.shape
    tm = _tile(T, TOK_TILE)

    def body(x_ref, g_ref, t_ref, dx_ref, loss_ref, dg_ref):
        @pl.when(pl.program_id(0) == 0)
        def _():
            loss_ref[...] = jnp.zeros_like(loss_ref)
            dg_ref[...] = jnp.zeros_like(dg_ref)

        xh, r = _rms_stats(x_ref[...])
        gv = g_ref[...]
        e = xh * gv - t_ref[...]
        loss_ref[...] += 0.5 * jnp.sum(jnp.mean(e * e, axis=-1, keepdims=True))
        dy = e * (1.0 / D)
        dg_ref[...] += _colsum(dy * xh)
        dx_ref[...] = _rms_bwd(dy, xh, r, gv)

    tok = pl.BlockSpec((tm, D), lambda i: (i, 0))
    vec = pl.BlockSpec((1, D), lambda i: (0, 0))
    return pl.pallas_call(
        body, grid=(T // tm,),
        in_specs=[tok, vec, tok],
        out_specs=[tok, pl.BlockSpec((SUBLANES, LANES), lambda i: (0, 0)), vec],
        out_shape=[jax.ShapeDtypeStruct((T, D), F32), jax.ShapeDtypeStruct((SUBLANES, LANES), F32),
                   jax.ShapeDtypeStruct((1, D), F32)],
        compiler_params=_params("arbitrary"), name="final_loss")(x, g, tgt)


def _adamw(w, g, m, v, name):
    R, Cc = w.shape
    tr = _tile(R, max(SUBLANES, (1 << 19) // Cc))
    c1 = 1.0 - ADAM_B1 ** ADAM_STEP
    c2 = 1.0 - ADAM_B2 ** ADAM_STEP

    def body(w_ref, g_ref, m_ref, v_ref, d_ref, nm_ref, nv_ref):
        gv = g_ref[...]
        nm = ADAM_B1 * m_ref[...] + (1.0 - ADAM_B1) * gv
        nv = ADAM_B2 * v_ref[...] + (1.0 - ADAM_B2) * (gv * gv)
        nm_ref[...] = nm
        nv_ref[...] = nv
        d_ref[...] = -ADAM_LR * ((nm / c1) / (jnp.sqrt(nv / c2) + ADAM_EPS) + ADAM_WD * w_ref[...])

    blk = pl.BlockSpec((tr, Cc), lambda i: (i, 0))
    sds = jax.ShapeDtypeStruct((R, Cc), F32)
    return pl.pallas_call(
        body, grid=(R // tr,), in_specs=[blk] * 4, out_specs=[blk] * 3, out_shape=[sds] * 3,
        compiler_params=_params("parallel"), name=name)(w, g, m, v)


def _here():
    return lax.axis_index("x"), lax.axis_index("y"), lax.axis_index("c")


def _chip_at(x, y, m):
    return x ^ (m >> 1), y ^ (m & 1)


ANY = pl.BlockSpec(memory_space=pl.ANY)


def _place_cast(srcs, idx, dtype, name):
    n = len(srcs)
    R, Cc = srcs[0].shape
    tr = _tile(R, max(16, (1 << 18) // Cc), 16)

    def body(i_ref, *refs):
        o_ref = refs[n]
        for k in range(n):
            o_ref[k] = refs[k][...].astype(dtype)

    blk = pl.BlockSpec((tr, Cc), lambda i, s: (i, 0))
    return pl.pallas_call(
        body,
        grid_spec=pltpu.PrefetchScalarGridSpec(
            num_scalar_prefetch=1, grid=(R // tr,), in_specs=[blk] * n,
            out_specs=pl.BlockSpec((n, None, tr, Cc), lambda i, s: (0, s[1], i, 0))),
        out_shape=jax.ShapeDtypeStruct((n, N_CHIPS, R, Cc), dtype),
        compiler_params=_params("parallel"), name=name)(idx, *srcs)


def _gather_weights(lands):
    n = len(lands)

    def body(*refs):
        outs = refs[n:2 * n]
        send1, recv1, send2, recv2 = refs[2 * n:]
        x, y, c = _here()
        own = 2 * x + y

        def half(ref, chip, cc):
            rh = ref.shape[-2] // 2
            lead = (slice(None),) * (len(ref.shape) - 3)
            return ref.at[lead + (chip, pl.ds(cc * rh, rh), slice(None))]

        first = []
        for k in range(n):
            for m in (1, 2, 3):
                px, py = _chip_at(x, y, m)
                cp = pltpu.make_async_remote_copy(
                    src_ref=half(outs[k], own, c), dst_ref=half(outs[k], own, c),
                    send_sem=send1.at[k, m - 1], recv_sem=recv1.at[k, m - 1],
                    device_id=(px, py, c), device_id_type=MESH)
                cp.start()
                first.append(cp)

        passed = []
        for k in range(n):
            for m in (1, 2, 3):
                px, py = _chip_at(x, y, m)
                peer = 2 * px + py
                got = half(outs[k], peer, c)
                pltpu.make_async_remote_copy(
                    src_ref=got, dst_ref=got, send_sem=send1.at[k, m - 1], recv_sem=recv1.at[k, m - 1],
                    device_id=(px, py, c), device_id_type=MESH).wait_recv()
                cp = pltpu.make_async_remote_copy(
                    src_ref=got, dst_ref=got, send_sem=send2.at[k, m - 1], recv_sem=recv2.at[k, m - 1],
                    device_id=(x, y, 1 - c), device_id_type=MESH)
                cp.start()
                passed.append(cp)

        for k in range(n):
            for m in (1, 2, 3):
                px, py = _chip_at(x, y, m)
                other = half(outs[k], 2 * px + py, 1 - c)
                pltpu.make_async_remote_copy(
                    src_ref=other, dst_ref=other, send_sem=send2.at[k, m - 1], recv_sem=recv2.at[k, m - 1],
                    device_id=(x, y, 1 - c), device_id_type=MESH).wait_recv()
        for cp in first + passed:
            cp.wait_send()

    return pl.pallas_call(
        body, in_specs=[ANY] * n, out_specs=[ANY] * n,
        out_shape=[jax.ShapeDtypeStruct(a.shape, a.dtype) for a in lands],
        input_output_aliases={k: k for k in range(n)},
        scratch_shapes=[pltpu.SemaphoreType.DMA((n, 3)), pltpu.SemaphoreType.DMA((n, 3)),
                        pltpu.SemaphoreType.DMA((n, 3)), pltpu.SemaphoreType.DMA((n, 3))],
        name="gather_weights")(*lands)


HBM = pl.BlockSpec(memory_space=pltpu.HBM)
SEM = pl.BlockSpec(memory_space=pltpu.SEMAPHORE)
EFFECT = pltpu.SideEffectType.DATAFLOW_SIDE_EFFECTING


def _in_hbm(a):
    return pltpu.with_memory_space_constraint(a, pltpu.HBM)


def _gather_copies(land_refs, send, recv):
    x, y, c = _here()
    own = 2 * x + y
    cps = []
    for k in range(len(land_refs)):
        lead = (slice(None),) * (len(land_refs[k].shape) - 3)
        mine = land_refs[k].at[lead + (own,)]
        for m in (1, 2, 3):
            px, py = _chip_at(x, y, m)
            cps.append(pltpu.make_async_remote_copy(
                src_ref=mine, dst_ref=mine, send_sem=send.at[3 * k + m - 1], recv_sem=recv.at[3 * k + m - 1],
                device_id=(px, py, c), device_id_type=MESH))
    return cps


def _gather_start(lands, after, name):
    n = len(lands)

    def body(*refs):
        lz = refs[:n]
        send, recv = refs[n + 1], refs[n + 2]
        token = refs[-1]
        for cp in _gather_copies(lz, send, recv):
            cp.start()
        token[...] = jnp.zeros_like(token)

    hbm = [pltpu.HBM(a.shape, a.dtype) for a in lands]
    outs = pl.pallas_call(
        body, name=name,
        in_specs=[HBM] * n + [ANY],
        out_specs=[SEM, SEM] + [HBM] * n + [pl.BlockSpec(memory_space=pltpu.VMEM)],
        out_shape=[pltpu.SemaphoreType.DMA((3 * n,)), pltpu.SemaphoreType.DMA((3 * n,))] + hbm
        + [jax.ShapeDtypeStruct((SUBLANES, LANES), F32)],
        input_output_aliases={k: 2 + k for k in range(n)},
        compiler_params=pltpu.CompilerParams(has_side_effects=EFFECT),
    )(*[_in_hbm(a) for a in lands], after)
    return outs[0], outs[1], outs[2:2 + n], outs[-1]


def _gather_wait(send, recv, lands, after, name):
    n = len(lands)

    def body(*refs):
        lz = refs[:n]
        send_r, recv_r = refs[n], refs[n + 1]
        for cp in _gather_copies(lz, send_r, recv_r):
            cp.wait_send()
            cp.wait_recv()

    hbm = [pltpu.HBM(a.shape, a.dtype) for a in lands]
    return pl.pallas_call(
        body, name=name,
        in_specs=[HBM] * n + [SEM, SEM, ANY],
        out_specs=[HBM] * n, out_shape=hbm,
        input_output_aliases={k: k for k in range(n)},
        compiler_params=pltpu.CompilerParams(has_side_effects=EFFECT),
    )(*lands, send, recv, after)


def _exchange_copies(part_refs, slot_refs, send, recv):
    x, y, c = _here()
    cps = []
    for k in range(len(part_refs)):
        for m in (1, 2, 3):
            px, py = _chip_at(x, y, m)
            cps.append(pltpu.make_async_remote_copy(
                src_ref=part_refs[k].at[2 * px + py], dst_ref=slot_refs[k].at[m - 1],
                send_sem=send.at[3 * k + m - 1], recv_sem=recv.at[3 * k + m - 1],
                device_id=(px, py, c), device_id_type=MESH))
    return cps


def _exchange_start(parts):
    n = len(parts)
    lands = [lax.empty((N_CHIPS - 1,) + p.shape[1:], p.dtype) for p in parts]

    def body(*refs):
        ins, lz = refs[:n], refs[n:2 * n]
        send, recv = refs[2 * n], refs[2 * n + 1]
        token = refs[-1]
        for cp in _exchange_copies(ins, lz, send, recv):
            cp.start()
        token[...] = jnp.zeros_like(token)

    hbm = [pltpu.HBM(a.shape, a.dtype) for a in list(parts) + lands]
    outs = pl.pallas_call(
        body, name="exchange_start",
        in_specs=[HBM] * (2 * n),
        out_specs=[SEM, SEM] + [HBM] * (2 * n) + [pl.BlockSpec(memory_space=pltpu.VMEM)],
        out_shape=[pltpu.SemaphoreType.DMA((3 * n,)), pltpu.SemaphoreType.DMA((3 * n,))] + hbm
        + [jax.ShapeDtypeStruct((SUBLANES, LANES), F32)],
        input_output_aliases={k: 2 + k for k in range(2 * n)},
        compiler_params=pltpu.CompilerParams(has_side_effects=EFFECT),
    )(*[_in_hbm(a) for a in parts], *[_in_hbm(a) for a in lands])
    return outs[0], outs[1], outs[2:2 + n], outs[2 + n:2 + 2 * n], outs[-1]


def _exchange_wait(send, recv, parts, lands, after):
    n = len(parts)

    def body(*refs):
        ins, lz = refs[:n], refs[n:2 * n]
        send_r, recv_r = refs[2 * n], refs[2 * n + 1]
        for cp in _exchange_copies(ins, lz, send_r, recv_r):
            cp.wait_send()
            cp.wait_recv()

    hbm = [pltpu.HBM(a.shape, a.dtype) for a in list(parts) + list(lands)]
    outs = pl.pallas_call(
        body, name="exchange_wait",
        in_specs=[HBM] * (2 * n) + [SEM, SEM, ANY],
        out_specs=[HBM] * (2 * n), out_shape=hbm,
        input_output_aliases={k: k for k in range(2 * n)},
        compiler_params=pltpu.CompilerParams(has_side_effects=EFFECT),
    )(*parts, *lands, send, recv, after)
    return outs[:n], outs[n:]


def _swap_halves_out(grads, name):
    n = len(grads)
    out_shapes = [jax.ShapeDtypeStruct((g.shape[0], g.shape[1] // 2, g.shape[2]), g.dtype) for g in grads]

    def body(*refs):
        ins, outs = refs[:n], refs[n:2 * n]
        send, recv = refs[2 * n:]
        x, y, c = _here()
        cps = []
        for k in range(n):
            rh = ins[k].shape[1] // 2
            cp = pltpu.make_async_remote_copy(
                src_ref=ins[k].at[:, pl.ds((1 - c) * rh, rh), :], dst_ref=outs[k],
                send_sem=send.at[k], recv_sem=recv.at[k], device_id=(x, y, 1 - c), device_id_type=MESH)
            cp.start()
            cps.append(cp)
        for cp in cps:
            cp.wait()

    return pl.pallas_call(
        body, in_specs=[ANY] * n, out_specs=[ANY] * n, out_shape=out_shapes,
        scratch_shapes=[pltpu.SemaphoreType.DMA((n,)), pltpu.SemaphoreType.DMA((n,))],
        name=name)(*grads)


def _add_cast(g, other, cidx, name):
    ns, R, Cc = g.shape
    rh = R // 2
    tr = _tile(rh, max(16, (1 << 18) // Cc), 16)
    nb = rh // tr

    def body(c_ref, g_ref, o_ref, s_ref):
        s_ref[...] = (g_ref[...] + o_ref[...]).astype(BF16)

    return pl.pallas_call(
        body,
        grid_spec=pltpu.PrefetchScalarGridSpec(
            num_scalar_prefetch=1, grid=(ns, nb),
            in_specs=[pl.BlockSpec((None, tr, Cc), lambda k, i, c: (k, c[0] * nb + i, 0)),
                      pl.BlockSpec((None, tr, Cc), lambda k, i, c: (k, i, 0))],
            out_specs=pl.BlockSpec((None, tr, Cc), lambda k, i, c: (k, i, 0))),
        out_shape=jax.ShapeDtypeStruct((ns, rh, Cc), BF16),
        compiler_params=_params("parallel", "parallel"), name=name)(cidx, g, other)


def _exchange_chips(parts):
    n = len(parts)
    out_shapes = [jax.ShapeDtypeStruct((N_CHIPS - 1,) + p.shape[1:], p.dtype) for p in parts]

    def body(*refs):
        ins, outs = refs[:n], refs[n:2 * n]
        send, recv = refs[2 * n:]
        cps = _exchange_copies(ins, outs, send, recv)
        for cp in cps:
            cp.start()
        for cp in cps:
            cp.wait()

    return pl.pallas_call(
        body, in_specs=[ANY] * n, out_specs=[ANY] * n, out_shape=out_shapes,
        scratch_shapes=[pltpu.SemaphoreType.DMA((3 * n,)), pltpu.SemaphoreType.DMA((3 * n,))],
        name="exchange_chips")(*parts)


def _sum_slots(part, got, idx, name):
    ns, rh, Cc = got.shape
    tr = _tile(rh, max(16, (1 << 17) // Cc), 16)
    nb = rh // tr

    def body(i_ref, p_ref, b_ref, o_ref):
        acc = p_ref[...].astype(F32)
        for m in range(ns):
            acc = acc + b_ref[m].astype(F32)
        o_ref[...] = acc

    return pl.pallas_call(
        body,
        grid_spec=pltpu.PrefetchScalarGridSpec(
            num_scalar_prefetch=1, grid=(nb,),
            in_specs=[pl.BlockSpec((None, tr, Cc), lambda i, s: (s[1], i, 0)),
                      pl.BlockSpec((ns, tr, Cc), lambda i, s: (0, i, 0))],
            out_specs=pl.BlockSpec((tr, Cc), lambda i, s: (s[0] * nb + i, 0))),
        out_shape=jax.ShapeDtypeStruct((2 * rh, Cc), F32),
        compiler_params=_params("parallel"), name=name)(idx, part, got)


def _share_halves(blocks):
    n = len(blocks)

    def body(*refs):
        ins, outs = refs[:n], refs[n:2 * n]
        send, recv = refs[2 * n:]
        x, y, c = _here()
        cps = []
        for k in range(n):
            rh = outs[k].shape[0] // 2
            mine = outs[k].at[pl.ds(c * rh, rh), :]
            cp = pltpu.make_async_remote_copy(
                src_ref=mine, dst_ref=mine, send_sem=send.at[k], recv_sem=recv.at[k],
                device_id=(x, y, 1 - c), device_id_type=MESH)
            cp.start()
            cps.append(cp)
        for cp in cps:
            cp.wait()

    return pl.pallas_call(
        body, in_specs=[ANY] * n, out_specs=[ANY] * n,
        out_shape=[jax.ShapeDtypeStruct(b.shape, b.dtype) for b in blocks],
        input_output_aliases={k: k for k in range(n)},
        scratch_shapes=[pltpu.SemaphoreType.DMA((n,)), pltpu.SemaphoreType.DMA((n,))],
        name="share_halves")(*blocks)


def _small_copies(p_ref, slot_ref, send, recv):
    x, y, c = _here()
    mine = slot_ref.at[4 * x + 2 * y + c]
    cps = []
    for m in range(1, N_DEV):
        peer = (x ^ (m >> 2), y ^ ((m >> 1) & 1), c ^ (m & 1))
        cps.append(pltpu.make_async_remote_copy(
            src_ref=p_ref, dst_ref=mine, send_sem=send.at[m - 1], recv_sem=recv.at[m - 1],
            device_id=peer, device_id_type=MESH))
    return cps


def _small_start(packed):
    slots = lax.empty((N_DEV,) + packed.shape, packed.dtype)

    def body(p_ref, s_ref, send, recv, p_thru, s_thru, token):
        for cp in _small_copies(p_ref, s_ref, send, recv):
            cp.start()
        token[...] = jnp.zeros_like(token)

    return pl.pallas_call(
        body, name="small_start",
        in_specs=[HBM, HBM],
        out_specs=[SEM, SEM, HBM, HBM, pl.BlockSpec(memory_space=pltpu.VMEM)],
        out_shape=[pltpu.SemaphoreType.DMA((N_DEV - 1,)), pltpu.SemaphoreType.DMA((N_DEV - 1,)),
                   pltpu.HBM(packed.shape, packed.dtype), pltpu.HBM(slots.shape, slots.dtype),
                   jax.ShapeDtypeStruct((SUBLANES, LANES), F32)],
        input_output_aliases={0: 2, 1: 3},
        compiler_params=pltpu.CompilerParams(has_side_effects=EFFECT),
    )(_in_hbm(packed), _in_hbm(slots))


def _small_wait(send, recv, packed, slots, after):
    def body(p_ref, s_ref, send_r, recv_r, after_ref, p_out, s_out):
        for cp in _small_copies(p_ref, s_ref, send_r, recv_r):
            cp.wait_send()
            cp.wait_recv()

    return pl.pallas_call(
        body, name="small_wait",
        in_specs=[HBM, HBM, SEM, SEM, ANY], out_specs=[HBM, HBM],
        out_shape=[pltpu.HBM(packed.shape, packed.dtype), pltpu.HBM(slots.shape, slots.dtype)],
        input_output_aliases={0: 0, 1: 1},
        compiler_params=pltpu.CompilerParams(has_side_effects=EFFECT),
    )(packed, slots, send, recv, after)


def _sum_devices(packed, slots, me):
    n, R, _ = slots.shape
    tr = _tile(R, 256)

    def body(m_ref, p_ref, s_ref, o_ref):
        own = p_ref[...]
        acc = None
        for d in range(n):
            term = jnp.where(m_ref[0] == d, own, s_ref[d])
            acc = term if acc is None else acc + term
        o_ref[...] = acc

    return pl.pallas_call(
        body,
        grid_spec=pltpu.PrefetchScalarGridSpec(
            num_scalar_prefetch=1, grid=(R // tr,),
            in_specs=[pl.BlockSpec((tr, LANES), lambda i, m: (i, 0)),
                      pl.BlockSpec((n, tr, LANES), lambda i, m: (0, i, 0))],
            out_specs=pl.BlockSpec((tr, LANES), lambda i, m: (i, 0))),
        out_shape=jax.ShapeDtypeStruct((R, LANES), F32),
        compiler_params=_params("parallel"), name="sum_devices")(me, packed, slots)


def _pack(arrs):
    rows, parts = [], []
    for a in arrs:
        flat = a.reshape(-1)
        r = -(-flat.shape[0] // (SUBLANES * LANES)) * SUBLANES
        parts.append(jnp.pad(flat, (0, r * LANES - flat.shape[0])).reshape(r, LANES))
        rows.append(r)
    return jnp.concatenate(parts, axis=0), rows


def _unpack(packed, rows, shapes):
    out, r0 = [], 0
    for r, shp in zip(rows, shapes):
        size = math.prod(shp)
        out.append(packed[r0:r0 + r].reshape(-1)[:size].reshape(shp))
        r0 += r
    return out


def _block_diag(w, per):
    H, dh, _ = w.shape
    w4 = w.reshape(H // per, per, dh, dh)
    eye = jnp.eye(per, dtype=w.dtype)
    return (w4[:, :, :, None, :] * eye[None, :, None, :, None]).reshape(H // per, per * dh, per * dh)


def _block_diag_take(d, per):
    n, s, _ = d.shape
    dh = s // per
    d5 = d.reshape(n, per, dh, per, dh)
    return jnp.stack([d5[:, h, :, h, :] for h in range(per)], axis=1).reshape(n * per, dh, dh)


def kernel(x, ffn1_norm, ffn1_w_gate, ffn1_w_up, ffn1_w_down, mix_norm, w_in, conv_dw, conv_dw_bias, conv_ln_g, conv_ln_b, lru_conv_w, lru_conv_b, lru_w_a, lru_b_a, lru_w_x, lru_b_x, lru_lambda, w_out, ffn2_norm, ffn2_w_gate, ffn2_w_up, ffn2_w_down, final_norm, loss_target, m_ffn1_norm, m_ffn1_w_gate, m_ffn1_w_up, m_ffn1_w_down, m_mix_norm, m_w_in, m_conv_dw, m_conv_dw_bias, m_conv_ln_g, m_conv_ln_b, m_lru_conv_w, m_lru_conv_b, m_lru_w_a, m_lru_b_a, m_lru_w_x, m_lru_b_x, m_lru_lambda, m_w_out, m_ffn2_norm, m_ffn2_w_gate, m_ffn2_w_up, m_ffn2_w_down, m_final_norm, v_ffn1_norm, v_ffn1_w_gate, v_ffn1_w_up, v_ffn1_w_down, v_mix_norm, v_w_in, v_conv_dw, v_conv_dw_bias, v_conv_ln_g, v_conv_ln_b, v_lru_conv_w, v_lru_conv_b, v_lru_w_a, v_lru_b_a, v_lru_w_x, v_lru_b_x, v_lru_lambda, v_w_out, v_ffn2_norm, v_ffn2_w_gate, v_ffn2_w_up, v_ffn2_w_down, v_final_norm):
    names = ['ffn1_norm', 'ffn1_w_gate', 'ffn1_w_up', 'ffn1_w_down', 'mix_norm', 'w_in', 'conv_dw', 'conv_dw_bias',
             'conv_ln_g', 'conv_ln_b', 'lru_conv_w', 'lru_conv_b', 'lru_w_a', 'lru_b_a', 'lru_w_x', 'lru_b_x',
             'lru_lambda', 'w_out', 'ffn2_norm', 'ffn2_w_gate', 'ffn2_w_up', 'ffn2_w_down', 'final_norm']
    env = dict(locals())
    W = {n: env[n] for n in names}
    M = {n: env['m_' + n] for n in names}
    V = {n: env['v_' + n] for n in names}

    xi, yi, ci = _here()
    chip = 2 * xi + yi
    cidx = ci.astype(jnp.int32).reshape(1)
    T, D = x.shape[-2], x.shape[-1]
    xs = x.reshape(T, D)
    tgt = loss_target.reshape(T, D)
    K, Cs = conv_dw.shape
    C = conv_dw_bias.shape[0]
    Wl = lru_conv_b.shape[0]
    K4 = lru_conv_w.shape[0]
    heads, dh, _ = lru_w_a.shape
    per = LANES // dh

    def row(v):
        return v.reshape(1, -1)

    tform = ('ffn1_w_gate', 'ffn1_w_up', 'ffn2_w_gate', 'ffn2_w_up')
    for n in tform:
        W[n], M[n], V[n] = W[n].T, M[n].T, V[n].T
    kp = -(-K // SUBLANES) * SUBLANES
    taps = jnp.concatenate([conv_dw, jnp.zeros((kp - K, Cs), F32), lru_conv_w,
                            jnp.zeros((2 * SUBLANES - K4, Cs), F32)], axis=0)
    idx = jnp.stack([ci, chip]).astype(jnp.int32)
    (wff1,) = _gather_weights([_place_cast([W['ffn1_w_gate'], W['ffn1_w_up'], ffn1_w_down], idx, BF16, "place_ffn1")])
    mixl = [_place_cast([w_in], idx, BF16, "place_w_in"), _place_cast([w_out], idx, BF16, "place_w_out"),
            _place_cast([taps], idx, F32, "place_taps")]
    msend, mrecv, mixl, mtok = _gather_start(mixl, wff1, "gather_mix_start")
    ff2l = _place_cast([W['ffn2_w_gate'], W['ffn2_w_up'], ffn2_w_down], idx, BF16, "place_ffn2")
    fsend, frecv, ff2l, ftok = _gather_start([ff2l], mtok, "gather_ffn2_start")
    wa_bd = _block_diag(lru_w_a, per).astype(BF16)
    wx_bd = _block_diag(lru_w_x, per).astype(BF16)

    x1, a1, b1 = _ffn_fwd(xs, row(ffn1_norm) + ftok[0:1, 0:1], wff1, "ffn1_fwd")
    win, wout, taps = _gather_wait(msend, mrecv, mixl, x1, "gather_mix_wait")
    win, wout, taps = win[0], wout.reshape(-1, D), taps[0]
    conv_w_full = taps[:, :K].transpose(1, 0, 2).reshape(K, N_CHIPS * Cs)
    lru_w4_full = taps[:, kp:kp + K4].transpose(1, 0, 2).reshape(K4, N_CHIPS * Cs)
    z = _mix_in_fwd(x1, row(mix_norm), win)
    u, u1 = _conv_fwd(z, conv_w_full, row(conv_dw_bias), row(conv_ln_g), row(conv_ln_b))
    yr, hs = _lru_fwd(z, 2 * C, lru_w4_full, row(lru_conv_b), wa_bd, row(lru_b_a), wx_bd, row(lru_b_x),
                      row(lru_lambda))
    x2 = _mix_out_fwd(x1, u, yr, wout)
    (wff2,) = _gather_wait(fsend, frecv, ff2l, x2, "gather_ffn2_wait")
    x3, a2, b2 = _ffn_fwd(x2, row(ffn2_norm), wff2, "ffn2_fwd")
    dx3, loss_blk, d_final = _final_loss(x3, row(final_norm), tgt)

    dx2, da2, db2, p2, hb2, dyh2, d_ffn2n = _ffn_bwd_tok(dx3, x2, row(ffn2_norm), a2, b2, wff2, "ffn2_bwd")
    dwg2, dwu2, dwd2 = _ffn_wgrad(hb2, dyh2, da2, db2, p2, ftok, "ffn2_wgrad")
    dcat, dwout = _mix_out_bwd(dx2, u, yr, wout)
    dzc, cst = _conv_bwd(dcat, u1, z, conv_w_full, row(conv_ln_g), row(conv_ln_b))
    dzx, dzg, lst, dwa_bd, dwx_bd = _lru_bwd(dcat, C, hs, z, 2 * C, lru_w4_full, row(lru_conv_b), wa_bd,
                                              row(lru_b_a), wx_bd, row(lru_b_x), row(lru_lambda))
    dx1, dwin, d_mixn = _mix_in_bwd(dzc, dzx, dzg, x1, dx2, row(mix_norm), win)

    early_names = ['w_in', 'w_out', 'ffn2_w_gate', 'ffn2_w_up', 'ffn2_w_down']
    early = [dwin, dwout.reshape(N_CHIPS, -1, D), dwg2, dwu2, dwd2]
    e_parts = [_add_cast(g, o, cidx, "add_cast_" + n)
               for g, o, n in zip(early, _swap_halves_out(early, "swap_halves_early"), early_names)]
    esend, erecv, e_parts, e_lands, etok = _exchange_start(e_parts)

    dx0, da1, db1, p1, hb1, dyh1, d_ffn1n = _ffn_bwd_tok(dx1, xs, row(ffn1_norm) + etok[0:1, 0:1], a1, b1, wff1,
                                                         "ffn1_bwd")

    small_names = ['ffn1_norm', 'mix_norm', 'conv_dw', 'conv_dw_bias', 'conv_ln_g', 'conv_ln_b', 'lru_conv_w',
                   'lru_conv_b', 'lru_w_a', 'lru_b_a', 'lru_w_x', 'lru_b_x', 'lru_lambda', 'ffn2_norm',
                   'final_norm']
    small = {
        'ffn1_norm': d_ffn1n, 'mix_norm': d_mixn, 'conv_dw': cst[:K], 'conv_dw_bias': cst[K + 1],
        'conv_ln_g': cst[K + 2], 'conv_ln_b': cst[K + 3], 'lru_conv_w': lst[:K4], 'lru_conv_b': lst[K4],
        'lru_w_a': _block_diag_take(dwa_bd, per), 'lru_b_a': lst[K4 + 1],
        'lru_w_x': _block_diag_take(dwx_bd, per), 'lru_b_x': lst[K4 + 2], 'lru_lambda': lst[K4 + 3],
        'ffn2_norm': d_ffn2n, 'final_norm': d_final,
    }
    packed, rows = _pack([small[n] for n in small_names])
    ssend, srecv, packed, sslots, stok = _small_start(packed)

    dwg1, dwu1, dwd1 = _ffn_wgrad(hb1, dyh1, da1, db1, p1, stok, "ffn1_wgrad")

    last_names = ['ffn1_w_gate', 'ffn1_w_up', 'ffn1_w_down']
    last = [dwg1, dwu1, dwd1]
    l_parts = [_add_cast(g, o, cidx, "add_cast_" + n)
               for g, o, n in zip(last, _swap_halves_out(last, "swap_halves_last"), last_names)]
    l_slots = _exchange_chips(l_parts)
    e_parts, e_slots = _exchange_wait(esend, erecv, e_parts, e_lands, dwd1)
    big_names = early_names + last_names
    halves = [_sum_slots(p, b, idx, "sum_slots_" + n)
              for p, b, n in zip(list(e_parts) + l_parts, list(e_slots) + list(l_slots), big_names)]
    G = dict(zip(big_names, _share_halves(halves)))

    full_shapes = [(K, C) if n == 'conv_dw' else (K4, Wl) if n == 'lru_conv_w' else W[n].shape for n in small_names]
    packed, sslots = _small_wait(ssend, srecv, packed, sslots, dwd1)
    summed = _sum_devices(packed, sslots, (4 * xi + 2 * yi + ci).astype(jnp.int32).reshape(1))
    for n, gsum in zip(small_names, _unpack(summed, rows, full_shapes)):
        if n == 'conv_dw':
            gsum = lax.dynamic_slice_in_dim(gsum, chip * Cs, Cs, axis=1)
        elif n == 'lru_conv_w':
            gsum = lax.dynamic_slice_in_dim(gsum, chip * lru_conv_w.shape[1], lru_conv_w.shape[1], axis=1)
        G[n] = gsum

    delta, new_m, new_v = {}, {}, {}
    for n in big_names:
        shp = W[n].shape
        g2 = G[n] if G[n].shape == shp else G[n].reshape(shp)
        G[n] = g2
        delta[n], new_m[n], new_v[n] = _adamw(W[n], g2, M[n], V[n], "adamw_" + n)
    pw, prow = _pack([W[n] for n in small_names])
    pg, _ = _pack([G[n] for n in small_names])
    pm, _ = _pack([M[n] for n in small_names])
    pv, _ = _pack([V[n] for n in small_names])
    sd, sm, sv = _adamw(pw, pg, pm, pv, "adamw_small")
    shapes = [W[n].shape for n in small_names]
    for n, a, b, c_ in zip(small_names, _unpack(sd, prow, shapes), _unpack(sm, prow, shapes),
                           _unpack(sv, prow, shapes)):
        delta[n], new_m[n], new_v[n] = a, b, c_

    loss = lax.psum(loss_blk[0, 0], ("x", "y", "c"))
    grad_x = dx0.reshape(x.shape)
    for n in tform:
        G[n], delta[n], new_m[n], new_v[n] = G[n].T, delta[n].T, new_m[n].T, new_v[n].T
    return (loss, grad_x, *[G[n] for n in names], *[delta[n] for n in names],
            *[new_m[n] for n in names], *[new_v[n] for n in names])
```

```python
import functools
import math

import jax
import jax.numpy as jnp
from jax import lax
from jax.experimental import pallas as pl
from jax.experimental.pallas import tpu as pltpu

F32 = jnp.float32
BF16 = jnp.bfloat16
MESH = pl.DeviceIdType.MESH

RMS_EPS = 1e-6
LN_EPS = 1e-5
LRU_C = 8.0
FFN_RES_SCALE = 0.5
ADAM_LR = 0.001
ADAM_B1 = 0.9
ADAM_B2 = 0.999
ADAM_EPS = 1e-08
ADAM_WD = 0.01
ADAM_STEP = 10

LANES = 128
SUBLANES = 8
CONV_HALO = 32
LRU_HALO = 8
ROW_CHUNK = 64
VMEM_LIMIT = 56 * 1024 * 1024
VMEM_LIMIT_BIG = 61 * 1024 * 1024
N_CHIPS = 4
N_DEV = 8
TOK_TILE = 1024
BWD_TILE = 512
FFN_BWD_TILE = 1024
BWD_ROWS = 32
FFN_BWD_CHAIN = 512
CONV_TILE = 512
LRU_TILE = 1024


def _dot(a, b):
    return jnp.dot(a, b, preferred_element_type=F32)


def _dot_nt(a, b):
    return lax.dot_general(a, b, (((1,), (1,)), ((), ())), preferred_element_type=F32)


def _dot_tn(a, b):
    return lax.dot_general(a, b, (((0,), (0,)), ((), ())), preferred_element_type=F32)


def _tile(n, pref, mult=SUBLANES):
    for t in range(min(pref, n), 0, -1):
        if n % t == 0 and t % mult == 0:
            return t
    return n


def _params(*sem, vmem=None):
    return pltpu.CompilerParams(dimension_semantics=sem, vmem_limit_bytes=vmem or VMEM_LIMIT)


def _rms_stats(x):
    r = lax.rsqrt(jnp.mean(x * x, axis=-1, keepdims=True) + RMS_EPS)
    return x * r, r


def _rms_bwd(dh, xh, r, g):
    dxh = dh * g
    return r * (dxh - xh * jnp.mean(dxh * xh, axis=-1, keepdims=True))


def _colsum(v):
    return jnp.sum(v, axis=0, keepdims=True)


def _ffn_fwd(x, g, wff, name):
    T, D = x.shape
    ns, fs = wff.shape[1], wff.shape[2]
    tm = _tile(T, TOK_TILE)
    mc = _tile(tm, FFN_BWD_CHAIN, 16)

    def body(x_ref, g_ref, wg_ref, wu_ref, wd_ref, y_ref, a_ref, b_ref, hb_ref, acc_ref):
        j = pl.program_id(1)

        @pl.when(j == 0)
        def _():
            xh, _ = _rms_stats(x_ref[...])
            hb_ref[...] = (xh * g_ref[...]).astype(BF16)
            acc_ref[...] = jnp.zeros_like(acc_ref)

        for q0 in range(0, tm, mc):
            blk = pl.ds(q0, mc)
            hb = hb_ref[blk, :]
            a = _dot_nt(hb, wg_ref[...])
            b = _dot_nt(hb, wu_ref[...])
            a_ref[blk, :] = a.astype(BF16)
            b_ref[blk, :] = b.astype(BF16)
            p = (a * jax.nn.sigmoid(a) * b).astype(BF16)
            acc_ref[blk, :] += _dot(p, wd_ref[...])

        @pl.when(j == ns - 1)
        def _():
            y_ref[...] = x_ref[...] + FFN_RES_SCALE * acc_ref[...]

    def wspec(n):
        return pl.BlockSpec((None, None, fs, D), lambda i, j: (n, j, 0, 0))

    mid = pl.BlockSpec((None, tm, fs), lambda i, j: (j, i, 0))
    return pl.pallas_call(
        body, grid=(T // tm, ns),
        in_specs=[pl.BlockSpec((tm, D), lambda i, j: (i, 0)), pl.BlockSpec((1, D), lambda i, j: (0, 0)),
                  wspec(0), wspec(1), wspec(2)],
        out_specs=[pl.BlockSpec((tm, D), lambda i, j: (i, 0)), mid, mid],
        out_shape=[jax.ShapeDtypeStruct((T, D), F32), jax.ShapeDtypeStruct((ns, T, fs), BF16),
                   jax.ShapeDtypeStruct((ns, T, fs), BF16)],
        scratch_shapes=[pltpu.VMEM((tm, D), BF16), pltpu.VMEM((tm, D), F32)],
        compiler_params=_params("parallel", "arbitrary"), name=name)(x, g, wff, wff, wff)


def _ffn_bwd_tok(dy, x, g, a, b, wff, name):
    T, D = x.shape
    ns, fs = wff.shape[1], wff.shape[2]
    tm = _tile(T, FFN_BWD_TILE)
    rc = _tile(tm, BWD_ROWS)
    mc = _tile(tm, FFN_BWD_CHAIN, rc)

    def body(dy_ref, x_ref, g_ref, a_ref, b_ref, wg_ref, wu_ref, wd_ref,
             dx_ref, da_ref, db_ref, p_ref, hb_ref, dyh_ref, dg_ref, dh_ref, dp_ref):
        i, j = pl.program_id(0), pl.program_id(1)

        @pl.when((i == 0) & (j == 0))
        def _():
            dg_ref[...] = jnp.zeros_like(dg_ref)

        @pl.when(j == 0)
        def _():
            for r0 in range(0, tm, rc):
                rows = pl.ds(r0, rc)
                xh, _ = _rms_stats(x_ref[rows, :])
                hb_ref[rows, :] = (xh * g_ref[...]).astype(BF16)
                dyh_ref[rows, :] = (FFN_RES_SCALE * dy_ref[rows, :]).astype(BF16)
            dh_ref[...] = jnp.zeros_like(dh_ref)

        for q0 in range(0, tm, mc):
            blk = pl.ds(q0, mc)
            dp_ref[blk, :] = _dot_nt(dyh_ref[blk, :], wd_ref[...])
            for r0 in range(q0, q0 + mc, rc):
                rows = pl.ds(r0, rc)
                av = a_ref[rows, :].astype(F32)
                bv = b_ref[rows, :].astype(F32)
                dp = dp_ref[rows, :]
                s = jax.nn.sigmoid(av)
                sl = av * s
                da_ref[rows, :] = (dp * bv * (s * (1.0 + av * (1.0 - s)))).astype(BF16)
                db_ref[rows, :] = (dp * sl).astype(BF16)
                p_ref[rows, :] = (sl * bv).astype(BF16)
            dh_ref[blk, :] += _dot(da_ref[blk, :], wg_ref[...]) + _dot(db_ref[blk, :], wu_ref[...])

        @pl.when(j == ns - 1)
        def _():
            gv = g_ref[...]
            dg = jnp.zeros((1, D), F32)
            for r0 in range(0, tm, rc):
                rows = pl.ds(r0, rc)
                xh, r = _rms_stats(x_ref[rows, :])
                dh = dh_ref[rows, :]
                dx_ref[rows, :] = dy_ref[rows, :] + _rms_bwd(dh, xh, r, gv)
                dg = dg + _colsum(dh * xh)
            dg_ref[...] += dg

    def wspec(n):
        return pl.BlockSpec((None, None, fs, D), lambda i, j: (n, j, 0, 0))

    tok = pl.BlockSpec((tm, D), lambda i, j: (i, 0), pipeline_mode=pl.Buffered(1))
    mid = pl.BlockSpec((None, tm, fs), lambda i, j: (j, i, 0))
    vec = pl.BlockSpec((1, D), lambda i, j: (0, 0))
    return pl.pallas_call(
        body, grid=(T // tm, ns),
        in_specs=[tok, tok, vec, mid, mid, wspec(0), wspec(1), wspec(2)],
        out_specs=[tok, mid, mid, mid, tok, tok, vec],
        out_shape=[jax.ShapeDtypeStruct((T, D), F32),
                   jax.ShapeDtypeStruct((ns, T, fs), BF16), jax.ShapeDtypeStruct((ns, T, fs), BF16),
                   jax.ShapeDtypeStruct((ns, T, fs), BF16),
                   jax.ShapeDtypeStruct((T, D), BF16), jax.ShapeDtypeStruct((T, D), BF16),
                   jax.ShapeDtypeStruct((1, D), F32)],
        scratch_shapes=[pltpu.VMEM((tm, D), F32), pltpu.VMEM((tm, fs), F32)],
        compiler_params=_params("arbitrary", "arbitrary", vmem=VMEM_LIMIT_BIG), name=name)(dy, x, g, a, b, wff, wff, wff)


def _ffn_wgrad(hb, dyh, da, db, p, after, name):
    T, D = hb.shape
    ns, _, fs = da.shape
    tm = _tile(T, TOK_TILE)

    def body(hb_ref, dyh_ref, da_ref, db_ref, p_ref, after_ref, dwg_ref, dwu_ref, dwd_ref):
        @pl.when(pl.program_id(1) == 0)
        def _():
            dwg_ref[...] = jnp.zeros_like(dwg_ref)
            dwu_ref[...] = jnp.zeros_like(dwu_ref)
            dwd_ref[...] = jnp.zeros_like(dwd_ref)

        hbv = hb_ref[...]
        dwg_ref[...] += _dot_tn(da_ref[...], hbv)
        dwu_ref[...] += _dot_tn(db_ref[...], hbv)
        dwd_ref[...] += _dot_tn(p_ref[...], dyh_ref[...])

    tok = pl.BlockSpec((tm, D), lambda j, i: (i, 0))
    mid = pl.BlockSpec((None, tm, fs), lambda j, i: (j, i, 0))
    wsp = pl.BlockSpec((None, fs, D), lambda j, i: (j, 0, 0))
    sds = jax.ShapeDtypeStruct((ns, fs, D), F32)
    return pl.pallas_call(
        body, grid=(ns, T // tm),
        in_specs=[tok, tok, mid, mid, mid, pl.BlockSpec((SUBLANES, LANES), lambda j, i: (0, 0))],
        out_specs=[wsp, wsp, wsp], out_shape=[sds, sds, sds],
        compiler_params=_params("parallel", "arbitrary"), name=name)(hb, dyh, da, db, p, after)


def _mix_in_fwd(x, g, win):
    T, D = x.shape
    ns, ws = win.shape[0], win.shape[2]
    tm = _tile(T, TOK_TILE)

    def body(x_ref, g_ref, w_ref, z_ref):
        xh, _ = _rms_stats(x_ref[...])
        hb = (xh * g_ref[...]).astype(BF16)
        for j in range(ns):
            z_ref[:, pl.ds(j * ws, ws)] = _dot(hb, w_ref[j])

    return pl.pallas_call(
        body, grid=(T // tm,),
        in_specs=[pl.BlockSpec((tm, D), lambda i: (i, 0)), pl.BlockSpec((1, D), lambda i: (0, 0)),
                  pl.BlockSpec((ns, D, ws), lambda i: (0, 0, 0), pipeline_mode=pl.Buffered(1))],
        out_specs=pl.BlockSpec((tm, ns * ws), lambda i: (i, 0)),
        out_shape=jax.ShapeDtypeStruct((T, ns * ws), F32),
        compiler_params=_params("parallel"), name="mix_in_fwd")(x, g, win)


def _tap_sum(buf, w_ref, ntaps, first_row, r0, rows, flip):
    acc = None
    for k in range(ntaps):
        off = (ntaps - 1 - k) if flip else k
        t = buf[pl.ds(first_row + r0 + off, rows), :] * w_ref[pl.ds(k, 1), :]
        acc = t if acc is None else acc + t
    return acc


def _shift_copies(buf, sh, rows):
    for r in range(1, SUBLANES):
        sh[r - 1, pl.ds(0, rows), :] = buf[pl.ds(r, rows), :]


def _tap_rows(buf, sh, off, r0, rows):
    r = off % SUBLANES
    if r == 0:
        return buf[pl.ds(off + r0, rows), :]
    return sh[r - 1, pl.ds(off - r + r0, rows), :]


def _tap_sum_tiles(buf, sh, w_ref, ntaps, first_row, r0, rows, flip):
    acc = None
    for k in range(ntaps):
        off = first_row + ((ntaps - 1 - k) if flip else k)
        t = _tap_rows(buf, sh, off, r0, rows) * w_ref[pl.ds(k, 1), :]
        acc = t if acc is None else acc + t
    return acc


def _conv_fwd(z, w, bias, lng, lnb):
    T = z.shape[0]
    K, C = w.shape
    tm = _tile(T, CONV_TILE, ROW_CHUNK)
    rc = min(ROW_CHUNK, tm)
    srows = tm + CONV_HALO - SUBLANES

    def body(cv_ref, cg_ref, w_ref, b_ref, g_ref, bb_ref, u_ref, u1_ref, buf, sh):
        @pl.when(pl.program_id(0) == 0)
        def _():
            buf[pl.ds(0, CONV_HALO), :] = jnp.zeros((CONV_HALO, C), F32)

        buf[pl.ds(CONV_HALO, tm), :] = cv_ref[...] * jax.nn.sigmoid(cg_ref[...])
        _shift_copies(buf, sh, srows)
        for r0 in range(0, tm, rc):
            u1 = _tap_sum_tiles(buf, sh, w_ref, K, CONV_HALO - (K - 1), r0, rc, False) + b_ref[...]
            u1_ref[pl.ds(r0, rc), :] = u1
            xc = u1 - jnp.mean(u1, axis=-1, keepdims=True)
            xh = xc * lax.rsqrt(jnp.mean(xc * xc, axis=-1, keepdims=True) + LN_EPS)
            u2 = xh * g_ref[...] + bb_ref[...]
            u_ref[pl.ds(r0, rc), :] = (u2 * jax.nn.sigmoid(u2)).astype(BF16)
        buf[pl.ds(0, CONV_HALO), :] = buf[pl.ds(tm, CONV_HALO), :]

    vec = pl.BlockSpec((1, C), lambda i: (0, 0))
    return pl.pallas_call(
        body, grid=(T // tm,),
        in_specs=[pl.BlockSpec((tm, C), lambda i: (i, 0)), pl.BlockSpec((tm, C), lambda i: (i, 1)),
                  pl.BlockSpec((K, C), lambda i: (0, 0)), vec, vec, vec],
        out_specs=[pl.BlockSpec((tm, C), lambda i: (i, 0)), pl.BlockSpec((tm, C), lambda i: (i, 0))],
        out_shape=[jax.ShapeDtypeStruct((T, C), BF16), jax.ShapeDtypeStruct((T, C), F32)],
        scratch_shapes=[pltpu.VMEM((CONV_HALO + tm, C), F32), pltpu.VMEM((SUBLANES - 1, srows, C), F32)],
        compiler_params=_params("arbitrary"), name="conv_fwd")(z, z, w, bias, lng, lnb)


def _conv_bwd(dcat, u1, z, w, lng, lnb):
    T = z.shape[0]
    K, C = w.shape
    tm = _tile(T, CONV_TILE, ROW_CHUNK)
    rc = min(ROW_CHUNK, tm)
    nI = T // tm
    hb = tm // CONV_HALO
    srows = ((K + 4 + SUBLANES - 1) // SUBLANES) * SUBLANES
    shrows = tm + CONV_HALO - SUBLANES

    def body(du_ref, u1_ref, cv_ref, cg_ref, cvp_ref, cgp_ref, w_ref, g_ref, bb_ref,
             dz_ref, st_ref, u0buf, d1buf, ush, dsh):
        i = pl.program_id(0)
        ti = nI - 1 - i

        @pl.when(i == 0)
        def _():
            st_ref[...] = jnp.zeros_like(st_ref)
            d1buf[pl.ds(tm, CONV_HALO), :] = jnp.zeros((CONV_HALO, C), F32)

        prev = cvp_ref[...] * jax.nn.sigmoid(cgp_ref[...])
        u0buf[pl.ds(0, CONV_HALO), :] = jnp.where(ti == 0, 0.0, prev)
        u0buf[pl.ds(CONV_HALO, tm), :] = cv_ref[...] * jax.nn.sigmoid(cg_ref[...])

        gv = g_ref[...]
        dbias = jnp.zeros((1, C), F32)
        dgain = jnp.zeros((1, C), F32)
        dlnb = jnp.zeros((1, C), F32)
        for r0 in range(0, tm, rc):
            u1 = u1_ref[pl.ds(r0, rc), :]
            xc = u1 - jnp.mean(u1, axis=-1, keepdims=True)
            rstd = lax.rsqrt(jnp.mean(xc * xc, axis=-1, keepdims=True) + LN_EPS)
            xh = xc * rstd
            u2 = xh * gv + bb_ref[...]
            s = jax.nn.sigmoid(u2)
            du2 = du_ref[pl.ds(r0, rc), :] * (s * (1.0 + u2 * (1.0 - s)))
            dgain = dgain + _colsum(du2 * xh)
            dlnb = dlnb + _colsum(du2)
            dxh = du2 * gv
            du1 = rstd * (dxh - jnp.mean(dxh, axis=-1, keepdims=True)
                          - xh * jnp.mean(dxh * xh, axis=-1, keepdims=True))
            dbias = dbias + _colsum(du1)
            d1buf[pl.ds(r0, rc), :] = du1
        st_ref[pl.ds(K + 1, 1), :] += dbias
        st_ref[pl.ds(K + 2, 1), :] += dgain
        st_ref[pl.ds(K + 3, 1), :] += dlnb

        _shift_copies(u0buf, ush, shrows)
        _shift_copies(d1buf, dsh, shrows)
        for k in range(K):
            acc = jnp.zeros((SUBLANES, C), F32)
            for r0 in range(0, tm, rc):
                prod = d1buf[pl.ds(r0, rc), :] * _tap_rows(u0buf, ush, CONV_HALO - (K - 1) + k, r0, rc)
                acc = acc + jnp.sum(prod.reshape(rc // SUBLANES, SUBLANES, C), axis=0)
            st_ref[pl.ds(k, 1), :] += _colsum(acc)

        for r0 in range(0, tm, rc):
            du0 = _tap_sum_tiles(d1buf, dsh, w_ref, K, 0, r0, rc, True)
            cv = cv_ref[pl.ds(r0, rc), :]
            sg = jax.nn.sigmoid(cg_ref[pl.ds(r0, rc), :])
            dz_ref[pl.ds(r0, rc), pl.ds(0, C)] = (du0 * sg).astype(BF16)
            dz_ref[pl.ds(r0, rc), pl.ds(C, C)] = (du0 * cv * sg * (1.0 - sg)).astype(BF16)
        d1buf[pl.ds(tm, CONV_HALO), :] = d1buf[pl.ds(0, CONV_HALO), :]

    def rev(col):
        return lambda i: (nI - 1 - i, col)

    def rev_prev(col):
        return lambda i: (jnp.maximum((nI - 1 - i) * hb - 1, 0), col)

    vec = pl.BlockSpec((1, C), lambda i: (0, 0))
    return pl.pallas_call(
        body, grid=(nI,),
        in_specs=[pl.BlockSpec((tm, C), rev(0)), pl.BlockSpec((tm, C), rev(0)),
                  pl.BlockSpec((tm, C), rev(0)), pl.BlockSpec((tm, C), rev(1)),
                  pl.BlockSpec((CONV_HALO, C), rev_prev(0)), pl.BlockSpec((CONV_HALO, C), rev_prev(1)),
                  pl.BlockSpec((K, C), lambda i: (0, 0)), vec, vec],
        out_specs=[pl.BlockSpec((tm, 2 * C), rev(0)), pl.BlockSpec((srows, C), lambda i: (0, 0))],
        out_shape=[jax.ShapeDtypeStruct((T, 2 * C), BF16), jax.ShapeDtypeStruct((srows, C), F32)],
        scratch_shapes=[pltpu.VMEM((CONV_HALO + tm, C), F32), pltpu.VMEM((tm + CONV_HALO, C), F32),
                        pltpu.VMEM((SUBLANES - 1, shrows, C), F32), pltpu.VMEM((SUBLANES - 1, shrows, C), F32)],
        compiler_params=_params("arbitrary"), name="conv_bwd")(dcat, u1, z, z, z, z, w, lng, lnb)


def _softplus(v):
    return jnp.maximum(v, 0.0) + jnp.log(1.0 + jnp.exp(-jnp.abs(v)))


def _gelu(v):
    c = math.sqrt(2.0 / math.pi)
    t = jnp.tanh(c * (v + 0.044715 * v * v * v))
    gl = 0.5 * v * (1.0 + t)
    dgl = 0.5 * (1.0 + t) + 0.5 * v * (1.0 - t * t) * c * (1.0 + 3.0 * 0.044715 * v * v)
    return gl, dgl


def _lru_gates(xr, wa, ba, wx, bx, lam):
    xb = xr.astype(BF16)
    r = jax.nn.sigmoid(_dot(xb, wa) + ba)
    ig = jax.nn.sigmoid(_dot(xb, wx) + bx)
    sp = _softplus(-lam)
    log_a = -LRU_C * r * sp
    a = jnp.exp(log_a)
    y = 2.0 * log_a
    series = -(y * (1.0 + y * (0.5 + y * (1.0 / 6.0 + y * (1.0 / 24.0)))))
    mult = jnp.sqrt(jnp.where(y > -0.02, series, 1.0 - jnp.exp(y)))
    return a, mult, r, ig, sp


def _scan_tile(a_s, b_s, carry, seg, reverse):
    def step(n, hp):
        hl, pr = hp
        k = (seg - 1 - n) if reverse else n
        rows = pl.ds(k, SUBLANES, stride=seg)
        av = a_s[rows, :]
        hl = av * hl + b_s[rows, :]
        pr = av * pr
        b_s[rows, :] = hl
        a_s[rows, :] = pr
        return hl, pr

    hl, pr = lax.fori_loop(0, seg, step, (jnp.zeros((SUBLANES, LANES), F32), jnp.ones((SUBLANES, LANES), F32)),
                           unroll=min(8, seg))
    cs = [None] * SUBLANES
    c = carry
    for s in (range(SUBLANES - 1, -1, -1) if reverse else range(SUBLANES)):
        cs[s] = c
        c = hl[s:s + 1, :] + pr[s:s + 1, :] * c
    return cs, c


def _lru_fwd(z, col0, w4, b4, wa, ba, wx, bx, lam):
    T = z.shape[0]
    K4, W = w4.shape
    nC = W // LANES
    tm = _tile(T, LRU_TILE, SUBLANES * SUBLANES)
    seg = tm // SUBLANES
    cx, cg = col0 // LANES, (col0 + W) // LANES

    def body(rx_ref, rg_ref, w4_ref, b4_ref, wa_ref, ba_ref, wx_ref, bx_ref, lam_ref,
             yr_ref, hs_ref, xbuf, a_s, b_s, hc):
        @pl.when(pl.program_id(1) == 0)
        def _():
            xbuf[pl.ds(0, LRU_HALO), :] = jnp.zeros((LRU_HALO, LANES), F32)
            hc[...] = jnp.zeros_like(hc)

        xbuf[pl.ds(LRU_HALO, tm), :] = rx_ref[...]
        xr = _tap_sum(xbuf, w4_ref, K4, LRU_HALO - (K4 - 1), 0, tm, False) + b4_ref[...]
        a, mult, _, ig, _ = _lru_gates(xr, wa_ref[...], ba_ref[...], wx_ref[...], bx_ref[...], lam_ref[...])
        a_s[...] = a
        b_s[...] = mult * ig * xr
        cs, cout = _scan_tile(a_s, b_s, hc[pl.ds(0, 1), :], seg, False)
        hc[pl.ds(0, 1), :] = cout
        for s in range(SUBLANES):
            rows = pl.ds(s * seg, seg)
            h = b_s[rows, :] + a_s[rows, :] * cs[s]
            hs_ref[rows, :] = h
            gl, _ = _gelu(rg_ref[rows, :])
            yr_ref[rows, :] = (h * gl).astype(BF16)
        xbuf[pl.ds(0, LRU_HALO), :] = xbuf[pl.ds(tm, LRU_HALO), :]

    vec = pl.BlockSpec((1, LANES), lambda c, i: (0, c))
    mat = pl.BlockSpec((None, LANES, LANES), lambda c, i: (c, 0, 0))
    return pl.pallas_call(
        body, grid=(nC, T // tm),
        in_specs=[pl.BlockSpec((tm, LANES), lambda c, i: (i, cx + c)),
                  pl.BlockSpec((tm, LANES), lambda c, i: (i, cg + c)),
                  pl.BlockSpec((K4, LANES), lambda c, i: (0, c)), vec, mat, vec, mat, vec, vec],
        out_specs=[pl.BlockSpec((tm, LANES), lambda c, i: (i, c)), pl.BlockSpec((tm, LANES), lambda c, i: (i, c))],
        out_shape=[jax.ShapeDtypeStruct((T, W), BF16), jax.ShapeDtypeStruct((T, W), F32)],
        scratch_shapes=[pltpu.VMEM((LRU_HALO + tm, LANES), F32), pltpu.VMEM((tm, LANES), F32),
                        pltpu.VMEM((tm, LANES), F32), pltpu.VMEM((SUBLANES, LANES), F32)],
        compiler_params=_params("parallel", "arbitrary"), name="lru_fwd")(z, z, w4, b4, wa, ba, wx, bx, lam)


def _lru_bwd(dcat, dcol0, hs, z, col0, w4, b4, wa, ba, wx, bx, lam):
    T = z.shape[0]
    K4, W = w4.shape
    assert K4 + 4 == SUBLANES
    nC = W // LANES
    tm = _tile(T, LRU_TILE, SUBLANES * SUBLANES)
    seg = tm // SUBLANES
    nI = T // tm
    hb = tm // LRU_HALO
    cx, cg, cd = col0 // LANES, (col0 + W) // LANES, dcol0 // LANES

    def body(dyr_ref, hs_ref, hsp_ref, rx_ref, rxp_ref, rg_ref, w4_ref, b4_ref, wa_ref, ba_ref, wx_ref, bx_ref,
             lam_ref, dzx_ref, dzg_ref, st_ref, dwa_ref, dwx_ref, xbuf, hbuf, abuf, a_s, b_s, dbuf, gc, anc):
        i = pl.program_id(1)
        ti = nI - 1 - i

        @pl.when(i == 0)
        def _():
            st_ref[...] = jnp.zeros_like(st_ref)
            dwa_ref[...] = jnp.zeros_like(dwa_ref)
            dwx_ref[...] = jnp.zeros_like(dwx_ref)
            gc[...] = jnp.zeros_like(gc)
            anc[...] = jnp.zeros_like(anc)
            dbuf[pl.ds(tm, LRU_HALO), :] = jnp.zeros((LRU_HALO, LANES), F32)

        xbuf[pl.ds(0, LRU_HALO), :] = jnp.where(ti == 0, 0.0, rxp_ref[...])
        xbuf[pl.ds(LRU_HALO, tm), :] = rx_ref[...]
        hbuf[pl.ds(0, LRU_HALO), :] = jnp.where(ti == 0, 0.0, hsp_ref[...])
        hbuf[pl.ds(LRU_HALO, tm), :] = hs_ref[...]

        wa, wx = wa_ref[...], wx_ref[...]
        lam_v = lam_ref[...]
        xr = _tap_sum(xbuf, w4_ref, K4, LRU_HALO - (K4 - 1), 0, tm, False) + b4_ref[...]
        a, mult, r, ig, sp = _lru_gates(xr, wa, ba_ref[...], wx, bx_ref[...], lam_v)

        dyr = dyr_ref[...]
        gl, dgl = _gelu(rg_ref[...])
        dzg_ref[...] = (dyr * hs_ref[...] * dgl).astype(BF16)

        abuf[pl.ds(0, tm), :] = a
        abuf[pl.ds(tm, LRU_HALO), :] = anc[...]
        a_s[...] = abuf[pl.ds(1, tm), :]
        b_s[...] = dyr * gl
        cs, cout = _scan_tile(a_s, b_s, gc[pl.ds(0, 1), :], seg, True)
        gc[pl.ds(0, 1), :] = cout
        anc[pl.ds(0, 1), :] = a[0:1, :]
        for s in range(SUBLANES):
            rows = pl.ds(s * seg, seg)
            b_s[rows, :] = b_s[rows, :] + a_s[rows, :] * cs[s]
        g = b_s[...]

        d_a = g * hbuf[pl.ds(LRU_HALO - 1, tm), :]
        gx_ = g * xr
        d_log_a = d_a * a - (gx_ * ig) * (a * a / mult)
        dga = (d_log_a * (-LRU_C * sp)) * r * (1.0 - r)
        dgx = (gx_ * mult) * ig * (1.0 - ig)
        dga_b, dgx_b = dga.astype(BF16), dgx.astype(BF16)
        dxr = g * mult * ig + _dot_nt(dga_b, wa) + _dot_nt(dgx_b, wx)
        xb = xr.astype(BF16)
        dwa_ref[...] += _dot_tn(xb, dga_b)
        dwx_ref[...] += _dot_tn(xb, dgx_b)
        st_ref[pl.ds(K4, 1), :] += _colsum(dxr)
        st_ref[pl.ds(K4 + 1, 1), :] += _colsum(dga)
        st_ref[pl.ds(K4 + 2, 1), :] += _colsum(dgx)
        st_ref[pl.ds(K4 + 3, 1), :] += _colsum(d_log_a * (-LRU_C * r)) * (-jax.nn.sigmoid(-lam_v))

        dbuf[pl.ds(0, tm), :] = dxr
        for k in range(K4):
            st_ref[pl.ds(k, 1), :] += _colsum(dxr * xbuf[pl.ds(LRU_HALO - (K4 - 1) + k, tm), :])
        dzx_ref[...] = _tap_sum(dbuf, w4_ref, K4, 0, 0, tm, True).astype(BF16)
        dbuf[pl.ds(tm, LRU_HALO), :] = dbuf[pl.ds(0, LRU_HALO), :]

    def rev(col):
        return lambda c, i: (nI - 1 - i, col + c)

    def rev_prev(col):
        return lambda c, i: (jnp.maximum((nI - 1 - i) * hb - 1, 0), col + c)

    vec = pl.BlockSpec((1, LANES), lambda c, i: (0, c))
    mat = pl.BlockSpec((None, LANES, LANES), lambda c, i: (c, 0, 0))
    big = pltpu.VMEM((tm, LANES), F32)
    halo = pltpu.VMEM((tm + LRU_HALO, LANES), F32)
    return pl.pallas_call(
        body, grid=(nC, nI),
        in_specs=[pl.BlockSpec((tm, LANES), rev(cd)),
                  pl.BlockSpec((tm, LANES), rev(0)), pl.BlockSpec((LRU_HALO, LANES), rev_prev(0)),
                  pl.BlockSpec((tm, LANES), rev(cx)), pl.BlockSpec((LRU_HALO, LANES), rev_prev(cx)),
                  pl.BlockSpec((tm, LANES), rev(cg)),
                  pl.BlockSpec((K4, LANES), lambda c, i: (0, c)), vec, mat, vec, mat, vec, vec],
        out_specs=[pl.BlockSpec((tm, LANES), rev(0)), pl.BlockSpec((tm, LANES), rev(0)),
                   pl.BlockSpec((SUBLANES, LANES), lambda c, i: (0, c)), mat, mat],
        out_shape=[jax.ShapeDtypeStruct((T, W), BF16), jax.ShapeDtypeStruct((T, W), BF16),
                   jax.ShapeDtypeStruct((SUBLANES, W), F32),
                   jax.ShapeDtypeStruct((nC, LANES, LANES), F32), jax.ShapeDtypeStruct((nC, LANES, LANES), F32)],
        scratch_shapes=[halo, halo, halo, big, big, halo,
                        pltpu.VMEM((SUBLANES, LANES), F32), pltpu.VMEM((SUBLANES, LANES), F32)],
        compiler_params=_params("parallel", "arbitrary"), name="lru_bwd")(
            dcat, hs, hs, z, z, z, w4, b4, wa, ba, wx, bx, lam)


def _mix_out_fwd(x, u, yr, wout):
    T, D = x.shape
    C, W = u.shape[1], yr.shape[1]
    tm = _tile(T, TOK_TILE)

    def body(x_ref, u_ref, yr_ref, w_ref, y_ref):
        y_ref[...] = (x_ref[...] + _dot(u_ref[...], w_ref[pl.ds(0, C), :])
                      + _dot(yr_ref[...], w_ref[pl.ds(C, W), :]))

    return pl.pallas_call(
        body, grid=(T // tm,),
        in_specs=[pl.BlockSpec((tm, D), lambda i: (i, 0)), pl.BlockSpec((tm, C), lambda i: (i, 0)),
                  pl.BlockSpec((tm, W), lambda i: (i, 0)),
                  pl.BlockSpec((C + W, D), lambda i: (0, 0), pipeline_mode=pl.Buffered(1))],
        out_specs=pl.BlockSpec((tm, D), lambda i: (i, 0)),
        out_shape=jax.ShapeDtypeStruct((T, D), F32),
        compiler_params=_params("parallel"), name="mix_out_fwd")(x, u, yr, wout)


def _mix_out_bwd(dy, u, yr, wout):
    T, D = dy.shape
    C, W = u.shape[1], yr.shape[1]
    tm = _tile(T, BWD_TILE)

    def body(dy_ref, u_ref, yr_ref, w_ref, dcat_ref, dw_ref):
        @pl.when(pl.program_id(0) == 0)
        def _():
            dw_ref[...] = jnp.zeros_like(dw_ref)

        dyb = dy_ref[...].astype(BF16)
        dcat_ref[...] = _dot_nt(dyb, w_ref[...])
        dw_ref[pl.ds(0, C), :] += _dot_tn(u_ref[...], dyb)
        dw_ref[pl.ds(C, W), :] += _dot_tn(yr_ref[...], dyb)

    return pl.pallas_call(
        body, grid=(T // tm,),
        in_specs=[pl.BlockSpec((tm, D), lambda i: (i, 0)), pl.BlockSpec((tm, C), lambda i: (i, 0)),
                  pl.BlockSpec((tm, W), lambda i: (i, 0)),
                  pl.BlockSpec((C + W, D), lambda i: (0, 0), pipeline_mode=pl.Buffered(1))],
        out_specs=[pl.BlockSpec((tm, C + W), lambda i: (i, 0)), pl.BlockSpec((C + W, D), lambda i: (0, 0))],
        out_shape=[jax.ShapeDtypeStruct((T, C + W), F32), jax.ShapeDtypeStruct((C + W, D), F32)],
        compiler_params=_params("arbitrary"), name="mix_out_bwd")(dy, u, yr, wout)


def _mix_in_bwd(dzc, dzx, dzg, x, dy, g, win):
    T, D = x.shape
    ns, ws = win.shape[0], win.shape[2]
    tm = _tile(T, BWD_TILE)
    parts = []
    for j in range(ns):
        lo = j * ws
        if lo < dzc.shape[1]:
            parts.append((0, lo))
        elif lo < dzc.shape[1] + dzx.shape[1]:
            parts.append((1, lo - dzc.shape[1]))
        else:
            parts.append((2, lo - dzc.shape[1] - dzx.shape[1]))

    def body(dzc_ref, dzx_ref, dzg_ref, x_ref, dy_ref, g_ref, w_ref, dx_ref, dw_ref, dg_ref):
        @pl.when(pl.program_id(0) == 0)
        def _():
            dw_ref[...] = jnp.zeros_like(dw_ref)
            dg_ref[...] = jnp.zeros_like(dg_ref)

        xh, r = _rms_stats(x_ref[...])
        gv = g_ref[...]
        hb = (xh * gv).astype(BF16)
        srcs = (dzc_ref, dzx_ref, dzg_ref)
        dh = jnp.zeros((tm, D), F32)
        for j, (si, off) in enumerate(parts):
            dzj = srcs[si][:, pl.ds(off, ws)]
            dh = dh + _dot_nt(dzj, w_ref[j])
            dw_ref[j] += _dot_tn(hb, dzj)
        dx_ref[...] = dy_ref[...] + _rms_bwd(dh, xh, r, gv)
        dg_ref[...] += _colsum(dh * xh)

    def tok(n):
        return pl.BlockSpec((tm, n), lambda i: (i, 0))

    vec = pl.BlockSpec((1, D), lambda i: (0, 0))
    return pl.pallas_call(
        body, grid=(T // tm,),
        in_specs=[tok(dzc.shape[1]), tok(dzx.shape[1]), tok(dzg.shape[1]), tok(D), tok(D), vec,
                  pl.BlockSpec((ns, D, ws), lambda i: (0, 0, 0), pipeline_mode=pl.Buffered(1))],
        out_specs=[tok(D), pl.BlockSpec((ns, D, ws), lambda i: (0, 0, 0)), vec],
        out_shape=[jax.ShapeDtypeStruct((T, D), F32), jax.ShapeDtypeStruct((ns, D, ws), F32),
                   jax.ShapeDtypeStruct((1, D), F32)],
        compiler_params=_params("arbitrary"), name="mix_in_bwd")(dzc, dzx, dzg, x, dy, g, win)


def _final_loss(x, g, tgt):
    T, D = x.shape
    tm = _tile(T, TOK_TILE)

    def body(x_ref, g_ref, t_ref, dx_ref, loss_ref, dg_ref):
        @pl.when(pl.program_id(0) == 0)
        def _():
            loss_ref[...] = jnp.zeros_like(loss_ref)
            dg_ref[...] = jnp.zeros_like(dg_ref)

        xh, r = _rms_stats(x_ref[...])
        gv = g_ref[...]
        e = xh * gv - t_ref[...]
        loss_ref[...] += 0.5 * jnp.sum(jnp.mean(e * e, axis=-1, keepdims=True))
        dy = e * (1.0 / D)
        dg_ref[...] += _colsum(dy * xh)
        dx_ref[...] = _rms_bwd(dy, xh, r, gv)

    tok = pl.BlockSpec((tm, D), lambda i: (i, 0))
    vec = pl.BlockSpec((1, D), lambda i: (0, 0))
    return pl.pallas_call(
        body, grid=(T // tm,),
        in_specs=[tok, vec, tok],
        out_specs=[tok, pl.BlockSpec((SUBLANES, LANES), lambda i: (0, 0)), vec],
        out_shape=[jax.ShapeDtypeStruct((T, D), F32), jax.ShapeDtypeStruct((SUBLANES, LANES), F32),
                   jax.ShapeDtypeStruct((1, D), F32)],
        compiler_params=_params("arbitrary"), name="final_loss")(x, g, tgt)


def _adamw(w, g, m, v, name):
    R, Cc = w.shape
    tr = _tile(R, max(SUBLANES, (1 << 19) // Cc))
    c1 = 1.0 - ADAM_B1 ** ADAM_STEP
    c2 = 1.0 - ADAM_B2 ** ADAM_STEP

    def body(w_ref, g_ref, m_ref, v_ref, d_ref, nm_ref, nv_ref):
        gv = g_ref[...]
        nm = ADAM_B1 * m_ref[...] + (1.0 - ADAM_B1) * gv
        nv = ADAM_B2 * v_ref[...] + (1.0 - ADAM_B2) * (gv * gv)
        nm_ref[...] = nm
        nv_ref[...] = nv
        d_ref[...] = -ADAM_LR * ((nm / c1) / (jnp.sqrt(nv / c2) + ADAM_EPS) + ADAM_WD * w_ref[...])

    blk = pl.BlockSpec((tr, Cc), lambda i: (i, 0))
    sds = jax.ShapeDtypeStruct((R, Cc), F32)
    return pl.pallas_call(
        body, grid=(R // tr,), in_specs=[blk] * 4, out_specs=[blk] * 3, out_shape=[sds] * 3,
        compiler_params=_params("parallel"), name=name)(w, g, m, v)


def _here():
    return lax.axis_index("x"), lax.axis_index("y"), lax.axis_index("c")


def _chip_at(x, y, m):
    return x ^ (m >> 1), y ^ (m & 1)


ANY = pl.BlockSpec(memory_space=pl.ANY)


def _place_cast(srcs, idx, dtype, name):
    n = len(srcs)
    R, Cc = srcs[0].shape
    tr = _tile(R, max(16, (1 << 18) // Cc), 16)

    def body(i_ref, *refs):
        o_ref = refs[n]
        for k in range(n):
            o_ref[k] = refs[k][...].astype(dtype)

    blk = pl.BlockSpec((tr, Cc), lambda i, s: (i, 0))
    return pl.pallas_call(
        body,
        grid_spec=pltpu.PrefetchScalarGridSpec(
            num_scalar_prefetch=1, grid=(R // tr,), in_specs=[blk] * n,
            out_specs=pl.BlockSpec((n, None, tr, Cc), lambda i, s: (0, s[1], i, 0))),
        out_shape=jax.ShapeDtypeStruct((n, N_CHIPS, R, Cc), dtype),
        compiler_params=_params("parallel"), name=name)(idx, *srcs)


def _gather_weights(lands):
    n = len(lands)

    def body(*refs):
        outs = refs[n:2 * n]
        send1, recv1, send2, recv2 = refs[2 * n:]
        x, y, c = _here()
        own = 2 * x + y

        def half(ref, chip, cc):
            rh = ref.shape[-2] // 2
            lead = (slice(None),) * (len(ref.shape) - 3)
            return ref.at[lead + (chip, pl.ds(cc * rh, rh), slice(None))]

        first = []
        for k in range(n):
            for m in (1, 2, 3):
                px, py = _chip_at(x, y, m)
                cp = pltpu.make_async_remote_copy(
                    src_ref=half(outs[k], own, c), dst_ref=half(outs[k], own, c),
                    send_sem=send1.at[k, m - 1], recv_sem=recv1.at[k, m - 1],
                    device_id=(px, py, c), device_id_type=MESH)
                cp.start()
                first.append(cp)

        passed = []
        for k in range(n):
            for m in (1, 2, 3):
                px, py = _chip_at(x, y, m)
                peer = 2 * px + py
                got = half(outs[k], peer, c)
                pltpu.make_async_remote_copy(
                    src_ref=got, dst_ref=got, send_sem=send1.at[k, m - 1], recv_sem=recv1.at[k, m - 1],
                    device_id=(px, py, c), device_id_type=MESH).wait_recv()
                cp = pltpu.make_async_remote_copy(
                    src_ref=got, dst_ref=got, send_sem=send2.at[k, m - 1], recv_sem=recv2.at[k, m - 1],
                    device_id=(x, y, 1 - c), device_id_type=MESH)
                cp.start()
                passed.append(cp)

        for k in range(n):
            for m in (1, 2, 3):
                px, py = _chip_at(x, y, m)
                other = half(outs[k], 2 * px + py, 1 - c)
                pltpu.make_async_remote_copy(
                    src_ref=other, dst_ref=other, send_sem=send2.at[k, m - 1], recv_sem=recv2.at[k, m - 1],
                    device_id=(x, y, 1 - c), device_id_type=MESH).wait_recv()
        for cp in first + passed:
            cp.wait_send()

    return pl.pallas_call(
        body, in_specs=[ANY] * n, out_specs=[ANY] * n,
        out_shape=[jax.ShapeDtypeStruct(a.shape, a.dtype) for a in lands],
        input_output_aliases={k: k for k in range(n)},
        scratch_shapes=[pltpu.SemaphoreType.DMA((n, 3)), pltpu.SemaphoreType.DMA((n, 3)),
                        pltpu.SemaphoreType.DMA((n, 3)), pltpu.SemaphoreType.DMA((n, 3))],
        name="gather_weights")(*lands)


HBM = pl.BlockSpec(memory_space=pltpu.HBM)
SEM = pl.BlockSpec(memory_space=pltpu.SEMAPHORE)
EFFECT = pltpu.SideEffectType.DATAFLOW_SIDE_EFFECTING


def _in_hbm(a):
    return pltpu.with_memory_space_constraint(a, pltpu.HBM)


def _gather_copies(land_refs, send, recv):
    x, y, c = _here()
    own = 2 * x + y
    cps = []
    for k in range(len(land_refs)):
        lead = (slice(None),) * (len(land_refs[k].shape) - 3)
        mine = land_refs[k].at[lead + (own,)]
        for m in (1, 2, 3):
            px, py = _chip_at(x, y, m)
            cps.append(pltpu.make_async_remote_copy(
                src_ref=mine, dst_ref=mine, send_sem=send.at[3 * k + m - 1], recv_sem=recv.at[3 * k + m - 1],
                device_id=(px, py, c), device_id_type=MESH))
    return cps


def _gather_start(lands, after, name):
    n = len(lands)

    def body(*refs):
        lz = refs[:n]
        send, recv = refs[n + 1], refs[n + 2]
        token = refs[-1]
        for cp in _gather_copies(lz, send, recv):
            cp.start()
        token[...] = jnp.zeros_like(token)

    hbm = [pltpu.HBM(a.shape, a.dtype) for a in lands]
    outs = pl.pallas_call(
        body, name=name,
        in_specs=[HBM] * n + [ANY],
        out_specs=[SEM, SEM] + [HBM] * n + [pl.BlockSpec(memory_space=pltpu.VMEM)],
        out_shape=[pltpu.SemaphoreType.DMA((3 * n,)), pltpu.SemaphoreType.DMA((3 * n,))] + hbm
        + [jax.ShapeDtypeStruct((SUBLANES, LANES), F32)],
        input_output_aliases={k: 2 + k for k in range(n)},
        compiler_params=pltpu.CompilerParams(has_side_effects=EFFECT),
    )(*[_in_hbm(a) for a in lands], after)
    return outs[0], outs[1], outs[2:2 + n], outs[-1]


def _gather_wait(send, recv, lands, after, name):
    n = len(lands)

    def body(*refs):
        lz = refs[:n]
        send_r, recv_r = refs[n], refs[n + 1]
        for cp in _gather_copies(lz, send_r, recv_r):
            cp.wait_send()
            cp.wait_recv()

    hbm = [pltpu.HBM(a.shape, a.dtype) for a in lands]
    return pl.pallas_call(
        body, name=name,
        in_specs=[HBM] * n + [SEM, SEM, ANY],
        out_specs=[HBM] * n, out_shape=hbm,
        input_output_aliases={k: k for k in range(n)},
        compiler_params=pltpu.CompilerParams(has_side_effects=EFFECT),
    )(*lands, send, recv, after)


def _exchange_copies(part_refs, slot_refs, send, recv):
    x, y, c = _here()
    cps = []
    for k in range(len(part_refs)):
        for m in (1, 2, 3):
            px, py = _chip_at(x, y, m)
            cps.append(pltpu.make_async_remote_copy(
                src_ref=part_refs[k].at[2 * px + py], dst_ref=slot_refs[k].at[m - 1],
                send_sem=send.at[3 * k + m - 1], recv_sem=recv.at[3 * k + m - 1],
                device_id=(px, py, c), device_id_type=MESH))
    return cps


def _exchange_start(parts):
    n = len(parts)
    lands = [lax.empty((N_CHIPS - 1,) + p.shape[1:], p.dtype) for p in parts]

    def body(*refs):
        ins, lz = refs[:n], refs[n:2 * n]
        send, recv = refs[2 * n], refs[2 * n + 1]
        token = refs[-1]
        for cp in _exchange_copies(ins, lz, send, recv):
            cp.start()
        token[...] = jnp.zeros_like(token)

    hbm = [pltpu.HBM(a.shape, a.dtype) for a in list(parts) + lands]
    outs = pl.pallas_call(
        body, name="exchange_start",
        in_specs=[HBM] * (2 * n),
        out_specs=[SEM, SEM] + [HBM] * (2 * n) + [pl.BlockSpec(memory_space=pltpu.VMEM)],
        out_shape=[pltpu.SemaphoreType.DMA((3 * n,)), pltpu.SemaphoreType.DMA((3 * n,))] + hbm
        + [jax.ShapeDtypeStruct((SUBLANES, LANES), F32)],
        input_output_aliases={k: 2 + k for k in range(2 * n)},
        compiler_params=pltpu.CompilerParams(has_side_effects=EFFECT),
    )(*[_in_hbm(a) for a in parts], *[_in_hbm(a) for a in lands])
    return outs[0], outs[1], outs[2:2 + n], outs[2 + n:2 + 2 * n], outs[-1]


def _exchange_wait(send, recv, parts, lands, after):
    n = len(parts)

    def body(*refs):
        ins, lz = refs[:n], refs[n:2 * n]
        send_r, recv_r = refs[2 * n], refs[2 * n + 1]
        for cp in _exchange_copies(ins, lz, send_r, recv_r):
            cp.wait_send()
            cp.wait_recv()

    hbm = [pltpu.HBM(a.shape, a.dtype) for a in list(parts) + list(lands)]
    outs = pl.pallas_call(
        body, name="exchange_wait",
        in_specs=[HBM] * (2 * n) + [SEM, SEM, ANY],
        out_specs=[HBM] * (2 * n), out_shape=hbm,
        input_output_aliases={k: k for k in range(2 * n)},
        compiler_params=pltpu.CompilerParams(has_side_effects=EFFECT),
    )(*parts, *lands, send, recv, after)
    return outs[:n], outs[n:]


def _swap_halves_out(grads, name):
    n = len(grads)
    out_shapes = [jax.ShapeDtypeStruct((g.shape[0], g.shape[1] // 2, g.shape[2]), g.dtype) for g in grads]

    def body(*refs):
        ins, outs = refs[:n], refs[n:2 * n]
        send, recv = refs[2 * n:]
        x, y, c = _here()
        cps = []
        for k in range(n):
            rh = ins[k].shape[1] // 2
            cp = pltpu.make_async_remote_copy(
                src_ref=ins[k].at[:, pl.ds((1 - c) * rh, rh), :], dst_ref=outs[k],
                send_sem=send.at[k], recv_sem=recv.at[k], device_id=(x, y, 1 - c), device_id_type=MESH)
            cp.start()
            cps.append(cp)
        for cp in cps:
            cp.wait()

    return pl.pallas_call(
        body, in_specs=[ANY] * n, out_specs=[ANY] * n, out_shape=out_shapes,
        scratch_shapes=[pltpu.SemaphoreType.DMA((n,)), pltpu.SemaphoreType.DMA((n,))],
        name=name)(*grads)


def _add_cast(g, other, cidx, name):
    ns, R, Cc = g.shape
    rh = R // 2
    tr = _tile(rh, max(16, (1 << 18) // Cc), 16)
    nb = rh // tr

    def body(c_ref, g_ref, o_ref, s_ref):
        s_ref[...] = (g_ref[...] + o_ref[...]).astype(BF16)

    return pl.pallas_call(
        body,
        grid_spec=pltpu.PrefetchScalarGridSpec(
            num_scalar_prefetch=1, grid=(ns, nb),
            in_specs=[pl.BlockSpec((None, tr, Cc), lambda k, i, c: (k, c[0] * nb + i, 0)),
                      pl.BlockSpec((None, tr, Cc), lambda k, i, c: (k, i, 0))],
            out_specs=pl.BlockSpec((None, tr, Cc), lambda k, i, c: (k, i, 0))),
        out_shape=jax.ShapeDtypeStruct((ns, rh, Cc), BF16),
        compiler_params=_params("parallel", "parallel"), name=name)(cidx, g, other)


def _exchange_chips(parts):
    n = len(parts)
    out_shapes = [jax.ShapeDtypeStruct((N_CHIPS - 1,) + p.shape[1:], p.dtype) for p in parts]

    def body(*refs):
        ins, outs = refs[:n], refs[n:2 * n]
        send, recv = refs[2 * n:]
        cps = _exchange_copies(ins, outs, send, recv)
        for cp in cps:
            cp.start()
        for cp in cps:
            cp.wait()

    return pl.pallas_call(
        body, in_specs=[ANY] * n, out_specs=[ANY] * n, out_shape=out_shapes,
        scratch_shapes=[pltpu.SemaphoreType.DMA((3 * n,)), pltpu.SemaphoreType.DMA((3 * n,))],
        name="exchange_chips")(*parts)


def _sum_slots(part, got, idx, name):
    ns, rh, Cc = got.shape
    tr = _tile(rh, max(16, (1 << 17) // Cc), 16)
    nb = rh // tr

    def body(i_ref, p_ref, b_ref, o_ref):
        acc = p_ref[...].astype(F32)
        for m in range(ns):
            acc = acc + b_ref[m].astype(F32)
        o_ref[...] = acc

    return pl.pallas_call(
        body,
        grid_spec=pltpu.PrefetchScalarGridSpec(
            num_scalar_prefetch=1, grid=(nb,),
            in_specs=[pl.BlockSpec((None, tr, Cc), lambda i, s: (s[1], i, 0)),
                      pl.BlockSpec((ns, tr, Cc), lambda i, s: (0, i, 0))],
            out_specs=pl.BlockSpec((tr, Cc), lambda i, s: (s[0] * nb + i, 0))),
        out_shape=jax.ShapeDtypeStruct((2 * rh, Cc), F32),
        compiler_params=_params("parallel"), name=name)(idx, part, got)


def _share_halves(blocks):
    n = len(blocks)

    def body(*refs):
        ins, outs = refs[:n], refs[n:2 * n]
        send, recv = refs[2 * n:]
        x, y, c = _here()
        cps = []
        for k in range(n):
            rh = outs[k].shape[0] // 2
            mine = outs[k].at[pl.ds(c * rh, rh), :]
            cp = pltpu.make_async_remote_copy(
                src_ref=mine, dst_ref=mine, send_sem=send.at[k], recv_sem=recv.at[k],
                device_id=(x, y, 1 - c), device_id_type=MESH)
            cp.start()
            cps.append(cp)
        for cp in cps:
            cp.wait()

    return pl.pallas_call(
        body, in_specs=[ANY] * n, out_specs=[ANY] * n,
        out_shape=[jax.ShapeDtypeStruct(b.shape, b.dtype) for b in blocks],
        input_output_aliases={k: k for k in range(n)},
        scratch_shapes=[pltpu.SemaphoreType.DMA((n,)), pltpu.SemaphoreType.DMA((n,))],
        name="share_halves")(*blocks)


def _small_copies(p_ref, slot_ref, send, recv):
    x, y, c = _here()
    mine = slot_ref.at[4 * x + 2 * y + c]
    cps = []
    for m in range(1, N_DEV):
        peer = (x ^ (m >> 2), y ^ ((m >> 1) & 1), c ^ (m & 1))
        cps.append(pltpu.make_async_remote_copy(
            src_ref=p_ref, dst_ref=mine, send_sem=send.at[m - 1], recv_sem=recv.at[m - 1],
            device_id=peer, device_id_type=MESH))
    return cps


def _small_start(packed):
    slots = lax.empty((N_DEV,) + packed.shape, packed.dtype)

    def body(p_ref, s_ref, send, recv, p_thru, s_thru, token):
        for cp in _small_copies(p_ref, s_ref, send, recv):
            cp.start()
        token[...] = jnp.zeros_like(token)

    return pl.pallas_call(
        body, name="small_start",
        in_specs=[HBM, HBM],
        out_specs=[SEM, SEM, HBM, HBM, pl.BlockSpec(memory_space=pltpu.VMEM)],
        out_shape=[pltpu.SemaphoreType.DMA((N_DEV - 1,)), pltpu.SemaphoreType.DMA((N_DEV - 1,)),
                   pltpu.HBM(packed.shape, packed.dtype), pltpu.HBM(slots.shape, slots.dtype),
                   jax.ShapeDtypeStruct((SUBLANES, LANES), F32)],
        input_output_aliases={0: 2, 1: 3},
        compiler_params=pltpu.CompilerParams(has_side_effects=EFFECT),
    )(_in_hbm(packed), _in_hbm(slots))


def _small_wait(send, recv, packed, slots, after):
    def body(p_ref, s_ref, send_r, recv_r, after_ref, p_out, s_out):
        for cp in _small_copies(p_ref, s_ref, send_r, recv_r):
            cp.wait_send()
            cp.wait_recv()

    return pl.pallas_call(
        body, name="small_wait",
        in_specs=[HBM, HBM, SEM, SEM, ANY], out_specs=[HBM, HBM],
        out_shape=[pltpu.HBM(packed.shape, packed.dtype), pltpu.HBM(slots.shape, slots.dtype)],
        input_output_aliases={0: 0, 1: 1},
        compiler_params=pltpu.CompilerParams(has_side_effects=EFFECT),
    )(packed, slots, send, recv, after)


def _sum_devices(packed, slots, me):
    n, R, _ = slots.shape
    tr = _tile(R, 256)

    def body(m_ref, p_ref, s_ref, o_ref):
        own = p_ref[...]
        acc = None
        for d in range(n):
            term = jnp.where(m_ref[0] == d, own, s_ref[d])
            acc = term if acc is None else acc + term
        o_ref[...] = acc

    return pl.pallas_call(
        body,
        grid_spec=pltpu.PrefetchScalarGridSpec(
            num_scalar_prefetch=1, grid=(R // tr,),
            in_specs=[pl.BlockSpec((tr, LANES), lambda i, m: (i, 0)),
                      pl.BlockSpec((n, tr, LANES), lambda i, m: (0, i, 0))],
            out_specs=pl.BlockSpec((tr, LANES), lambda i, m: (i, 0))),
        out_shape=jax.ShapeDtypeStruct((R, LANES), F32),
        compiler_params=_params("parallel"), name="sum_devices")(me, packed, slots)


def _pack(arrs):
    rows, parts = [], []
    for a in arrs:
        flat = a.reshape(-1)
        r = -(-flat.shape[0] // (SUBLANES * LANES)) * SUBLANES
        parts.append(jnp.pad(flat, (0, r * LANES - flat.shape[0])).reshape(r, LANES))
        rows.append(r)
    return jnp.concatenate(parts, axis=0), rows


def _unpack(packed, rows, shapes):
    out, r0 = [], 0
    for r, shp in zip(rows, shapes):
        size = math.prod(shp)
        out.append(packed[r0:r0 + r].reshape(-1)[:size].reshape(shp))
        r0 += r
    return out


def _block_diag(w, per):
    H, dh, _ = w.shape
    w4 = w.reshape(H // per, per, dh, dh)
    eye = jnp.eye(per, dtype=w.dtype)
    return (w4[:, :, :, None, :] * eye[None, :, None, :, None]).reshape(H // per, per * dh, per * dh)


def _block_diag_take(d, per):
    n, s, _ = d.shape
    dh = s // per
    d5 = d.reshape(n, per, dh, per, dh)
    return jnp.stack([d5[:, h, :, h, :] for h in range(per)], axis=1).reshape(n * per, dh, dh)


def kernel(x, ffn1_norm, ffn1_w_gate, ffn1_w_up, ffn1_w_down, mix_norm, w_in, conv_dw, conv_dw_bias, conv_ln_g, conv_ln_b, lru_conv_w, lru_conv_b, lru_w_a, lru_b_a, lru_w_x, lru_b_x, lru_lambda, w_out, ffn2_norm, ffn2_w_gate, ffn2_w_up, ffn2_w_down, final_norm, loss_target, m_ffn1_norm, m_ffn1_w_gate, m_ffn1_w_up, m_ffn1_w_down, m_mix_norm, m_w_in, m_conv_dw, m_conv_dw_bias, m_conv_ln_g, m_conv_ln_b, m_lru_conv_w, m_lru_conv_b, m_lru_w_a, m_lru_b_a, m_lru_w_x, m_lru_b_x, m_lru_lambda, m_w_out, m_ffn2_norm, m_ffn2_w_gate, m_ffn2_w_up, m_ffn2_w_down, m_final_norm, v_ffn1_norm, v_ffn1_w_gate, v_ffn1_w_up, v_ffn1_w_down, v_mix_norm, v_w_in, v_conv_dw, v_conv_dw_bias, v_conv_ln_g, v_conv_ln_b, v_lru_conv_w, v_lru_conv_b, v_lru_w_a, v_lru_b_a, v_lru_w_x, v_lru_b_x, v_lru_lambda, v_w_out, v_ffn2_norm, v_ffn2_w_gate, v_ffn2_w_up, v_ffn2_w_down, v_final_norm):
    names = ['ffn1_norm', 'ffn1_w_gate', 'ffn1_w_up', 'ffn1_w_down', 'mix_norm', 'w_in', 'conv_dw', 'conv_dw_bias',
             'conv_ln_g', 'conv_ln_b', 'lru_conv_w', 'lru_conv_b', 'lru_w_a', 'lru_b_a', 'lru_w_x', 'lru_b_x',
             'lru_lambda', 'w_out', 'ffn2_norm', 'ffn2_w_gate', 'ffn2_w_up', 'ffn2_w_down', 'final_norm']
    env = dict(locals())
    W = {n: env[n] for n in names}
    M = {n: env['m_' + n] for n in names}
    V = {n: env['v_' + n] for n in names}

    xi, yi, ci = _here()
    chip = 2 * xi + yi
    cidx = ci.astype(jnp.int32).reshape(1)
    T, D = x.shape[-2], x.shape[-1]
    xs = x.reshape(T, D)
    tgt = loss_target.reshape(T, D)
    K, Cs = conv_dw.shape
    C = conv_dw_bias.shape[0]
    Wl = lru_conv_b.shape[0]
    K4 = lru_conv_w.shape[0]
    heads, dh, _ = lru_w_a.shape
    per = LANES // dh

    def row(v):
        return v.reshape(1, -1)

    tform = ('ffn1_w_gate', 'ffn1_w_up', 'ffn2_w_gate', 'ffn2_w_up')
    for n in tform:
        W[n], M[n], V[n] = W[n].T, M[n].T, V[n].T
    kp = -(-K // SUBLANES) * SUBLANES
    taps = jnp.concatenate([conv_dw, jnp.zeros((kp - K, Cs), F32), lru_conv_w,
                            jnp.zeros((2 * SUBLANES - K4, Cs), F32)], axis=0)
    idx = jnp.stack([ci, chip]).astype(jnp.int32)
    (wff1,) = _gather_weights([_place_cast([W['ffn1_w_gate'], W['ffn1_w_up'], ffn1_w_down], idx, BF16, "place_ffn1")])
    mixl = [_place_cast([w_in], idx, BF16, "place_w_in"), _place_cast([w_out], idx, BF16, "place_w_out"),
            _place_cast([taps], idx, F32, "place_taps")]
    msend, mrecv, mixl, mtok = _gather_start(mixl, wff1, "gather_mix_start")
    ff2l = _place_cast([W['ffn2_w_gate'], W['ffn2_w_up'], ffn2_w_down], idx, BF16, "place_ffn2")
    fsend, frecv, ff2l, ftok = _gather_start([ff2l], mtok, "gather_ffn2_start")
    wa_bd = _block_diag(lru_w_a, per).astype(BF16)
    wx_bd = _block_diag(lru_w_x, per).astype(BF16)

    x1, a1, b1 = _ffn_fwd(xs, row(ffn1_norm) + ftok[0:1, 0:1], wff1, "ffn1_fwd")
    win, wout, taps = _gather_wait(msend, mrecv, mixl, x1, "gather_mix_wait")
    win, wout, taps = win[0], wout.reshape(-1, D), taps[0]
    conv_w_full = taps[:, :K].transpose(1, 0, 2).reshape(K, N_CHIPS * Cs)
    lru_w4_full = taps[:, kp:kp + K4].transpose(1, 0, 2).reshape(K4, N_CHIPS * Cs)
    z = _mix_in_fwd(x1, row(mix_norm), win)
    u, u1 = _conv_fwd(z, conv_w_full, row(conv_dw_bias), row(conv_ln_g), row(conv_ln_b))
    yr, hs = _lru_fwd(z, 2 * C, lru_w4_full, row(lru_conv_b), wa_bd, row(lru_b_a), wx_bd, row(lru_b_x),
                      row(lru_lambda))
    x2 = _mix_out_fwd(x1, u, yr, wout)
    (wff2,) = _gather_wait(fsend, frecv, ff2l, x2, "gather_ffn2_wait")
    x3, a2, b2 = _ffn_fwd(x2, row(ffn2_norm), wff2, "ffn2_fwd")
    dx3, loss_blk, d_final = _final_loss(x3, row(final_norm), tgt)

    dx2, da2, db2, p2, hb2, dyh2, d_ffn2n = _ffn_bwd_tok(dx3, x2, row(ffn2_norm), a2, b2, wff2, "ffn2_bwd")
    dwg2, dwu2, dwd2 = _ffn_wgrad(hb2, dyh2, da2, db2, p2, ftok, "ffn2_wgrad")
    dcat, dwout = _mix_out_bwd(dx2, u, yr, wout)
    dzc, cst = _conv_bwd(dcat, u1, z, conv_w_full, row(conv_ln_g), row(conv_ln_b))
    dzx, dzg, lst, dwa_bd, dwx_bd = _lru_bwd(dcat, C, hs, z, 2 * C, lru_w4_full, row(lru_conv_b), wa_bd,
                                              row(lru_b_a), wx_bd, row(lru_b_x), row(lru_lambda))
    dx1, dwin, d_mixn = _mix_in_bwd(dzc, dzx, dzg, x1, dx2, row(mix_norm), win)

    early_names = ['w_in', 'w_out', 'ffn2_w_gate', 'ffn2_w_up', 'ffn2_w_down']
    early = [dwin, dwout.reshape(N_CHIPS, -1, D), dwg2, dwu2, dwd2]
    e_parts = [_add_cast(g, o, cidx, "add_cast_" + n)
               for g, o, n in zip(early, _swap_halves_out(early, "swap_halves_early"), early_names)]
    esend, erecv, e_parts, e_lands, etok = _exchange_start(e_parts)

    dx0, da1, db1, p1, hb1, dyh1, d_ffn1n = _ffn_bwd_tok(dx1, xs, row(ffn1_norm) + etok[0:1, 0:1], a1, b1, wff1,
                                                         "ffn1_bwd")

    small_names = ['ffn1_norm', 'mix_norm', 'conv_dw', 'conv_dw_bias', 'conv_ln_g', 'conv_ln_b', 'lru_conv_w',
                   'lru_conv_b', 'lru_w_a', 'lru_b_a', 'lru_w_x', 'lru_b_x', 'lru_lambda', 'ffn2_norm',
                   'final_norm']
    small = {
        'ffn1_norm': d_ffn1n, 'mix_norm': d_mixn, 'conv_dw': cst[:K], 'conv_dw_bias': cst[K + 1],
        'conv_ln_g': cst[K + 2], 'conv_ln_b': cst[K + 3], 'lru_conv_w': lst[:K4], 'lru_conv_b': lst[K4],
        'lru_w_a': _block_diag_take(dwa_bd, per), 'lru_b_a': lst[K4 + 1],
        'lru_w_x': _block_diag_take(dwx_bd, per), 'lru_b_x': lst[K4 + 2], 'lru_lambda': lst[K4 + 3],
        'ffn2_norm': d_ffn2n, 'final_norm': d_final,
    }
    packed, rows = _pack([small[n] for n in small_names])
    ssend, srecv, packed, sslots, stok = _small_start(packed)

    dwg1, dwu1, dwd1 = _ffn_wgrad(hb1, dyh1, da1, db1, p1, stok, "ffn1_wgrad")

    last_names = ['ffn1_w_gate', 'ffn1_w_up', 'ffn1_w_down']
    last = [dwg1, dwu1, dwd1]
    l_parts = [_add_cast(g, o, cidx, "add_cast_" + n)
               for g, o, n in zip(last, _swap_halves_out(last, "swap_halves_last"), last_names)]
    l_slots = _exchange_chips(l_parts)
    e_parts, e_slots = _exchange_wait(esend, erecv, e_parts, e_lands, dwd1)
    big_names = early_names + last_names
    halves = [_sum_slots(p, b, idx, "sum_slots_" + n)
              for p, b, n in zip(list(e_parts) + l_parts, list(e_slots) + list(l_slots), big_names)]
    G = dict(zip(big_names, _share_halves(halves)))

    full_shapes = [(K, C) if n == 'conv_dw' else (K4, Wl) if n == 'lru_conv_w' else W[n].shape for n in small_names]
    packed, sslots = _small_wait(ssend, srecv, packed, sslots, dwd1)
    summed = _sum_devices(packed, sslots, (4 * xi + 2 * yi + ci).astype(jnp.int32).reshape(1))
    for n, gsum in zip(small_names, _unpack(summed, rows, full_shapes)):
        if n == 'conv_dw':
            gsum = lax.dynamic_slice_in_dim(gsum, chip * Cs, Cs, axis=1)
        elif n == 'lru_conv_w':
            gsum = lax.dynamic_slice_in_dim(gsum, chip * lru_conv_w.shape[1], lru_conv_w.shape[1], axis=1)
        G[n] = gsum

    delta, new_m, new_v = {}, {}, {}
    for n in big_names:
        shp = W[n].shape
        g2 = G[n] if G[n].shape == shp else G[n].reshape(shp)
        G[n] = g2
        delta[n], new_m[n], new_v[n] = _adamw(W[n], g2, M[n], V[n], "adamw_" + n)
    pw, prow = _pack([W[n] for n in small_names])
    pg, _ = _pack([G[n] for n in small_names])
    pm, _ = _pack([M[n] for n in small_names])
    pv, _ = _pack([V[n] for n in small_names])
    sd, sm, sv = _adamw(pw, pg, pm, pv, "adamw_small")
    shapes = [W[n].shape for n in small_names]
    for n, a, b, c_ in zip(small_names, _unpack(sd, prow, shapes), _unpack(sm, prow, shapes),
                           _unpack(sv, prow, shapes)):
        delta[n], new_m[n], new_v[n] = a, b, c_

    loss = lax.psum(loss_blk[0, 0], ("x", "y", "c"))
    grad_x = dx0.reshape(x.shape)
    for n in tform:
        G[n], delta[n], new_m[n], new_v[n] = G[n].T, delta[n].T, new_m[n].T, new_v[n].T
    return (loss, grad_x, *[G[n] for n in names], *[delta[n] for n in names],
            *[new_m[n] for n in names], *[new_v[n] for n in names])
```

```python
import functools
import math

import jax
import jax.numpy as jnp
from jax import lax
from jax.experimental import pallas as pl
from jax.experimental.pallas import tpu as pltpu

F32 = jnp.float32
BF16 = jnp.bfloat16
MESH = pl.DeviceIdType.MESH

RMS_EPS = 1e-6
LN_EPS = 1e-5
LRU_C = 8.0
FFN_RES_SCALE = 0.5
ADAM_LR = 0.001
ADAM_B1 = 0.9
ADAM_B2 = 0.999
ADAM_EPS = 1e-08
ADAM_WD = 0.01
ADAM_STEP = 10

LANES = 128
SUBLANES = 8
CONV_HALO = 32
LRU_HALO = 8
ROW_CHUNK = 64
VMEM_LIMIT = 56 * 1024 * 1024
VMEM_LIMIT_BIG = 61 * 1024 * 1024
N_CHIPS = 4
N_DEV = 8
TOK_TILE = 1024
BWD_TILE = 512
FFN_BWD_TILE = 512
BWD_ROWS = 32
FFN_BWD_CHAIN = 256
CONV_TILE = 512
LRU_TILE = 1024


def _dot(a, b):
    return jnp.dot(a, b, preferred_element_type=F32)


def _dot_nt(a, b):
    return lax.dot_general(a, b, (((1,), (1,)), ((), ())), preferred_element_type=F32)


def _dot_tn(a, b):
    return lax.dot_general(a, b, (((0,), (0,)), ((), ())), preferred_element_type=F32)


def _tile(n, pref, mult=SUBLANES):
    for t in range(min(pref, n), 0, -1):
        if n % t == 0 and t % mult == 0:
            return t
    return n


def _params(*sem, vmem=None):
    return pltpu.CompilerParams(dimension_semantics=sem, vmem_limit_bytes=vmem or VMEM_LIMIT)


def _rms_stats(x):
    r = lax.rsqrt(jnp.mean(x * x, axis=-1, keepdims=True) + RMS_EPS)
    return x * r, r


def _rms_bwd(dh, xh, r, g):
    dxh = dh * g
    return r * (dxh - xh * jnp.mean(dxh * xh, axis=-1, keepdims=True))


def _colsum(v):
    return jnp.sum(v, axis=0, keepdims=True)


def _ffn_fwd(x, g, wff, name):
    T, D = x.shape
    ns, fs = wff.shape[1], wff.shape[2]
    tm = _tile(T, TOK_TILE)
    mc = _tile(tm, FFN_BWD_CHAIN, 16)

    def body(x_ref, g_ref, wg_ref, wu_ref, wd_ref, y_ref, a_ref, b_ref, hb_ref, acc_ref):
        j = pl.program_id(1)

        @pl.when(j == 0)
        def _():
            xh, _ = _rms_stats(x_ref[...])
            hb_ref[...] = (xh * g_ref[...]).astype(BF16)
            acc_ref[...] = jnp.zeros_like(acc_ref)

        for q0 in range(0, tm, mc):
            blk = pl.ds(q0, mc)
            hb = hb_ref[blk, :]
            a = _dot_nt(hb, wg_ref[...])
            b = _dot_nt(hb, wu_ref[...])
            a_ref[blk, :] = a.astype(BF16)
            b_ref[blk, :] = b.astype(BF16)
            p = (a * jax.nn.sigmoid(a) * b).astype(BF16)
            acc_ref[blk, :] += _dot(p, wd_ref[...])

        @pl.when(j == ns - 1)
        def _():
            y_ref[...] = x_ref[...] + FFN_RES_SCALE * acc_ref[...]

    def wspec(n):
        return pl.BlockSpec((None, None, fs, D), lambda i, j: (n, j, 0, 0))

    mid = pl.BlockSpec((None, tm, fs), lambda i, j: (j, i, 0))
    return pl.pallas_call(
        body, grid=(T // tm, ns),
        in_specs=[pl.BlockSpec((tm, D), lambda i, j: (i, 0)), pl.BlockSpec((1, D), lambda i, j: (0, 0)),
                  wspec(0), wspec(1), wspec(2)],
        out_specs=[pl.BlockSpec((tm, D), lambda i, j: (i, 0)), mid, mid],
        out_shape=[jax.ShapeDtypeStruct((T, D), F32), jax.ShapeDtypeStruct((ns, T, fs), BF16),
                   jax.ShapeDtypeStruct((ns, T, fs), BF16)],
        scratch_shapes=[pltpu.VMEM((tm, D), BF16), pltpu.VMEM((tm, D), F32)],
        compiler_params=_params("parallel", "arbitrary"), name=name)(x, g, wff, wff, wff)


def _ffn_bwd_tok(dy, x, g, a, b, wff, name):
    T, D = x.shape
    ns, fs = wff.shape[1], wff.shape[2]
    tm = _tile(T, FFN_BWD_TILE)
    rc = _tile(tm, BWD_ROWS)
    mc = _tile(tm, FFN_BWD_CHAIN, rc)

    def body(dy_ref, x_ref, g_ref, a_ref, b_ref, wg_ref, wu_ref, wd_ref,
             dx_ref, da_ref, db_ref, p_ref, hb_ref, dyh_ref, dg_ref, dh_ref, dp_ref):
        i, j = pl.program_id(0), pl.program_id(1)

        @pl.when((i == 0) & (j == 0))
        def _():
            dg_ref[...] = jnp.zeros_like(dg_ref)

        @pl.when(j == 0)
        def _():
            for r0 in range(0, tm, rc):
                rows = pl.ds(r0, rc)
                xh, _ = _rms_stats(x_ref[rows, :])
                hb_ref[rows, :] = (xh * g_ref[...]).astype(BF16)
                dyh_ref[rows, :] = (FFN_RES_SCALE * dy_ref[rows, :]).astype(BF16)
            dh_ref[...] = jnp.zeros_like(dh_ref)

        for q0 in range(0, tm, mc):
            blk = pl.ds(q0, mc)
            dp_ref[blk, :] = _dot_nt(dyh_ref[blk, :], wd_ref[...])
            for r0 in range(q0, q0 + mc, rc):
                rows = pl.ds(r0, rc)
                av = a_ref[rows, :].astype(F32)
                bv = b_ref[rows, :].astype(F32)
                dp = dp_ref[rows, :]
                s = jax.nn.sigmoid(av)
                sl = av * s
                da_ref[rows, :] = (dp * bv * (s * (1.0 + av * (1.0 - s)))).astype(BF16)
                db_ref[rows, :] = (dp * sl).astype(BF16)
                p_ref[rows, :] = (sl * bv).astype(BF16)
            dh_ref[blk, :] += _dot(da_ref[blk, :], wg_ref[...]) + _dot(db_ref[blk, :], wu_ref[...])

        @pl.when(j == ns - 1)
        def _():
            gv = g_ref[...]
            dg = jnp.zeros((1, D), F32)
            for r0 in range(0, tm, rc):
                rows = pl.ds(r0, rc)
                xh, r = _rms_stats(x_ref[rows, :])
                dh = dh_ref[rows, :]
                dx_ref[rows, :] = dy_ref[rows, :] + _rms_bwd(dh, xh, r, gv)
                dg = dg + _colsum(dh * xh)
            dg_ref[...] += dg

    def wspec(n):
        return pl.BlockSpec((None, None, fs, D), lambda i, j: (n, j, 0, 0))

    tok = pl.BlockSpec((tm, D), lambda i, j: (i, 0))
    mid = pl.BlockSpec((None, tm, fs), lambda i, j: (j, i, 0))
    vec = pl.BlockSpec((1, D), lambda i, j: (0, 0))
    return pl.pallas_call(
        body, grid=(T // tm, ns),
        in_specs=[tok, tok, vec, mid, mid, wspec(0), wspec(1), wspec(2)],
        out_specs=[tok, mid, mid, mid, tok, tok, vec],
        out_shape=[jax.ShapeDtypeStruct((T, D), F32),
                   jax.ShapeDtypeStruct((ns, T, fs), BF16), jax.ShapeDtypeStruct((ns, T, fs), BF16),
                   jax.ShapeDtypeStruct((ns, T, fs), BF16),
                   jax.ShapeDtypeStruct((T, D), BF16), jax.ShapeDtypeStruct((T, D), BF16),
                   jax.ShapeDtypeStruct((1, D), F32)],
        scratch_shapes=[pltpu.VMEM((tm, D), F32), pltpu.VMEM((tm, fs), F32)],
        compiler_params=_params("arbitrary", "arbitrary", vmem=VMEM_LIMIT_BIG), name=name)(dy, x, g, a, b, wff, wff, wff)


def _ffn_wgrad(hb, dyh, da, db, p, after, name):
    T, D = hb.shape
    ns, _, fs = da.shape
    tm = _tile(T, TOK_TILE)

    def body(hb_ref, dyh_ref, da_ref, db_ref, p_ref, after_ref, dwg_ref, dwu_ref, dwd_ref):
        @pl.when(pl.program_id(1) == 0)
        def _():
            dwg_ref[...] = jnp.zeros_like(dwg_ref)
            dwu_ref[...] = jnp.zeros_like(dwu_ref)
            dwd_ref[...] = jnp.zeros_like(dwd_ref)

        hbv = hb_ref[...]
        dwg_ref[...] += _dot_tn(da_ref[...], hbv)
        dwu_ref[...] += _dot_tn(db_ref[...], hbv)
        dwd_ref[...] += _dot_tn(p_ref[...], dyh_ref[...])

    tok = pl.BlockSpec((tm, D), lambda j, i: (i, 0))
    mid = pl.BlockSpec((None, tm, fs), lambda j, i: (j, i, 0))
    wsp = pl.BlockSpec((None, fs, D), lambda j, i: (j, 0, 0))
    sds = jax.ShapeDtypeStruct((ns, fs, D), F32)
    return pl.pallas_call(
        body, grid=(ns, T // tm),
        in_specs=[tok, tok, mid, mid, mid, pl.BlockSpec((SUBLANES, LANES), lambda j, i: (0, 0))],
        out_specs=[wsp, wsp, wsp], out_shape=[sds, sds, sds],
        compiler_params=_params("parallel", "arbitrary"), name=name)(hb, dyh, da, db, p, after)


def _mix_in_fwd(x, g, win):
    T, D = x.shape
    ns, ws = win.shape[0], win.shape[2]
    tm = _tile(T, TOK_TILE)

    def body(x_ref, g_ref, w_ref, z_ref):
        xh, _ = _rms_stats(x_ref[...])
        hb = (xh * g_ref[...]).astype(BF16)
        for j in range(ns):
            z_ref[:, pl.ds(j * ws, ws)] = _dot(hb, w_ref[j])

    return pl.pallas_call(
        body, grid=(T // tm,),
        in_specs=[pl.BlockSpec((tm, D), lambda i: (i, 0)), pl.BlockSpec((1, D), lambda i: (0, 0)),
                  pl.BlockSpec((ns, D, ws), lambda i: (0, 0, 0), pipeline_mode=pl.Buffered(1))],
        out_specs=pl.BlockSpec((tm, ns * ws), lambda i: (i, 0)),
        out_shape=jax.ShapeDtypeStruct((T, ns * ws), F32),
        compiler_params=_params("parallel"), name="mix_in_fwd")(x, g, win)


def _tap_sum(buf, w_ref, ntaps, first_row, r0, rows, flip):
    acc = None
    for k in range(ntaps):
        off = (ntaps - 1 - k) if flip else k
        t = buf[pl.ds(first_row + r0 + off, rows), :] * w_ref[pl.ds(k, 1), :]
        acc = t if acc is None else acc + t
    return acc


def _shift_copies(buf, sh, rows):
    for r in range(1, SUBLANES):
        sh[r - 1, pl.ds(0, rows), :] = buf[pl.ds(r, rows), :]


def _tap_rows(buf, sh, off, r0, rows):
    r = off % SUBLANES
    if r == 0:
        return buf[pl.ds(off + r0, rows), :]
    return sh[r - 1, pl.ds(off - r + r0, rows), :]


def _tap_sum_tiles(buf, sh, w_ref, ntaps, first_row, r0, rows, flip):
    acc = None
    for k in range(ntaps):
        off = first_row + ((ntaps - 1 - k) if flip else k)
        t = _tap_rows(buf, sh, off, r0, rows) * w_ref[pl.ds(k, 1), :]
        acc = t if acc is None else acc + t
    return acc


def _conv_fwd(z, w, bias, lng, lnb):
    T = z.shape[0]
    K, C = w.shape
    tm = _tile(T, CONV_TILE, ROW_CHUNK)
    rc = min(ROW_CHUNK, tm)
    srows = tm + CONV_HALO - SUBLANES

    def body(cv_ref, cg_ref, w_ref, b_ref, g_ref, bb_ref, u_ref, u1_ref, buf, sh):
        @pl.when(pl.program_id(0) == 0)
        def _():
            buf[pl.ds(0, CONV_HALO), :] = jnp.zeros((CONV_HALO, C), F32)

        buf[pl.ds(CONV_HALO, tm), :] = cv_ref[...] * jax.nn.sigmoid(cg_ref[...])
        _shift_copies(buf, sh, srows)
        for r0 in range(0, tm, rc):
            u1 = _tap_sum_tiles(buf, sh, w_ref, K, CONV_HALO - (K - 1), r0, rc, False) + b_ref[...]
            u1_ref[pl.ds(r0, rc), :] = u1
            xc = u1 - jnp.mean(u1, axis=-1, keepdims=True)
            xh = xc * lax.rsqrt(jnp.mean(xc * xc, axis=-1, keepdims=True) + LN_EPS)
            u2 = xh * g_ref[...] + bb_ref[...]
            u_ref[pl.ds(r0, rc), :] = (u2 * jax.nn.sigmoid(u2)).astype(BF16)
        buf[pl.ds(0, CONV_HALO), :] = buf[pl.ds(tm, CONV_HALO), :]

    vec = pl.BlockSpec((1, C), lambda i: (0, 0))
    return pl.pallas_call(
        body, grid=(T // tm,),
        in_specs=[pl.BlockSpec((tm, C), lambda i: (i, 0)), pl.BlockSpec((tm, C), lambda i: (i, 1)),
                  pl.BlockSpec((K, C), lambda i: (0, 0)), vec, vec, vec],
        out_specs=[pl.BlockSpec((tm, C), lambda i: (i, 0)), pl.BlockSpec((tm, C), lambda i: (i, 0))],
        out_shape=[jax.ShapeDtypeStruct((T, C), BF16), jax.ShapeDtypeStruct((T, C), F32)],
        scratch_shapes=[pltpu.VMEM((CONV_HALO + tm, C), F32), pltpu.VMEM((SUBLANES - 1, srows, C), F32)],
        compiler_params=_params("arbitrary"), name="conv_fwd")(z, z, w, bias, lng, lnb)


def _conv_bwd(dcat, u1, z, w, lng, lnb):
    T = z.shape[0]
    K, C = w.shape
    tm = _tile(T, CONV_TILE, ROW_CHUNK)
    rc = min(ROW_CHUNK, tm)
    nI = T // tm
    hb = tm // CONV_HALO
    srows = ((K + 4 + SUBLANES - 1) // SUBLANES) * SUBLANES
    shrows = tm + CONV_HALO - SUBLANES

    def body(du_ref, u1_ref, cv_ref, cg_ref, cvp_ref, cgp_ref, w_ref, g_ref, bb_ref,
             dz_ref, st_ref, u0buf, d1buf, ush, dsh):
        i = pl.program_id(0)
        ti = nI - 1 - i

        @pl.when(i == 0)
        def _():
            st_ref[...] = jnp.zeros_like(st_ref)
            d1buf[pl.ds(tm, CONV_HALO), :] = jnp.zeros((CONV_HALO, C), F32)

        prev = cvp_ref[...] * jax.nn.sigmoid(cgp_ref[...])
        u0buf[pl.ds(0, CONV_HALO), :] = jnp.where(ti == 0, 0.0, prev)
        u0buf[pl.ds(CONV_HALO, tm), :] = cv_ref[...] * jax.nn.sigmoid(cg_ref[...])

        gv = g_ref[...]
        dbias = jnp.zeros((1, C), F32)
        dgain = jnp.zeros((1, C), F32)
        dlnb = jnp.zeros((1, C), F32)
        for r0 in range(0, tm, rc):
            u1 = u1_ref[pl.ds(r0, rc), :]
            xc = u1 - jnp.mean(u1, axis=-1, keepdims=True)
            rstd = lax.rsqrt(jnp.mean(xc * xc, axis=-1, keepdims=True) + LN_EPS)
            xh = xc * rstd
            u2 = xh * gv + bb_ref[...]
            s = jax.nn.sigmoid(u2)
            du2 = du_ref[pl.ds(r0, rc), :] * (s * (1.0 + u2 * (1.0 - s)))
            dgain = dgain + _colsum(du2 * xh)
            dlnb = dlnb + _colsum(du2)
            dxh = du2 * gv
            du1 = rstd * (dxh - jnp.mean(dxh, axis=-1, keepdims=True)
                          - xh * jnp.mean(dxh * xh, axis=-1, keepdims=True))
            dbias = dbias + _colsum(du1)
            d1buf[pl.ds(r0, rc), :] = du1
        st_ref[pl.ds(K + 1, 1), :] += dbias
        st_ref[pl.ds(K + 2, 1), :] += dgain
        st_ref[pl.ds(K + 3, 1), :] += dlnb

        _shift_copies(u0buf, ush, shrows)
        _shift_copies(d1buf, dsh, shrows)
        for k in range(K):
            acc = jnp.zeros((SUBLANES, C), F32)
            for r0 in range(0, tm, rc):
                prod = d1buf[pl.ds(r0, rc), :] * _tap_rows(u0buf, ush, CONV_HALO - (K - 1) + k, r0, rc)
                acc = acc + jnp.sum(prod.reshape(rc // SUBLANES, SUBLANES, C), axis=0)
            st_ref[pl.ds(k, 1), :] += _colsum(acc)

        for r0 in range(0, tm, rc):
            du0 = _tap_sum_tiles(d1buf, dsh, w_ref, K, 0, r0, rc, True)
            cv = cv_ref[pl.ds(r0, rc), :]
            sg = jax.nn.sigmoid(cg_ref[pl.ds(r0, rc), :])
            dz_ref[pl.ds(r0, rc), pl.ds(0, C)] = (du0 * sg).astype(BF16)
            dz_ref[pl.ds(r0, rc), pl.ds(C, C)] = (du0 * cv * sg * (1.0 - sg)).astype(BF16)
        d1buf[pl.ds(tm, CONV_HALO), :] = d1buf[pl.ds(0, CONV_HALO), :]

    def rev(col):
        return lambda i: (nI - 1 - i, col)

    def rev_prev(col):
        return lambda i: (jnp.maximum((nI - 1 - i) * hb - 1, 0), col)

    vec = pl.BlockSpec((1, C), lambda i: (0, 0))
    return pl.pallas_call(
        body, grid=(nI,),
        in_specs=[pl.BlockSpec((tm, C), rev(0)), pl.BlockSpec((tm, C), rev(0)),
                  pl.BlockSpec((tm, C), rev(0)), pl.BlockSpec((tm, C), rev(1)),
                  pl.BlockSpec((CONV_HALO, C), rev_prev(0)), pl.BlockSpec((CONV_HALO, C), rev_prev(1)),
                  pl.BlockSpec((K, C), lambda i: (0, 0)), vec, vec],
        out_specs=[pl.BlockSpec((tm, 2 * C), rev(0)), pl.BlockSpec((srows, C), lambda i: (0, 0))],
        out_shape=[jax.ShapeDtypeStruct((T, 2 * C), BF16), jax.ShapeDtypeStruct((srows, C), F32)],
        scratch_shapes=[pltpu.VMEM((CONV_HALO + tm, C), F32), pltpu.VMEM((tm + CONV_HALO, C), F32),
                        pltpu.VMEM((SUBLANES - 1, shrows, C), F32), pltpu.VMEM((SUBLANES - 1, shrows, C), F32)],
        compiler_params=_params("arbitrary"), name="conv_bwd")(dcat, u1, z, z, z, z, w, lng, lnb)


def _softplus(v):
    return jnp.maximum(v, 0.0) + jnp.log(1.0 + jnp.exp(-jnp.abs(v)))


def _gelu(v):
    c = math.sqrt(2.0 / math.pi)
    t = jnp.tanh(c * (v + 0.044715 * v * v * v))
    gl = 0.5 * v * (1.0 + t)
    dgl = 0.5 * (1.0 + t) + 0.5 * v * (1.0 - t * t) * c * (1.0 + 3.0 * 0.044715 * v * v)
    return gl, dgl


def _lru_gates(xr, wa, ba, wx, bx, lam):
    xb = xr.astype(BF16)
    r = jax.nn.sigmoid(_dot(xb, wa) + ba)
    ig = jax.nn.sigmoid(_dot(xb, wx) + bx)
    sp = _softplus(-lam)
    log_a = -LRU_C * r * sp
    a = jnp.exp(log_a)
    y = 2.0 * log_a
    series = -(y * (1.0 + y * (0.5 + y * (1.0 / 6.0 + y * (1.0 / 24.0)))))
    mult = jnp.sqrt(jnp.where(y > -0.02, series, 1.0 - jnp.exp(y)))
    return a, mult, r, ig, sp


def _scan_tile(a_s, b_s, carry, seg, reverse):
    def step(n, hp):
        hl, pr = hp
        k = (seg - 1 - n) if reverse else n
        rows = pl.ds(k, SUBLANES, stride=seg)
        av = a_s[rows, :]
        hl = av * hl + b_s[rows, :]
        pr = av * pr
        b_s[rows, :] = hl
        a_s[rows, :] = pr
        return hl, pr

    hl, pr = lax.fori_loop(0, seg, step, (jnp.zeros((SUBLANES, LANES), F32), jnp.ones((SUBLANES, LANES), F32)),
                           unroll=min(8, seg))
    cs = [None] * SUBLANES
    c = carry
    for s in (range(SUBLANES - 1, -1, -1) if reverse else range(SUBLANES)):
        cs[s] = c
        c = hl[s:s + 1, :] + pr[s:s + 1, :] * c
    return cs, c


def _lru_fwd(z, col0, w4, b4, wa, ba, wx, bx, lam):
    T = z.shape[0]
    K4, W = w4.shape
    nC = W // LANES
    tm = _tile(T, LRU_TILE, SUBLANES * SUBLANES)
    seg = tm // SUBLANES
    cx, cg = col0 // LANES, (col0 + W) // LANES

    def body(rx_ref, rg_ref, w4_ref, b4_ref, wa_ref, ba_ref, wx_ref, bx_ref, lam_ref,
             yr_ref, hs_ref, xbuf, a_s, b_s, hc):
        @pl.when(pl.program_id(1) == 0)
        def _():
            xbuf[pl.ds(0, LRU_HALO), :] = jnp.zeros((LRU_HALO, LANES), F32)
            hc[...] = jnp.zeros_like(hc)

        xbuf[pl.ds(LRU_HALO, tm), :] = rx_ref[...]
        xr = _tap_sum(xbuf, w4_ref, K4, LRU_HALO - (K4 - 1), 0, tm, False) + b4_ref[...]
        a, mult, _, ig, _ = _lru_gates(xr, wa_ref[...], ba_ref[...], wx_ref[...], bx_ref[...], lam_ref[...])
        a_s[...] = a
        b_s[...] = mult * ig * xr
        cs, cout = _scan_tile(a_s, b_s, hc[pl.ds(0, 1), :], seg, False)
        hc[pl.ds(0, 1), :] = cout
        for s in range(SUBLANES):
            rows = pl.ds(s * seg, seg)
            h = b_s[rows, :] + a_s[rows, :] * cs[s]
            hs_ref[rows, :] = h
            gl, _ = _gelu(rg_ref[rows, :])
            yr_ref[rows, :] = (h * gl).astype(BF16)
        xbuf[pl.ds(0, LRU_HALO), :] = xbuf[pl.ds(tm, LRU_HALO), :]

    vec = pl.BlockSpec((1, LANES), lambda c, i: (0, c))
    mat = pl.BlockSpec((None, LANES, LANES), lambda c, i: (c, 0, 0))
    return pl.pallas_call(
        body, grid=(nC, T // tm),
        in_specs=[pl.BlockSpec((tm, LANES), lambda c, i: (i, cx + c)),
                  pl.BlockSpec((tm, LANES), lambda c, i: (i, cg + c)),
                  pl.BlockSpec((K4, LANES), lambda c, i: (0, c)), vec, mat, vec, mat, vec, vec],
        out_specs=[pl.BlockSpec((tm, LANES), lambda c, i: (i, c)), pl.BlockSpec((tm, LANES), lambda c, i: (i, c))],
        out_shape=[jax.ShapeDtypeStruct((T, W), BF16), jax.ShapeDtypeStruct((T, W), F32)],
        scratch_shapes=[pltpu.VMEM((LRU_HALO + tm, LANES), F32), pltpu.VMEM((tm, LANES), F32),
                        pltpu.VMEM((tm, LANES), F32), pltpu.VMEM((SUBLANES, LANES), F32)],
        compiler_params=_params("parallel", "arbitrary"), name="lru_fwd")(z, z, w4, b4, wa, ba, wx, bx, lam)


def _lru_bwd(dcat, dcol0, hs, z, col0, w4, b4, wa, ba, wx, bx, lam):
    T = z.shape[0]
    K4, W = w4.shape
    assert K4 + 4 == SUBLANES
    nC = W // LANES
    tm = _tile(T, LRU_TILE, SUBLANES * SUBLANES)
    seg = tm // SUBLANES
    nI = T // tm
    hb = tm // LRU_HALO
    cx, cg, cd = col0 // LANES, (col0 + W) // LANES, dcol0 // LANES

    def body(dyr_ref, hs_ref, hsp_ref, rx_ref, rxp_ref, rg_ref, w4_ref, b4_ref, wa_ref, ba_ref, wx_ref, bx_ref,
             lam_ref, dzx_ref, dzg_ref, st_ref, dwa_ref, dwx_ref, xbuf, hbuf, abuf, a_s, b_s, dbuf, gc, anc):
        i = pl.program_id(1)
        ti = nI - 1 - i

        @pl.when(i == 0)
        def _():
            st_ref[...] = jnp.zeros_like(st_ref)
            dwa_ref[...] = jnp.zeros_like(dwa_ref)
            dwx_ref[...] = jnp.zeros_like(dwx_ref)
            gc[...] = jnp.zeros_like(gc)
            anc[...] = jnp.zeros_like(anc)
            dbuf[pl.ds(tm, LRU_HALO), :] = jnp.zeros((LRU_HALO, LANES), F32)

        xbuf[pl.ds(0, LRU_HALO), :] = jnp.where(ti == 0, 0.0, rxp_ref[...])
        xbuf[pl.ds(LRU_HALO, tm), :] = rx_ref[...]
        hbuf[pl.ds(0, LRU_HALO), :] = jnp.where(ti == 0, 0.0, hsp_ref[...])
        hbuf[pl.ds(LRU_HALO, tm), :] = hs_ref[...]

        wa, wx = wa_ref[...], wx_ref[...]
        lam_v = lam_ref[...]
        xr = _tap_sum(xbuf, w4_ref, K4, LRU_HALO - (K4 - 1), 0, tm, False) + b4_ref[...]
        a, mult, r, ig, sp = _lru_gates(xr, wa, ba_ref[...], wx, bx_ref[...], lam_v)

        dyr = dyr_ref[...]
        gl, dgl = _gelu(rg_ref[...])
        dzg_ref[...] = (dyr * hs_ref[...] * dgl).astype(BF16)

        abuf[pl.ds(0, tm), :] = a
        abuf[pl.ds(tm, LRU_HALO), :] = anc[...]
        a_s[...] = abuf[pl.ds(1, tm), :]
        b_s[...] = dyr * gl
        cs, cout = _scan_tile(a_s, b_s, gc[pl.ds(0, 1), :], seg, True)
        gc[pl.ds(0, 1), :] = cout
        anc[pl.ds(0, 1), :] = a[0:1, :]
        for s in range(SUBLANES):
            rows = pl.ds(s * seg, seg)
            b_s[rows, :] = b_s[rows, :] + a_s[rows, :] * cs[s]
        g = b_s[...]

        d_a = g * hbuf[pl.ds(LRU_HALO - 1, tm), :]
        gx_ = g * xr
        d_log_a = d_a * a - (gx_ * ig) * (a * a / mult)
        dga = (d_log_a * (-LRU_C * sp)) * r * (1.0 - r)
        dgx = (gx_ * mult) * ig * (1.0 - ig)
        dga_b, dgx_b = dga.astype(BF16), dgx.astype(BF16)
        dxr = g * mult * ig + _dot_nt(dga_b, wa) + _dot_nt(dgx_b, wx)
        xb = xr.astype(BF16)
        dwa_ref[...] += _dot_tn(xb, dga_b)
        dwx_ref[...] += _dot_tn(xb, dgx_b)
        st_ref[pl.ds(K4, 1), :] += _colsum(dxr)
        st_ref[pl.ds(K4 + 1, 1), :] += _colsum(dga)
        st_ref[pl.ds(K4 + 2, 1), :] += _colsum(dgx)
        st_ref[pl.ds(K4 + 3, 1), :] += _colsum(d_log_a * (-LRU_C * r)) * (-jax.nn.sigmoid(-lam_v))

        dbuf[pl.ds(0, tm), :] = dxr
        for k in range(K4):
            st_ref[pl.ds(k, 1), :] += _colsum(dxr * xbuf[pl.ds(LRU_HALO - (K4 - 1) + k, tm), :])
        dzx_ref[...] = _tap_sum(dbuf, w4_ref, K4, 0, 0, tm, True).astype(BF16)
        dbuf[pl.ds(tm, LRU_HALO), :] = dbuf[pl.ds(0, LRU_HALO), :]

    def rev(col):
        return lambda c, i: (nI - 1 - i, col + c)

    def rev_prev(col):
        return lambda c, i: (jnp.maximum((nI - 1 - i) * hb - 1, 0), col + c)

    vec = pl.BlockSpec((1, LANES), lambda c, i: (0, c))
    mat = pl.BlockSpec((None, LANES, LANES), lambda c, i: (c, 0, 0))
    big = pltpu.VMEM((tm, LANES), F32)
    halo = pltpu.VMEM((tm + LRU_HALO, LANES), F32)
    return pl.pallas_call(
        body, grid=(nC, nI),
        in_specs=[pl.BlockSpec((tm, LANES), rev(cd)),
                  pl.BlockSpec((tm, LANES), rev(0)), pl.BlockSpec((LRU_HALO, LANES), rev_prev(0)),
                  pl.BlockSpec((tm, LANES), rev(cx)), pl.BlockSpec((LRU_HALO, LANES), rev_prev(cx)),
                  pl.BlockSpec((tm, LANES), rev(cg)),
                  pl.BlockSpec((K4, LANES), lambda c, i: (0, c)), vec, mat, vec, mat, vec, vec],
        out_specs=[pl.BlockSpec((tm, LANES), rev(0)), pl.BlockSpec((tm, LANES), rev(0)),
                   pl.BlockSpec((SUBLANES, LANES), lambda c, i: (0, c)), mat, mat],
        out_shape=[jax.ShapeDtypeStruct((T, W), BF16), jax.ShapeDtypeStruct((T, W), BF16),
                   jax.ShapeDtypeStruct((SUBLANES, W), F32),
                   jax.ShapeDtypeStruct((nC, LANES, LANES), F32), jax.ShapeDtypeStruct((nC, LANES, LANES), F32)],
        scratch_shapes=[halo, halo, halo, big, big, halo,
                        pltpu.VMEM((SUBLANES, LANES), F32), pltpu.VMEM((SUBLANES, LANES), F32)],
        compiler_params=_params("parallel", "arbitrary"), name="lru_bwd")(
            dcat, hs, hs, z, z, z, w4, b4, wa, ba, wx, bx, lam)


def _mix_out_fwd(x, u, yr, wout):
    T, D = x.shape
    C, W = u.shape[1], yr.shape[1]
    tm = _tile(T, TOK_TILE)

    def body(x_ref, u_ref, yr_ref, w_ref, y_ref):
        y_ref[...] = (x_ref[...] + _dot(u_ref[...], w_ref[pl.ds(0, C), :])
                      + _dot(yr_ref[...], w_ref[pl.ds(C, W), :]))

    return pl.pallas_call(
        body, grid=(T // tm,),
        in_specs=[pl.BlockSpec((tm, D), lambda i: (i, 0)), pl.BlockSpec((tm, C), lambda i: (i, 0)),
                  pl.BlockSpec((tm, W), lambda i: (i, 0)),
                  pl.BlockSpec((C + W, D), lambda i: (0, 0), pipeline_mode=pl.Buffered(1))],
        out_specs=pl.BlockSpec((tm, D), lambda i: (i, 0)),
        out_shape=jax.ShapeDtypeStruct((T, D), F32),
        compiler_params=_params("parallel"), name="mix_out_fwd")(x, u, yr, wout)


def _mix_out_bwd(dy, u, yr, wout):
    T, D = dy.shape
    C, W = u.shape[1], yr.shape[1]
    tm = _tile(T, BWD_TILE)

    def body(dy_ref, u_ref, yr_ref, w_ref, dcat_ref, dw_ref):
        @pl.when(pl.program_id(0) == 0)
        def _():
            dw_ref[...] = jnp.zeros_like(dw_ref)

        dyb = dy_ref[...].astype(BF16)
        dcat_ref[...] = _dot_nt(dyb, w_ref[...])
        dw_ref[pl.ds(0, C), :] += _dot_tn(u_ref[...], dyb)
        dw_ref[pl.ds(C, W), :] += _dot_tn(yr_ref[...], dyb)

    return pl.pallas_call(
        body, grid=(T // tm,),
        in_specs=[pl.BlockSpec((tm, D), lambda i: (i, 0)), pl.BlockSpec((tm, C), lambda i: (i, 0)),
                  pl.BlockSpec((tm, W), lambda i: (i, 0)),
                  pl.BlockSpec((C + W, D), lambda i: (0, 0), pipeline_mode=pl.Buffered(1))],
        out_specs=[pl.BlockSpec((tm, C + W), lambda i: (i, 0)), pl.BlockSpec((C + W, D), lambda i: (0, 0))],
        out_shape=[jax.ShapeDtypeStruct((T, C + W), F32), jax.ShapeDtypeStruct((C + W, D), F32)],
        compiler_params=_params("arbitrary"), name="mix_out_bwd")(dy, u, yr, wout)


def _mix_in_bwd(dzc, dzx, dzg, x, dy, g, win):
    T, D = x.shape
    ns, ws = win.shape[0], win.shape[2]
    tm = _tile(T, BWD_TILE)
    parts = []
    for j in range(ns):
        lo = j * ws
        if lo < dzc.shape[1]:
            parts.append((0, lo))
        elif lo < dzc.shape[1] + dzx.shape[1]:
            parts.append((1, lo - dzc.shape[1]))
        else:
            parts.append((2, lo - dzc.shape[1] - dzx.shape[1]))

    def body(dzc_ref, dzx_ref, dzg_ref, x_ref, dy_ref, g_ref, w_ref, dx_ref, dw_ref, dg_ref):
        @pl.when(pl.program_id(0) == 0)
        def _():
            dw_ref[...] = jnp.zeros_like(dw_ref)
            dg_ref[...] = jnp.zeros_like(dg_ref)

        xh, r = _rms_stats(x_ref[...])
        gv = g_ref[...]
        hb = (xh * gv).astype(BF16)
        srcs = (dzc_ref, dzx_ref, dzg_ref)
        dh = jnp.zeros((tm, D), F32)
        for j, (si, off) in enumerate(parts):
            dzj = srcs[si][:, pl.ds(off, ws)]
            dh = dh + _dot_nt(dzj, w_ref[j])
            dw_ref[j] += _dot_tn(hb, dzj)
        dx_ref[...] = dy_ref[...] + _rms_bwd(dh, xh, r, gv)
        dg_ref[...] += _colsum(dh * xh)

    def tok(n):
        return pl.BlockSpec((tm, n), lambda i: (i, 0))

    vec = pl.BlockSpec((1, D), lambda i: (0, 0))
    return pl.pallas_call(
        body, grid=(T // tm,),
        in_specs=[tok(dzc.shape[1]), tok(dzx.shape[1]), tok(dzg.shape[1]), tok(D), tok(D), vec,
                  pl.BlockSpec((ns, D, ws), lambda i: (0, 0, 0), pipeline_mode=pl.Buffered(1))],
        out_specs=[tok(D), pl.BlockSpec((ns, D, ws), lambda i: (0, 0, 0)), vec],
        out_shape=[jax.ShapeDtypeStruct((T, D), F32), jax.ShapeDtypeStruct((ns, D, ws), F32),
                   jax.ShapeDtypeStruct((1, D), F32)],
        compiler_params=_params("arbitrary"), name="mix_in_bwd")(dzc, dzx, dzg, x, dy, g, win)


def _final_loss(x, g, tgt):
    T, D = x.shape
    tm = _tile(T, TOK_TILE)

    def body(x_ref, g_ref, t_ref, dx_ref, loss_ref, dg_ref):
        @pl.when(pl.program_id(0) == 0)
        def _():
            loss_ref[...] = jnp.zeros_like(loss_ref)
            dg_ref[...] = jnp.zeros_like(dg_ref)

        xh, r = _rms_stats(x_ref[...])
        gv = g_ref[...]
        e = xh * gv - t_ref[...]
        loss_ref[...] += 0.5 * jnp.sum(jnp.mean(e * e, axis=-1, keepdims=True))
        dy = e * (1.0 / D)
        dg_ref[...] += _colsum(dy * xh)
        dx_ref[...] = _rms_bwd(dy, xh, r, gv)

    tok = pl.BlockSpec((tm, D), lambda i: (i, 0))
    vec = pl.BlockSpec((1, D), lambda i: (0, 0))
    return pl.pallas_call(
        body, grid=(T // tm,),
        in_specs=[tok, vec, tok],
        out_specs=[tok, pl.BlockSpec((SUBLANES, LANES), lambda i: (0, 0)), vec],
        out_shape=[jax.ShapeDtypeStruct((T, D), F32), jax.ShapeDtypeStruct((SUBLANES, LANES), F32),
                   jax.ShapeDtypeStruct((1, D), F32)],
        compiler_params=_params("arbitrary"), name="final_loss")(x, g, tgt)


def _adamw(w, g, m, v, name):
    R, Cc = w.shape
    tr = _tile(R, max(SUBLANES, (1 << 19) // Cc))
    c1 = 1.0 - ADAM_B1 ** ADAM_STEP
    c2 = 1.0 - ADAM_B2 ** ADAM_STEP

    def body(w_ref, g_ref, m_ref, v_ref, d_ref, nm_ref, nv_ref):
        gv = g_ref[...]
        nm = ADAM_B1 * m_ref[...] + (1.0 - ADAM_B1) * gv
        nv = ADAM_B2 * v_ref[...] + (1.0 - ADAM_B2) * (gv * gv)
        nm_ref[...] = nm
        nv_ref[...] = nv
        d_ref[...] = -ADAM_LR * ((nm / c1) / (jnp.sqrt(nv / c2) + ADAM_EPS) + ADAM_WD * w_ref[...])

    blk = pl.BlockSpec((tr, Cc), lambda i: (i, 0))
    sds = jax.ShapeDtypeStruct((R, Cc), F32)
    return pl.pallas_call(
        body, grid=(R // tr,), in_specs=[blk] * 4, out_specs=[blk] * 3, out_shape=[sds] * 3,
        compiler_params=_params("parallel"), name=name)(w, g, m, v)


def _here():
    return lax.axis_index("x"), lax.axis_index("y"), lax.axis_index("c")


def _chip_at(x, y, m):
    return x ^ (m >> 1), y ^ (m & 1)


ANY = pl.BlockSpec(memory_space=pl.ANY)


def _place_cast(srcs, idx, dtype, name):
    n = len(srcs)
    R, Cc = srcs[0].shape
    tr = _tile(R, max(16, (1 << 18) // Cc), 16)

    def body(i_ref, *refs):
        o_ref = refs[n]
        for k in range(n):
            o_ref[k] = refs[k][...].astype(dtype)

    blk = pl.BlockSpec((tr, Cc), lambda i, s: (i, 0))
    return pl.pallas_call(
        body,
        grid_spec=pltpu.PrefetchScalarGridSpec(
            num_scalar_prefetch=1, grid=(R // tr,), in_specs=[blk] * n,
            out_specs=pl.BlockSpec((n, None, tr, Cc), lambda i, s: (0, s[1], i, 0))),
        out_shape=jax.ShapeDtypeStruct((n, N_CHIPS, R, Cc), dtype),
        compiler_params=_params("parallel"), name=name)(idx, *srcs)


def _gather_weights(lands):
    n = len(lands)

    def body(*refs):
        outs = refs[n:2 * n]
        send1, recv1, send2, recv2 = refs[2 * n:]
        x, y, c = _here()
        own = 2 * x + y

        def half(ref, chip, cc):
            rh = ref.shape[-2] // 2
            lead = (slice(None),) * (len(ref.shape) - 3)
            return ref.at[lead + (chip, pl.ds(cc * rh, rh), slice(None))]

        first = []
        for k in range(n):
            for m in (1, 2, 3):
                px, py = _chip_at(x, y, m)
                cp = pltpu.make_async_remote_copy(
                    src_ref=half(outs[k], own, c), dst_ref=half(outs[k], own, c),
                    send_sem=send1.at[k, m - 1], recv_sem=recv1.at[k, m - 1],
                    device_id=(px, py, c), device_id_type=MESH)
                cp.start()
                first.append(cp)

        passed = []
        for k in range(n):
            for m in (1, 2, 3):
                px, py = _chip_at(x, y, m)
                peer = 2 * px + py
                got = half(outs[k], peer, c)
                pltpu.make_async_remote_copy(
                    src_ref=got, dst_ref=got, send_sem=send1.at[k, m - 1], recv_sem=recv1.at[k, m - 1],
                    device_id=(px, py, c), device_id_type=MESH).wait_recv()
                cp = pltpu.make_async_remote_copy(
                    src_ref=got, dst_ref=got, send_sem=send2.at[k, m - 1], recv_sem=recv2.at[k, m - 1],
                    device_id=(x, y, 1 - c), device_id_type=MESH)
                cp.start()
                passed.append(cp)

        for k in range(n):
            for m in (1, 2, 3):
                px, py = _chip_at(x, y, m)
                other = half(outs[k], 2 * px + py, 1 - c)
                pltpu.make_async_remote_copy(
                    src_ref=other, dst_ref=other, send_sem=send2.at[k, m - 1], recv_sem=recv2.at[k, m - 1],
                    device_id=(x, y, 1 - c), device_id_type=MESH).wait_recv()
        for cp in first + passed:
            cp.wait_send()

    return pl.pallas_call(
        body, in_specs=[ANY] * n, out_specs=[ANY] * n,
        out_shape=[jax.ShapeDtypeStruct(a.shape, a.dtype) for a in lands],
        input_output_aliases={k: k for k in range(n)},
        scratch_shapes=[pltpu.SemaphoreType.DMA((n, 3)), pltpu.SemaphoreType.DMA((n, 3)),
                        pltpu.SemaphoreType.DMA((n, 3)), pltpu.SemaphoreType.DMA((n, 3))],
        name="gather_weights")(*lands)


HBM = pl.BlockSpec(memory_space=pltpu.HBM)
SEM = pl.BlockSpec(memory_space=pltpu.SEMAPHORE)
EFFECT = pltpu.SideEffectType.DATAFLOW_SIDE_EFFECTING


def _in_hbm(a):
    return pltpu.with_memory_space_constraint(a, pltpu.HBM)


def _gather_copies(land_refs, send, recv):
    x, y, c = _here()
    own = 2 * x + y
    cps = []
    for k in range(len(land_refs)):
        lead = (slice(None),) * (len(land_refs[k].shape) - 3)
        mine = land_refs[k].at[lead + (own,)]
        for m in (1, 2, 3):
            px, py = _chip_at(x, y, m)
            cps.append(pltpu.make_async_remote_copy(
                src_ref=mine, dst_ref=mine, send_sem=send.at[3 * k + m - 1], recv_sem=recv.at[3 * k + m - 1],
                device_id=(px, py, c), device_id_type=MESH))
    return cps


def _gather_start(lands, after, name):
    n = len(lands)

    def body(*refs):
        lz = refs[:n]
        send, recv = refs[n + 1], refs[n + 2]
        token = refs[-1]
        for cp in _gather_copies(lz, send, recv):
            cp.start()
        token[...] = jnp.zeros_like(token)

    hbm = [pltpu.HBM(a.shape, a.dtype) for a in lands]
    outs = pl.pallas_call(
        body, name=name,
        in_specs=[HBM] * n + [ANY],
        out_specs=[SEM, SEM] + [HBM] * n + [pl.BlockSpec(memory_space=pltpu.VMEM)],
        out_shape=[pltpu.SemaphoreType.DMA((3 * n,)), pltpu.SemaphoreType.DMA((3 * n,))] + hbm
        + [jax.ShapeDtypeStruct((SUBLANES, LANES), F32)],
        input_output_aliases={k: 2 + k for k in range(n)},
        compiler_params=pltpu.CompilerParams(has_side_effects=EFFECT),
    )(*[_in_hbm(a) for a in lands], after)
    return outs[0], outs[1], outs[2:2 + n], outs[-1]


def _gather_wait(send, recv, lands, after, name):
    n = len(lands)

    def body(*refs):
        lz = refs[:n]
        send_r, recv_r = refs[n], refs[n + 1]
        for cp in _gather_copies(lz, send_r, recv_r):
            cp.wait_send()
            cp.wait_recv()

    hbm = [pltpu.HBM(a.shape, a.dtype) for a in lands]
    return pl.pallas_call(
        body, name=name,
        in_specs=[HBM] * n + [SEM, SEM, ANY],
        out_specs=[HBM] * n, out_shape=hbm,
        input_output_aliases={k: k for k in range(n)},
        compiler_params=pltpu.CompilerParams(has_side_effects=EFFECT),
    )(*lands, send, recv, after)


def _exchange_copies(part_refs, slot_refs, send, recv):
    x, y, c = _here()
    cps = []
    for k in range(len(part_refs)):
        for m in (1, 2, 3):
            px, py = _chip_at(x, y, m)
            cps.append(pltpu.make_async_remote_copy(
                src_ref=part_refs[k].at[2 * px + py], dst_ref=slot_refs[k].at[m - 1],
                send_sem=send.at[3 * k + m - 1], recv_sem=recv.at[3 * k + m - 1],
                device_id=(px, py, c), device_id_type=MESH))
    return cps


def _exchange_start(parts, name):
    n = len(parts)
    lands = [lax.empty((N_CHIPS - 1,) + p.shape[1:], p.dtype) for p in parts]

    def body(*refs):
        ins, lz = refs[:n], refs[n:2 * n]
        send, recv = refs[2 * n], refs[2 * n + 1]
        token = refs[-1]
        for cp in _exchange_copies(ins, lz, send, recv):
            cp.start()
        token[...] = jnp.zeros_like(token)

    hbm = [pltpu.HBM(a.shape, a.dtype) for a in list(parts) + lands]
    outs = pl.pallas_call(
        body, name=name,
        in_specs=[HBM] * (2 * n),
        out_specs=[SEM, SEM] + [HBM] * (2 * n) + [pl.BlockSpec(memory_space=pltpu.VMEM)],
        out_shape=[pltpu.SemaphoreType.DMA((3 * n,)), pltpu.SemaphoreType.DMA((3 * n,))] + hbm
        + [jax.ShapeDtypeStruct((SUBLANES, LANES), F32)],
        input_output_aliases={k: 2 + k for k in range(2 * n)},
        compiler_params=pltpu.CompilerParams(has_side_effects=EFFECT),
    )(*[_in_hbm(a) for a in parts], *[_in_hbm(a) for a in lands])
    return outs[0], outs[1], outs[2:2 + n], outs[2 + n:2 + 2 * n], outs[-1]


def _exchange_wait(send, recv, parts, lands, after, name):
    n = len(parts)

    def body(*refs):
        ins, lz = refs[:n], refs[n:2 * n]
        send_r, recv_r = refs[2 * n], refs[2 * n + 1]
        for cp in _exchange_copies(ins, lz, send_r, recv_r):
            cp.wait_send()
            cp.wait_recv()

    hbm = [pltpu.HBM(a.shape, a.dtype) for a in list(parts) + list(lands)]
    outs = pl.pallas_call(
        body, name=name,
        in_specs=[HBM] * (2 * n) + [SEM, SEM, ANY],
        out_specs=[HBM] * (2 * n), out_shape=hbm,
        input_output_aliases={k: k for k in range(2 * n)},
        compiler_params=pltpu.CompilerParams(has_side_effects=EFFECT),
    )(*parts, *lands, send, recv, after)
    return outs[:n], outs[n:]


def _swap_halves_out(grads, name):
    n = len(grads)
    out_shapes = [jax.ShapeDtypeStruct((g.shape[0], g.shape[1] // 2, g.shape[2]), g.dtype) for g in grads]

    def body(*refs):
        ins, outs = refs[:n], refs[n:2 * n]
        send, recv = refs[2 * n:]
        x, y, c = _here()
        cps = []
        for k in range(n):
            rh = ins[k].shape[1] // 2
            cp = pltpu.make_async_remote_copy(
                src_ref=ins[k].at[:, pl.ds((1 - c) * rh, rh), :], dst_ref=outs[k],
                send_sem=send.at[k], recv_sem=recv.at[k], device_id=(x, y, 1 - c), device_id_type=MESH)
            cp.start()
            cps.append(cp)
        for cp in cps:
            cp.wait()

    return pl.pallas_call(
        body, in_specs=[ANY] * n, out_specs=[ANY] * n, out_shape=out_shapes,
        scratch_shapes=[pltpu.SemaphoreType.DMA((n,)), pltpu.SemaphoreType.DMA((n,))],
        name=name)(*grads)


def _swap_copies(grad_refs, land_refs, send, recv):
    x, y, c = _here()
    cps = []
    for k in range(len(grad_refs)):
        rh = grad_refs[k].shape[1] // 2
        cps.append(pltpu.make_async_remote_copy(
            src_ref=grad_refs[k].at[:, pl.ds((1 - c) * rh, rh), :], dst_ref=land_refs[k],
            send_sem=send.at[k], recv_sem=recv.at[k], device_id=(x, y, 1 - c), device_id_type=MESH))
    return cps


def _swap_start(grads, name):
    n = len(grads)
    lands = [lax.empty((g.shape[0], g.shape[1] // 2, g.shape[2]), g.dtype) for g in grads]

    def body(*refs):
        ins, lz = refs[:n], refs[n:2 * n]
        send, recv = refs[2 * n], refs[2 * n + 1]
        token = refs[-1]
        for cp in _swap_copies(ins, lz, send, recv):
            cp.start()
        token[...] = jnp.zeros_like(token)

    hbm = [pltpu.HBM(a.shape, a.dtype) for a in list(grads) + lands]
    outs = pl.pallas_call(
        body, name=name,
        in_specs=[HBM] * (2 * n),
        out_specs=[SEM, SEM] + [HBM] * (2 * n) + [pl.BlockSpec(memory_space=pltpu.VMEM)],
        out_shape=[pltpu.SemaphoreType.DMA((n,)), pltpu.SemaphoreType.DMA((n,))] + hbm
        + [jax.ShapeDtypeStruct((SUBLANES, LANES), F32)],
        input_output_aliases={k: 2 + k for k in range(2 * n)},
        compiler_params=pltpu.CompilerParams(has_side_effects=EFFECT),
    )(*[_in_hbm(a) for a in grads], *[_in_hbm(a) for a in lands])
    return outs[0], outs[1], outs[2:2 + n], outs[2 + n:2 + 2 * n], outs[-1]


def _swap_wait(send, recv, grads, lands, after, name):
    n = len(grads)

    def body(*refs):
        ins, lz = refs[:n], refs[n:2 * n]
        send_r, recv_r = refs[2 * n], refs[2 * n + 1]
        for cp in _swap_copies(ins, lz, send_r, recv_r):
            cp.wait_send()
            cp.wait_recv()

    hbm = [pltpu.HBM(a.shape, a.dtype) for a in list(grads) + list(lands)]
    outs = pl.pallas_call(
        body, name=name,
        in_specs=[HBM] * (2 * n) + [SEM, SEM, ANY],
        out_specs=[HBM] * (2 * n), out_shape=hbm,
        input_output_aliases={k: k for k in range(2 * n)},
        compiler_params=pltpu.CompilerParams(has_side_effects=EFFECT),
    )(*grads, *lands, send, recv, after)
    return outs[:n], outs[n:]


def _add_cast(g, other, cidx, name):
    ns, R, Cc = g.shape
    rh = R // 2
    tr = _tile(rh, max(16, (1 << 18) // Cc), 16)
    nb = rh // tr

    def body(c_ref, g_ref, o_ref, s_ref):
        s_ref[...] = (g_ref[...] + o_ref[...]).astype(BF16)

    return pl.pallas_call(
        body,
        grid_spec=pltpu.PrefetchScalarGridSpec(
            num_scalar_prefetch=1, grid=(ns, nb),
            in_specs=[pl.BlockSpec((None, tr, Cc), lambda k, i, c: (k, c[0] * nb + i, 0)),
                      pl.BlockSpec((None, tr, Cc), lambda k, i, c: (k, i, 0))],
            out_specs=pl.BlockSpec((None, tr, Cc), lambda k, i, c: (k, i, 0))),
        out_shape=jax.ShapeDtypeStruct((ns, rh, Cc), BF16),
        compiler_params=_params("parallel", "parallel"), name=name)(cidx, g, other)


def _sum_slots(part, got, idx, name):
    ns, rh, Cc = got.shape
    tr = _tile(rh, max(16, (1 << 17) // Cc), 16)
    nb = rh // tr

    def body(i_ref, p_ref, b_ref, o_ref):
        acc = p_ref[...].astype(F32)
        for m in range(ns):
            acc = acc + b_ref[m].astype(F32)
        o_ref[...] = acc

    return pl.pallas_call(
        body,
        grid_spec=pltpu.PrefetchScalarGridSpec(
            num_scalar_prefetch=1, grid=(nb,),
            in_specs=[pl.BlockSpec((None, tr, Cc), lambda i, s: (s[1], i, 0)),
                      pl.BlockSpec((ns, tr, Cc), lambda i, s: (0, i, 0))],
            out_specs=pl.BlockSpec((tr, Cc), lambda i, s: (s[0] * nb + i, 0))),
        out_shape=jax.ShapeDtypeStruct((2 * rh, Cc), F32),
        compiler_params=_params("parallel"), name=name)(idx, part, got)


def _share_halves(blocks, name):
    n = len(blocks)

    def body(*refs):
        ins, outs = refs[:n], refs[n:2 * n]
        send, recv = refs[2 * n:]
        x, y, c = _here()
        cps = []
        for k in range(n):
            rh = outs[k].shape[0] // 2
            mine = outs[k].at[pl.ds(c * rh, rh), :]
            cp = pltpu.make_async_remote_copy(
                src_ref=mine, dst_ref=mine, send_sem=send.at[k], recv_sem=recv.at[k],
                device_id=(x, y, 1 - c), device_id_type=MESH)
            cp.start()
            cps.append(cp)
        for cp in cps:
            cp.wait()

    return pl.pallas_call(
        body, in_specs=[ANY] * n, out_specs=[ANY] * n,
        out_shape=[jax.ShapeDtypeStruct(b.shape, b.dtype) for b in blocks],
        input_output_aliases={k: k for k in range(n)},
        scratch_shapes=[pltpu.SemaphoreType.DMA((n,)), pltpu.SemaphoreType.DMA((n,))],
        name=name)(*blocks)


def _small_copies(p_ref, slot_ref, send, recv):
    x, y, c = _here()
    mine = slot_ref.at[4 * x + 2 * y + c]
    cps = []
    for m in range(1, N_DEV):
        peer = (x ^ (m >> 2), y ^ ((m >> 1) & 1), c ^ (m & 1))
        cps.append(pltpu.make_async_remote_copy(
            src_ref=p_ref, dst_ref=mine, send_sem=send.at[m - 1], recv_sem=recv.at[m - 1],
            device_id=peer, device_id_type=MESH))
    return cps


def _small_start(packed):
    slots = lax.empty((N_DEV,) + packed.shape, packed.dtype)

    def body(p_ref, s_ref, send, recv, p_thru, s_thru, token):
        for cp in _small_copies(p_ref, s_ref, send, recv):
            cp.start()
        token[...] = jnp.zeros_like(token)

    return pl.pallas_call(
        body, name="small_start",
        in_specs=[HBM, HBM],
        out_specs=[SEM, SEM, HBM, HBM, pl.BlockSpec(memory_space=pltpu.VMEM)],
        out_shape=[pltpu.SemaphoreType.DMA((N_DEV - 1,)), pltpu.SemaphoreType.DMA((N_DEV - 1,)),
                   pltpu.HBM(packed.shape, packed.dtype), pltpu.HBM(slots.shape, slots.dtype),
                   jax.ShapeDtypeStruct((SUBLANES, LANES), F32)],
        input_output_aliases={0: 2, 1: 3},
        compiler_params=pltpu.CompilerParams(has_side_effects=EFFECT),
    )(_in_hbm(packed), _in_hbm(slots))


def _small_wait(send, recv, packed, slots, after):
    def body(p_ref, s_ref, send_r, recv_r, after_ref, p_out, s_out):
        for cp in _small_copies(p_ref, s_ref, send_r, recv_r):
            cp.wait_send()
            cp.wait_recv()

    return pl.pallas_call(
        body, name="small_wait",
        in_specs=[HBM, HBM, SEM, SEM, ANY], out_specs=[HBM, HBM],
        out_shape=[pltpu.HBM(packed.shape, packed.dtype), pltpu.HBM(slots.shape, slots.dtype)],
        input_output_aliases={0: 0, 1: 1},
        compiler_params=pltpu.CompilerParams(has_side_effects=EFFECT),
    )(packed, slots, send, recv, after)


def _sum_devices(packed, slots, me):
    n, R, _ = slots.shape
    tr = _tile(R, 256)

    def body(m_ref, p_ref, s_ref, o_ref):
        own = p_ref[...]
        acc = None
        for d in range(n):
            term = jnp.where(m_ref[0] == d, own, s_ref[d])
            acc = term if acc is None else acc + term
        o_ref[...] = acc

    return pl.pallas_call(
        body,
        grid_spec=pltpu.PrefetchScalarGridSpec(
            num_scalar_prefetch=1, grid=(R // tr,),
            in_specs=[pl.BlockSpec((tr, LANES), lambda i, m: (i, 0)),
                      pl.BlockSpec((n, tr, LANES), lambda i, m: (0, i, 0))],
            out_specs=pl.BlockSpec((tr, LANES), lambda i, m: (i, 0))),
        out_shape=jax.ShapeDtypeStruct((R, LANES), F32),
        compiler_params=_params("parallel"), name="sum_devices")(me, packed, slots)


def _pack(arrs):
    rows, parts = [], []
    for a in arrs:
        flat = a.reshape(-1)
        r = -(-flat.shape[0] // (SUBLANES * LANES)) * SUBLANES
        parts.append(jnp.pad(flat, (0, r * LANES - flat.shape[0])).reshape(r, LANES))
        rows.append(r)
    return jnp.concatenate(parts, axis=0), rows


def _unpack(packed, rows, shapes):
    out, r0 = [], 0
    for r, shp in zip(rows, shapes):
        size = math.prod(shp)
        out.append(packed[r0:r0 + r].reshape(-1)[:size].reshape(shp))
        r0 += r
    return out


def _block_diag(w, per):
    H, dh, _ = w.shape
    w4 = w.reshape(H // per, per, dh, dh)
    eye = jnp.eye(per, dtype=w.dtype)
    return (w4[:, :, :, None, :] * eye[None, :, None, :, None]).reshape(H // per, per * dh, per * dh)


def _block_diag_take(d, per):
    n, s, _ = d.shape
    dh = s // per
    d5 = d.reshape(n, per, dh, per, dh)
    return jnp.stack([d5[:, h, :, h, :] for h in range(per)], axis=1).reshape(n * per, dh, dh)


def kernel(x, ffn1_norm, ffn1_w_gate, ffn1_w_up, ffn1_w_down, mix_norm, w_in, conv_dw, conv_dw_bias, conv_ln_g, conv_ln_b, lru_conv_w, lru_conv_b, lru_w_a, lru_b_a, lru_w_x, lru_b_x, lru_lambda, w_out, ffn2_norm, ffn2_w_gate, ffn2_w_up, ffn2_w_down, final_norm, loss_target, m_ffn1_norm, m_ffn1_w_gate, m_ffn1_w_up, m_ffn1_w_down, m_mix_norm, m_w_in, m_conv_dw, m_conv_dw_bias, m_conv_ln_g, m_conv_ln_b, m_lru_conv_w, m_lru_conv_b, m_lru_w_a, m_lru_b_a, m_lru_w_x, m_lru_b_x, m_lru_lambda, m_w_out, m_ffn2_norm, m_ffn2_w_gate, m_ffn2_w_up, m_ffn2_w_down, m_final_norm, v_ffn1_norm, v_ffn1_w_gate, v_ffn1_w_up, v_ffn1_w_down, v_mix_norm, v_w_in, v_conv_dw, v_conv_dw_bias, v_conv_ln_g, v_conv_ln_b, v_lru_conv_w, v_lru_conv_b, v_lru_w_a, v_lru_b_a, v_lru_w_x, v_lru_b_x, v_lru_lambda, v_w_out, v_ffn2_norm, v_ffn2_w_gate, v_ffn2_w_up, v_ffn2_w_down, v_final_norm):
    names = ['ffn1_norm', 'ffn1_w_gate', 'ffn1_w_up', 'ffn1_w_down', 'mix_norm', 'w_in', 'conv_dw', 'conv_dw_bias',
             'conv_ln_g', 'conv_ln_b', 'lru_conv_w', 'lru_conv_b', 'lru_w_a', 'lru_b_a', 'lru_w_x', 'lru_b_x',
             'lru_lambda', 'w_out', 'ffn2_norm', 'ffn2_w_gate', 'ffn2_w_up', 'ffn2_w_down', 'final_norm']
    env = dict(locals())
    W = {n: env[n] for n in names}
    M = {n: env['m_' + n] for n in names}
    V = {n: env['v_' + n] for n in names}

    xi, yi, ci = _here()
    chip = 2 * xi + yi
    cidx = ci.astype(jnp.int32).reshape(1)
    T, D = x.shape[-2], x.shape[-1]
    xs = x.reshape(T, D)
    tgt = loss_target.reshape(T, D)
    K, Cs = conv_dw.shape
    C = conv_dw_bias.shape[0]
    Wl = lru_conv_b.shape[0]
    K4 = lru_conv_w.shape[0]
    heads, dh, _ = lru_w_a.shape
    per = LANES // dh

    def row(v):
        return v.reshape(1, -1)

    tform = ('ffn1_w_gate', 'ffn1_w_up', 'ffn2_w_gate', 'ffn2_w_up')
    for n in tform:
        W[n], M[n], V[n] = W[n].T, M[n].T, V[n].T
    kp = -(-K // SUBLANES) * SUBLANES
    taps = jnp.concatenate([conv_dw, jnp.zeros((kp - K, Cs), F32), lru_conv_w,
                            jnp.zeros((2 * SUBLANES - K4, Cs), F32)], axis=0)
    idx = jnp.stack([ci, chip]).astype(jnp.int32)
    (wff1,) = _gather_weights([_place_cast([W['ffn1_w_gate'], W['ffn1_w_up'], ffn1_w_down], idx, BF16, "place_ffn1")])
    mixl = [_place_cast([w_in], idx, BF16, "place_w_in"), _place_cast([w_out], idx, BF16, "place_w_out"),
            _place_cast([taps], idx, F32, "place_taps")]
    msend, mrecv, mixl, mtok = _gather_start(mixl, wff1, "gather_mix_start")
    ff2l = _place_cast([W['ffn2_w_gate'], W['ffn2_w_up'], ffn2_w_down], idx, BF16, "place_ffn2")
    fsend, frecv, ff2l, ftok = _gather_start([ff2l], mtok, "gather_ffn2_start")
    wa_bd = _block_diag(lru_w_a, per).astype(BF16)
    wx_bd = _block_diag(lru_w_x, per).astype(BF16)

    x1, a1, b1 = _ffn_fwd(xs, row(ffn1_norm) + ftok[0:1, 0:1], wff1, "ffn1_fwd")
    win, wout, taps = _gather_wait(msend, mrecv, mixl, x1, "gather_mix_wait")
    win, wout, taps = win[0], wout.reshape(-1, D), taps[0]
    conv_w_full = taps[:, :K].transpose(1, 0, 2).reshape(K, N_CHIPS * Cs)
    lru_w4_full = taps[:, kp:kp + K4].transpose(1, 0, 2).reshape(K4, N_CHIPS * Cs)
    z = _mix_in_fwd(x1, row(mix_norm), win)
    u, u1 = _conv_fwd(z, conv_w_full, row(conv_dw_bias), row(conv_ln_g), row(conv_ln_b))
    yr, hs = _lru_fwd(z, 2 * C, lru_w4_full, row(lru_conv_b), wa_bd, row(lru_b_a), wx_bd, row(lru_b_x),
                      row(lru_lambda))
    x2 = _mix_out_fwd(x1, u, yr, wout)
    (wff2,) = _gather_wait(fsend, frecv, ff2l, x2, "gather_ffn2_wait")
    x3, a2, b2 = _ffn_fwd(x2, row(ffn2_norm), wff2, "ffn2_fwd")
    dx3, loss_blk, d_final = _final_loss(x3, row(final_norm), tgt)

    dx2, da2, db2, p2, hb2, dyh2, d_ffn2n = _ffn_bwd_tok(dx3, x2, row(ffn2_norm), a2, b2, wff2, "ffn2_bwd")
    dwg2, dwu2, dwd2 = _ffn_wgrad(hb2, dyh2, da2, db2, p2, ftok, "ffn2_wgrad")
    wsend, wrecv, f2g, f2o, wtok = _swap_start([dwg2, dwu2, dwd2], "swap_ffn2_start")
    dcat, dwout = _mix_out_bwd(dx2, u, yr, wout)
    dzc, cst = _conv_bwd(dcat, u1, z, conv_w_full, row(conv_ln_g) + wtok[0:1, 0:1], row(conv_ln_b))
    dzx, dzg, lst, dwa_bd, dwx_bd = _lru_bwd(dcat, C, hs, z, 2 * C, lru_w4_full, row(lru_conv_b), wa_bd,
                                              row(lru_b_a), wx_bd, row(lru_b_x), row(lru_lambda))
    dx1, dwin, d_mixn = _mix_in_bwd(dzc, dzx, dzg, x1, dx2, row(mix_norm), win)

    early_names = ['w_in', 'w_out', 'ffn2_w_gate', 'ffn2_w_up', 'ffn2_w_down']
    mixg = [dwin, dwout.reshape(N_CHIPS, -1, D)]
    mixo = _swap_halves_out(mixg, "swap_halves_mix")
    f2g, f2o = _swap_wait(wsend, wrecv, f2g, f2o, dwin, "swap_ffn2_wait")
    e_parts = [_add_cast(g, o, cidx, "add_cast_" + n)
               for g, o, n in zip(mixg + list(f2g), list(mixo) + list(f2o), early_names)]
    esend, erecv, e_parts, e_lands, etok = _exchange_start(e_parts, "exchange_early_start")

    dx0, da1, db1, p1, hb1, dyh1, d_ffn1n = _ffn_bwd_tok(dx1, xs, row(ffn1_norm) + etok[0:1, 0:1], a1, b1, wff1,
                                                         "ffn1_bwd")

    small_names = ['ffn1_norm', 'mix_norm', 'conv_dw', 'conv_dw_bias', 'conv_ln_g', 'conv_ln_b', 'lru_conv_w',
                   'lru_conv_b', 'lru_w_a', 'lru_b_a', 'lru_w_x', 'lru_b_x', 'lru_lambda', 'ffn2_norm',
                   'final_norm']
    small = {
        'ffn1_norm': d_ffn1n, 'mix_norm': d_mixn, 'conv_dw': cst[:K], 'conv_dw_bias': cst[K + 1],
        'conv_ln_g': cst[K + 2], 'conv_ln_b': cst[K + 3], 'lru_conv_w': lst[:K4], 'lru_conv_b': lst[K4],
        'lru_w_a': _block_diag_take(dwa_bd, per), 'lru_b_a': lst[K4 + 1],
        'lru_w_x': _block_diag_take(dwx_bd, per), 'lru_b_x': lst[K4 + 2], 'lru_lambda': lst[K4 + 3],
        'ffn2_norm': d_ffn2n, 'final_norm': d_final,
    }
    packed, rows = _pack([small[n] for n in small_names])
    ssend, srecv, packed, sslots, stok = _small_start(packed)

    dwg1, dwu1, dwd1 = _ffn_wgrad(hb1, dyh1, da1, db1, p1, stok, "ffn1_wgrad")

    last_names = ['ffn1_w_gate', 'ffn1_w_up', 'ffn1_w_down']
    last = [dwg1, dwu1, dwd1]
    l_parts = [_add_cast(g, o, cidx, "add_cast_" + n)
               for g, o, n in zip(last, _swap_halves_out(last, "swap_halves_last"), last_names)]
    lsend, lrecv, l_parts, l_lands, ltok = _exchange_start(l_parts, "exchange_last_start")
    e_parts, e_slots = _exchange_wait(esend, erecv, e_parts, e_lands, ltok, "exchange_early_wait")
    delta, new_m, new_v = {}, {}, {}

    def finish(group, parts, slots, tag):
        halves = [_sum_slots(p, b, idx, "sum_slots_" + n) for p, b, n in zip(parts, slots, group)]
        for n, g in zip(group, _share_halves(halves, "share_halves_" + tag)):
            G[n] = g
            delta[n], new_m[n], new_v[n] = _adamw(W[n], g, M[n], V[n], "adamw_" + n)

    G = {}
    finish(early_names, e_parts, e_slots, "early")

    full_shapes = [(K, C) if n == 'conv_dw' else (K4, Wl) if n == 'lru_conv_w' else W[n].shape for n in small_names]
    packed, sslots = _small_wait(ssend, srecv, packed, sslots, dwd1)
    summed = _sum_devices(packed, sslots, (4 * xi + 2 * yi + ci).astype(jnp.int32).reshape(1))
    for n, gsum in zip(small_names, _unpack(summed, rows, full_shapes)):
        if n == 'conv_dw':
            gsum = lax.dynamic_slice_in_dim(gsum, chip * Cs, Cs, axis=1)
        elif n == 'lru_conv_w':
            gsum = lax.dynamic_slice_in_dim(gsum, chip * lru_conv_w.shape[1], lru_conv_w.shape[1], axis=1)
        G[n] = gsum

    pw, prow = _pack([W[n] for n in small_names])
    pg, _ = _pack([G[n] for n in small_names])
    pm, _ = _pack([M[n] for n in small_names])
    pv, _ = _pack([V[n] for n in small_names])
    sd, sm, sv = _adamw(pw, pg, pm, pv, "adamw_small")
    shapes = [W[n].shape for n in small_names]
    for n, a, b, c_ in zip(small_names, _unpack(sd, prow, shapes), _unpack(sm, prow, shapes),
                           _unpack(sv, prow, shapes)):
        delta[n], new_m[n], new_v[n] = a, b, c_

    done = sd[0:SUBLANES] + delta[early_names[-1]][0:SUBLANES, 0:LANES]
    l_parts, l_slots = _exchange_wait(lsend, lrecv, l_parts, l_lands, done, "exchange_last_wait")
    finish(last_names, l_parts, l_slots, "last")

    loss = lax.psum(loss_blk[0, 0], ("x", "y", "c"))
    grad_x = dx0.reshape(x.shape)
    for n in tform:
        G[n], delta[n], new_m[n], new_v[n] = G[n].T, delta[n].T, new_m[n].T, new_v[n].T
    return (loss, grad_x, *[G[n] for n in names], *[delta[n] for n in names],
            *[new_m[n] for n in names], *[new_v[n] for n in names])
```

```python
import functools
import math

import jax
import jax.numpy as jnp
from jax import lax
from jax.experimental import pallas as pl
from jax.experimental.pallas import tpu as pltpu

F32 = jnp.float32
BF16 = jnp.bfloat16
MESH = pl.DeviceIdType.MESH

RMS_EPS = 1e-6
LN_EPS = 1e-5
LRU_C = 8.0
FFN_RES_SCALE = 0.5
ADAM_LR = 0.001
ADAM_B1 = 0.9
ADAM_B2 = 0.999
ADAM_EPS = 1e-08
ADAM_WD = 0.01
ADAM_STEP = 10

LANES = 128
SUBLANES = 8
CONV_HALO = 32
LRU_HALO = 8
ROW_CHUNK = 64
VMEM_LIMIT = 56 * 1024 * 1024
VMEM_LIMIT_BIG = 61 * 1024 * 1024
N_CHIPS = 4
N_DEV = 8
TOK_TILE = 1024
BWD_TILE = 512
FFN_BWD_TILE = 512
BWD_ROWS = 32
FFN_BWD_CHAIN = 256
CONV_TILE = 512
LRU_TILE = 1024
LRU_GROUPS = 4


def _dot(a, b):
    return jnp.dot(a, b, preferred_element_type=F32)


def _dot_nt(a, b):
    return lax.dot_general(a, b, (((1,), (1,)), ((), ())), preferred_element_type=F32)


def _dot_tn(a, b):
    return lax.dot_general(a, b, (((0,), (0,)), ((), ())), preferred_element_type=F32)


def _tile(n, pref, mult=SUBLANES):
    for t in range(min(pref, n), 0, -1):
        if n % t == 0 and t % mult == 0:
            return t
    return n


def _params(*sem, vmem=None):
    return pltpu.CompilerParams(dimension_semantics=sem, vmem_limit_bytes=vmem or VMEM_LIMIT)


def _rms_stats(x):
    r = lax.rsqrt(jnp.mean(x * x, axis=-1, keepdims=True) + RMS_EPS)
    return x * r, r


def _rms_bwd(dh, xh, r, g):
    dxh = dh * g
    return r * (dxh - xh * jnp.mean(dxh * xh, axis=-1, keepdims=True))


def _colsum(v):
    return jnp.sum(v, axis=0, keepdims=True)


def _ffn_fwd(x, g, wff, name):
    T, D = x.shape
    ns, fs = wff.shape[1], wff.shape[2]
    tm = _tile(T, TOK_TILE)
    mc = _tile(tm, FFN_BWD_CHAIN, 16)

    def body(x_ref, g_ref, wg_ref, wu_ref, wd_ref, y_ref, a_ref, b_ref, hb_ref, acc_ref):
        j = pl.program_id(1)

        @pl.when(j == 0)
        def _():
            xh, _ = _rms_stats(x_ref[...])
            hb_ref[...] = (xh * g_ref[...]).astype(BF16)
            acc_ref[...] = jnp.zeros_like(acc_ref)

        for q0 in range(0, tm, mc):
            blk = pl.ds(q0, mc)
            hb = hb_ref[blk, :]
            a = _dot_nt(hb, wg_ref[...])
            b = _dot_nt(hb, wu_ref[...])
            a_ref[blk, :] = a.astype(BF16)
            b_ref[blk, :] = b.astype(BF16)
            p = (a * jax.nn.sigmoid(a) * b).astype(BF16)
            acc_ref[blk, :] += _dot(p, wd_ref[...])

        @pl.when(j == ns - 1)
        def _():
            y_ref[...] = x_ref[...] + FFN_RES_SCALE * acc_ref[...]

    def wspec(n):
        return pl.BlockSpec((None, None, fs, D), lambda i, j: (n, j, 0, 0))

    mid = pl.BlockSpec((None, tm, fs), lambda i, j: (j, i, 0))
    return pl.pallas_call(
        body, grid=(T // tm, ns),
        in_specs=[pl.BlockSpec((tm, D), lambda i, j: (i, 0)), pl.BlockSpec((1, D), lambda i, j: (0, 0)),
                  wspec(0), wspec(1), wspec(2)],
        out_specs=[pl.BlockSpec((tm, D), lambda i, j: (i, 0)), mid, mid],
        out_shape=[jax.ShapeDtypeStruct((T, D), F32), jax.ShapeDtypeStruct((ns, T, fs), BF16),
                   jax.ShapeDtypeStruct((ns, T, fs), BF16)],
        scratch_shapes=[pltpu.VMEM((tm, D), BF16), pltpu.VMEM((tm, D), F32)],
        compiler_params=_params("parallel", "arbitrary"), name=name)(x, g, wff, wff, wff)


def _ffn_bwd_tok(dy, x, g, a, b, wff, name):
    T, D = x.shape
    ns, fs = wff.shape[1], wff.shape[2]
    tm = _tile(T, FFN_BWD_TILE)
    rc = _tile(tm, BWD_ROWS)
    mc = _tile(tm, FFN_BWD_CHAIN, rc)

    def body(dy_ref, x_ref, g_ref, a_ref, b_ref, wg_ref, wu_ref, wd_ref,
             dx_ref, da_ref, db_ref, p_ref, hb_ref, dyh_ref, dg_ref, dh_ref, dp_ref):
        i, j = pl.program_id(0), pl.program_id(1)

        @pl.when((i == 0) & (j == 0))
        def _():
            dg_ref[...] = jnp.zeros_like(dg_ref)

        @pl.when(j == 0)
        def _():
            for r0 in range(0, tm, rc):
                rows = pl.ds(r0, rc)
                xh, _ = _rms_stats(x_ref[rows, :])
                hb_ref[rows, :] = (xh * g_ref[...]).astype(BF16)
                dyh_ref[rows, :] = (FFN_RES_SCALE * dy_ref[rows, :]).astype(BF16)
            dh_ref[...] = jnp.zeros_like(dh_ref)

        for q0 in range(0, tm, mc):
            blk = pl.ds(q0, mc)
            dp_ref[blk, :] = _dot_nt(dyh_ref[blk, :], wd_ref[...])
            for r0 in range(q0, q0 + mc, rc):
                rows = pl.ds(r0, rc)
                av = a_ref[rows, :].astype(F32)
                bv = b_ref[rows, :].astype(F32)
                dp = dp_ref[rows, :]
                s = jax.nn.sigmoid(av)
                sl = av * s
                da_ref[rows, :] = (dp * bv * (s * (1.0 + av * (1.0 - s)))).astype(BF16)
                db_ref[rows, :] = (dp * sl).astype(BF16)
                p_ref[rows, :] = (sl * bv).astype(BF16)
            dh_ref[blk, :] += _dot(da_ref[blk, :], wg_ref[...]) + _dot(db_ref[blk, :], wu_ref[...])

        @pl.when(j == ns - 1)
        def _():
            gv = g_ref[...]
            dg = jnp.zeros((1, D), F32)
            for r0 in range(0, tm, rc):
                rows = pl.ds(r0, rc)
                xh, r = _rms_stats(x_ref[rows, :])
                dh = dh_ref[rows, :]
                dx_ref[rows, :] = dy_ref[rows, :] + _rms_bwd(dh, xh, r, gv)
                dg = dg + _colsum(dh * xh)
            dg_ref[...] += dg

    def wspec(n):
        return pl.BlockSpec((None, None, fs, D), lambda i, j: (n, j, 0, 0))

    tok = pl.BlockSpec((tm, D), lambda i, j: (i, 0))
    mid = pl.BlockSpec((None, tm, fs), lambda i, j: (j, i, 0))
    vec = pl.BlockSpec((1, D), lambda i, j: (0, 0))
    return pl.pallas_call(
        body, grid=(T // tm, ns),
        in_specs=[tok, tok, vec, mid, mid, wspec(0), wspec(1), wspec(2)],
        out_specs=[tok, mid, mid, mid, tok, tok, vec],
        out_shape=[jax.ShapeDtypeStruct((T, D), F32),
                   jax.ShapeDtypeStruct((ns, T, fs), BF16), jax.ShapeDtypeStruct((ns, T, fs), BF16),
                   jax.ShapeDtypeStruct((ns, T, fs), BF16),
                   jax.ShapeDtypeStruct((T, D), BF16), jax.ShapeDtypeStruct((T, D), BF16),
                   jax.ShapeDtypeStruct((1, D), F32)],
        scratch_shapes=[pltpu.VMEM((tm, D), F32), pltpu.VMEM((tm, fs), F32)],
        compiler_params=_params("arbitrary", "arbitrary", vmem=VMEM_LIMIT_BIG), name=name)(dy, x, g, a, b, wff, wff, wff)


def _ffn_wgrad(hb, dyh, da, db, p, after, name):
    T, D = hb.shape
    ns, _, fs = da.shape
    tm = _tile(T, TOK_TILE)

    def body(hb_ref, dyh_ref, da_ref, db_ref, p_ref, after_ref, dwg_ref, dwu_ref, dwd_ref):
        @pl.when(pl.program_id(1) == 0)
        def _():
            dwg_ref[...] = jnp.zeros_like(dwg_ref)
            dwu_ref[...] = jnp.zeros_like(dwu_ref)
            dwd_ref[...] = jnp.zeros_like(dwd_ref)

        hbv = hb_ref[...]
        dwg_ref[...] += _dot_tn(da_ref[...], hbv)
        dwu_ref[...] += _dot_tn(db_ref[...], hbv)
        dwd_ref[...] += _dot_tn(p_ref[...], dyh_ref[...])

    tok = pl.BlockSpec((tm, D), lambda j, i: (i, 0))
    mid = pl.BlockSpec((None, tm, fs), lambda j, i: (j, i, 0))
    wsp = pl.BlockSpec((None, fs, D), lambda j, i: (j, 0, 0))
    sds = jax.ShapeDtypeStruct((ns, fs, D), F32)
    return pl.pallas_call(
        body, grid=(ns, T // tm),
        in_specs=[tok, tok, mid, mid, mid, pl.BlockSpec((SUBLANES, LANES), lambda j, i: (0, 0))],
        out_specs=[wsp, wsp, wsp], out_shape=[sds, sds, sds],
        compiler_params=_params("parallel", "arbitrary"), name=name)(hb, dyh, da, db, p, after)


def _mix_in_fwd(x, g, win):
    T, D = x.shape
    ns, ws = win.shape[0], win.shape[2]
    tm = _tile(T, TOK_TILE)

    def body(x_ref, g_ref, w_ref, z_ref):
        xh, _ = _rms_stats(x_ref[...])
        hb = (xh * g_ref[...]).astype(BF16)
        for j in range(ns):
            z_ref[:, pl.ds(j * ws, ws)] = _dot(hb, w_ref[j])

    return pl.pallas_call(
        body, grid=(T // tm,),
        in_specs=[pl.BlockSpec((tm, D), lambda i: (i, 0)), pl.BlockSpec((1, D), lambda i: (0, 0)),
                  pl.BlockSpec((ns, D, ws), lambda i: (0, 0, 0), pipeline_mode=pl.Buffered(1))],
        out_specs=pl.BlockSpec((tm, ns * ws), lambda i: (i, 0)),
        out_shape=jax.ShapeDtypeStruct((T, ns * ws), F32),
        compiler_params=_params("parallel"), name="mix_in_fwd")(x, g, win)


def _tap_sum(buf, w_ref, ntaps, first_row, r0, rows, flip):
    acc = None
    for k in range(ntaps):
        off = (ntaps - 1 - k) if flip else k
        t = buf[pl.ds(first_row + r0 + off, rows), :] * w_ref[pl.ds(k, 1), :]
        acc = t if acc is None else acc + t
    return acc


def _shift_copies(buf, sh, rows):
    for r in range(1, SUBLANES):
        sh[r - 1, pl.ds(0, rows), :] = buf[pl.ds(r, rows), :]


def _tap_rows(buf, sh, off, r0, rows):
    r = off % SUBLANES
    if r == 0:
        return buf[pl.ds(off + r0, rows), :]
    return sh[r - 1, pl.ds(off - r + r0, rows), :]


def _tap_sum_tiles(buf, sh, w_ref, ntaps, first_row, r0, rows, flip):
    acc = None
    for k in range(ntaps):
        off = first_row + ((ntaps - 1 - k) if flip else k)
        t = _tap_rows(buf, sh, off, r0, rows) * w_ref[pl.ds(k, 1), :]
        acc = t if acc is None else acc + t
    return acc


def _conv_fwd(z, w, bias, lng, lnb):
    T = z.shape[0]
    K, C = w.shape
    tm = _tile(T, CONV_TILE, ROW_CHUNK)
    rc = min(ROW_CHUNK, tm)
    srows = tm + CONV_HALO - SUBLANES

    def body(cv_ref, cg_ref, w_ref, b_ref, g_ref, bb_ref, u_ref, u1_ref, buf, sh):
        @pl.when(pl.program_id(0) == 0)
        def _():
            buf[pl.ds(0, CONV_HALO), :] = jnp.zeros((CONV_HALO, C), F32)

        buf[pl.ds(CONV_HALO, tm), :] = cv_ref[...] * jax.nn.sigmoid(cg_ref[...])
        _shift_copies(buf, sh, srows)
        for r0 in range(0, tm, rc):
            u1 = _tap_sum_tiles(buf, sh, w_ref, K, CONV_HALO - (K - 1), r0, rc, False) + b_ref[...]
            u1_ref[pl.ds(r0, rc), :] = u1
            xc = u1 - jnp.mean(u1, axis=-1, keepdims=True)
            xh = xc * lax.rsqrt(jnp.mean(xc * xc, axis=-1, keepdims=True) + LN_EPS)
            u2 = xh * g_ref[...] + bb_ref[...]
            u_ref[pl.ds(r0, rc), :] = (u2 * jax.nn.sigmoid(u2)).astype(BF16)
        buf[pl.ds(0, CONV_HALO), :] = buf[pl.ds(tm, CONV_HALO), :]

    vec = pl.BlockSpec((1, C), lambda i: (0, 0))
    return pl.pallas_call(
        body, grid=(T // tm,),
        in_specs=[pl.BlockSpec((tm, C), lambda i: (i, 0)), pl.BlockSpec((tm, C), lambda i: (i, 1)),
                  pl.BlockSpec((K, C), lambda i: (0, 0)), vec, vec, vec],
        out_specs=[pl.BlockSpec((tm, C), lambda i: (i, 0)), pl.BlockSpec((tm, C), lambda i: (i, 0))],
        out_shape=[jax.ShapeDtypeStruct((T, C), BF16), jax.ShapeDtypeStruct((T, C), F32)],
        scratch_shapes=[pltpu.VMEM((CONV_HALO + tm, C), F32), pltpu.VMEM((SUBLANES - 1, srows, C), F32)],
        compiler_params=_params("arbitrary"), name="conv_fwd")(z, z, w, bias, lng, lnb)


def _conv_bwd(dcat, u1, z, w, lng, lnb):
    T = z.shape[0]
    K, C = w.shape
    tm = _tile(T, CONV_TILE, ROW_CHUNK)
    rc = min(ROW_CHUNK, tm)
    nI = T // tm
    hb = tm // CONV_HALO
    srows = ((K + 4 + SUBLANES - 1) // SUBLANES) * SUBLANES
    shrows = tm + CONV_HALO - SUBLANES

    def body(du_ref, u1_ref, cv_ref, cg_ref, cvp_ref, cgp_ref, w_ref, g_ref, bb_ref,
             dz_ref, st_ref, u0buf, d1buf, ush, dsh):
        i = pl.program_id(0)
        ti = nI - 1 - i

        @pl.when(i == 0)
        def _():
            st_ref[...] = jnp.zeros_like(st_ref)
            d1buf[pl.ds(tm, CONV_HALO), :] = jnp.zeros((CONV_HALO, C), F32)

        prev = cvp_ref[...] * jax.nn.sigmoid(cgp_ref[...])
        u0buf[pl.ds(0, CONV_HALO), :] = jnp.where(ti == 0, 0.0, prev)
        u0buf[pl.ds(CONV_HALO, tm), :] = cv_ref[...] * jax.nn.sigmoid(cg_ref[...])

        gv = g_ref[...]
        dbias = jnp.zeros((1, C), F32)
        dgain = jnp.zeros((1, C), F32)
        dlnb = jnp.zeros((1, C), F32)
        for r0 in range(0, tm, rc):
            u1 = u1_ref[pl.ds(r0, rc), :]
            xc = u1 - jnp.mean(u1, axis=-1, keepdims=True)
            rstd = lax.rsqrt(jnp.mean(xc * xc, axis=-1, keepdims=True) + LN_EPS)
            xh = xc * rstd
            u2 = xh * gv + bb_ref[...]
            s = jax.nn.sigmoid(u2)
            du2 = du_ref[pl.ds(r0, rc), :] * (s * (1.0 + u2 * (1.0 - s)))
            dgain = dgain + _colsum(du2 * xh)
            dlnb = dlnb + _colsum(du2)
            dxh = du2 * gv
            du1 = rstd * (dxh - jnp.mean(dxh, axis=-1, keepdims=True)
                          - xh * jnp.mean(dxh * xh, axis=-1, keepdims=True))
            dbias = dbias + _colsum(du1)
            d1buf[pl.ds(r0, rc), :] = du1
        st_ref[pl.ds(K + 1, 1), :] += dbias
        st_ref[pl.ds(K + 2, 1), :] += dgain
        st_ref[pl.ds(K + 3, 1), :] += dlnb

        _shift_copies(u0buf, ush, shrows)
        _shift_copies(d1buf, dsh, shrows)
        for k in range(K):
            acc = jnp.zeros((SUBLANES, C), F32)
            for r0 in range(0, tm, rc):
                prod = d1buf[pl.ds(r0, rc), :] * _tap_rows(u0buf, ush, CONV_HALO - (K - 1) + k, r0, rc)
                acc = acc + jnp.sum(prod.reshape(rc // SUBLANES, SUBLANES, C), axis=0)
            st_ref[pl.ds(k, 1), :] += _colsum(acc)

        for r0 in range(0, tm, rc):
            du0 = _tap_sum_tiles(d1buf, dsh, w_ref, K, 0, r0, rc, True)
            cv = cv_ref[pl.ds(r0, rc), :]
            sg = jax.nn.sigmoid(cg_ref[pl.ds(r0, rc), :])
            dz_ref[pl.ds(r0, rc), pl.ds(0, C)] = (du0 * sg).astype(BF16)
            dz_ref[pl.ds(r0, rc), pl.ds(C, C)] = (du0 * cv * sg * (1.0 - sg)).astype(BF16)
        d1buf[pl.ds(tm, CONV_HALO), :] = d1buf[pl.ds(0, CONV_HALO), :]

    def rev(col):
        return lambda i: (nI - 1 - i, col)

    def rev_prev(col):
        return lambda i: (jnp.maximum((nI - 1 - i) * hb - 1, 0), col)

    vec = pl.BlockSpec((1, C), lambda i: (0, 0))
    return pl.pallas_call(
        body, grid=(nI,),
        in_specs=[pl.BlockSpec((tm, C), rev(0)), pl.BlockSpec((tm, C), rev(0)),
                  pl.BlockSpec((tm, C), rev(0)), pl.BlockSpec((tm, C), rev(1)),
                  pl.BlockSpec((CONV_HALO, C), rev_prev(0)), pl.BlockSpec((CONV_HALO, C), rev_prev(1)),
                  pl.BlockSpec((K, C), lambda i: (0, 0)), vec, vec],
        out_specs=[pl.BlockSpec((tm, 2 * C), rev(0)), pl.BlockSpec((srows, C), lambda i: (0, 0))],
        out_shape=[jax.ShapeDtypeStruct((T, 2 * C), BF16), jax.ShapeDtypeStruct((srows, C), F32)],
        scratch_shapes=[pltpu.VMEM((CONV_HALO + tm, C), F32), pltpu.VMEM((tm + CONV_HALO, C), F32),
                        pltpu.VMEM((SUBLANES - 1, shrows, C), F32), pltpu.VMEM((SUBLANES - 1, shrows, C), F32)],
        compiler_params=_params("arbitrary"), name="conv_bwd")(dcat, u1, z, z, z, z, w, lng, lnb)


def _softplus(v):
    return jnp.maximum(v, 0.0) + jnp.log(1.0 + jnp.exp(-jnp.abs(v)))


def _gelu(v):
    c = math.sqrt(2.0 / math.pi)
    t = jnp.tanh(c * (v + 0.044715 * v * v * v))
    gl = 0.5 * v * (1.0 + t)
    dgl = 0.5 * (1.0 + t) + 0.5 * v * (1.0 - t * t) * c * (1.0 + 3.0 * 0.044715 * v * v)
    return gl, dgl


def _lru_gates(xr, wa, ba, wx, bx, lam):
    xb = xr.astype(BF16)
    r = jax.nn.sigmoid(_dot(xb, wa) + ba)
    ig = jax.nn.sigmoid(_dot(xb, wx) + bx)
    sp = _softplus(-lam)
    log_a = -LRU_C * r * sp
    a = jnp.exp(log_a)
    y = 2.0 * log_a
    series = -(y * (1.0 + y * (0.5 + y * (1.0 / 6.0 + y * (1.0 / 24.0)))))
    mult = jnp.sqrt(jnp.where(y > -0.02, series, 1.0 - jnp.exp(y)))
    return a, mult, r, ig, sp


def _scan_tile(a_s, b_s, h_s, p_s, carry, seg, reverse):
    hl = [jnp.zeros((SUBLANES, LANES), F32)] * LRU_GROUPS
    pr = [jnp.ones((SUBLANES, LANES), F32)] * LRU_GROUPS
    for n in range(seg):
        for g in range(LRU_GROUPS):
            rows = pl.ds(g * SUBLANES * seg + ((seg - 1 - n) if reverse else n), SUBLANES, stride=seg)
            av = a_s[rows, :]
            hl[g] = av * hl[g] + b_s[rows, :]
            pr[g] = av * pr[g]
            h_s[rows, :] = hl[g]
            p_s[rows, :] = pr[g]
    nseg = SUBLANES * LRU_GROUPS
    cs = [None] * nseg
    c = carry
    for s in (range(nseg - 1, -1, -1) if reverse else range(nseg)):
        g, r = divmod(s, SUBLANES)
        cs[s] = c
        c = hl[g][r:r + 1, :] + pr[g][r:r + 1, :] * c
    return cs, c


def _lru_fwd(z, col0, w4, b4, wa, ba, wx, bx, lam):
    T = z.shape[0]
    K4, W = w4.shape
    nC = W // LANES
    tm = _tile(T, LRU_TILE, SUBLANES * SUBLANES * LRU_GROUPS)
    seg = tm // (SUBLANES * LRU_GROUPS)
    cx, cg = col0 // LANES, (col0 + W) // LANES

    def body(rx_ref, rg_ref, w4_ref, b4_ref, wa_ref, ba_ref, wx_ref, bx_ref, lam_ref,
             yr_ref, hs_ref, xbuf, a_s, b_s, h_s, p_s, hc):
        @pl.when(pl.program_id(1) == 0)
        def _():
            xbuf[pl.ds(0, LRU_HALO), :] = jnp.zeros((LRU_HALO, LANES), F32)
            hc[...] = jnp.zeros_like(hc)

        xbuf[pl.ds(LRU_HALO, tm), :] = rx_ref[...]
        xr = _tap_sum(xbuf, w4_ref, K4, LRU_HALO - (K4 - 1), 0, tm, False) + b4_ref[...]
        a, mult, _, ig, _ = _lru_gates(xr, wa_ref[...], ba_ref[...], wx_ref[...], bx_ref[...], lam_ref[...])
        a_s[...] = a
        b_s[...] = mult * ig * xr
        cs, cout = _scan_tile(a_s, b_s, h_s, p_s, hc[pl.ds(0, 1), :], seg, False)
        hc[pl.ds(0, 1), :] = cout
        for s in range(SUBLANES * LRU_GROUPS):
            rows = pl.ds(s * seg, seg)
            h = h_s[rows, :] + p_s[rows, :] * cs[s]
            hs_ref[rows, :] = h
            gl, _ = _gelu(rg_ref[rows, :])
            yr_ref[rows, :] = (h * gl).astype(BF16)
        xbuf[pl.ds(0, LRU_HALO), :] = xbuf[pl.ds(tm, LRU_HALO), :]

    vec = pl.BlockSpec((1, LANES), lambda c, i: (0, c))
    mat = pl.BlockSpec((None, LANES, LANES), lambda c, i: (c, 0, 0))
    return pl.pallas_call(
        body, grid=(nC, T // tm),
        in_specs=[pl.BlockSpec((tm, LANES), lambda c, i: (i, cx + c)),
                  pl.BlockSpec((tm, LANES), lambda c, i: (i, cg + c)),
                  pl.BlockSpec((K4, LANES), lambda c, i: (0, c)), vec, mat, vec, mat, vec, vec],
        out_specs=[pl.BlockSpec((tm, LANES), lambda c, i: (i, c)), pl.BlockSpec((tm, LANES), lambda c, i: (i, c))],
        out_shape=[jax.ShapeDtypeStruct((T, W), BF16), jax.ShapeDtypeStruct((T, W), F32)],
        scratch_shapes=[pltpu.VMEM((LRU_HALO + tm, LANES), F32)] + [pltpu.VMEM((tm, LANES), F32)] * 4
        + [pltpu.VMEM((SUBLANES, LANES), F32)],
        compiler_params=_params("parallel", "arbitrary"), name="lru_fwd")(z, z, w4, b4, wa, ba, wx, bx, lam)


def _lru_bwd(dcat, dcol0, hs, z, col0, w4, b4, wa, ba, wx, bx, lam):
    T = z.shape[0]
    K4, W = w4.shape
    assert K4 + 4 == SUBLANES
    nC = W // LANES
    tm = _tile(T, LRU_TILE, SUBLANES * SUBLANES * LRU_GROUPS)
    seg = tm // (SUBLANES * LRU_GROUPS)
    nI = T // tm
    hb = tm // LRU_HALO
    cx, cg, cd = col0 // LANES, (col0 + W) // LANES, dcol0 // LANES

    def body(dyr_ref, hs_ref, hsp_ref, rx_ref, rxp_ref, rg_ref, w4_ref, b4_ref, wa_ref, ba_ref, wx_ref, bx_ref,
             lam_ref, dzx_ref, dzg_ref, st_ref, dwa_ref, dwx_ref, xbuf, hbuf, abuf, a_s, b_s, h_s, p_s, dbuf, gc, anc):
        i = pl.program_id(1)
        ti = nI - 1 - i

        @pl.when(i == 0)
        def _():
            st_ref[...] = jnp.zeros_like(st_ref)
            dwa_ref[...] = jnp.zeros_like(dwa_ref)
            dwx_ref[...] = jnp.zeros_like(dwx_ref)
            gc[...] = jnp.zeros_like(gc)
            anc[...] = jnp.zeros_like(anc)
            dbuf[pl.ds(tm, LRU_HALO), :] = jnp.zeros((LRU_HALO, LANES), F32)

        xbuf[pl.ds(0, LRU_HALO), :] = jnp.where(ti == 0, 0.0, rxp_ref[...])
        xbuf[pl.ds(LRU_HALO, tm), :] = rx_ref[...]
        hbuf[pl.ds(0, LRU_HALO), :] = jnp.where(ti == 0, 0.0, hsp_ref[...])
        hbuf[pl.ds(LRU_HALO, tm), :] = hs_ref[...]

        wa, wx = wa_ref[...], wx_ref[...]
        lam_v = lam_ref[...]
        xr = _tap_sum(xbuf, w4_ref, K4, LRU_HALO - (K4 - 1), 0, tm, False) + b4_ref[...]
        a, mult, r, ig, sp = _lru_gates(xr, wa, ba_ref[...], wx, bx_ref[...], lam_v)

        dyr = dyr_ref[...]
        gl, dgl = _gelu(rg_ref[...])
        dzg_ref[...] = (dyr * hs_ref[...] * dgl).astype(BF16)

        abuf[pl.ds(0, tm), :] = a
        abuf[pl.ds(tm, LRU_HALO), :] = anc[...]
        a_s[...] = abuf[pl.ds(1, tm), :]
        b_s[...] = dyr * gl
        cs, cout = _scan_tile(a_s, b_s, h_s, p_s, gc[pl.ds(0, 1), :], seg, True)
        gc[pl.ds(0, 1), :] = cout
        anc[pl.ds(0, 1), :] = a[0:1, :]
        for s in range(SUBLANES * LRU_GROUPS):
            rows = pl.ds(s * seg, seg)
            b_s[rows, :] = h_s[rows, :] + p_s[rows, :] * cs[s]
        g = b_s[...]

        d_a = g * hbuf[pl.ds(LRU_HALO - 1, tm), :]
        gx_ = g * xr
        d_log_a = d_a * a - (gx_ * ig) * (a * a / mult)
        dga = (d_log_a * (-LRU_C * sp)) * r * (1.0 - r)
        dgx = (gx_ * mult) * ig * (1.0 - ig)
        dga_b, dgx_b = dga.astype(BF16), dgx.astype(BF16)
        dxr = g * mult * ig + _dot_nt(dga_b, wa) + _dot_nt(dgx_b, wx)
        xb = xr.astype(BF16)
        dwa_ref[...] += _dot_tn(xb, dga_b)
        dwx_ref[...] += _dot_tn(xb, dgx_b)
        st_ref[pl.ds(K4, 1), :] += _colsum(dxr)
        st_ref[pl.ds(K4 + 1, 1), :] += _colsum(dga)
        st_ref[pl.ds(K4 + 2, 1), :] += _colsum(dgx)
        st_ref[pl.ds(K4 + 3, 1), :] += _colsum(d_log_a * (-LRU_C * r)) * (-jax.nn.sigmoid(-lam_v))

        dbuf[pl.ds(0, tm), :] = dxr
        for k in range(K4):
            st_ref[pl.ds(k, 1), :] += _colsum(dxr * xbuf[pl.ds(LRU_HALO - (K4 - 1) + k, tm), :])
        dzx_ref[...] = _tap_sum(dbuf, w4_ref, K4, 0, 0, tm, True).astype(BF16)
        dbuf[pl.ds(tm, LRU_HALO), :] = dbuf[pl.ds(0, LRU_HALO), :]

    def rev(col):
        return lambda c, i: (nI - 1 - i, col + c)

    def rev_prev(col):
        return lambda c, i: (jnp.maximum((nI - 1 - i) * hb - 1, 0), col + c)

    vec = pl.BlockSpec((1, LANES), lambda c, i: (0, c))
    mat = pl.BlockSpec((None, LANES, LANES), lambda c, i: (c, 0, 0))
    big = pltpu.VMEM((tm, LANES), F32)
    halo = pltpu.VMEM((tm + LRU_HALO, LANES), F32)
    return pl.pallas_call(
        body, grid=(nC, nI),
        in_specs=[pl.BlockSpec((tm, LANES), rev(cd)),
                  pl.BlockSpec((tm, LANES), rev(0)), pl.BlockSpec((LRU_HALO, LANES), rev_prev(0)),
                  pl.BlockSpec((tm, LANES), rev(cx)), pl.BlockSpec((LRU_HALO, LANES), rev_prev(cx)),
                  pl.BlockSpec((tm, LANES), rev(cg)),
                  pl.BlockSpec((K4, LANES), lambda c, i: (0, c)), vec, mat, vec, mat, vec, vec],
        out_specs=[pl.BlockSpec((tm, LANES), rev(0)), pl.BlockSpec((tm, LANES), rev(0)),
                   pl.BlockSpec((SUBLANES, LANES), lambda c, i: (0, c)), mat, mat],
        out_shape=[jax.ShapeDtypeStruct((T, W), BF16), jax.ShapeDtypeStruct((T, W), BF16),
                   jax.ShapeDtypeStruct((SUBLANES, W), F32),
                   jax.ShapeDtypeStruct((nC, LANES, LANES), F32), jax.ShapeDtypeStruct((nC, LANES, LANES), F32)],
        scratch_shapes=[halo, halo, halo, big, big, big, big, halo,
                        pltpu.VMEM((SUBLANES, LANES), F32), pltpu.VMEM((SUBLANES, LANES), F32)],
        compiler_params=_params("parallel", "arbitrary"), name="lru_bwd")(
            dcat, hs, hs, z, z, z, w4, b4, wa, ba, wx, bx, lam)


def _mix_out_fwd(x, u, yr, wout):
    T, D = x.shape
    C, W = u.shape[1], yr.shape[1]
    tm = _tile(T, TOK_TILE)

    def body(x_ref, u_ref, yr_ref, w_ref, y_ref):
        y_ref[...] = (x_ref[...] + _dot(u_ref[...], w_ref[pl.ds(0, C), :])
                      + _dot(yr_ref[...], w_ref[pl.ds(C, W), :]))

    return pl.pallas_call(
        body, grid=(T // tm,),
        in_specs=[pl.BlockSpec((tm, D), lambda i: (i, 0)), pl.BlockSpec((tm, C), lambda i: (i, 0)),
                  pl.BlockSpec((tm, W), lambda i: (i, 0)),
                  pl.BlockSpec((C + W, D), lambda i: (0, 0), pipeline_mode=pl.Buffered(1))],
        out_specs=pl.BlockSpec((tm, D), lambda i: (i, 0)),
        out_shape=jax.ShapeDtypeStruct((T, D), F32),
        compiler_params=_params("parallel"), name="mix_out_fwd")(x, u, yr, wout)


def _mix_out_bwd(dy, u, yr, wout):
    T, D = dy.shape
    C, W = u.shape[1], yr.shape[1]
    tm = _tile(T, BWD_TILE)

    def body(dy_ref, u_ref, yr_ref, w_ref, dcat_ref, dw_ref):
        @pl.when(pl.program_id(0) == 0)
        def _():
            dw_ref[...] = jnp.zeros_like(dw_ref)

        dyb = dy_ref[...].astype(BF16)
        dcat_ref[...] = _dot_nt(dyb, w_ref[...])
        dw_ref[pl.ds(0, C), :] += _dot_tn(u_ref[...], dyb)
        dw_ref[pl.ds(C, W), :] += _dot_tn(yr_ref[...], dyb)

    return pl.pallas_call(
        body, grid=(T // tm,),
        in_specs=[pl.BlockSpec((tm, D), lambda i: (i, 0)), pl.BlockSpec((tm, C), lambda i: (i, 0)),
                  pl.BlockSpec((tm, W), lambda i: (i, 0)),
                  pl.BlockSpec((C + W, D), lambda i: (0, 0), pipeline_mode=pl.Buffered(1))],
        out_specs=[pl.BlockSpec((tm, C + W), lambda i: (i, 0)), pl.BlockSpec((C + W, D), lambda i: (0, 0))],
        out_shape=[jax.ShapeDtypeStruct((T, C + W), F32), jax.ShapeDtypeStruct((C + W, D), F32)],
        compiler_params=_params("arbitrary"), name="mix_out_bwd")(dy, u, yr, wout)


def _mix_in_bwd(dzc, dzx, dzg, x, dy, g, win):
    T, D = x.shape
    ns, ws = win.shape[0], win.shape[2]
    tm = _tile(T, BWD_TILE)
    parts = []
    for j in range(ns):
        lo = j * ws
        if lo < dzc.shape[1]:
            parts.append((0, lo))
        elif lo < dzc.shape[1] + dzx.shape[1]:
            parts.append((1, lo - dzc.shape[1]))
        else:
            parts.append((2, lo - dzc.shape[1] - dzx.shape[1]))

    def body(dzc_ref, dzx_ref, dzg_ref, x_ref, dy_ref, g_ref, w_ref, dx_ref, dw_ref, dg_ref):
        @pl.when(pl.program_id(0) == 0)
        def _():
            dw_ref[...] = jnp.zeros_like(dw_ref)
            dg_ref[...] = jnp.zeros_like(dg_ref)

        xh, r = _rms_stats(x_ref[...])
        gv = g_ref[...]
        hb = (xh * gv).astype(BF16)
        srcs = (dzc_ref, dzx_ref, dzg_ref)
        dh = jnp.zeros((tm, D), F32)
        for j, (si, off) in enumerate(parts):
            dzj = srcs[si][:, pl.ds(off, ws)]
            dh = dh + _dot_nt(dzj, w_ref[j])
            dw_ref[j] += _dot_tn(hb, dzj)
        dx_ref[...] = dy_ref[...] + _rms_bwd(dh, xh, r, gv)
        dg_ref[...] += _colsum(dh * xh)

    def tok(n):
        return pl.BlockSpec((tm, n), lambda i: (i, 0))

    vec = pl.BlockSpec((1, D), lambda i: (0, 0))
    return pl.pallas_call(
        body, grid=(T // tm,),
        in_specs=[tok(dzc.shape[1]), tok(dzx.shape[1]), tok(dzg.shape[1]), tok(D), tok(D), vec,
                  pl.BlockSpec((ns, D, ws), lambda i: (0, 0, 0), pipeline_mode=pl.Buffered(1))],
        out_specs=[tok(D), pl.BlockSpec((ns, D, ws), lambda i: (0, 0, 0)), vec],
        out_shape=[jax.ShapeDtypeStruct((T, D), F32), jax.ShapeDtypeStruct((ns, D, ws), F32),
                   jax.ShapeDtypeStruct((1, D), F32)],
        compiler_params=_params("arbitrary"), name="mix_in_bwd")(dzc, dzx, dzg, x, dy, g, win)


def _final_loss(x, g, tgt):
    T, D = x.shape
    tm = _tile(T, TOK_TILE)

    def body(x_ref, g_ref, t_ref, dx_ref, loss_ref, dg_ref):
        @pl.when(pl.program_id(0) == 0)
        def _():
            loss_ref[...] = jnp.zeros_like(loss_ref)
            dg_ref[...] = jnp.zeros_like(dg_ref)

        xh, r = _rms_stats(x_ref[...])
        gv = g_ref[...]
        e = xh * gv - t_ref[...]
        loss_ref[...] += 0.5 * jnp.sum(jnp.mean(e * e, axis=-1, keepdims=True))
        dy = e * (1.0 / D)
        dg_ref[...] += _colsum(dy * xh)
        dx_ref[...] = _rms_bwd(dy, xh, r, gv)

    tok = pl.BlockSpec((tm, D), lambda i: (i, 0))
    vec = pl.BlockSpec((1, D), lambda i: (0, 0))
    return pl.pallas_call(
        body, grid=(T // tm,),
        in_specs=[tok, vec, tok],
        out_specs=[tok, pl.BlockSpec((SUBLANES, LANES), lambda i: (0, 0)), vec],
        out_shape=[jax.ShapeDtypeStruct((T, D), F32), jax.ShapeDtypeStruct((SUBLANES, LANES), F32),
                   jax.ShapeDtypeStruct((1, D), F32)],
        compiler_params=_params("arbitrary"), name="final_loss")(x, g, tgt)


def _adamw(w, g, m, v, name):
    R, Cc = w.shape
    tr = _tile(R, max(SUBLANES, (1 << 19) // Cc))
    c1 = 1.0 - ADAM_B1 ** ADAM_STEP
    c2 = 1.0 - ADAM_B2 ** ADAM_STEP

    def body(w_ref, g_ref, m_ref, v_ref, d_ref, nm_ref, nv_ref):
        gv = g_ref[...]
        nm = ADAM_B1 * m_ref[...] + (1.0 - ADAM_B1) * gv
        nv = ADAM_B2 * v_ref[...] + (1.0 - ADAM_B2) * (gv * gv)
        nm_ref[...] = nm
        nv_ref[...] = nv
        d_ref[...] = -ADAM_LR * ((nm / c1) / (jnp.sqrt(nv / c2) + ADAM_EPS) + ADAM_WD * w_ref[...])

    blk = pl.BlockSpec((tr, Cc), lambda i: (i, 0))
    sds = jax.ShapeDtypeStruct((R, Cc), F32)
    return pl.pallas_call(
        body, grid=(R // tr,), in_specs=[blk] * 4, out_specs=[blk] * 3, out_shape=[sds] * 3,
        compiler_params=_params("parallel"), name=name)(w, g, m, v)


def _here():
    return lax.axis_index("x"), lax.axis_index("y"), lax.axis_index("c")


def _chip_at(x, y, m):
    return x ^ (m >> 1), y ^ (m & 1)


ANY = pl.BlockSpec(memory_space=pl.ANY)


def _place_cast(srcs, idx, dtype, name):
    n = len(srcs)
    R, Cc = srcs[0].shape
    tr = _tile(R, max(16, (1 << 18) // Cc), 16)

    def body(i_ref, *refs):
        o_ref = refs[n]
        for k in range(n):
            o_ref[k] = refs[k][...].astype(dtype)

    blk = pl.BlockSpec((tr, Cc), lambda i, s: (i, 0))
    return pl.pallas_call(
        body,
        grid_spec=pltpu.PrefetchScalarGridSpec(
            num_scalar_prefetch=1, grid=(R // tr,), in_specs=[blk] * n,
            out_specs=pl.BlockSpec((n, None, tr, Cc), lambda i, s: (0, s[1], i, 0))),
        out_shape=jax.ShapeDtypeStruct((n, N_CHIPS, R, Cc), dtype),
        compiler_params=_params("parallel"), name=name)(idx, *srcs)


def _gather_weights(lands):
    n = len(lands)

    def body(*refs):
        outs = refs[n:2 * n]
        send1, recv1, send2, recv2 = refs[2 * n:]
        x, y, c = _here()
        own = 2 * x + y

        def half(ref, chip, cc):
            rh = ref.shape[-2] // 2
            lead = (slice(None),) * (len(ref.shape) - 3)
            return ref.at[lead + (chip, pl.ds(cc * rh, rh), slice(None))]

        first = []
        for k in range(n):
            for m in (1, 2, 3):
                px, py = _chip_at(x, y, m)
                cp = pltpu.make_async_remote_copy(
                    src_ref=half(outs[k], own, c), dst_ref=half(outs[k], own, c),
                    send_sem=send1.at[k, m - 1], recv_sem=recv1.at[k, m - 1],
                    device_id=(px, py, c), device_id_type=MESH)
                cp.start()
                first.append(cp)

        passed = []
        for k in range(n):
            for m in (1, 2, 3):
                px, py = _chip_at(x, y, m)
                peer = 2 * px + py
                got = half(outs[k], peer, c)
                pltpu.make_async_remote_copy(
                    src_ref=got, dst_ref=got, send_sem=send1.at[k, m - 1], recv_sem=recv1.at[k, m - 1],
                    device_id=(px, py, c), device_id_type=MESH).wait_recv()
                cp = pltpu.make_async_remote_copy(
                    src_ref=got, dst_ref=got, send_sem=send2.at[k, m - 1], recv_sem=recv2.at[k, m - 1],
                    device_id=(x, y, 1 - c), device_id_type=MESH)
                cp.start()
                passed.append(cp)

        for k in range(n):
            for m in (1, 2, 3):
                px, py = _chip_at(x, y, m)
                other = half(outs[k], 2 * px + py, 1 - c)
                pltpu.make_async_remote_copy(
                    src_ref=other, dst_ref=other, send_sem=send2.at[k, m - 1], recv_sem=recv2.at[k, m - 1],
                    device_id=(x, y, 1 - c), device_id_type=MESH).wait_recv()
        for cp in first + passed:
            cp.wait_send()

    return pl.pallas_call(
        body, in_specs=[ANY] * n, out_specs=[ANY] * n,
        out_shape=[jax.ShapeDtypeStruct(a.shape, a.dtype) for a in lands],
        input_output_aliases={k: k for k in range(n)},
        scratch_shapes=[pltpu.SemaphoreType.DMA((n, 3)), pltpu.SemaphoreType.DMA((n, 3)),
                        pltpu.SemaphoreType.DMA((n, 3)), pltpu.SemaphoreType.DMA((n, 3))],
        name="gather_weights")(*lands)


HBM = pl.BlockSpec(memory_space=pltpu.HBM)
SEM = pl.BlockSpec(memory_space=pltpu.SEMAPHORE)
EFFECT = pltpu.SideEffectType.DATAFLOW_SIDE_EFFECTING


def _in_hbm(a):
    return pltpu.with_memory_space_constraint(a, pltpu.HBM)


def _gather_copies(land_refs, send, recv):
    x, y, c = _here()
    own = 2 * x + y
    cps = []
    for k in range(len(land_refs)):
        lead = (slice(None),) * (len(land_refs[k].shape) - 3)
        mine = land_refs[k].at[lead + (own,)]
        for m in (1, 2, 3):
            px, py = _chip_at(x, y, m)
            cps.append(pltpu.make_async_remote_copy(
                src_ref=mine, dst_ref=mine, send_sem=send.at[3 * k + m - 1], recv_sem=recv.at[3 * k + m - 1],
                device_id=(px, py, c), device_id_type=MESH))
    return cps


def _gather_start(lands, after, name):
    n = len(lands)

    def body(*refs):
        lz = refs[:n]
        send, recv = refs[n + 1], refs[n + 2]
        token = refs[-1]
        for cp in _gather_copies(lz, send, recv):
            cp.start()
        token[...] = jnp.zeros_like(token)

    hbm = [pltpu.HBM(a.shape, a.dtype) for a in lands]
    outs = pl.pallas_call(
        body, name=name,
        in_specs=[HBM] * n + [ANY],
        out_specs=[SEM, SEM] + [HBM] * n + [pl.BlockSpec(memory_space=pltpu.VMEM)],
        out_shape=[pltpu.SemaphoreType.DMA((3 * n,)), pltpu.SemaphoreType.DMA((3 * n,))] + hbm
        + [jax.ShapeDtypeStruct((SUBLANES, LANES), F32)],
        input_output_aliases={k: 2 + k for k in range(n)},
        compiler_params=pltpu.CompilerParams(has_side_effects=EFFECT),
    )(*[_in_hbm(a) for a in lands], after)
    return outs[0], outs[1], outs[2:2 + n], outs[-1]


def _gather_wait(send, recv, lands, after, name):
    n = len(lands)

    def body(*refs):
        lz = refs[:n]
        send_r, recv_r = refs[n], refs[n + 1]
        for cp in _gather_copies(lz, send_r, recv_r):
            cp.wait_send()
            cp.wait_recv()

    hbm = [pltpu.HBM(a.shape, a.dtype) for a in lands]
    return pl.pallas_call(
        body, name=name,
        in_specs=[HBM] * n + [SEM, SEM, ANY],
        out_specs=[HBM] * n, out_shape=hbm,
        input_output_aliases={k: k for k in range(n)},
        compiler_params=pltpu.CompilerParams(has_side_effects=EFFECT),
    )(*lands, send, recv, after)


def _exchange_copies(part_refs, slot_refs, send, recv):
    x, y, c = _here()
    cps = []
    for k in range(len(part_refs)):
        for m in (1, 2, 3):
            px, py = _chip_at(x, y, m)
            cps.append(pltpu.make_async_remote_copy(
                src_ref=part_refs[k].at[2 * px + py], dst_ref=slot_refs[k].at[m - 1],
                send_sem=send.at[3 * k + m - 1], recv_sem=recv.at[3 * k + m - 1],
                device_id=(px, py, c), device_id_type=MESH))
    return cps


def _exchange_start(parts, name):
    n = len(parts)
    lands = [lax.empty((N_CHIPS - 1,) + p.shape[1:], p.dtype) for p in parts]

    def body(*refs):
        ins, lz = refs[:n], refs[n:2 * n]
        send, recv = refs[2 * n], refs[2 * n + 1]
        token = refs[-1]
        for cp in _exchange_copies(ins, lz, send, recv):
            cp.start()
        token[...] = jnp.zeros_like(token)

    hbm = [pltpu.HBM(a.shape, a.dtype) for a in list(parts) + lands]
    outs = pl.pallas_call(
        body, name=name,
        in_specs=[HBM] * (2 * n),
        out_specs=[SEM, SEM] + [HBM] * (2 * n) + [pl.BlockSpec(memory_space=pltpu.VMEM)],
        out_shape=[pltpu.SemaphoreType.DMA((3 * n,)), pltpu.SemaphoreType.DMA((3 * n,))] + hbm
        + [jax.ShapeDtypeStruct((SUBLANES, LANES), F32)],
        input_output_aliases={k: 2 + k for k in range(2 * n)},
        compiler_params=pltpu.CompilerParams(has_side_effects=EFFECT),
    )(*[_in_hbm(a) for a in parts], *[_in_hbm(a) for a in lands])
    return outs[0], outs[1], outs[2:2 + n], outs[2 + n:2 + 2 * n], outs[-1]


def _exchange_wait(send, recv, parts, lands, after, name):
    n = len(parts)

    def body(*refs):
        ins, lz = refs[:n], refs[n:2 * n]
        send_r, recv_r = refs[2 * n], refs[2 * n + 1]
        for cp in _exchange_copies(ins, lz, send_r, recv_r):
            cp.wait_send()
            cp.wait_recv()

    hbm = [pltpu.HBM(a.shape, a.dtype) for a in list(parts) + list(lands)]
    outs = pl.pallas_call(
        body, name=name,
        in_specs=[HBM] * (2 * n) + [SEM, SEM, ANY],
        out_specs=[HBM] * (2 * n), out_shape=hbm,
        input_output_aliases={k: k for k in range(2 * n)},
        compiler_params=pltpu.CompilerParams(has_side_effects=EFFECT),
    )(*parts, *lands, send, recv, after)
    return outs[:n], outs[n:]


def _swap_halves_out(grads, name):
    n = len(grads)
    out_shapes = [jax.ShapeDtypeStruct((g.shape[0], g.shape[1] // 2, g.shape[2]), g.dtype) for g in grads]

    def body(*refs):
        ins, outs = refs[:n], refs[n:2 * n]
        send, recv = refs[2 * n:]
        x, y, c = _here()
        cps = []
        for k in range(n):
            rh = ins[k].shape[1] // 2
            cp = pltpu.make_async_remote_copy(
                src_ref=ins[k].at[:, pl.ds((1 - c) * rh, rh), :], dst_ref=outs[k],
                send_sem=send.at[k], recv_sem=recv.at[k], device_id=(x, y, 1 - c), device_id_type=MESH)
            cp.start()
            cps.append(cp)
        for cp in cps:
            cp.wait()

    return pl.pallas_call(
        body, in_specs=[ANY] * n, out_specs=[ANY] * n, out_shape=out_shapes,
        scratch_shapes=[pltpu.SemaphoreType.DMA((n,)), pltpu.SemaphoreType.DMA((n,))],
        name=name)(*grads)


def _swap_copies(grad_refs, land_refs, send, recv):
    x, y, c = _here()
    cps = []
    for k in range(len(grad_refs)):
        rh = grad_refs[k].shape[1] // 2
        cps.append(pltpu.make_async_remote_copy(
            src_ref=grad_refs[k].at[:, pl.ds((1 - c) * rh, rh), :], dst_ref=land_refs[k],
            send_sem=send.at[k], recv_sem=recv.at[k], device_id=(x, y, 1 - c), device_id_type=MESH))
    return cps


def _swap_start(grads, name):
    n = len(grads)
    lands = [lax.empty((g.shape[0], g.shape[1] // 2, g.shape[2]), g.dtype) for g in grads]

    def body(*refs):
        ins, lz = refs[:n], refs[n:2 * n]
        send, recv = refs[2 * n], refs[2 * n + 1]
        token = refs[-1]
        for cp in _swap_copies(ins, lz, send, recv):
            cp.start()
        token[...] = jnp.zeros_like(token)

    hbm = [pltpu.HBM(a.shape, a.dtype) for a in list(grads) + lands]
    outs = pl.pallas_call(
        body, name=name,
        in_specs=[HBM] * (2 * n),
        out_specs=[SEM, SEM] + [HBM] * (2 * n) + [pl.BlockSpec(memory_space=pltpu.VMEM)],
        out_shape=[pltpu.SemaphoreType.DMA((n,)), pltpu.SemaphoreType.DMA((n,))] + hbm
        + [jax.ShapeDtypeStruct((SUBLANES, LANES), F32)],
        input_output_aliases={k: 2 + k for k in range(2 * n)},
        compiler_params=pltpu.CompilerParams(has_side_effects=EFFECT),
    )(*[_in_hbm(a) for a in grads], *[_in_hbm(a) for a in lands])
    return outs[0], outs[1], outs[2:2 + n], outs[2 + n:2 + 2 * n], outs[-1]


def _swap_wait(send, recv, grads, lands, after, name):
    n = len(grads)

    def body(*refs):
        ins, lz = refs[:n], refs[n:2 * n]
        send_r, recv_r = refs[2 * n], refs[2 * n + 1]
        for cp in _swap_copies(ins, lz, send_r, recv_r):
            cp.wait_send()
            cp.wait_recv()

    hbm = [pltpu.HBM(a.shape, a.dtype) for a in list(grads) + list(lands)]
    outs = pl.pallas_call(
        body, name=name,
        in_specs=[HBM] * (2 * n) + [SEM, SEM, ANY],
        out_specs=[HBM] * (2 * n), out_shape=hbm,
        input_output_aliases={k: k for k in range(2 * n)},
        compiler_params=pltpu.CompilerParams(has_side_effects=EFFECT),
    )(*grads, *lands, send, recv, after)
    return outs[:n], outs[n:]


def _add_cast(g, other, cidx, name):
    ns, R, Cc = g.shape
    rh = R // 2
    tr = _tile(rh, max(16, (1 << 19) // Cc), 16)
    nb = rh // tr

    def body(c_ref, g_ref, o_ref, s_ref):
        s_ref[...] = (g_ref[...] + o_ref[...]).astype(BF16)

    return pl.pallas_call(
        body,
        grid_spec=pltpu.PrefetchScalarGridSpec(
            num_scalar_prefetch=1, grid=(ns, nb),
            in_specs=[pl.BlockSpec((None, tr, Cc), lambda k, i, c: (k, c[0] * nb + i, 0)),
                      pl.BlockSpec((None, tr, Cc), lambda k, i, c: (k, i, 0))],
            out_specs=pl.BlockSpec((None, tr, Cc), lambda k, i, c: (k, i, 0))),
        out_shape=jax.ShapeDtypeStruct((ns, rh, Cc), BF16),
        compiler_params=_params("parallel", "parallel"), name=name)(cidx, g, other)


def _sum_slots(part, got, idx, name):
    ns, rh, Cc = got.shape
    tr = _tile(rh, max(16, (1 << 18) // Cc), 16)
    nb = rh // tr

    def body(i_ref, p_ref, b_ref, o_ref):
        acc = p_ref[...].astype(F32)
        for m in range(ns):
            acc = acc + b_ref[m].astype(F32)
        o_ref[...] = acc

    return pl.pallas_call(
        body,
        grid_spec=pltpu.PrefetchScalarGridSpec(
            num_scalar_prefetch=1, grid=(nb,),
            in_specs=[pl.BlockSpec((None, tr, Cc), lambda i, s: (s[1], i, 0)),
                      pl.BlockSpec((ns, tr, Cc), lambda i, s: (0, i, 0))],
            out_specs=pl.BlockSpec((tr, Cc), lambda i, s: (s[0] * nb + i, 0))),
        out_shape=jax.ShapeDtypeStruct((2 * rh, Cc), F32),
        compiler_params=_params("parallel"), name=name)(idx, part, got)


def _share_halves(blocks, name):
    n = len(blocks)

    def body(*refs):
        ins, outs = refs[:n], refs[n:2 * n]
        send, recv = refs[2 * n:]
        x, y, c = _here()
        cps = []
        for k in range(n):
            rh = outs[k].shape[0] // 2
            mine = outs[k].at[pl.ds(c * rh, rh), :]
            cp = pltpu.make_async_remote_copy(
                src_ref=mine, dst_ref=mine, send_sem=send.at[k], recv_sem=recv.at[k],
                device_id=(x, y, 1 - c), device_id_type=MESH)
            cp.start()
            cps.append(cp)
        for cp in cps:
            cp.wait()

    return pl.pallas_call(
        body, in_specs=[ANY] * n, out_specs=[ANY] * n,
        out_shape=[jax.ShapeDtypeStruct(b.shape, b.dtype) for b in blocks],
        input_output_aliases={k: k for k in range(n)},
        scratch_shapes=[pltpu.SemaphoreType.DMA((n,)), pltpu.SemaphoreType.DMA((n,))],
        name=name)(*blocks)


def _small_copies(p_ref, slot_ref, send, recv):
    x, y, c = _here()
    mine = slot_ref.at[4 * x + 2 * y + c]
    cps = []
    for m in range(1, N_DEV):
        peer = (x ^ (m >> 2), y ^ ((m >> 1) & 1), c ^ (m & 1))
        cps.append(pltpu.make_async_remote_copy(
            src_ref=p_ref, dst_ref=mine, send_sem=send.at[m - 1], recv_sem=recv.at[m - 1],
            device_id=peer, device_id_type=MESH))
    return cps


def _small_start(packed):
    slots = lax.empty((N_DEV,) + packed.shape, packed.dtype)

    def body(p_ref, s_ref, send, recv, p_thru, s_thru, token):
        for cp in _small_copies(p_ref, s_ref, send, recv):
            cp.start()
        token[...] = jnp.zeros_like(token)

    return pl.pallas_call(
        body, name="small_start",
        in_specs=[HBM, HBM],
        out_specs=[SEM, SEM, HBM, HBM, pl.BlockSpec(memory_space=pltpu.VMEM)],
        out_shape=[pltpu.SemaphoreType.DMA((N_DEV - 1,)), pltpu.SemaphoreType.DMA((N_DEV - 1,)),
                   pltpu.HBM(packed.shape, packed.dtype), pltpu.HBM(slots.shape, slots.dtype),
                   jax.ShapeDtypeStruct((SUBLANES, LANES), F32)],
        input_output_aliases={0: 2, 1: 3},
        compiler_params=pltpu.CompilerParams(has_side_effects=EFFECT),
    )(_in_hbm(packed), _in_hbm(slots))


def _small_wait(send, recv, packed, slots, after):
    def body(p_ref, s_ref, send_r, recv_r, after_ref, p_out, s_out):
        for cp in _small_copies(p_ref, s_ref, send_r, recv_r):
            cp.wait_send()
            cp.wait_recv()

    return pl.pallas_call(
        body, name="small_wait",
        in_specs=[HBM, HBM, SEM, SEM, ANY], out_specs=[HBM, HBM],
        out_shape=[pltpu.HBM(packed.shape, packed.dtype), pltpu.HBM(slots.shape, slots.dtype)],
        input_output_aliases={0: 0, 1: 1},
        compiler_params=pltpu.CompilerParams(has_side_effects=EFFECT),
    )(packed, slots, send, recv, after)


def _sum_devices(packed, slots, me):
    n, R, _ = slots.shape
    tr = _tile(R, 256)

    def body(m_ref, p_ref, s_ref, o_ref):
        own = p_ref[...]
        acc = None
        for d in range(n):
            term = jnp.where(m_ref[0] == d, own, s_ref[d])
            acc = term if acc is None else acc + term
        o_ref[...] = acc

    return pl.pallas_call(
        body,
        grid_spec=pltpu.PrefetchScalarGridSpec(
            num_scalar_prefetch=1, grid=(R // tr,),
            in_specs=[pl.BlockSpec((tr, LANES), lambda i, m: (i, 0)),
                      pl.BlockSpec((n, tr, LANES), lambda i, m: (0, i, 0))],
            out_specs=pl.BlockSpec((tr, LANES), lambda i, m: (i, 0))),
        out_shape=jax.ShapeDtypeStruct((R, LANES), F32),
        compiler_params=_params("parallel"), name="sum_devices")(me, packed, slots)


def _pack(arrs):
    rows, parts = [], []
    for a in arrs:
        flat = a.reshape(-1)
        r = -(-flat.shape[0] // (SUBLANES * LANES)) * SUBLANES
        parts.append(jnp.pad(flat, (0, r * LANES - flat.shape[0])).reshape(r, LANES))
        rows.append(r)
    return jnp.concatenate(parts, axis=0), rows


def _unpack(packed, rows, shapes):
    out, r0 = [], 0
    for r, shp in zip(rows, shapes):
        size = math.prod(shp)
        out.append(packed[r0:r0 + r].reshape(-1)[:size].reshape(shp))
        r0 += r
    return out


def _block_diag(w, per):
    H, dh, _ = w.shape
    w4 = w.reshape(H // per, per, dh, dh)
    eye = jnp.eye(per, dtype=w.dtype)
    return (w4[:, :, :, None, :] * eye[None, :, None, :, None]).reshape(H // per, per * dh, per * dh)


def _block_diag_take(d, per):
    n, s, _ = d.shape
    dh = s // per
    d5 = d.reshape(n, per, dh, per, dh)
    return jnp.stack([d5[:, h, :, h, :] for h in range(per)], axis=1).reshape(n * per, dh, dh)


def kernel(x, ffn1_norm, ffn1_w_gate, ffn1_w_up, ffn1_w_down, mix_norm, w_in, conv_dw, conv_dw_bias, conv_ln_g, conv_ln_b, lru_conv_w, lru_conv_b, lru_w_a, lru_b_a, lru_w_x, lru_b_x, lru_lambda, w_out, ffn2_norm, ffn2_w_gate, ffn2_w_up, ffn2_w_down, final_norm, loss_target, m_ffn1_norm, m_ffn1_w_gate, m_ffn1_w_up, m_ffn1_w_down, m_mix_norm, m_w_in, m_conv_dw, m_conv_dw_bias, m_conv_ln_g, m_conv_ln_b, m_lru_conv_w, m_lru_conv_b, m_lru_w_a, m_lru_b_a, m_lru_w_x, m_lru_b_x, m_lru_lambda, m_w_out, m_ffn2_norm, m_ffn2_w_gate, m_ffn2_w_up, m_ffn2_w_down, m_final_norm, v_ffn1_norm, v_ffn1_w_gate, v_ffn1_w_up, v_ffn1_w_down, v_mix_norm, v_w_in, v_conv_dw, v_conv_dw_bias, v_conv_ln_g, v_conv_ln_b, v_lru_conv_w, v_lru_conv_b, v_lru_w_a, v_lru_b_a, v_lru_w_x, v_lru_b_x, v_lru_lambda, v_w_out, v_ffn2_norm, v_ffn2_w_gate, v_ffn2_w_up, v_ffn2_w_down, v_final_norm):
    names = ['ffn1_norm', 'ffn1_w_gate', 'ffn1_w_up', 'ffn1_w_down', 'mix_norm', 'w_in', 'conv_dw', 'conv_dw_bias',
             'conv_ln_g', 'conv_ln_b', 'lru_conv_w', 'lru_conv_b', 'lru_w_a', 'lru_b_a', 'lru_w_x', 'lru_b_x',
             'lru_lambda', 'w_out', 'ffn2_norm', 'ffn2_w_gate', 'ffn2_w_up', 'ffn2_w_down', 'final_norm']
    env = dict(locals())
    W = {n: env[n] for n in names}
    M = {n: env['m_' + n] for n in names}
    V = {n: env['v_' + n] for n in names}

    xi, yi, ci = _here()
    chip = 2 * xi + yi
    cidx = ci.astype(jnp.int32).reshape(1)
    T, D = x.shape[-2], x.shape[-1]
    xs = x.reshape(T, D)
    tgt = loss_target.reshape(T, D)
    K, Cs = conv_dw.shape
    C = conv_dw_bias.shape[0]
    Wl = lru_conv_b.shape[0]
    K4 = lru_conv_w.shape[0]
    heads, dh, _ = lru_w_a.shape
    per = LANES // dh

    def row(v):
        return v.reshape(1, -1)

    tform = ('ffn1_w_gate', 'ffn1_w_up', 'ffn2_w_gate', 'ffn2_w_up')
    for n in tform:
        W[n], M[n], V[n] = W[n].T, M[n].T, V[n].T
    kp = -(-K // SUBLANES) * SUBLANES
    taps = jnp.concatenate([conv_dw, jnp.zeros((kp - K, Cs), F32), lru_conv_w,
                            jnp.zeros((2 * SUBLANES - K4, Cs), F32)], axis=0)
    idx = jnp.stack([ci, chip]).astype(jnp.int32)
    (wff1,) = _gather_weights([_place_cast([W['ffn1_w_gate'], W['ffn1_w_up'], ffn1_w_down], idx, BF16, "place_ffn1")])
    mixl = [_place_cast([w_in], idx, BF16, "place_w_in"), _place_cast([w_out], idx, BF16, "place_w_out"),
            _place_cast([taps], idx, F32, "place_taps")]
    msend, mrecv, mixl, mtok = _gather_start(mixl, wff1, "gather_mix_start")
    ff2l = _place_cast([W['ffn2_w_gate'], W['ffn2_w_up'], ffn2_w_down], idx, BF16, "place_ffn2")
    fsend, frecv, ff2l, ftok = _gather_start([ff2l], mtok, "gather_ffn2_start")
    wa_bd = _block_diag(lru_w_a, per).astype(BF16)
    wx_bd = _block_diag(lru_w_x, per).astype(BF16)

    x1, a1, b1 = _ffn_fwd(xs, row(ffn1_norm) + ftok[0:1, 0:1], wff1, "ffn1_fwd")
    win, wout, taps = _gather_wait(msend, mrecv, mixl, x1, "gather_mix_wait")
    win, wout, taps = win[0], wout.reshape(-1, D), taps[0]
    conv_w_full = taps[:, :K].transpose(1, 0, 2).reshape(K, N_CHIPS * Cs)
    lru_w4_full = taps[:, kp:kp + K4].transpose(1, 0, 2).reshape(K4, N_CHIPS * Cs)
    z = _mix_in_fwd(x1, row(mix_norm), win)
    u, u1 = _conv_fwd(z, conv_w_full, row(conv_dw_bias), row(conv_ln_g), row(conv_ln_b))
    yr, hs = _lru_fwd(z, 2 * C, lru_w4_full, row(lru_conv_b), wa_bd, row(lru_b_a), wx_bd, row(lru_b_x),
                      row(lru_lambda))
    x2 = _mix_out_fwd(x1, u, yr, wout)
    (wff2,) = _gather_wait(fsend, frecv, ff2l, x2, "gather_ffn2_wait")
    x3, a2, b2 = _ffn_fwd(x2, row(ffn2_norm), wff2, "ffn2_fwd")
    dx3, loss_blk, d_final = _final_loss(x3, row(final_norm), tgt)

    dx2, da2, db2, p2, hb2, dyh2, d_ffn2n = _ffn_bwd_tok(dx3, x2, row(ffn2_norm), a2, b2, wff2, "ffn2_bwd")
    dwg2, dwu2, dwd2 = _ffn_wgrad(hb2, dyh2, da2, db2, p2, ftok, "ffn2_wgrad")
    wsend, wrecv, f2g, f2o, wtok = _swap_start([dwg2, dwu2, dwd2], "swap_ffn2_start")
    dcat, dwout = _mix_out_bwd(dx2, u, yr, wout)
    dzc, cst = _conv_bwd(dcat, u1, z, conv_w_full, row(conv_ln_g) + wtok[0:1, 0:1], row(conv_ln_b))
    dzx, dzg, lst, dwa_bd, dwx_bd = _lru_bwd(dcat, C, hs, z, 2 * C, lru_w4_full, row(lru_conv_b), wa_bd,
                                              row(lru_b_a), wx_bd, row(lru_b_x), row(lru_lambda))
    dx1, dwin, d_mixn = _mix_in_bwd(dzc, dzx, dzg, x1, dx2, row(mix_norm), win)

    early_names = ['w_in', 'w_out', 'ffn2_w_gate', 'ffn2_w_up', 'ffn2_w_down']
    mixg = [dwin, dwout.reshape(N_CHIPS, -1, D)]
    mixo = _swap_halves_out(mixg, "swap_halves_mix")
    f2g, f2o = _swap_wait(wsend, wrecv, f2g, f2o, dwin, "swap_ffn2_wait")
    e_parts = [_add_cast(g, o, cidx, "add_cast_" + n)
               for g, o, n in zip(mixg + list(f2g), list(mixo) + list(f2o), early_names)]
    esend, erecv, e_parts, e_lands, etok = _exchange_start(e_parts, "exchange_early_start")

    dx0, da1, db1, p1, hb1, dyh1, d_ffn1n = _ffn_bwd_tok(dx1, xs, row(ffn1_norm) + etok[0:1, 0:1], a1, b1, wff1,
                                                         "ffn1_bwd")

    small_names = ['ffn1_norm', 'mix_norm', 'conv_dw', 'conv_dw_bias', 'conv_ln_g', 'conv_ln_b', 'lru_conv_w',
                   'lru_conv_b', 'lru_w_a', 'lru_b_a', 'lru_w_x', 'lru_b_x', 'lru_lambda', 'ffn2_norm',
                   'final_norm']
    small = {
        'ffn1_norm': d_ffn1n, 'mix_norm': d_mixn, 'conv_dw': cst[:K], 'conv_dw_bias': cst[K + 1],
        'conv_ln_g': cst[K + 2], 'conv_ln_b': cst[K + 3], 'lru_conv_w': lst[:K4], 'lru_conv_b': lst[K4],
        'lru_w_a': _block_diag_take(dwa_bd, per), 'lru_b_a': lst[K4 + 1],
        'lru_w_x': _block_diag_take(dwx_bd, per), 'lru_b_x': lst[K4 + 2], 'lru_lambda': lst[K4 + 3],
        'ffn2_norm': d_ffn2n, 'final_norm': d_final,
    }
    packed, rows = _pack([small[n] for n in small_names] + [loss_blk[0:1, 0:1]])
    ssend, srecv, packed, sslots, stok = _small_start(packed)

    dwg1, dwu1, dwd1 = _ffn_wgrad(hb1, dyh1, da1, db1, p1, stok, "ffn1_wgrad")

    last_names = ['ffn1_w_gate', 'ffn1_w_up', 'ffn1_w_down']
    last = [dwg1, dwu1, dwd1]
    l_parts = [_add_cast(g, o, cidx, "add_cast_" + n)
               for g, o, n in zip(last, _swap_halves_out(last, "swap_halves_last"), last_names)]
    lsend, lrecv, l_parts, l_lands, ltok = _exchange_start(l_parts, "exchange_last_start")
    e_parts, e_slots = _exchange_wait(esend, erecv, e_parts, e_lands, ltok, "exchange_early_wait")
    delta, new_m, new_v = {}, {}, {}

    def finish(group, parts, slots, tag):
        halves = [_sum_slots(p, b, idx, "sum_slots_" + n) for p, b, n in zip(parts, slots, group)]
        for n, g in zip(group, _share_halves(halves, "share_halves_" + tag)):
            G[n] = g
            delta[n], new_m[n], new_v[n] = _adamw(W[n], g, M[n], V[n], "adamw_" + n)

    G = {}
    finish(early_names, e_parts, e_slots, "early")

    full_shapes = [(K, C) if n == 'conv_dw' else (K4, Wl) if n == 'lru_conv_w' else W[n].shape for n in small_names]
    packed, sslots = _small_wait(ssend, srecv, packed, sslots, dwd1)
    summed = _sum_devices(packed, sslots, (4 * xi + 2 * yi + ci).astype(jnp.int32).reshape(1))
    *small_sums, loss_sum = _unpack(summed, rows, full_shapes + [(1, 1)])
    for n, gsum in zip(small_names, small_sums):
        if n == 'conv_dw':
            gsum = lax.dynamic_slice_in_dim(gsum, chip * Cs, Cs, axis=1)
        elif n == 'lru_conv_w':
            gsum = lax.dynamic_slice_in_dim(gsum, chip * lru_conv_w.shape[1], lru_conv_w.shape[1], axis=1)
        G[n] = gsum

    pw, prow = _pack([W[n] for n in small_names])
    pg, _ = _pack([G[n] for n in small_names])
    pm, _ = _pack([M[n] for n in small_names])
    pv, _ = _pack([V[n] for n in small_names])
    sd, sm, sv = _adamw(pw, pg, pm, pv, "adamw_small")
    shapes = [W[n].shape for n in small_names]
    for n, a, b, c_ in zip(small_names, _unpack(sd, prow, shapes), _unpack(sm, prow, shapes),
                           _unpack(sv, prow, shapes)):
        delta[n], new_m[n], new_v[n] = a, b, c_

    done = sd[0:SUBLANES] + delta[early_names[-1]][0:SUBLANES, 0:LANES]
    l_parts, l_slots = _exchange_wait(lsend, lrecv, l_parts, l_lands, done, "exchange_last_wait")
    finish(last_names, l_parts, l_slots, "last")

    loss = loss_sum[0, 0]
    grad_x = dx0.reshape(x.shape)
    for n in tform:
        G[n], delta[n], new_m[n], new_v[n] = G[n].T, delta[n].T, new_m[n].T, new_v[n].T
    return (loss, grad_x, *[G[n] for n in names], *[delta[n] for n in names],
            *[new_m[n] for n in names], *[new_v[n] for n in names])
```

```python
import functools
import math

import jax
import jax.numpy as jnp
from jax import lax
from jax.experimental import pallas as pl
from jax.experimental.pallas import tpu as pltpu

F32 = jnp.float32
BF16 = jnp.bfloat16
MESH = pl.DeviceIdType.MESH

RMS_EPS = 1e-6
LN_EPS = 1e-5
LRU_C = 8.0
FFN_RES_SCALE = 0.5
ADAM_LR = 0.001
ADAM_B1 = 0.9
ADAM_B2 = 0.999
ADAM_EPS = 1e-08
ADAM_WD = 0.01
ADAM_STEP = 10

LANES = 128
SUBLANES = 8
CONV_HALO = 32
LRU_HALO = 8
ROW_CHUNK = 64
VMEM_LIMIT = 56 * 1024 * 1024
N_CHIPS = 4
N_DEV = 8
TOK_TILE = 1024
BWD_TILE = 512
FFN_BWD_TILE = 512
BWD_ROWS = 32
FFN_BWD_CHAIN = 256
CONV_TILE = 512
LRU_TILE = 1024
LRU_GROUPS = 4


def _dot(a, b):
    return jnp.dot(a, b, preferred_element_type=F32)


def _dot_nt(a, b):
    return lax.dot_general(a, b, (((1,), (1,)), ((), ())), preferred_element_type=F32)


def _dot_tn(a, b):
    return lax.dot_general(a, b, (((0,), (0,)), ((), ())), preferred_element_type=F32)


def _tile(n, pref, mult=SUBLANES):
    for t in range(min(pref, n), 0, -1):
        if n % t == 0 and t % mult == 0:
            return t
    return n


def _params(*sem):
    return pltpu.CompilerParams(dimension_semantics=sem, vmem_limit_bytes=VMEM_LIMIT)


def _rms_stats(x):
    r = lax.rsqrt(jnp.mean(x * x, axis=-1, keepdims=True) + RMS_EPS)
    return x * r, r


def _rms_bwd(dh, xh, r, g):
    dxh = dh * g
    return r * (dxh - xh * jnp.mean(dxh * xh, axis=-1, keepdims=True))


def _colsum(v):
    return jnp.sum(v, axis=0, keepdims=True)


def _ffn_fwd(x, g, wff, name):
    T, D = x.shape
    ns, fs = wff.shape[1], wff.shape[2]
    tm = _tile(T, TOK_TILE)
    mc = _tile(tm, FFN_BWD_CHAIN, 16)

    def body(x_ref, g_ref, wg_ref, wu_ref, wd_ref, y_ref, a_ref, b_ref, hb_ref, acc_ref):
        j = pl.program_id(1)

        @pl.when(j == 0)
        def _():
            xh, _ = _rms_stats(x_ref[...])
            hb_ref[...] = (xh * g_ref[...]).astype(BF16)
            acc_ref[...] = jnp.zeros_like(acc_ref)

        for q0 in range(0, tm, mc):
            blk = pl.ds(q0, mc)
            hb = hb_ref[blk, :]
            a = _dot_nt(hb, wg_ref[...])
            b = _dot_nt(hb, wu_ref[...])
            a_ref[blk, :] = a.astype(BF16)
            b_ref[blk, :] = b.astype(BF16)
            p = (a * jax.nn.sigmoid(a) * b).astype(BF16)
            acc_ref[blk, :] += _dot(p, wd_ref[...])

        @pl.when(j == ns - 1)
        def _():
            y_ref[...] = x_ref[...] + FFN_RES_SCALE * acc_ref[...]

    def wspec(n):
        return pl.BlockSpec((None, None, fs, D), lambda i, j: (n, j, 0, 0))

    mid = pl.BlockSpec((None, tm, fs), lambda i, j: (j, i, 0))
    return pl.pallas_call(
        body, grid=(T // tm, ns),
        in_specs=[pl.BlockSpec((tm, D), lambda i, j: (i, 0)), pl.BlockSpec((1, D), lambda i, j: (0, 0)),
                  wspec(0), wspec(1), wspec(2)],
        out_specs=[pl.BlockSpec((tm, D), lambda i, j: (i, 0)), mid, mid],
        out_shape=[jax.ShapeDtypeStruct((T, D), F32), jax.ShapeDtypeStruct((ns, T, fs), BF16),
                   jax.ShapeDtypeStruct((ns, T, fs), BF16)],
        scratch_shapes=[pltpu.VMEM((tm, D), BF16), pltpu.VMEM((tm, D), F32)],
        compiler_params=_params("parallel", "arbitrary"), name=name)(x, g, wff, wff, wff)


def _ffn_bwd_tok(dy, x, g, a, b, wff, name):
    T, D = x.shape
    ns, fs = wff.shape[1], wff.shape[2]
    tm = _tile(T, FFN_BWD_TILE)
    rc = _tile(tm, BWD_ROWS)
    mc = _tile(tm, FFN_BWD_CHAIN, rc)

    def body(dy_ref, x_ref, g_ref, a_ref, b_ref, wg_ref, wu_ref, wd0_ref, wdn_ref,
             dx_ref, da_ref, db_ref, p_ref, hb_ref, dyh_ref, dg_ref, dh_ref, dp_ref):
        i, j = pl.program_id(0), pl.program_id(1)
        cur = dp_ref.at[j % 2]
        nxt = dp_ref.at[(j + 1) % 2]

        @pl.when((i == 0) & (j == 0))
        def _():
            dg_ref[...] = jnp.zeros_like(dg_ref)

        @pl.when(j == 0)
        def _():
            for r0 in range(0, tm, rc):
                rows = pl.ds(r0, rc)
                xh, _ = _rms_stats(x_ref[rows, :])
                hb_ref[rows, :] = (xh * g_ref[...]).astype(BF16)
                dyh_ref[rows, :] = (FFN_RES_SCALE * dy_ref[rows, :]).astype(BF16)
            dh_ref[...] = jnp.zeros_like(dh_ref)
            cur[...] = _dot_nt(dyh_ref[...], wd0_ref[...])

        def chains(with_next):
            for q0 in range(0, tm, mc):
                blk = pl.ds(q0, mc)
                for r0 in range(q0, q0 + mc, rc):
                    rows = pl.ds(r0, rc)
                    av = a_ref[rows, :].astype(F32)
                    bv = b_ref[rows, :].astype(F32)
                    dp = cur[rows, :]
                    s = jax.nn.sigmoid(av)
                    sl = av * s
                    da_ref[rows, :] = (dp * bv * (s * (1.0 + av * (1.0 - s)))).astype(BF16)
                    db_ref[rows, :] = (dp * sl).astype(BF16)
                    p_ref[rows, :] = (sl * bv).astype(BF16)
                if with_next:
                    nxt[blk, :] = _dot_nt(dyh_ref[blk, :], wdn_ref[...])
                dh_ref[blk, :] += _dot(da_ref[blk, :], wg_ref[...]) + _dot(db_ref[blk, :], wu_ref[...])

        pl.when(j < ns - 1)(functools.partial(chains, True))
        pl.when(j == ns - 1)(functools.partial(chains, False))

        @pl.when(j == ns - 1)
        def _():
            gv = g_ref[...]
            dg = jnp.zeros((1, D), F32)
            for r0 in range(0, tm, rc):
                rows = pl.ds(r0, rc)
                xh, r = _rms_stats(x_ref[rows, :])
                dh = dh_ref[rows, :]
                dx_ref[rows, :] = dy_ref[rows, :] + _rms_bwd(dh, xh, r, gv)
                dg = dg + _colsum(dh * xh)
            dg_ref[...] += dg

    def wspec(n):
        return pl.BlockSpec((None, None, fs, D), lambda i, j: (n, j, 0, 0))

    tok = pl.BlockSpec((tm, D), lambda i, j: (i, 0))
    mid = pl.BlockSpec((None, tm, fs), lambda i, j: (j, i, 0))
    vec = pl.BlockSpec((1, D), lambda i, j: (0, 0))
    return pl.pallas_call(
        body, grid=(T // tm, ns),
        in_specs=[tok, tok, vec, mid, mid, wspec(0), wspec(1),
                  pl.BlockSpec((None, None, fs, D), lambda i, j: (2, 0, 0, 0)),
                  pl.BlockSpec((None, None, fs, D), lambda i, j: (2, jnp.minimum(j + 1, ns - 1), 0, 0))],
        out_specs=[tok, mid, mid, mid, tok, tok, vec],
        out_shape=[jax.ShapeDtypeStruct((T, D), F32),
                   jax.ShapeDtypeStruct((ns, T, fs), BF16), jax.ShapeDtypeStruct((ns, T, fs), BF16),
                   jax.ShapeDtypeStruct((ns, T, fs), BF16),
                   jax.ShapeDtypeStruct((T, D), BF16), jax.ShapeDtypeStruct((T, D), BF16),
                   jax.ShapeDtypeStruct((1, D), F32)],
        scratch_shapes=[pltpu.VMEM((tm, D), F32), pltpu.VMEM((2, tm, fs), F32)],
        compiler_params=_params("arbitrary", "arbitrary"), name=name)(dy, x, g, a, b, wff, wff, wff, wff)


def _ffn_wgrad(hb, dyh, da, db, p, after, name):
    T, D = hb.shape
    ns, _, fs = da.shape
    tm = _tile(T, TOK_TILE)

    def body(hb_ref, dyh_ref, da_ref, db_ref, p_ref, after_ref, dwg_ref, dwu_ref, dwd_ref):
        @pl.when(pl.program_id(1) == 0)
        def _():
            dwg_ref[...] = jnp.zeros_like(dwg_ref)
            dwu_ref[...] = jnp.zeros_like(dwu_ref)
            dwd_ref[...] = jnp.zeros_like(dwd_ref)

        hbv = hb_ref[...]
        dwg_ref[...] += _dot_tn(da_ref[...], hbv)
        dwu_ref[...] += _dot_tn(db_ref[...], hbv)
        dwd_ref[...] += _dot_tn(p_ref[...], dyh_ref[...])

    tok = pl.BlockSpec((tm, D), lambda j, i: (i, 0))
    mid = pl.BlockSpec((None, tm, fs), lambda j, i: (j, i, 0))
    wsp = pl.BlockSpec((None, fs, D), lambda j, i: (j, 0, 0))
    sds = jax.ShapeDtypeStruct((ns, fs, D), F32)
    return pl.pallas_call(
        body, grid=(ns, T // tm),
        in_specs=[tok, tok, mid, mid, mid, pl.BlockSpec((SUBLANES, LANES), lambda j, i: (0, 0))],
        out_specs=[wsp, wsp, wsp], out_shape=[sds, sds, sds],
        compiler_params=_params("parallel", "arbitrary"), name=name)(hb, dyh, da, db, p, after)


def _mix_in_fwd(x, g, win):
    T, D = x.shape
    ns, ws = win.shape[0], win.shape[2]
    tm = _tile(T, TOK_TILE)

    def body(x_ref, g_ref, w_ref, z_ref):
        xh, _ = _rms_stats(x_ref[...])
        hb = (xh * g_ref[...]).astype(BF16)
        for j in range(ns):
            z_ref[:, pl.ds(j * ws, ws)] = _dot(hb, w_ref[j])

    return pl.pallas_call(
        body, grid=(T // tm,),
        in_specs=[pl.BlockSpec((tm, D), lambda i: (i, 0)), pl.BlockSpec((1, D), lambda i: (0, 0)),
                  pl.BlockSpec((ns, D, ws), lambda i: (0, 0, 0), pipeline_mode=pl.Buffered(1))],
        out_specs=pl.BlockSpec((tm, ns * ws), lambda i: (i, 0)),
        out_shape=jax.ShapeDtypeStruct((T, ns * ws), F32),
        compiler_params=_params("parallel"), name="mix_in_fwd")(x, g, win)


def _tap_sum(buf, w_ref, ntaps, first_row, r0, rows, flip):
    acc = None
    for k in range(ntaps):
        off = (ntaps - 1 - k) if flip else k
        t = buf[pl.ds(first_row + r0 + off, rows), :] * w_ref[pl.ds(k, 1), :]
        acc = t if acc is None else acc + t
    return acc


def _shift_copies(buf, sh, rows):
    for r in range(1, SUBLANES):
        sh[r - 1, pl.ds(0, rows), :] = buf[pl.ds(r, rows), :]


def _tap_rows(buf, sh, off, r0, rows):
    r = off % SUBLANES
    if r == 0:
        return buf[pl.ds(off + r0, rows), :]
    return sh[r - 1, pl.ds(off - r + r0, rows), :]


def _tap_sum_tiles(buf, sh, w_ref, ntaps, first_row, r0, rows, flip):
    acc = None
    for k in range(ntaps):
        off = first_row + ((ntaps - 1 - k) if flip else k)
        t = _tap_rows(buf, sh, off, r0, rows) * w_ref[pl.ds(k, 1), :]
        acc = t if acc is None else acc + t
    return acc


def _conv_fwd(z, w, bias, lng, lnb):
    T = z.shape[0]
    K, C = w.shape
    tm = _tile(T, CONV_TILE, ROW_CHUNK)
    rc = min(ROW_CHUNK, tm)
    srows = tm + CONV_HALO - SUBLANES

    def body(cv_ref, cg_ref, w_ref, b_ref, g_ref, bb_ref, u_ref, u1_ref, buf, sh):
        @pl.when(pl.program_id(0) == 0)
        def _():
            buf[pl.ds(0, CONV_HALO), :] = jnp.zeros((CONV_HALO, C), F32)

        buf[pl.ds(CONV_HALO, tm), :] = cv_ref[...] * jax.nn.sigmoid(cg_ref[...])
        _shift_copies(buf, sh, srows)
        for r0 in range(0, tm, rc):
            u1 = _tap_sum_tiles(buf, sh, w_ref, K, CONV_HALO - (K - 1), r0, rc, False) + b_ref[...]
            u1_ref[pl.ds(r0, rc), :] = u1
            xc = u1 - jnp.mean(u1, axis=-1, keepdims=True)
            xh = xc * lax.rsqrt(jnp.mean(xc * xc, axis=-1, keepdims=True) + LN_EPS)
            u2 = xh * g_ref[...] + bb_ref[...]
            u_ref[pl.ds(r0, rc), :] = (u2 * jax.nn.sigmoid(u2)).astype(BF16)
        buf[pl.ds(0, CONV_HALO), :] = buf[pl.ds(tm, CONV_HALO), :]

    vec = pl.BlockSpec((1, C), lambda i: (0, 0))
    return pl.pallas_call(
        body, grid=(T // tm,),
        in_specs=[pl.BlockSpec((tm, C), lambda i: (i, 0)), pl.BlockSpec((tm, C), lambda i: (i, 1)),
                  pl.BlockSpec((K, C), lambda i: (0, 0)), vec, vec, vec],
        out_specs=[pl.BlockSpec((tm, C), lambda i: (i, 0)), pl.BlockSpec((tm, C), lambda i: (i, 0))],
        out_shape=[jax.ShapeDtypeStruct((T, C), BF16), jax.ShapeDtypeStruct((T, C), F32)],
        scratch_shapes=[pltpu.VMEM((CONV_HALO + tm, C), F32), pltpu.VMEM((SUBLANES - 1, srows, C), F32)],
        compiler_params=_params("arbitrary"), name="conv_fwd")(z, z, w, bias, lng, lnb)


def _conv_bwd(dcat, u1, z, w, lng, lnb):
    T = z.shape[0]
    K, C = w.shape
    tm = _tile(T, CONV_TILE, ROW_CHUNK)
    rc = min(ROW_CHUNK, tm)
    nI = T // tm
    hb = tm // CONV_HALO
    srows = ((K + 4 + SUBLANES - 1) // SUBLANES) * SUBLANES
    shrows = tm + CONV_HALO - SUBLANES

    def body(du_ref, u1_ref, cv_ref, cg_ref, cvp_ref, cgp_ref, w_ref, g_ref, bb_ref,
             dz_ref, st_ref, u0buf, d1buf, ush, dsh):
        i = pl.program_id(0)
        ti = nI - 1 - i

        @pl.when(i == 0)
        def _():
            st_ref[...] = jnp.zeros_like(st_ref)
            d1buf[pl.ds(tm, CONV_HALO), :] = jnp.zeros((CONV_HALO, C), F32)

        prev = cvp_ref[...] * jax.nn.sigmoid(cgp_ref[...])
        u0buf[pl.ds(0, CONV_HALO), :] = jnp.where(ti == 0, 0.0, prev)
        u0buf[pl.ds(CONV_HALO, tm), :] = cv_ref[...] * jax.nn.sigmoid(cg_ref[...])

        gv = g_ref[...]
        dbias = jnp.zeros((1, C), F32)
        dgain = jnp.zeros((1, C), F32)
        dlnb = jnp.zeros((1, C), F32)
        for r0 in range(0, tm, rc):
            u1 = u1_ref[pl.ds(r0, rc), :]
            xc = u1 - jnp.mean(u1, axis=-1, keepdims=True)
            rstd = lax.rsqrt(jnp.mean(xc * xc, axis=-1, keepdims=True) + LN_EPS)
            xh = xc * rstd
            u2 = xh * gv + bb_ref[...]
            s = jax.nn.sigmoid(u2)
            du2 = du_ref[pl.ds(r0, rc), :] * (s * (1.0 + u2 * (1.0 - s)))
            dgain = dgain + _colsum(du2 * xh)
            dlnb = dlnb + _colsum(du2)
            dxh = du2 * gv
            du1 = rstd * (dxh - jnp.mean(dxh, axis=-1, keepdims=True)
                          - xh * jnp.mean(dxh * xh, axis=-1, keepdims=True))
            dbias = dbias + _colsum(du1)
            d1buf[pl.ds(r0, rc), :] = du1
        st_ref[pl.ds(K + 1, 1), :] += dbias
        st_ref[pl.ds(K + 2, 1), :] += dgain
        st_ref[pl.ds(K + 3, 1), :] += dlnb

        _shift_copies(u0buf, ush, shrows)
        _shift_copies(d1buf, dsh, shrows)
        for k in range(K):
            acc = jnp.zeros((SUBLANES, C), F32)
            for r0 in range(0, tm, rc):
                prod = d1buf[pl.ds(r0, rc), :] * _tap_rows(u0buf, ush, CONV_HALO - (K - 1) + k, r0, rc)
                acc = acc + jnp.sum(prod.reshape(rc // SUBLANES, SUBLANES, C), axis=0)
            st_ref[pl.ds(k, 1), :] += _colsum(acc)

        for r0 in range(0, tm, rc):
            du0 = _tap_sum_tiles(d1buf, dsh, w_ref, K, 0, r0, rc, True)
            cv = cv_ref[pl.ds(r0, rc), :]
            sg = jax.nn.sigmoid(cg_ref[pl.ds(r0, rc), :])
            dz_ref[pl.ds(r0, rc), pl.ds(0, C)] = (du0 * sg).astype(BF16)
            dz_ref[pl.ds(r0, rc), pl.ds(C, C)] = (du0 * cv * sg * (1.0 - sg)).astype(BF16)
        d1buf[pl.ds(tm, CONV_HALO), :] = d1buf[pl.ds(0, CONV_HALO), :]

    def rev(col):
        return lambda i: (nI - 1 - i, col)

    def rev_prev(col):
        return lambda i: (jnp.maximum((nI - 1 - i) * hb - 1, 0), col)

    vec = pl.BlockSpec((1, C), lambda i: (0, 0))
    return pl.pallas_call(
        body, grid=(nI,),
        in_specs=[pl.BlockSpec((tm, C), rev(0)), pl.BlockSpec((tm, C), rev(0)),
                  pl.BlockSpec((tm, C), rev(0)), pl.BlockSpec((tm, C), rev(1)),
                  pl.BlockSpec((CONV_HALO, C), rev_prev(0)), pl.BlockSpec((CONV_HALO, C), rev_prev(1)),
                  pl.BlockSpec((K, C), lambda i: (0, 0)), vec, vec],
        out_specs=[pl.BlockSpec((tm, 2 * C), rev(0)), pl.BlockSpec((srows, C), lambda i: (0, 0))],
        out_shape=[jax.ShapeDtypeStruct((T, 2 * C), BF16), jax.ShapeDtypeStruct((srows, C), F32)],
        scratch_shapes=[pltpu.VMEM((CONV_HALO + tm, C), F32), pltpu.VMEM((tm + CONV_HALO, C), F32),
                        pltpu.VMEM((SUBLANES - 1, shrows, C), F32), pltpu.VMEM((SUBLANES - 1, shrows, C), F32)],
        compiler_params=_params("arbitrary"), name="conv_bwd")(dcat, u1, z, z, z, z, w, lng, lnb)


def _softplus(v):
    return jnp.maximum(v, 0.0) + jnp.log(1.0 + jnp.exp(-jnp.abs(v)))


def _gelu(v):
    c = math.sqrt(2.0 / math.pi)
    t = jnp.tanh(c * (v + 0.044715 * v * v * v))
    gl = 0.5 * v * (1.0 + t)
    dgl = 0.5 * (1.0 + t) + 0.5 * v * (1.0 - t * t) * c * (1.0 + 3.0 * 0.044715 * v * v)
    return gl, dgl


def _lru_gates(xr, wa, ba, wx, bx, lam):
    xb = xr.astype(BF16)
    r = jax.nn.sigmoid(_dot(xb, wa) + ba)
    ig = jax.nn.sigmoid(_dot(xb, wx) + bx)
    sp = _softplus(-lam)
    log_a = -LRU_C * r * sp
    a = jnp.exp(log_a)
    y = 2.0 * log_a
    series = -(y * (1.0 + y * (0.5 + y * (1.0 / 6.0 + y * (1.0 / 24.0)))))
    mult = jnp.sqrt(jnp.where(y > -0.02, series, 1.0 - jnp.exp(y)))
    return a, mult, r, ig, sp


def _scan_tile(a_s, b_s, h_s, p_s, carry, seg, reverse):
    hl = [jnp.zeros((SUBLANES, LANES), F32)] * LRU_GROUPS
    pr = [jnp.ones((SUBLANES, LANES), F32)] * LRU_GROUPS
    for n in range(seg):
        for g in range(LRU_GROUPS):
            rows = pl.ds(g * SUBLANES * seg + ((seg - 1 - n) if reverse else n), SUBLANES, stride=seg)
            av = a_s[rows, :]
            hl[g] = av * hl[g] + b_s[rows, :]
            pr[g] = av * pr[g]
            h_s[rows, :] = hl[g]
            p_s[rows, :] = pr[g]
    nseg = SUBLANES * LRU_GROUPS
    cs = [None] * nseg
    c = carry
    for s in (range(nseg - 1, -1, -1) if reverse else range(nseg)):
        g, r = divmod(s, SUBLANES)
        cs[s] = c
        c = hl[g][r:r + 1, :] + pr[g][r:r + 1, :] * c
    return cs, c


def _lru_fwd(z, col0, w4, b4, wa, ba, wx, bx, lam):
    T = z.shape[0]
    K4, W = w4.shape
    nC = W // LANES
    tm = _tile(T, LRU_TILE, SUBLANES * SUBLANES * LRU_GROUPS)
    seg = tm // (SUBLANES * LRU_GROUPS)
    cx, cg = col0 // LANES, (col0 + W) // LANES

    def body(rx_ref, rg_ref, w4_ref, b4_ref, wa_ref, ba_ref, wx_ref, bx_ref, lam_ref,
             yr_ref, hs_ref, xbuf, a_s, b_s, h_s, p_s, hc):
        @pl.when(pl.program_id(1) == 0)
        def _():
            xbuf[pl.ds(0, LRU_HALO), :] = jnp.zeros((LRU_HALO, LANES), F32)
            hc[...] = jnp.zeros_like(hc)

        xbuf[pl.ds(LRU_HALO, tm), :] = rx_ref[...]
        xr = _tap_sum(xbuf, w4_ref, K4, LRU_HALO - (K4 - 1), 0, tm, False) + b4_ref[...]
        a, mult, _, ig, _ = _lru_gates(xr, wa_ref[...], ba_ref[...], wx_ref[...], bx_ref[...], lam_ref[...])
        a_s[...] = a
        b_s[...] = mult * ig * xr
        cs, cout = _scan_tile(a_s, b_s, h_s, p_s, hc[pl.ds(0, 1), :], seg, False)
        hc[pl.ds(0, 1), :] = cout
        for s in range(SUBLANES * LRU_GROUPS):
            rows = pl.ds(s * seg, seg)
            h = h_s[rows, :] + p_s[rows, :] * cs[s]
            hs_ref[rows, :] = h
            gl, _ = _gelu(rg_ref[rows, :])
            yr_ref[rows, :] = (h * gl).astype(BF16)
        xbuf[pl.ds(0, LRU_HALO), :] = xbuf[pl.ds(tm, LRU_HALO), :]

    vec = pl.BlockSpec((1, LANES), lambda c, i: (0, c))
    mat = pl.BlockSpec((None, LANES, LANES), lambda c, i: (c, 0, 0))
    return pl.pallas_call(
        body, grid=(nC, T // tm),
        in_specs=[pl.BlockSpec((tm, LANES), lambda c, i: (i, cx + c)),
                  pl.BlockSpec((tm, LANES), lambda c, i: (i, cg + c)),
                  pl.BlockSpec((K4, LANES), lambda c, i: (0, c)), vec, mat, vec, mat, vec, vec],
        out_specs=[pl.BlockSpec((tm, LANES), lambda c, i: (i, c)), pl.BlockSpec((tm, LANES), lambda c, i: (i, c))],
        out_shape=[jax.ShapeDtypeStruct((T, W), BF16), jax.ShapeDtypeStruct((T, W), F32)],
        scratch_shapes=[pltpu.VMEM((LRU_HALO + tm, LANES), F32)] + [pltpu.VMEM((tm, LANES), F32)] * 4
        + [pltpu.VMEM((SUBLANES, LANES), F32)],
        compiler_params=_params("parallel", "arbitrary"), name="lru_fwd")(z, z, w4, b4, wa, ba, wx, bx, lam)


def _lru_bwd(dcat, dcol0, hs, z, col0, w4, b4, wa, ba, wx, bx, lam):
    T = z.shape[0]
    K4, W = w4.shape
    assert K4 + 4 == SUBLANES
    nC = W // LANES
    tm = _tile(T, LRU_TILE, SUBLANES * SUBLANES * LRU_GROUPS)
    seg = tm // (SUBLANES * LRU_GROUPS)
    nI = T // tm
    hb = tm // LRU_HALO
    cx, cg, cd = col0 // LANES, (col0 + W) // LANES, dcol0 // LANES

    def body(dyr_ref, hs_ref, hsp_ref, rx_ref, rxp_ref, rg_ref, w4_ref, b4_ref, wa_ref, ba_ref, wx_ref, bx_ref,
             lam_ref, dzx_ref, dzg_ref, st_ref, dwa_ref, dwx_ref, xbuf, hbuf, abuf, a_s, b_s, h_s, p_s, dbuf, gc, anc):
        i = pl.program_id(1)
        ti = nI - 1 - i

        @pl.when(i == 0)
        def _():
            st_ref[...] = jnp.zeros_like(st_ref)
            dwa_ref[...] = jnp.zeros_like(dwa_ref)
            dwx_ref[...] = jnp.zeros_like(dwx_ref)
            gc[...] = jnp.zeros_like(gc)
            anc[...] = jnp.zeros_like(anc)
            dbuf[pl.ds(tm, LRU_HALO), :] = jnp.zeros((LRU_HALO, LANES), F32)

        xbuf[pl.ds(0, LRU_HALO), :] = jnp.where(ti == 0, 0.0, rxp_ref[...])
        xbuf[pl.ds(LRU_HALO, tm), :] = rx_ref[...]
        hbuf[pl.ds(0, LRU_HALO), :] = jnp.where(ti == 0, 0.0, hsp_ref[...])
        hbuf[pl.ds(LRU_HALO, tm), :] = hs_ref[...]

        wa, wx = wa_ref[...], wx_ref[...]
        lam_v = lam_ref[...]
        xr = _tap_sum(xbuf, w4_ref, K4, LRU_HALO - (K4 - 1), 0, tm, False) + b4_ref[...]
        a, mult, r, ig, sp = _lru_gates(xr, wa, ba_ref[...], wx, bx_ref[...], lam_v)

        dyr = dyr_ref[...]
        gl, dgl = _gelu(rg_ref[...])
        dzg_ref[...] = (dyr * hs_ref[...] * dgl).astype(BF16)

        abuf[pl.ds(0, tm), :] = a
        abuf[pl.ds(tm, LRU_HALO), :] = anc[...]
        a_s[...] = abuf[pl.ds(1, tm), :]
        b_s[...] = dyr * gl
        cs, cout = _scan_tile(a_s, b_s, h_s, p_s, gc[pl.ds(0, 1), :], seg, True)
        gc[pl.ds(0, 1), :] = cout
        anc[pl.ds(0, 1), :] = a[0:1, :]
        for s in range(SUBLANES * LRU_GROUPS):
            rows = pl.ds(s * seg, seg)
            b_s[rows, :] = h_s[rows, :] + p_s[rows, :] * cs[s]
        g = b_s[...]

        d_a = g * hbuf[pl.ds(LRU_HALO - 1, tm), :]
        gx_ = g * xr
        d_log_a = d_a * a - (gx_ * ig) * (a * a / mult)
        dga = (d_log_a * (-LRU_C * sp)) * r * (1.0 - r)
        dgx = (gx_ * mult) * ig * (1.0 - ig)
        dga_b, dgx_b = dga.astype(BF16), dgx.astype(BF16)
        dxr = g * mult * ig + _dot_nt(dga_b, wa) + _dot_nt(dgx_b, wx)
        xb = xr.astype(BF16)
        dwa_ref[...] += _dot_tn(xb, dga_b)
        dwx_ref[...] += _dot_tn(xb, dgx_b)
        st_ref[pl.ds(K4, 1), :] += _colsum(dxr)
        st_ref[pl.ds(K4 + 1, 1), :] += _colsum(dga)
        st_ref[pl.ds(K4 + 2, 1), :] += _colsum(dgx)
        st_ref[pl.ds(K4 + 3, 1), :] += _colsum(d_log_a * (-LRU_C * r)) * (-jax.nn.sigmoid(-lam_v))

        dbuf[pl.ds(0, tm), :] = dxr
        for k in range(K4):
            st_ref[pl.ds(k, 1), :] += _colsum(dxr * xbuf[pl.ds(LRU_HALO - (K4 - 1) + k, tm), :])
        dzx_ref[...] = _tap_sum(dbuf, w4_ref, K4, 0, 0, tm, True).astype(BF16)
        dbuf[pl.ds(tm, LRU_HALO), :] = dbuf[pl.ds(0, LRU_HALO), :]

    def rev(col):
        return lambda c, i: (nI - 1 - i, col + c)

    def rev_prev(col):
        return lambda c, i: (jnp.maximum((nI - 1 - i) * hb - 1, 0), col + c)

    vec = pl.BlockSpec((1, LANES), lambda c, i: (0, c))
    mat = pl.BlockSpec((None, LANES, LANES), lambda c, i: (c, 0, 0))
    big = pltpu.VMEM((tm, LANES), F32)
    halo = pltpu.VMEM((tm + LRU_HALO, LANES), F32)
    return pl.pallas_call(
        body, grid=(nC, nI),
        in_specs=[pl.BlockSpec((tm, LANES), rev(cd)),
                  pl.BlockSpec((tm, LANES), rev(0)), pl.BlockSpec((LRU_HALO, LANES), rev_prev(0)),
                  pl.BlockSpec((tm, LANES), rev(cx)), pl.BlockSpec((LRU_HALO, LANES), rev_prev(cx)),
                  pl.BlockSpec((tm, LANES), rev(cg)),
                  pl.BlockSpec((K4, LANES), lambda c, i: (0, c)), vec, mat, vec, mat, vec, vec],
        out_specs=[pl.BlockSpec((tm, LANES), rev(0)), pl.BlockSpec((tm, LANES), rev(0)),
                   pl.BlockSpec((SUBLANES, LANES), lambda c, i: (0, c)), mat, mat],
        out_shape=[jax.ShapeDtypeStruct((T, W), BF16), jax.ShapeDtypeStruct((T, W), BF16),
                   jax.ShapeDtypeStruct((SUBLANES, W), F32),
                   jax.ShapeDtypeStruct((nC, LANES, LANES), F32), jax.ShapeDtypeStruct((nC, LANES, LANES), F32)],
        scratch_shapes=[halo, halo, halo, big, big, big, big, halo,
                        pltpu.VMEM((SUBLANES, LANES), F32), pltpu.VMEM((SUBLANES, LANES), F32)],
        compiler_params=_params("parallel", "arbitrary"), name="lru_bwd")(
            dcat, hs, hs, z, z, z, w4, b4, wa, ba, wx, bx, lam)


def _mix_out_fwd(x, u, yr, wout):
    T, D = x.shape
    C, W = u.shape[1], yr.shape[1]
    tm = _tile(T, TOK_TILE)

    def body(x_ref, u_ref, yr_ref, w_ref, y_ref):
        y_ref[...] = (x_ref[...] + _dot(u_ref[...], w_ref[pl.ds(0, C), :])
                      + _dot(yr_ref[...], w_ref[pl.ds(C, W), :]))

    return pl.pallas_call(
        body, grid=(T // tm,),
        in_specs=[pl.BlockSpec((tm, D), lambda i: (i, 0)), pl.BlockSpec((tm, C), lambda i: (i, 0)),
                  pl.BlockSpec((tm, W), lambda i: (i, 0)),
                  pl.BlockSpec((C + W, D), lambda i: (0, 0), pipeline_mode=pl.Buffered(1))],
        out_specs=pl.BlockSpec((tm, D), lambda i: (i, 0)),
        out_shape=jax.ShapeDtypeStruct((T, D), F32),
        compiler_params=_params("parallel"), name="mix_out_fwd")(x, u, yr, wout)


def _mix_out_bwd(dy, u, yr, wout):
    T, D = dy.shape
    C, W = u.shape[1], yr.shape[1]
    tm = _tile(T, BWD_TILE)

    def body(dy_ref, u_ref, yr_ref, w_ref, dcat_ref, dw_ref):
        @pl.when(pl.program_id(0) == 0)
        def _():
            dw_ref[...] = jnp.zeros_like(dw_ref)

        dyb = dy_ref[...].astype(BF16)
        dcat_ref[...] = _dot_nt(dyb, w_ref[...])
        dw_ref[pl.ds(0, C), :] += _dot_tn(u_ref[...], dyb)
        dw_ref[pl.ds(C, W), :] += _dot_tn(yr_ref[...], dyb)

    return pl.pallas_call(
        body, grid=(T // tm,),
        in_specs=[pl.BlockSpec((tm, D), lambda i: (i, 0)), pl.BlockSpec((tm, C), lambda i: (i, 0)),
                  pl.BlockSpec((tm, W), lambda i: (i, 0)),
                  pl.BlockSpec((C + W, D), lambda i: (0, 0), pipeline_mode=pl.Buffered(1))],
        out_specs=[pl.BlockSpec((tm, C + W), lambda i: (i, 0)), pl.BlockSpec((C + W, D), lambda i: (0, 0))],
        out_shape=[jax.ShapeDtypeStruct((T, C + W), F32), jax.ShapeDtypeStruct((C + W, D), F32)],
        compiler_params=_params("arbitrary"), name="mix_out_bwd")(dy, u, yr, wout)


def _mix_in_bwd(dzc, dzx, dzg, x, dy, g, win):
    T, D = x.shape
    ns, ws = win.shape[0], win.shape[2]
    tm = _tile(T, BWD_TILE)
    parts = []
    for j in range(ns):
        lo = j * ws
        if lo < dzc.shape[1]:
            parts.append((0, lo))
        elif lo < dzc.shape[1] + dzx.shape[1]:
            parts.append((1, lo - dzc.shape[1]))
        else:
            parts.append((2, lo - dzc.shape[1] - dzx.shape[1]))

    def body(dzc_ref, dzx_ref, dzg_ref, x_ref, dy_ref, g_ref, w_ref, dx_ref, dw_ref, dg_ref):
        @pl.when(pl.program_id(0) == 0)
        def _():
            dw_ref[...] = jnp.zeros_like(dw_ref)
            dg_ref[...] = jnp.zeros_like(dg_ref)

        xh, r = _rms_stats(x_ref[...])
        gv = g_ref[...]
        hb = (xh * gv).astype(BF16)
        srcs = (dzc_ref, dzx_ref, dzg_ref)
        dh = jnp.zeros((tm, D), F32)
        for j, (si, off) in enumerate(parts):
            dzj = srcs[si][:, pl.ds(off, ws)]
            dh = dh + _dot_nt(dzj, w_ref[j])
            dw_ref[j] += _dot_tn(hb, dzj)
        dx_ref[...] = dy_ref[...] + _rms_bwd(dh, xh, r, gv)
        dg_ref[...] += _colsum(dh * xh)

    def tok(n):
        return pl.BlockSpec((tm, n), lambda i: (i, 0))

    vec = pl.BlockSpec((1, D), lambda i: (0, 0))
    return pl.pallas_call(
        body, grid=(T // tm,),
        in_specs=[tok(dzc.shape[1]), tok(dzx.shape[1]), tok(dzg.shape[1]), tok(D), tok(D), vec,
                  pl.BlockSpec((ns, D, ws), lambda i: (0, 0, 0), pipeline_mode=pl.Buffered(1))],
        out_specs=[tok(D), pl.BlockSpec((ns, D, ws), lambda i: (0, 0, 0)), vec],
        out_shape=[jax.ShapeDtypeStruct((T, D), F32), jax.ShapeDtypeStruct((ns, D, ws), F32),
                   jax.ShapeDtypeStruct((1, D), F32)],
        compiler_params=_params("arbitrary"), name="mix_in_bwd")(dzc, dzx, dzg, x, dy, g, win)


def _final_loss(x, g, tgt):
    T, D = x.shape
    tm = _tile(T, TOK_TILE)

    def body(x_ref, g_ref, t_ref, dx_ref, loss_ref, dg_ref):
        @pl.when(pl.program_id(0) == 0)
        def _():
            loss_ref[...] = jnp.zeros_like(loss_ref)
            dg_ref[...] = jnp.zeros_like(dg_ref)

        xh, r = _rms_stats(x_ref[...])
        gv = g_ref[...]
        e = xh * gv - t_ref[...]
        loss_ref[...] += 0.5 * jnp.sum(jnp.mean(e * e, axis=-1, keepdims=True))
        dy = e * (1.0 / D)
        dg_ref[...] += _colsum(dy * xh)
        dx_ref[...] = _rms_bwd(dy, xh, r, gv)

    tok = pl.BlockSpec((tm, D), lambda i: (i, 0))
    vec = pl.BlockSpec((1, D), lambda i: (0, 0))
    return pl.pallas_call(
        body, grid=(T // tm,),
        in_specs=[tok, vec, tok],
        out_specs=[tok, pl.BlockSpec((SUBLANES, LANES), lambda i: (0, 0)), vec],
        out_shape=[jax.ShapeDtypeStruct((T, D), F32), jax.ShapeDtypeStruct((SUBLANES, LANES), F32),
                   jax.ShapeDtypeStruct((1, D), F32)],
        compiler_params=_params("arbitrary"), name="final_loss")(x, g, tgt)


def _adamw(w, g, m, v, name):
    R, Cc = w.shape
    tr = _tile(R, max(SUBLANES, (1 << 19) // Cc))
    c1 = 1.0 - ADAM_B1 ** ADAM_STEP
    c2 = 1.0 - ADAM_B2 ** ADAM_STEP

    def body(w_ref, g_ref, m_ref, v_ref, d_ref, nm_ref, nv_ref):
        gv = g_ref[...]
        nm = ADAM_B1 * m_ref[...] + (1.0 - ADAM_B1) * gv
        nv = ADAM_B2 * v_ref[...] + (1.0 - ADAM_B2) * (gv * gv)
        nm_ref[...] = nm
        nv_ref[...] = nv
        d_ref[...] = -ADAM_LR * ((nm / c1) / (jnp.sqrt(nv / c2) + ADAM_EPS) + ADAM_WD * w_ref[...])

    blk = pl.BlockSpec((tr, Cc), lambda i: (i, 0))
    sds = jax.ShapeDtypeStruct((R, Cc), F32)
    return pl.pallas_call(
        body, grid=(R // tr,), in_specs=[blk] * 4, out_specs=[blk] * 3, out_shape=[sds] * 3,
        compiler_params=_params("parallel"), name=name)(w, g, m, v)


def _here():
    return lax.axis_index("x"), lax.axis_index("y"), lax.axis_index("c")


def _chip_at(x, y, m):
    return x ^ (m >> 1), y ^ (m & 1)


ANY = pl.BlockSpec(memory_space=pl.ANY)


def _place_cast(srcs, idx, dtype, name):
    n = len(srcs)
    R, Cc = srcs[0].shape
    tr = _tile(R, max(16, (1 << 18) // Cc), 16)

    def body(i_ref, *refs):
        o_ref = refs[n]
        for k in range(n):
            o_ref[k] = refs[k][...].astype(dtype)

    blk = pl.BlockSpec((tr, Cc), lambda i, s: (i, 0))
    return pl.pallas_call(
        body,
        grid_spec=pltpu.PrefetchScalarGridSpec(
            num_scalar_prefetch=1, grid=(R // tr,), in_specs=[blk] * n,
            out_specs=pl.BlockSpec((n, None, tr, Cc), lambda i, s: (0, s[1], i, 0))),
        out_shape=jax.ShapeDtypeStruct((n, N_CHIPS, R, Cc), dtype),
        compiler_params=_params("parallel"), name=name)(idx, *srcs)


def _gather_weights(lands):
    n = len(lands)

    def body(*refs):
        outs = refs[n:2 * n]
        send1, recv1, send2, recv2 = refs[2 * n:]
        x, y, c = _here()
        own = 2 * x + y

        def half(ref, chip, cc):
            rh = ref.shape[-2] // 2
            lead = (slice(None),) * (len(ref.shape) - 3)
            return ref.at[lead + (chip, pl.ds(cc * rh, rh), slice(None))]

        first = []
        for k in range(n):
            for m in (1, 2, 3):
                px, py = _chip_at(x, y, m)
                cp = pltpu.make_async_remote_copy(
                    src_ref=half(outs[k], own, c), dst_ref=half(outs[k], own, c),
                    send_sem=send1.at[k, m - 1], recv_sem=recv1.at[k, m - 1],
                    device_id=(px, py, c), device_id_type=MESH)
                cp.start()
                first.append(cp)

        passed = []
        for k in range(n):
            for m in (1, 2, 3):
                px, py = _chip_at(x, y, m)
                peer = 2 * px + py
                got = half(outs[k], peer, c)
                pltpu.make_async_remote_copy(
                    src_ref=got, dst_ref=got, send_sem=send1.at[k, m - 1], recv_sem=recv1.at[k, m - 1],
                    device_id=(px, py, c), device_id_type=MESH).wait_recv()
                cp = pltpu.make_async_remote_copy(
                    src_ref=got, dst_ref=got, send_sem=send2.at[k, m - 1], recv_sem=recv2.at[k, m - 1],
                    device_id=(x, y, 1 - c), device_id_type=MESH)
                cp.start()
                passed.append(cp)

        for k in range(n):
            for m in (1, 2, 3):
                px, py = _chip_at(x, y, m)
                other = half(outs[k], 2 * px + py, 1 - c)
                pltpu.make_async_remote_copy(
                    src_ref=other, dst_ref=other, send_sem=send2.at[k, m - 1], recv_sem=recv2.at[k, m - 1],
                    device_id=(x, y, 1 - c), device_id_type=MESH).wait_recv()
        for cp in first + passed:
            cp.wait_send()

    return pl.pallas_call(
        body, in_specs=[ANY] * n, out_specs=[ANY] * n,
        out_shape=[jax.ShapeDtypeStruct(a.shape, a.dtype) for a in lands],
        input_output_aliases={k: k for k in range(n)},
        scratch_shapes=[pltpu.SemaphoreType.DMA((n, 3)), pltpu.SemaphoreType.DMA((n, 3)),
                        pltpu.SemaphoreType.DMA((n, 3)), pltpu.SemaphoreType.DMA((n, 3))],
        name="gather_weights")(*lands)


HBM = pl.BlockSpec(memory_space=pltpu.HBM)
SEM = pl.BlockSpec(memory_space=pltpu.SEMAPHORE)
EFFECT = pltpu.SideEffectType.DATAFLOW_SIDE_EFFECTING


def _in_hbm(a):
    return pltpu.with_memory_space_constraint(a, pltpu.HBM)


def _gather_copies(land_refs, send, recv):
    x, y, c = _here()
    own = 2 * x + y
    cps = []
    for k in range(len(land_refs)):
        lead = (slice(None),) * (len(land_refs[k].shape) - 3)
        mine = land_refs[k].at[lead + (own,)]
        for m in (1, 2, 3):
            px, py = _chip_at(x, y, m)
            cps.append(pltpu.make_async_remote_copy(
                src_ref=mine, dst_ref=mine, send_sem=send.at[3 * k + m - 1], recv_sem=recv.at[3 * k + m - 1],
                device_id=(px, py, c), device_id_type=MESH))
    return cps


def _gather_start(lands, after, name):
    n = len(lands)

    def body(*refs):
        lz = refs[:n]
        send, recv = refs[n + 1], refs[n + 2]
        token = refs[-1]
        for cp in _gather_copies(lz, send, recv):
            cp.start()
        token[...] = jnp.zeros_like(token)

    hbm = [pltpu.HBM(a.shape, a.dtype) for a in lands]
    outs = pl.pallas_call(
        body, name=name,
        in_specs=[HBM] * n + [ANY],
        out_specs=[SEM, SEM] + [HBM] * n + [pl.BlockSpec(memory_space=pltpu.VMEM)],
        out_shape=[pltpu.SemaphoreType.DMA((3 * n,)), pltpu.SemaphoreType.DMA((3 * n,))] + hbm
        + [jax.ShapeDtypeStruct((SUBLANES, LANES), F32)],
        input_output_aliases={k: 2 + k for k in range(n)},
        compiler_params=pltpu.CompilerParams(has_side_effects=EFFECT),
    )(*[_in_hbm(a) for a in lands], after)
    return outs[0], outs[1], outs[2:2 + n], outs[-1]


def _gather_wait(send, recv, lands, after, name):
    n = len(lands)

    def body(*refs):
        lz = refs[:n]
        send_r, recv_r = refs[n], refs[n + 1]
        for cp in _gather_copies(lz, send_r, recv_r):
            cp.wait_send()
            cp.wait_recv()

    hbm = [pltpu.HBM(a.shape, a.dtype) for a in lands]
    return pl.pallas_call(
        body, name=name,
        in_specs=[HBM] * n + [SEM, SEM, ANY],
        out_specs=[HBM] * n, out_shape=hbm,
        input_output_aliases={k: k for k in range(n)},
        compiler_params=pltpu.CompilerParams(has_side_effects=EFFECT),
    )(*lands, send, recv, after)


def _exchange_copies(part_refs, slot_refs, send, recv):
    x, y, c = _here()
    cps = []
    for k in range(len(part_refs)):
        for m in (1, 2, 3):
            px, py = _chip_at(x, y, m)
            cps.append(pltpu.make_async_remote_copy(
                src_ref=part_refs[k].at[2 * px + py], dst_ref=slot_refs[k].at[m - 1],
                send_sem=send.at[3 * k + m - 1], recv_sem=recv.at[3 * k + m - 1],
                device_id=(px, py, c), device_id_type=MESH))
    return cps


def _exchange_start(parts, name):
    n = len(parts)
    lands = [lax.empty((N_CHIPS - 1,) + p.shape[1:], p.dtype) for p in parts]

    def body(*refs):
        ins, lz = refs[:n], refs[n:2 * n]
        send, recv = refs[2 * n], refs[2 * n + 1]
        token = refs[-1]
        for cp in _exchange_copies(ins, lz, send, recv):
            cp.start()
        token[...] = jnp.zeros_like(token)

    hbm = [pltpu.HBM(a.shape, a.dtype) for a in list(parts) + lands]
    outs = pl.pallas_call(
        body, name=name,
        in_specs=[HBM] * (2 * n),
        out_specs=[SEM, SEM] + [HBM] * (2 * n) + [pl.BlockSpec(memory_space=pltpu.VMEM)],
        out_shape=[pltpu.SemaphoreType.DMA((3 * n,)), pltpu.SemaphoreType.DMA((3 * n,))] + hbm
        + [jax.ShapeDtypeStruct((SUBLANES, LANES), F32)],
        input_output_aliases={k: 2 + k for k in range(2 * n)},
        compiler_params=pltpu.CompilerParams(has_side_effects=EFFECT),
    )(*[_in_hbm(a) for a in parts], *[_in_hbm(a) for a in lands])
    return outs[0], outs[1], outs[2:2 + n], outs[2 + n:2 + 2 * n], outs[-1]


def _exchange_wait(send, recv, parts, lands, after, name):
    n = len(parts)

    def body(*refs):
        ins, lz = refs[:n], refs[n:2 * n]
        send_r, recv_r = refs[2 * n], refs[2 * n + 1]
        for cp in _exchange_copies(ins, lz, send_r, recv_r):
            cp.wait_send()
            cp.wait_recv()

    hbm = [pltpu.HBM(a.shape, a.dtype) for a in list(parts) + list(lands)]
    outs = pl.pallas_call(
        body, name=name,
        in_specs=[HBM] * (2 * n) + [SEM, SEM, ANY],
        out_specs=[HBM] * (2 * n), out_shape=hbm,
        input_output_aliases={k: k for k in range(2 * n)},
        compiler_params=pltpu.CompilerParams(has_side_effects=EFFECT),
    )(*parts, *lands, send, recv, after)
    return outs[:n], outs[n:]


def _swap_halves_out(grads, name):
    n = len(grads)
    out_shapes = [jax.ShapeDtypeStruct((g.shape[0], g.shape[1] // 2, g.shape[2]), g.dtype) for g in grads]

    def body(*refs):
        ins, outs = refs[:n], refs[n:2 * n]
        send, recv = refs[2 * n:]
        x, y, c = _here()
        cps = []
        for k in range(n):
            rh = ins[k].shape[1] // 2
            cp = pltpu.make_async_remote_copy(
                src_ref=ins[k].at[:, pl.ds((1 - c) * rh, rh), :], dst_ref=outs[k],
                send_sem=send.at[k], recv_sem=recv.at[k], device_id=(x, y, 1 - c), device_id_type=MESH)
            cp.start()
            cps.append(cp)
        for cp in cps:
            cp.wait()

    return pl.pallas_call(
        body, in_specs=[ANY] * n, out_specs=[ANY] * n, out_shape=out_shapes,
        scratch_shapes=[pltpu.SemaphoreType.DMA((n,)), pltpu.SemaphoreType.DMA((n,))],
        name=name)(*grads)


def _swap_copies(grad_refs, land_refs, send, recv):
    x, y, c = _here()
    cps = []
    for k in range(len(grad_refs)):
        rh = grad_refs[k].shape[1] // 2
        cps.append(pltpu.make_async_remote_copy(
            src_ref=grad_refs[k].at[:, pl.ds((1 - c) * rh, rh), :], dst_ref=land_refs[k],
            send_sem=send.at[k], recv_sem=recv.at[k], device_id=(x, y, 1 - c), device_id_type=MESH))
    return cps


def _swap_start(grads, name):
    n = len(grads)
    lands = [lax.empty((g.shape[0], g.shape[1] // 2, g.shape[2]), g.dtype) for g in grads]

    def body(*refs):
        ins, lz = refs[:n], refs[n:2 * n]
        send, recv = refs[2 * n], refs[2 * n + 1]
        token = refs[-1]
        for cp in _swap_copies(ins, lz, send, recv):
            cp.start()
        token[...] = jnp.zeros_like(token)

    hbm = [pltpu.HBM(a.shape, a.dtype) for a in list(grads) + lands]
    outs = pl.pallas_call(
        body, name=name,
        in_specs=[HBM] * (2 * n),
        out_specs=[SEM, SEM] + [HBM] * (2 * n) + [pl.BlockSpec(memory_space=pltpu.VMEM)],
        out_shape=[pltpu.SemaphoreType.DMA((n,)), pltpu.SemaphoreType.DMA((n,))] + hbm
        + [jax.ShapeDtypeStruct((SUBLANES, LANES), F32)],
        input_output_aliases={k: 2 + k for k in range(2 * n)},
        compiler_params=pltpu.CompilerParams(has_side_effects=EFFECT),
    )(*[_in_hbm(a) for a in grads], *[_in_hbm(a) for a in lands])
    return outs[0], outs[1], outs[2:2 + n], outs[2 + n:2 + 2 * n], outs[-1]


def _swap_wait(send, recv, grads, lands, after, name):
    n = len(grads)

    def body(*refs):
        ins, lz = refs[:n], refs[n:2 * n]
        send_r, recv_r = refs[2 * n], refs[2 * n + 1]
        for cp in _swap_copies(ins, lz, send_r, recv_r):
            cp.wait_send()
            cp.wait_recv()

    hbm = [pltpu.HBM(a.shape, a.dtype) for a in list(grads) + list(lands)]
    outs = pl.pallas_call(
        body, name=name,
        in_specs=[HBM] * (2 * n) + [SEM, SEM, ANY],
        out_specs=[HBM] * (2 * n), out_shape=hbm,
        input_output_aliases={k: k for k in range(2 * n)},
        compiler_params=pltpu.CompilerParams(has_side_effects=EFFECT),
    )(*grads, *lands, send, recv, after)
    return outs[:n], outs[n:]


def _add_cast(g, other, cidx, name):
    ns, R, Cc = g.shape
    rh = R // 2
    tr = _tile(rh, max(16, (1 << 19) // Cc), 16)
    nb = rh // tr

    def body(c_ref, g_ref, o_ref, s_ref):
        s_ref[...] = (g_ref[...] + o_ref[...]).astype(BF16)

    return pl.pallas_call(
        body,
        grid_spec=pltpu.PrefetchScalarGridSpec(
            num_scalar_prefetch=1, grid=(ns, nb),
            in_specs=[pl.BlockSpec((None, tr, Cc), lambda k, i, c: (k, c[0] * nb + i, 0)),
                      pl.BlockSpec((None, tr, Cc), lambda k, i, c: (k, i, 0))],
            out_specs=pl.BlockSpec((None, tr, Cc), lambda k, i, c: (k, i, 0))),
        out_shape=jax.ShapeDtypeStruct((ns, rh, Cc), BF16),
        compiler_params=_params("parallel", "parallel"), name=name)(cidx, g, other)


def _sum_slots(part, got, idx, name):
    ns, rh, Cc = got.shape
    tr = _tile(rh, max(16, (1 << 18) // Cc), 16)
    nb = rh // tr

    def body(i_ref, p_ref, b_ref, o_ref):
        acc = p_ref[...].astype(F32)
        for m in range(ns):
            acc = acc + b_ref[m].astype(F32)
        o_ref[...] = acc

    return pl.pallas_call(
        body,
        grid_spec=pltpu.PrefetchScalarGridSpec(
            num_scalar_prefetch=1, grid=(nb,),
            in_specs=[pl.BlockSpec((None, tr, Cc), lambda i, s: (s[1], i, 0)),
                      pl.BlockSpec((ns, tr, Cc), lambda i, s: (0, i, 0))],
            out_specs=pl.BlockSpec((tr, Cc), lambda i, s: (s[0] * nb + i, 0))),
        out_shape=jax.ShapeDtypeStruct((2 * rh, Cc), F32),
        compiler_params=_params("parallel"), name=name)(idx, part, got)


def _share_halves(blocks, name):
    n = len(blocks)

    def body(*refs):
        ins, outs = refs[:n], refs[n:2 * n]
        send, recv = refs[2 * n:]
        x, y, c = _here()
        cps = []
        for k in range(n):
            rh = outs[k].shape[0] // 2
            mine = outs[k].at[pl.ds(c * rh, rh), :]
            cp = pltpu.make_async_remote_copy(
                src_ref=mine, dst_ref=mine, send_sem=send.at[k], recv_sem=recv.at[k],
                device_id=(x, y, 1 - c), device_id_type=MESH)
            cp.start()
            cps.append(cp)
        for cp in cps:
            cp.wait()

    return pl.pallas_call(
        body, in_specs=[ANY] * n, out_specs=[ANY] * n,
        out_shape=[jax.ShapeDtypeStruct(b.shape, b.dtype) for b in blocks],
        input_output_aliases={k: k for k in range(n)},
        scratch_shapes=[pltpu.SemaphoreType.DMA((n,)), pltpu.SemaphoreType.DMA((n,))],
        name=name)(*blocks)


def _small_copies(p_ref, slot_ref, send, recv):
    x, y, c = _here()
    mine = slot_ref.at[4 * x + 2 * y + c]
    cps = []
    for m in range(1, N_DEV):
        peer = (x ^ (m >> 2), y ^ ((m >> 1) & 1), c ^ (m & 1))
        cps.append(pltpu.make_async_remote_copy(
            src_ref=p_ref, dst_ref=mine, send_sem=send.at[m - 1], recv_sem=recv.at[m - 1],
            device_id=peer, device_id_type=MESH))
    return cps


def _small_start(packed):
    slots = lax.empty((N_DEV,) + packed.shape, packed.dtype)

    def body(p_ref, s_ref, send, recv, p_thru, s_thru, token):
        for cp in _small_copies(p_ref, s_ref, send, recv):
            cp.start()
        token[...] = jnp.zeros_like(token)

    return pl.pallas_call(
        body, name="small_start",
        in_specs=[HBM, HBM],
        out_specs=[SEM, SEM, HBM, HBM, pl.BlockSpec(memory_space=pltpu.VMEM)],
        out_shape=[pltpu.SemaphoreType.DMA((N_DEV - 1,)), pltpu.SemaphoreType.DMA((N_DEV - 1,)),
                   pltpu.HBM(packed.shape, packed.dtype), pltpu.HBM(slots.shape, slots.dtype),
                   jax.ShapeDtypeStruct((SUBLANES, LANES), F32)],
        input_output_aliases={0: 2, 1: 3},
        compiler_params=pltpu.CompilerParams(has_side_effects=EFFECT),
    )(_in_hbm(packed), _in_hbm(slots))


def _small_wait(send, recv, packed, slots, after):
    def body(p_ref, s_ref, send_r, recv_r, after_ref, p_out, s_out):
        for cp in _small_copies(p_ref, s_ref, send_r, recv_r):
            cp.wait_send()
            cp.wait_recv()

    return pl.pallas_call(
        body, name="small_wait",
        in_specs=[HBM, HBM, SEM, SEM, ANY], out_specs=[HBM, HBM],
        out_shape=[pltpu.HBM(packed.shape, packed.dtype), pltpu.HBM(slots.shape, slots.dtype)],
        input_output_aliases={0: 0, 1: 1},
        compiler_params=pltpu.CompilerParams(has_side_effects=EFFECT),
    )(packed, slots, send, recv, after)


def _sum_devices(packed, slots, me):
    n, R, _ = slots.shape
    tr = _tile(R, 1024)

    def body(m_ref, p_ref, s_ref, o_ref):
        own = p_ref[...]
        acc = None
        for d in range(n):
            term = jnp.where(m_ref[0] == d, own, s_ref[d])
            acc = term if acc is None else acc + term
        o_ref[...] = acc

    return pl.pallas_call(
        body,
        grid_spec=pltpu.PrefetchScalarGridSpec(
            num_scalar_prefetch=1, grid=(R // tr,),
            in_specs=[pl.BlockSpec((tr, LANES), lambda i, m: (i, 0)),
                      pl.BlockSpec((n, tr, LANES), lambda i, m: (0, i, 0))],
            out_specs=pl.BlockSpec((tr, LANES), lambda i, m: (i, 0))),
        out_shape=jax.ShapeDtypeStruct((R, LANES), F32),
        compiler_params=_params("parallel"), name="sum_devices")(me, packed, slots)


def _pack(arrs):
    rows, parts = [], []
    for a in arrs:
        flat = a.reshape(-1)
        r = -(-flat.shape[0] // (SUBLANES * LANES)) * SUBLANES
        parts.append(jnp.pad(flat, (0, r * LANES - flat.shape[0])).reshape(r, LANES))
        rows.append(r)
    return jnp.concatenate(parts, axis=0), rows


def _unpack(packed, rows, shapes):
    out, r0 = [], 0
    for r, shp in zip(rows, shapes):
        size = math.prod(shp)
        out.append(packed[r0:r0 + r].reshape(-1)[:size].reshape(shp))
        r0 += r
    return out


def _block_diag(w, per):
    H, dh, _ = w.shape
    w4 = w.reshape(H // per, per, dh, dh)
    eye = jnp.eye(per, dtype=w.dtype)
    return (w4[:, :, :, None, :] * eye[None, :, None, :, None]).reshape(H // per, per * dh, per * dh)


def _block_diag_take(d, per):
    n, s, _ = d.shape
    dh = s // per
    d5 = d.reshape(n, per, dh, per, dh)
    return jnp.stack([d5[:, h, :, h, :] for h in range(per)], axis=1).reshape(n * per, dh, dh)


def kernel(x, ffn1_norm, ffn1_w_gate, ffn1_w_up, ffn1_w_down, mix_norm, w_in, conv_dw, conv_dw_bias, conv_ln_g, conv_ln_b, lru_conv_w, lru_conv_b, lru_w_a, lru_b_a, lru_w_x, lru_b_x, lru_lambda, w_out, ffn2_norm, ffn2_w_gate, ffn2_w_up, ffn2_w_down, final_norm, loss_target, m_ffn1_norm, m_ffn1_w_gate, m_ffn1_w_up, m_ffn1_w_down, m_mix_norm, m_w_in, m_conv_dw, m_conv_dw_bias, m_conv_ln_g, m_conv_ln_b, m_lru_conv_w, m_lru_conv_b, m_lru_w_a, m_lru_b_a, m_lru_w_x, m_lru_b_x, m_lru_lambda, m_w_out, m_ffn2_norm, m_ffn2_w_gate, m_ffn2_w_up, m_ffn2_w_down, m_final_norm, v_ffn1_norm, v_ffn1_w_gate, v_ffn1_w_up, v_ffn1_w_down, v_mix_norm, v_w_in, v_conv_dw, v_conv_dw_bias, v_conv_ln_g, v_conv_ln_b, v_lru_conv_w, v_lru_conv_b, v_lru_w_a, v_lru_b_a, v_lru_w_x, v_lru_b_x, v_lru_lambda, v_w_out, v_ffn2_norm, v_ffn2_w_gate, v_ffn2_w_up, v_ffn2_w_down, v_final_norm):
    names = ['ffn1_norm', 'ffn1_w_gate', 'ffn1_w_up', 'ffn1_w_down', 'mix_norm', 'w_in', 'conv_dw', 'conv_dw_bias',
             'conv_ln_g', 'conv_ln_b', 'lru_conv_w', 'lru_conv_b', 'lru_w_a', 'lru_b_a', 'lru_w_x', 'lru_b_x',
             'lru_lambda', 'w_out', 'ffn2_norm', 'ffn2_w_gate', 'ffn2_w_up', 'ffn2_w_down', 'final_norm']
    env = dict(locals())
    W = {n: env[n] for n in names}
    M = {n: env['m_' + n] for n in names}
    V = {n: env['v_' + n] for n in names}

    xi, yi, ci = _here()
    chip = 2 * xi + yi
    cidx = ci.astype(jnp.int32).reshape(1)
    T, D = x.shape[-2], x.shape[-1]
    xs = x.reshape(T, D)
    tgt = loss_target.reshape(T, D)
    K, Cs = conv_dw.shape
    C = conv_dw_bias.shape[0]
    Wl = lru_conv_b.shape[0]
    K4 = lru_conv_w.shape[0]
    heads, dh, _ = lru_w_a.shape
    per = LANES // dh

    def row(v):
        return v.reshape(1, -1)

    tform = ('ffn1_w_gate', 'ffn1_w_up', 'ffn2_w_gate', 'ffn2_w_up')
    for n in tform:
        W[n], M[n], V[n] = W[n].T, M[n].T, V[n].T
    kp = -(-K // SUBLANES) * SUBLANES
    taps = jnp.concatenate([conv_dw, jnp.zeros((kp - K, Cs), F32), lru_conv_w,
                            jnp.zeros((2 * SUBLANES - K4, Cs), F32)], axis=0)
    idx = jnp.stack([ci, chip]).astype(jnp.int32)
    (wff1,) = _gather_weights([_place_cast([W['ffn1_w_gate'], W['ffn1_w_up'], ffn1_w_down], idx, BF16, "place_ffn1")])
    mixl = [_place_cast([w_in], idx, BF16, "place_w_in"), _place_cast([w_out], idx, BF16, "place_w_out"),
            _place_cast([taps], idx, F32, "place_taps")]
    msend, mrecv, mixl, mtok = _gather_start(mixl, wff1, "gather_mix_start")
    ff2l = _place_cast([W['ffn2_w_gate'], W['ffn2_w_up'], ffn2_w_down], idx, BF16, "place_ffn2")
    fsend, frecv, ff2l, ftok = _gather_start([ff2l], mtok, "gather_ffn2_start")
    wa_bd = _block_diag(lru_w_a, per).astype(BF16)
    wx_bd = _block_diag(lru_w_x, per).astype(BF16)

    x1, a1, b1 = _ffn_fwd(xs, row(ffn1_norm) + ftok[0:1, 0:1], wff1, "ffn1_fwd")
    win, wout, taps = _gather_wait(msend, mrecv, mixl, x1, "gather_mix_wait")
    win, wout, taps = win[0], wout.reshape(-1, D), taps[0]
    conv_w_full = taps[:, :K].transpose(1, 0, 2).reshape(K, N_CHIPS * Cs)
    lru_w4_full = taps[:, kp:kp + K4].transpose(1, 0, 2).reshape(K4, N_CHIPS * Cs)
    z = _mix_in_fwd(x1, row(mix_norm), win)
    u, u1 = _conv_fwd(z, conv_w_full, row(conv_dw_bias), row(conv_ln_g), row(conv_ln_b))
    yr, hs = _lru_fwd(z, 2 * C, lru_w4_full, row(lru_conv_b), wa_bd, row(lru_b_a), wx_bd, row(lru_b_x),
                      row(lru_lambda))
    x2 = _mix_out_fwd(x1, u, yr, wout)
    (wff2,) = _gather_wait(fsend, frecv, ff2l, x2, "gather_ffn2_wait")
    x3, a2, b2 = _ffn_fwd(x2, row(ffn2_norm), wff2, "ffn2_fwd")
    dx3, loss_blk, d_final = _final_loss(x3, row(final_norm), tgt)

    dx2, da2, db2, p2, hb2, dyh2, d_ffn2n = _ffn_bwd_tok(dx3, x2, row(ffn2_norm), a2, b2, wff2, "ffn2_bwd")
    dwg2, dwu2, dwd2 = _ffn_wgrad(hb2, dyh2, da2, db2, p2, ftok, "ffn2_wgrad")
    wsend, wrecv, f2g, f2o, wtok = _swap_start([dwg2, dwu2, dwd2], "swap_ffn2_start")
    dcat, dwout = _mix_out_bwd(dx2, u, yr, wout)
    dzc, cst = _conv_bwd(dcat, u1, z, conv_w_full, row(conv_ln_g) + wtok[0:1, 0:1], row(conv_ln_b))
    dzx, dzg, lst, dwa_bd, dwx_bd = _lru_bwd(dcat, C, hs, z, 2 * C, lru_w4_full, row(lru_conv_b), wa_bd,
                                              row(lru_b_a), wx_bd, row(lru_b_x), row(lru_lambda))
    dx1, dwin, d_mixn = _mix_in_bwd(dzc, dzx, dzg, x1, dx2, row(mix_norm), win)

    early_names = ['w_in', 'w_out', 'ffn2_w_gate', 'ffn2_w_up', 'ffn2_w_down']
    mixg = [dwin, dwout.reshape(N_CHIPS, -1, D)]
    mixo = _swap_halves_out(mixg, "swap_halves_mix")
    f2g, f2o = _swap_wait(wsend, wrecv, f2g, f2o, dwin, "swap_ffn2_wait")
    e_parts = [_add_cast(g, o, cidx, "add_cast_" + n)
               for g, o, n in zip(mixg + list(f2g), list(mixo) + list(f2o), early_names)]
    esend, erecv, e_parts, e_lands, etok = _exchange_start(e_parts, "exchange_early_start")

    dx0, da1, db1, p1, hb1, dyh1, d_ffn1n = _ffn_bwd_tok(dx1, xs, row(ffn1_norm) + etok[0:1, 0:1], a1, b1, wff1,
                                                         "ffn1_bwd")

    small_names = ['ffn1_norm', 'mix_norm', 'conv_dw', 'conv_dw_bias', 'conv_ln_g', 'conv_ln_b', 'lru_conv_w',
                   'lru_conv_b', 'lru_w_a', 'lru_b_a', 'lru_w_x', 'lru_b_x', 'lru_lambda', 'ffn2_norm',
                   'final_norm']
    small = {
        'ffn1_norm': d_ffn1n, 'mix_norm': d_mixn, 'conv_dw': cst[:K], 'conv_dw_bias': cst[K + 1],
        'conv_ln_g': cst[K + 2], 'conv_ln_b': cst[K + 3], 'lru_conv_w': lst[:K4], 'lru_conv_b': lst[K4],
        'lru_w_a': _block_diag_take(dwa_bd, per), 'lru_b_a': lst[K4 + 1],
        'lru_w_x': _block_diag_take(dwx_bd, per), 'lru_b_x': lst[K4 + 2], 'lru_lambda': lst[K4 + 3],
        'ffn2_norm': d_ffn2n, 'final_norm': d_final,
    }
    packed, rows = _pack([small[n] for n in small_names] + [loss_blk[0:1, 0:1]])
    ssend, srecv, packed, sslots, stok = _small_start(packed)

    dwg1, dwu1, dwd1 = _ffn_wgrad(hb1, dyh1, da1, db1, p1, stok, "ffn1_wgrad")

    last_names = ['ffn1_w_gate', 'ffn1_w_up', 'ffn1_w_down']
    last = [dwg1, dwu1, dwd1]
    l_parts = [_add_cast(g, o, cidx, "add_cast_" + n)
               for g, o, n in zip(last, _swap_halves_out(last, "swap_halves_last"), last_names)]
    lsend, lrecv, l_parts, l_lands, ltok = _exchange_start(l_parts, "exchange_last_start")
    e_parts, e_slots = _exchange_wait(esend, erecv, e_parts, e_lands, ltok, "exchange_early_wait")
    delta, new_m, new_v = {}, {}, {}

    def finish(group, parts, slots, tag):
        halves = [_sum_slots(p, b, idx, "sum_slots_" + n) for p, b, n in zip(parts, slots, group)]
        for n, g in zip(group, _share_halves(halves, "share_halves_" + tag)):
            G[n] = g
            delta[n], new_m[n], new_v[n] = _adamw(W[n], g, M[n], V[n], "adamw_" + n)

    G = {}
    finish(early_names, e_parts, e_slots, "early")

    full_shapes = [(K, C) if n == 'conv_dw' else (K4, Wl) if n == 'lru_conv_w' else W[n].shape for n in small_names]
    packed, sslots = _small_wait(ssend, srecv, packed, sslots, dwd1)
    summed = _sum_devices(packed, sslots, (4 * xi + 2 * yi + ci).astype(jnp.int32).reshape(1))
    *small_sums, loss_sum = _unpack(summed, rows, full_shapes + [(1, 1)])
    for n, gsum in zip(small_names, small_sums):
        if n == 'conv_dw':
            gsum = lax.dynamic_slice_in_dim(gsum, chip * Cs, Cs, axis=1)
        elif n == 'lru_conv_w':
            gsum = lax.dynamic_slice_in_dim(gsum, chip * lru_conv_w.shape[1], lru_conv_w.shape[1], axis=1)
        G[n] = gsum

    pw, prow = _pack([W[n] for n in small_names])
    pg, _ = _pack([G[n] for n in small_names])
    pm, _ = _pack([M[n] for n in small_names])
    pv, _ = _pack([V[n] for n in small_names])
    sd, sm, sv = _adamw(pw, pg, pm, pv, "adamw_small")
    shapes = [W[n].shape for n in small_names]
    for n, a, b, c_ in zip(small_names, _unpack(sd, prow, shapes), _unpack(sm, prow, shapes),
                           _unpack(sv, prow, shapes)):
        delta[n], new_m[n], new_v[n] = a, b, c_

    done = sd[0:SUBLANES] + delta[early_names[-1]][0:SUBLANES, 0:LANES]
    l_parts, l_slots = _exchange_wait(lsend, lrecv, l_parts, l_lands, done, "exchange_last_wait")
    finish(last_names, l_parts, l_slots, "last")

    loss = loss_sum[0, 0]
    grad_x = dx0.reshape(x.shape)
    for n in tform:
        G[n], delta[n], new_m[n], new_v[n] = G[n].T, delta[n].T, new_m[n].T, new_v[n].T
    return (loss, grad_x, *[G[n] for n in names], *[delta[n] for n in names],
            *[new_m[n] for n in names], *[new_v[n] for n in names])
```

```python
import functools
import math

import jax
import jax.numpy as jnp
from jax import lax
from jax.experimental import pallas as pl
from jax.experimental.pallas import tpu as pltpu

F32 = jnp.float32
BF16 = jnp.bfloat16
MESH = pl.DeviceIdType.MESH

RMS_EPS = 1e-6
LN_EPS = 1e-5
LRU_C = 8.0
FFN_RES_SCALE = 0.5
ADAM_LR = 0.001
ADAM_B1 = 0.9
ADAM_B2 = 0.999
ADAM_EPS = 1e-08
ADAM_WD = 0.01
ADAM_STEP = 10

LANES = 128
SUBLANES = 8
CONV_HALO = 32
LRU_HALO = 8
ROW_CHUNK = 64
VMEM_LIMIT = 56 * 1024 * 1024
N_CHIPS = 4
N_DEV = 8
TOK_TILE = 1024
BWD_TILE = 512
FFN_BWD_TILE = 512
BWD_ROWS = 32
FFN_BWD_CHAIN = 256
CONV_TILE = 512
LRU_TILE = 1024
LRU_GROUPS = 4


def _dot(a, b):
    return jnp.dot(a, b, preferred_element_type=F32)


def _dot_nt(a, b):
    return lax.dot_general(a, b, (((1,), (1,)), ((), ())), preferred_element_type=F32)


def _dot_tn(a, b):
    return lax.dot_general(a, b, (((0,), (0,)), ((), ())), preferred_element_type=F32)


def _tile(n, pref, mult=SUBLANES):
    for t in range(min(pref, n), 0, -1):
        if n % t == 0 and t % mult == 0:
            return t
    return n


def _params(*sem):
    return pltpu.CompilerParams(dimension_semantics=sem, vmem_limit_bytes=VMEM_LIMIT)


def _rms_stats(x):
    r = lax.rsqrt(jnp.mean(x * x, axis=-1, keepdims=True) + RMS_EPS)
    return x * r, r


def _rms_bwd(dh, xh, r, g):
    dxh = dh * g
    return r * (dxh - xh * jnp.mean(dxh * xh, axis=-1, keepdims=True))


def _colsum(v):
    return jnp.sum(v, axis=0, keepdims=True)


def _ffn_fwd(x, g, wff, name):
    T, D = x.shape
    ns, fs = wff.shape[1], wff.shape[2]
    tm = _tile(T, TOK_TILE)
    mc = _tile(tm, FFN_BWD_CHAIN, 16)

    def body(x_ref, g_ref, wg_ref, wu_ref, wd_ref, y_ref, a_ref, b_ref, hb_ref, acc_ref):
        j = pl.program_id(1)

        @pl.when(j == 0)
        def _():
            xh, _ = _rms_stats(x_ref[...])
            hb_ref[...] = (xh * g_ref[...]).astype(BF16)
            acc_ref[...] = jnp.zeros_like(acc_ref)

        for q0 in range(0, tm, mc):
            blk = pl.ds(q0, mc)
            hb = hb_ref[blk, :]
            a = _dot_nt(hb, wg_ref[...])
            b = _dot_nt(hb, wu_ref[...])
            a_ref[blk, :] = a.astype(BF16)
            b_ref[blk, :] = b.astype(BF16)
            p = (a * jax.nn.sigmoid(a) * b).astype(BF16)
            acc_ref[blk, :] += _dot(p, wd_ref[...])

        @pl.when(j == ns - 1)
        def _():
            y_ref[...] = x_ref[...] + FFN_RES_SCALE * acc_ref[...]

    def wspec(n):
        return pl.BlockSpec((None, None, fs, D), lambda i, j: (n, j, 0, 0))

    mid = pl.BlockSpec((None, tm, fs), lambda i, j: (j, i, 0))
    return pl.pallas_call(
        body, grid=(T // tm, ns),
        in_specs=[pl.BlockSpec((tm, D), lambda i, j: (i, 0)), pl.BlockSpec((1, D), lambda i, j: (0, 0)),
                  wspec(0), wspec(1), wspec(2)],
        out_specs=[pl.BlockSpec((tm, D), lambda i, j: (i, 0)), mid, mid],
        out_shape=[jax.ShapeDtypeStruct((T, D), F32), jax.ShapeDtypeStruct((ns, T, fs), BF16),
                   jax.ShapeDtypeStruct((ns, T, fs), BF16)],
        scratch_shapes=[pltpu.VMEM((tm, D), BF16), pltpu.VMEM((tm, D), F32)],
        compiler_params=_params("parallel", "arbitrary"), name=name)(x, g, wff, wff, wff)


def _ffn_bwd_tok(dy, x, g, a, b, wff, name):
    T, D = x.shape
    ns, fs = wff.shape[1], wff.shape[2]
    tm = _tile(T, FFN_BWD_TILE)
    rc = _tile(tm, BWD_ROWS)
    mc = _tile(tm, FFN_BWD_CHAIN, rc)

    def body(dy_ref, x_ref, g_ref, a_ref, b_ref, wg_ref, wu_ref, wd0_ref, wdn_ref,
             dx_ref, da_ref, db_ref, p_ref, hb_ref, dyh_ref, dg_ref, dh_ref, dp_ref):
        i, j = pl.program_id(0), pl.program_id(1)
        cur = dp_ref.at[j % 2]
        nxt = dp_ref.at[(j + 1) % 2]

        @pl.when((i == 0) & (j == 0))
        def _():
            dg_ref[...] = jnp.zeros_like(dg_ref)

        @pl.when(j == 0)
        def _():
            for r0 in range(0, tm, rc):
                rows = pl.ds(r0, rc)
                xh, _ = _rms_stats(x_ref[rows, :])
                hb_ref[rows, :] = (xh * g_ref[...]).astype(BF16)
                dyh_ref[rows, :] = (FFN_RES_SCALE * dy_ref[rows, :]).astype(BF16)
            dh_ref[...] = jnp.zeros_like(dh_ref)
            cur[...] = _dot_nt(dyh_ref[...], wd0_ref[...])

        def chains(with_next):
            for q0 in range(0, tm, mc):
                blk = pl.ds(q0, mc)
                for r0 in range(q0, q0 + mc, rc):
                    rows = pl.ds(r0, rc)
                    av = a_ref[rows, :].astype(F32)
                    bv = b_ref[rows, :].astype(F32)
                    dp = cur[rows, :]
                    s = jax.nn.sigmoid(av)
                    sl = av * s
                    da_ref[rows, :] = (dp * bv * (s * (1.0 + av * (1.0 - s)))).astype(BF16)
                    db_ref[rows, :] = (dp * sl).astype(BF16)
                    p_ref[rows, :] = (sl * bv).astype(BF16)
                if with_next:
                    nxt[blk, :] = _dot_nt(dyh_ref[blk, :], wdn_ref[...])
                dh_ref[blk, :] += _dot(da_ref[blk, :], wg_ref[...]) + _dot(db_ref[blk, :], wu_ref[...])

        pl.when(j < ns - 1)(functools.partial(chains, True))
        pl.when(j == ns - 1)(functools.partial(chains, False))

        @pl.when(j == ns - 1)
        def _():
            gv = g_ref[...]
            dg = jnp.zeros((1, D), F32)
            for r0 in range(0, tm, rc):
                rows = pl.ds(r0, rc)
                xh, r = _rms_stats(x_ref[rows, :])
                dh = dh_ref[rows, :]
                dx_ref[rows, :] = dy_ref[rows, :] + _rms_bwd(dh, xh, r, gv)
                dg = dg + _colsum(dh * xh)
            dg_ref[...] += dg

    def wspec(n):
        return pl.BlockSpec((None, None, fs, D), lambda i, j: (n, j, 0, 0))

    tok = pl.BlockSpec((tm, D), lambda i, j: (i, 0))
    mid = pl.BlockSpec((None, tm, fs), lambda i, j: (j, i, 0))
    vec = pl.BlockSpec((1, D), lambda i, j: (0, 0))
    return pl.pallas_call(
        body, grid=(T // tm, ns),
        in_specs=[tok, tok, vec, mid, mid, wspec(0), wspec(1),
                  pl.BlockSpec((None, None, fs, D), lambda i, j: (2, 0, 0, 0)),
                  pl.BlockSpec((None, None, fs, D), lambda i, j: (2, jnp.minimum(j + 1, ns - 1), 0, 0))],
        out_specs=[tok, mid, mid, mid, tok, tok, vec],
        out_shape=[jax.ShapeDtypeStruct((T, D), F32),
                   jax.ShapeDtypeStruct((ns, T, fs), BF16), jax.ShapeDtypeStruct((ns, T, fs), BF16),
                   jax.ShapeDtypeStruct((ns, T, fs), BF16),
                   jax.ShapeDtypeStruct((T, D), BF16), jax.ShapeDtypeStruct((T, D), BF16),
                   jax.ShapeDtypeStruct((1, D), F32)],
        scratch_shapes=[pltpu.VMEM((tm, D), F32), pltpu.VMEM((2, tm, fs), F32)],
        compiler_params=_params("arbitrary", "arbitrary"), name=name)(dy, x, g, a, b, wff, wff, wff, wff)


def _ffn_wgrad(lhs, rhs, after, name):
    n = len(lhs)
    T, D = rhs.shape
    ns, _, fs = lhs[0].shape
    tm = _tile(T, TOK_TILE)

    def body(*refs):
        rhs_ref, lhs_refs, out_refs = refs[0], refs[1:1 + n], refs[2 + n:]

        @pl.when(pl.program_id(1) == 0)
        def _():
            for o in out_refs:
                o[...] = jnp.zeros_like(o)

        rv = rhs_ref[...]
        for l, o in zip(lhs_refs, out_refs):
            o[...] += _dot_tn(l[...], rv)

    tok = pl.BlockSpec((tm, D), lambda j, i: (i, 0))
    mid = pl.BlockSpec((None, tm, fs), lambda j, i: (j, i, 0))
    wsp = pl.BlockSpec((None, fs, D), lambda j, i: (j, 0, 0))
    sds = jax.ShapeDtypeStruct((ns, fs, D), F32)
    return pl.pallas_call(
        body, grid=(ns, T // tm),
        in_specs=[tok] + [mid] * n + [pl.BlockSpec((SUBLANES, LANES), lambda j, i: (0, 0))],
        out_specs=[wsp] * n, out_shape=[sds] * n,
        compiler_params=_params("parallel", "arbitrary"), name=name)(rhs, *lhs, after)


def _mix_in_fwd(x, g, win):
    T, D = x.shape
    ns, ws = win.shape[0], win.shape[2]
    tm = _tile(T, TOK_TILE)

    def body(x_ref, g_ref, w_ref, z_ref):
        xh, _ = _rms_stats(x_ref[...])
        hb = (xh * g_ref[...]).astype(BF16)
        for j in range(ns):
            z_ref[:, pl.ds(j * ws, ws)] = _dot(hb, w_ref[j])

    return pl.pallas_call(
        body, grid=(T // tm,),
        in_specs=[pl.BlockSpec((tm, D), lambda i: (i, 0)), pl.BlockSpec((1, D), lambda i: (0, 0)),
                  pl.BlockSpec((ns, D, ws), lambda i: (0, 0, 0), pipeline_mode=pl.Buffered(1))],
        out_specs=pl.BlockSpec((tm, ns * ws), lambda i: (i, 0)),
        out_shape=jax.ShapeDtypeStruct((T, ns * ws), F32),
        compiler_params=_params("parallel"), name="mix_in_fwd")(x, g, win)


def _tap_sum(buf, w_ref, ntaps, first_row, r0, rows, flip):
    acc = None
    for k in range(ntaps):
        off = (ntaps - 1 - k) if flip else k
        t = buf[pl.ds(first_row + r0 + off, rows), :] * w_ref[pl.ds(k, 1), :]
        acc = t if acc is None else acc + t
    return acc


def _shift_copies(buf, sh, rows):
    for r in range(1, SUBLANES):
        sh[r - 1, pl.ds(0, rows), :] = buf[pl.ds(r, rows), :]


def _tap_rows(buf, sh, off, r0, rows):
    r = off % SUBLANES
    if r == 0:
        return buf[pl.ds(off + r0, rows), :]
    return sh[r - 1, pl.ds(off - r + r0, rows), :]


def _tap_sum_tiles(buf, sh, w_ref, ntaps, first_row, r0, rows, flip):
    acc = None
    for k in range(ntaps):
        off = first_row + ((ntaps - 1 - k) if flip else k)
        t = _tap_rows(buf, sh, off, r0, rows) * w_ref[pl.ds(k, 1), :]
        acc = t if acc is None else acc + t
    return acc


def _conv_fwd(z, w, bias, lng, lnb):
    T = z.shape[0]
    K, C = w.shape
    tm = _tile(T, CONV_TILE, ROW_CHUNK)
    rc = min(ROW_CHUNK, tm)
    srows = tm + CONV_HALO - SUBLANES

    def body(cv_ref, cg_ref, w_ref, b_ref, g_ref, bb_ref, u_ref, u1_ref, buf, sh):
        @pl.when(pl.program_id(0) == 0)
        def _():
            buf[pl.ds(0, CONV_HALO), :] = jnp.zeros((CONV_HALO, C), F32)

        buf[pl.ds(CONV_HALO, tm), :] = cv_ref[...] * jax.nn.sigmoid(cg_ref[...])
        _shift_copies(buf, sh, srows)
        for r0 in range(0, tm, rc):
            u1 = _tap_sum_tiles(buf, sh, w_ref, K, CONV_HALO - (K - 1), r0, rc, False) + b_ref[...]
            u1_ref[pl.ds(r0, rc), :] = u1
            xc = u1 - jnp.mean(u1, axis=-1, keepdims=True)
            xh = xc * lax.rsqrt(jnp.mean(xc * xc, axis=-1, keepdims=True) + LN_EPS)
            u2 = xh * g_ref[...] + bb_ref[...]
            u_ref[pl.ds(r0, rc), :] = (u2 * jax.nn.sigmoid(u2)).astype(BF16)
        buf[pl.ds(0, CONV_HALO), :] = buf[pl.ds(tm, CONV_HALO), :]

    vec = pl.BlockSpec((1, C), lambda i: (0, 0))
    return pl.pallas_call(
        body, grid=(T // tm,),
        in_specs=[pl.BlockSpec((tm, C), lambda i: (i, 0)), pl.BlockSpec((tm, C), lambda i: (i, 1)),
                  pl.BlockSpec((K, C), lambda i: (0, 0)), vec, vec, vec],
        out_specs=[pl.BlockSpec((tm, C), lambda i: (i, 0)), pl.BlockSpec((tm, C), lambda i: (i, 0))],
        out_shape=[jax.ShapeDtypeStruct((T, C), BF16), jax.ShapeDtypeStruct((T, C), F32)],
        scratch_shapes=[pltpu.VMEM((CONV_HALO + tm, C), F32), pltpu.VMEM((SUBLANES - 1, srows, C), F32)],
        compiler_params=_params("arbitrary"), name="conv_fwd")(z, z, w, bias, lng, lnb)


def _conv_bwd(dcat, u1, z, w, lng, lnb):
    T = z.shape[0]
    K, C = w.shape
    tm = _tile(T, CONV_TILE, ROW_CHUNK)
    rc = min(ROW_CHUNK, tm)
    nI = T // tm
    hb = tm // CONV_HALO
    srows = ((K + 4 + SUBLANES - 1) // SUBLANES) * SUBLANES
    shrows = tm + CONV_HALO - SUBLANES

    def body(du_ref, u1_ref, cv_ref, cg_ref, cvp_ref, cgp_ref, w_ref, g_ref, bb_ref,
             dz_ref, st_ref, u0buf, d1buf, ush, dsh):
        i = pl.program_id(0)
        ti = nI - 1 - i

        @pl.when(i == 0)
        def _():
            st_ref[...] = jnp.zeros_like(st_ref)
            d1buf[pl.ds(tm, CONV_HALO), :] = jnp.zeros((CONV_HALO, C), F32)

        prev = cvp_ref[...] * jax.nn.sigmoid(cgp_ref[...])
        u0buf[pl.ds(0, CONV_HALO), :] = jnp.where(ti == 0, 0.0, prev)
        u0buf[pl.ds(CONV_HALO, tm), :] = cv_ref[...] * jax.nn.sigmoid(cg_ref[...])

        gv = g_ref[...]
        dbias = jnp.zeros((1, C), F32)
        dgain = jnp.zeros((1, C), F32)
        dlnb = jnp.zeros((1, C), F32)
        for r0 in range(0, tm, rc):
            u1 = u1_ref[pl.ds(r0, rc), :]
            xc = u1 - jnp.mean(u1, axis=-1, keepdims=True)
            rstd = lax.rsqrt(jnp.mean(xc * xc, axis=-1, keepdims=True) + LN_EPS)
            xh = xc * rstd
            u2 = xh * gv + bb_ref[...]
            s = jax.nn.sigmoid(u2)
            du2 = du_ref[pl.ds(r0, rc), :] * (s * (1.0 + u2 * (1.0 - s)))
            dgain = dgain + _colsum(du2 * xh)
            dlnb = dlnb + _colsum(du2)
            dxh = du2 * gv
            du1 = rstd * (dxh - jnp.mean(dxh, axis=-1, keepdims=True)
                          - xh * jnp.mean(dxh * xh, axis=-1, keepdims=True))
            dbias = dbias + _colsum(du1)
            d1buf[pl.ds(r0, rc), :] = du1
        st_ref[pl.ds(K + 1, 1), :] += dbias
        st_ref[pl.ds(K + 2, 1), :] += dgain
        st_ref[pl.ds(K + 3, 1), :] += dlnb

        _shift_copies(u0buf, ush, shrows)
        _shift_copies(d1buf, dsh, shrows)
        for k in range(K):
            acc = jnp.zeros((SUBLANES, C), F32)
            for r0 in range(0, tm, rc):
                prod = d1buf[pl.ds(r0, rc), :] * _tap_rows(u0buf, ush, CONV_HALO - (K - 1) + k, r0, rc)
                acc = acc + jnp.sum(prod.reshape(rc // SUBLANES, SUBLANES, C), axis=0)
            st_ref[pl.ds(k, 1), :] += _colsum(acc)

        for r0 in range(0, tm, rc):
            du0 = _tap_sum_tiles(d1buf, dsh, w_ref, K, 0, r0, rc, True)
            cv = cv_ref[pl.ds(r0, rc), :]
            sg = jax.nn.sigmoid(cg_ref[pl.ds(r0, rc), :])
            dz_ref[pl.ds(r0, rc), pl.ds(0, C)] = (du0 * sg).astype(BF16)
            dz_ref[pl.ds(r0, rc), pl.ds(C, C)] = (du0 * cv * sg * (1.0 - sg)).astype(BF16)
        d1buf[pl.ds(tm, CONV_HALO), :] = d1buf[pl.ds(0, CONV_HALO), :]

    def rev(col):
        return lambda i: (nI - 1 - i, col)

    def rev_prev(col):
        return lambda i: (jnp.maximum((nI - 1 - i) * hb - 1, 0), col)

    vec = pl.BlockSpec((1, C), lambda i: (0, 0))
    return pl.pallas_call(
        body, grid=(nI,),
        in_specs=[pl.BlockSpec((tm, C), rev(0)), pl.BlockSpec((tm, C), rev(0)),
                  pl.BlockSpec((tm, C), rev(0)), pl.BlockSpec((tm, C), rev(1)),
                  pl.BlockSpec((CONV_HALO, C), rev_prev(0)), pl.BlockSpec((CONV_HALO, C), rev_prev(1)),
                  pl.BlockSpec((K, C), lambda i: (0, 0)), vec, vec],
        out_specs=[pl.BlockSpec((tm, 2 * C), rev(0)), pl.BlockSpec((srows, C), lambda i: (0, 0))],
        out_shape=[jax.ShapeDtypeStruct((T, 2 * C), BF16), jax.ShapeDtypeStruct((srows, C), F32)],
        scratch_shapes=[pltpu.VMEM((CONV_HALO + tm, C), F32), pltpu.VMEM((tm + CONV_HALO, C), F32),
                        pltpu.VMEM((SUBLANES - 1, shrows, C), F32), pltpu.VMEM((SUBLANES - 1, shrows, C), F32)],
        compiler_params=_params("arbitrary"), name="conv_bwd")(dcat, u1, z, z, z, z, w, lng, lnb)


def _softplus(v):
    return jnp.maximum(v, 0.0) + jnp.log(1.0 + jnp.exp(-jnp.abs(v)))


def _gelu(v):
    c = math.sqrt(2.0 / math.pi)
    t = jnp.tanh(c * (v + 0.044715 * v * v * v))
    gl = 0.5 * v * (1.0 + t)
    dgl = 0.5 * (1.0 + t) + 0.5 * v * (1.0 - t * t) * c * (1.0 + 3.0 * 0.044715 * v * v)
    return gl, dgl


def _lru_gates(xr, wa, ba, wx, bx, lam):
    xb = xr.astype(BF16)
    r = jax.nn.sigmoid(_dot(xb, wa) + ba)
    ig = jax.nn.sigmoid(_dot(xb, wx) + bx)
    sp = _softplus(-lam)
    log_a = -LRU_C * r * sp
    a = jnp.exp(log_a)
    y = 2.0 * log_a
    series = -(y * (1.0 + y * (0.5 + y * (1.0 / 6.0 + y * (1.0 / 24.0)))))
    mult = jnp.sqrt(jnp.where(y > -0.02, series, 1.0 - jnp.exp(y)))
    return a, mult, r, ig, sp


def _scan_tile(a_s, b_s, h_s, p_s, carry, seg, reverse):
    hl = [jnp.zeros((SUBLANES, LANES), F32)] * LRU_GROUPS
    pr = [jnp.ones((SUBLANES, LANES), F32)] * LRU_GROUPS
    for n in range(seg):
        for g in range(LRU_GROUPS):
            rows = pl.ds(g * SUBLANES * seg + ((seg - 1 - n) if reverse else n), SUBLANES, stride=seg)
            av = a_s[rows, :]
            hl[g] = av * hl[g] + b_s[rows, :]
            pr[g] = av * pr[g]
            h_s[rows, :] = hl[g]
            p_s[rows, :] = pr[g]
    nseg = SUBLANES * LRU_GROUPS
    cs = [None] * nseg
    c = carry
    for s in (range(nseg - 1, -1, -1) if reverse else range(nseg)):
        g, r = divmod(s, SUBLANES)
        cs[s] = c
        c = hl[g][r:r + 1, :] + pr[g][r:r + 1, :] * c
    return cs, c


def _lru_fwd(z, col0, w4, b4, wa, ba, wx, bx, lam):
    T = z.shape[0]
    K4, W = w4.shape
    nC = W // LANES
    tm = _tile(T, LRU_TILE, SUBLANES * SUBLANES * LRU_GROUPS)
    seg = tm // (SUBLANES * LRU_GROUPS)
    cx, cg = col0 // LANES, (col0 + W) // LANES

    def body(rx_ref, rg_ref, w4_ref, b4_ref, wa_ref, ba_ref, wx_ref, bx_ref, lam_ref,
             yr_ref, hs_ref, xbuf, a_s, b_s, h_s, p_s, hc):
        @pl.when(pl.program_id(1) == 0)
        def _():
            xbuf[pl.ds(0, LRU_HALO), :] = jnp.zeros((LRU_HALO, LANES), F32)
            hc[...] = jnp.zeros_like(hc)

        xbuf[pl.ds(LRU_HALO, tm), :] = rx_ref[...]
        xr = _tap_sum(xbuf, w4_ref, K4, LRU_HALO - (K4 - 1), 0, tm, False) + b4_ref[...]
        a, mult, _, ig, _ = _lru_gates(xr, wa_ref[...], ba_ref[...], wx_ref[...], bx_ref[...], lam_ref[...])
        a_s[...] = a
        b_s[...] = mult * ig * xr
        cs, cout = _scan_tile(a_s, b_s, h_s, p_s, hc[pl.ds(0, 1), :], seg, False)
        hc[pl.ds(0, 1), :] = cout
        for s in range(SUBLANES * LRU_GROUPS):
            rows = pl.ds(s * seg, seg)
            h = h_s[rows, :] + p_s[rows, :] * cs[s]
            hs_ref[rows, :] = h
            gl, _ = _gelu(rg_ref[rows, :])
            yr_ref[rows, :] = (h * gl).astype(BF16)
        xbuf[pl.ds(0, LRU_HALO), :] = xbuf[pl.ds(tm, LRU_HALO), :]

    vec = pl.BlockSpec((1, LANES), lambda c, i: (0, c))
    mat = pl.BlockSpec((None, LANES, LANES), lambda c, i: (c, 0, 0))
    return pl.pallas_call(
        body, grid=(nC, T // tm),
        in_specs=[pl.BlockSpec((tm, LANES), lambda c, i: (i, cx + c)),
                  pl.BlockSpec((tm, LANES), lambda c, i: (i, cg + c)),
                  pl.BlockSpec((K4, LANES), lambda c, i: (0, c)), vec, mat, vec, mat, vec, vec],
        out_specs=[pl.BlockSpec((tm, LANES), lambda c, i: (i, c)), pl.BlockSpec((tm, LANES), lambda c, i: (i, c))],
        out_shape=[jax.ShapeDtypeStruct((T, W), BF16), jax.ShapeDtypeStruct((T, W), F32)],
        scratch_shapes=[pltpu.VMEM((LRU_HALO + tm, LANES), F32)] + [pltpu.VMEM((tm, LANES), F32)] * 4
        + [pltpu.VMEM((SUBLANES, LANES), F32)],
        compiler_params=_params("parallel", "arbitrary"), name="lru_fwd")(z, z, w4, b4, wa, ba, wx, bx, lam)


def _lru_bwd(dcat, dcol0, hs, z, col0, w4, b4, wa, ba, wx, bx, lam):
    T = z.shape[0]
    K4, W = w4.shape
    assert K4 + 4 == SUBLANES
    nC = W // LANES
    tm = _tile(T, LRU_TILE, SUBLANES * SUBLANES * LRU_GROUPS)
    seg = tm // (SUBLANES * LRU_GROUPS)
    nI = T // tm
    hb = tm // LRU_HALO
    cx, cg, cd = col0 // LANES, (col0 + W) // LANES, dcol0 // LANES

    def body(dyr_ref, hs_ref, hsp_ref, rx_ref, rxp_ref, rg_ref, w4_ref, b4_ref, wa_ref, ba_ref, wx_ref, bx_ref,
             lam_ref, dzx_ref, dzg_ref, st_ref, dwa_ref, dwx_ref, xbuf, hbuf, abuf, a_s, b_s, h_s, p_s, dbuf, gc, anc):
        i = pl.program_id(1)
        ti = nI - 1 - i

        @pl.when(i == 0)
        def _():
            st_ref[...] = jnp.zeros_like(st_ref)
            dwa_ref[...] = jnp.zeros_like(dwa_ref)
            dwx_ref[...] = jnp.zeros_like(dwx_ref)
            gc[...] = jnp.zeros_like(gc)
            anc[...] = jnp.zeros_like(anc)
            dbuf[pl.ds(tm, LRU_HALO), :] = jnp.zeros((LRU_HALO, LANES), F32)

        xbuf[pl.ds(0, LRU_HALO), :] = jnp.where(ti == 0, 0.0, rxp_ref[...])
        xbuf[pl.ds(LRU_HALO, tm), :] = rx_ref[...]
        hbuf[pl.ds(0, LRU_HALO), :] = jnp.where(ti == 0, 0.0, hsp_ref[...])
        hbuf[pl.ds(LRU_HALO, tm), :] = hs_ref[...]

        wa, wx = wa_ref[...], wx_ref[...]
        lam_v = lam_ref[...]
        xr = _tap_sum(xbuf, w4_ref, K4, LRU_HALO - (K4 - 1), 0, tm, False) + b4_ref[...]
        a, mult, r, ig, sp = _lru_gates(xr, wa, ba_ref[...], wx, bx_ref[...], lam_v)

        dyr = dyr_ref[...]
        gl, dgl = _gelu(rg_ref[...])
        dzg_ref[...] = (dyr * hs_ref[...] * dgl).astype(BF16)

        abuf[pl.ds(0, tm), :] = a
        abuf[pl.ds(tm, LRU_HALO), :] = anc[...]
        a_s[...] = abuf[pl.ds(1, tm), :]
        b_s[...] = dyr * gl
        cs, cout = _scan_tile(a_s, b_s, h_s, p_s, gc[pl.ds(0, 1), :], seg, True)
        gc[pl.ds(0, 1), :] = cout
        anc[pl.ds(0, 1), :] = a[0:1, :]
        for s in range(SUBLANES * LRU_GROUPS):
            rows = pl.ds(s * seg, seg)
            b_s[rows, :] = h_s[rows, :] + p_s[rows, :] * cs[s]
        g = b_s[...]

        d_a = g * hbuf[pl.ds(LRU_HALO - 1, tm), :]
        gx_ = g * xr
        d_log_a = d_a * a - (gx_ * ig) * (a * a / mult)
        dga = (d_log_a * (-LRU_C * sp)) * r * (1.0 - r)
        dgx = (gx_ * mult) * ig * (1.0 - ig)
        dga_b, dgx_b = dga.astype(BF16), dgx.astype(BF16)
        dxr = g * mult * ig + _dot_nt(dga_b, wa) + _dot_nt(dgx_b, wx)
        xb = xr.astype(BF16)
        dwa_ref[...] += _dot_tn(xb, dga_b)
        dwx_ref[...] += _dot_tn(xb, dgx_b)
        st_ref[pl.ds(K4, 1), :] += _colsum(dxr)
        st_ref[pl.ds(K4 + 1, 1), :] += _colsum(dga)
        st_ref[pl.ds(K4 + 2, 1), :] += _colsum(dgx)
        st_ref[pl.ds(K4 + 3, 1), :] += _colsum(d_log_a * (-LRU_C * r)) * (-jax.nn.sigmoid(-lam_v))

        dbuf[pl.ds(0, tm), :] = dxr
        for k in range(K4):
            st_ref[pl.ds(k, 1), :] += _colsum(dxr * xbuf[pl.ds(LRU_HALO - (K4 - 1) + k, tm), :])
        dzx_ref[...] = _tap_sum(dbuf, w4_ref, K4, 0, 0, tm, True).astype(BF16)
        dbuf[pl.ds(tm, LRU_HALO), :] = dbuf[pl.ds(0, LRU_HALO), :]

    def rev(col):
        return lambda c, i: (nI - 1 - i, col + c)

    def rev_prev(col):
        return lambda c, i: (jnp.maximum((nI - 1 - i) * hb - 1, 0), col + c)

    vec = pl.BlockSpec((1, LANES), lambda c, i: (0, c))
    mat = pl.BlockSpec((None, LANES, LANES), lambda c, i: (c, 0, 0))
    big = pltpu.VMEM((tm, LANES), F32)
    halo = pltpu.VMEM((tm + LRU_HALO, LANES), F32)
    return pl.pallas_call(
        body, grid=(nC, nI),
        in_specs=[pl.BlockSpec((tm, LANES), rev(cd)),
                  pl.BlockSpec((tm, LANES), rev(0)), pl.BlockSpec((LRU_HALO, LANES), rev_prev(0)),
                  pl.BlockSpec((tm, LANES), rev(cx)), pl.BlockSpec((LRU_HALO, LANES), rev_prev(cx)),
                  pl.BlockSpec((tm, LANES), rev(cg)),
                  pl.BlockSpec((K4, LANES), lambda c, i: (0, c)), vec, mat, vec, mat, vec, vec],
        out_specs=[pl.BlockSpec((tm, LANES), rev(0)), pl.BlockSpec((tm, LANES), rev(0)),
                   pl.BlockSpec((SUBLANES, LANES), lambda c, i: (0, c)), mat, mat],
        out_shape=[jax.ShapeDtypeStruct((T, W), BF16), jax.ShapeDtypeStruct((T, W), BF16),
                   jax.ShapeDtypeStruct((SUBLANES, W), F32),
                   jax.ShapeDtypeStruct((nC, LANES, LANES), F32), jax.ShapeDtypeStruct((nC, LANES, LANES), F32)],
        scratch_shapes=[halo, halo, halo, big, big, big, big, halo,
                        pltpu.VMEM((SUBLANES, LANES), F32), pltpu.VMEM((SUBLANES, LANES), F32)],
        compiler_params=_params("parallel", "arbitrary"), name="lru_bwd")(
            dcat, hs, hs, z, z, z, w4, b4, wa, ba, wx, bx, lam)


def _mix_out_fwd(x, u, yr, wout):
    T, D = x.shape
    C, W = u.shape[1], yr.shape[1]
    tm = _tile(T, TOK_TILE)

    def body(x_ref, u_ref, yr_ref, w_ref, y_ref):
        y_ref[...] = (x_ref[...] + _dot(u_ref[...], w_ref[pl.ds(0, C), :])
                      + _dot(yr_ref[...], w_ref[pl.ds(C, W), :]))

    return pl.pallas_call(
        body, grid=(T // tm,),
        in_specs=[pl.BlockSpec((tm, D), lambda i: (i, 0)), pl.BlockSpec((tm, C), lambda i: (i, 0)),
                  pl.BlockSpec((tm, W), lambda i: (i, 0)),
                  pl.BlockSpec((C + W, D), lambda i: (0, 0), pipeline_mode=pl.Buffered(1))],
        out_specs=pl.BlockSpec((tm, D), lambda i: (i, 0)),
        out_shape=jax.ShapeDtypeStruct((T, D), F32),
        compiler_params=_params("parallel"), name="mix_out_fwd")(x, u, yr, wout)


def _mix_out_bwd(dy, u, yr, wout):
    T, D = dy.shape
    C, W = u.shape[1], yr.shape[1]
    tm = _tile(T, BWD_TILE)

    def body(dy_ref, u_ref, yr_ref, w_ref, dcat_ref, dw_ref):
        @pl.when(pl.program_id(0) == 0)
        def _():
            dw_ref[...] = jnp.zeros_like(dw_ref)

        dyb = dy_ref[...].astype(BF16)
        dcat_ref[...] = _dot_nt(dyb, w_ref[...])
        dw_ref[pl.ds(0, C), :] += _dot_tn(u_ref[...], dyb)
        dw_ref[pl.ds(C, W), :] += _dot_tn(yr_ref[...], dyb)

    return pl.pallas_call(
        body, grid=(T // tm,),
        in_specs=[pl.BlockSpec((tm, D), lambda i: (i, 0)), pl.BlockSpec((tm, C), lambda i: (i, 0)),
                  pl.BlockSpec((tm, W), lambda i: (i, 0)),
                  pl.BlockSpec((C + W, D), lambda i: (0, 0), pipeline_mode=pl.Buffered(1))],
        out_specs=[pl.BlockSpec((tm, C + W), lambda i: (i, 0)), pl.BlockSpec((C + W, D), lambda i: (0, 0))],
        out_shape=[jax.ShapeDtypeStruct((T, C + W), F32), jax.ShapeDtypeStruct((C + W, D), F32)],
        compiler_params=_params("arbitrary"), name="mix_out_bwd")(dy, u, yr, wout)


def _mix_in_bwd(dzc, dzx, dzg, x, dy, g, win):
    T, D = x.shape
    ns, ws = win.shape[0], win.shape[2]
    tm = _tile(T, BWD_TILE)
    parts = []
    for j in range(ns):
        lo = j * ws
        if lo < dzc.shape[1]:
            parts.append((0, lo))
        elif lo < dzc.shape[1] + dzx.shape[1]:
            parts.append((1, lo - dzc.shape[1]))
        else:
            parts.append((2, lo - dzc.shape[1] - dzx.shape[1]))

    def body(dzc_ref, dzx_ref, dzg_ref, x_ref, dy_ref, g_ref, w_ref, dx_ref, dw_ref, dg_ref):
        @pl.when(pl.program_id(0) == 0)
        def _():
            dw_ref[...] = jnp.zeros_like(dw_ref)
            dg_ref[...] = jnp.zeros_like(dg_ref)

        xh, r = _rms_stats(x_ref[...])
        gv = g_ref[...]
        hb = (xh * gv).astype(BF16)
        srcs = (dzc_ref, dzx_ref, dzg_ref)
        dh = jnp.zeros((tm, D), F32)
        for j, (si, off) in enumerate(parts):
            dzj = srcs[si][:, pl.ds(off, ws)]
            dh = dh + _dot_nt(dzj, w_ref[j])
            dw_ref[j] += _dot_tn(hb, dzj)
        dx_ref[...] = dy_ref[...] + _rms_bwd(dh, xh, r, gv)
        dg_ref[...] += _colsum(dh * xh)

    def tok(n):
        return pl.BlockSpec((tm, n), lambda i: (i, 0))

    vec = pl.BlockSpec((1, D), lambda i: (0, 0))
    return pl.pallas_call(
        body, grid=(T // tm,),
        in_specs=[tok(dzc.shape[1]), tok(dzx.shape[1]), tok(dzg.shape[1]), tok(D), tok(D), vec,
                  pl.BlockSpec((ns, D, ws), lambda i: (0, 0, 0), pipeline_mode=pl.Buffered(1))],
        out_specs=[tok(D), pl.BlockSpec((ns, D, ws), lambda i: (0, 0, 0)), vec],
        out_shape=[jax.ShapeDtypeStruct((T, D), F32), jax.ShapeDtypeStruct((ns, D, ws), F32),
                   jax.ShapeDtypeStruct((1, D), F32)],
        compiler_params=_params("arbitrary"), name="mix_in_bwd")(dzc, dzx, dzg, x, dy, g, win)


def _final_loss(x, g, tgt):
    T, D = x.shape
    tm = _tile(T, TOK_TILE)

    def body(x_ref, g_ref, t_ref, dx_ref, loss_ref, dg_ref):
        @pl.when(pl.program_id(0) == 0)
        def _():
            loss_ref[...] = jnp.zeros_like(loss_ref)
            dg_ref[...] = jnp.zeros_like(dg_ref)

        xh, r = _rms_stats(x_ref[...])
        gv = g_ref[...]
        e = xh * gv - t_ref[...]
        loss_ref[...] += 0.5 * jnp.sum(jnp.mean(e * e, axis=-1, keepdims=True))
        dy = e * (1.0 / D)
        dg_ref[...] += _colsum(dy * xh)
        dx_ref[...] = _rms_bwd(dy, xh, r, gv)

    tok = pl.BlockSpec((tm, D), lambda i: (i, 0))
    vec = pl.BlockSpec((1, D), lambda i: (0, 0))
    return pl.pallas_call(
        body, grid=(T // tm,),
        in_specs=[tok, vec, tok],
        out_specs=[tok, pl.BlockSpec((SUBLANES, LANES), lambda i: (0, 0)), vec],
        out_shape=[jax.ShapeDtypeStruct((T, D), F32), jax.ShapeDtypeStruct((SUBLANES, LANES), F32),
                   jax.ShapeDtypeStruct((1, D), F32)],
        compiler_params=_params("arbitrary"), name="final_loss")(x, g, tgt)


def _adamw(w, g, m, v, name):
    R, Cc = w.shape
    tr = _tile(R, max(SUBLANES, (1 << 19) // Cc))
    c1 = 1.0 - ADAM_B1 ** ADAM_STEP
    c2 = 1.0 - ADAM_B2 ** ADAM_STEP

    def body(w_ref, g_ref, m_ref, v_ref, d_ref, nm_ref, nv_ref):
        gv = g_ref[...]
        nm = ADAM_B1 * m_ref[...] + (1.0 - ADAM_B1) * gv
        nv = ADAM_B2 * v_ref[...] + (1.0 - ADAM_B2) * (gv * gv)
        nm_ref[...] = nm
        nv_ref[...] = nv
        d_ref[...] = -ADAM_LR * ((nm / c1) / (jnp.sqrt(nv / c2) + ADAM_EPS) + ADAM_WD * w_ref[...])

    blk = pl.BlockSpec((tr, Cc), lambda i: (i, 0))
    sds = jax.ShapeDtypeStruct((R, Cc), F32)
    return pl.pallas_call(
        body, grid=(R // tr,), in_specs=[blk] * 4, out_specs=[blk] * 3, out_shape=[sds] * 3,
        compiler_params=_params("parallel"), name=name)(w, g, m, v)


def _here():
    return lax.axis_index("x"), lax.axis_index("y"), lax.axis_index("c")


def _chip_at(x, y, m):
    return x ^ (m >> 1), y ^ (m & 1)


ANY = pl.BlockSpec(memory_space=pl.ANY)


def _place_cast(srcs, idx, dtype, name):
    n = len(srcs)
    R, Cc = srcs[0].shape
    tr = _tile(R, max(16, (1 << 18) // Cc), 16)

    def body(i_ref, *refs):
        o_ref = refs[n]
        for k in range(n):
            o_ref[k] = refs[k][...].astype(dtype)

    blk = pl.BlockSpec((tr, Cc), lambda i, s: (i, 0))
    return pl.pallas_call(
        body,
        grid_spec=pltpu.PrefetchScalarGridSpec(
            num_scalar_prefetch=1, grid=(R // tr,), in_specs=[blk] * n,
            out_specs=pl.BlockSpec((n, None, tr, Cc), lambda i, s: (0, s[1], i, 0))),
        out_shape=jax.ShapeDtypeStruct((n, N_CHIPS, R, Cc), dtype),
        compiler_params=_params("parallel"), name=name)(idx, *srcs)


def _gather_weights(lands):
    n = len(lands)

    def body(*refs):
        outs = refs[n:2 * n]
        send1, recv1, send2, recv2 = refs[2 * n:]
        x, y, c = _here()
        own = 2 * x + y

        def half(ref, chip, cc):
            rh = ref.shape[-2] // 2
            lead = (slice(None),) * (len(ref.shape) - 3)
            return ref.at[lead + (chip, pl.ds(cc * rh, rh), slice(None))]

        first = []
        for k in range(n):
            for m in (1, 2, 3):
                px, py = _chip_at(x, y, m)
                cp = pltpu.make_async_remote_copy(
                    src_ref=half(outs[k], own, c), dst_ref=half(outs[k], own, c),
                    send_sem=send1.at[k, m - 1], recv_sem=recv1.at[k, m - 1],
                    device_id=(px, py, c), device_id_type=MESH)
                cp.start()
                first.append(cp)

        passed = []
        for k in range(n):
            for m in (1, 2, 3):
                px, py = _chip_at(x, y, m)
                peer = 2 * px + py
                got = half(outs[k], peer, c)
                pltpu.make_async_remote_copy(
                    src_ref=got, dst_ref=got, send_sem=send1.at[k, m - 1], recv_sem=recv1.at[k, m - 1],
                    device_id=(px, py, c), device_id_type=MESH).wait_recv()
                cp = pltpu.make_async_remote_copy(
                    src_ref=got, dst_ref=got, send_sem=send2.at[k, m - 1], recv_sem=recv2.at[k, m - 1],
                    device_id=(x, y, 1 - c), device_id_type=MESH)
                cp.start()
                passed.append(cp)

        for k in range(n):
            for m in (1, 2, 3):
                px, py = _chip_at(x, y, m)
                other = half(outs[k], 2 * px + py, 1 - c)
                pltpu.make_async_remote_copy(
                    src_ref=other, dst_ref=other, send_sem=send2.at[k, m - 1], recv_sem=recv2.at[k, m - 1],
                    device_id=(x, y, 1 - c), device_id_type=MESH).wait_recv()
        for cp in first + passed:
            cp.wait_send()

    return pl.pallas_call(
        body, in_specs=[ANY] * n, out_specs=[ANY] * n,
        out_shape=[jax.ShapeDtypeStruct(a.shape, a.dtype) for a in lands],
        input_output_aliases={k: k for k in range(n)},
        scratch_shapes=[pltpu.SemaphoreType.DMA((n, 3)), pltpu.SemaphoreType.DMA((n, 3)),
                        pltpu.SemaphoreType.DMA((n, 3)), pltpu.SemaphoreType.DMA((n, 3))],
        name="gather_weights")(*lands)


HBM = pl.BlockSpec(memory_space=pltpu.HBM)
SEM = pl.BlockSpec(memory_space=pltpu.SEMAPHORE)
EFFECT = pltpu.SideEffectType.DATAFLOW_SIDE_EFFECTING


def _in_hbm(a):
    return pltpu.with_memory_space_constraint(a, pltpu.HBM)


def _gather_copies(land_refs, send, recv):
    x, y, c = _here()
    own = 2 * x + y
    cps = []
    for k in range(len(land_refs)):
        lead = (slice(None),) * (len(land_refs[k].shape) - 3)
        mine = land_refs[k].at[lead + (own,)]
        for m in (1, 2, 3):
            px, py = _chip_at(x, y, m)
            cps.append(pltpu.make_async_remote_copy(
                src_ref=mine, dst_ref=mine, send_sem=send.at[3 * k + m - 1], recv_sem=recv.at[3 * k + m - 1],
                device_id=(px, py, c), device_id_type=MESH))
    return cps


def _gather_start(lands, after, name):
    n = len(lands)

    def body(*refs):
        lz = refs[:n]
        send, recv = refs[n + 1], refs[n + 2]
        token = refs[-1]
        for cp in _gather_copies(lz, send, recv):
            cp.start()
        token[...] = jnp.zeros_like(token)

    hbm = [pltpu.HBM(a.shape, a.dtype) for a in lands]
    outs = pl.pallas_call(
        body, name=name,
        in_specs=[HBM] * n + [ANY],
        out_specs=[SEM, SEM] + [HBM] * n + [pl.BlockSpec(memory_space=pltpu.VMEM)],
        out_shape=[pltpu.SemaphoreType.DMA((3 * n,)), pltpu.SemaphoreType.DMA((3 * n,))] + hbm
        + [jax.ShapeDtypeStruct((SUBLANES, LANES), F32)],
        input_output_aliases={k: 2 + k for k in range(n)},
        compiler_params=pltpu.CompilerParams(has_side_effects=EFFECT),
    )(*[_in_hbm(a) for a in lands], after)
    return outs[0], outs[1], outs[2:2 + n], outs[-1]


def _gather_wait(send, recv, lands, after, name):
    n = len(lands)

    def body(*refs):
        lz = refs[:n]
        send_r, recv_r = refs[n], refs[n + 1]
        for cp in _gather_copies(lz, send_r, recv_r):
            cp.wait_send()
            cp.wait_recv()

    hbm = [pltpu.HBM(a.shape, a.dtype) for a in lands]
    return pl.pallas_call(
        body, name=name,
        in_specs=[HBM] * n + [SEM, SEM, ANY],
        out_specs=[HBM] * n, out_shape=hbm,
        input_output_aliases={k: k for k in range(n)},
        compiler_params=pltpu.CompilerParams(has_side_effects=EFFECT),
    )(*lands, send, recv, after)


def _exchange_copies(part_refs, slot_refs, send, recv):
    x, y, c = _here()
    cps = []
    for k in range(len(part_refs)):
        for m in (1, 2, 3):
            px, py = _chip_at(x, y, m)
            cps.append(pltpu.make_async_remote_copy(
                src_ref=part_refs[k].at[2 * px + py], dst_ref=slot_refs[k].at[m - 1],
                send_sem=send.at[3 * k + m - 1], recv_sem=recv.at[3 * k + m - 1],
                device_id=(px, py, c), device_id_type=MESH))
    return cps


def _exchange_start(parts, name):
    n = len(parts)
    lands = [lax.empty((N_CHIPS - 1,) + p.shape[1:], p.dtype) for p in parts]

    def body(*refs):
        ins, lz = refs[:n], refs[n:2 * n]
        send, recv = refs[2 * n], refs[2 * n + 1]
        token = refs[-1]
        for cp in _exchange_copies(ins, lz, send, recv):
            cp.start()
        token[...] = jnp.zeros_like(token)

    hbm = [pltpu.HBM(a.shape, a.dtype) for a in list(parts) + lands]
    outs = pl.pallas_call(
        body, name=name,
        in_specs=[HBM] * (2 * n),
        out_specs=[SEM, SEM] + [HBM] * (2 * n) + [pl.BlockSpec(memory_space=pltpu.VMEM)],
        out_shape=[pltpu.SemaphoreType.DMA((3 * n,)), pltpu.SemaphoreType.DMA((3 * n,))] + hbm
        + [jax.ShapeDtypeStruct((SUBLANES, LANES), F32)],
        input_output_aliases={k: 2 + k for k in range(2 * n)},
        compiler_params=pltpu.CompilerParams(has_side_effects=EFFECT),
    )(*[_in_hbm(a) for a in parts], *[_in_hbm(a) for a in lands])
    return outs[0], outs[1], outs[2:2 + n], outs[2 + n:2 + 2 * n], outs[-1]


def _exchange_wait(send, recv, parts, lands, after, name):
    n = len(parts)

    def body(*refs):
        ins, lz = refs[:n], refs[n:2 * n]
        send_r, recv_r = refs[2 * n], refs[2 * n + 1]
        for cp in _exchange_copies(ins, lz, send_r, recv_r):
            cp.wait_send()
            cp.wait_recv()

    hbm = [pltpu.HBM(a.shape, a.dtype) for a in list(parts) + list(lands)]
    outs = pl.pallas_call(
        body, name=name,
        in_specs=[HBM] * (2 * n) + [SEM, SEM, ANY],
        out_specs=[HBM] * (2 * n), out_shape=hbm,
        input_output_aliases={k: k for k in range(2 * n)},
        compiler_params=pltpu.CompilerParams(has_side_effects=EFFECT),
    )(*parts, *lands, send, recv, after)
    return outs[:n], outs[n:]


def _swap_halves_out(grads, name):
    n = len(grads)
    out_shapes = [jax.ShapeDtypeStruct((g.shape[0], g.shape[1] // 2, g.shape[2]), g.dtype) for g in grads]

    def body(*refs):
        ins, outs = refs[:n], refs[n:2 * n]
        send, recv = refs[2 * n:]
        x, y, c = _here()
        cps = []
        for k in range(n):
            rh = ins[k].shape[1] // 2
            cp = pltpu.make_async_remote_copy(
                src_ref=ins[k].at[:, pl.ds((1 - c) * rh, rh), :], dst_ref=outs[k],
                send_sem=send.at[k], recv_sem=recv.at[k], device_id=(x, y, 1 - c), device_id_type=MESH)
            cp.start()
            cps.append(cp)
        for cp in cps:
            cp.wait()

    return pl.pallas_call(
        body, in_specs=[ANY] * n, out_specs=[ANY] * n, out_shape=out_shapes,
        scratch_shapes=[pltpu.SemaphoreType.DMA((n,)), pltpu.SemaphoreType.DMA((n,))],
        name=name)(*grads)


def _swap_copies(grad_refs, land_refs, send, recv):
    x, y, c = _here()
    cps = []
    for k in range(len(grad_refs)):
        rh = grad_refs[k].shape[1] // 2
        cps.append(pltpu.make_async_remote_copy(
            src_ref=grad_refs[k].at[:, pl.ds((1 - c) * rh, rh), :], dst_ref=land_refs[k],
            send_sem=send.at[k], recv_sem=recv.at[k], device_id=(x, y, 1 - c), device_id_type=MESH))
    return cps


def _swap_start(grads, name):
    n = len(grads)
    lands = [lax.empty((g.shape[0], g.shape[1] // 2, g.shape[2]), g.dtype) for g in grads]

    def body(*refs):
        ins, lz = refs[:n], refs[n:2 * n]
        send, recv = refs[2 * n], refs[2 * n + 1]
        token = refs[-1]
        for cp in _swap_copies(ins, lz, send, recv):
            cp.start()
        token[...] = jnp.zeros_like(token)

    hbm = [pltpu.HBM(a.shape, a.dtype) for a in list(grads) + lands]
    outs = pl.pallas_call(
        body, name=name,
        in_specs=[HBM] * (2 * n),
        out_specs=[SEM, SEM] + [HBM] * (2 * n) + [pl.BlockSpec(memory_space=pltpu.VMEM)],
        out_shape=[pltpu.SemaphoreType.DMA((n,)), pltpu.SemaphoreType.DMA((n,))] + hbm
        + [jax.ShapeDtypeStruct((SUBLANES, LANES), F32)],
        input_output_aliases={k: 2 + k for k in range(2 * n)},
        compiler_params=pltpu.CompilerParams(has_side_effects=EFFECT),
    )(*[_in_hbm(a) for a in grads], *[_in_hbm(a) for a in lands])
    return outs[0], outs[1], outs[2:2 + n], outs[2 + n:2 + 2 * n], outs[-1]


def _swap_wait(send, recv, grads, lands, after, name):
    n = len(grads)

    def body(*refs):
        ins, lz = refs[:n], refs[n:2 * n]
        send_r, recv_r = refs[2 * n], refs[2 * n + 1]
        for cp in _swap_copies(ins, lz, send_r, recv_r):
            cp.wait_send()
            cp.wait_recv()

    hbm = [pltpu.HBM(a.shape, a.dtype) for a in list(grads) + list(lands)]
    outs = pl.pallas_call(
        body, name=name,
        in_specs=[HBM] * (2 * n) + [SEM, SEM, ANY],
        out_specs=[HBM] * (2 * n), out_shape=hbm,
        input_output_aliases={k: k for k in range(2 * n)},
        compiler_params=pltpu.CompilerParams(has_side_effects=EFFECT),
    )(*grads, *lands, send, recv, after)
    return outs[:n], outs[n:]


def _add_cast(g, other, cidx, name):
    ns, R, Cc = g.shape
    rh = R // 2
    tr = _tile(rh, max(16, (1 << 19) // Cc), 16)
    nb = rh // tr

    def body(c_ref, g_ref, o_ref, s_ref):
        s_ref[...] = (g_ref[...] + o_ref[...]).astype(BF16)

    return pl.pallas_call(
        body,
        grid_spec=pltpu.PrefetchScalarGridSpec(
            num_scalar_prefetch=1, grid=(ns, nb),
            in_specs=[pl.BlockSpec((None, tr, Cc), lambda k, i, c: (k, c[0] * nb + i, 0)),
                      pl.BlockSpec((None, tr, Cc), lambda k, i, c: (k, i, 0))],
            out_specs=pl.BlockSpec((None, tr, Cc), lambda k, i, c: (k, i, 0))),
        out_shape=jax.ShapeDtypeStruct((ns, rh, Cc), BF16),
        compiler_params=_params("parallel", "parallel"), name=name)(cidx, g, other)


def _sum_slots(part, got, idx, name):
    ns, rh, Cc = got.shape
    tr = _tile(rh, max(16, (1 << 18) // Cc), 16)
    nb = rh // tr

    def body(i_ref, p_ref, b_ref, o_ref):
        acc = p_ref[...].astype(F32)
        for m in range(ns):
            acc = acc + b_ref[m].astype(F32)
        o_ref[...] = acc

    return pl.pallas_call(
        body,
        grid_spec=pltpu.PrefetchScalarGridSpec(
            num_scalar_prefetch=1, grid=(nb,),
            in_specs=[pl.BlockSpec((None, tr, Cc), lambda i, s: (s[1], i, 0)),
                      pl.BlockSpec((ns, tr, Cc), lambda i, s: (0, i, 0))],
            out_specs=pl.BlockSpec((tr, Cc), lambda i, s: (s[0] * nb + i, 0))),
        out_shape=jax.ShapeDtypeStruct((2 * rh, Cc), F32),
        compiler_params=_params("parallel"), name=name)(idx, part, got)


def _share_halves(blocks, name):
    n = len(blocks)

    def body(*refs):
        ins, outs = refs[:n], refs[n:2 * n]
        send, recv = refs[2 * n:]
        x, y, c = _here()
        cps = []
        for k in range(n):
            rh = outs[k].shape[0] // 2
            mine = outs[k].at[pl.ds(c * rh, rh), :]
            cp = pltpu.make_async_remote_copy(
                src_ref=mine, dst_ref=mine, send_sem=send.at[k], recv_sem=recv.at[k],
                device_id=(x, y, 1 - c), device_id_type=MESH)
            cp.start()
            cps.append(cp)
        for cp in cps:
            cp.wait()

    return pl.pallas_call(
        body, in_specs=[ANY] * n, out_specs=[ANY] * n,
        out_shape=[jax.ShapeDtypeStruct(b.shape, b.dtype) for b in blocks],
        input_output_aliases={k: k for k in range(n)},
        scratch_shapes=[pltpu.SemaphoreType.DMA((n,)), pltpu.SemaphoreType.DMA((n,))],
        name=name)(*blocks)


def _small_copies(p_ref, slot_ref, send, recv):
    x, y, c = _here()
    mine = slot_ref.at[4 * x + 2 * y + c]
    cps = []
    for m in range(1, N_DEV):
        peer = (x ^ (m >> 2), y ^ ((m >> 1) & 1), c ^ (m & 1))
        cps.append(pltpu.make_async_remote_copy(
            src_ref=p_ref, dst_ref=mine, send_sem=send.at[m - 1], recv_sem=recv.at[m - 1],
            device_id=peer, device_id_type=MESH))
    return cps


def _small_start(packed):
    slots = lax.empty((N_DEV,) + packed.shape, packed.dtype)

    def body(p_ref, s_ref, send, recv, p_thru, s_thru, token):
        for cp in _small_copies(p_ref, s_ref, send, recv):
            cp.start()
        token[...] = jnp.zeros_like(token)

    return pl.pallas_call(
        body, name="small_start",
        in_specs=[HBM, HBM],
        out_specs=[SEM, SEM, HBM, HBM, pl.BlockSpec(memory_space=pltpu.VMEM)],
        out_shape=[pltpu.SemaphoreType.DMA((N_DEV - 1,)), pltpu.SemaphoreType.DMA((N_DEV - 1,)),
                   pltpu.HBM(packed.shape, packed.dtype), pltpu.HBM(slots.shape, slots.dtype),
                   jax.ShapeDtypeStruct((SUBLANES, LANES), F32)],
        input_output_aliases={0: 2, 1: 3},
        compiler_params=pltpu.CompilerParams(has_side_effects=EFFECT),
    )(_in_hbm(packed), _in_hbm(slots))


def _small_wait(send, recv, packed, slots, after):
    def body(p_ref, s_ref, send_r, recv_r, after_ref, p_out, s_out):
        for cp in _small_copies(p_ref, s_ref, send_r, recv_r):
            cp.wait_send()
            cp.wait_recv()

    return pl.pallas_call(
        body, name="small_wait",
        in_specs=[HBM, HBM, SEM, SEM, ANY], out_specs=[HBM, HBM],
        out_shape=[pltpu.HBM(packed.shape, packed.dtype), pltpu.HBM(slots.shape, slots.dtype)],
        input_output_aliases={0: 0, 1: 1},
        compiler_params=pltpu.CompilerParams(has_side_effects=EFFECT),
    )(packed, slots, send, recv, after)


def _sum_devices(packed, slots, me):
    n, R, _ = slots.shape
    tr = _tile(R, 1024)

    def body(m_ref, p_ref, s_ref, o_ref):
        own = p_ref[...]
        acc = None
        for d in range(n):
            term = jnp.where(m_ref[0] == d, own, s_ref[d])
            acc = term if acc is None else acc + term
        o_ref[...] = acc

    return pl.pallas_call(
        body,
        grid_spec=pltpu.PrefetchScalarGridSpec(
            num_scalar_prefetch=1, grid=(R // tr,),
            in_specs=[pl.BlockSpec((tr, LANES), lambda i, m: (i, 0)),
                      pl.BlockSpec((n, tr, LANES), lambda i, m: (0, i, 0))],
            out_specs=pl.BlockSpec((tr, LANES), lambda i, m: (i, 0))),
        out_shape=jax.ShapeDtypeStruct((R, LANES), F32),
        compiler_params=_params("parallel"), name="sum_devices")(me, packed, slots)


def _pack(arrs):
    rows, parts = [], []
    for a in arrs:
        flat = a.reshape(-1)
        r = -(-flat.shape[0] // (SUBLANES * LANES)) * SUBLANES
        parts.append(jnp.pad(flat, (0, r * LANES - flat.shape[0])).reshape(r, LANES))
        rows.append(r)
    return jnp.concatenate(parts, axis=0), rows


def _unpack(packed, rows, shapes):
    out, r0 = [], 0
    for r, shp in zip(rows, shapes):
        size = math.prod(shp)
        out.append(packed[r0:r0 + r].reshape(-1)[:size].reshape(shp))
        r0 += r
    return out


def _block_diag(w, per):
    H, dh, _ = w.shape
    w4 = w.reshape(H // per, per, dh, dh)
    eye = jnp.eye(per, dtype=w.dtype)
    return (w4[:, :, :, None, :] * eye[None, :, None, :, None]).reshape(H // per, per * dh, per * dh)


def _block_diag_take(d, per):
    n, s, _ = d.shape
    dh = s // per
    d5 = d.reshape(n, per, dh, per, dh)
    return jnp.stack([d5[:, h, :, h, :] for h in range(per)], axis=1).reshape(n * per, dh, dh)


def kernel(x, ffn1_norm, ffn1_w_gate, ffn1_w_up, ffn1_w_down, mix_norm, w_in, conv_dw, conv_dw_bias, conv_ln_g, conv_ln_b, lru_conv_w, lru_conv_b, lru_w_a, lru_b_a, lru_w_x, lru_b_x, lru_lambda, w_out, ffn2_norm, ffn2_w_gate, ffn2_w_up, ffn2_w_down, final_norm, loss_target, m_ffn1_norm, m_ffn1_w_gate, m_ffn1_w_up, m_ffn1_w_down, m_mix_norm, m_w_in, m_conv_dw, m_conv_dw_bias, m_conv_ln_g, m_conv_ln_b, m_lru_conv_w, m_lru_conv_b, m_lru_w_a, m_lru_b_a, m_lru_w_x, m_lru_b_x, m_lru_lambda, m_w_out, m_ffn2_norm, m_ffn2_w_gate, m_ffn2_w_up, m_ffn2_w_down, m_final_norm, v_ffn1_norm, v_ffn1_w_gate, v_ffn1_w_up, v_ffn1_w_down, v_mix_norm, v_w_in, v_conv_dw, v_conv_dw_bias, v_conv_ln_g, v_conv_ln_b, v_lru_conv_w, v_lru_conv_b, v_lru_w_a, v_lru_b_a, v_lru_w_x, v_lru_b_x, v_lru_lambda, v_w_out, v_ffn2_norm, v_ffn2_w_gate, v_ffn2_w_up, v_ffn2_w_down, v_final_norm):
    names = ['ffn1_norm', 'ffn1_w_gate', 'ffn1_w_up', 'ffn1_w_down', 'mix_norm', 'w_in', 'conv_dw', 'conv_dw_bias',
             'conv_ln_g', 'conv_ln_b', 'lru_conv_w', 'lru_conv_b', 'lru_w_a', 'lru_b_a', 'lru_w_x', 'lru_b_x',
             'lru_lambda', 'w_out', 'ffn2_norm', 'ffn2_w_gate', 'ffn2_w_up', 'ffn2_w_down', 'final_norm']
    env = dict(locals())
    W = {n: env[n] for n in names}
    M = {n: env['m_' + n] for n in names}
    V = {n: env['v_' + n] for n in names}

    xi, yi, ci = _here()
    chip = 2 * xi + yi
    cidx = ci.astype(jnp.int32).reshape(1)
    T, D = x.shape[-2], x.shape[-1]
    xs = x.reshape(T, D)
    tgt = loss_target.reshape(T, D)
    K, Cs = conv_dw.shape
    C = conv_dw_bias.shape[0]
    Wl = lru_conv_b.shape[0]
    K4 = lru_conv_w.shape[0]
    heads, dh, _ = lru_w_a.shape
    per = LANES // dh

    def row(v):
        return v.reshape(1, -1)

    tform = ('ffn1_w_gate', 'ffn1_w_up', 'ffn2_w_gate', 'ffn2_w_up')
    for n in tform:
        W[n], M[n], V[n] = W[n].T, M[n].T, V[n].T
    kp = -(-K // SUBLANES) * SUBLANES
    taps = jnp.concatenate([conv_dw, jnp.zeros((kp - K, Cs), F32), lru_conv_w,
                            jnp.zeros((2 * SUBLANES - K4, Cs), F32)], axis=0)
    idx = jnp.stack([ci, chip]).astype(jnp.int32)
    (wff1,) = _gather_weights([_place_cast([W['ffn1_w_gate'], W['ffn1_w_up'], ffn1_w_down], idx, BF16, "place_ffn1")])
    mixl = [_place_cast([w_in], idx, BF16, "place_w_in"), _place_cast([w_out], idx, BF16, "place_w_out"),
            _place_cast([taps], idx, F32, "place_taps")]
    msend, mrecv, mixl, mtok = _gather_start(mixl, wff1, "gather_mix_start")
    ff2l = _place_cast([W['ffn2_w_gate'], W['ffn2_w_up'], ffn2_w_down], idx, BF16, "place_ffn2")
    fsend, frecv, ff2l, ftok = _gather_start([ff2l], mtok, "gather_ffn2_start")
    wa_bd = _block_diag(lru_w_a, per).astype(BF16)
    wx_bd = _block_diag(lru_w_x, per).astype(BF16)

    x1, a1, b1 = _ffn_fwd(xs, row(ffn1_norm) + ftok[0:1, 0:1], wff1, "ffn1_fwd")
    win, wout, taps = _gather_wait(msend, mrecv, mixl, x1, "gather_mix_wait")
    win, wout, taps = win[0], wout.reshape(-1, D), taps[0]
    conv_w_full = taps[:, :K].transpose(1, 0, 2).reshape(K, N_CHIPS * Cs)
    lru_w4_full = taps[:, kp:kp + K4].transpose(1, 0, 2).reshape(K4, N_CHIPS * Cs)
    z = _mix_in_fwd(x1, row(mix_norm), win)
    u, u1 = _conv_fwd(z, conv_w_full, row(conv_dw_bias), row(conv_ln_g), row(conv_ln_b))
    yr, hs = _lru_fwd(z, 2 * C, lru_w4_full, row(lru_conv_b), wa_bd, row(lru_b_a), wx_bd, row(lru_b_x),
                      row(lru_lambda))
    x2 = _mix_out_fwd(x1, u, yr, wout)
    (wff2,) = _gather_wait(fsend, frecv, ff2l, x2, "gather_ffn2_wait")
    x3, a2, b2 = _ffn_fwd(x2, row(ffn2_norm), wff2, "ffn2_fwd")
    dx3, loss_blk, d_final = _final_loss(x3, row(final_norm), tgt)

    dx2, da2, db2, p2, hb2, dyh2, d_ffn2n = _ffn_bwd_tok(dx3, x2, row(ffn2_norm), a2, b2, wff2, "ffn2_bwd")
    dwg2, dwu2 = _ffn_wgrad([da2, db2], hb2, ftok, "ffn2_wgrad_gu")
    (dwd2,) = _ffn_wgrad([p2], dyh2, ftok, "ffn2_wgrad_d")
    wsend, wrecv, f2g, f2o, wtok = _swap_start([dwg2, dwu2, dwd2], "swap_ffn2_start")
    dcat, dwout = _mix_out_bwd(dx2, u, yr, wout)
    dzc, cst = _conv_bwd(dcat, u1, z, conv_w_full, row(conv_ln_g) + wtok[0:1, 0:1], row(conv_ln_b))
    dzx, dzg, lst, dwa_bd, dwx_bd = _lru_bwd(dcat, C, hs, z, 2 * C, lru_w4_full, row(lru_conv_b), wa_bd,
                                              row(lru_b_a), wx_bd, row(lru_b_x), row(lru_lambda))
    dx1, dwin, d_mixn = _mix_in_bwd(dzc, dzx, dzg, x1, dx2, row(mix_norm), win)

    early_names = ['w_in', 'w_out', 'ffn2_w_gate', 'ffn2_w_up', 'ffn2_w_down']
    mixg = [dwin, dwout.reshape(N_CHIPS, -1, D)]
    mixo = _swap_halves_out(mixg, "swap_halves_mix")
    f2g, f2o = _swap_wait(wsend, wrecv, f2g, f2o, dwin, "swap_ffn2_wait")
    e_parts = [_add_cast(g, o, cidx, "add_cast_" + n)
               for g, o, n in zip(mixg + list(f2g), list(mixo) + list(f2o), early_names)]
    esend, erecv, e_parts, e_lands, etok = _exchange_start(e_parts, "exchange_early_start")

    dx0, da1, db1, p1, hb1, dyh1, d_ffn1n = _ffn_bwd_tok(dx1, xs, row(ffn1_norm) + etok[0:1, 0:1], a1, b1, wff1,
                                                         "ffn1_bwd")

    small_names = ['ffn1_norm', 'mix_norm', 'conv_dw', 'conv_dw_bias', 'conv_ln_g', 'conv_ln_b', 'lru_conv_w',
                   'lru_conv_b', 'lru_w_a', 'lru_b_a', 'lru_w_x', 'lru_b_x', 'lru_lambda', 'ffn2_norm',
                   'final_norm']
    small = {
        'ffn1_norm': d_ffn1n, 'mix_norm': d_mixn, 'conv_dw': cst[:K], 'conv_dw_bias': cst[K + 1],
        'conv_ln_g': cst[K + 2], 'conv_ln_b': cst[K + 3], 'lru_conv_w': lst[:K4], 'lru_conv_b': lst[K4],
        'lru_w_a': _block_diag_take(dwa_bd, per), 'lru_b_a': lst[K4 + 1],
        'lru_w_x': _block_diag_take(dwx_bd, per), 'lru_b_x': lst[K4 + 2], 'lru_lambda': lst[K4 + 3],
        'ffn2_norm': d_ffn2n, 'final_norm': d_final,
    }
    packed, rows = _pack([small[n] for n in small_names] + [loss_blk[0:1, 0:1]])
    ssend, srecv, packed, sslots, stok = _small_start(packed)

    gu_names, d_names = ['ffn1_w_gate', 'ffn1_w_up'], ['ffn1_w_down']
    gu = _ffn_wgrad([da1, db1], hb1, stok, "ffn1_wgrad_gu")
    gu_parts = [_add_cast(g, o, cidx, "add_cast_" + n)
                for g, o, n in zip(gu, _swap_halves_out(gu, "swap_halves_gu"), gu_names)]
    gsend, grecv, gu_parts, gu_lands, gtok = _exchange_start(gu_parts, "exchange_gu_start")
    dn = _ffn_wgrad([p1], dyh1, gtok, "ffn1_wgrad_d")
    dwd1 = dn[0]
    d_parts = [_add_cast(g, o, cidx, "add_cast_" + n)
               for g, o, n in zip(dn, _swap_halves_out(dn, "swap_halves_d"), d_names)]
    dsend, drecv, d_parts, d_lands, ltok = _exchange_start(d_parts, "exchange_d_start")
    e_parts, e_slots = _exchange_wait(esend, erecv, e_parts, e_lands, ltok, "exchange_early_wait")
    delta, new_m, new_v = {}, {}, {}

    def finish(group, parts, slots, tag):
        halves = [_sum_slots(p, b, idx, "sum_slots_" + n) for p, b, n in zip(parts, slots, group)]
        for n, g in zip(group, _share_halves(halves, "share_halves_" + tag)):
            G[n] = g
            delta[n], new_m[n], new_v[n] = _adamw(W[n], g, M[n], V[n], "adamw_" + n)

    G = {}
    finish(early_names, e_parts, e_slots, "early")

    full_shapes = [(K, C) if n == 'conv_dw' else (K4, Wl) if n == 'lru_conv_w' else W[n].shape for n in small_names]
    packed, sslots = _small_wait(ssend, srecv, packed, sslots, dwd1)
    summed = _sum_devices(packed, sslots, (4 * xi + 2 * yi + ci).astype(jnp.int32).reshape(1))
    *small_sums, loss_sum = _unpack(summed, rows, full_shapes + [(1, 1)])
    for n, gsum in zip(small_names, small_sums):
        if n == 'conv_dw':
            gsum = lax.dynamic_slice_in_dim(gsum, chip * Cs, Cs, axis=1)
        elif n == 'lru_conv_w':
            gsum = lax.dynamic_slice_in_dim(gsum, chip * lru_conv_w.shape[1], lru_conv_w.shape[1], axis=1)
        G[n] = gsum

    pw, prow = _pack([W[n] for n in small_names])
    pg, _ = _pack([G[n] for n in small_names])
    pm, _ = _pack([M[n] for n in small_names])
    pv, _ = _pack([V[n] for n in small_names])
    sd, sm, sv = _adamw(pw, pg, pm, pv, "adamw_small")
    shapes = [W[n].shape for n in small_names]
    for n, a, b, c_ in zip(small_names, _unpack(sd, prow, shapes), _unpack(sm, prow, shapes),
                           _unpack(sv, prow, shapes)):
        delta[n], new_m[n], new_v[n] = a, b, c_

    done = sd[0:SUBLANES] + delta[early_names[-1]][0:SUBLANES, 0:LANES]
    gu_parts, gu_slots = _exchange_wait(gsend, grecv, gu_parts, gu_lands, done, "exchange_gu_wait")
    d_parts, d_slots = _exchange_wait(dsend, drecv, d_parts, d_lands, gu_slots[0], "exchange_d_wait")
    finish(gu_names + d_names, list(gu_parts) + list(d_parts), list(gu_slots) + list(d_slots), "last")

    loss = loss_sum[0, 0]
    grad_x = dx0.reshape(x.shape)
    for n in tform:
        G[n], delta[n], new_m[n], new_v[n] = G[n].T, delta[n].T, new_m[n].T, new_v[n].T
    return (loss, grad_x, *[G[n] for n in names], *[delta[n] for n in names],
            *[new_m[n] for n in names], *[new_v[n] for n in names])
```

```python
import functools
import math

import jax
import jax.numpy as jnp
from jax import lax
from jax.experimental import pallas as pl
from jax.experimental.pallas import tpu as pltpu

F32 = jnp.float32
BF16 = jnp.bfloat16
MESH = pl.DeviceIdType.MESH

RMS_EPS = 1e-6
LN_EPS = 1e-5
LRU_C = 8.0
FFN_RES_SCALE = 0.5
ADAM_LR = 0.001
ADAM_B1 = 0.9
ADAM_B2 = 0.999
ADAM_EPS = 1e-08
ADAM_WD = 0.01
ADAM_STEP = 10

LANES = 128
SUBLANES = 8
CONV_HALO = 32
LRU_HALO = 8
ROW_CHUNK = 64
VMEM_LIMIT = 56 * 1024 * 1024
N_CHIPS = 4
N_DEV = 8
TOK_TILE = 1024
BWD_TILE = 512
FFN_BWD_TILE = 512
BWD_ROWS = 32
FFN_BWD_CHAIN = 256
CONV_TILE = 512
LRU_TILE = 1024
LRU_GROUPS = 4


def _dot(a, b):
    return jnp.dot(a, b, preferred_element_type=F32)


def _dot_nt(a, b):
    return lax.dot_general(a, b, (((1,), (1,)), ((), ())), preferred_element_type=F32)


def _dot_tn(a, b):
    return lax.dot_general(a, b, (((0,), (0,)), ((), ())), preferred_element_type=F32)


def _tile(n, pref, mult=SUBLANES):
    for t in range(min(pref, n), 0, -1):
        if n % t == 0 and t % mult == 0:
            return t
    return n


def _params(*sem):
    return pltpu.CompilerParams(dimension_semantics=sem, vmem_limit_bytes=VMEM_LIMIT)


def _rms_stats(x):
    r = lax.rsqrt(jnp.mean(x * x, axis=-1, keepdims=True) + RMS_EPS)
    return x * r, r


def _rms_bwd(dh, xh, r, g):
    dxh = dh * g
    return r * (dxh - xh * jnp.mean(dxh * xh, axis=-1, keepdims=True))


def _colsum(v):
    return jnp.sum(v, axis=0, keepdims=True)


def _ffn_fwd(x, g, wff, name, head=None):
    T, D = x.shape
    ns, fs = wff.shape[1], wff.shape[2]
    tm = _tile(T, TOK_TILE)
    mc = _tile(tm, FFN_BWD_CHAIN, 16)
    rc = _tile(tm, FFN_BWD_CHAIN)

    def body(*refs):
        x_ref, g_ref, wg_ref, wu_ref, wd_ref = refs[:5]
        if head is None:
            y_ref, a_ref, b_ref, hb_ref, acc_ref = refs[5:]
        else:
            gf_ref, t_ref, y_ref, a_ref, b_ref, loss_ref, dgf_ref, hb_ref, acc_ref = refs[5:]
        j = pl.program_id(1)

        @pl.when(j == 0)
        def _():
            xh, _ = _rms_stats(x_ref[...])
            hb_ref[...] = (xh * g_ref[...]).astype(BF16)
            acc_ref[...] = jnp.zeros_like(acc_ref)

        if head is not None:
            @pl.when((pl.program_id(0) == 0) & (j == 0))
            def _():
                loss_ref[...] = jnp.zeros_like(loss_ref)
                dgf_ref[...] = jnp.zeros_like(dgf_ref)

        for q0 in range(0, tm, mc):
            blk = pl.ds(q0, mc)
            hb = hb_ref[blk, :]
            a = _dot_nt(hb, wg_ref[...])
            b = _dot_nt(hb, wu_ref[...])
            a_ref[blk, :] = a.astype(BF16)
            b_ref[blk, :] = b.astype(BF16)
            p = (a * jax.nn.sigmoid(a) * b).astype(BF16)
            acc_ref[blk, :] += _dot(p, wd_ref[...])

        @pl.when(j == ns - 1)
        def _():
            if head is None:
                y_ref[...] = x_ref[...] + FFN_RES_SCALE * acc_ref[...]
                return
            gv = gf_ref[...]
            loss = jnp.zeros((), F32)
            dg = jnp.zeros((1, D), F32)
            for r0 in range(0, tm, rc):
                rows = pl.ds(r0, rc)
                xh, r = _rms_stats(x_ref[rows, :] + FFN_RES_SCALE * acc_ref[rows, :])
                e = xh * gv - t_ref[rows, :]
                loss = loss + 0.5 * jnp.sum(jnp.mean(e * e, axis=-1, keepdims=True))
                dy = e * (1.0 / D)
                dg = dg + _colsum(dy * xh)
                y_ref[rows, :] = _rms_bwd(dy, xh, r, gv)
            loss_ref[...] += loss
            dgf_ref[...] += dg

    def wspec(n):
        return pl.BlockSpec((None, None, fs, D), lambda i, j: (n, j, 0, 0))

    tok = pl.BlockSpec((tm, D), lambda i, j: (i, 0))
    vec = pl.BlockSpec((1, D), lambda i, j: (0, 0))
    mid = pl.BlockSpec((None, tm, fs), lambda i, j: (j, i, 0))
    in_specs = [tok, vec, wspec(0), wspec(1), wspec(2)]
    out_specs = [tok, mid, mid]
    out_shape = [jax.ShapeDtypeStruct((T, D), F32), jax.ShapeDtypeStruct((ns, T, fs), BF16),
                 jax.ShapeDtypeStruct((ns, T, fs), BF16)]
    args = [x, g, wff, wff, wff]
    if head is not None:
        in_specs += [vec, tok]
        out_specs += [pl.BlockSpec((SUBLANES, LANES), lambda i, j: (0, 0)), vec]
        out_shape += [jax.ShapeDtypeStruct((SUBLANES, LANES), F32), jax.ShapeDtypeStruct((1, D), F32)]
        args += list(head)
    return pl.pallas_call(
        body, grid=(T // tm, ns), in_specs=in_specs, out_specs=out_specs, out_shape=out_shape,
        scratch_shapes=[pltpu.VMEM((tm, D), BF16), pltpu.VMEM((tm, D), F32)],
        compiler_params=_params("arbitrary", "arbitrary"), name=name)(*args)


def _ffn_bwd_tok(dy, x, g, a, b, wff, name):
    T, D = x.shape
    ns, fs = wff.shape[1], wff.shape[2]
    tm = _tile(T, FFN_BWD_TILE)
    rc = _tile(tm, BWD_ROWS)
    mc = _tile(tm, FFN_BWD_CHAIN, rc)

    def body(dy_ref, x_ref, g_ref, a_ref, b_ref, wg_ref, wu_ref, wd0_ref, wdn_ref,
             dx_ref, da_ref, db_ref, p_ref, hb_ref, dyh_ref, dg_ref, dh_ref, dp_ref):
        i, j = pl.program_id(0), pl.program_id(1)
        cur = dp_ref.at[j % 2]
        nxt = dp_ref.at[(j + 1) % 2]

        @pl.when((i == 0) & (j == 0))
        def _():
            dg_ref[...] = jnp.zeros_like(dg_ref)

        @pl.when(j == 0)
        def _():
            for r0 in range(0, tm, rc):
                rows = pl.ds(r0, rc)
                xh, _ = _rms_stats(x_ref[rows, :])
                hb_ref[rows, :] = (xh * g_ref[...]).astype(BF16)
                dyh_ref[rows, :] = (FFN_RES_SCALE * dy_ref[rows, :]).astype(BF16)
            dh_ref[...] = jnp.zeros_like(dh_ref)
            cur[...] = _dot_nt(dyh_ref[...], wd0_ref[...])

        def chains(with_next):
            for q0 in range(0, tm, mc):
                blk = pl.ds(q0, mc)
                for r0 in range(q0, q0 + mc, rc):
                    rows = pl.ds(r0, rc)
                    av = a_ref[rows, :].astype(F32)
                    bv = b_ref[rows, :].astype(F32)
                    dp = cur[rows, :]
                    s = jax.nn.sigmoid(av)
                    sl = av * s
                    da_ref[rows, :] = (dp * bv * (s * (1.0 + av * (1.0 - s)))).astype(BF16)
                    db_ref[rows, :] = (dp * sl).astype(BF16)
                    p_ref[rows, :] = (sl * bv).astype(BF16)
                if with_next:
                    nxt[blk, :] = _dot_nt(dyh_ref[blk, :], wdn_ref[...])
                dh_ref[blk, :] += _dot(da_ref[blk, :], wg_ref[...]) + _dot(db_ref[blk, :], wu_ref[...])

        pl.when(j < ns - 1)(functools.partial(chains, True))
        pl.when(j == ns - 1)(functools.partial(chains, False))

        @pl.when(j == ns - 1)
        def _():
            gv = g_ref[...]
            dg = jnp.zeros((1, D), F32)
            for r0 in range(0, tm, rc):
                rows = pl.ds(r0, rc)
                xh, r = _rms_stats(x_ref[rows, :])
                dh = dh_ref[rows, :]
                dx_ref[rows, :] = dy_ref[rows, :] + _rms_bwd(dh, xh, r, gv)
                dg = dg + _colsum(dh * xh)
            dg_ref[...] += dg

    def wspec(n):
        return pl.BlockSpec((None, None, fs, D), lambda i, j: (n, j, 0, 0))

    tok = pl.BlockSpec((tm, D), lambda i, j: (i, 0))
    mid = pl.BlockSpec((None, tm, fs), lambda i, j: (j, i, 0))
    vec = pl.BlockSpec((1, D), lambda i, j: (0, 0))
    return pl.pallas_call(
        body, grid=(T // tm, ns),
        in_specs=[tok, tok, vec, mid, mid, wspec(0), wspec(1),
                  pl.BlockSpec((None, None, fs, D), lambda i, j: (2, 0, 0, 0)),
                  pl.BlockSpec((None, None, fs, D), lambda i, j: (2, jnp.minimum(j + 1, ns - 1), 0, 0))],
        out_specs=[tok, mid, mid, mid, tok, tok, vec],
        out_shape=[jax.ShapeDtypeStruct((T, D), F32),
                   jax.ShapeDtypeStruct((ns, T, fs), BF16), jax.ShapeDtypeStruct((ns, T, fs), BF16),
                   jax.ShapeDtypeStruct((ns, T, fs), BF16),
                   jax.ShapeDtypeStruct((T, D), BF16), jax.ShapeDtypeStruct((T, D), BF16),
                   jax.ShapeDtypeStruct((1, D), F32)],
        scratch_shapes=[pltpu.VMEM((tm, D), F32), pltpu.VMEM((2, tm, fs), F32)],
        compiler_params=_params("arbitrary", "arbitrary"), name=name)(dy, x, g, a, b, wff, wff, wff, wff)


def _ffn_wgrad(groups, after, name):
    flat = [(l, gi) for gi, (ls, _) in enumerate(groups) for l in ls]
    ng, n = len(groups), len(flat)
    T, D = groups[0][1].shape
    ns, _, fs = flat[0][0].shape
    tm = _tile(T, TOK_TILE)

    def body(*refs):
        rhs_refs, lhs_refs, out_refs = refs[:ng], refs[ng:ng + n], refs[ng + n + 1:]

        @pl.when(pl.program_id(1) == 0)
        def _():
            for o in out_refs:
                o[...] = jnp.zeros_like(o)

        rvs = [r[...] for r in rhs_refs]
        for l, o, (_, gi) in zip(lhs_refs, out_refs, flat):
            o[...] += _dot_tn(l[...], rvs[gi])

    tok = pl.BlockSpec((tm, D), lambda j, i: (i, 0))
    mid = pl.BlockSpec((None, tm, fs), lambda j, i: (j, i, 0))
    wsp = pl.BlockSpec((None, fs, D), lambda j, i: (j, 0, 0))
    sds = jax.ShapeDtypeStruct((ns, fs, D), F32)
    return pl.pallas_call(
        body, grid=(ns, T // tm),
        in_specs=[tok] * ng + [mid] * n + [pl.BlockSpec((SUBLANES, LANES), lambda j, i: (0, 0))],
        out_specs=[wsp] * n, out_shape=[sds] * n,
        compiler_params=_params("parallel", "arbitrary"), name=name)(
            *[r for _, r in groups], *[l for l, _ in flat], after)


def _mix_in_fwd(x, g, win):
    T, D = x.shape
    ns, ws = win.shape[0], win.shape[2]
    tm = _tile(T, TOK_TILE)

    def body(x_ref, g_ref, w_ref, z_ref):
        xh, _ = _rms_stats(x_ref[...])
        hb = (xh * g_ref[...]).astype(BF16)
        for j in range(ns):
            z_ref[:, pl.ds(j * ws, ws)] = _dot(hb, w_ref[j])

    return pl.pallas_call(
        body, grid=(T // tm,),
        in_specs=[pl.BlockSpec((tm, D), lambda i: (i, 0)), pl.BlockSpec((1, D), lambda i: (0, 0)),
                  pl.BlockSpec((ns, D, ws), lambda i: (0, 0, 0), pipeline_mode=pl.Buffered(1))],
        out_specs=pl.BlockSpec((tm, ns * ws), lambda i: (i, 0)),
        out_shape=jax.ShapeDtypeStruct((T, ns * ws), F32),
        compiler_params=_params("parallel"), name="mix_in_fwd")(x, g, win)


def _tap_sum(buf, w_ref, ntaps, first_row, r0, rows, flip):
    acc = None
    for k in range(ntaps):
        off = (ntaps - 1 - k) if flip else k
        t = buf[pl.ds(first_row + r0 + off, rows), :] * w_ref[pl.ds(k, 1), :]
        acc = t if acc is None else acc + t
    return acc


def _shift_copies(buf, sh, rows):
    for r in range(1, SUBLANES):
        sh[r - 1, pl.ds(0, rows), :] = buf[pl.ds(r, rows), :]


def _tap_rows(buf, sh, off, r0, rows):
    r = off % SUBLANES
    if r == 0:
        return buf[pl.ds(off + r0, rows), :]
    return sh[r - 1, pl.ds(off - r + r0, rows), :]


def _tap_sum_tiles(buf, sh, w_ref, ntaps, first_row, r0, rows, flip):
    acc = None
    for k in range(ntaps):
        off = first_row + ((ntaps - 1 - k) if flip else k)
        t = _tap_rows(buf, sh, off, r0, rows) * w_ref[pl.ds(k, 1), :]
        acc = t if acc is None else acc + t
    return acc


def _conv_fwd(z, w, bias, lng, lnb):
    T = z.shape[0]
    K, C = w.shape
    tm = _tile(T, CONV_TILE, ROW_CHUNK)
    rc = min(ROW_CHUNK, tm)
    srows = tm + CONV_HALO - SUBLANES

    def body(cv_ref, cg_ref, w_ref, b_ref, g_ref, bb_ref, u_ref, u1_ref, buf, sh):
        @pl.when(pl.program_id(0) == 0)
        def _():
            buf[pl.ds(0, CONV_HALO), :] = jnp.zeros((CONV_HALO, C), F32)

        buf[pl.ds(CONV_HALO, tm), :] = cv_ref[...] * jax.nn.sigmoid(cg_ref[...])
        _shift_copies(buf, sh, srows)
        for r0 in range(0, tm, rc):
            u1 = _tap_sum_tiles(buf, sh, w_ref, K, CONV_HALO - (K - 1), r0, rc, False) + b_ref[...]
            u1_ref[pl.ds(r0, rc), :] = u1
            xc = u1 - jnp.mean(u1, axis=-1, keepdims=True)
            xh = xc * lax.rsqrt(jnp.mean(xc * xc, axis=-1, keepdims=True) + LN_EPS)
            u2 = xh * g_ref[...] + bb_ref[...]
            u_ref[pl.ds(r0, rc), :] = (u2 * jax.nn.sigmoid(u2)).astype(BF16)
        buf[pl.ds(0, CONV_HALO), :] = buf[pl.ds(tm, CONV_HALO), :]

    vec = pl.BlockSpec((1, C), lambda i: (0, 0))
    return pl.pallas_call(
        body, grid=(T // tm,),
        in_specs=[pl.BlockSpec((tm, C), lambda i: (i, 0)), pl.BlockSpec((tm, C), lambda i: (i, 1)),
                  pl.BlockSpec((K, C), lambda i: (0, 0)), vec, vec, vec],
        out_specs=[pl.BlockSpec((tm, C), lambda i: (i, 0)), pl.BlockSpec((tm, C), lambda i: (i, 0))],
        out_shape=[jax.ShapeDtypeStruct((T, C), BF16), jax.ShapeDtypeStruct((T, C), F32)],
        scratch_shapes=[pltpu.VMEM((CONV_HALO + tm, C), F32), pltpu.VMEM((SUBLANES - 1, srows, C), F32)],
        compiler_params=_params("arbitrary"), name="conv_fwd")(z, z, w, bias, lng, lnb)


def _conv_bwd(dcat, u1, z, w, lng, lnb):
    T = z.shape[0]
    K, C = w.shape
    tm = _tile(T, CONV_TILE, ROW_CHUNK)
    rc = min(ROW_CHUNK, tm)
    nI = T // tm
    hb = tm // CONV_HALO
    srows = ((K + 4 + SUBLANES - 1) // SUBLANES) * SUBLANES
    shrows = tm + CONV_HALO - SUBLANES

    def body(du_ref, u1_ref, cv_ref, cg_ref, cvp_ref, cgp_ref, w_ref, g_ref, bb_ref,
             dz_ref, st_ref, u0buf, d1buf, ush, dsh):
        i = pl.program_id(0)
        ti = nI - 1 - i

        @pl.when(i == 0)
        def _():
            st_ref[...] = jnp.zeros_like(st_ref)
            d1buf[pl.ds(tm, CONV_HALO), :] = jnp.zeros((CONV_HALO, C), F32)

        prev = cvp_ref[...] * jax.nn.sigmoid(cgp_ref[...])
        u0buf[pl.ds(0, CONV_HALO), :] = jnp.where(ti == 0, 0.0, prev)
        u0buf[pl.ds(CONV_HALO, tm), :] = cv_ref[...] * jax.nn.sigmoid(cg_ref[...])

        gv = g_ref[...]
        dbias = jnp.zeros((1, C), F32)
        dgain = jnp.zeros((1, C), F32)
        dlnb = jnp.zeros((1, C), F32)
        for r0 in range(0, tm, rc):
            u1 = u1_ref[pl.ds(r0, rc), :]
            xc = u1 - jnp.mean(u1, axis=-1, keepdims=True)
            rstd = lax.rsqrt(jnp.mean(xc * xc, axis=-1, keepdims=True) + LN_EPS)
            xh = xc * rstd
            u2 = xh * gv + bb_ref[...]
            s = jax.nn.sigmoid(u2)
            du2 = du_ref[pl.ds(r0, rc), :] * (s * (1.0 + u2 * (1.0 - s)))
            dgain = dgain + _colsum(du2 * xh)
            dlnb = dlnb + _colsum(du2)
            dxh = du2 * gv
            du1 = rstd * (dxh - jnp.mean(dxh, axis=-1, keepdims=True)
                          - xh * jnp.mean(dxh * xh, axis=-1, keepdims=True))
            dbias = dbias + _colsum(du1)
            d1buf[pl.ds(r0, rc), :] = du1
        st_ref[pl.ds(K + 1, 1), :] += dbias
        st_ref[pl.ds(K + 2, 1), :] += dgain
        st_ref[pl.ds(K + 3, 1), :] += dlnb

        _shift_copies(u0buf, ush, shrows)
        _shift_copies(d1buf, dsh, shrows)
        for k in range(K):
            acc = jnp.zeros((SUBLANES, C), F32)
            for r0 in range(0, tm, rc):
                prod = d1buf[pl.ds(r0, rc), :] * _tap_rows(u0buf, ush, CONV_HALO - (K - 1) + k, r0, rc)
                acc = acc + jnp.sum(prod.reshape(rc // SUBLANES, SUBLANES, C), axis=0)
            st_ref[pl.ds(k, 1), :] += _colsum(acc)

        for r0 in range(0, tm, rc):
            du0 = _tap_sum_tiles(d1buf, dsh, w_ref, K, 0, r0, rc, True)
            cv = cv_ref[pl.ds(r0, rc), :]
            sg = jax.nn.sigmoid(cg_ref[pl.ds(r0, rc), :])
            dz_ref[pl.ds(r0, rc), pl.ds(0, C)] = (du0 * sg).astype(BF16)
            dz_ref[pl.ds(r0, rc), pl.ds(C, C)] = (du0 * cv * sg * (1.0 - sg)).astype(BF16)
        d1buf[pl.ds(tm, CONV_HALO), :] = d1buf[pl.ds(0, CONV_HALO), :]

    def rev(col):
        return lambda i: (nI - 1 - i, col)

    def rev_prev(col):
        return lambda i: (jnp.maximum((nI - 1 - i) * hb - 1, 0), col)

    vec = pl.BlockSpec((1, C), lambda i: (0, 0))
    return pl.pallas_call(
        body, grid=(nI,),
        in_specs=[pl.BlockSpec((tm, C), rev(0)), pl.BlockSpec((tm, C), rev(0)),
                  pl.BlockSpec((tm, C), rev(0)), pl.BlockSpec((tm, C), rev(1)),
                  pl.BlockSpec((CONV_HALO, C), rev_prev(0)), pl.BlockSpec((CONV_HALO, C), rev_prev(1)),
                  pl.BlockSpec((K, C), lambda i: (0, 0)), vec, vec],
        out_specs=[pl.BlockSpec((tm, 2 * C), rev(0)), pl.BlockSpec((srows, C), lambda i: (0, 0))],
        out_shape=[jax.ShapeDtypeStruct((T, 2 * C), BF16), jax.ShapeDtypeStruct((srows, C), F32)],
        scratch_shapes=[pltpu.VMEM((CONV_HALO + tm, C), F32), pltpu.VMEM((tm + CONV_HALO, C), F32),
                        pltpu.VMEM((SUBLANES - 1, shrows, C), F32), pltpu.VMEM((SUBLANES - 1, shrows, C), F32)],
        compiler_params=_params("arbitrary"), name="conv_bwd")(dcat, u1, z, z, z, z, w, lng, lnb)


def _softplus(v):
    return jnp.maximum(v, 0.0) + jnp.log(1.0 + jnp.exp(-jnp.abs(v)))


def _gelu(v):
    c = math.sqrt(2.0 / math.pi)
    t = jnp.tanh(c * (v + 0.044715 * v * v * v))
    gl = 0.5 * v * (1.0 + t)
    dgl = 0.5 * (1.0 + t) + 0.5 * v * (1.0 - t * t) * c * (1.0 + 3.0 * 0.044715 * v * v)
    return gl, dgl


def _lru_gates(xr, wa, ba, wx, bx, lam):
    xb = xr.astype(BF16)
    r = jax.nn.sigmoid(_dot(xb, wa) + ba)
    ig = jax.nn.sigmoid(_dot(xb, wx) + bx)
    sp = _softplus(-lam)
    log_a = -LRU_C * r * sp
    a = jnp.exp(log_a)
    y = 2.0 * log_a
    series = -(y * (1.0 + y * (0.5 + y * (1.0 / 6.0 + y * (1.0 / 24.0)))))
    mult = jnp.sqrt(jnp.where(y > -0.02, series, 1.0 - jnp.exp(y)))
    return a, mult, r, ig, sp


def _scan_tile(a_s, b_s, h_s, p_s, carry, seg, reverse):
    hl = [jnp.zeros((SUBLANES, LANES), F32)] * LRU_GROUPS
    pr = [jnp.ones((SUBLANES, LANES), F32)] * LRU_GROUPS
    for n in range(seg):
        for g in range(LRU_GROUPS):
            rows = pl.ds(g * SUBLANES * seg + ((seg - 1 - n) if reverse else n), SUBLANES, stride=seg)
            av = a_s[rows, :]
            hl[g] = av * hl[g] + b_s[rows, :]
            pr[g] = av * pr[g]
            h_s[rows, :] = hl[g]
            p_s[rows, :] = pr[g]
    nseg = SUBLANES * LRU_GROUPS
    cs = [None] * nseg
    c = carry
    for s in (range(nseg - 1, -1, -1) if reverse else range(nseg)):
        g, r = divmod(s, SUBLANES)
        cs[s] = c
        c = hl[g][r:r + 1, :] + pr[g][r:r + 1, :] * c
    return cs, c


def _lru_fwd(z, col0, w4, b4, wa, ba, wx, bx, lam):
    T = z.shape[0]
    K4, W = w4.shape
    nC = W // LANES
    tm = _tile(T, LRU_TILE, SUBLANES * SUBLANES * LRU_GROUPS)
    seg = tm // (SUBLANES * LRU_GROUPS)
    cx, cg = col0 // LANES, (col0 + W) // LANES

    def body(rx_ref, rg_ref, w4_ref, b4_ref, wa_ref, ba_ref, wx_ref, bx_ref, lam_ref,
             yr_ref, hs_ref, xbuf, a_s, b_s, h_s, p_s, hc):
        @pl.when(pl.program_id(1) == 0)
        def _():
            xbuf[pl.ds(0, LRU_HALO), :] = jnp.zeros((LRU_HALO, LANES), F32)
            hc[...] = jnp.zeros_like(hc)

        xbuf[pl.ds(LRU_HALO, tm), :] = rx_ref[...]
        xr = _tap_sum(xbuf, w4_ref, K4, LRU_HALO - (K4 - 1), 0, tm, False) + b4_ref[...]
        a, mult, _, ig, _ = _lru_gates(xr, wa_ref[...], ba_ref[...], wx_ref[...], bx_ref[...], lam_ref[...])
        a_s[...] = a
        b_s[...] = mult * ig * xr
        cs, cout = _scan_tile(a_s, b_s, h_s, p_s, hc[pl.ds(0, 1), :], seg, False)
        hc[pl.ds(0, 1), :] = cout
        for s in range(SUBLANES * LRU_GROUPS):
            rows = pl.ds(s * seg, seg)
            h = h_s[rows, :] + p_s[rows, :] * cs[s]
            hs_ref[rows, :] = h
            gl, _ = _gelu(rg_ref[rows, :])
            yr_ref[rows, :] = (h * gl).astype(BF16)
        xbuf[pl.ds(0, LRU_HALO), :] = xbuf[pl.ds(tm, LRU_HALO), :]

    vec = pl.BlockSpec((1, LANES), lambda c, i: (0, c))
    mat = pl.BlockSpec((None, LANES, LANES), lambda c, i: (c, 0, 0))
    return pl.pallas_call(
        body, grid=(nC, T // tm),
        in_specs=[pl.BlockSpec((tm, LANES), lambda c, i: (i, cx + c)),
                  pl.BlockSpec((tm, LANES), lambda c, i: (i, cg + c)),
                  pl.BlockSpec((K4, LANES), lambda c, i: (0, c)), vec, mat, vec, mat, vec, vec],
        out_specs=[pl.BlockSpec((tm, LANES), lambda c, i: (i, c)), pl.BlockSpec((tm, LANES), lambda c, i: (i, c))],
        out_shape=[jax.ShapeDtypeStruct((T, W), BF16), jax.ShapeDtypeStruct((T, W), F32)],
        scratch_shapes=[pltpu.VMEM((LRU_HALO + tm, LANES), F32)] + [pltpu.VMEM((tm, LANES), F32)] * 4
        + [pltpu.VMEM((SUBLANES, LANES), F32)],
        compiler_params=_params("parallel", "arbitrary"), name="lru_fwd")(z, z, w4, b4, wa, ba, wx, bx, lam)


def _lru_bwd(dcat, dcol0, hs, z, col0, w4, b4, wa, ba, wx, bx, lam):
    T = z.shape[0]
    K4, W = w4.shape
    assert K4 + 4 == SUBLANES
    nC = W // LANES
    tm = _tile(T, LRU_TILE, SUBLANES * SUBLANES * LRU_GROUPS)
    seg = tm // (SUBLANES * LRU_GROUPS)
    nI = T // tm
    hb = tm // LRU_HALO
    cx, cg, cd = col0 // LANES, (col0 + W) // LANES, dcol0 // LANES

    def body(dyr_ref, hs_ref, hsp_ref, rx_ref, rxp_ref, rg_ref, w4_ref, b4_ref, wa_ref, ba_ref, wx_ref, bx_ref,
             lam_ref, dzx_ref, dzg_ref, st_ref, dwa_ref, dwx_ref, xbuf, hbuf, abuf, a_s, b_s, h_s, p_s, dbuf, gc, anc):
        i = pl.program_id(1)
        ti = nI - 1 - i

        @pl.when(i == 0)
        def _():
            st_ref[...] = jnp.zeros_like(st_ref)
            dwa_ref[...] = jnp.zeros_like(dwa_ref)
            dwx_ref[...] = jnp.zeros_like(dwx_ref)
            gc[...] = jnp.zeros_like(gc)
            anc[...] = jnp.zeros_like(anc)
            dbuf[pl.ds(tm, LRU_HALO), :] = jnp.zeros((LRU_HALO, LANES), F32)

        xbuf[pl.ds(0, LRU_HALO), :] = jnp.where(ti == 0, 0.0, rxp_ref[...])
        xbuf[pl.ds(LRU_HALO, tm), :] = rx_ref[...]
        hbuf[pl.ds(0, LRU_HALO), :] = jnp.where(ti == 0, 0.0, hsp_ref[...])
        hbuf[pl.ds(LRU_HALO, tm), :] = hs_ref[...]

        wa, wx = wa_ref[...], wx_ref[...]
        lam_v = lam_ref[...]
        xr = _tap_sum(xbuf, w4_ref, K4, LRU_HALO - (K4 - 1), 0, tm, False) + b4_ref[...]
        a, mult, r, ig, sp = _lru_gates(xr, wa, ba_ref[...], wx, bx_ref[...], lam_v)

        dyr = dyr_ref[...]
        gl, dgl = _gelu(rg_ref[...])
        dzg_ref[...] = (dyr * hs_ref[...] * dgl).astype(BF16)

        abuf[pl.ds(0, tm), :] = a
        abuf[pl.ds(tm, LRU_HALO), :] = anc[...]
        a_s[...] = abuf[pl.ds(1, tm), :]
        b_s[...] = dyr * gl
        cs, cout = _scan_tile(a_s, b_s, h_s, p_s, gc[pl.ds(0, 1), :], seg, True)
        gc[pl.ds(0, 1), :] = cout
        anc[pl.ds(0, 1), :] = a[0:1, :]
        for s in range(SUBLANES * LRU_GROUPS):
            rows = pl.ds(s * seg, seg)
            b_s[rows, :] = h_s[rows, :] + p_s[rows, :] * cs[s]
        g = b_s[...]

        d_a = g * hbuf[pl.ds(LRU_HALO - 1, tm), :]
        gx_ = g * xr
        d_log_a = d_a * a - (gx_ * ig) * (a * a / mult)
        dga = (d_log_a * (-LRU_C * sp)) * r * (1.0 - r)
        dgx = (gx_ * mult) * ig * (1.0 - ig)
        dga_b, dgx_b = dga.astype(BF16), dgx.astype(BF16)
        dxr = g * mult * ig + _dot_nt(dga_b, wa) + _dot_nt(dgx_b, wx)
        xb = xr.astype(BF16)
        dwa_ref[...] += _dot_tn(xb, dga_b)
        dwx_ref[...] += _dot_tn(xb, dgx_b)
        st_ref[pl.ds(K4, 1), :] += _colsum(dxr)
        st_ref[pl.ds(K4 + 1, 1), :] += _colsum(dga)
        st_ref[pl.ds(K4 + 2, 1), :] += _colsum(dgx)
        st_ref[pl.ds(K4 + 3, 1), :] += _colsum(d_log_a * (-LRU_C * r)) * (-jax.nn.sigmoid(-lam_v))

        dbuf[pl.ds(0, tm), :] = dxr
        for k in range(K4):
            st_ref[pl.ds(k, 1), :] += _colsum(dxr * xbuf[pl.ds(LRU_HALO - (K4 - 1) + k, tm), :])
        dzx_ref[...] = _tap_sum(dbuf, w4_ref, K4, 0, 0, tm, True).astype(BF16)
        dbuf[pl.ds(tm, LRU_HALO), :] = dbuf[pl.ds(0, LRU_HALO), :]

    def rev(col):
        return lambda c, i: (nI - 1 - i, col + c)

    def rev_prev(col):
        return lambda c, i: (jnp.maximum((nI - 1 - i) * hb - 1, 0), col + c)

    vec = pl.BlockSpec((1, LANES), lambda c, i: (0, c))
    mat = pl.BlockSpec((None, LANES, LANES), lambda c, i: (c, 0, 0))
    big = pltpu.VMEM((tm, LANES), F32)
    halo = pltpu.VMEM((tm + LRU_HALO, LANES), F32)
    return pl.pallas_call(
        body, grid=(nC, nI),
        in_specs=[pl.BlockSpec((tm, LANES), rev(cd)),
                  pl.BlockSpec((tm, LANES), rev(0)), pl.BlockSpec((LRU_HALO, LANES), rev_prev(0)),
                  pl.BlockSpec((tm, LANES), rev(cx)), pl.BlockSpec((LRU_HALO, LANES), rev_prev(cx)),
                  pl.BlockSpec((tm, LANES), rev(cg)),
                  pl.BlockSpec((K4, LANES), lambda c, i: (0, c)), vec, mat, vec, mat, vec, vec],
        out_specs=[pl.BlockSpec((tm, LANES), rev(0)), pl.BlockSpec((tm, LANES), rev(0)),
                   pl.BlockSpec((SUBLANES, LANES), lambda c, i: (0, c)), mat, mat],
        out_shape=[jax.ShapeDtypeStruct((T, W), BF16), jax.ShapeDtypeStruct((T, W), BF16),
                   jax.ShapeDtypeStruct((SUBLANES, W), F32),
                   jax.ShapeDtypeStruct((nC, LANES, LANES), F32), jax.ShapeDtypeStruct((nC, LANES, LANES), F32)],
        scratch_shapes=[halo, halo, halo, big, big, big, big, halo,
                        pltpu.VMEM((SUBLANES, LANES), F32), pltpu.VMEM((SUBLANES, LANES), F32)],
        compiler_params=_params("parallel", "arbitrary"), name="lru_bwd")(
            dcat, hs, hs, z, z, z, w4, b4, wa, ba, wx, bx, lam)


def _mix_out_fwd(x, u, yr, wout):
    T, D = x.shape
    C, W = u.shape[1], yr.shape[1]
    tm = _tile(T, TOK_TILE)

    def body(x_ref, u_ref, yr_ref, w_ref, y_ref):
        y_ref[...] = (x_ref[...] + _dot(u_ref[...], w_ref[pl.ds(0, C), :])
                      + _dot(yr_ref[...], w_ref[pl.ds(C, W), :]))

    return pl.pallas_call(
        body, grid=(T // tm,),
        in_specs=[pl.BlockSpec((tm, D), lambda i: (i, 0)), pl.BlockSpec((tm, C), lambda i: (i, 0)),
                  pl.BlockSpec((tm, W), lambda i: (i, 0)),
                  pl.BlockSpec((C + W, D), lambda i: (0, 0), pipeline_mode=pl.Buffered(1))],
        out_specs=pl.BlockSpec((tm, D), lambda i: (i, 0)),
        out_shape=jax.ShapeDtypeStruct((T, D), F32),
        compiler_params=_params("parallel"), name="mix_out_fwd")(x, u, yr, wout)


def _mix_out_bwd(dy, u, yr, wout):
    T, D = dy.shape
    C, W = u.shape[1], yr.shape[1]
    tm = _tile(T, BWD_TILE)

    def body(dy_ref, u_ref, yr_ref, w_ref, dcat_ref, dw_ref):
        @pl.when(pl.program_id(0) == 0)
        def _():
            dw_ref[...] = jnp.zeros_like(dw_ref)

        dyb = dy_ref[...].astype(BF16)
        dcat_ref[...] = _dot_nt(dyb, w_ref[...])
        dw_ref[pl.ds(0, C), :] += _dot_tn(u_ref[...], dyb)
        dw_ref[pl.ds(C, W), :] += _dot_tn(yr_ref[...], dyb)

    return pl.pallas_call(
        body, grid=(T // tm,),
        in_specs=[pl.BlockSpec((tm, D), lambda i: (i, 0)), pl.BlockSpec((tm, C), lambda i: (i, 0)),
                  pl.BlockSpec((tm, W), lambda i: (i, 0)),
                  pl.BlockSpec((C + W, D), lambda i: (0, 0), pipeline_mode=pl.Buffered(1))],
        out_specs=[pl.BlockSpec((tm, C + W), lambda i: (i, 0)), pl.BlockSpec((C + W, D), lambda i: (0, 0))],
        out_shape=[jax.ShapeDtypeStruct((T, C + W), F32), jax.ShapeDtypeStruct((C + W, D), F32)],
        compiler_params=_params("arbitrary"), name="mix_out_bwd")(dy, u, yr, wout)


def _mix_in_bwd(dzc, dzx, dzg, x, dy, g, win):
    T, D = x.shape
    ns, ws = win.shape[0], win.shape[2]
    tm = _tile(T, BWD_TILE)
    parts = []
    for j in range(ns):
        lo = j * ws
        if lo < dzc.shape[1]:
            parts.append((0, lo))
        elif lo < dzc.shape[1] + dzx.shape[1]:
            parts.append((1, lo - dzc.shape[1]))
        else:
            parts.append((2, lo - dzc.shape[1] - dzx.shape[1]))

    def body(dzc_ref, dzx_ref, dzg_ref, x_ref, dy_ref, g_ref, w_ref, dx_ref, dw_ref, dg_ref):
        @pl.when(pl.program_id(0) == 0)
        def _():
            dw_ref[...] = jnp.zeros_like(dw_ref)
            dg_ref[...] = jnp.zeros_like(dg_ref)

        xh, r = _rms_stats(x_ref[...])
        gv = g_ref[...]
        hb = (xh * gv).astype(BF16)
        srcs = (dzc_ref, dzx_ref, dzg_ref)
        dh = jnp.zeros((tm, D), F32)
        for j, (si, off) in enumerate(parts):
            dzj = srcs[si][:, pl.ds(off, ws)]
            dh = dh + _dot_nt(dzj, w_ref[j])
            dw_ref[j] += _dot_tn(hb, dzj)
        dx_ref[...] = dy_ref[...] + _rms_bwd(dh, xh, r, gv)
        dg_ref[...] += _colsum(dh * xh)

    def tok(n):
        return pl.BlockSpec((tm, n), lambda i: (i, 0))

    vec = pl.BlockSpec((1, D), lambda i: (0, 0))
    return pl.pallas_call(
        body, grid=(T // tm,),
        in_specs=[tok(dzc.shape[1]), tok(dzx.shape[1]), tok(dzg.shape[1]), tok(D), tok(D), vec,
                  pl.BlockSpec((ns, D, ws), lambda i: (0, 0, 0), pipeline_mode=pl.Buffered(1))],
        out_specs=[tok(D), pl.BlockSpec((ns, D, ws), lambda i: (0, 0, 0)), vec],
        out_shape=[jax.ShapeDtypeStruct((T, D), F32), jax.ShapeDtypeStruct((ns, D, ws), F32),
                   jax.ShapeDtypeStruct((1, D), F32)],
        compiler_params=_params("arbitrary"), name="mix_in_bwd")(dzc, dzx, dzg, x, dy, g, win)


def _adamw(w, g, m, v, name):
    R, Cc = w.shape
    tr = _tile(R, max(SUBLANES, (1 << 19) // Cc))
    c1 = 1.0 - ADAM_B1 ** ADAM_STEP
    c2 = 1.0 - ADAM_B2 ** ADAM_STEP

    def body(w_ref, g_ref, m_ref, v_ref, d_ref, nm_ref, nv_ref):
        gv = g_ref[...]
        nm = ADAM_B1 * m_ref[...] + (1.0 - ADAM_B1) * gv
        nv = ADAM_B2 * v_ref[...] + (1.0 - ADAM_B2) * (gv * gv)
        nm_ref[...] = nm
        nv_ref[...] = nv
        d_ref[...] = -ADAM_LR * ((nm / c1) / (jnp.sqrt(nv / c2) + ADAM_EPS) + ADAM_WD * w_ref[...])

    blk = pl.BlockSpec((tr, Cc), lambda i: (i, 0))
    sds = jax.ShapeDtypeStruct((R, Cc), F32)
    return pl.pallas_call(
        body, grid=(R // tr,), in_specs=[blk] * 4, out_specs=[blk] * 3, out_shape=[sds] * 3,
        compiler_params=_params("parallel"), name=name)(w, g, m, v)


def _here():
    return lax.axis_index("x"), lax.axis_index("y"), lax.axis_index("c")


def _chip_at(x, y, m):
    return x ^ (m >> 1), y ^ (m & 1)


ANY = pl.BlockSpec(memory_space=pl.ANY)


def _place_cast(srcs, idx, dtype, name):
    n = len(srcs)
    R, Cc = srcs[0].shape
    tr = _tile(R, max(16, (1 << 18) // Cc), 16)

    def body(i_ref, *refs):
        o_ref = refs[n]
        for k in range(n):
            o_ref[k] = refs[k][...].astype(dtype)

    blk = pl.BlockSpec((tr, Cc), lambda i, s: (i, 0))
    return pl.pallas_call(
        body,
        grid_spec=pltpu.PrefetchScalarGridSpec(
            num_scalar_prefetch=1, grid=(R // tr,), in_specs=[blk] * n,
            out_specs=pl.BlockSpec((n, None, tr, Cc), lambda i, s: (0, s[1], i, 0))),
        out_shape=jax.ShapeDtypeStruct((n, N_CHIPS, R, Cc), dtype),
        compiler_params=_params("parallel"), name=name)(idx, *srcs)


def _gather_weights(lands):
    n = len(lands)

    def body(*refs):
        outs = refs[n:2 * n]
        send1, recv1, send2, recv2 = refs[2 * n:]
        x, y, c = _here()
        own = 2 * x + y

        def half(ref, chip, cc):
            rh = ref.shape[-2] // 2
            lead = (slice(None),) * (len(ref.shape) - 3)
            return ref.at[lead + (chip, pl.ds(cc * rh, rh), slice(None))]

        first = []
        for k in range(n):
            for m in (1, 2, 3):
                px, py = _chip_at(x, y, m)
                cp = pltpu.make_async_remote_copy(
                    src_ref=half(outs[k], own, c), dst_ref=half(outs[k], own, c),
                    send_sem=send1.at[k, m - 1], recv_sem=recv1.at[k, m - 1],
                    device_id=(px, py, c), device_id_type=MESH)
                cp.start()
                first.append(cp)

        passed = []
        for k in range(n):
            for m in (1, 2, 3):
                px, py = _chip_at(x, y, m)
                peer = 2 * px + py
                got = half(outs[k], peer, c)
                pltpu.make_async_remote_copy(
                    src_ref=got, dst_ref=got, send_sem=send1.at[k, m - 1], recv_sem=recv1.at[k, m - 1],
                    device_id=(px, py, c), device_id_type=MESH).wait_recv()
                cp = pltpu.make_async_remote_copy(
                    src_ref=got, dst_ref=got, send_sem=send2.at[k, m - 1], recv_sem=recv2.at[k, m - 1],
                    device_id=(x, y, 1 - c), device_id_type=MESH)
                cp.start()
                passed.append(cp)

        for k in range(n):
            for m in (1, 2, 3):
                px, py = _chip_at(x, y, m)
                other = half(outs[k], 2 * px + py, 1 - c)
                pltpu.make_async_remote_copy(
                    src_ref=other, dst_ref=other, send_sem=send2.at[k, m - 1], recv_sem=recv2.at[k, m - 1],
                    device_id=(x, y, 1 - c), device_id_type=MESH).wait_recv()
        for cp in first + passed:
            cp.wait_send()

    return pl.pallas_call(
        body, in_specs=[ANY] * n, out_specs=[ANY] * n,
        out_shape=[jax.ShapeDtypeStruct(a.shape, a.dtype) for a in lands],
        input_output_aliases={k: k for k in range(n)},
        scratch_shapes=[pltpu.SemaphoreType.DMA((n, 3)), pltpu.SemaphoreType.DMA((n, 3)),
                        pltpu.SemaphoreType.DMA((n, 3)), pltpu.SemaphoreType.DMA((n, 3))],
        name="gather_weights")(*lands)


HBM = pl.BlockSpec(memory_space=pltpu.HBM)
SEM = pl.BlockSpec(memory_space=pltpu.SEMAPHORE)
EFFECT = pltpu.SideEffectType.DATAFLOW_SIDE_EFFECTING


def _in_hbm(a):
    return pltpu.with_memory_space_constraint(a, pltpu.HBM)


def _gather_copies(land_refs, send, recv):
    x, y, c = _here()
    own = 2 * x + y
    cps = []
    for k in range(len(land_refs)):
        lead = (slice(None),) * (len(land_refs[k].shape) - 3)
        mine = land_refs[k].at[lead + (own,)]
        for m in (1, 2, 3):
            px, py = _chip_at(x, y, m)
            cps.append(pltpu.make_async_remote_copy(
                src_ref=mine, dst_ref=mine, send_sem=send.at[3 * k + m - 1], recv_sem=recv.at[3 * k + m - 1],
                device_id=(px, py, c), device_id_type=MESH))
    return cps


def _gather_start(lands, after, name):
    n = len(lands)

    def body(*refs):
        lz = refs[:n]
        send, recv = refs[n + 1], refs[n + 2]
        token = refs[-1]
        for cp in _gather_copies(lz, send, recv):
            cp.start()
        token[...] = jnp.zeros_like(token)

    hbm = [pltpu.HBM(a.shape, a.dtype) for a in lands]
    outs = pl.pallas_call(
        body, name=name,
        in_specs=[HBM] * n + [ANY],
        out_specs=[SEM, SEM] + [HBM] * n + [pl.BlockSpec(memory_space=pltpu.VMEM)],
        out_shape=[pltpu.SemaphoreType.DMA((3 * n,)), pltpu.SemaphoreType.DMA((3 * n,))] + hbm
        + [jax.ShapeDtypeStruct((SUBLANES, LANES), F32)],
        input_output_aliases={k: 2 + k for k in range(n)},
        compiler_params=pltpu.CompilerParams(has_side_effects=EFFECT),
    )(*[_in_hbm(a) for a in lands], after)
    return outs[0], outs[1], outs[2:2 + n], outs[-1]


def _gather_wait(send, recv, lands, after, name):
    n = len(lands)

    def body(*refs):
        lz = refs[:n]
        send_r, recv_r = refs[n], refs[n + 1]
        for cp in _gather_copies(lz, send_r, recv_r):
            cp.wait_send()
            cp.wait_recv()

    hbm = [pltpu.HBM(a.shape, a.dtype) for a in lands]
    return pl.pallas_call(
        body, name=name,
        in_specs=[HBM] * n + [SEM, SEM, ANY],
        out_specs=[HBM] * n, out_shape=hbm,
        input_output_aliases={k: k for k in range(n)},
        compiler_params=pltpu.CompilerParams(has_side_effects=EFFECT),
    )(*lands, send, recv, after)


def _exchange_copies(part_refs, slot_refs, send, recv):
    x, y, c = _here()
    cps = []
    for k in range(len(part_refs)):
        for m in (1, 2, 3):
            px, py = _chip_at(x, y, m)
            cps.append(pltpu.make_async_remote_copy(
                src_ref=part_refs[k].at[2 * px + py], dst_ref=slot_refs[k].at[m - 1],
                send_sem=send.at[3 * k + m - 1], recv_sem=recv.at[3 * k + m - 1],
                device_id=(px, py, c), device_id_type=MESH))
    return cps


def _exchange_start(parts, name):
    n = len(parts)
    lands = [lax.empty((N_CHIPS - 1,) + p.shape[1:], p.dtype) for p in parts]

    def body(*refs):
        ins, lz = refs[:n], refs[n:2 * n]
        send, recv = refs[2 * n], refs[2 * n + 1]
        token = refs[-1]
        for cp in _exchange_copies(ins, lz, send, recv):
            cp.start()
        token[...] = jnp.zeros_like(token)

    hbm = [pltpu.HBM(a.shape, a.dtype) for a in list(parts) + lands]
    outs = pl.pallas_call(
        body, name=name,
        in_specs=[HBM] * (2 * n),
        out_specs=[SEM, SEM] + [HBM] * (2 * n) + [pl.BlockSpec(memory_space=pltpu.VMEM)],
        out_shape=[pltpu.SemaphoreType.DMA((3 * n,)), pltpu.SemaphoreType.DMA((3 * n,))] + hbm
        + [jax.ShapeDtypeStruct((SUBLANES, LANES), F32)],
        input_output_aliases={k: 2 + k for k in range(2 * n)},
        compiler_params=pltpu.CompilerParams(has_side_effects=EFFECT),
    )(*[_in_hbm(a) for a in parts], *[_in_hbm(a) for a in lands])
    return outs[0], outs[1], outs[2:2 + n], outs[2 + n:2 + 2 * n], outs[-1]


def _exchange_wait(send, recv, parts, lands, after, name):
    n = len(parts)

    def body(*refs):
        ins, lz = refs[:n], refs[n:2 * n]
        send_r, recv_r = refs[2 * n], refs[2 * n + 1]
        for cp in _exchange_copies(ins, lz, send_r, recv_r):
            cp.wait_send()
            cp.wait_recv()

    hbm = [pltpu.HBM(a.shape, a.dtype) for a in list(parts) + list(lands)]
    outs = pl.pallas_call(
        body, name=name,
        in_specs=[HBM] * (2 * n) + [SEM, SEM, ANY],
        out_specs=[HBM] * (2 * n), out_shape=hbm,
        input_output_aliases={k: k for k in range(2 * n)},
        compiler_params=pltpu.CompilerParams(has_side_effects=EFFECT),
    )(*parts, *lands, send, recv, after)
    return outs[:n], outs[n:]


def _swap_halves_out(grads, name):
    n = len(grads)
    out_shapes = [jax.ShapeDtypeStruct((g.shape[0], g.shape[1] // 2, g.shape[2]), g.dtype) for g in grads]

    def body(*refs):
        ins, outs = refs[:n], refs[n:2 * n]
        send, recv = refs[2 * n:]
        x, y, c = _here()
        cps = []
        for k in range(n):
            rh = ins[k].shape[1] // 2
            cp = pltpu.make_async_remote_copy(
                src_ref=ins[k].at[:, pl.ds((1 - c) * rh, rh), :], dst_ref=outs[k],
                send_sem=send.at[k], recv_sem=recv.at[k], device_id=(x, y, 1 - c), device_id_type=MESH)
            cp.start()
            cps.append(cp)
        for cp in cps:
            cp.wait()

    return pl.pallas_call(
        body, in_specs=[ANY] * n, out_specs=[ANY] * n, out_shape=out_shapes,
        scratch_shapes=[pltpu.SemaphoreType.DMA((n,)), pltpu.SemaphoreType.DMA((n,))],
        name=name)(*grads)


def _swap_copies(grad_refs, land_refs, send, recv):
    x, y, c = _here()
    cps = []
    for k in range(len(grad_refs)):
        rh = grad_refs[k].shape[1] // 2
        cps.append(pltpu.make_async_remote_copy(
            src_ref=grad_refs[k].at[:, pl.ds((1 - c) * rh, rh), :], dst_ref=land_refs[k],
            send_sem=send.at[k], recv_sem=recv.at[k], device_id=(x, y, 1 - c), device_id_type=MESH))
    return cps


def _swap_start(grads, name):
    n = len(grads)
    lands = [lax.empty((g.shape[0], g.shape[1] // 2, g.shape[2]), g.dtype) for g in grads]

    def body(*refs):
        ins, lz = refs[:n], refs[n:2 * n]
        send, recv = refs[2 * n], refs[2 * n + 1]
        token = refs[-1]
        for cp in _swap_copies(ins, lz, send, recv):
            cp.start()
        token[...] = jnp.zeros_like(token)

    hbm = [pltpu.HBM(a.shape, a.dtype) for a in list(grads) + lands]
    outs = pl.pallas_call(
        body, name=name,
        in_specs=[HBM] * (2 * n),
        out_specs=[SEM, SEM] + [HBM] * (2 * n) + [pl.BlockSpec(memory_space=pltpu.VMEM)],
        out_shape=[pltpu.SemaphoreType.DMA((n,)), pltpu.SemaphoreType.DMA((n,))] + hbm
        + [jax.ShapeDtypeStruct((SUBLANES, LANES), F32)],
        input_output_aliases={k: 2 + k for k in range(2 * n)},
        compiler_params=pltpu.CompilerParams(has_side_effects=EFFECT),
    )(*[_in_hbm(a) for a in grads], *[_in_hbm(a) for a in lands])
    return outs[0], outs[1], outs[2:2 + n], outs[2 + n:2 + 2 * n], outs[-1]


def _swap_wait(send, recv, grads, lands, after, name):
    n = len(grads)

    def body(*refs):
        ins, lz = refs[:n], refs[n:2 * n]
        send_r, recv_r = refs[2 * n], refs[2 * n + 1]
        for cp in _swap_copies(ins, lz, send_r, recv_r):
            cp.wait_send()
            cp.wait_recv()

    hbm = [pltpu.HBM(a.shape, a.dtype) for a in list(grads) + list(lands)]
    outs = pl.pallas_call(
        body, name=name,
        in_specs=[HBM] * (2 * n) + [SEM, SEM, ANY],
        out_specs=[HBM] * (2 * n), out_shape=hbm,
        input_output_aliases={k: k for k in range(2 * n)},
        compiler_params=pltpu.CompilerParams(has_side_effects=EFFECT),
    )(*grads, *lands, send, recv, after)
    return outs[:n], outs[n:]


def _add_cast(g, other, cidx, name):
    ns, R, Cc = g.shape
    rh = R // 2
    tr = _tile(rh, max(16, (1 << 19) // Cc), 16)
    nb = rh // tr

    def body(c_ref, g_ref, o_ref, s_ref):
        s_ref[...] = (g_ref[...] + o_ref[...]).astype(BF16)

    return pl.pallas_call(
        body,
        grid_spec=pltpu.PrefetchScalarGridSpec(
            num_scalar_prefetch=1, grid=(ns, nb),
            in_specs=[pl.BlockSpec((None, tr, Cc), lambda k, i, c: (k, c[0] * nb + i, 0)),
                      pl.BlockSpec((None, tr, Cc), lambda k, i, c: (k, i, 0))],
            out_specs=pl.BlockSpec((None, tr, Cc), lambda k, i, c: (k, i, 0))),
        out_shape=jax.ShapeDtypeStruct((ns, rh, Cc), BF16),
        compiler_params=_params("parallel", "parallel"), name=name)(cidx, g, other)


def _sum_slots(part, got, idx, name):
    ns, rh, Cc = got.shape
    tr = _tile(rh, max(16, (1 << 18) // Cc), 16)
    nb = rh // tr

    def body(i_ref, p_ref, b_ref, o_ref):
        acc = p_ref[...].astype(F32)
        for m in range(ns):
            acc = acc + b_ref[m].astype(F32)
        o_ref[...] = acc

    return pl.pallas_call(
        body,
        grid_spec=pltpu.PrefetchScalarGridSpec(
            num_scalar_prefetch=1, grid=(nb,),
            in_specs=[pl.BlockSpec((None, tr, Cc), lambda i, s: (s[1], i, 0)),
                      pl.BlockSpec((ns, tr, Cc), lambda i, s: (0, i, 0))],
            out_specs=pl.BlockSpec((tr, Cc), lambda i, s: (s[0] * nb + i, 0))),
        out_shape=jax.ShapeDtypeStruct((2 * rh, Cc), F32),
        compiler_params=_params("parallel"), name=name)(idx, part, got)


def _share_halves(blocks, name):
    n = len(blocks)

    def body(*refs):
        ins, outs = refs[:n], refs[n:2 * n]
        send, recv = refs[2 * n:]
        x, y, c = _here()
        cps = []
        for k in range(n):
            rh = outs[k].shape[0] // 2
            mine = outs[k].at[pl.ds(c * rh, rh), :]
            cp = pltpu.make_async_remote_copy(
                src_ref=mine, dst_ref=mine, send_sem=send.at[k], recv_sem=recv.at[k],
                device_id=(x, y, 1 - c), device_id_type=MESH)
            cp.start()
            cps.append(cp)
        for cp in cps:
            cp.wait()

    return pl.pallas_call(
        body, in_specs=[ANY] * n, out_specs=[ANY] * n,
        out_shape=[jax.ShapeDtypeStruct(b.shape, b.dtype) for b in blocks],
        input_output_aliases={k: k for k in range(n)},
        scratch_shapes=[pltpu.SemaphoreType.DMA((n,)), pltpu.SemaphoreType.DMA((n,))],
        name=name)(*blocks)


def _small_copies(p_ref, slot_ref, send, recv):
    x, y, c = _here()
    mine = slot_ref.at[4 * x + 2 * y + c]
    cps = []
    for m in range(1, N_DEV):
        peer = (x ^ (m >> 2), y ^ ((m >> 1) & 1), c ^ (m & 1))
        cps.append(pltpu.make_async_remote_copy(
            src_ref=p_ref, dst_ref=mine, send_sem=send.at[m - 1], recv_sem=recv.at[m - 1],
            device_id=peer, device_id_type=MESH))
    return cps


def _small_start(packed):
    slots = lax.empty((N_DEV,) + packed.shape, packed.dtype)

    def body(p_ref, s_ref, send, recv, p_thru, s_thru, token):
        for cp in _small_copies(p_ref, s_ref, send, recv):
            cp.start()
        token[...] = jnp.zeros_like(token)

    return pl.pallas_call(
        body, name="small_start",
        in_specs=[HBM, HBM],
        out_specs=[SEM, SEM, HBM, HBM, pl.BlockSpec(memory_space=pltpu.VMEM)],
        out_shape=[pltpu.SemaphoreType.DMA((N_DEV - 1,)), pltpu.SemaphoreType.DMA((N_DEV - 1,)),
                   pltpu.HBM(packed.shape, packed.dtype), pltpu.HBM(slots.shape, slots.dtype),
                   jax.ShapeDtypeStruct((SUBLANES, LANES), F32)],
        input_output_aliases={0: 2, 1: 3},
        compiler_params=pltpu.CompilerParams(has_side_effects=EFFECT),
    )(_in_hbm(packed), _in_hbm(slots))


def _small_wait(send, recv, packed, slots, after):
    def body(p_ref, s_ref, send_r, recv_r, after_ref, p_out, s_out):
        for cp in _small_copies(p_ref, s_ref, send_r, recv_r):
            cp.wait_send()
            cp.wait_recv()

    return pl.pallas_call(
        body, name="small_wait",
        in_specs=[HBM, HBM, SEM, SEM, ANY], out_specs=[HBM, HBM],
        out_shape=[pltpu.HBM(packed.shape, packed.dtype), pltpu.HBM(slots.shape, slots.dtype)],
        input_output_aliases={0: 0, 1: 1},
        compiler_params=pltpu.CompilerParams(has_side_effects=EFFECT),
    )(packed, slots, send, recv, after)


def _sum_devices(packed, slots, me):
    n, R, _ = slots.shape
    tr = _tile(R, 1024)

    def body(m_ref, p_ref, s_ref, o_ref):
        own = p_ref[...]
        acc = None
        for d in range(n):
            term = jnp.where(m_ref[0] == d, own, s_ref[d])
            acc = term if acc is None else acc + term
        o_ref[...] = acc

    return pl.pallas_call(
        body,
        grid_spec=pltpu.PrefetchScalarGridSpec(
            num_scalar_prefetch=1, grid=(R // tr,),
            in_specs=[pl.BlockSpec((tr, LANES), lambda i, m: (i, 0)),
                      pl.BlockSpec((n, tr, LANES), lambda i, m: (0, i, 0))],
            out_specs=pl.BlockSpec((tr, LANES), lambda i, m: (i, 0))),
        out_shape=jax.ShapeDtypeStruct((R, LANES), F32),
        compiler_params=_params("parallel"), name="sum_devices")(me, packed, slots)


def _pack(arrs):
    rows, parts = [], []
    for a in arrs:
        flat = a.reshape(-1)
        r = -(-flat.shape[0] // (SUBLANES * LANES)) * SUBLANES
        parts.append(jnp.pad(flat, (0, r * LANES - flat.shape[0])).reshape(r, LANES))
        rows.append(r)
    return jnp.concatenate(parts, axis=0), rows


def _unpack(packed, rows, shapes):
    out, r0 = [], 0
    for r, shp in zip(rows, shapes):
        size = math.prod(shp)
        out.append(packed[r0:r0 + r].reshape(-1)[:size].reshape(shp))
        r0 += r
    return out


def _block_diag(w, per):
    H, dh, _ = w.shape
    w4 = w.reshape(H // per, per, dh, dh)
    eye = jnp.eye(per, dtype=w.dtype)
    return (w4[:, :, :, None, :] * eye[None, :, None, :, None]).reshape(H // per, per * dh, per * dh)


def _block_diag_take(d, per):
    n, s, _ = d.shape
    dh = s // per
    d5 = d.reshape(n, per, dh, per, dh)
    return jnp.stack([d5[:, h, :, h, :] for h in range(per)], axis=1).reshape(n * per, dh, dh)


def kernel(x, ffn1_norm, ffn1_w_gate, ffn1_w_up, ffn1_w_down, mix_norm, w_in, conv_dw, conv_dw_bias, conv_ln_g, conv_ln_b, lru_conv_w, lru_conv_b, lru_w_a, lru_b_a, lru_w_x, lru_b_x, lru_lambda, w_out, ffn2_norm, ffn2_w_gate, ffn2_w_up, ffn2_w_down, final_norm, loss_target, m_ffn1_norm, m_ffn1_w_gate, m_ffn1_w_up, m_ffn1_w_down, m_mix_norm, m_w_in, m_conv_dw, m_conv_dw_bias, m_conv_ln_g, m_conv_ln_b, m_lru_conv_w, m_lru_conv_b, m_lru_w_a, m_lru_b_a, m_lru_w_x, m_lru_b_x, m_lru_lambda, m_w_out, m_ffn2_norm, m_ffn2_w_gate, m_ffn2_w_up, m_ffn2_w_down, m_final_norm, v_ffn1_norm, v_ffn1_w_gate, v_ffn1_w_up, v_ffn1_w_down, v_mix_norm, v_w_in, v_conv_dw, v_conv_dw_bias, v_conv_ln_g, v_conv_ln_b, v_lru_conv_w, v_lru_conv_b, v_lru_w_a, v_lru_b_a, v_lru_w_x, v_lru_b_x, v_lru_lambda, v_w_out, v_ffn2_norm, v_ffn2_w_gate, v_ffn2_w_up, v_ffn2_w_down, v_final_norm):
    names = ['ffn1_norm', 'ffn1_w_gate', 'ffn1_w_up', 'ffn1_w_down', 'mix_norm', 'w_in', 'conv_dw', 'conv_dw_bias',
             'conv_ln_g', 'conv_ln_b', 'lru_conv_w', 'lru_conv_b', 'lru_w_a', 'lru_b_a', 'lru_w_x', 'lru_b_x',
             'lru_lambda', 'w_out', 'ffn2_norm', 'ffn2_w_gate', 'ffn2_w_up', 'ffn2_w_down', 'final_norm']
    env = dict(locals())
    W = {n: env[n] for n in names}
    M = {n: env['m_' + n] for n in names}
    V = {n: env['v_' + n] for n in names}

    xi, yi, ci = _here()
    chip = 2 * xi + yi
    cidx = ci.astype(jnp.int32).reshape(1)
    T, D = x.shape[-2], x.shape[-1]
    xs = x.reshape(T, D)
    tgt = loss_target.reshape(T, D)
    K, Cs = conv_dw.shape
    C = conv_dw_bias.shape[0]
    Wl = lru_conv_b.shape[0]
    K4 = lru_conv_w.shape[0]
    heads, dh, _ = lru_w_a.shape
    per = LANES // dh

    def row(v):
        return v.reshape(1, -1)

    tform = ('ffn1_w_gate', 'ffn1_w_up', 'ffn2_w_gate', 'ffn2_w_up')
    for n in tform:
        W[n], M[n], V[n] = W[n].T, M[n].T, V[n].T
    kp = -(-K // SUBLANES) * SUBLANES
    taps = jnp.concatenate([conv_dw, jnp.zeros((kp - K, Cs), F32), lru_conv_w,
                            jnp.zeros((2 * SUBLANES - K4, Cs), F32)], axis=0)
    idx = jnp.stack([ci, chip]).astype(jnp.int32)
    (wff1,) = _gather_weights([_place_cast([W['ffn1_w_gate'], W['ffn1_w_up'], ffn1_w_down], idx, BF16, "place_ffn1")])
    mixl = [_place_cast([w_in], idx, BF16, "place_w_in"), _place_cast([w_out], idx, BF16, "place_w_out"),
            _place_cast([taps], idx, F32, "place_taps")]
    msend, mrecv, mixl, mtok = _gather_start(mixl, wff1, "gather_mix_start")
    ff2l = _place_cast([W['ffn2_w_gate'], W['ffn2_w_up'], ffn2_w_down], idx, BF16, "place_ffn2")
    fsend, frecv, ff2l, ftok = _gather_start([ff2l], mtok, "gather_ffn2_start")
    wa_bd = _block_diag(lru_w_a, per).astype(BF16)
    wx_bd = _block_diag(lru_w_x, per).astype(BF16)

    x1, a1, b1 = _ffn_fwd(xs, row(ffn1_norm) + ftok[0:1, 0:1], wff1, "ffn1_fwd")
    win, wout, taps = _gather_wait(msend, mrecv, mixl, x1, "gather_mix_wait")
    win, wout, taps = win[0], wout.reshape(-1, D), taps[0]
    conv_w_full = taps[:, :K].transpose(1, 0, 2).reshape(K, N_CHIPS * Cs)
    lru_w4_full = taps[:, kp:kp + K4].transpose(1, 0, 2).reshape(K4, N_CHIPS * Cs)
    z = _mix_in_fwd(x1, row(mix_norm), win)
    u, u1 = _conv_fwd(z, conv_w_full, row(conv_dw_bias), row(conv_ln_g), row(conv_ln_b))
    yr, hs = _lru_fwd(z, 2 * C, lru_w4_full, row(lru_conv_b), wa_bd, row(lru_b_a), wx_bd, row(lru_b_x),
                      row(lru_lambda))
    x2 = _mix_out_fwd(x1, u, yr, wout)
    (wff2,) = _gather_wait(fsend, frecv, ff2l, x2, "gather_ffn2_wait")
    dx3, a2, b2, loss_blk, d_final = _ffn_fwd(x2, row(ffn2_norm), wff2, "ffn2_fwd", head=(row(final_norm), tgt))

    dx2, da2, db2, p2, hb2, dyh2, d_ffn2n = _ffn_bwd_tok(dx3, x2, row(ffn2_norm), a2, b2, wff2, "ffn2_bwd")
    dwg2, dwu2, dwd2 = _ffn_wgrad([([da2, db2], hb2), ([p2], dyh2)], ftok, "ffn2_wgrad")
    wsend, wrecv, f2g, f2o, wtok = _swap_start([dwg2, dwu2, dwd2], "swap_ffn2_start")
    dcat, dwout = _mix_out_bwd(dx2, u, yr, wout)
    dzc, cst = _conv_bwd(dcat, u1, z, conv_w_full, row(conv_ln_g) + wtok[0:1, 0:1], row(conv_ln_b))
    dzx, dzg, lst, dwa_bd, dwx_bd = _lru_bwd(dcat, C, hs, z, 2 * C, lru_w4_full, row(lru_conv_b), wa_bd,
                                              row(lru_b_a), wx_bd, row(lru_b_x), row(lru_lambda))
    dx1, dwin, d_mixn = _mix_in_bwd(dzc, dzx, dzg, x1, dx2, row(mix_norm), win)

    early_names = ['w_in', 'w_out', 'ffn2_w_gate', 'ffn2_w_up', 'ffn2_w_down']
    mixg = [dwin, dwout.reshape(N_CHIPS, -1, D)]
    mixo = _swap_halves_out(mixg, "swap_halves_mix")
    f2g, f2o = _swap_wait(wsend, wrecv, f2g, f2o, dwin, "swap_ffn2_wait")
    e_parts = [_add_cast(g, o, cidx, "add_cast_" + n)
               for g, o, n in zip(mixg + list(f2g), list(mixo) + list(f2o), early_names)]
    esend, erecv, e_parts, e_lands, etok = _exchange_start(e_parts, "exchange_early_start")

    dx0, da1, db1, p1, hb1, dyh1, d_ffn1n = _ffn_bwd_tok(dx1, xs, row(ffn1_norm) + etok[0:1, 0:1], a1, b1, wff1,
                                                         "ffn1_bwd")

    small_names = ['ffn1_norm', 'mix_norm', 'conv_dw', 'conv_dw_bias', 'conv_ln_g', 'conv_ln_b', 'lru_conv_w',
                   'lru_conv_b', 'lru_w_a', 'lru_b_a', 'lru_w_x', 'lru_b_x', 'lru_lambda', 'ffn2_norm',
                   'final_norm']
    small = {
        'ffn1_norm': d_ffn1n, 'mix_norm': d_mixn, 'conv_dw': cst[:K], 'conv_dw_bias': cst[K + 1],
        'conv_ln_g': cst[K + 2], 'conv_ln_b': cst[K + 3], 'lru_conv_w': lst[:K4], 'lru_conv_b': lst[K4],
        'lru_w_a': _block_diag_take(dwa_bd, per), 'lru_b_a': lst[K4 + 1],
        'lru_w_x': _block_diag_take(dwx_bd, per), 'lru_b_x': lst[K4 + 2], 'lru_lambda': lst[K4 + 3],
        'ffn2_norm': d_ffn2n, 'final_norm': d_final,
    }
    packed, rows = _pack([small[n] for n in small_names] + [loss_blk[0:1, 0:1]])
    ssend, srecv, packed, sslots, stok = _small_start(packed)

    gu_names, d_names = ['ffn1_w_gate', 'ffn1_w_up'], ['ffn1_w_down']
    gu = _ffn_wgrad([([da1, db1], hb1)], stok, "ffn1_wgrad_gu")
    gu_parts = [_add_cast(g, o, cidx, "add_cast_" + n)
                for g, o, n in zip(gu, _swap_halves_out(gu, "swap_halves_gu"), gu_names)]
    gsend, grecv, gu_parts, gu_lands, gtok = _exchange_start(gu_parts, "exchange_gu_start")
    dn = _ffn_wgrad([([p1], dyh1)], gtok, "ffn1_wgrad_d")
    dwd1 = dn[0]
    d_parts = [_add_cast(g, o, cidx, "add_cast_" + n)
               for g, o, n in zip(dn, _swap_halves_out(dn, "swap_halves_d"), d_names)]
    dsend, drecv, d_parts, d_lands, ltok = _exchange_start(d_parts, "exchange_d_start")
    e_parts, e_slots = _exchange_wait(esend, erecv, e_parts, e_lands, ltok, "exchange_early_wait")
    delta, new_m, new_v = {}, {}, {}

    def finish(group, parts, slots, tag):
        halves = [_sum_slots(p, b, idx, "sum_slots_" + n) for p, b, n in zip(parts, slots, group)]
        for n, g in zip(group, _share_halves(halves, "share_halves_" + tag)):
            G[n] = g
            delta[n], new_m[n], new_v[n] = _adamw(W[n], g, M[n], V[n], "adamw_" + n)

    G = {}
    finish(early_names, e_parts, e_slots, "early")

    full_shapes = [(K, C) if n == 'conv_dw' else (K4, Wl) if n == 'lru_conv_w' else W[n].shape for n in small_names]
    packed, sslots = _small_wait(ssend, srecv, packed, sslots, dwd1)
    summed = _sum_devices(packed, sslots, (4 * xi + 2 * yi + ci).astype(jnp.int32).reshape(1))
    *small_sums, loss_sum = _unpack(summed, rows, full_shapes + [(1, 1)])
    for n, gsum in zip(small_names, small_sums):
        if n == 'conv_dw':
            gsum = lax.dynamic_slice_in_dim(gsum, chip * Cs, Cs, axis=1)
        elif n == 'lru_conv_w':
            gsum = lax.dynamic_slice_in_dim(gsum, chip * lru_conv_w.shape[1], lru_conv_w.shape[1], axis=1)
        G[n] = gsum

    pw, prow = _pack([W[n] for n in small_names])
    pg, _ = _pack([G[n] for n in small_names])
    pm, _ = _pack([M[n] for n in small_names])
    pv, _ = _pack([V[n] for n in small_names])
    sd, sm, sv = _adamw(pw, pg, pm, pv, "adamw_small")
    shapes = [W[n].shape for n in small_names]
    for n, a, b, c_ in zip(small_names, _unpack(sd, prow, shapes), _unpack(sm, prow, shapes),
                           _unpack(sv, prow, shapes)):
        delta[n], new_m[n], new_v[n] = a, b, c_

    done = sd[0:SUBLANES] + delta[early_names[-1]][0:SUBLANES, 0:LANES]
    gu_parts, gu_slots = _exchange_wait(gsend, grecv, gu_parts, gu_lands, done, "exchange_gu_wait")
    d_parts, d_slots = _exchange_wait(dsend, drecv, d_parts, d_lands, gu_slots[0], "exchange_d_wait")
    finish(gu_names + d_names, list(gu_parts) + list(d_parts), list(gu_slots) + list(d_slots), "last")

    loss = loss_sum[0, 0]
    grad_x = dx0.reshape(x.shape)
    for n in tform:
        G[n], delta[n], new_m[n], new_v[n] = G[n].T, delta[n].T, new_m[n].T, new_v[n].T
    return (loss, grad_x, *[G[n] for n in names], *[delta[n] for n in names],
            *[new_m[n] for n in names], *[new_v[n] for n in names])
```

```python
import functools
import math

import jax
import jax.numpy as jnp
from jax import lax
from jax.experimental import pallas as pl
from jax.experimental.pallas import tpu as pltpu

F32 = jnp.float32
BF16 = jnp.bfloat16
MESH = pl.DeviceIdType.MESH

RMS_EPS = 1e-6
LN_EPS = 1e-5
LRU_C = 8.0
FFN_RES_SCALE = 0.5
ADAM_LR = 0.001
ADAM_B1 = 0.9
ADAM_B2 = 0.999
ADAM_EPS = 1e-08
ADAM_WD = 0.01
ADAM_STEP = 10

LANES = 128
SUBLANES = 8
CONV_HALO = 32
LRU_HALO = 8
ROW_CHUNK = 64
VMEM_LIMIT = 56 * 1024 * 1024
N_CHIPS = 4
N_DEV = 8
TOK_TILE = 1024
BWD_TILE = 512
FFN_BWD_TILE = 512
BWD_ROWS = 32
FFN_BWD_CHAIN = 256
CONV_TILE = 512
LRU_TILE = 2048
LRU_GROUPS = 8


def _dot(a, b):
    return jnp.dot(a, b, preferred_element_type=F32)


def _dot_nt(a, b):
    return lax.dot_general(a, b, (((1,), (1,)), ((), ())), preferred_element_type=F32)


def _dot_tn(a, b):
    return lax.dot_general(a, b, (((0,), (0,)), ((), ())), preferred_element_type=F32)


def _tile(n, pref, mult=SUBLANES):
    for t in range(min(pref, n), 0, -1):
        if n % t == 0 and t % mult == 0:
            return t
    return n


def _params(*sem):
    return pltpu.CompilerParams(dimension_semantics=sem, vmem_limit_bytes=VMEM_LIMIT)


def _rms_stats(x):
    r = lax.rsqrt(jnp.mean(x * x, axis=-1, keepdims=True) + RMS_EPS)
    return x * r, r


def _rms_bwd(dh, xh, r, g):
    dxh = dh * g
    return r * (dxh - xh * jnp.mean(dxh * xh, axis=-1, keepdims=True))


def _colsum(v):
    return jnp.sum(v, axis=0, keepdims=True)


def _ffn_fwd(x, g, wff, name, head=None):
    T, D = x.shape
    ns, fs = wff.shape[1], wff.shape[2]
    tm = _tile(T, TOK_TILE)
    mc = _tile(tm, FFN_BWD_CHAIN, 16)
    rc = _tile(tm, FFN_BWD_CHAIN)

    def body(*refs):
        x_ref, g_ref, wg_ref, wu_ref, wd_ref = refs[:5]
        if head is None:
            y_ref, a_ref, b_ref, hb_ref, acc_ref = refs[5:]
        else:
            gf_ref, t_ref, y_ref, a_ref, b_ref, loss_ref, dgf_ref, hb_ref, acc_ref = refs[5:]
        j = pl.program_id(1)

        @pl.when(j == 0)
        def _():
            xh, _ = _rms_stats(x_ref[...])
            hb_ref[...] = (xh * g_ref[...]).astype(BF16)
            acc_ref[...] = jnp.zeros_like(acc_ref)

        if head is not None:
            @pl.when((pl.program_id(0) == 0) & (j == 0))
            def _():
                loss_ref[...] = jnp.zeros_like(loss_ref)
                dgf_ref[...] = jnp.zeros_like(dgf_ref)

        for q0 in range(0, tm, mc):
            blk = pl.ds(q0, mc)
            hb = hb_ref[blk, :]
            a = _dot_nt(hb, wg_ref[...])
            b = _dot_nt(hb, wu_ref[...])
            a_ref[blk, :] = a.astype(BF16)
            b_ref[blk, :] = b.astype(BF16)
            p = (a * jax.nn.sigmoid(a) * b).astype(BF16)
            acc_ref[blk, :] += _dot(p, wd_ref[...])

        @pl.when(j == ns - 1)
        def _():
            if head is None:
                y_ref[...] = x_ref[...] + FFN_RES_SCALE * acc_ref[...]
                return
            gv = gf_ref[...]
            loss = jnp.zeros((), F32)
            dg = jnp.zeros((1, D), F32)
            for r0 in range(0, tm, rc):
                rows = pl.ds(r0, rc)
                xh, r = _rms_stats(x_ref[rows, :] + FFN_RES_SCALE * acc_ref[rows, :])
                e = xh * gv - t_ref[rows, :]
                loss = loss + 0.5 * jnp.sum(jnp.mean(e * e, axis=-1, keepdims=True))
                dy = e * (1.0 / D)
                dg = dg + _colsum(dy * xh)
                y_ref[rows, :] = _rms_bwd(dy, xh, r, gv)
            loss_ref[...] += loss
            dgf_ref[...] += dg

    def wspec(n):
        return pl.BlockSpec((None, None, fs, D), lambda i, j: (n, j, 0, 0))

    tok = pl.BlockSpec((tm, D), lambda i, j: (i, 0))
    vec = pl.BlockSpec((1, D), lambda i, j: (0, 0))
    mid = pl.BlockSpec((None, tm, fs), lambda i, j: (j, i, 0))
    in_specs = [tok, vec, wspec(0), wspec(1), wspec(2)]
    out_specs = [tok, mid, mid]
    out_shape = [jax.ShapeDtypeStruct((T, D), F32), jax.ShapeDtypeStruct((ns, T, fs), BF16),
                 jax.ShapeDtypeStruct((ns, T, fs), BF16)]
    args = [x, g, wff, wff, wff]
    if head is not None:
        in_specs += [vec, tok]
        out_specs += [pl.BlockSpec((SUBLANES, LANES), lambda i, j: (0, 0)), vec]
        out_shape += [jax.ShapeDtypeStruct((SUBLANES, LANES), F32), jax.ShapeDtypeStruct((1, D), F32)]
        args += list(head)
    return pl.pallas_call(
        body, grid=(T // tm, ns), in_specs=in_specs, out_specs=out_specs, out_shape=out_shape,
        scratch_shapes=[pltpu.VMEM((tm, D), BF16), pltpu.VMEM((tm, D), F32)],
        compiler_params=_params("arbitrary", "arbitrary"), name=name)(*args)


def _ffn_bwd_tok(dy, x, g, a, b, wff, name):
    T, D = x.shape
    ns, fs = wff.shape[1], wff.shape[2]
    tm = _tile(T, FFN_BWD_TILE)
    rc = _tile(tm, BWD_ROWS)
    mc = _tile(tm, FFN_BWD_CHAIN, rc)

    def body(dy_ref, x_ref, g_ref, a_ref, b_ref, wg_ref, wu_ref, wd0_ref, wdn_ref,
             dx_ref, da_ref, db_ref, p_ref, hb_ref, dyh_ref, dg_ref, dh_ref, dp_ref):
        i, j = pl.program_id(0), pl.program_id(1)
        cur = dp_ref.at[j % 2]
        nxt = dp_ref.at[(j + 1) % 2]

        @pl.when((i == 0) & (j == 0))
        def _():
            dg_ref[...] = jnp.zeros_like(dg_ref)

        @pl.when(j == 0)
        def _():
            for r0 in range(0, tm, rc):
                rows = pl.ds(r0, rc)
                xh, _ = _rms_stats(x_ref[rows, :])
                hb_ref[rows, :] = (xh * g_ref[...]).astype(BF16)
                dyh_ref[rows, :] = (FFN_RES_SCALE * dy_ref[rows, :]).astype(BF16)
            dh_ref[...] = jnp.zeros_like(dh_ref)
            cur[...] = _dot_nt(dyh_ref[...], wd0_ref[...])

        def chains(with_next):
            for q0 in range(0, tm, mc):
                blk = pl.ds(q0, mc)
                for r0 in range(q0, q0 + mc, rc):
                    rows = pl.ds(r0, rc)
                    av = a_ref[rows, :].astype(F32)
                    bv = b_ref[rows, :].astype(F32)
                    dp = cur[rows, :]
                    s = jax.nn.sigmoid(av)
                    sl = av * s
                    da_ref[rows, :] = (dp * bv * (s * (1.0 + av * (1.0 - s)))).astype(BF16)
                    db_ref[rows, :] = (dp * sl).astype(BF16)
                    p_ref[rows, :] = (sl * bv).astype(BF16)
                if with_next:
                    nxt[blk, :] = _dot_nt(dyh_ref[blk, :], wdn_ref[...])
                dh_ref[blk, :] += _dot(da_ref[blk, :], wg_ref[...]) + _dot(db_ref[blk, :], wu_ref[...])

        pl.when(j < ns - 1)(functools.partial(chains, True))
        pl.when(j == ns - 1)(functools.partial(chains, False))

        @pl.when(j == ns - 1)
        def _():
            gv = g_ref[...]
            dg = jnp.zeros((1, D), F32)
            for r0 in range(0, tm, rc):
                rows = pl.ds(r0, rc)
                xh, r = _rms_stats(x_ref[rows, :])
                dh = dh_ref[rows, :]
                dx_ref[rows, :] = dy_ref[rows, :] + _rms_bwd(dh, xh, r, gv)
                dg = dg + _colsum(dh * xh)
            dg_ref[...] += dg

    def wspec(n):
        return pl.BlockSpec((None, None, fs, D), lambda i, j: (n, j, 0, 0))

    tok = pl.BlockSpec((tm, D), lambda i, j: (i, 0))
    mid = pl.BlockSpec((None, tm, fs), lambda i, j: (j, i, 0))
    vec = pl.BlockSpec((1, D), lambda i, j: (0, 0))
    return pl.pallas_call(
        body, grid=(T // tm, ns),
        in_specs=[tok, tok, vec, mid, mid, wspec(0), wspec(1),
                  pl.BlockSpec((None, None, fs, D), lambda i, j: (2, 0, 0, 0)),
                  pl.BlockSpec((None, None, fs, D), lambda i, j: (2, jnp.minimum(j + 1, ns - 1), 0, 0))],
        out_specs=[tok, mid, mid, mid, tok, tok, vec],
        out_shape=[jax.ShapeDtypeStruct((T, D), F32),
                   jax.ShapeDtypeStruct((ns, T, fs), BF16), jax.ShapeDtypeStruct((ns, T, fs), BF16),
                   jax.ShapeDtypeStruct((ns, T, fs), BF16),
                   jax.ShapeDtypeStruct((T, D), BF16), jax.ShapeDtypeStruct((T, D), BF16),
                   jax.ShapeDtypeStruct((1, D), F32)],
        scratch_shapes=[pltpu.VMEM((tm, D), F32), pltpu.VMEM((2, tm, fs), F32)],
        compiler_params=_params("arbitrary", "arbitrary"), name=name)(dy, x, g, a, b, wff, wff, wff, wff)


def _ffn_wgrad(groups, after, name):
    flat = [(l, gi) for gi, (ls, _) in enumerate(groups) for l in ls]
    ng, n = len(groups), len(flat)
    T, D = groups[0][1].shape
    ns, _, fs = flat[0][0].shape
    tm = _tile(T, TOK_TILE)

    def body(*refs):
        rhs_refs, lhs_refs, out_refs = refs[:ng], refs[ng:ng + n], refs[ng + n + 1:]

        @pl.when(pl.program_id(1) == 0)
        def _():
            for o in out_refs:
                o[...] = jnp.zeros_like(o)

        rvs = [r[...] for r in rhs_refs]
        for l, o, (_, gi) in zip(lhs_refs, out_refs, flat):
            o[...] += _dot_tn(l[...], rvs[gi])

    tok = pl.BlockSpec((tm, D), lambda j, i: (i, 0))
    mid = pl.BlockSpec((None, tm, fs), lambda j, i: (j, i, 0))
    wsp = pl.BlockSpec((None, fs, D), lambda j, i: (j, 0, 0))
    sds = jax.ShapeDtypeStruct((ns, fs, D), F32)
    return pl.pallas_call(
        body, grid=(ns, T // tm),
        in_specs=[tok] * ng + [mid] * n + [pl.BlockSpec((SUBLANES, LANES), lambda j, i: (0, 0))],
        out_specs=[wsp] * n, out_shape=[sds] * n,
        compiler_params=_params("parallel", "arbitrary"), name=name)(
            *[r for _, r in groups], *[l for l, _ in flat], after)


def _mix_in_fwd(x, g, win):
    T, D = x.shape
    ns, ws = win.shape[0], win.shape[2]
    tm = _tile(T, TOK_TILE)

    def body(x_ref, g_ref, w_ref, z_ref):
        xh, _ = _rms_stats(x_ref[...])
        hb = (xh * g_ref[...]).astype(BF16)
        for j in range(ns):
            z_ref[:, pl.ds(j * ws, ws)] = _dot(hb, w_ref[j])

    return pl.pallas_call(
        body, grid=(T // tm,),
        in_specs=[pl.BlockSpec((tm, D), lambda i: (i, 0)), pl.BlockSpec((1, D), lambda i: (0, 0)),
                  pl.BlockSpec((ns, D, ws), lambda i: (0, 0, 0), pipeline_mode=pl.Buffered(1))],
        out_specs=pl.BlockSpec((tm, ns * ws), lambda i: (i, 0)),
        out_shape=jax.ShapeDtypeStruct((T, ns * ws), F32),
        compiler_params=_params("parallel"), name="mix_in_fwd")(x, g, win)


def _tap_sum(buf, w_ref, ntaps, first_row, r0, rows, flip):
    acc = None
    for k in range(ntaps):
        off = (ntaps - 1 - k) if flip else k
        t = buf[pl.ds(first_row + r0 + off, rows), :] * w_ref[pl.ds(k, 1), :]
        acc = t if acc is None else acc + t
    return acc


def _shift_copies(buf, sh, rows):
    for r in range(1, SUBLANES):
        sh[r - 1, pl.ds(0, rows), :] = buf[pl.ds(r, rows), :]


def _tap_rows(buf, sh, off, r0, rows):
    r = off % SUBLANES
    if r == 0:
        return buf[pl.ds(off + r0, rows), :]
    return sh[r - 1, pl.ds(off - r + r0, rows), :]


def _tap_sum_tiles(buf, sh, w_ref, ntaps, first_row, r0, rows, flip):
    acc = None
    for k in range(ntaps):
        off = first_row + ((ntaps - 1 - k) if flip else k)
        t = _tap_rows(buf, sh, off, r0, rows) * w_ref[pl.ds(k, 1), :]
        acc = t if acc is None else acc + t
    return acc


def _conv_fwd(z, w, bias, lng, lnb):
    T = z.shape[0]
    K, C = w.shape
    tm = _tile(T, CONV_TILE, ROW_CHUNK)
    rc = min(ROW_CHUNK, tm)
    srows = tm + CONV_HALO - SUBLANES

    def body(cv_ref, cg_ref, w_ref, b_ref, g_ref, bb_ref, u_ref, u1_ref, buf, sh):
        @pl.when(pl.program_id(0) == 0)
        def _():
            buf[pl.ds(0, CONV_HALO), :] = jnp.zeros((CONV_HALO, C), F32)

        buf[pl.ds(CONV_HALO, tm), :] = cv_ref[...] * jax.nn.sigmoid(cg_ref[...])
        _shift_copies(buf, sh, srows)
        for r0 in range(0, tm, rc):
            u1 = _tap_sum_tiles(buf, sh, w_ref, K, CONV_HALO - (K - 1), r0, rc, False) + b_ref[...]
            u1_ref[pl.ds(r0, rc), :] = u1
            xc = u1 - jnp.mean(u1, axis=-1, keepdims=True)
            xh = xc * lax.rsqrt(jnp.mean(xc * xc, axis=-1, keepdims=True) + LN_EPS)
            u2 = xh * g_ref[...] + bb_ref[...]
            u_ref[pl.ds(r0, rc), :] = (u2 * jax.nn.sigmoid(u2)).astype(BF16)
        buf[pl.ds(0, CONV_HALO), :] = buf[pl.ds(tm, CONV_HALO), :]

    vec = pl.BlockSpec((1, C), lambda i: (0, 0))
    return pl.pallas_call(
        body, grid=(T // tm,),
        in_specs=[pl.BlockSpec((tm, C), lambda i: (i, 0)), pl.BlockSpec((tm, C), lambda i: (i, 1)),
                  pl.BlockSpec((K, C), lambda i: (0, 0)), vec, vec, vec],
        out_specs=[pl.BlockSpec((tm, C), lambda i: (i, 0)), pl.BlockSpec((tm, C), lambda i: (i, 0))],
        out_shape=[jax.ShapeDtypeStruct((T, C), BF16), jax.ShapeDtypeStruct((T, C), F32)],
        scratch_shapes=[pltpu.VMEM((CONV_HALO + tm, C), F32), pltpu.VMEM((SUBLANES - 1, srows, C), F32)],
        compiler_params=_params("arbitrary"), name="conv_fwd")(z, z, w, bias, lng, lnb)


def _conv_bwd(dcat, u1, z, w, lng, lnb):
    T = z.shape[0]
    K, C = w.shape
    tm = _tile(T, CONV_TILE, ROW_CHUNK)
    rc = min(ROW_CHUNK, tm)
    nI = T // tm
    hb = tm // CONV_HALO
    srows = ((K + 4 + SUBLANES - 1) // SUBLANES) * SUBLANES
    shrows = tm + CONV_HALO - SUBLANES

    def body(du_ref, u1_ref, cv_ref, cg_ref, cvp_ref, cgp_ref, w_ref, g_ref, bb_ref,
             dz_ref, st_ref, u0buf, d1buf, ush, dsh):
        i = pl.program_id(0)
        ti = nI - 1 - i

        @pl.when(i == 0)
        def _():
            st_ref[...] = jnp.zeros_like(st_ref)
            d1buf[pl.ds(tm, CONV_HALO), :] = jnp.zeros((CONV_HALO, C), F32)

        prev = cvp_ref[...] * jax.nn.sigmoid(cgp_ref[...])
        u0buf[pl.ds(0, CONV_HALO), :] = jnp.where(ti == 0, 0.0, prev)
        u0buf[pl.ds(CONV_HALO, tm), :] = cv_ref[...] * jax.nn.sigmoid(cg_ref[...])

        gv = g_ref[...]
        dbias = jnp.zeros((1, C), F32)
        dgain = jnp.zeros((1, C), F32)
        dlnb = jnp.zeros((1, C), F32)
        for r0 in range(0, tm, rc):
            u1 = u1_ref[pl.ds(r0, rc), :]
            xc = u1 - jnp.mean(u1, axis=-1, keepdims=True)
            rstd = lax.rsqrt(jnp.mean(xc * xc, axis=-1, keepdims=True) + LN_EPS)
            xh = xc * rstd
            u2 = xh * gv + bb_ref[...]
            s = jax.nn.sigmoid(u2)
            du2 = du_ref[pl.ds(r0, rc), :] * (s * (1.0 + u2 * (1.0 - s)))
            dgain = dgain + _colsum(du2 * xh)
            dlnb = dlnb + _colsum(du2)
            dxh = du2 * gv
            du1 = rstd * (dxh - jnp.mean(dxh, axis=-1, keepdims=True)
                          - xh * jnp.mean(dxh * xh, axis=-1, keepdims=True))
            dbias = dbias + _colsum(du1)
            d1buf[pl.ds(r0, rc), :] = du1
        st_ref[pl.ds(K + 1, 1), :] += dbias
        st_ref[pl.ds(K + 2, 1), :] += dgain
        st_ref[pl.ds(K + 3, 1), :] += dlnb

        _shift_copies(u0buf, ush, shrows)
        _shift_copies(d1buf, dsh, shrows)
        for k in range(K):
            acc = jnp.zeros((SUBLANES, C), F32)
            for r0 in range(0, tm, rc):
                prod = d1buf[pl.ds(r0, rc), :] * _tap_rows(u0buf, ush, CONV_HALO - (K - 1) + k, r0, rc)
                acc = acc + jnp.sum(prod.reshape(rc // SUBLANES, SUBLANES, C), axis=0)
            st_ref[pl.ds(k, 1), :] += _colsum(acc)

        for r0 in range(0, tm, rc):
            du0 = _tap_sum_tiles(d1buf, dsh, w_ref, K, 0, r0, rc, True)
            cv = cv_ref[pl.ds(r0, rc), :]
            sg = jax.nn.sigmoid(cg_ref[pl.ds(r0, rc), :])
            dz_ref[pl.ds(r0, rc), pl.ds(0, C)] = (du0 * sg).astype(BF16)
            dz_ref[pl.ds(r0, rc), pl.ds(C, C)] = (du0 * cv * sg * (1.0 - sg)).astype(BF16)
        d1buf[pl.ds(tm, CONV_HALO), :] = d1buf[pl.ds(0, CONV_HALO), :]

    def rev(col):
        return lambda i: (nI - 1 - i, col)

    def rev_prev(col):
        return lambda i: (jnp.maximum((nI - 1 - i) * hb - 1, 0), col)

    vec = pl.BlockSpec((1, C), lambda i: (0, 0))
    return pl.pallas_call(
        body, grid=(nI,),
        in_specs=[pl.BlockSpec((tm, C), rev(0)), pl.BlockSpec((tm, C), rev(0)),
                  pl.BlockSpec((tm, C), rev(0)), pl.BlockSpec((tm, C), rev(1)),
                  pl.BlockSpec((CONV_HALO, C), rev_prev(0)), pl.BlockSpec((CONV_HALO, C), rev_prev(1)),
                  pl.BlockSpec((K, C), lambda i: (0, 0)), vec, vec],
        out_specs=[pl.BlockSpec((tm, 2 * C), rev(0)), pl.BlockSpec((srows, C), lambda i: (0, 0))],
        out_shape=[jax.ShapeDtypeStruct((T, 2 * C), BF16), jax.ShapeDtypeStruct((srows, C), F32)],
        scratch_shapes=[pltpu.VMEM((CONV_HALO + tm, C), F32), pltpu.VMEM((tm + CONV_HALO, C), F32),
                        pltpu.VMEM((SUBLANES - 1, shrows, C), F32), pltpu.VMEM((SUBLANES - 1, shrows, C), F32)],
        compiler_params=_params("arbitrary"), name="conv_bwd")(dcat, u1, z, z, z, z, w, lng, lnb)


def _softplus(v):
    return jnp.maximum(v, 0.0) + jnp.log(1.0 + jnp.exp(-jnp.abs(v)))


def _gelu(v):
    c = math.sqrt(2.0 / math.pi)
    t = jnp.tanh(c * (v + 0.044715 * v * v * v))
    gl = 0.5 * v * (1.0 + t)
    dgl = 0.5 * (1.0 + t) + 0.5 * v * (1.0 - t * t) * c * (1.0 + 3.0 * 0.044715 * v * v)
    return gl, dgl


def _lru_gates(xr, wa, ba, wx, bx, lam):
    xb = xr.astype(BF16)
    r = jax.nn.sigmoid(_dot(xb, wa) + ba)
    ig = jax.nn.sigmoid(_dot(xb, wx) + bx)
    sp = _softplus(-lam)
    log_a = -LRU_C * r * sp
    a = jnp.exp(log_a)
    y = 2.0 * log_a
    series = -(y * (1.0 + y * (0.5 + y * (1.0 / 6.0 + y * (1.0 / 24.0)))))
    mult = jnp.sqrt(jnp.where(y > -0.02, series, 1.0 - jnp.exp(y)))
    return a, mult, r, ig, sp


def _scan_tile(a_s, b_s, h_s, p_s, carry, seg, reverse):
    hl = [jnp.zeros((SUBLANES, LANES), F32)] * LRU_GROUPS
    pr = [jnp.ones((SUBLANES, LANES), F32)] * LRU_GROUPS
    for n in range(seg):
        for g in range(LRU_GROUPS):
            rows = pl.ds(g * SUBLANES * seg + ((seg - 1 - n) if reverse else n), SUBLANES, stride=seg)
            av = a_s[rows, :]
            hl[g] = av * hl[g] + b_s[rows, :]
            pr[g] = av * pr[g]
            h_s[rows, :] = hl[g]
            p_s[rows, :] = pr[g]
    nseg = SUBLANES * LRU_GROUPS
    cs = [None] * nseg
    c = carry
    for s in (range(nseg - 1, -1, -1) if reverse else range(nseg)):
        g, r = divmod(s, SUBLANES)
        cs[s] = c
        c = hl[g][r:r + 1, :] + pr[g][r:r + 1, :] * c
    return cs, c


def _lru_fwd(z, col0, w4, b4, wa, ba, wx, bx, lam):
    T = z.shape[0]
    K4, W = w4.shape
    nC = W // LANES
    tm = _tile(T, LRU_TILE, SUBLANES * SUBLANES * LRU_GROUPS)
    seg = tm // (SUBLANES * LRU_GROUPS)
    cx, cg = col0 // LANES, (col0 + W) // LANES

    def body(rx_ref, rg_ref, w4_ref, b4_ref, wa_ref, ba_ref, wx_ref, bx_ref, lam_ref,
             yr_ref, hs_ref, xbuf, a_s, b_s, h_s, p_s, hc):
        @pl.when(pl.program_id(1) == 0)
        def _():
            xbuf[pl.ds(0, LRU_HALO), :] = jnp.zeros((LRU_HALO, LANES), F32)
            hc[...] = jnp.zeros_like(hc)

        xbuf[pl.ds(LRU_HALO, tm), :] = rx_ref[...]
        xr = _tap_sum(xbuf, w4_ref, K4, LRU_HALO - (K4 - 1), 0, tm, False) + b4_ref[...]
        a, mult, _, ig, _ = _lru_gates(xr, wa_ref[...], ba_ref[...], wx_ref[...], bx_ref[...], lam_ref[...])
        a_s[...] = a
        b_s[...] = mult * ig * xr
        cs, cout = _scan_tile(a_s, b_s, h_s, p_s, hc[pl.ds(0, 1), :], seg, False)
        hc[pl.ds(0, 1), :] = cout
        for s in range(SUBLANES * LRU_GROUPS):
            rows = pl.ds(s * seg, seg)
            h = h_s[rows, :] + p_s[rows, :] * cs[s]
            hs_ref[rows, :] = h
            gl, _ = _gelu(rg_ref[rows, :])
            yr_ref[rows, :] = (h * gl).astype(BF16)
        xbuf[pl.ds(0, LRU_HALO), :] = xbuf[pl.ds(tm, LRU_HALO), :]

    vec = pl.BlockSpec((1, LANES), lambda c, i: (0, c))
    mat = pl.BlockSpec((None, LANES, LANES), lambda c, i: (c, 0, 0))
    return pl.pallas_call(
        body, grid=(nC, T // tm),
        in_specs=[pl.BlockSpec((tm, LANES), lambda c, i: (i, cx + c)),
                  pl.BlockSpec((tm, LANES), lambda c, i: (i, cg + c)),
                  pl.BlockSpec((K4, LANES), lambda c, i: (0, c)), vec, mat, vec, mat, vec, vec],
        out_specs=[pl.BlockSpec((tm, LANES), lambda c, i: (i, c)), pl.BlockSpec((tm, LANES), lambda c, i: (i, c))],
        out_shape=[jax.ShapeDtypeStruct((T, W), BF16), jax.ShapeDtypeStruct((T, W), F32)],
        scratch_shapes=[pltpu.VMEM((LRU_HALO + tm, LANES), F32)] + [pltpu.VMEM((tm, LANES), F32)] * 4
        + [pltpu.VMEM((SUBLANES, LANES), F32)],
        compiler_params=_params("parallel", "arbitrary"), name="lru_fwd")(z, z, w4, b4, wa, ba, wx, bx, lam)


def _lru_bwd(dcat, dcol0, hs, z, col0, w4, b4, wa, ba, wx, bx, lam):
    T = z.shape[0]
    K4, W = w4.shape
    assert K4 + 4 == SUBLANES
    nC = W // LANES
    tm = _tile(T, LRU_TILE, SUBLANES * SUBLANES * LRU_GROUPS)
    seg = tm // (SUBLANES * LRU_GROUPS)
    nI = T // tm
    hb = tm // LRU_HALO
    cx, cg, cd = col0 // LANES, (col0 + W) // LANES, dcol0 // LANES

    def body(dyr_ref, hs_ref, hsp_ref, rx_ref, rxp_ref, rg_ref, w4_ref, b4_ref, wa_ref, ba_ref, wx_ref, bx_ref,
             lam_ref, dzx_ref, dzg_ref, st_ref, dwa_ref, dwx_ref, xbuf, hbuf, abuf, a_s, b_s, h_s, p_s, dbuf, gc, anc):
        i = pl.program_id(1)
        ti = nI - 1 - i

        @pl.when(i == 0)
        def _():
            st_ref[...] = jnp.zeros_like(st_ref)
            dwa_ref[...] = jnp.zeros_like(dwa_ref)
            dwx_ref[...] = jnp.zeros_like(dwx_ref)
            gc[...] = jnp.zeros_like(gc)
            anc[...] = jnp.zeros_like(anc)
            dbuf[pl.ds(tm, LRU_HALO), :] = jnp.zeros((LRU_HALO, LANES), F32)

        xbuf[pl.ds(0, LRU_HALO), :] = jnp.where(ti == 0, 0.0, rxp_ref[...])
        xbuf[pl.ds(LRU_HALO, tm), :] = rx_ref[...]
        hbuf[pl.ds(0, LRU_HALO), :] = jnp.where(ti == 0, 0.0, hsp_ref[...])
        hbuf[pl.ds(LRU_HALO, tm), :] = hs_ref[...]

        wa, wx = wa_ref[...], wx_ref[...]
        lam_v = lam_ref[...]
        xr = _tap_sum(xbuf, w4_ref, K4, LRU_HALO - (K4 - 1), 0, tm, False) + b4_ref[...]
        a, mult, r, ig, sp = _lru_gates(xr, wa, ba_ref[...], wx, bx_ref[...], lam_v)

        dyr = dyr_ref[...]
        gl, dgl = _gelu(rg_ref[...])
        dzg_ref[...] = (dyr * hs_ref[...] * dgl).astype(BF16)

        abuf[pl.ds(0, tm), :] = a
        abuf[pl.ds(tm, LRU_HALO), :] = anc[...]
        a_s[...] = abuf[pl.ds(1, tm), :]
        b_s[...] = dyr * gl
        cs, cout = _scan_tile(a_s, b_s, h_s, p_s, gc[pl.ds(0, 1), :], seg, True)
        gc[pl.ds(0, 1), :] = cout
        anc[pl.ds(0, 1), :] = a[0:1, :]
        for s in range(SUBLANES * LRU_GROUPS):
            rows = pl.ds(s * seg, seg)
            b_s[rows, :] = h_s[rows, :] + p_s[rows, :] * cs[s]
        g = b_s[...]

        d_a = g * hbuf[pl.ds(LRU_HALO - 1, tm), :]
        gx_ = g * xr
        d_log_a = d_a * a - (gx_ * ig) * (a * a / mult)
        dga = (d_log_a * (-LRU_C * sp)) * r * (1.0 - r)
        dgx = (gx_ * mult) * ig * (1.0 - ig)
        dga_b, dgx_b = dga.astype(BF16), dgx.astype(BF16)
        dxr = g * mult * ig + _dot_nt(dga_b, wa) + _dot_nt(dgx_b, wx)
        xb = xr.astype(BF16)
        dwa_ref[...] += _dot_tn(xb, dga_b)
        dwx_ref[...] += _dot_tn(xb, dgx_b)
        st_ref[pl.ds(K4, 1), :] += _colsum(dxr)
        st_ref[pl.ds(K4 + 1, 1), :] += _colsum(dga)
        st_ref[pl.ds(K4 + 2, 1), :] += _colsum(dgx)
        st_ref[pl.ds(K4 + 3, 1), :] += _colsum(d_log_a * (-LRU_C * r)) * (-jax.nn.sigmoid(-lam_v))

        dbuf[pl.ds(0, tm), :] = dxr
        for k in range(K4):
            st_ref[pl.ds(k, 1), :] += _colsum(dxr * xbuf[pl.ds(LRU_HALO - (K4 - 1) + k, tm), :])
        dzx_ref[...] = _tap_sum(dbuf, w4_ref, K4, 0, 0, tm, True).astype(BF16)
        dbuf[pl.ds(tm, LRU_HALO), :] = dbuf[pl.ds(0, LRU_HALO), :]

    def rev(col):
        return lambda c, i: (nI - 1 - i, col + c)

    def rev_prev(col):
        return lambda c, i: (jnp.maximum((nI - 1 - i) * hb - 1, 0), col + c)

    vec = pl.BlockSpec((1, LANES), lambda c, i: (0, c))
    mat = pl.BlockSpec((None, LANES, LANES), lambda c, i: (c, 0, 0))
    big = pltpu.VMEM((tm, LANES), F32)
    halo = pltpu.VMEM((tm + LRU_HALO, LANES), F32)
    return pl.pallas_call(
        body, grid=(nC, nI),
        in_specs=[pl.BlockSpec((tm, LANES), rev(cd)),
                  pl.BlockSpec((tm, LANES), rev(0)), pl.BlockSpec((LRU_HALO, LANES), rev_prev(0)),
                  pl.BlockSpec((tm, LANES), rev(cx)), pl.BlockSpec((LRU_HALO, LANES), rev_prev(cx)),
                  pl.BlockSpec((tm, LANES), rev(cg)),
                  pl.BlockSpec((K4, LANES), lambda c, i: (0, c)), vec, mat, vec, mat, vec, vec],
        out_specs=[pl.BlockSpec((tm, LANES), rev(0)), pl.BlockSpec((tm, LANES), rev(0)),
                   pl.BlockSpec((SUBLANES, LANES), lambda c, i: (0, c)), mat, mat],
        out_shape=[jax.ShapeDtypeStruct((T, W), BF16), jax.ShapeDtypeStruct((T, W), BF16),
                   jax.ShapeDtypeStruct((SUBLANES, W), F32),
                   jax.ShapeDtypeStruct((nC, LANES, LANES), F32), jax.ShapeDtypeStruct((nC, LANES, LANES), F32)],
        scratch_shapes=[halo, halo, halo, big, big, big, big, halo,
                        pltpu.VMEM((SUBLANES, LANES), F32), pltpu.VMEM((SUBLANES, LANES), F32)],
        compiler_params=_params("parallel", "arbitrary"), name="lru_bwd")(
            dcat, hs, hs, z, z, z, w4, b4, wa, ba, wx, bx, lam)


def _mix_out_fwd(x, u, yr, wout):
    T, D = x.shape
    C, W = u.shape[1], yr.shape[1]
    tm = _tile(T, TOK_TILE)

    def body(x_ref, u_ref, yr_ref, w_ref, y_ref):
        y_ref[...] = (x_ref[...] + _dot(u_ref[...], w_ref[pl.ds(0, C), :])
                      + _dot(yr_ref[...], w_ref[pl.ds(C, W), :]))

    return pl.pallas_call(
        body, grid=(T // tm,),
        in_specs=[pl.BlockSpec((tm, D), lambda i: (i, 0)), pl.BlockSpec((tm, C), lambda i: (i, 0)),
                  pl.BlockSpec((tm, W), lambda i: (i, 0)),
                  pl.BlockSpec((C + W, D), lambda i: (0, 0), pipeline_mode=pl.Buffered(1))],
        out_specs=pl.BlockSpec((tm, D), lambda i: (i, 0)),
        out_shape=jax.ShapeDtypeStruct((T, D), F32),
        compiler_params=_params("parallel"), name="mix_out_fwd")(x, u, yr, wout)


def _mix_out_bwd(dy, u, yr, wout):
    T, D = dy.shape
    C, W = u.shape[1], yr.shape[1]
    tm = _tile(T, BWD_TILE)

    def body(dy_ref, u_ref, yr_ref, w_ref, dcat_ref, dw_ref):
        @pl.when(pl.program_id(0) == 0)
        def _():
            dw_ref[...] = jnp.zeros_like(dw_ref)

        dyb = dy_ref[...].astype(BF16)
        dcat_ref[...] = _dot_nt(dyb, w_ref[...])
        dw_ref[pl.ds(0, C), :] += _dot_tn(u_ref[...], dyb)
        dw_ref[pl.ds(C, W), :] += _dot_tn(yr_ref[...], dyb)

    return pl.pallas_call(
        body, grid=(T // tm,),
        in_specs=[pl.BlockSpec((tm, D), lambda i: (i, 0)), pl.BlockSpec((tm, C), lambda i: (i, 0)),
                  pl.BlockSpec((tm, W), lambda i: (i, 0)),
                  pl.BlockSpec((C + W, D), lambda i: (0, 0), pipeline_mode=pl.Buffered(1))],
        out_specs=[pl.BlockSpec((tm, C + W), lambda i: (i, 0)), pl.BlockSpec((C + W, D), lambda i: (0, 0))],
        out_shape=[jax.ShapeDtypeStruct((T, C + W), F32), jax.ShapeDtypeStruct((C + W, D), F32)],
        compiler_params=_params("arbitrary"), name="mix_out_bwd")(dy, u, yr, wout)


def _mix_in_bwd(dzc, dzx, dzg, x, dy, g, win):
    T, D = x.shape
    ns, ws = win.shape[0], win.shape[2]
    tm = _tile(T, BWD_TILE)
    parts = []
    for j in range(ns):
        lo = j * ws
        if lo < dzc.shape[1]:
            parts.append((0, lo))
        elif lo < dzc.shape[1] + dzx.shape[1]:
            parts.append((1, lo - dzc.shape[1]))
        else:
            parts.append((2, lo - dzc.shape[1] - dzx.shape[1]))

    def body(dzc_ref, dzx_ref, dzg_ref, x_ref, dy_ref, g_ref, w_ref, dx_ref, dw_ref, dg_ref):
        @pl.when(pl.program_id(0) == 0)
        def _():
            dw_ref[...] = jnp.zeros_like(dw_ref)
            dg_ref[...] = jnp.zeros_like(dg_ref)

        xh, r = _rms_stats(x_ref[...])
        gv = g_ref[...]
        hb = (xh * gv).astype(BF16)
        srcs = (dzc_ref, dzx_ref, dzg_ref)
        dh = jnp.zeros((tm, D), F32)
        for j, (si, off) in enumerate(parts):
            dzj = srcs[si][:, pl.ds(off, ws)]
            dh = dh + _dot_nt(dzj, w_ref[j])
            dw_ref[j] += _dot_tn(hb, dzj)
        dx_ref[...] = dy_ref[...] + _rms_bwd(dh, xh, r, gv)
        dg_ref[...] += _colsum(dh * xh)

    def tok(n):
        return pl.BlockSpec((tm, n), lambda i: (i, 0))

    vec = pl.BlockSpec((1, D), lambda i: (0, 0))
    return pl.pallas_call(
        body, grid=(T // tm,),
        in_specs=[tok(dzc.shape[1]), tok(dzx.shape[1]), tok(dzg.shape[1]), tok(D), tok(D), vec,
                  pl.BlockSpec((ns, D, ws), lambda i: (0, 0, 0), pipeline_mode=pl.Buffered(1))],
        out_specs=[tok(D), pl.BlockSpec((ns, D, ws), lambda i: (0, 0, 0)), vec],
        out_shape=[jax.ShapeDtypeStruct((T, D), F32), jax.ShapeDtypeStruct((ns, D, ws), F32),
                   jax.ShapeDtypeStruct((1, D), F32)],
        compiler_params=_params("arbitrary"), name="mix_in_bwd")(dzc, dzx, dzg, x, dy, g, win)


def _adamw(w, g, m, v, name):
    R, Cc = w.shape
    tr = _tile(R, max(SUBLANES, (1 << 19) // Cc))
    c1 = 1.0 - ADAM_B1 ** ADAM_STEP
    c2 = 1.0 - ADAM_B2 ** ADAM_STEP

    def body(w_ref, g_ref, m_ref, v_ref, d_ref, nm_ref, nv_ref):
        gv = g_ref[...]
        nm = ADAM_B1 * m_ref[...] + (1.0 - ADAM_B1) * gv
        nv = ADAM_B2 * v_ref[...] + (1.0 - ADAM_B2) * (gv * gv)
        nm_ref[...] = nm
        nv_ref[...] = nv
        d_ref[...] = -ADAM_LR * ((nm / c1) / (jnp.sqrt(nv / c2) + ADAM_EPS) + ADAM_WD * w_ref[...])

    blk = pl.BlockSpec((tr, Cc), lambda i: (i, 0))
    sds = jax.ShapeDtypeStruct((R, Cc), F32)
    return pl.pallas_call(
        body, grid=(R // tr,), in_specs=[blk] * 4, out_specs=[blk] * 3, out_shape=[sds] * 3,
        compiler_params=_params("parallel"), name=name)(w, g, m, v)


def _here():
    return lax.axis_index("x"), lax.axis_index("y"), lax.axis_index("c")


def _chip_at(x, y, m):
    return x ^ (m >> 1), y ^ (m & 1)


ANY = pl.BlockSpec(memory_space=pl.ANY)


def _place_cast(srcs, idx, dtype, name):
    n = len(srcs)
    R, Cc = srcs[0].shape
    tr = _tile(R, max(16, (1 << 18) // Cc), 16)

    def body(i_ref, *refs):
        o_ref = refs[n]
        for k in range(n):
            o_ref[k] = refs[k][...].astype(dtype)

    blk = pl.BlockSpec((tr, Cc), lambda i, s: (i, 0))
    return pl.pallas_call(
        body,
        grid_spec=pltpu.PrefetchScalarGridSpec(
            num_scalar_prefetch=1, grid=(R // tr,), in_specs=[blk] * n,
            out_specs=pl.BlockSpec((n, None, tr, Cc), lambda i, s: (0, s[1], i, 0))),
        out_shape=jax.ShapeDtypeStruct((n, N_CHIPS, R, Cc), dtype),
        compiler_params=_params("parallel"), name=name)(idx, *srcs)


def _gather_weights(lands):
    n = len(lands)

    def body(*refs):
        outs = refs[n:2 * n]
        send1, recv1, send2, recv2 = refs[2 * n:]
        x, y, c = _here()
        own = 2 * x + y

        def half(ref, chip, cc):
            rh = ref.shape[-2] // 2
            lead = (slice(None),) * (len(ref.shape) - 3)
            return ref.at[lead + (chip, pl.ds(cc * rh, rh), slice(None))]

        first = []
        for k in range(n):
            for m in (1, 2, 3):
                px, py = _chip_at(x, y, m)
                cp = pltpu.make_async_remote_copy(
                    src_ref=half(outs[k], own, c), dst_ref=half(outs[k], own, c),
                    send_sem=send1.at[k, m - 1], recv_sem=recv1.at[k, m - 1],
                    device_id=(px, py, c), device_id_type=MESH)
                cp.start()
                first.append(cp)

        passed = []
        for k in range(n):
            for m in (1, 2, 3):
                px, py = _chip_at(x, y, m)
                peer = 2 * px + py
                got = half(outs[k], peer, c)
                pltpu.make_async_remote_copy(
                    src_ref=got, dst_ref=got, send_sem=send1.at[k, m - 1], recv_sem=recv1.at[k, m - 1],
                    device_id=(px, py, c), device_id_type=MESH).wait_recv()
                cp = pltpu.make_async_remote_copy(
                    src_ref=got, dst_ref=got, send_sem=send2.at[k, m - 1], recv_sem=recv2.at[k, m - 1],
                    device_id=(x, y, 1 - c), device_id_type=MESH)
                cp.start()
                passed.append(cp)

        for k in range(n):
            for m in (1, 2, 3):
                px, py = _chip_at(x, y, m)
                other = half(outs[k], 2 * px + py, 1 - c)
                pltpu.make_async_remote_copy(
                    src_ref=other, dst_ref=other, send_sem=send2.at[k, m - 1], recv_sem=recv2.at[k, m - 1],
                    device_id=(x, y, 1 - c), device_id_type=MESH).wait_recv()
        for cp in first + passed:
            cp.wait_send()

    return pl.pallas_call(
        body, in_specs=[ANY] * n, out_specs=[ANY] * n,
        out_shape=[jax.ShapeDtypeStruct(a.shape, a.dtype) for a in lands],
        input_output_aliases={k: k for k in range(n)},
        scratch_shapes=[pltpu.SemaphoreType.DMA((n, 3)), pltpu.SemaphoreType.DMA((n, 3)),
                        pltpu.SemaphoreType.DMA((n, 3)), pltpu.SemaphoreType.DMA((n, 3))],
        name="gather_weights")(*lands)


HBM = pl.BlockSpec(memory_space=pltpu.HBM)
SEM = pl.BlockSpec(memory_space=pltpu.SEMAPHORE)
EFFECT = pltpu.SideEffectType.DATAFLOW_SIDE_EFFECTING


def _in_hbm(a):
    return pltpu.with_memory_space_constraint(a, pltpu.HBM)


def _gather_copies(land_refs, send, recv):
    x, y, c = _here()
    own = 2 * x + y
    cps = []
    for k in range(len(land_refs)):
        lead = (slice(None),) * (len(land_refs[k].shape) - 3)
        mine = land_refs[k].at[lead + (own,)]
        for m in (1, 2, 3):
            px, py = _chip_at(x, y, m)
            cps.append(pltpu.make_async_remote_copy(
                src_ref=mine, dst_ref=mine, send_sem=send.at[3 * k + m - 1], recv_sem=recv.at[3 * k + m - 1],
                device_id=(px, py, c), device_id_type=MESH))
    return cps


def _gather_start(lands, after, name):
    n = len(lands)

    def body(*refs):
        lz = refs[:n]
        send, recv = refs[n + 1], refs[n + 2]
        token = refs[-1]
        for cp in _gather_copies(lz, send, recv):
            cp.start()
        token[...] = jnp.zeros_like(token)

    hbm = [pltpu.HBM(a.shape, a.dtype) for a in lands]
    outs = pl.pallas_call(
        body, name=name,
        in_specs=[HBM] * n + [ANY],
        out_specs=[SEM, SEM] + [HBM] * n + [pl.BlockSpec(memory_space=pltpu.VMEM)],
        out_shape=[pltpu.SemaphoreType.DMA((3 * n,)), pltpu.SemaphoreType.DMA((3 * n,))] + hbm
        + [jax.ShapeDtypeStruct((SUBLANES, LANES), F32)],
        input_output_aliases={k: 2 + k for k in range(n)},
        compiler_params=pltpu.CompilerParams(has_side_effects=EFFECT),
    )(*[_in_hbm(a) for a in lands], after)
    return outs[0], outs[1], outs[2:2 + n], outs[-1]


def _gather_wait(send, recv, lands, after, name):
    n = len(lands)

    def body(*refs):
        lz = refs[:n]
        send_r, recv_r = refs[n], refs[n + 1]
        for cp in _gather_copies(lz, send_r, recv_r):
            cp.wait_send()
            cp.wait_recv()

    hbm = [pltpu.HBM(a.shape, a.dtype) for a in lands]
    return pl.pallas_call(
        body, name=name,
        in_specs=[HBM] * n + [SEM, SEM, ANY],
        out_specs=[HBM] * n, out_shape=hbm,
        input_output_aliases={k: k for k in range(n)},
        compiler_params=pltpu.CompilerParams(has_side_effects=EFFECT),
    )(*lands, send, recv, after)


def _exchange_copies(part_refs, slot_refs, send, recv):
    x, y, c = _here()
    cps = []
    for k in range(len(part_refs)):
        for m in (1, 2, 3):
            px, py = _chip_at(x, y, m)
            cps.append(pltpu.make_async_remote_copy(
                src_ref=part_refs[k].at[2 * px + py], dst_ref=slot_refs[k].at[m - 1],
                send_sem=send.at[3 * k + m - 1], recv_sem=recv.at[3 * k + m - 1],
                device_id=(px, py, c), device_id_type=MESH))
    return cps


def _exchange_start(parts, name):
    n = len(parts)
    lands = [lax.empty((N_CHIPS - 1,) + p.shape[1:], p.dtype) for p in parts]

    def body(*refs):
        ins, lz = refs[:n], refs[n:2 * n]
        send, recv = refs[2 * n], refs[2 * n + 1]
        token = refs[-1]
        for cp in _exchange_copies(ins, lz, send, recv):
            cp.start()
        token[...] = jnp.zeros_like(token)

    hbm = [pltpu.HBM(a.shape, a.dtype) for a in list(parts) + lands]
    outs = pl.pallas_call(
        body, name=name,
        in_specs=[HBM] * (2 * n),
        out_specs=[SEM, SEM] + [HBM] * (2 * n) + [pl.BlockSpec(memory_space=pltpu.VMEM)],
        out_shape=[pltpu.SemaphoreType.DMA((3 * n,)), pltpu.SemaphoreType.DMA((3 * n,))] + hbm
        + [jax.ShapeDtypeStruct((SUBLANES, LANES), F32)],
        input_output_aliases={k: 2 + k for k in range(2 * n)},
        compiler_params=pltpu.CompilerParams(has_side_effects=EFFECT),
    )(*[_in_hbm(a) for a in parts], *[_in_hbm(a) for a in lands])
    return outs[0], outs[1], outs[2:2 + n], outs[2 + n:2 + 2 * n], outs[-1]


def _exchange_wait(send, recv, parts, lands, after, name):
    n = len(parts)

    def body(*refs):
        ins, lz = refs[:n], refs[n:2 * n]
        send_r, recv_r = refs[2 * n], refs[2 * n + 1]
        for cp in _exchange_copies(ins, lz, send_r, recv_r):
            cp.wait_send()
            cp.wait_recv()

    hbm = [pltpu.HBM(a.shape, a.dtype) for a in list(parts) + list(lands)]
    outs = pl.pallas_call(
        body, name=name,
        in_specs=[HBM] * (2 * n) + [SEM, SEM, ANY],
        out_specs=[HBM] * (2 * n), out_shape=hbm,
        input_output_aliases={k: k for k in range(2 * n)},
        compiler_params=pltpu.CompilerParams(has_side_effects=EFFECT),
    )(*parts, *lands, send, recv, after)
    return outs[:n], outs[n:]


def _swap_halves_out(grads, name):
    n = len(grads)
    out_shapes = [jax.ShapeDtypeStruct((g.shape[0], g.shape[1] // 2, g.shape[2]), g.dtype) for g in grads]

    def body(*refs):
        ins, outs = refs[:n], refs[n:2 * n]
        send, recv = refs[2 * n:]
        x, y, c = _here()
        cps = []
        for k in range(n):
            rh = ins[k].shape[1] // 2
            cp = pltpu.make_async_remote_copy(
                src_ref=ins[k].at[:, pl.ds((1 - c) * rh, rh), :], dst_ref=outs[k],
                send_sem=send.at[k], recv_sem=recv.at[k], device_id=(x, y, 1 - c), device_id_type=MESH)
            cp.start()
            cps.append(cp)
        for cp in cps:
            cp.wait()

    return pl.pallas_call(
        body, in_specs=[ANY] * n, out_specs=[ANY] * n, out_shape=out_shapes,
        scratch_shapes=[pltpu.SemaphoreType.DMA((n,)), pltpu.SemaphoreType.DMA((n,))],
        name=name)(*grads)


def _swap_copies(grad_refs, land_refs, send, recv):
    x, y, c = _here()
    cps = []
    for k in range(len(grad_refs)):
        rh = grad_refs[k].shape[1] // 2
        cps.append(pltpu.make_async_remote_copy(
            src_ref=grad_refs[k].at[:, pl.ds((1 - c) * rh, rh), :], dst_ref=land_refs[k],
            send_sem=send.at[k], recv_sem=recv.at[k], device_id=(x, y, 1 - c), device_id_type=MESH))
    return cps


def _swap_start(grads, name):
    n = len(grads)
    lands = [lax.empty((g.shape[0], g.shape[1] // 2, g.shape[2]), g.dtype) for g in grads]

    def body(*refs):
        ins, lz = refs[:n], refs[n:2 * n]
        send, recv = refs[2 * n], refs[2 * n + 1]
        token = refs[-1]
        for cp in _swap_copies(ins, lz, send, recv):
            cp.start()
        token[...] = jnp.zeros_like(token)

    hbm = [pltpu.HBM(a.shape, a.dtype) for a in list(grads) + lands]
    outs = pl.pallas_call(
        body, name=name,
        in_specs=[HBM] * (2 * n),
        out_specs=[SEM, SEM] + [HBM] * (2 * n) + [pl.BlockSpec(memory_space=pltpu.VMEM)],
        out_shape=[pltpu.SemaphoreType.DMA((n,)), pltpu.SemaphoreType.DMA((n,))] + hbm
        + [jax.ShapeDtypeStruct((SUBLANES, LANES), F32)],
        input_output_aliases={k: 2 + k for k in range(2 * n)},
        compiler_params=pltpu.CompilerParams(has_side_effects=EFFECT),
    )(*[_in_hbm(a) for a in grads], *[_in_hbm(a) for a in lands])
    return outs[0], outs[1], outs[2:2 + n], outs[2 + n:2 + 2 * n], outs[-1]


def _swap_wait(send, recv, grads, lands, after, name):
    n = len(grads)

    def body(*refs):
        ins, lz = refs[:n], refs[n:2 * n]
        send_r, recv_r = refs[2 * n], refs[2 * n + 1]
        for cp in _swap_copies(ins, lz, send_r, recv_r):
            cp.wait_send()
            cp.wait_recv()

    hbm = [pltpu.HBM(a.shape, a.dtype) for a in list(grads) + list(lands)]
    outs = pl.pallas_call(
        body, name=name,
        in_specs=[HBM] * (2 * n) + [SEM, SEM, ANY],
        out_specs=[HBM] * (2 * n), out_shape=hbm,
        input_output_aliases={k: k for k in range(2 * n)},
        compiler_params=pltpu.CompilerParams(has_side_effects=EFFECT),
    )(*grads, *lands, send, recv, after)
    return outs[:n], outs[n:]


def _add_cast(g, other, cidx, name):
    ns, R, Cc = g.shape
    rh = R // 2
    tr = _tile(rh, max(16, (1 << 19) // Cc), 16)
    nb = rh // tr

    def body(c_ref, g_ref, o_ref, s_ref):
        s_ref[...] = (g_ref[...] + o_ref[...]).astype(BF16)

    return pl.pallas_call(
        body,
        grid_spec=pltpu.PrefetchScalarGridSpec(
            num_scalar_prefetch=1, grid=(ns, nb),
            in_specs=[pl.BlockSpec((None, tr, Cc), lambda k, i, c: (k, c[0] * nb + i, 0)),
                      pl.BlockSpec((None, tr, Cc), lambda k, i, c: (k, i, 0))],
            out_specs=pl.BlockSpec((None, tr, Cc), lambda k, i, c: (k, i, 0))),
        out_shape=jax.ShapeDtypeStruct((ns, rh, Cc), BF16),
        compiler_params=_params("parallel", "parallel"), name=name)(cidx, g, other)


def _sum_slots(part, got, idx, name):
    ns, rh, Cc = got.shape
    tr = _tile(rh, max(16, (1 << 18) // Cc), 16)
    nb = rh // tr

    def body(i_ref, p_ref, b_ref, o_ref):
        acc = p_ref[...].astype(F32)
        for m in range(ns):
            acc = acc + b_ref[m].astype(F32)
        o_ref[...] = acc

    return pl.pallas_call(
        body,
        grid_spec=pltpu.PrefetchScalarGridSpec(
            num_scalar_prefetch=1, grid=(nb,),
            in_specs=[pl.BlockSpec((None, tr, Cc), lambda i, s: (s[1], i, 0)),
                      pl.BlockSpec((ns, tr, Cc), lambda i, s: (0, i, 0))],
            out_specs=pl.BlockSpec((tr, Cc), lambda i, s: (s[0] * nb + i, 0))),
        out_shape=jax.ShapeDtypeStruct((2 * rh, Cc), F32),
        compiler_params=_params("parallel"), name=name)(idx, part, got)


def _share_halves(blocks, name):
    n = len(blocks)

    def body(*refs):
        ins, outs = refs[:n], refs[n:2 * n]
        send, recv = refs[2 * n:]
        x, y, c = _here()
        cps = []
        for k in range(n):
            rh = outs[k].shape[0] // 2
            mine = outs[k].at[pl.ds(c * rh, rh), :]
            cp = pltpu.make_async_remote_copy(
                src_ref=mine, dst_ref=mine, send_sem=send.at[k], recv_sem=recv.at[k],
                device_id=(x, y, 1 - c), device_id_type=MESH)
            cp.start()
            cps.append(cp)
        for cp in cps:
            cp.wait()

    return pl.pallas_call(
        body, in_specs=[ANY] * n, out_specs=[ANY] * n,
        out_shape=[jax.ShapeDtypeStruct(b.shape, b.dtype) for b in blocks],
        input_output_aliases={k: k for k in range(n)},
        scratch_shapes=[pltpu.SemaphoreType.DMA((n,)), pltpu.SemaphoreType.DMA((n,))],
        name=name)(*blocks)


def _small_copies(p_ref, slot_ref, send, recv):
    x, y, c = _here()
    mine = slot_ref.at[4 * x + 2 * y + c]
    cps = []
    for m in range(1, N_DEV):
        peer = (x ^ (m >> 2), y ^ ((m >> 1) & 1), c ^ (m & 1))
        cps.append(pltpu.make_async_remote_copy(
            src_ref=p_ref, dst_ref=mine, send_sem=send.at[m - 1], recv_sem=recv.at[m - 1],
            device_id=peer, device_id_type=MESH))
    return cps


def _small_start(packed):
    slots = lax.empty((N_DEV,) + packed.shape, packed.dtype)

    def body(p_ref, s_ref, send, recv, p_thru, s_thru, token):
        for cp in _small_copies(p_ref, s_ref, send, recv):
            cp.start()
        token[...] = jnp.zeros_like(token)

    return pl.pallas_call(
        body, name="small_start",
        in_specs=[HBM, HBM],
        out_specs=[SEM, SEM, HBM, HBM, pl.BlockSpec(memory_space=pltpu.VMEM)],
        out_shape=[pltpu.SemaphoreType.DMA((N_DEV - 1,)), pltpu.SemaphoreType.DMA((N_DEV - 1,)),
                   pltpu.HBM(packed.shape, packed.dtype), pltpu.HBM(slots.shape, slots.dtype),
                   jax.ShapeDtypeStruct((SUBLANES, LANES), F32)],
        input_output_aliases={0: 2, 1: 3},
        compiler_params=pltpu.CompilerParams(has_side_effects=EFFECT),
    )(_in_hbm(packed), _in_hbm(slots))


def _small_wait(send, recv, packed, slots, after):
    def body(p_ref, s_ref, send_r, recv_r, after_ref, p_out, s_out):
        for cp in _small_copies(p_ref, s_ref, send_r, recv_r):
            cp.wait_send()
            cp.wait_recv()

    return pl.pallas_call(
        body, name="small_wait",
        in_specs=[HBM, HBM, SEM, SEM, ANY], out_specs=[HBM, HBM],
        out_shape=[pltpu.HBM(packed.shape, packed.dtype), pltpu.HBM(slots.shape, slots.dtype)],
        input_output_aliases={0: 0, 1: 1},
        compiler_params=pltpu.CompilerParams(has_side_effects=EFFECT),
    )(packed, slots, send, recv, after)


def _sum_devices(packed, slots, me):
    n, R, _ = slots.shape
    tr = _tile(R, 1024)

    def body(m_ref, p_ref, s_ref, o_ref):
        own = p_ref[...]
        acc = None
        for d in range(n):
            term = jnp.where(m_ref[0] == d, own, s_ref[d])
            acc = term if acc is None else acc + term
        o_ref[...] = acc

    return pl.pallas_call(
        body,
        grid_spec=pltpu.PrefetchScalarGridSpec(
            num_scalar_prefetch=1, grid=(R // tr,),
            in_specs=[pl.BlockSpec((tr, LANES), lambda i, m: (i, 0)),
                      pl.BlockSpec((n, tr, LANES), lambda i, m: (0, i, 0))],
            out_specs=pl.BlockSpec((tr, LANES), lambda i, m: (i, 0))),
        out_shape=jax.ShapeDtypeStruct((R, LANES), F32),
        compiler_params=_params("parallel"), name="sum_devices")(me, packed, slots)


def _pack(arrs):
    rows, parts = [], []
    for a in arrs:
        flat = a.reshape(-1)
        r = -(-flat.shape[0] // (SUBLANES * LANES)) * SUBLANES
        parts.append(jnp.pad(flat, (0, r * LANES - flat.shape[0])).reshape(r, LANES))
        rows.append(r)
    return jnp.concatenate(parts, axis=0), rows


def _unpack(packed, rows, shapes):
    out, r0 = [], 0
    for r, shp in zip(rows, shapes):
        size = math.prod(shp)
        out.append(packed[r0:r0 + r].reshape(-1)[:size].reshape(shp))
        r0 += r
    return out


def _block_diag(w, per):
    H, dh, _ = w.shape
    w4 = w.reshape(H // per, per, dh, dh)
    eye = jnp.eye(per, dtype=w.dtype)
    return (w4[:, :, :, None, :] * eye[None, :, None, :, None]).reshape(H // per, per * dh, per * dh)


def _block_diag_take(d, per):
    n, s, _ = d.shape
    dh = s // per
    d5 = d.reshape(n, per, dh, per, dh)
    return jnp.stack([d5[:, h, :, h, :] for h in range(per)], axis=1).reshape(n * per, dh, dh)


def kernel(x, ffn1_norm, ffn1_w_gate, ffn1_w_up, ffn1_w_down, mix_norm, w_in, conv_dw, conv_dw_bias, conv_ln_g, conv_ln_b, lru_conv_w, lru_conv_b, lru_w_a, lru_b_a, lru_w_x, lru_b_x, lru_lambda, w_out, ffn2_norm, ffn2_w_gate, ffn2_w_up, ffn2_w_down, final_norm, loss_target, m_ffn1_norm, m_ffn1_w_gate, m_ffn1_w_up, m_ffn1_w_down, m_mix_norm, m_w_in, m_conv_dw, m_conv_dw_bias, m_conv_ln_g, m_conv_ln_b, m_lru_conv_w, m_lru_conv_b, m_lru_w_a, m_lru_b_a, m_lru_w_x, m_lru_b_x, m_lru_lambda, m_w_out, m_ffn2_norm, m_ffn2_w_gate, m_ffn2_w_up, m_ffn2_w_down, m_final_norm, v_ffn1_norm, v_ffn1_w_gate, v_ffn1_w_up, v_ffn1_w_down, v_mix_norm, v_w_in, v_conv_dw, v_conv_dw_bias, v_conv_ln_g, v_conv_ln_b, v_lru_conv_w, v_lru_conv_b, v_lru_w_a, v_lru_b_a, v_lru_w_x, v_lru_b_x, v_lru_lambda, v_w_out, v_ffn2_norm, v_ffn2_w_gate, v_ffn2_w_up, v_ffn2_w_down, v_final_norm):
    names = ['ffn1_norm', 'ffn1_w_gate', 'ffn1_w_up', 'ffn1_w_down', 'mix_norm', 'w_in', 'conv_dw', 'conv_dw_bias',
             'conv_ln_g', 'conv_ln_b', 'lru_conv_w', 'lru_conv_b', 'lru_w_a', 'lru_b_a', 'lru_w_x', 'lru_b_x',
             'lru_lambda', 'w_out', 'ffn2_norm', 'ffn2_w_gate', 'ffn2_w_up', 'ffn2_w_down', 'final_norm']
    env = dict(locals())
    W = {n: env[n] for n in names}
    M = {n: env['m_' + n] for n in names}
    V = {n: env['v_' + n] for n in names}

    xi, yi, ci = _here()
    chip = 2 * xi + yi
    cidx = ci.astype(jnp.int32).reshape(1)
    T, D = x.shape[-2], x.shape[-1]
    xs = x.reshape(T, D)
    tgt = loss_target.reshape(T, D)
    K, Cs = conv_dw.shape
    C = conv_dw_bias.shape[0]
    Wl = lru_conv_b.shape[0]
    K4 = lru_conv_w.shape[0]
    heads, dh, _ = lru_w_a.shape
    per = LANES // dh

    def row(v):
        return v.reshape(1, -1)

    tform = ('ffn1_w_gate', 'ffn1_w_up', 'ffn2_w_gate', 'ffn2_w_up')
    for n in tform:
        W[n], M[n], V[n] = W[n].T, M[n].T, V[n].T
    kp = -(-K // SUBLANES) * SUBLANES
    taps = jnp.concatenate([conv_dw, jnp.zeros((kp - K, Cs), F32), lru_conv_w,
                            jnp.zeros((2 * SUBLANES - K4, Cs), F32)], axis=0)
    idx = jnp.stack([ci, chip]).astype(jnp.int32)
    (wff1,) = _gather_weights([_place_cast([W['ffn1_w_gate'], W['ffn1_w_up'], ffn1_w_down], idx, BF16, "place_ffn1")])
    mixl = [_place_cast([w_in], idx, BF16, "place_w_in"), _place_cast([w_out], idx, BF16, "place_w_out"),
            _place_cast([taps], idx, F32, "place_taps")]
    msend, mrecv, mixl, mtok = _gather_start(mixl, wff1, "gather_mix_start")
    ff2l = _place_cast([W['ffn2_w_gate'], W['ffn2_w_up'], ffn2_w_down], idx, BF16, "place_ffn2")
    fsend, frecv, ff2l, ftok = _gather_start([ff2l], mtok, "gather_ffn2_start")
    wa_bd = _block_diag(lru_w_a, per).astype(BF16)
    wx_bd = _block_diag(lru_w_x, per).astype(BF16)

    x1, a1, b1 = _ffn_fwd(xs, row(ffn1_norm) + ftok[0:1, 0:1], wff1, "ffn1_fwd")
    win, wout, taps = _gather_wait(msend, mrecv, mixl, x1, "gather_mix_wait")
    win, wout, taps = win[0], wout.reshape(-1, D), taps[0]
    conv_w_full = taps[:, :K].transpose(1, 0, 2).reshape(K, N_CHIPS * Cs)
    lru_w4_full = taps[:, kp:kp + K4].transpose(1, 0, 2).reshape(K4, N_CHIPS * Cs)
    z = _mix_in_fwd(x1, row(mix_norm), win)
    u, u1 = _conv_fwd(z, conv_w_full, row(conv_dw_bias), row(conv_ln_g), row(conv_ln_b))
    yr, hs = _lru_fwd(z, 2 * C, lru_w4_full, row(lru_conv_b), wa_bd, row(lru_b_a), wx_bd, row(lru_b_x),
                      row(lru_lambda))
    x2 = _mix_out_fwd(x1, u, yr, wout)
    (wff2,) = _gather_wait(fsend, frecv, ff2l, x2, "gather_ffn2_wait")
    dx3, a2, b2, loss_blk, d_final = _ffn_fwd(x2, row(ffn2_norm), wff2, "ffn2_fwd", head=(row(final_norm), tgt))

    dx2, da2, db2, p2, hb2, dyh2, d_ffn2n = _ffn_bwd_tok(dx3, x2, row(ffn2_norm), a2, b2, wff2, "ffn2_bwd")
    dwg2, dwu2, dwd2 = _ffn_wgrad([([da2, db2], hb2), ([p2], dyh2)], ftok, "ffn2_wgrad")
    wsend, wrecv, f2g, f2o, wtok = _swap_start([dwg2, dwu2, dwd2], "swap_ffn2_start")
    dcat, dwout = _mix_out_bwd(dx2, u, yr, wout)
    dzc, cst = _conv_bwd(dcat, u1, z, conv_w_full, row(conv_ln_g) + wtok[0:1, 0:1], row(conv_ln_b))
    dzx, dzg, lst, dwa_bd, dwx_bd = _lru_bwd(dcat, C, hs, z, 2 * C, lru_w4_full, row(lru_conv_b), wa_bd,
                                              row(lru_b_a), wx_bd, row(lru_b_x), row(lru_lambda))
    dx1, dwin, d_mixn = _mix_in_bwd(dzc, dzx, dzg, x1, dx2, row(mix_norm), win)

    early_names = ['w_in', 'w_out', 'ffn2_w_gate', 'ffn2_w_up', 'ffn2_w_down']
    mixg = [dwin, dwout.reshape(N_CHIPS, -1, D)]
    mixo = _swap_halves_out(mixg, "swap_halves_mix")
    f2g, f2o = _swap_wait(wsend, wrecv, f2g, f2o, dwin, "swap_ffn2_wait")
    e_parts = [_add_cast(g, o, cidx, "add_cast_" + n)
               for g, o, n in zip(mixg + list(f2g), list(mixo) + list(f2o), early_names)]
    esend, erecv, e_parts, e_lands, etok = _exchange_start(e_parts, "exchange_early_start")

    dx0, da1, db1, p1, hb1, dyh1, d_ffn1n = _ffn_bwd_tok(dx1, xs, row(ffn1_norm) + etok[0:1, 0:1], a1, b1, wff1,
                                                         "ffn1_bwd")

    small_names = ['ffn1_norm', 'mix_norm', 'conv_dw', 'conv_dw_bias', 'conv_ln_g', 'conv_ln_b', 'lru_conv_w',
                   'lru_conv_b', 'lru_w_a', 'lru_b_a', 'lru_w_x', 'lru_b_x', 'lru_lambda', 'ffn2_norm',
                   'final_norm']
    small = {
        'ffn1_norm': d_ffn1n, 'mix_norm': d_mixn, 'conv_dw': cst[:K], 'conv_dw_bias': cst[K + 1],
        'conv_ln_g': cst[K + 2], 'conv_ln_b': cst[K + 3], 'lru_conv_w': lst[:K4], 'lru_conv_b': lst[K4],
        'lru_w_a': _block_diag_take(dwa_bd, per), 'lru_b_a': lst[K4 + 1],
        'lru_w_x': _block_diag_take(dwx_bd, per), 'lru_b_x': lst[K4 + 2], 'lru_lambda': lst[K4 + 3],
        'ffn2_norm': d_ffn2n, 'final_norm': d_final,
    }
    packed, rows = _pack([small[n] for n in small_names] + [loss_blk[0:1, 0:1]])
    ssend, srecv, packed, sslots, stok = _small_start(packed)

    gu_names, d_names = ['ffn1_w_gate', 'ffn1_w_up'], ['ffn1_w_down']
    dn = _ffn_wgrad([([p1], dyh1)], stok, "ffn1_wgrad_d")
    xsend, xrecv, dn, dn_other, xtok = _swap_start(dn, "swap_d_start")
    gu = _ffn_wgrad([([da1, db1], hb1)], xtok, "ffn1_wgrad_gu")
    dwd1 = gu[0]
    dn, dn_other = _swap_wait(xsend, xrecv, dn, dn_other, dwd1, "swap_d_wait")
    d_parts = [_add_cast(g, o, cidx, "add_cast_" + n) for g, o, n in zip(dn, dn_other, d_names)]
    dsend, drecv, d_parts, d_lands, dtok = _exchange_start(d_parts, "exchange_d_start")
    gu_parts = [_add_cast(g, o, cidx, "add_cast_" + n)
                for g, o, n in zip(gu, _swap_halves_out(gu, "swap_halves_gu"), gu_names)]
    gsend, grecv, gu_parts, gu_lands, ltok = _exchange_start(gu_parts, "exchange_gu_start")
    e_parts, e_slots = _exchange_wait(esend, erecv, e_parts, e_lands, ltok, "exchange_early_wait")
    delta, new_m, new_v = {}, {}, {}

    def finish(group, parts, slots, tag):
        halves = [_sum_slots(p, b, idx, "sum_slots_" + n) for p, b, n in zip(parts, slots, group)]
        for n, g in zip(group, _share_halves(halves, "share_halves_" + tag)):
            G[n] = g
            delta[n], new_m[n], new_v[n] = _adamw(W[n], g, M[n], V[n], "adamw_" + n)

    G = {}
    finish(early_names, e_parts, e_slots, "early")

    full_shapes = [(K, C) if n == 'conv_dw' else (K4, Wl) if n == 'lru_conv_w' else W[n].shape for n in small_names]
    packed, sslots = _small_wait(ssend, srecv, packed, sslots, dwd1)
    summed = _sum_devices(packed, sslots, (4 * xi + 2 * yi + ci).astype(jnp.int32).reshape(1))
    *small_sums, loss_sum = _unpack(summed, rows, full_shapes + [(1, 1)])
    for n, gsum in zip(small_names, small_sums):
        if n == 'conv_dw':
            gsum = lax.dynamic_slice_in_dim(gsum, chip * Cs, Cs, axis=1)
        elif n == 'lru_conv_w':
            gsum = lax.dynamic_slice_in_dim(gsum, chip * lru_conv_w.shape[1], lru_conv_w.shape[1], axis=1)
        G[n] = gsum

    pw, prow = _pack([W[n] for n in small_names])
    pg, _ = _pack([G[n] for n in small_names])
    pm, _ = _pack([M[n] for n in small_names])
    pv, _ = _pack([V[n] for n in small_names])
    sd, sm, sv = _adamw(pw, pg, pm, pv, "adamw_small")
    shapes = [W[n].shape for n in small_names]
    for n, a, b, c_ in zip(small_names, _unpack(sd, prow, shapes), _unpack(sm, prow, shapes),
                           _unpack(sv, prow, shapes)):
        delta[n], new_m[n], new_v[n] = a, b, c_

    done = sd[0:SUBLANES] + delta[early_names[-1]][0:SUBLANES, 0:LANES]
    gu_parts, gu_slots = _exchange_wait(gsend, grecv, gu_parts, gu_lands, done, "exchange_gu_wait")
    d_parts, d_slots = _exchange_wait(dsend, drecv, d_parts, d_lands, gu_slots[0], "exchange_d_wait")
    finish(gu_names + d_names, list(gu_parts) + list(d_parts), list(gu_slots) + list(d_slots), "last")

    loss = loss_sum[0, 0]
    grad_x = dx0.reshape(x.shape)
    for n in tform:
        G[n], delta[n], new_m[n], new_v[n] = G[n].T, delta[n].T, new_m[n].T, new_v[n].T
    return (loss, grad_x, *[G[n] for n in names], *[delta[n] for n in names],
            *[new_m[n] for n in names], *[new_v[n] for n in names])
```

```python
import functools
import math

import jax
import jax.numpy as jnp
from jax import lax
from jax.experimental import pallas as pl
from jax.experimental.pallas import tpu as pltpu

F32 = jnp.float32
BF16 = jnp.bfloat16
MESH = pl.DeviceIdType.MESH

RMS_EPS = 1e-6
LN_EPS = 1e-5
LRU_C = 8.0
FFN_RES_SCALE = 0.5
ADAM_LR = 0.001
ADAM_B1 = 0.9
ADAM_B2 = 0.999
ADAM_EPS = 1e-08
ADAM_WD = 0.01
ADAM_STEP = 10

LANES = 128
SUBLANES = 8
CONV_HALO = 32
LRU_HALO = 8
ROW_CHUNK = 64
VMEM_LIMIT = 56 * 1024 * 1024
N_CHIPS = 4
N_DEV = 8
TOK_TILE = 1024
BWD_TILE = 512
FFN_BWD_TILE = 512
BWD_ROWS = 32
FFN_BWD_CHAIN = 256
CONV_TILE = 512
LRU_TILE = 2048
LRU_GROUPS = 8


def _dot(a, b):
    return jnp.dot(a, b, preferred_element_type=F32)


def _dot_nt(a, b):
    return lax.dot_general(a, b, (((1,), (1,)), ((), ())), preferred_element_type=F32)


def _dot_tn(a, b):
    return lax.dot_general(a, b, (((0,), (0,)), ((), ())), preferred_element_type=F32)


def _tile(n, pref, mult=SUBLANES):
    for t in range(min(pref, n), 0, -1):
        if n % t == 0 and t % mult == 0:
            return t
    return n


def _params(*sem):
    return pltpu.CompilerParams(dimension_semantics=sem, vmem_limit_bytes=VMEM_LIMIT)


def _rms_stats(x):
    r = lax.rsqrt(jnp.mean(x * x, axis=-1, keepdims=True) + RMS_EPS)
    return x * r, r


def _rms_bwd(dh, xh, r, g):
    dxh = dh * g
    return r * (dxh - xh * jnp.mean(dxh * xh, axis=-1, keepdims=True))


def _colsum(v):
    return jnp.sum(v, axis=0, keepdims=True)


def _ffn_fwd(x, g, wff, name, head=None):
    T, D = x.shape
    ns, fs = wff.shape[1], wff.shape[2]
    tm = _tile(T, TOK_TILE)
    mc = _tile(tm, FFN_BWD_CHAIN, 16)
    rc = _tile(tm, FFN_BWD_CHAIN)

    def body(*refs):
        x_ref, g_ref, wg_ref, wu_ref, wd_ref = refs[:5]
        if head is None:
            y_ref, a_ref, b_ref, hb_ref, acc_ref = refs[5:]
        else:
            gf_ref, t_ref, y_ref, a_ref, b_ref, loss_ref, dgf_ref, hb_ref, acc_ref = refs[5:]
        j = pl.program_id(1)

        @pl.when(j == 0)
        def _():
            xh, _ = _rms_stats(x_ref[...])
            hb_ref[...] = (xh * g_ref[...]).astype(BF16)
            acc_ref[...] = jnp.zeros_like(acc_ref)

        if head is not None:
            @pl.when((pl.program_id(0) == 0) & (j == 0))
            def _():
                loss_ref[...] = jnp.zeros_like(loss_ref)
                dgf_ref[...] = jnp.zeros_like(dgf_ref)

        for q0 in range(0, tm, mc):
            blk = pl.ds(q0, mc)
            hb = hb_ref[blk, :]
            a = _dot_nt(hb, wg_ref[...])
            b = _dot_nt(hb, wu_ref[...])
            a_ref[blk, :] = a.astype(BF16)
            b_ref[blk, :] = b.astype(BF16)
            p = (a * jax.nn.sigmoid(a) * b).astype(BF16)
            acc_ref[blk, :] += _dot(p, wd_ref[...])

        @pl.when(j == ns - 1)
        def _():
            if head is None:
                y_ref[...] = x_ref[...] + FFN_RES_SCALE * acc_ref[...]
                return
            gv = gf_ref[...]
            loss = jnp.zeros((), F32)
            dg = jnp.zeros((1, D), F32)
            for r0 in range(0, tm, rc):
                rows = pl.ds(r0, rc)
                xh, r = _rms_stats(x_ref[rows, :] + FFN_RES_SCALE * acc_ref[rows, :])
                e = xh * gv - t_ref[rows, :]
                loss = loss + 0.5 * jnp.sum(jnp.mean(e * e, axis=-1, keepdims=True))
                dy = e * (1.0 / D)
                dg = dg + _colsum(dy * xh)
                y_ref[rows, :] = _rms_bwd(dy, xh, r, gv)
            loss_ref[...] += loss
            dgf_ref[...] += dg

    def wspec(n):
        return pl.BlockSpec((None, None, fs, D), lambda i, j: (n, j, 0, 0))

    tok = pl.BlockSpec((tm, D), lambda i, j: (i, 0))
    vec = pl.BlockSpec((1, D), lambda i, j: (0, 0))
    mid = pl.BlockSpec((None, tm, fs), lambda i, j: (j, i, 0))
    in_specs = [tok, vec, wspec(0), wspec(1), wspec(2)]
    out_specs = [tok, mid, mid]
    out_shape = [jax.ShapeDtypeStruct((T, D), F32), jax.ShapeDtypeStruct((ns, T, fs), BF16),
                 jax.ShapeDtypeStruct((ns, T, fs), BF16)]
    args = [x, g, wff, wff, wff]
    if head is not None:
        in_specs += [vec, tok]
        out_specs += [pl.BlockSpec((SUBLANES, LANES), lambda i, j: (0, 0)), vec]
        out_shape += [jax.ShapeDtypeStruct((SUBLANES, LANES), F32), jax.ShapeDtypeStruct((1, D), F32)]
        args += list(head)
    return pl.pallas_call(
        body, grid=(T // tm, ns), in_specs=in_specs, out_specs=out_specs, out_shape=out_shape,
        scratch_shapes=[pltpu.VMEM((tm, D), BF16), pltpu.VMEM((tm, D), F32)],
        compiler_params=_params("arbitrary", "arbitrary"), name=name)(*args)


def _ffn_bwd_tok(dy, x, g, a, b, wff, name):
    T, D = x.shape
    ns, fs = wff.shape[1], wff.shape[2]
    tm = _tile(T, FFN_BWD_TILE)
    rc = _tile(tm, BWD_ROWS)
    mc = _tile(tm, FFN_BWD_CHAIN, rc)

    def body(dy_ref, x_ref, g_ref, a_ref, b_ref, wg_ref, wu_ref, wd0_ref, wdn_ref,
             dx_ref, da_ref, db_ref, p_ref, hb_ref, dyh_ref, dg_ref, dh_ref, dp_ref):
        i, j = pl.program_id(0), pl.program_id(1)
        cur = dp_ref.at[j % 2]
        nxt = dp_ref.at[(j + 1) % 2]

        @pl.when((i == 0) & (j == 0))
        def _():
            dg_ref[...] = jnp.zeros_like(dg_ref)

        @pl.when(j == 0)
        def _():
            for r0 in range(0, tm, rc):
                rows = pl.ds(r0, rc)
                xh, _ = _rms_stats(x_ref[rows, :])
                hb_ref[rows, :] = (xh * g_ref[...]).astype(BF16)
                dyh_ref[rows, :] = (FFN_RES_SCALE * dy_ref[rows, :]).astype(BF16)
            dh_ref[...] = jnp.zeros_like(dh_ref)
            cur[...] = _dot_nt(dyh_ref[...], wd0_ref[...])

        def chains(with_next):
            for q0 in range(0, tm, mc):
                blk = pl.ds(q0, mc)
                for r0 in range(q0, q0 + mc, rc):
                    rows = pl.ds(r0, rc)
                    av = a_ref[rows, :].astype(F32)
                    bv = b_ref[rows, :].astype(F32)
                    dp = cur[rows, :]
                    s = jax.nn.sigmoid(av)
                    sl = av * s
                    da_ref[rows, :] = (dp * bv * (s * (1.0 + av * (1.0 - s)))).astype(BF16)
                    db_ref[rows, :] = (dp * sl).astype(BF16)
                    p_ref[rows, :] = (sl * bv).astype(BF16)
                if with_next:
                    nxt[blk, :] = _dot_nt(dyh_ref[blk, :], wdn_ref[...])
                dh_ref[blk, :] += _dot(da_ref[blk, :], wg_ref[...]) + _dot(db_ref[blk, :], wu_ref[...])

        pl.when(j < ns - 1)(functools.partial(chains, True))
        pl.when(j == ns - 1)(functools.partial(chains, False))

        @pl.when(j == ns - 1)
        def _():
            gv = g_ref[...]
            dg = jnp.zeros((1, D), F32)
            for r0 in range(0, tm, rc):
                rows = pl.ds(r0, rc)
                xh, r = _rms_stats(x_ref[rows, :])
                dh = dh_ref[rows, :]
                dx_ref[rows, :] = dy_ref[rows, :] + _rms_bwd(dh, xh, r, gv)
                dg = dg + _colsum(dh * xh)
            dg_ref[...] += dg

    def wspec(n):
        return pl.BlockSpec((None, None, fs, D), lambda i, j: (n, j, 0, 0))

    tok = pl.BlockSpec((tm, D), lambda i, j: (i, 0))
    mid = pl.BlockSpec((None, tm, fs), lambda i, j: (j, i, 0))
    vec = pl.BlockSpec((1, D), lambda i, j: (0, 0))
    return pl.pallas_call(
        body, grid=(T // tm, ns),
        in_specs=[tok, tok, vec, mid, mid, wspec(0), wspec(1),
                  pl.BlockSpec((None, None, fs, D), lambda i, j: (2, 0, 0, 0)),
                  pl.BlockSpec((None, None, fs, D), lambda i, j: (2, jnp.minimum(j + 1, ns - 1), 0, 0))],
        out_specs=[tok, mid, mid, mid, tok, tok, vec],
        out_shape=[jax.ShapeDtypeStruct((T, D), F32),
                   jax.ShapeDtypeStruct((ns, T, fs), BF16), jax.ShapeDtypeStruct((ns, T, fs), BF16),
                   jax.ShapeDtypeStruct((ns, T, fs), BF16),
                   jax.ShapeDtypeStruct((T, D), BF16), jax.ShapeDtypeStruct((T, D), BF16),
                   jax.ShapeDtypeStruct((1, D), F32)],
        scratch_shapes=[pltpu.VMEM((tm, D), F32), pltpu.VMEM((2, tm, fs), F32)],
        compiler_params=_params("arbitrary", "arbitrary"), name=name)(dy, x, g, a, b, wff, wff, wff, wff)


def _ffn_wgrad(groups, after, name):
    flat = [(l, gi) for gi, (ls, _) in enumerate(groups) for l in ls]
    ng, n = len(groups), len(flat)
    T, D = groups[0][1].shape
    ns, _, fs = flat[0][0].shape
    tm = _tile(T, TOK_TILE)

    def body(*refs):
        rhs_refs, lhs_refs, out_refs = refs[:ng], refs[ng:ng + n], refs[ng + n + 1:]

        @pl.when(pl.program_id(1) == 0)
        def _():
            for o in out_refs:
                o[...] = jnp.zeros_like(o)

        rvs = [r[...] for r in rhs_refs]
        for l, o, (_, gi) in zip(lhs_refs, out_refs, flat):
            o[...] += _dot_tn(l[...], rvs[gi])

    tok = pl.BlockSpec((tm, D), lambda j, i: (i, 0))
    mid = pl.BlockSpec((None, tm, fs), lambda j, i: (j, i, 0))
    wsp = pl.BlockSpec((None, fs, D), lambda j, i: (j, 0, 0))
    sds = jax.ShapeDtypeStruct((ns, fs, D), F32)
    return pl.pallas_call(
        body, grid=(ns, T // tm),
        in_specs=[tok] * ng + [mid] * n + [pl.BlockSpec((SUBLANES, LANES), lambda j, i: (0, 0))],
        out_specs=[wsp] * n, out_shape=[sds] * n,
        compiler_params=_params("parallel", "arbitrary"), name=name)(
            *[r for _, r in groups], *[l for l, _ in flat], after)


def _mix_in_fwd(x, g, win):
    T, D = x.shape
    ns, ws = win.shape[0], win.shape[2]
    tm = _tile(T, TOK_TILE)

    def body(x_ref, g_ref, w_ref, z_ref):
        xh, _ = _rms_stats(x_ref[...])
        hb = (xh * g_ref[...]).astype(BF16)
        for j in range(ns):
            z_ref[:, pl.ds(j * ws, ws)] = _dot(hb, w_ref[j])

    return pl.pallas_call(
        body, grid=(T // tm,),
        in_specs=[pl.BlockSpec((tm, D), lambda i: (i, 0)), pl.BlockSpec((1, D), lambda i: (0, 0)),
                  pl.BlockSpec((ns, D, ws), lambda i: (0, 0, 0), pipeline_mode=pl.Buffered(1))],
        out_specs=pl.BlockSpec((tm, ns * ws), lambda i: (i, 0)),
        out_shape=jax.ShapeDtypeStruct((T, ns * ws), F32),
        compiler_params=_params("parallel"), name="mix_in_fwd")(x, g, win)


def _tap_sum(buf, w_ref, ntaps, first_row, r0, rows, flip):
    acc = None
    for k in range(ntaps):
        off = (ntaps - 1 - k) if flip else k
        t = buf[pl.ds(first_row + r0 + off, rows), :] * w_ref[pl.ds(k, 1), :]
        acc = t if acc is None else acc + t
    return acc


def _shift_copies(buf, sh, rows):
    for r in range(1, SUBLANES):
        sh[r - 1, pl.ds(0, rows), :] = buf[pl.ds(r, rows), :]


def _tap_rows(buf, sh, off, r0, rows):
    r = off % SUBLANES
    if r == 0:
        return buf[pl.ds(off + r0, rows), :]
    return sh[r - 1, pl.ds(off - r + r0, rows), :]


def _tap_sum_tiles(buf, sh, w_ref, ntaps, first_row, r0, rows, flip):
    acc = None
    for k in range(ntaps):
        off = first_row + ((ntaps - 1 - k) if flip else k)
        t = _tap_rows(buf, sh, off, r0, rows) * w_ref[pl.ds(k, 1), :]
        acc = t if acc is None else acc + t
    return acc


def _conv_fwd(z, w, bias, lng, lnb):
    T = z.shape[0]
    K, C = w.shape
    tm = _tile(T, CONV_TILE, ROW_CHUNK)
    rc = min(ROW_CHUNK, tm)
    srows = tm + CONV_HALO - SUBLANES

    def body(cv_ref, cg_ref, w_ref, b_ref, g_ref, bb_ref, u_ref, u1_ref, buf, sh):
        @pl.when(pl.program_id(0) == 0)
        def _():
            buf[pl.ds(0, CONV_HALO), :] = jnp.zeros((CONV_HALO, C), F32)

        buf[pl.ds(CONV_HALO, tm), :] = cv_ref[...] * jax.nn.sigmoid(cg_ref[...])
        _shift_copies(buf, sh, srows)
        for r0 in range(0, tm, rc):
            u1 = _tap_sum_tiles(buf, sh, w_ref, K, CONV_HALO - (K - 1), r0, rc, False) + b_ref[...]
            u1_ref[pl.ds(r0, rc), :] = u1
            xc = u1 - jnp.mean(u1, axis=-1, keepdims=True)
            xh = xc * lax.rsqrt(jnp.mean(xc * xc, axis=-1, keepdims=True) + LN_EPS)
            u2 = xh * g_ref[...] + bb_ref[...]
            u_ref[pl.ds(r0, rc), :] = (u2 * jax.nn.sigmoid(u2)).astype(BF16)
        buf[pl.ds(0, CONV_HALO), :] = buf[pl.ds(tm, CONV_HALO), :]

    vec = pl.BlockSpec((1, C), lambda i: (0, 0))
    return pl.pallas_call(
        body, grid=(T // tm,),
        in_specs=[pl.BlockSpec((tm, C), lambda i: (i, 0)), pl.BlockSpec((tm, C), lambda i: (i, 1)),
                  pl.BlockSpec((K, C), lambda i: (0, 0)), vec, vec, vec],
        out_specs=[pl.BlockSpec((tm, C), lambda i: (i, 0)), pl.BlockSpec((tm, C), lambda i: (i, 0))],
        out_shape=[jax.ShapeDtypeStruct((T, C), BF16), jax.ShapeDtypeStruct((T, C), F32)],
        scratch_shapes=[pltpu.VMEM((CONV_HALO + tm, C), F32), pltpu.VMEM((SUBLANES - 1, srows, C), F32)],
        compiler_params=_params("arbitrary"), name="conv_fwd")(z, z, w, bias, lng, lnb)


def _conv_bwd(dcat, u1, z, w, lng, lnb):
    T = z.shape[0]
    K, C = w.shape
    tm = _tile(T, CONV_TILE, ROW_CHUNK)
    rc = min(ROW_CHUNK, tm)
    nI = T // tm
    hb = tm // CONV_HALO
    srows = ((K + 4 + SUBLANES - 1) // SUBLANES) * SUBLANES
    shrows = tm + CONV_HALO - SUBLANES

    def body(du_ref, u1_ref, cv_ref, cg_ref, cvp_ref, cgp_ref, w_ref, g_ref, bb_ref,
             dz_ref, st_ref, u0buf, d1buf, ush, dsh):
        i = pl.program_id(0)
        ti = nI - 1 - i

        @pl.when(i == 0)
        def _():
            st_ref[...] = jnp.zeros_like(st_ref)
            d1buf[pl.ds(tm, CONV_HALO), :] = jnp.zeros((CONV_HALO, C), F32)

        prev = cvp_ref[...] * jax.nn.sigmoid(cgp_ref[...])
        u0buf[pl.ds(0, CONV_HALO), :] = jnp.where(ti == 0, 0.0, prev)
        u0buf[pl.ds(CONV_HALO, tm), :] = cv_ref[...] * jax.nn.sigmoid(cg_ref[...])

        gv = g_ref[...]
        dbias = jnp.zeros((1, C), F32)
        dgain = jnp.zeros((1, C), F32)
        dlnb = jnp.zeros((1, C), F32)
        for r0 in range(0, tm, rc):
            u1 = u1_ref[pl.ds(r0, rc), :]
            xc = u1 - jnp.mean(u1, axis=-1, keepdims=True)
            rstd = lax.rsqrt(jnp.mean(xc * xc, axis=-1, keepdims=True) + LN_EPS)
            xh = xc * rstd
            u2 = xh * gv + bb_ref[...]
            s = jax.nn.sigmoid(u2)
            du2 = du_ref[pl.ds(r0, rc), :] * (s * (1.0 + u2 * (1.0 - s)))
            dgain = dgain + _colsum(du2 * xh)
            dlnb = dlnb + _colsum(du2)
            dxh = du2 * gv
            du1 = rstd * (dxh - jnp.mean(dxh, axis=-1, keepdims=True)
                          - xh * jnp.mean(dxh * xh, axis=-1, keepdims=True))
            dbias = dbias + _colsum(du1)
            d1buf[pl.ds(r0, rc), :] = du1
        st_ref[pl.ds(K + 1, 1), :] += dbias
        st_ref[pl.ds(K + 2, 1), :] += dgain
        st_ref[pl.ds(K + 3, 1), :] += dlnb

        _shift_copies(u0buf, ush, shrows)
        _shift_copies(d1buf, dsh, shrows)
        for k in range(K):
            acc = jnp.zeros((SUBLANES, C), F32)
            for r0 in range(0, tm, rc):
                prod = d1buf[pl.ds(r0, rc), :] * _tap_rows(u0buf, ush, CONV_HALO - (K - 1) + k, r0, rc)
                acc = acc + jnp.sum(prod.reshape(rc // SUBLANES, SUBLANES, C), axis=0)
            st_ref[pl.ds(k, 1), :] += _colsum(acc)

        for r0 in range(0, tm, rc):
            du0 = _tap_sum_tiles(d1buf, dsh, w_ref, K, 0, r0, rc, True)
            cv = cv_ref[pl.ds(r0, rc), :]
            sg = jax.nn.sigmoid(cg_ref[pl.ds(r0, rc), :])
            dz_ref[pl.ds(r0, rc), pl.ds(0, C)] = (du0 * sg).astype(BF16)
            dz_ref[pl.ds(r0, rc), pl.ds(C, C)] = (du0 * cv * sg * (1.0 - sg)).astype(BF16)
        d1buf[pl.ds(tm, CONV_HALO), :] = d1buf[pl.ds(0, CONV_HALO), :]

    def rev(col):
        return lambda i: (nI - 1 - i, col)

    def rev_prev(col):
        return lambda i: (jnp.maximum((nI - 1 - i) * hb - 1, 0), col)

    vec = pl.BlockSpec((1, C), lambda i: (0, 0))
    return pl.pallas_call(
        body, grid=(nI,),
        in_specs=[pl.BlockSpec((tm, C), rev(0)), pl.BlockSpec((tm, C), rev(0)),
                  pl.BlockSpec((tm, C), rev(0)), pl.BlockSpec((tm, C), rev(1)),
                  pl.BlockSpec((CONV_HALO, C), rev_prev(0)), pl.BlockSpec((CONV_HALO, C), rev_prev(1)),
                  pl.BlockSpec((K, C), lambda i: (0, 0)), vec, vec],
        out_specs=[pl.BlockSpec((tm, 2 * C), rev(0)), pl.BlockSpec((srows, C), lambda i: (0, 0))],
        out_shape=[jax.ShapeDtypeStruct((T, 2 * C), BF16), jax.ShapeDtypeStruct((srows, C), F32)],
        scratch_shapes=[pltpu.VMEM((CONV_HALO + tm, C), F32), pltpu.VMEM((tm + CONV_HALO, C), F32),
                        pltpu.VMEM((SUBLANES - 1, shrows, C), F32), pltpu.VMEM((SUBLANES - 1, shrows, C), F32)],
        compiler_params=_params("arbitrary"), name="conv_bwd")(dcat, u1, z, z, z, z, w, lng, lnb)


def _softplus(v):
    return jnp.maximum(v, 0.0) + jnp.log(1.0 + jnp.exp(-jnp.abs(v)))


def _gelu(v):
    c = math.sqrt(2.0 / math.pi)
    t = jnp.tanh(c * (v + 0.044715 * v * v * v))
    gl = 0.5 * v * (1.0 + t)
    dgl = 0.5 * (1.0 + t) + 0.5 * v * (1.0 - t * t) * c * (1.0 + 3.0 * 0.044715 * v * v)
    return gl, dgl


def _lru_gates(xr, wa, ba, wx, bx, lam):
    xb = xr.astype(BF16)
    r = jax.nn.sigmoid(_dot(xb, wa) + ba)
    ig = jax.nn.sigmoid(_dot(xb, wx) + bx)
    sp = _softplus(-lam)
    log_a = -LRU_C * r * sp
    a = jnp.exp(log_a)
    y = 2.0 * log_a
    series = -(y * (1.0 + y * (0.5 + y * (1.0 / 6.0 + y * (1.0 / 24.0)))))
    mult = jnp.sqrt(jnp.where(y > -0.02, series, 1.0 - jnp.exp(y)))
    return a, mult, r, ig, sp


def _scan_tile(a_s, b_s, h_s, p_s, carry, seg, reverse):
    hl = [jnp.zeros((SUBLANES, LANES), F32)] * LRU_GROUPS
    pr = [jnp.ones((SUBLANES, LANES), F32)] * LRU_GROUPS
    for n in range(seg):
        for g in range(LRU_GROUPS):
            rows = pl.ds(g * SUBLANES * seg + ((seg - 1 - n) if reverse else n), SUBLANES, stride=seg)
            av = a_s[rows, :]
            hl[g] = av * hl[g] + b_s[rows, :]
            pr[g] = av * pr[g]
            h_s[rows, :] = hl[g]
            p_s[rows, :] = pr[g]
    nseg = SUBLANES * LRU_GROUPS
    cs = [None] * nseg
    c = carry
    for s in (range(nseg - 1, -1, -1) if reverse else range(nseg)):
        g, r = divmod(s, SUBLANES)
        cs[s] = c
        c = hl[g][r:r + 1, :] + pr[g][r:r + 1, :] * c
    return cs, c


def _lru_fwd(z, col0, w4, b4, wa, ba, wx, bx, lam):
    T = z.shape[0]
    K4, W = w4.shape
    nC = W // LANES
    tm = _tile(T, LRU_TILE, SUBLANES * SUBLANES * LRU_GROUPS)
    seg = tm // (SUBLANES * LRU_GROUPS)
    cx, cg = col0 // LANES, (col0 + W) // LANES

    def body(rx_ref, rg_ref, w4_ref, b4_ref, wa_ref, ba_ref, wx_ref, bx_ref, lam_ref,
             yr_ref, hs_ref, xbuf, a_s, b_s, h_s, p_s, hc):
        @pl.when(pl.program_id(1) == 0)
        def _():
            xbuf[pl.ds(0, LRU_HALO), :] = jnp.zeros((LRU_HALO, LANES), F32)
            hc[...] = jnp.zeros_like(hc)

        xbuf[pl.ds(LRU_HALO, tm), :] = rx_ref[...]
        xr = _tap_sum(xbuf, w4_ref, K4, LRU_HALO - (K4 - 1), 0, tm, False) + b4_ref[...]
        a, mult, _, ig, _ = _lru_gates(xr, wa_ref[...], ba_ref[...], wx_ref[...], bx_ref[...], lam_ref[...])
        a_s[...] = a
        b_s[...] = mult * ig * xr
        cs, cout = _scan_tile(a_s, b_s, h_s, p_s, hc[pl.ds(0, 1), :], seg, False)
        hc[pl.ds(0, 1), :] = cout
        for s in range(SUBLANES * LRU_GROUPS):
            rows = pl.ds(s * seg, seg)
            h = h_s[rows, :] + p_s[rows, :] * cs[s]
            hs_ref[rows, :] = h
            gl, _ = _gelu(rg_ref[rows, :])
            yr_ref[rows, :] = (h * gl).astype(BF16)
        xbuf[pl.ds(0, LRU_HALO), :] = xbuf[pl.ds(tm, LRU_HALO), :]

    vec = pl.BlockSpec((1, LANES), lambda c, i: (0, c))
    mat = pl.BlockSpec((None, LANES, LANES), lambda c, i: (c, 0, 0))
    return pl.pallas_call(
        body, grid=(nC, T // tm),
        in_specs=[pl.BlockSpec((tm, LANES), lambda c, i: (i, cx + c)),
                  pl.BlockSpec((tm, LANES), lambda c, i: (i, cg + c)),
                  pl.BlockSpec((K4, LANES), lambda c, i: (0, c)), vec, mat, vec, mat, vec, vec],
        out_specs=[pl.BlockSpec((tm, LANES), lambda c, i: (i, c)), pl.BlockSpec((tm, LANES), lambda c, i: (i, c))],
        out_shape=[jax.ShapeDtypeStruct((T, W), BF16), jax.ShapeDtypeStruct((T, W), F32)],
        scratch_shapes=[pltpu.VMEM((LRU_HALO + tm, LANES), F32)] + [pltpu.VMEM((tm, LANES), F32)] * 4
        + [pltpu.VMEM((SUBLANES, LANES), F32)],
        compiler_params=_params("parallel", "arbitrary"), name="lru_fwd")(z, z, w4, b4, wa, ba, wx, bx, lam)


def _lru_bwd(dcat, dcol0, hs, z, col0, w4, b4, wa, ba, wx, bx, lam):
    T = z.shape[0]
    K4, W = w4.shape
    assert K4 + 4 == SUBLANES
    nC = W // LANES
    tm = _tile(T, LRU_TILE, SUBLANES * SUBLANES * LRU_GROUPS)
    seg = tm // (SUBLANES * LRU_GROUPS)
    nI = T // tm
    hb = tm // LRU_HALO
    cx, cg, cd = col0 // LANES, (col0 + W) // LANES, dcol0 // LANES

    def body(dyr_ref, hs_ref, hsp_ref, rx_ref, rxp_ref, rg_ref, w4_ref, b4_ref, wa_ref, ba_ref, wx_ref, bx_ref,
             lam_ref, dzx_ref, dzg_ref, st_ref, dwa_ref, dwx_ref, xbuf, hbuf, abuf, a_s, b_s, h_s, p_s, dbuf, gc, anc):
        i = pl.program_id(1)
        ti = nI - 1 - i

        @pl.when(i == 0)
        def _():
            st_ref[...] = jnp.zeros_like(st_ref)
            dwa_ref[...] = jnp.zeros_like(dwa_ref)
            dwx_ref[...] = jnp.zeros_like(dwx_ref)
            gc[...] = jnp.zeros_like(gc)
            anc[...] = jnp.zeros_like(anc)
            dbuf[pl.ds(tm, LRU_HALO), :] = jnp.zeros((LRU_HALO, LANES), F32)

        xbuf[pl.ds(0, LRU_HALO), :] = jnp.where(ti == 0, 0.0, rxp_ref[...])
        xbuf[pl.ds(LRU_HALO, tm), :] = rx_ref[...]
        hbuf[pl.ds(0, LRU_HALO), :] = jnp.where(ti == 0, 0.0, hsp_ref[...])
        hbuf[pl.ds(LRU_HALO, tm), :] = hs_ref[...]

        wa, wx = wa_ref[...], wx_ref[...]
        lam_v = lam_ref[...]
        xr = _tap_sum(xbuf, w4_ref, K4, LRU_HALO - (K4 - 1), 0, tm, False) + b4_ref[...]
        a, mult, r, ig, sp = _lru_gates(xr, wa, ba_ref[...], wx, bx_ref[...], lam_v)

        dyr = dyr_ref[...]
        gl, dgl = _gelu(rg_ref[...])
        dzg_ref[...] = (dyr * hs_ref[...] * dgl).astype(BF16)

        abuf[pl.ds(0, tm), :] = a
        abuf[pl.ds(tm, LRU_HALO), :] = anc[...]
        a_s[...] = abuf[pl.ds(1, tm), :]
        b_s[...] = dyr * gl
        cs, cout = _scan_tile(a_s, b_s, h_s, p_s, gc[pl.ds(0, 1), :], seg, True)
        gc[pl.ds(0, 1), :] = cout
        anc[pl.ds(0, 1), :] = a[0:1, :]
        for s in range(SUBLANES * LRU_GROUPS):
            rows = pl.ds(s * seg, seg)
            b_s[rows, :] = h_s[rows, :] + p_s[rows, :] * cs[s]
        g = b_s[...]

        d_a = g * hbuf[pl.ds(LRU_HALO - 1, tm), :]
        gx_ = g * xr
        d_log_a = d_a * a - (gx_ * ig) * (a * a / mult)
        dga = (d_log_a * (-LRU_C * sp)) * r * (1.0 - r)
        dgx = (gx_ * mult) * ig * (1.0 - ig)
        dga_b, dgx_b = dga.astype(BF16), dgx.astype(BF16)
        dxr = g * mult * ig + _dot_nt(dga_b, wa) + _dot_nt(dgx_b, wx)
        xb = xr.astype(BF16)
        dwa_ref[...] += _dot_tn(xb, dga_b)
        dwx_ref[...] += _dot_tn(xb, dgx_b)
        st_ref[pl.ds(K4, 1), :] += _colsum(dxr)
        st_ref[pl.ds(K4 + 1, 1), :] += _colsum(dga)
        st_ref[pl.ds(K4 + 2, 1), :] += _colsum(dgx)
        st_ref[pl.ds(K4 + 3, 1), :] += _colsum(d_log_a * (-LRU_C * r)) * (-jax.nn.sigmoid(-lam_v))

        dbuf[pl.ds(0, tm), :] = dxr
        for k in range(K4):
            st_ref[pl.ds(k, 1), :] += _colsum(dxr * xbuf[pl.ds(LRU_HALO - (K4 - 1) + k, tm), :])
        dzx_ref[...] = _tap_sum(dbuf, w4_ref, K4, 0, 0, tm, True).astype(BF16)
        dbuf[pl.ds(tm, LRU_HALO), :] = dbuf[pl.ds(0, LRU_HALO), :]

    def rev(col):
        return lambda c, i: (nI - 1 - i, col + c)

    def rev_prev(col):
        return lambda c, i: (jnp.maximum((nI - 1 - i) * hb - 1, 0), col + c)

    vec = pl.BlockSpec((1, LANES), lambda c, i: (0, c))
    mat = pl.BlockSpec((None, LANES, LANES), lambda c, i: (c, 0, 0))
    big = pltpu.VMEM((tm, LANES), F32)
    halo = pltpu.VMEM((tm + LRU_HALO, LANES), F32)
    return pl.pallas_call(
        body, grid=(nC, nI),
        in_specs=[pl.BlockSpec((tm, LANES), rev(cd)),
                  pl.BlockSpec((tm, LANES), rev(0)), pl.BlockSpec((LRU_HALO, LANES), rev_prev(0)),
                  pl.BlockSpec((tm, LANES), rev(cx)), pl.BlockSpec((LRU_HALO, LANES), rev_prev(cx)),
                  pl.BlockSpec((tm, LANES), rev(cg)),
                  pl.BlockSpec((K4, LANES), lambda c, i: (0, c)), vec, mat, vec, mat, vec, vec],
        out_specs=[pl.BlockSpec((tm, LANES), rev(0)), pl.BlockSpec((tm, LANES), rev(0)),
                   pl.BlockSpec((SUBLANES, LANES), lambda c, i: (0, c)), mat, mat],
        out_shape=[jax.ShapeDtypeStruct((T, W), BF16), jax.ShapeDtypeStruct((T, W), BF16),
                   jax.ShapeDtypeStruct((SUBLANES, W), F32),
                   jax.ShapeDtypeStruct((nC, LANES, LANES), F32), jax.ShapeDtypeStruct((nC, LANES, LANES), F32)],
        scratch_shapes=[halo, halo, halo, big, big, big, big, halo,
                        pltpu.VMEM((SUBLANES, LANES), F32), pltpu.VMEM((SUBLANES, LANES), F32)],
        compiler_params=_params("parallel", "arbitrary"), name="lru_bwd")(
            dcat, hs, hs, z, z, z, w4, b4, wa, ba, wx, bx, lam)


def _mix_out_fwd(x, u, yr, wout):
    T, D = x.shape
    C, W = u.shape[1], yr.shape[1]
    tm = _tile(T, TOK_TILE)

    def body(x_ref, u_ref, yr_ref, w_ref, y_ref):
        y_ref[...] = (x_ref[...] + _dot(u_ref[...], w_ref[pl.ds(0, C), :])
                      + _dot(yr_ref[...], w_ref[pl.ds(C, W), :]))

    return pl.pallas_call(
        body, grid=(T // tm,),
        in_specs=[pl.BlockSpec((tm, D), lambda i: (i, 0)), pl.BlockSpec((tm, C), lambda i: (i, 0)),
                  pl.BlockSpec((tm, W), lambda i: (i, 0)),
                  pl.BlockSpec((C + W, D), lambda i: (0, 0), pipeline_mode=pl.Buffered(1))],
        out_specs=pl.BlockSpec((tm, D), lambda i: (i, 0)),
        out_shape=jax.ShapeDtypeStruct((T, D), F32),
        compiler_params=_params("parallel"), name="mix_out_fwd")(x, u, yr, wout)


def _mix_out_bwd(dy, u, yr, wout):
    T, D = dy.shape
    C, W = u.shape[1], yr.shape[1]
    tm = _tile(T, BWD_TILE)

    def body(dy_ref, u_ref, yr_ref, w_ref, dcat_ref, dw_ref):
        @pl.when(pl.program_id(0) == 0)
        def _():
            dw_ref[...] = jnp.zeros_like(dw_ref)

        dyb = dy_ref[...].astype(BF16)
        dcat_ref[...] = _dot_nt(dyb, w_ref[...])
        dw_ref[pl.ds(0, C), :] += _dot_tn(u_ref[...], dyb)
        dw_ref[pl.ds(C, W), :] += _dot_tn(yr_ref[...], dyb)

    return pl.pallas_call(
        body, grid=(T // tm,),
        in_specs=[pl.BlockSpec((tm, D), lambda i: (i, 0)), pl.BlockSpec((tm, C), lambda i: (i, 0)),
                  pl.BlockSpec((tm, W), lambda i: (i, 0)),
                  pl.BlockSpec((C + W, D), lambda i: (0, 0), pipeline_mode=pl.Buffered(1))],
        out_specs=[pl.BlockSpec((tm, C + W), lambda i: (i, 0)), pl.BlockSpec((C + W, D), lambda i: (0, 0))],
        out_shape=[jax.ShapeDtypeStruct((T, C + W), F32), jax.ShapeDtypeStruct((C + W, D), F32)],
        compiler_params=_params("arbitrary"), name="mix_out_bwd")(dy, u, yr, wout)


def _mix_in_bwd(dzc, dzx, dzg, x, dy, g, win):
    T, D = x.shape
    ns, ws = win.shape[0], win.shape[2]
    tm = _tile(T, BWD_TILE)
    parts = []
    for j in range(ns):
        lo = j * ws
        if lo < dzc.shape[1]:
            parts.append((0, lo))
        elif lo < dzc.shape[1] + dzx.shape[1]:
            parts.append((1, lo - dzc.shape[1]))
        else:
            parts.append((2, lo - dzc.shape[1] - dzx.shape[1]))

    def body(dzc_ref, dzx_ref, dzg_ref, x_ref, dy_ref, g_ref, w_ref, dx_ref, dw_ref, dg_ref):
        @pl.when(pl.program_id(0) == 0)
        def _():
            dw_ref[...] = jnp.zeros_like(dw_ref)
            dg_ref[...] = jnp.zeros_like(dg_ref)

        xh, r = _rms_stats(x_ref[...])
        gv = g_ref[...]
        hb = (xh * gv).astype(BF16)
        srcs = (dzc_ref, dzx_ref, dzg_ref)
        dh = jnp.zeros((tm, D), F32)
        for j, (si, off) in enumerate(parts):
            dzj = srcs[si][:, pl.ds(off, ws)]
            dh = dh + _dot_nt(dzj, w_ref[j])
            dw_ref[j] += _dot_tn(hb, dzj)
        dx_ref[...] = dy_ref[...] + _rms_bwd(dh, xh, r, gv)
        dg_ref[...] += _colsum(dh * xh)

    def tok(n):
        return pl.BlockSpec((tm, n), lambda i: (i, 0))

    vec = pl.BlockSpec((1, D), lambda i: (0, 0))
    return pl.pallas_call(
        body, grid=(T // tm,),
        in_specs=[tok(dzc.shape[1]), tok(dzx.shape[1]), tok(dzg.shape[1]), tok(D), tok(D), vec,
                  pl.BlockSpec((ns, D, ws), lambda i: (0, 0, 0), pipeline_mode=pl.Buffered(1))],
        out_specs=[tok(D), pl.BlockSpec((ns, D, ws), lambda i: (0, 0, 0)), vec],
        out_shape=[jax.ShapeDtypeStruct((T, D), F32), jax.ShapeDtypeStruct((ns, D, ws), F32),
                   jax.ShapeDtypeStruct((1, D), F32)],
        compiler_params=_params("arbitrary"), name="mix_in_bwd")(dzc, dzx, dzg, x, dy, g, win)


def _adamw(w, g, m, v, name):
    R, Cc = w.shape
    tr = _tile(R, max(SUBLANES, (1 << 19) // Cc))
    c1 = 1.0 - ADAM_B1 ** ADAM_STEP
    c2 = 1.0 - ADAM_B2 ** ADAM_STEP

    def body(w_ref, g_ref, m_ref, v_ref, d_ref, nm_ref, nv_ref):
        gv = g_ref[...]
        nm = ADAM_B1 * m_ref[...] + (1.0 - ADAM_B1) * gv
        nv = ADAM_B2 * v_ref[...] + (1.0 - ADAM_B2) * (gv * gv)
        nm_ref[...] = nm
        nv_ref[...] = nv
        d_ref[...] = -ADAM_LR * ((nm / c1) / (jnp.sqrt(nv / c2) + ADAM_EPS) + ADAM_WD * w_ref[...])

    blk = pl.BlockSpec((tr, Cc), lambda i: (i, 0))
    sds = jax.ShapeDtypeStruct((R, Cc), F32)
    return pl.pallas_call(
        body, grid=(R // tr,), in_specs=[blk] * 4, out_specs=[blk] * 3, out_shape=[sds] * 3,
        compiler_params=_params("parallel"), name=name)(w, g, m, v)


def _here():
    return lax.axis_index("x"), lax.axis_index("y"), lax.axis_index("c")


def _chip_at(x, y, m):
    return x ^ (m >> 1), y ^ (m & 1)


ANY = pl.BlockSpec(memory_space=pl.ANY)


def _place_cast(srcs, idx, dtype, name):
    n = len(srcs)
    R, Cc = srcs[0].shape
    tr = _tile(R, max(16, (1 << 18) // Cc), 16)

    def body(i_ref, *refs):
        o_ref = refs[n]
        for k in range(n):
            o_ref[k] = refs[k][...].astype(dtype)

    blk = pl.BlockSpec((tr, Cc), lambda i, s: (i, 0))
    return pl.pallas_call(
        body,
        grid_spec=pltpu.PrefetchScalarGridSpec(
            num_scalar_prefetch=1, grid=(R // tr,), in_specs=[blk] * n,
            out_specs=pl.BlockSpec((n, None, tr, Cc), lambda i, s: (0, s[1], i, 0))),
        out_shape=jax.ShapeDtypeStruct((n, N_CHIPS, R, Cc), dtype),
        compiler_params=_params("parallel"), name=name)(idx, *srcs)


def _gather_weights(lands):
    n = len(lands)

    def body(*refs):
        outs = refs[n:2 * n]
        send1, recv1, send2, recv2 = refs[2 * n:]
        x, y, c = _here()
        own = 2 * x + y

        def half(ref, chip, cc):
            rh = ref.shape[-2] // 2
            lead = (slice(None),) * (len(ref.shape) - 3)
            return ref.at[lead + (chip, pl.ds(cc * rh, rh), slice(None))]

        first = []
        for k in range(n):
            for m in (1, 2, 3):
                px, py = _chip_at(x, y, m)
                cp = pltpu.make_async_remote_copy(
                    src_ref=half(outs[k], own, c), dst_ref=half(outs[k], own, c),
                    send_sem=send1.at[k, m - 1], recv_sem=recv1.at[k, m - 1],
                    device_id=(px, py, c), device_id_type=MESH)
                cp.start()
                first.append(cp)

        passed = []
        for k in range(n):
            for m in (1, 2, 3):
                px, py = _chip_at(x, y, m)
                peer = 2 * px + py
                got = half(outs[k], peer, c)
                pltpu.make_async_remote_copy(
                    src_ref=got, dst_ref=got, send_sem=send1.at[k, m - 1], recv_sem=recv1.at[k, m - 1],
                    device_id=(px, py, c), device_id_type=MESH).wait_recv()
                cp = pltpu.make_async_remote_copy(
                    src_ref=got, dst_ref=got, send_sem=send2.at[k, m - 1], recv_sem=recv2.at[k, m - 1],
                    device_id=(x, y, 1 - c), device_id_type=MESH)
                cp.start()
                passed.append(cp)

        for k in range(n):
            for m in (1, 2, 3):
                px, py = _chip_at(x, y, m)
                other = half(outs[k], 2 * px + py, 1 - c)
                pltpu.make_async_remote_copy(
                    src_ref=other, dst_ref=other, send_sem=send2.at[k, m - 1], recv_sem=recv2.at[k, m - 1],
                    device_id=(x, y, 1 - c), device_id_type=MESH).wait_recv()
        for cp in first + passed:
            cp.wait_send()

    return pl.pallas_call(
        body, in_specs=[ANY] * n, out_specs=[ANY] * n,
        out_shape=[jax.ShapeDtypeStruct(a.shape, a.dtype) for a in lands],
        input_output_aliases={k: k for k in range(n)},
        scratch_shapes=[pltpu.SemaphoreType.DMA((n, 3)), pltpu.SemaphoreType.DMA((n, 3)),
                        pltpu.SemaphoreType.DMA((n, 3)), pltpu.SemaphoreType.DMA((n, 3))],
        name="gather_weights")(*lands)


HBM = pl.BlockSpec(memory_space=pltpu.HBM)
SEM = pl.BlockSpec(memory_space=pltpu.SEMAPHORE)
EFFECT = pltpu.SideEffectType.DATAFLOW_SIDE_EFFECTING


def _in_hbm(a):
    return pltpu.with_memory_space_constraint(a, pltpu.HBM)


def _gather_copies(land_refs, send, recv):
    x, y, c = _here()
    own = 2 * x + y
    cps = []
    for k in range(len(land_refs)):
        lead = (slice(None),) * (len(land_refs[k].shape) - 3)
        mine = land_refs[k].at[lead + (own,)]
        for m in (1, 2, 3):
            px, py = _chip_at(x, y, m)
            cps.append(pltpu.make_async_remote_copy(
                src_ref=mine, dst_ref=mine, send_sem=send.at[3 * k + m - 1], recv_sem=recv.at[3 * k + m - 1],
                device_id=(px, py, c), device_id_type=MESH))
    return cps


def _gather_start(lands, after, name):
    n = len(lands)

    def body(*refs):
        lz = refs[:n]
        send, recv = refs[n + 1], refs[n + 2]
        token = refs[-1]
        for cp in _gather_copies(lz, send, recv):
            cp.start()
        token[...] = jnp.zeros_like(token)

    hbm = [pltpu.HBM(a.shape, a.dtype) for a in lands]
    outs = pl.pallas_call(
        body, name=name,
        in_specs=[HBM] * n + [ANY],
        out_specs=[SEM, SEM] + [HBM] * n + [pl.BlockSpec(memory_space=pltpu.VMEM)],
        out_shape=[pltpu.SemaphoreType.DMA((3 * n,)), pltpu.SemaphoreType.DMA((3 * n,))] + hbm
        + [jax.ShapeDtypeStruct((SUBLANES, LANES), F32)],
        input_output_aliases={k: 2 + k for k in range(n)},
        compiler_params=pltpu.CompilerParams(has_side_effects=EFFECT),
    )(*[_in_hbm(a) for a in lands], after)
    return outs[0], outs[1], outs[2:2 + n], outs[-1]


def _gather_wait(send, recv, lands, after, name):
    n = len(lands)

    def body(*refs):
        lz = refs[:n]
        send_r, recv_r = refs[n], refs[n + 1]
        for cp in _gather_copies(lz, send_r, recv_r):
            cp.wait_send()
            cp.wait_recv()

    hbm = [pltpu.HBM(a.shape, a.dtype) for a in lands]
    return pl.pallas_call(
        body, name=name,
        in_specs=[HBM] * n + [SEM, SEM, ANY],
        out_specs=[HBM] * n, out_shape=hbm,
        input_output_aliases={k: k for k in range(n)},
        compiler_params=pltpu.CompilerParams(has_side_effects=EFFECT),
    )(*lands, send, recv, after)


def _exchange_copies(part_refs, slot_refs, send, recv):
    x, y, c = _here()
    cps = []
    for k in range(len(part_refs)):
        for m in (1, 2, 3):
            px, py = _chip_at(x, y, m)
            cps.append(pltpu.make_async_remote_copy(
                src_ref=part_refs[k].at[2 * px + py], dst_ref=slot_refs[k].at[m - 1],
                send_sem=send.at[3 * k + m - 1], recv_sem=recv.at[3 * k + m - 1],
                device_id=(px, py, c), device_id_type=MESH))
    return cps


def _exchange_start(parts, name):
    n = len(parts)
    lands = [lax.empty((N_CHIPS - 1,) + p.shape[1:], p.dtype) for p in parts]

    def body(*refs):
        ins, lz = refs[:n], refs[n:2 * n]
        send, recv = refs[2 * n], refs[2 * n + 1]
        token = refs[-1]
        for cp in _exchange_copies(ins, lz, send, recv):
            cp.start()
        token[...] = jnp.zeros_like(token)

    hbm = [pltpu.HBM(a.shape, a.dtype) for a in list(parts) + lands]
    outs = pl.pallas_call(
        body, name=name,
        in_specs=[HBM] * (2 * n),
        out_specs=[SEM, SEM] + [HBM] * (2 * n) + [pl.BlockSpec(memory_space=pltpu.VMEM)],
        out_shape=[pltpu.SemaphoreType.DMA((3 * n,)), pltpu.SemaphoreType.DMA((3 * n,))] + hbm
        + [jax.ShapeDtypeStruct((SUBLANES, LANES), F32)],
        input_output_aliases={k: 2 + k for k in range(2 * n)},
        compiler_params=pltpu.CompilerParams(has_side_effects=EFFECT),
    )(*[_in_hbm(a) for a in parts], *[_in_hbm(a) for a in lands])
    return outs[0], outs[1], outs[2:2 + n], outs[2 + n:2 + 2 * n], outs[-1]


def _exchange_wait(send, recv, parts, lands, after, name):
    n = len(parts)

    def body(*refs):
        ins, lz = refs[:n], refs[n:2 * n]
        send_r, recv_r = refs[2 * n], refs[2 * n + 1]
        for cp in _exchange_copies(ins, lz, send_r, recv_r):
            cp.wait_send()
            cp.wait_recv()

    hbm = [pltpu.HBM(a.shape, a.dtype) for a in list(parts) + list(lands)]
    outs = pl.pallas_call(
        body, name=name,
        in_specs=[HBM] * (2 * n) + [SEM, SEM, ANY],
        out_specs=[HBM] * (2 * n), out_shape=hbm,
        input_output_aliases={k: k for k in range(2 * n)},
        compiler_params=pltpu.CompilerParams(has_side_effects=EFFECT),
    )(*parts, *lands, send, recv, after)
    return outs[:n], outs[n:]


def _swap_halves_out(grads, name):
    n = len(grads)
    out_shapes = [jax.ShapeDtypeStruct((g.shape[0], g.shape[1] // 2, g.shape[2]), g.dtype) for g in grads]

    def body(*refs):
        ins, outs = refs[:n], refs[n:2 * n]
        send, recv = refs[2 * n:]
        x, y, c = _here()
        cps = []
        for k in range(n):
            rh = ins[k].shape[1] // 2
            cp = pltpu.make_async_remote_copy(
                src_ref=ins[k].at[:, pl.ds((1 - c) * rh, rh), :], dst_ref=outs[k],
                send_sem=send.at[k], recv_sem=recv.at[k], device_id=(x, y, 1 - c), device_id_type=MESH)
            cp.start()
            cps.append(cp)
        for cp in cps:
            cp.wait()

    return pl.pallas_call(
        body, in_specs=[ANY] * n, out_specs=[ANY] * n, out_shape=out_shapes,
        scratch_shapes=[pltpu.SemaphoreType.DMA((n,)), pltpu.SemaphoreType.DMA((n,))],
        name=name)(*grads)


def _swap_copies(grad_refs, land_refs, send, recv):
    x, y, c = _here()
    cps = []
    for k in range(len(grad_refs)):
        rh = grad_refs[k].shape[1] // 2
        cps.append(pltpu.make_async_remote_copy(
            src_ref=grad_refs[k].at[:, pl.ds((1 - c) * rh, rh), :], dst_ref=land_refs[k],
            send_sem=send.at[k], recv_sem=recv.at[k], device_id=(x, y, 1 - c), device_id_type=MESH))
    return cps


def _swap_start(grads, name):
    n = len(grads)
    lands = [lax.empty((g.shape[0], g.shape[1] // 2, g.shape[2]), g.dtype) for g in grads]

    def body(*refs):
        ins, lz = refs[:n], refs[n:2 * n]
        send, recv = refs[2 * n], refs[2 * n + 1]
        token = refs[-1]
        for cp in _swap_copies(ins, lz, send, recv):
            cp.start()
        token[...] = jnp.zeros_like(token)

    hbm = [pltpu.HBM(a.shape, a.dtype) for a in list(grads) + lands]
    outs = pl.pallas_call(
        body, name=name,
        in_specs=[HBM] * (2 * n),
        out_specs=[SEM, SEM] + [HBM] * (2 * n) + [pl.BlockSpec(memory_space=pltpu.VMEM)],
        out_shape=[pltpu.SemaphoreType.DMA((n,)), pltpu.SemaphoreType.DMA((n,))] + hbm
        + [jax.ShapeDtypeStruct((SUBLANES, LANES), F32)],
        input_output_aliases={k: 2 + k for k in range(2 * n)},
        compiler_params=pltpu.CompilerParams(has_side_effects=EFFECT),
    )(*[_in_hbm(a) for a in grads], *[_in_hbm(a) for a in lands])
    return outs[0], outs[1], outs[2:2 + n], outs[2 + n:2 + 2 * n], outs[-1]


def _swap_wait(send, recv, grads, lands, after, name):
    n = len(grads)

    def body(*refs):
        ins, lz = refs[:n], refs[n:2 * n]
        send_r, recv_r = refs[2 * n], refs[2 * n + 1]
        for cp in _swap_copies(ins, lz, send_r, recv_r):
            cp.wait_send()
            cp.wait_recv()

    hbm = [pltpu.HBM(a.shape, a.dtype) for a in list(grads) + list(lands)]
    outs = pl.pallas_call(
        body, name=name,
        in_specs=[HBM] * (2 * n) + [SEM, SEM, ANY],
        out_specs=[HBM] * (2 * n), out_shape=hbm,
        input_output_aliases={k: k for k in range(2 * n)},
        compiler_params=pltpu.CompilerParams(has_side_effects=EFFECT),
    )(*grads, *lands, send, recv, after)
    return outs[:n], outs[n:]


def _add_cast(g, other, cidx, name):
    ns, R, Cc = g.shape
    rh = R // 2
    tr = _tile(rh, max(16, (1 << 19) // Cc), 16)
    nb = rh // tr

    def body(c_ref, g_ref, o_ref, s_ref):
        s_ref[...] = (g_ref[...] + o_ref[...]).astype(BF16)

    return pl.pallas_call(
        body,
        grid_spec=pltpu.PrefetchScalarGridSpec(
            num_scalar_prefetch=1, grid=(ns, nb),
            in_specs=[pl.BlockSpec((None, tr, Cc), lambda k, i, c: (k, c[0] * nb + i, 0)),
                      pl.BlockSpec((None, tr, Cc), lambda k, i, c: (k, i, 0))],
            out_specs=pl.BlockSpec((None, tr, Cc), lambda k, i, c: (k, i, 0))),
        out_shape=jax.ShapeDtypeStruct((ns, rh, Cc), BF16),
        compiler_params=_params("parallel", "parallel"), name=name)(cidx, g, other)


def _sum_slots(part, got, idx, name):
    ns, rh, Cc = got.shape
    tr = _tile(rh, max(16, (1 << 18) // Cc), 16)
    nb = rh // tr

    def body(i_ref, p_ref, b_ref, o_ref):
        acc = p_ref[...].astype(F32)
        for m in range(ns):
            acc = acc + b_ref[m].astype(F32)
        o_ref[...] = acc

    return pl.pallas_call(
        body,
        grid_spec=pltpu.PrefetchScalarGridSpec(
            num_scalar_prefetch=1, grid=(nb,),
            in_specs=[pl.BlockSpec((None, tr, Cc), lambda i, s: (s[1], i, 0)),
                      pl.BlockSpec((ns, tr, Cc), lambda i, s: (0, i, 0))],
            out_specs=pl.BlockSpec((tr, Cc), lambda i, s: (s[0] * nb + i, 0))),
        out_shape=jax.ShapeDtypeStruct((2 * rh, Cc), F32),
        compiler_params=_params("parallel"), name=name)(idx, part, got)


def _share_halves(blocks, name):
    n = len(blocks)

    def body(*refs):
        ins, outs = refs[:n], refs[n:2 * n]
        send, recv = refs[2 * n:]
        x, y, c = _here()
        cps = []
        for k in range(n):
            rh = outs[k].shape[0] // 2
            mine = outs[k].at[pl.ds(c * rh, rh), :]
            cp = pltpu.make_async_remote_copy(
                src_ref=mine, dst_ref=mine, send_sem=send.at[k], recv_sem=recv.at[k],
                device_id=(x, y, 1 - c), device_id_type=MESH)
            cp.start()
            cps.append(cp)
        for cp in cps:
            cp.wait()

    return pl.pallas_call(
        body, in_specs=[ANY] * n, out_specs=[ANY] * n,
        out_shape=[jax.ShapeDtypeStruct(b.shape, b.dtype) for b in blocks],
        input_output_aliases={k: k for k in range(n)},
        scratch_shapes=[pltpu.SemaphoreType.DMA((n,)), pltpu.SemaphoreType.DMA((n,))],
        name=name)(*blocks)


def _small_copies(p_ref, slot_ref, send, recv):
    x, y, c = _here()
    mine = slot_ref.at[4 * x + 2 * y + c]
    cps = []
    for m in range(1, N_DEV):
        peer = (x ^ (m >> 2), y ^ ((m >> 1) & 1), c ^ (m & 1))
        cps.append(pltpu.make_async_remote_copy(
            src_ref=p_ref, dst_ref=mine, send_sem=send.at[m - 1], recv_sem=recv.at[m - 1],
            device_id=peer, device_id_type=MESH))
    return cps


def _small_start(packed):
    slots = lax.empty((N_DEV,) + packed.shape, packed.dtype)

    def body(p_ref, s_ref, send, recv, p_thru, s_thru, token):
        for cp in _small_copies(p_ref, s_ref, send, recv):
            cp.start()
        token[...] = jnp.zeros_like(token)

    return pl.pallas_call(
        body, name="small_start",
        in_specs=[HBM, HBM],
        out_specs=[SEM, SEM, HBM, HBM, pl.BlockSpec(memory_space=pltpu.VMEM)],
        out_shape=[pltpu.SemaphoreType.DMA((N_DEV - 1,)), pltpu.SemaphoreType.DMA((N_DEV - 1,)),
                   pltpu.HBM(packed.shape, packed.dtype), pltpu.HBM(slots.shape, slots.dtype),
                   jax.ShapeDtypeStruct((SUBLANES, LANES), F32)],
        input_output_aliases={0: 2, 1: 3},
        compiler_params=pltpu.CompilerParams(has_side_effects=EFFECT),
    )(_in_hbm(packed), _in_hbm(slots))


def _small_wait(send, recv, packed, slots, after):
    def body(p_ref, s_ref, send_r, recv_r, after_ref, p_out, s_out):
        for cp in _small_copies(p_ref, s_ref, send_r, recv_r):
            cp.wait_send()
            cp.wait_recv()

    return pl.pallas_call(
        body, name="small_wait",
        in_specs=[HBM, HBM, SEM, SEM, ANY], out_specs=[HBM, HBM],
        out_shape=[pltpu.HBM(packed.shape, packed.dtype), pltpu.HBM(slots.shape, slots.dtype)],
        input_output_aliases={0: 0, 1: 1},
        compiler_params=pltpu.CompilerParams(has_side_effects=EFFECT),
    )(packed, slots, send, recv, after)


def _sum_devices(packed, slots, me):
    n, R, _ = slots.shape
    tr = _tile(R, 1024)

    def body(m_ref, p_ref, s_ref, o_ref):
        own = p_ref[...]
        acc = None
        for d in range(n):
            term = jnp.where(m_ref[0] == d, own, s_ref[d])
            acc = term if acc is None else acc + term
        o_ref[...] = acc

    return pl.pallas_call(
        body,
        grid_spec=pltpu.PrefetchScalarGridSpec(
            num_scalar_prefetch=1, grid=(R // tr,),
            in_specs=[pl.BlockSpec((tr, LANES), lambda i, m: (i, 0)),
                      pl.BlockSpec((n, tr, LANES), lambda i, m: (0, i, 0))],
            out_specs=pl.BlockSpec((tr, LANES), lambda i, m: (i, 0))),
        out_shape=jax.ShapeDtypeStruct((R, LANES), F32),
        compiler_params=_params("parallel"), name="sum_devices")(me, packed, slots)


def _pack(arrs):
    rows, parts = [], []
    for a in arrs:
        flat = a.reshape(-1)
        r = -(-flat.shape[0] // (SUBLANES * LANES)) * SUBLANES
        parts.append(jnp.pad(flat, (0, r * LANES - flat.shape[0])).reshape(r, LANES))
        rows.append(r)
    return jnp.concatenate(parts, axis=0), rows


def _unpack(packed, rows, shapes):
    out, r0 = [], 0
    for r, shp in zip(rows, shapes):
        size = math.prod(shp)
        out.append(packed[r0:r0 + r].reshape(-1)[:size].reshape(shp))
        r0 += r
    return out


def _block_diag(w, per):
    H, dh, _ = w.shape
    w4 = w.reshape(H // per, per, dh, dh)
    eye = jnp.eye(per, dtype=w.dtype)
    return (w4[:, :, :, None, :] * eye[None, :, None, :, None]).reshape(H // per, per * dh, per * dh)


def _block_diag_take(d, per):
    n, s, _ = d.shape
    dh = s // per
    d5 = d.reshape(n, per, dh, per, dh)
    return jnp.stack([d5[:, h, :, h, :] for h in range(per)], axis=1).reshape(n * per, dh, dh)


def kernel(x, ffn1_norm, ffn1_w_gate, ffn1_w_up, ffn1_w_down, mix_norm, w_in, conv_dw, conv_dw_bias, conv_ln_g, conv_ln_b, lru_conv_w, lru_conv_b, lru_w_a, lru_b_a, lru_w_x, lru_b_x, lru_lambda, w_out, ffn2_norm, ffn2_w_gate, ffn2_w_up, ffn2_w_down, final_norm, loss_target, m_ffn1_norm, m_ffn1_w_gate, m_ffn1_w_up, m_ffn1_w_down, m_mix_norm, m_w_in, m_conv_dw, m_conv_dw_bias, m_conv_ln_g, m_conv_ln_b, m_lru_conv_w, m_lru_conv_b, m_lru_w_a, m_lru_b_a, m_lru_w_x, m_lru_b_x, m_lru_lambda, m_w_out, m_ffn2_norm, m_ffn2_w_gate, m_ffn2_w_up, m_ffn2_w_down, m_final_norm, v_ffn1_norm, v_ffn1_w_gate, v_ffn1_w_up, v_ffn1_w_down, v_mix_norm, v_w_in, v_conv_dw, v_conv_dw_bias, v_conv_ln_g, v_conv_ln_b, v_lru_conv_w, v_lru_conv_b, v_lru_w_a, v_lru_b_a, v_lru_w_x, v_lru_b_x, v_lru_lambda, v_w_out, v_ffn2_norm, v_ffn2_w_gate, v_ffn2_w_up, v_ffn2_w_down, v_final_norm):
    names = ['ffn1_norm', 'ffn1_w_gate', 'ffn1_w_up', 'ffn1_w_down', 'mix_norm', 'w_in', 'conv_dw', 'conv_dw_bias',
             'conv_ln_g', 'conv_ln_b', 'lru_conv_w', 'lru_conv_b', 'lru_w_a', 'lru_b_a', 'lru_w_x', 'lru_b_x',
             'lru_lambda', 'w_out', 'ffn2_norm', 'ffn2_w_gate', 'ffn2_w_up', 'ffn2_w_down', 'final_norm']
    env = dict(locals())
    W = {n: env[n] for n in names}
    M = {n: env['m_' + n] for n in names}
    V = {n: env['v_' + n] for n in names}

    xi, yi, ci = _here()
    chip = 2 * xi + yi
    cidx = ci.astype(jnp.int32).reshape(1)
    T, D = x.shape[-2], x.shape[-1]
    xs = x.reshape(T, D)
    tgt = loss_target.reshape(T, D)
    K, Cs = conv_dw.shape
    C = conv_dw_bias.shape[0]
    Wl = lru_conv_b.shape[0]
    K4 = lru_conv_w.shape[0]
    heads, dh, _ = lru_w_a.shape
    per = LANES // dh

    def row(v):
        return v.reshape(1, -1)

    tform = ('ffn1_w_gate', 'ffn1_w_up', 'ffn2_w_gate', 'ffn2_w_up')
    for n in tform:
        W[n], M[n], V[n] = W[n].T, M[n].T, V[n].T
    kp = -(-K // SUBLANES) * SUBLANES
    taps = jnp.concatenate([conv_dw, jnp.zeros((kp - K, Cs), F32), lru_conv_w,
                            jnp.zeros((2 * SUBLANES - K4, Cs), F32)], axis=0)
    idx = jnp.stack([ci, chip]).astype(jnp.int32)
    (wff1,) = _gather_weights([_place_cast([W['ffn1_w_gate'], W['ffn1_w_up'], ffn1_w_down], idx, BF16, "place_ffn1")])
    mixl = [_place_cast([w_in], idx, BF16, "place_w_in"), _place_cast([w_out], idx, BF16, "place_w_out"),
            _place_cast([taps], idx, F32, "place_taps")]
    msend, mrecv, mixl, mtok = _gather_start(mixl, wff1, "gather_mix_start")
    ff2l = _place_cast([W['ffn2_w_gate'], W['ffn2_w_up'], ffn2_w_down], idx, BF16, "place_ffn2")
    fsend, frecv, ff2l, ftok = _gather_start([ff2l], mtok, "gather_ffn2_start")
    wa_bd = _block_diag(lru_w_a, per).astype(BF16)
    wx_bd = _block_diag(lru_w_x, per).astype(BF16)

    x1, a1, b1 = _ffn_fwd(xs, row(ffn1_norm) + ftok[0:1, 0:1], wff1, "ffn1_fwd")
    win, wout, taps = _gather_wait(msend, mrecv, mixl, x1, "gather_mix_wait")
    win, wout, taps = win[0], wout.reshape(-1, D), taps[0]
    conv_w_full = taps[:, :K].transpose(1, 0, 2).reshape(K, N_CHIPS * Cs)
    lru_w4_full = taps[:, kp:kp + K4].transpose(1, 0, 2).reshape(K4, N_CHIPS * Cs)
    z = _mix_in_fwd(x1, row(mix_norm), win)
    u, u1 = _conv_fwd(z, conv_w_full, row(conv_dw_bias), row(conv_ln_g), row(conv_ln_b))
    yr, hs = _lru_fwd(z, 2 * C, lru_w4_full, row(lru_conv_b), wa_bd, row(lru_b_a), wx_bd, row(lru_b_x),
                      row(lru_lambda))
    x2 = _mix_out_fwd(x1, u, yr, wout)
    (wff2,) = _gather_wait(fsend, frecv, ff2l, x2, "gather_ffn2_wait")
    dx3, a2, b2, loss_blk, d_final = _ffn_fwd(x2, row(ffn2_norm), wff2, "ffn2_fwd", head=(row(final_norm), tgt))

    dx2, da2, db2, p2, hb2, dyh2, d_ffn2n = _ffn_bwd_tok(dx3, x2, row(ffn2_norm), a2, b2, wff2, "ffn2_bwd")
    dwg2, dwu2, dwd2 = _ffn_wgrad([([da2, db2], hb2), ([p2], dyh2)], ftok, "ffn2_wgrad")
    wsend, wrecv, f2g, f2o, wtok = _swap_start([dwg2, dwu2, dwd2], "swap_ffn2_start")
    dcat, dwout = _mix_out_bwd(dx2, u, yr, wout)
    dzc, cst = _conv_bwd(dcat, u1, z, conv_w_full, row(conv_ln_g) + wtok[0:1, 0:1], row(conv_ln_b))
    dzx, dzg, lst, dwa_bd, dwx_bd = _lru_bwd(dcat, C, hs, z, 2 * C, lru_w4_full, row(lru_conv_b), wa_bd,
                                              row(lru_b_a), wx_bd, row(lru_b_x), row(lru_lambda))
    dx1, dwin, d_mixn = _mix_in_bwd(dzc, dzx, dzg, x1, dx2, row(mix_norm), win)

    early_names = ['w_in', 'w_out', 'ffn2_w_gate', 'ffn2_w_up', 'ffn2_w_down']
    mixg = [dwin, dwout.reshape(N_CHIPS, -1, D)]
    mixo = _swap_halves_out(mixg, "swap_halves_mix")
    f2g, f2o = _swap_wait(wsend, wrecv, f2g, f2o, dwin, "swap_ffn2_wait")
    e_parts = [_add_cast(g, o, cidx, "add_cast_" + n)
               for g, o, n in zip(mixg + list(f2g), list(mixo) + list(f2o), early_names)]
    esend, erecv, e_parts, e_lands, etok = _exchange_start(e_parts, "exchange_early_start")

    dx0, da1, db1, p1, hb1, dyh1, d_ffn1n = _ffn_bwd_tok(dx1, xs, row(ffn1_norm) + etok[0:1, 0:1], a1, b1, wff1,
                                                         "ffn1_bwd")

    small_names = ['ffn1_norm', 'mix_norm', 'conv_dw', 'conv_dw_bias', 'conv_ln_g', 'conv_ln_b', 'lru_conv_w',
                   'lru_conv_b', 'lru_w_a', 'lru_b_a', 'lru_w_x', 'lru_b_x', 'lru_lambda', 'ffn2_norm',
                   'final_norm']
    small = {
        'ffn1_norm': d_ffn1n, 'mix_norm': d_mixn, 'conv_dw': cst[:K], 'conv_dw_bias': cst[K + 1],
        'conv_ln_g': cst[K + 2], 'conv_ln_b': cst[K + 3], 'lru_conv_w': lst[:K4], 'lru_conv_b': lst[K4],
        'lru_w_a': _block_diag_take(dwa_bd, per), 'lru_b_a': lst[K4 + 1],
        'lru_w_x': _block_diag_take(dwx_bd, per), 'lru_b_x': lst[K4 + 2], 'lru_lambda': lst[K4 + 3],
        'ffn2_norm': d_ffn2n, 'final_norm': d_final,
    }
    packed, rows = _pack([small[n] for n in small_names] + [loss_blk[0:1, 0:1]])
    ssend, srecv, packed, sslots, stok = _small_start(packed)

    gu_names, d_names = ['ffn1_w_gate', 'ffn1_w_up'], ['ffn1_w_down']
    gu = _ffn_wgrad([([da1, db1], hb1)], stok, "ffn1_wgrad_gu")
    gu_parts = [_add_cast(g, o, cidx, "add_cast_" + n)
                for g, o, n in zip(gu, _swap_halves_out(gu, "swap_halves_gu"), gu_names)]
    gsend, grecv, gu_parts, gu_lands, gtok = _exchange_start(gu_parts, "exchange_gu_start")
    dn = _ffn_wgrad([([p1], dyh1)], gtok, "ffn1_wgrad_d")
    dwd1 = dn[0]
    d_parts = [_add_cast(g, o, cidx, "add_cast_" + n)
               for g, o, n in zip(dn, _swap_halves_out(dn, "swap_halves_d"), d_names)]
    dsend, drecv, d_parts, d_lands, ltok = _exchange_start(d_parts, "exchange_d_start")
    e_parts, e_slots = _exchange_wait(esend, erecv, e_parts, e_lands, ltok, "exchange_early_wait")
    delta, new_m, new_v = {}, {}, {}

    def finish(group, parts, slots, tag):
        halves = [_sum_slots(p, b, idx, "sum_slots_" + n) for p, b, n in zip(parts, slots, group)]
        for n, g in zip(group, _share_halves(halves, "share_halves_" + tag)):
            G[n] = g
            delta[n], new_m[n], new_v[n] = _adamw(W[n], g, M[n], V[n], "adamw_" + n)

    G = {}
    finish(early_names, e_parts, e_slots, "early")

    full_shapes = [(K, C) if n == 'conv_dw' else (K4, Wl) if n == 'lru_conv_w' else W[n].shape for n in small_names]
    packed, sslots = _small_wait(ssend, srecv, packed, sslots, dwd1)
    summed = _sum_devices(packed, sslots, (4 * xi + 2 * yi + ci).astype(jnp.int32).reshape(1))
    *small_sums, loss_sum = _unpack(summed, rows, full_shapes + [(1, 1)])
    for n, gsum in zip(small_names, small_sums):
        if n == 'conv_dw':
            gsum = lax.dynamic_slice_in_dim(gsum, chip * Cs, Cs, axis=1)
        elif n == 'lru_conv_w':
            gsum = lax.dynamic_slice_in_dim(gsum, chip * lru_conv_w.shape[1], lru_conv_w.shape[1], axis=1)
        G[n] = gsum

    pw, prow = _pack([W[n] for n in small_names])
    pg, _ = _pack([G[n] for n in small_names])
    pm, _ = _pack([M[n] for n in small_names])
    pv, _ = _pack([V[n] for n in small_names])
    sd, sm, sv = _adamw(pw, pg, pm, pv, "adamw_small")
    shapes = [W[n].shape for n in small_names]
    for n, a, b, c_ in zip(small_names, _unpack(sd, prow, shapes), _unpack(sm, prow, shapes),
                           _unpack(sv, prow, shapes)):
        delta[n], new_m[n], new_v[n] = a, b, c_

    done = sd[0:SUBLANES] + delta[early_names[-1]][0:SUBLANES, 0:LANES]
    gu_parts, gu_slots = _exchange_wait(gsend, grecv, gu_parts, gu_lands, done, "exchange_gu_wait")
    d_parts, d_slots = _exchange_wait(dsend, drecv, d_parts, d_lands, gu_slots[0], "exchange_d_wait")
    finish(gu_names + d_names, list(gu_parts) + list(d_parts), list(gu_slots) + list(d_slots), "last")

    loss = loss_sum[0, 0]
    grad_x = dx0.reshape(x.shape)
    for n in tform:
        G[n], delta[n], new_m[n], new_v[n] = G[n].T, delta[n].T, new_m[n].T, new_v[n].T
    return (loss, grad_x, *[G[n] for n in names], *[delta[n] for n in names],
            *[new_m[n] for n in names], *[new_v[n] for n in names])
```

```python
import functools
import math

import jax
import jax.numpy as jnp
from jax import lax
from jax.experimental import pallas as pl
from jax.experimental.pallas import tpu as pltpu

F32 = jnp.float32
BF16 = jnp.bfloat16
MESH = pl.DeviceIdType.MESH

RMS_EPS = 1e-6
LN_EPS = 1e-5
LRU_C = 8.0
FFN_RES_SCALE = 0.5
ADAM_LR = 0.001
ADAM_B1 = 0.9
ADAM_B2 = 0.999
ADAM_EPS = 1e-08
ADAM_WD = 0.01
ADAM_STEP = 10

LANES = 128
SUBLANES = 8
CONV_HALO = 32
LRU_HALO = 8
ROW_CHUNK = 64
VMEM_LIMIT = 56 * 1024 * 1024
N_CHIPS = 4
N_DEV = 8
TOK_TILE = 1024
BWD_TILE = 512
FFN_BWD_TILE = 512
BWD_ROWS = 32
FFN_BWD_CHAIN = 256
CONV_TILE = 512
LRU_TILE = 2048
LRU_GROUPS = 8


def _dot(a, b):
    return jnp.dot(a, b, preferred_element_type=F32)


def _dot_nt(a, b):
    return lax.dot_general(a, b, (((1,), (1,)), ((), ())), preferred_element_type=F32)


def _dot_tn(a, b):
    return lax.dot_general(a, b, (((0,), (0,)), ((), ())), preferred_element_type=F32)


def _tile(n, pref, mult=SUBLANES):
    for t in range(min(pref, n), 0, -1):
        if n % t == 0 and t % mult == 0:
            return t
    return n


def _params(*sem):
    return pltpu.CompilerParams(dimension_semantics=sem, vmem_limit_bytes=VMEM_LIMIT)


def _rms_stats(x):
    r = lax.rsqrt(jnp.mean(x * x, axis=-1, keepdims=True) + RMS_EPS)
    return x * r, r


def _rms_bwd(dh, xh, r, g):
    dxh = dh * g
    return r * (dxh - xh * jnp.mean(dxh * xh, axis=-1, keepdims=True))


def _colsum(v):
    return jnp.sum(v, axis=0, keepdims=True)


def _ffn_fwd(x, g, wff, name, head=None):
    T, D = x.shape
    ns, fs = wff.shape[1], wff.shape[2]
    tm = _tile(T, TOK_TILE)
    mc = _tile(tm, FFN_BWD_CHAIN, 16)
    rc = _tile(tm, FFN_BWD_CHAIN)

    def body(*refs):
        x_ref, g_ref, wg_ref, wu_ref, wd_ref = refs[:5]
        if head is None:
            y_ref, a_ref, b_ref, hb_ref, acc_ref = refs[5:]
        else:
            gf_ref, t_ref, y_ref, a_ref, b_ref, loss_ref, dgf_ref, hb_ref, acc_ref = refs[5:]
        j = pl.program_id(1)

        @pl.when(j == 0)
        def _():
            xh, _ = _rms_stats(x_ref[...])
            hb_ref[...] = (xh * g_ref[...]).astype(BF16)
            acc_ref[...] = jnp.zeros_like(acc_ref)

        if head is not None:
            @pl.when((pl.program_id(0) == 0) & (j == 0))
            def _():
                loss_ref[...] = jnp.zeros_like(loss_ref)
                dgf_ref[...] = jnp.zeros_like(dgf_ref)

        for q0 in range(0, tm, mc):
            blk = pl.ds(q0, mc)
            hb = hb_ref[blk, :]
            a = _dot_nt(hb, wg_ref[...])
            b = _dot_nt(hb, wu_ref[...])
            a_ref[blk, :] = a.astype(BF16)
            b_ref[blk, :] = b.astype(BF16)
            p = (a * jax.nn.sigmoid(a) * b).astype(BF16)
            acc_ref[blk, :] += _dot(p, wd_ref[...])

        @pl.when(j == ns - 1)
        def _():
            if head is None:
                y_ref[...] = x_ref[...] + FFN_RES_SCALE * acc_ref[...]
                return
            gv = gf_ref[...]
            loss = jnp.zeros((), F32)
            dg = jnp.zeros((1, D), F32)
            for r0 in range(0, tm, rc):
                rows = pl.ds(r0, rc)
                xh, r = _rms_stats(x_ref[rows, :] + FFN_RES_SCALE * acc_ref[rows, :])
                e = xh * gv - t_ref[rows, :]
                loss = loss + 0.5 * jnp.sum(jnp.mean(e * e, axis=-1, keepdims=True))
                dy = e * (1.0 / D)
                dg = dg + _colsum(dy * xh)
                y_ref[rows, :] = _rms_bwd(dy, xh, r, gv)
            loss_ref[...] += loss
            dgf_ref[...] += dg

    def wspec(n):
        return pl.BlockSpec((None, None, fs, D), lambda i, j: (n, j, 0, 0))

    tok = pl.BlockSpec((tm, D), lambda i, j: (i, 0))
    vec = pl.BlockSpec((1, D), lambda i, j: (0, 0))
    mid = pl.BlockSpec((None, tm, fs), lambda i, j: (j, i, 0))
    in_specs = [tok, vec, wspec(0), wspec(1), wspec(2)]
    out_specs = [tok, mid, mid]
    out_shape = [jax.ShapeDtypeStruct((T, D), F32), jax.ShapeDtypeStruct((ns, T, fs), BF16),
                 jax.ShapeDtypeStruct((ns, T, fs), BF16)]
    args = [x, g, wff, wff, wff]
    if head is not None:
        in_specs += [vec, tok]
        out_specs += [pl.BlockSpec((SUBLANES, LANES), lambda i, j: (0, 0)), vec]
        out_shape += [jax.ShapeDtypeStruct((SUBLANES, LANES), F32), jax.ShapeDtypeStruct((1, D), F32)]
        args += list(head)
    return pl.pallas_call(
        body, grid=(T // tm, ns), in_specs=in_specs, out_specs=out_specs, out_shape=out_shape,
        scratch_shapes=[pltpu.VMEM((tm, D), BF16), pltpu.VMEM((tm, D), F32)],
        compiler_params=_params("arbitrary", "arbitrary"), name=name)(*args)


def _ffn_bwd_tok(dy, x, g, a, b, wff, name):
    T, D = x.shape
    ns, fs = wff.shape[1], wff.shape[2]
    tm = _tile(T, FFN_BWD_TILE)
    rc = _tile(tm, BWD_ROWS)
    mc = _tile(tm, FFN_BWD_CHAIN, rc)

    def body(dy_ref, x_ref, g_ref, a_ref, b_ref, w_ref,
             dx_ref, da_ref, db_ref, p_ref, hb_ref, dyh_ref, dg_ref, dh_ref, dp_ref):
        i, j = pl.program_id(0), pl.program_id(1)
        cur = dp_ref.at[j % 2]
        nxt = dp_ref.at[(j + 1) % 2]
        wg_ref, wu_ref = w_ref.at[0, j], w_ref.at[1, j]
        wd0_ref, wdn_ref = w_ref.at[2, 0], w_ref.at[2, jnp.minimum(j + 1, ns - 1)]

        @pl.when((i == 0) & (j == 0))
        def _():
            dg_ref[...] = jnp.zeros_like(dg_ref)

        @pl.when(j == 0)
        def _():
            for r0 in range(0, tm, rc):
                rows = pl.ds(r0, rc)
                xh, _ = _rms_stats(x_ref[rows, :])
                hb_ref[rows, :] = (xh * g_ref[...]).astype(BF16)
                dyh_ref[rows, :] = (FFN_RES_SCALE * dy_ref[rows, :]).astype(BF16)
            dh_ref[...] = jnp.zeros_like(dh_ref)
            cur[...] = _dot_nt(dyh_ref[...], wd0_ref[...])

        def chains(with_next):
            for q0 in range(0, tm, mc):
                blk = pl.ds(q0, mc)
                for r0 in range(q0, q0 + mc, rc):
                    rows = pl.ds(r0, rc)
                    av = a_ref[rows, :].astype(F32)
                    bv = b_ref[rows, :].astype(F32)
                    dp = cur[rows, :]
                    s = jax.nn.sigmoid(av)
                    sl = av * s
                    da_ref[rows, :] = (dp * bv * (s * (1.0 + av * (1.0 - s)))).astype(BF16)
                    db_ref[rows, :] = (dp * sl).astype(BF16)
                    p_ref[rows, :] = (sl * bv).astype(BF16)
                if with_next:
                    nxt[blk, :] = _dot_nt(dyh_ref[blk, :], wdn_ref[...])
                dh_ref[blk, :] += _dot(da_ref[blk, :], wg_ref[...]) + _dot(db_ref[blk, :], wu_ref[...])

        pl.when(j < ns - 1)(functools.partial(chains, True))
        pl.when(j == ns - 1)(functools.partial(chains, False))

        @pl.when(j == ns - 1)
        def _():
            gv = g_ref[...]
            dg = jnp.zeros((1, D), F32)
            for r0 in range(0, tm, rc):
                rows = pl.ds(r0, rc)
                xh, r = _rms_stats(x_ref[rows, :])
                dh = dh_ref[rows, :]
                dx_ref[rows, :] = dy_ref[rows, :] + _rms_bwd(dh, xh, r, gv)
                dg = dg + _colsum(dh * xh)
            dg_ref[...] += dg

    tok = pl.BlockSpec((tm, D), lambda i, j: (i, 0))
    mid = pl.BlockSpec((None, tm, fs), lambda i, j: (j, i, 0))
    vec = pl.BlockSpec((1, D), lambda i, j: (0, 0))
    return pl.pallas_call(
        body, grid=(T // tm, ns),
        in_specs=[tok, tok, vec, mid, mid,
                  pl.BlockSpec(wff.shape, lambda i, j: (0, 0, 0, 0), pipeline_mode=pl.Buffered(1))],
        out_specs=[tok, mid, mid, mid, tok, tok, vec],
        out_shape=[jax.ShapeDtypeStruct((T, D), F32),
                   jax.ShapeDtypeStruct((ns, T, fs), BF16), jax.ShapeDtypeStruct((ns, T, fs), BF16),
                   jax.ShapeDtypeStruct((ns, T, fs), BF16),
                   jax.ShapeDtypeStruct((T, D), BF16), jax.ShapeDtypeStruct((T, D), BF16),
                   jax.ShapeDtypeStruct((1, D), F32)],
        scratch_shapes=[pltpu.VMEM((tm, D), F32), pltpu.VMEM((2, tm, fs), F32)],
        compiler_params=_params("arbitrary", "arbitrary"), name=name)(dy, x, g, a, b, wff)


def _ffn_wgrad(groups, after, name):
    flat = [(l, gi) for gi, (ls, _) in enumerate(groups) for l in ls]
    ng, n = len(groups), len(flat)
    T, D = groups[0][1].shape
    ns, _, fs = flat[0][0].shape
    tm = _tile(T, TOK_TILE)

    def body(*refs):
        rhs_refs, lhs_refs, out_refs = refs[:ng], refs[ng:ng + n], refs[ng + n + 1:]

        @pl.when(pl.program_id(1) == 0)
        def _():
            for o in out_refs:
                o[...] = jnp.zeros_like(o)

        rvs = [r[...] for r in rhs_refs]
        for l, o, (_, gi) in zip(lhs_refs, out_refs, flat):
            o[...] += _dot_tn(l[...], rvs[gi])

    tok = pl.BlockSpec((tm, D), lambda j, i: (i, 0))
    mid = pl.BlockSpec((None, tm, fs), lambda j, i: (j, i, 0))
    wsp = pl.BlockSpec((None, fs, D), lambda j, i: (j, 0, 0))
    sds = jax.ShapeDtypeStruct((ns, fs, D), F32)
    return pl.pallas_call(
        body, grid=(ns, T // tm),
        in_specs=[tok] * ng + [mid] * n + [pl.BlockSpec((SUBLANES, LANES), lambda j, i: (0, 0))],
        out_specs=[wsp] * n, out_shape=[sds] * n,
        compiler_params=_params("parallel", "arbitrary"), name=name)(
            *[r for _, r in groups], *[l for l, _ in flat], after)


def _mix_in_fwd(x, g, win):
    T, D = x.shape
    ns, ws = win.shape[0], win.shape[2]
    tm = _tile(T, TOK_TILE)

    def body(x_ref, g_ref, w_ref, z_ref):
        xh, _ = _rms_stats(x_ref[...])
        hb = (xh * g_ref[...]).astype(BF16)
        for j in range(ns):
            z_ref[:, pl.ds(j * ws, ws)] = _dot(hb, w_ref[j])

    return pl.pallas_call(
        body, grid=(T // tm,),
        in_specs=[pl.BlockSpec((tm, D), lambda i: (i, 0)), pl.BlockSpec((1, D), lambda i: (0, 0)),
                  pl.BlockSpec((ns, D, ws), lambda i: (0, 0, 0), pipeline_mode=pl.Buffered(1))],
        out_specs=pl.BlockSpec((tm, ns * ws), lambda i: (i, 0)),
        out_shape=jax.ShapeDtypeStruct((T, ns * ws), F32),
        compiler_params=_params("parallel"), name="mix_in_fwd")(x, g, win)


def _tap_sum(buf, w_ref, ntaps, first_row, r0, rows, flip):
    acc = None
    for k in range(ntaps):
        off = (ntaps - 1 - k) if flip else k
        t = buf[pl.ds(first_row + r0 + off, rows), :] * w_ref[pl.ds(k, 1), :]
        acc = t if acc is None else acc + t
    return acc


def _shift_copies(buf, sh, rows):
    for r in range(1, SUBLANES):
        sh[r - 1, pl.ds(0, rows), :] = buf[pl.ds(r, rows), :]


def _tap_rows(buf, sh, off, r0, rows):
    r = off % SUBLANES
    if r == 0:
        return buf[pl.ds(off + r0, rows), :]
    return sh[r - 1, pl.ds(off - r + r0, rows), :]


def _tap_sum_tiles(buf, sh, w_ref, ntaps, first_row, r0, rows, flip):
    acc = None
    for k in range(ntaps):
        off = first_row + ((ntaps - 1 - k) if flip else k)
        t = _tap_rows(buf, sh, off, r0, rows) * w_ref[pl.ds(k, 1), :]
        acc = t if acc is None else acc + t
    return acc


def _conv_fwd(z, w, bias, lng, lnb):
    T = z.shape[0]
    K, C = w.shape
    tm = _tile(T, CONV_TILE, ROW_CHUNK)
    rc = min(ROW_CHUNK, tm)
    srows = tm + CONV_HALO - SUBLANES

    def body(cv_ref, cg_ref, w_ref, b_ref, g_ref, bb_ref, u_ref, u1_ref, buf, sh):
        @pl.when(pl.program_id(0) == 0)
        def _():
            buf[pl.ds(0, CONV_HALO), :] = jnp.zeros((CONV_HALO, C), F32)

        buf[pl.ds(CONV_HALO, tm), :] = cv_ref[...] * jax.nn.sigmoid(cg_ref[...])
        _shift_copies(buf, sh, srows)
        for r0 in range(0, tm, rc):
            u1 = _tap_sum_tiles(buf, sh, w_ref, K, CONV_HALO - (K - 1), r0, rc, False) + b_ref[...]
            u1_ref[pl.ds(r0, rc), :] = u1
            xc = u1 - jnp.mean(u1, axis=-1, keepdims=True)
            xh = xc * lax.rsqrt(jnp.mean(xc * xc, axis=-1, keepdims=True) + LN_EPS)
            u2 = xh * g_ref[...] + bb_ref[...]
            u_ref[pl.ds(r0, rc), :] = (u2 * jax.nn.sigmoid(u2)).astype(BF16)
        buf[pl.ds(0, CONV_HALO), :] = buf[pl.ds(tm, CONV_HALO), :]

    vec = pl.BlockSpec((1, C), lambda i: (0, 0))
    return pl.pallas_call(
        body, grid=(T // tm,),
        in_specs=[pl.BlockSpec((tm, C), lambda i: (i, 0)), pl.BlockSpec((tm, C), lambda i: (i, 1)),
                  pl.BlockSpec((K, C), lambda i: (0, 0)), vec, vec, vec],
        out_specs=[pl.BlockSpec((tm, C), lambda i: (i, 0)), pl.BlockSpec((tm, C), lambda i: (i, 0))],
        out_shape=[jax.ShapeDtypeStruct((T, C), BF16), jax.ShapeDtypeStruct((T, C), F32)],
        scratch_shapes=[pltpu.VMEM((CONV_HALO + tm, C), F32), pltpu.VMEM((SUBLANES - 1, srows, C), F32)],
        compiler_params=_params("arbitrary"), name="conv_fwd")(z, z, w, bias, lng, lnb)


def _conv_bwd(dcat, u1, z, w, lng, lnb):
    T = z.shape[0]
    K, C = w.shape
    tm = _tile(T, CONV_TILE, ROW_CHUNK)
    rc = min(ROW_CHUNK, tm)
    nI = T // tm
    hb = tm // CONV_HALO
    srows = ((K + 4 + SUBLANES - 1) // SUBLANES) * SUBLANES
    shrows = tm + CONV_HALO - SUBLANES

    def body(du_ref, u1_ref, cv_ref, cg_ref, cvp_ref, cgp_ref, w_ref, g_ref, bb_ref,
             dz_ref, st_ref, u0buf, d1buf, ush, dsh):
        i = pl.program_id(0)
        ti = nI - 1 - i

        @pl.when(i == 0)
        def _():
            st_ref[...] = jnp.zeros_like(st_ref)
            d1buf[pl.ds(tm, CONV_HALO), :] = jnp.zeros((CONV_HALO, C), F32)

        prev = cvp_ref[...] * jax.nn.sigmoid(cgp_ref[...])
        u0buf[pl.ds(0, CONV_HALO), :] = jnp.where(ti == 0, 0.0, prev)
        u0buf[pl.ds(CONV_HALO, tm), :] = cv_ref[...] * jax.nn.sigmoid(cg_ref[...])

        gv = g_ref[...]
        dbias = jnp.zeros((1, C), F32)
        dgain = jnp.zeros((1, C), F32)
        dlnb = jnp.zeros((1, C), F32)
        for r0 in range(0, tm, rc):
            u1 = u1_ref[pl.ds(r0, rc), :]
            xc = u1 - jnp.mean(u1, axis=-1, keepdims=True)
            rstd = lax.rsqrt(jnp.mean(xc * xc, axis=-1, keepdims=True) + LN_EPS)
            xh = xc * rstd
            u2 = xh * gv + bb_ref[...]
            s = jax.nn.sigmoid(u2)
            du2 = du_ref[pl.ds(r0, rc), :] * (s * (1.0 + u2 * (1.0 - s)))
            dgain = dgain + _colsum(du2 * xh)
            dlnb = dlnb + _colsum(du2)
            dxh = du2 * gv
            du1 = rstd * (dxh - jnp.mean(dxh, axis=-1, keepdims=True)
                          - xh * jnp.mean(dxh * xh, axis=-1, keepdims=True))
            dbias = dbias + _colsum(du1)
            d1buf[pl.ds(r0, rc), :] = du1
        st_ref[pl.ds(K + 1, 1), :] += dbias
        st_ref[pl.ds(K + 2, 1), :] += dgain
        st_ref[pl.ds(K + 3, 1), :] += dlnb

        _shift_copies(u0buf, ush, shrows)
        _shift_copies(d1buf, dsh, shrows)
        for k in range(K):
            acc = jnp.zeros((SUBLANES, C), F32)
            for r0 in range(0, tm, rc):
                prod = d1buf[pl.ds(r0, rc), :] * _tap_rows(u0buf, ush, CONV_HALO - (K - 1) + k, r0, rc)
                acc = acc + jnp.sum(prod.reshape(rc // SUBLANES, SUBLANES, C), axis=0)
            st_ref[pl.ds(k, 1), :] += _colsum(acc)

        for r0 in range(0, tm, rc):
            du0 = _tap_sum_tiles(d1buf, dsh, w_ref, K, 0, r0, rc, True)
            cv = cv_ref[pl.ds(r0, rc), :]
            sg = jax.nn.sigmoid(cg_ref[pl.ds(r0, rc), :])
            dz_ref[pl.ds(r0, rc), pl.ds(0, C)] = (du0 * sg).astype(BF16)
            dz_ref[pl.ds(r0, rc), pl.ds(C, C)] = (du0 * cv * sg * (1.0 - sg)).astype(BF16)
        d1buf[pl.ds(tm, CONV_HALO), :] = d1buf[pl.ds(0, CONV_HALO), :]

    def rev(col):
        return lambda i: (nI - 1 - i, col)

    def rev_prev(col):
        return lambda i: (jnp.maximum((nI - 1 - i) * hb - 1, 0), col)

    vec = pl.BlockSpec((1, C), lambda i: (0, 0))
    return pl.pallas_call(
        body, grid=(nI,),
        in_specs=[pl.BlockSpec((tm, C), rev(0)), pl.BlockSpec((tm, C), rev(0)),
                  pl.BlockSpec((tm, C), rev(0)), pl.BlockSpec((tm, C), rev(1)),
                  pl.BlockSpec((CONV_HALO, C), rev_prev(0)), pl.BlockSpec((CONV_HALO, C), rev_prev(1)),
                  pl.BlockSpec((K, C), lambda i: (0, 0)), vec, vec],
        out_specs=[pl.BlockSpec((tm, 2 * C), rev(0)), pl.BlockSpec((srows, C), lambda i: (0, 0))],
        out_shape=[jax.ShapeDtypeStruct((T, 2 * C), BF16), jax.ShapeDtypeStruct((srows, C), F32)],
        scratch_shapes=[pltpu.VMEM((CONV_HALO + tm, C), F32), pltpu.VMEM((tm + CONV_HALO, C), F32),
                        pltpu.VMEM((SUBLANES - 1, shrows, C), F32), pltpu.VMEM((SUBLANES - 1, shrows, C), F32)],
        compiler_params=_params("arbitrary"), name="conv_bwd")(dcat, u1, z, z, z, z, w, lng, lnb)


def _softplus(v):
    return jnp.maximum(v, 0.0) + jnp.log(1.0 + jnp.exp(-jnp.abs(v)))


def _gelu(v):
    c = math.sqrt(2.0 / math.pi)
    t = jnp.tanh(c * (v + 0.044715 * v * v * v))
    gl = 0.5 * v * (1.0 + t)
    dgl = 0.5 * (1.0 + t) + 0.5 * v * (1.0 - t * t) * c * (1.0 + 3.0 * 0.044715 * v * v)
    return gl, dgl


def _lru_gates(xr, wa, ba, wx, bx, lam):
    xb = xr.astype(BF16)
    r = jax.nn.sigmoid(_dot(xb, wa) + ba)
    ig = jax.nn.sigmoid(_dot(xb, wx) + bx)
    sp = _softplus(-lam)
    log_a = -LRU_C * r * sp
    a = jnp.exp(log_a)
    y = 2.0 * log_a
    series = -(y * (1.0 + y * (0.5 + y * (1.0 / 6.0 + y * (1.0 / 24.0)))))
    mult = jnp.sqrt(jnp.where(y > -0.02, series, 1.0 - jnp.exp(y)))
    return a, mult, r, ig, sp


def _scan_tile(a_s, b_s, h_s, p_s, carry, seg, reverse):
    hl = [jnp.zeros((SUBLANES, LANES), F32)] * LRU_GROUPS
    pr = [jnp.ones((SUBLANES, LANES), F32)] * LRU_GROUPS
    for n in range(seg):
        for g in range(LRU_GROUPS):
            rows = pl.ds(g * SUBLANES * seg + ((seg - 1 - n) if reverse else n), SUBLANES, stride=seg)
            av = a_s[rows, :]
            hl[g] = av * hl[g] + b_s[rows, :]
            pr[g] = av * pr[g]
            h_s[rows, :] = hl[g]
            p_s[rows, :] = pr[g]
    nseg = SUBLANES * LRU_GROUPS
    cs = [None] * nseg
    c = carry
    for s in (range(nseg - 1, -1, -1) if reverse else range(nseg)):
        g, r = divmod(s, SUBLANES)
        cs[s] = c
        c = hl[g][r:r + 1, :] + pr[g][r:r + 1, :] * c
    return cs, c


def _lru_fwd(z, col0, w4, b4, wa, ba, wx, bx, lam):
    T = z.shape[0]
    K4, W = w4.shape
    nC = W // LANES
    tm = _tile(T, LRU_TILE, SUBLANES * SUBLANES * LRU_GROUPS)
    seg = tm // (SUBLANES * LRU_GROUPS)
    cx, cg = col0 // LANES, (col0 + W) // LANES

    def body(rx_ref, rg_ref, w4_ref, b4_ref, wa_ref, ba_ref, wx_ref, bx_ref, lam_ref,
             yr_ref, hs_ref, xbuf, a_s, b_s, h_s, p_s, hc):
        @pl.when(pl.program_id(1) == 0)
        def _():
            xbuf[pl.ds(0, LRU_HALO), :] = jnp.zeros((LRU_HALO, LANES), F32)
            hc[...] = jnp.zeros_like(hc)

        xbuf[pl.ds(LRU_HALO, tm), :] = rx_ref[...]
        xr = _tap_sum(xbuf, w4_ref, K4, LRU_HALO - (K4 - 1), 0, tm, False) + b4_ref[...]
        a, mult, _, ig, _ = _lru_gates(xr, wa_ref[...], ba_ref[...], wx_ref[...], bx_ref[...], lam_ref[...])
        a_s[...] = a
        b_s[...] = mult * ig * xr
        cs, cout = _scan_tile(a_s, b_s, h_s, p_s, hc[pl.ds(0, 1), :], seg, False)
        hc[pl.ds(0, 1), :] = cout
        for s in range(SUBLANES * LRU_GROUPS):
            rows = pl.ds(s * seg, seg)
            h = h_s[rows, :] + p_s[rows, :] * cs[s]
            hs_ref[rows, :] = h
            gl, _ = _gelu(rg_ref[rows, :])
            yr_ref[rows, :] = (h * gl).astype(BF16)
        xbuf[pl.ds(0, LRU_HALO), :] = xbuf[pl.ds(tm, LRU_HALO), :]

    vec = pl.BlockSpec((1, LANES), lambda c, i: (0, c))
    mat = pl.BlockSpec((None, LANES, LANES), lambda c, i: (c, 0, 0))
    return pl.pallas_call(
        body, grid=(nC, T // tm),
        in_specs=[pl.BlockSpec((tm, LANES), lambda c, i: (i, cx + c)),
                  pl.BlockSpec((tm, LANES), lambda c, i: (i, cg + c)),
                  pl.BlockSpec((K4, LANES), lambda c, i: (0, c)), vec, mat, vec, mat, vec, vec],
        out_specs=[pl.BlockSpec((tm, LANES), lambda c, i: (i, c)), pl.BlockSpec((tm, LANES), lambda c, i: (i, c))],
        out_shape=[jax.ShapeDtypeStruct((T, W), BF16), jax.ShapeDtypeStruct((T, W), F32)],
        scratch_shapes=[pltpu.VMEM((LRU_HALO + tm, LANES), F32)] + [pltpu.VMEM((tm, LANES), F32)] * 4
        + [pltpu.VMEM((SUBLANES, LANES), F32)],
        compiler_params=_params("parallel", "arbitrary"), name="lru_fwd")(z, z, w4, b4, wa, ba, wx, bx, lam)


def _lru_bwd(dcat, dcol0, hs, z, col0, w4, b4, wa, ba, wx, bx, lam):
    T = z.shape[0]
    K4, W = w4.shape
    assert K4 + 4 == SUBLANES
    nC = W // LANES
    tm = _tile(T, LRU_TILE, SUBLANES * SUBLANES * LRU_GROUPS)
    seg = tm // (SUBLANES * LRU_GROUPS)
    nI = T // tm
    hb = tm // LRU_HALO
    cx, cg, cd = col0 // LANES, (col0 + W) // LANES, dcol0 // LANES

    def body(dyr_ref, hs_ref, hsp_ref, rx_ref, rxp_ref, rg_ref, w4_ref, b4_ref, wa_ref, ba_ref, wx_ref, bx_ref,
             lam_ref, dzx_ref, dzg_ref, st_ref, dwa_ref, dwx_ref, xbuf, hbuf, abuf, a_s, b_s, h_s, p_s, dbuf, gc, anc):
        i = pl.program_id(1)
        ti = nI - 1 - i

        @pl.when(i == 0)
        def _():
            st_ref[...] = jnp.zeros_like(st_ref)
            dwa_ref[...] = jnp.zeros_like(dwa_ref)
            dwx_ref[...] = jnp.zeros_like(dwx_ref)
            gc[...] = jnp.zeros_like(gc)
            anc[...] = jnp.zeros_like(anc)
            dbuf[pl.ds(tm, LRU_HALO), :] = jnp.zeros((LRU_HALO, LANES), F32)

        xbuf[pl.ds(0, LRU_HALO), :] = jnp.where(ti == 0, 0.0, rxp_ref[...])
        xbuf[pl.ds(LRU_HALO, tm), :] = rx_ref[...]
        hbuf[pl.ds(0, LRU_HALO), :] = jnp.where(ti == 0, 0.0, hsp_ref[...])
        hbuf[pl.ds(LRU_HALO, tm), :] = hs_ref[...]

        wa, wx = wa_ref[...], wx_ref[...]
        lam_v = lam_ref[...]
        xr = _tap_sum(xbuf, w4_ref, K4, LRU_HALO - (K4 - 1), 0, tm, False) + b4_ref[...]
        a, mult, r, ig, sp = _lru_gates(xr, wa, ba_ref[...], wx, bx_ref[...], lam_v)

        dyr = dyr_ref[...]
        gl, dgl = _gelu(rg_ref[...])
        dzg_ref[...] = (dyr * hs_ref[...] * dgl).astype(BF16)

        abuf[pl.ds(0, tm), :] = a
        abuf[pl.ds(tm, LRU_HALO), :] = anc[...]
        a_s[...] = abuf[pl.ds(1, tm), :]
        b_s[...] = dyr * gl
        cs, cout = _scan_tile(a_s, b_s, h_s, p_s, gc[pl.ds(0, 1), :], seg, True)
        gc[pl.ds(0, 1), :] = cout
        anc[pl.ds(0, 1), :] = a[0:1, :]
        for s in range(SUBLANES * LRU_GROUPS):
            rows = pl.ds(s * seg, seg)
            b_s[rows, :] = h_s[rows, :] + p_s[rows, :] * cs[s]
        g = b_s[...]

        d_a = g * hbuf[pl.ds(LRU_HALO - 1, tm), :]
        gx_ = g * xr
        d_log_a = d_a * a - (gx_ * ig) * (a * a / mult)
        dga = (d_log_a * (-LRU_C * sp)) * r * (1.0 - r)
        dgx = (gx_ * mult) * ig * (1.0 - ig)
        dga_b, dgx_b = dga.astype(BF16), dgx.astype(BF16)
        dxr = g * mult * ig + _dot_nt(dga_b, wa) + _dot_nt(dgx_b, wx)
        xb = xr.astype(BF16)
        dwa_ref[...] += _dot_tn(xb, dga_b)
        dwx_ref[...] += _dot_tn(xb, dgx_b)
        st_ref[pl.ds(K4, 1), :] += _colsum(dxr)
        st_ref[pl.ds(K4 + 1, 1), :] += _colsum(dga)
        st_ref[pl.ds(K4 + 2, 1), :] += _colsum(dgx)
        st_ref[pl.ds(K4 + 3, 1), :] += _colsum(d_log_a * (-LRU_C * r)) * (-jax.nn.sigmoid(-lam_v))

        dbuf[pl.ds(0, tm), :] = dxr
        for k in range(K4):
            st_ref[pl.ds(k, 1), :] += _colsum(dxr * xbuf[pl.ds(LRU_HALO - (K4 - 1) + k, tm), :])
        dzx_ref[...] = _tap_sum(dbuf, w4_ref, K4, 0, 0, tm, True).astype(BF16)
        dbuf[pl.ds(tm, LRU_HALO), :] = dbuf[pl.ds(0, LRU_HALO), :]

    def rev(col):
        return lambda c, i: (nI - 1 - i, col + c)

    def rev_prev(col):
        return lambda c, i: (jnp.maximum((nI - 1 - i) * hb - 1, 0), col + c)

    vec = pl.BlockSpec((1, LANES), lambda c, i: (0, c))
    mat = pl.BlockSpec((None, LANES, LANES), lambda c, i: (c, 0, 0))
    big = pltpu.VMEM((tm, LANES), F32)
    halo = pltpu.VMEM((tm + LRU_HALO, LANES), F32)
    return pl.pallas_call(
        body, grid=(nC, nI),
        in_specs=[pl.BlockSpec((tm, LANES), rev(cd)),
                  pl.BlockSpec((tm, LANES), rev(0)), pl.BlockSpec((LRU_HALO, LANES), rev_prev(0)),
                  pl.BlockSpec((tm, LANES), rev(cx)), pl.BlockSpec((LRU_HALO, LANES), rev_prev(cx)),
                  pl.BlockSpec((tm, LANES), rev(cg)),
                  pl.BlockSpec((K4, LANES), lambda c, i: (0, c)), vec, mat, vec, mat, vec, vec],
        out_specs=[pl.BlockSpec((tm, LANES), rev(0)), pl.BlockSpec((tm, LANES), rev(0)),
                   pl.BlockSpec((SUBLANES, LANES), lambda c, i: (0, c)), mat, mat],
        out_shape=[jax.ShapeDtypeStruct((T, W), BF16), jax.ShapeDtypeStruct((T, W), BF16),
                   jax.ShapeDtypeStruct((SUBLANES, W), F32),
                   jax.ShapeDtypeStruct((nC, LANES, LANES), F32), jax.ShapeDtypeStruct((nC, LANES, LANES), F32)],
        scratch_shapes=[halo, halo, halo, big, big, big, big, halo,
                        pltpu.VMEM((SUBLANES, LANES), F32), pltpu.VMEM((SUBLANES, LANES), F32)],
        compiler_params=_params("parallel", "arbitrary"), name="lru_bwd")(
            dcat, hs, hs, z, z, z, w4, b4, wa, ba, wx, bx, lam)


def _mix_out_fwd(x, u, yr, wout):
    T, D = x.shape
    C, W = u.shape[1], yr.shape[1]
    tm = _tile(T, TOK_TILE)

    def body(x_ref, u_ref, yr_ref, w_ref, y_ref):
        y_ref[...] = (x_ref[...] + _dot(u_ref[...], w_ref[pl.ds(0, C), :])
                      + _dot(yr_ref[...], w_ref[pl.ds(C, W), :]))

    return pl.pallas_call(
        body, grid=(T // tm,),
        in_specs=[pl.BlockSpec((tm, D), lambda i: (i, 0)), pl.BlockSpec((tm, C), lambda i: (i, 0)),
                  pl.BlockSpec((tm, W), lambda i: (i, 0)),
                  pl.BlockSpec((C + W, D), lambda i: (0, 0), pipeline_mode=pl.Buffered(1))],
        out_specs=pl.BlockSpec((tm, D), lambda i: (i, 0)),
        out_shape=jax.ShapeDtypeStruct((T, D), F32),
        compiler_params=_params("parallel"), name="mix_out_fwd")(x, u, yr, wout)


def _mix_out_bwd(dy, u, yr, wout):
    T, D = dy.shape
    C, W = u.shape[1], yr.shape[1]
    tm = _tile(T, BWD_TILE)

    def body(dy_ref, u_ref, yr_ref, w_ref, dcat_ref, dw_ref):
        @pl.when(pl.program_id(0) == 0)
        def _():
            dw_ref[...] = jnp.zeros_like(dw_ref)

        dyb = dy_ref[...].astype(BF16)
        dcat_ref[...] = _dot_nt(dyb, w_ref[...])
        dw_ref[pl.ds(0, C), :] += _dot_tn(u_ref[...], dyb)
        dw_ref[pl.ds(C, W), :] += _dot_tn(yr_ref[...], dyb)

    return pl.pallas_call(
        body, grid=(T // tm,),
        in_specs=[pl.BlockSpec((tm, D), lambda i: (i, 0)), pl.BlockSpec((tm, C), lambda i: (i, 0)),
                  pl.BlockSpec((tm, W), lambda i: (i, 0)),
                  pl.BlockSpec((C + W, D), lambda i: (0, 0), pipeline_mode=pl.Buffered(1))],
        out_specs=[pl.BlockSpec((tm, C + W), lambda i: (i, 0)), pl.BlockSpec((C + W, D), lambda i: (0, 0))],
        out_shape=[jax.ShapeDtypeStruct((T, C + W), F32), jax.ShapeDtypeStruct((C + W, D), F32)],
        compiler_params=_params("arbitrary"), name="mix_out_bwd")(dy, u, yr, wout)


def _mix_in_bwd(dzc, dzx, dzg, x, dy, g, win):
    T, D = x.shape
    ns, ws = win.shape[0], win.shape[2]
    tm = _tile(T, BWD_TILE)
    parts = []
    for j in range(ns):
        lo = j * ws
        if lo < dzc.shape[1]:
            parts.append((0, lo))
        elif lo < dzc.shape[1] + dzx.shape[1]:
            parts.append((1, lo - dzc.shape[1]))
        else:
            parts.append((2, lo - dzc.shape[1] - dzx.shape[1]))

    def body(dzc_ref, dzx_ref, dzg_ref, x_ref, dy_ref, g_ref, w_ref, dx_ref, dw_ref, dg_ref):
        @pl.when(pl.program_id(0) == 0)
        def _():
            dw_ref[...] = jnp.zeros_like(dw_ref)
            dg_ref[...] = jnp.zeros_like(dg_ref)

        xh, r = _rms_stats(x_ref[...])
        gv = g_ref[...]
        hb = (xh * gv).astype(BF16)
        srcs = (dzc_ref, dzx_ref, dzg_ref)
        dh = jnp.zeros((tm, D), F32)
        for j, (si, off) in enumerate(parts):
            dzj = srcs[si][:, pl.ds(off, ws)]
            dh = dh + _dot_nt(dzj, w_ref[j])
            dw_ref[j] += _dot_tn(hb, dzj)
        dx_ref[...] = dy_ref[...] + _rms_bwd(dh, xh, r, gv)
        dg_ref[...] += _colsum(dh * xh)

    def tok(n):
        return pl.BlockSpec((tm, n), lambda i: (i, 0))

    vec = pl.BlockSpec((1, D), lambda i: (0, 0))
    return pl.pallas_call(
        body, grid=(T // tm,),
        in_specs=[tok(dzc.shape[1]), tok(dzx.shape[1]), tok(dzg.shape[1]), tok(D), tok(D), vec,
                  pl.BlockSpec((ns, D, ws), lambda i: (0, 0, 0), pipeline_mode=pl.Buffered(1))],
        out_specs=[tok(D), pl.BlockSpec((ns, D, ws), lambda i: (0, 0, 0)), vec],
        out_shape=[jax.ShapeDtypeStruct((T, D), F32), jax.ShapeDtypeStruct((ns, D, ws), F32),
                   jax.ShapeDtypeStruct((1, D), F32)],
        compiler_params=_params("arbitrary"), name="mix_in_bwd")(dzc, dzx, dzg, x, dy, g, win)


def _adamw(w, g, m, v, name):
    R, Cc = w.shape
    tr = _tile(R, max(SUBLANES, (1 << 19) // Cc))
    c1 = 1.0 - ADAM_B1 ** ADAM_STEP
    c2 = 1.0 - ADAM_B2 ** ADAM_STEP

    def body(w_ref, g_ref, m_ref, v_ref, d_ref, nm_ref, nv_ref):
        gv = g_ref[...]
        nm = ADAM_B1 * m_ref[...] + (1.0 - ADAM_B1) * gv
        nv = ADAM_B2 * v_ref[...] + (1.0 - ADAM_B2) * (gv * gv)
        nm_ref[...] = nm
        nv_ref[...] = nv
        d_ref[...] = -ADAM_LR * ((nm / c1) / (jnp.sqrt(nv / c2) + ADAM_EPS) + ADAM_WD * w_ref[...])

    blk = pl.BlockSpec((tr, Cc), lambda i: (i, 0))
    sds = jax.ShapeDtypeStruct((R, Cc), F32)
    return pl.pallas_call(
        body, grid=(R // tr,), in_specs=[blk] * 4, out_specs=[blk] * 3, out_shape=[sds] * 3,
        compiler_params=_params("parallel"), name=name)(w, g, m, v)


def _here():
    return lax.axis_index("x"), lax.axis_index("y"), lax.axis_index("c")


def _chip_at(x, y, m):
    return x ^ (m >> 1), y ^ (m & 1)


ANY = pl.BlockSpec(memory_space=pl.ANY)


def _place_cast(srcs, idx, dtype, name):
    n = len(srcs)
    R, Cc = srcs[0].shape
    tr = _tile(R, max(16, (1 << 18) // Cc), 16)

    def body(i_ref, *refs):
        o_ref = refs[n]
        for k in range(n):
            o_ref[k] = refs[k][...].astype(dtype)

    blk = pl.BlockSpec((tr, Cc), lambda i, s: (i, 0))
    return pl.pallas_call(
        body,
        grid_spec=pltpu.PrefetchScalarGridSpec(
            num_scalar_prefetch=1, grid=(R // tr,), in_specs=[blk] * n,
            out_specs=pl.BlockSpec((n, None, tr, Cc), lambda i, s: (0, s[1], i, 0))),
        out_shape=jax.ShapeDtypeStruct((n, N_CHIPS, R, Cc), dtype),
        compiler_params=_params("parallel"), name=name)(idx, *srcs)


def _gather_weights(lands):
    n = len(lands)

    def body(*refs):
        outs = refs[n:2 * n]
        send1, recv1, send2, recv2 = refs[2 * n:]
        x, y, c = _here()
        own = 2 * x + y

        def half(ref, chip, cc):
            rh = ref.shape[-2] // 2
            lead = (slice(None),) * (len(ref.shape) - 3)
            return ref.at[lead + (chip, pl.ds(cc * rh, rh), slice(None))]

        first = []
        for k in range(n):
            for m in (1, 2, 3):
                px, py = _chip_at(x, y, m)
                cp = pltpu.make_async_remote_copy(
                    src_ref=half(outs[k], own, c), dst_ref=half(outs[k], own, c),
                    send_sem=send1.at[k, m - 1], recv_sem=recv1.at[k, m - 1],
                    device_id=(px, py, c), device_id_type=MESH)
                cp.start()
                first.append(cp)

        passed = []
        for k in range(n):
            for m in (1, 2, 3):
                px, py = _chip_at(x, y, m)
                peer = 2 * px + py
                got = half(outs[k], peer, c)
                pltpu.make_async_remote_copy(
                    src_ref=got, dst_ref=got, send_sem=send1.at[k, m - 1], recv_sem=recv1.at[k, m - 1],
                    device_id=(px, py, c), device_id_type=MESH).wait_recv()
                cp = pltpu.make_async_remote_copy(
                    src_ref=got, dst_ref=got, send_sem=send2.at[k, m - 1], recv_sem=recv2.at[k, m - 1],
                    device_id=(x, y, 1 - c), device_id_type=MESH)
                cp.start()
                passed.append(cp)

        for k in range(n):
            for m in (1, 2, 3):
                px, py = _chip_at(x, y, m)
                other = half(outs[k], 2 * px + py, 1 - c)
                pltpu.make_async_remote_copy(
                    src_ref=other, dst_ref=other, send_sem=send2.at[k, m - 1], recv_sem=recv2.at[k, m - 1],
                    device_id=(x, y, 1 - c), device_id_type=MESH).wait_recv()
        for cp in first + passed:
            cp.wait_send()

    return pl.pallas_call(
        body, in_specs=[ANY] * n, out_specs=[ANY] * n,
        out_shape=[jax.ShapeDtypeStruct(a.shape, a.dtype) for a in lands],
        input_output_aliases={k: k for k in range(n)},
        scratch_shapes=[pltpu.SemaphoreType.DMA((n, 3)), pltpu.SemaphoreType.DMA((n, 3)),
                        pltpu.SemaphoreType.DMA((n, 3)), pltpu.SemaphoreType.DMA((n, 3))],
        name="gather_weights")(*lands)


HBM = pl.BlockSpec(memory_space=pltpu.HBM)
SEM = pl.BlockSpec(memory_space=pltpu.SEMAPHORE)
EFFECT = pltpu.SideEffectType.DATAFLOW_SIDE_EFFECTING


def _in_hbm(a):
    return pltpu.with_memory_space_constraint(a, pltpu.HBM)


def _gather_copies(land_refs, send, recv):
    x, y, c = _here()
    own = 2 * x + y
    cps = []
    for k in range(len(land_refs)):
        lead = (slice(None),) * (len(land_refs[k].shape) - 3)
        mine = land_refs[k].at[lead + (own,)]
        for m in (1, 2, 3):
            px, py = _chip_at(x, y, m)
            cps.append(pltpu.make_async_remote_copy(
                src_ref=mine, dst_ref=mine, send_sem=send.at[3 * k + m - 1], recv_sem=recv.at[3 * k + m - 1],
                device_id=(px, py, c), device_id_type=MESH))
    return cps


def _gather_start(lands, after, name):
    n = len(lands)

    def body(*refs):
        lz = refs[:n]
        send, recv = refs[n + 1], refs[n + 2]
        token = refs[-1]
        for cp in _gather_copies(lz, send, recv):
            cp.start()
        token[...] = jnp.zeros_like(token)

    hbm = [pltpu.HBM(a.shape, a.dtype) for a in lands]
    outs = pl.pallas_call(
        body, name=name,
        in_specs=[HBM] * n + [ANY],
        out_specs=[SEM, SEM] + [HBM] * n + [pl.BlockSpec(memory_space=pltpu.VMEM)],
        out_shape=[pltpu.SemaphoreType.DMA((3 * n,)), pltpu.SemaphoreType.DMA((3 * n,))] + hbm
        + [jax.ShapeDtypeStruct((SUBLANES, LANES), F32)],
        input_output_aliases={k: 2 + k for k in range(n)},
        compiler_params=pltpu.CompilerParams(has_side_effects=EFFECT),
    )(*[_in_hbm(a) for a in lands], after)
    return outs[0], outs[1], outs[2:2 + n], outs[-1]


def _gather_wait(send, recv, lands, after, name):
    n = len(lands)

    def body(*refs):
        lz = refs[:n]
        send_r, recv_r = refs[n], refs[n + 1]
        for cp in _gather_copies(lz, send_r, recv_r):
            cp.wait_send()
            cp.wait_recv()

    hbm = [pltpu.HBM(a.shape, a.dtype) for a in lands]
    return pl.pallas_call(
        body, name=name,
        in_specs=[HBM] * n + [SEM, SEM, ANY],
        out_specs=[HBM] * n, out_shape=hbm,
        input_output_aliases={k: k for k in range(n)},
        compiler_params=pltpu.CompilerParams(has_side_effects=EFFECT),
    )(*lands, send, recv, after)


def _exchange_copies(part_refs, slot_refs, send, recv):
    x, y, c = _here()
    cps = []
    for k in range(len(part_refs)):
        for m in (1, 2, 3):
            px, py = _chip_at(x, y, m)
            cps.append(pltpu.make_async_remote_copy(
                src_ref=part_refs[k].at[2 * px + py], dst_ref=slot_refs[k].at[m - 1],
                send_sem=send.at[3 * k + m - 1], recv_sem=recv.at[3 * k + m - 1],
                device_id=(px, py, c), device_id_type=MESH))
    return cps


def _exchange_start(parts, name):
    n = len(parts)
    lands = [lax.empty((N_CHIPS - 1,) + p.shape[1:], p.dtype) for p in parts]

    def body(*refs):
        ins, lz = refs[:n], refs[n:2 * n]
        send, recv = refs[2 * n], refs[2 * n + 1]
        token = refs[-1]
        for cp in _exchange_copies(ins, lz, send, recv):
            cp.start()
        token[...] = jnp.zeros_like(token)

    hbm = [pltpu.HBM(a.shape, a.dtype) for a in list(parts) + lands]
    outs = pl.pallas_call(
        body, name=name,
        in_specs=[HBM] * (2 * n),
        out_specs=[SEM, SEM] + [HBM] * (2 * n) + [pl.BlockSpec(memory_space=pltpu.VMEM)],
        out_shape=[pltpu.SemaphoreType.DMA((3 * n,)), pltpu.SemaphoreType.DMA((3 * n,))] + hbm
        + [jax.ShapeDtypeStruct((SUBLANES, LANES), F32)],
        input_output_aliases={k: 2 + k for k in range(2 * n)},
        compiler_params=pltpu.CompilerParams(has_side_effects=EFFECT),
    )(*[_in_hbm(a) for a in parts], *[_in_hbm(a) for a in lands])
    return outs[0], outs[1], outs[2:2 + n], outs[2 + n:2 + 2 * n], outs[-1]


def _exchange_wait(send, recv, parts, lands, after, name):
    n = len(parts)

    def body(*refs):
        ins, lz = refs[:n], refs[n:2 * n]
        send_r, recv_r = refs[2 * n], refs[2 * n + 1]
        for cp in _exchange_copies(ins, lz, send_r, recv_r):
            cp.wait_send()
            cp.wait_recv()

    hbm = [pltpu.HBM(a.shape, a.dtype) for a in list(parts) + list(lands)]
    outs = pl.pallas_call(
        body, name=name,
        in_specs=[HBM] * (2 * n) + [SEM, SEM, ANY],
        out_specs=[HBM] * (2 * n), out_shape=hbm,
        input_output_aliases={k: k for k in range(2 * n)},
        compiler_params=pltpu.CompilerParams(has_side_effects=EFFECT),
    )(*parts, *lands, send, recv, after)
    return outs[:n], outs[n:]


def _swap_halves_out(grads, name):
    n = len(grads)
    out_shapes = [jax.ShapeDtypeStruct((g.shape[0], g.shape[1] // 2, g.shape[2]), g.dtype) for g in grads]

    def body(*refs):
        ins, outs = refs[:n], refs[n:2 * n]
        send, recv = refs[2 * n:]
        x, y, c = _here()
        cps = []
        for k in range(n):
            rh = ins[k].shape[1] // 2
            cp = pltpu.make_async_remote_copy(
                src_ref=ins[k].at[:, pl.ds((1 - c) * rh, rh), :], dst_ref=outs[k],
                send_sem=send.at[k], recv_sem=recv.at[k], device_id=(x, y, 1 - c), device_id_type=MESH)
            cp.start()
            cps.append(cp)
        for cp in cps:
            cp.wait()

    return pl.pallas_call(
        body, in_specs=[ANY] * n, out_specs=[ANY] * n, out_shape=out_shapes,
        scratch_shapes=[pltpu.SemaphoreType.DMA((n,)), pltpu.SemaphoreType.DMA((n,))],
        name=name)(*grads)


def _swap_copies(grad_refs, land_refs, send, recv):
    x, y, c = _here()
    cps = []
    for k in range(len(grad_refs)):
        rh = grad_refs[k].shape[1] // 2
        cps.append(pltpu.make_async_remote_copy(
            src_ref=grad_refs[k].at[:, pl.ds((1 - c) * rh, rh), :], dst_ref=land_refs[k],
            send_sem=send.at[k], recv_sem=recv.at[k], device_id=(x, y, 1 - c), device_id_type=MESH))
    return cps


def _swap_start(grads, name):
    n = len(grads)
    lands = [lax.empty((g.shape[0], g.shape[1] // 2, g.shape[2]), g.dtype) for g in grads]

    def body(*refs):
        ins, lz = refs[:n], refs[n:2 * n]
        send, recv = refs[2 * n], refs[2 * n + 1]
        token = refs[-1]
        for cp in _swap_copies(ins, lz, send, recv):
            cp.start()
        token[...] = jnp.zeros_like(token)

    hbm = [pltpu.HBM(a.shape, a.dtype) for a in list(grads) + lands]
    outs = pl.pallas_call(
        body, name=name,
        in_specs=[HBM] * (2 * n),
        out_specs=[SEM, SEM] + [HBM] * (2 * n) + [pl.BlockSpec(memory_space=pltpu.VMEM)],
        out_shape=[pltpu.SemaphoreType.DMA((n,)), pltpu.SemaphoreType.DMA((n,))] + hbm
        + [jax.ShapeDtypeStruct((SUBLANES, LANES), F32)],
        input_output_aliases={k: 2 + k for k in range(2 * n)},
        compiler_params=pltpu.CompilerParams(has_side_effects=EFFECT),
    )(*[_in_hbm(a) for a in grads], *[_in_hbm(a) for a in lands])
    return outs[0], outs[1], outs[2:2 + n], outs[2 + n:2 + 2 * n], outs[-1]


def _swap_wait(send, recv, grads, lands, after, name):
    n = len(grads)

    def body(*refs):
        ins, lz = refs[:n], refs[n:2 * n]
        send_r, recv_r = refs[2 * n], refs[2 * n + 1]
        for cp in _swap_copies(ins, lz, send_r, recv_r):
            cp.wait_send()
            cp.wait_recv()

    hbm = [pltpu.HBM(a.shape, a.dtype) for a in list(grads) + list(lands)]
    outs = pl.pallas_call(
        body, name=name,
        in_specs=[HBM] * (2 * n) + [SEM, SEM, ANY],
        out_specs=[HBM] * (2 * n), out_shape=hbm,
        input_output_aliases={k: k for k in range(2 * n)},
        compiler_params=pltpu.CompilerParams(has_side_effects=EFFECT),
    )(*grads, *lands, send, recv, after)
    return outs[:n], outs[n:]


def _add_cast(g, other, cidx, name):
    ns, R, Cc = g.shape
    rh = R // 2
    tr = _tile(rh, max(16, (1 << 19) // Cc), 16)
    nb = rh // tr

    def body(c_ref, g_ref, o_ref, s_ref):
        s_ref[...] = (g_ref[...] + o_ref[...]).astype(BF16)

    return pl.pallas_call(
        body,
        grid_spec=pltpu.PrefetchScalarGridSpec(
            num_scalar_prefetch=1, grid=(ns, nb),
            in_specs=[pl.BlockSpec((None, tr, Cc), lambda k, i, c: (k, c[0] * nb + i, 0)),
                      pl.BlockSpec((None, tr, Cc), lambda k, i, c: (k, i, 0))],
            out_specs=pl.BlockSpec((None, tr, Cc), lambda k, i, c: (k, i, 0))),
        out_shape=jax.ShapeDtypeStruct((ns, rh, Cc), BF16),
        compiler_params=_params("parallel", "parallel"), name=name)(cidx, g, other)


def _sum_slots(part, got, idx, name):
    ns, rh, Cc = got.shape
    tr = _tile(rh, max(16, (1 << 18) // Cc), 16)
    nb = rh // tr

    def body(i_ref, p_ref, b_ref, o_ref):
        acc = p_ref[...].astype(F32)
        for m in range(ns):
            acc = acc + b_ref[m].astype(F32)
        o_ref[...] = acc

    return pl.pallas_call(
        body,
        grid_spec=pltpu.PrefetchScalarGridSpec(
            num_scalar_prefetch=1, grid=(nb,),
            in_specs=[pl.BlockSpec((None, tr, Cc), lambda i, s: (s[1], i, 0)),
                      pl.BlockSpec((ns, tr, Cc), lambda i, s: (0, i, 0))],
            out_specs=pl.BlockSpec((tr, Cc), lambda i, s: (s[0] * nb + i, 0))),
        out_shape=jax.ShapeDtypeStruct((2 * rh, Cc), F32),
        compiler_params=_params("parallel"), name=name)(idx, part, got)


def _share_halves(blocks, name):
    n = len(blocks)

    def body(*refs):
        ins, outs = refs[:n], refs[n:2 * n]
        send, recv = refs[2 * n:]
        x, y, c = _here()
        cps = []
        for k in range(n):
            rh = outs[k].shape[0] // 2
            mine = outs[k].at[pl.ds(c * rh, rh), :]
            cp = pltpu.make_async_remote_copy(
                src_ref=mine, dst_ref=mine, send_sem=send.at[k], recv_sem=recv.at[k],
                device_id=(x, y, 1 - c), device_id_type=MESH)
            cp.start()
            cps.append(cp)
        for cp in cps:
            cp.wait()

    return pl.pallas_call(
        body, in_specs=[ANY] * n, out_specs=[ANY] * n,
        out_shape=[jax.ShapeDtypeStruct(b.shape, b.dtype) for b in blocks],
        input_output_aliases={k: k for k in range(n)},
        scratch_shapes=[pltpu.SemaphoreType.DMA((n,)), pltpu.SemaphoreType.DMA((n,))],
        name=name)(*blocks)


def _small_copies(p_ref, slot_ref, send, recv):
    x, y, c = _here()
    mine = slot_ref.at[4 * x + 2 * y + c]
    cps = []
    for m in range(1, N_DEV):
        peer = (x ^ (m >> 2), y ^ ((m >> 1) & 1), c ^ (m & 1))
        cps.append(pltpu.make_async_remote_copy(
            src_ref=p_ref, dst_ref=mine, send_sem=send.at[m - 1], recv_sem=recv.at[m - 1],
            device_id=peer, device_id_type=MESH))
    return cps


def _small_start(packed):
    slots = lax.empty((N_DEV,) + packed.shape, packed.dtype)

    def body(p_ref, s_ref, send, recv, p_thru, s_thru, token):
        for cp in _small_copies(p_ref, s_ref, send, recv):
            cp.start()
        token[...] = jnp.zeros_like(token)

    return pl.pallas_call(
        body, name="small_start",
        in_specs=[HBM, HBM],
        out_specs=[SEM, SEM, HBM, HBM, pl.BlockSpec(memory_space=pltpu.VMEM)],
        out_shape=[pltpu.SemaphoreType.DMA((N_DEV - 1,)), pltpu.SemaphoreType.DMA((N_DEV - 1,)),
                   pltpu.HBM(packed.shape, packed.dtype), pltpu.HBM(slots.shape, slots.dtype),
                   jax.ShapeDtypeStruct((SUBLANES, LANES), F32)],
        input_output_aliases={0: 2, 1: 3},
        compiler_params=pltpu.CompilerParams(has_side_effects=EFFECT),
    )(_in_hbm(packed), _in_hbm(slots))


def _small_wait(send, recv, packed, slots, after):
    def body(p_ref, s_ref, send_r, recv_r, after_ref, p_out, s_out):
        for cp in _small_copies(p_ref, s_ref, send_r, recv_r):
            cp.wait_send()
            cp.wait_recv()

    return pl.pallas_call(
        body, name="small_wait",
        in_specs=[HBM, HBM, SEM, SEM, ANY], out_specs=[HBM, HBM],
        out_shape=[pltpu.HBM(packed.shape, packed.dtype), pltpu.HBM(slots.shape, slots.dtype)],
        input_output_aliases={0: 0, 1: 1},
        compiler_params=pltpu.CompilerParams(has_side_effects=EFFECT),
    )(packed, slots, send, recv, after)


def _sum_devices(packed, slots, me):
    n, R, _ = slots.shape
    tr = _tile(R, 1024)

    def body(m_ref, p_ref, s_ref, o_ref):
        own = p_ref[...]
        acc = None
        for d in range(n):
            term = jnp.where(m_ref[0] == d, own, s_ref[d])
            acc = term if acc is None else acc + term
        o_ref[...] = acc

    return pl.pallas_call(
        body,
        grid_spec=pltpu.PrefetchScalarGridSpec(
            num_scalar_prefetch=1, grid=(R // tr,),
            in_specs=[pl.BlockSpec((tr, LANES), lambda i, m: (i, 0)),
                      pl.BlockSpec((n, tr, LANES), lambda i, m: (0, i, 0))],
            out_specs=pl.BlockSpec((tr, LANES), lambda i, m: (i, 0))),
        out_shape=jax.ShapeDtypeStruct((R, LANES), F32),
        compiler_params=_params("parallel"), name="sum_devices")(me, packed, slots)


def _pack(arrs):
    rows, parts = [], []
    for a in arrs:
        flat = a.reshape(-1)
        r = -(-flat.shape[0] // (SUBLANES * LANES)) * SUBLANES
        parts.append(jnp.pad(flat, (0, r * LANES - flat.shape[0])).reshape(r, LANES))
        rows.append(r)
    return jnp.concatenate(parts, axis=0), rows


def _unpack(packed, rows, shapes):
    out, r0 = [], 0
    for r, shp in zip(rows, shapes):
        size = math.prod(shp)
        out.append(packed[r0:r0 + r].reshape(-1)[:size].reshape(shp))
        r0 += r
    return out


def _block_diag(w, per):
    H, dh, _ = w.shape
    w4 = w.reshape(H // per, per, dh, dh)
    eye = jnp.eye(per, dtype=w.dtype)
    return (w4[:, :, :, None, :] * eye[None, :, None, :, None]).reshape(H // per, per * dh, per * dh)


def _block_diag_take(d, per):
    n, s, _ = d.shape
    dh = s // per
    d5 = d.reshape(n, per, dh, per, dh)
    return jnp.stack([d5[:, h, :, h, :] for h in range(per)], axis=1).reshape(n * per, dh, dh)


def kernel(x, ffn1_norm, ffn1_w_gate, ffn1_w_up, ffn1_w_down, mix_norm, w_in, conv_dw, conv_dw_bias, conv_ln_g, conv_ln_b, lru_conv_w, lru_conv_b, lru_w_a, lru_b_a, lru_w_x, lru_b_x, lru_lambda, w_out, ffn2_norm, ffn2_w_gate, ffn2_w_up, ffn2_w_down, final_norm, loss_target, m_ffn1_norm, m_ffn1_w_gate, m_ffn1_w_up, m_ffn1_w_down, m_mix_norm, m_w_in, m_conv_dw, m_conv_dw_bias, m_conv_ln_g, m_conv_ln_b, m_lru_conv_w, m_lru_conv_b, m_lru_w_a, m_lru_b_a, m_lru_w_x, m_lru_b_x, m_lru_lambda, m_w_out, m_ffn2_norm, m_ffn2_w_gate, m_ffn2_w_up, m_ffn2_w_down, m_final_norm, v_ffn1_norm, v_ffn1_w_gate, v_ffn1_w_up, v_ffn1_w_down, v_mix_norm, v_w_in, v_conv_dw, v_conv_dw_bias, v_conv_ln_g, v_conv_ln_b, v_lru_conv_w, v_lru_conv_b, v_lru_w_a, v_lru_b_a, v_lru_w_x, v_lru_b_x, v_lru_lambda, v_w_out, v_ffn2_norm, v_ffn2_w_gate, v_ffn2_w_up, v_ffn2_w_down, v_final_norm):
    names = ['ffn1_norm', 'ffn1_w_gate', 'ffn1_w_up', 'ffn1_w_down', 'mix_norm', 'w_in', 'conv_dw', 'conv_dw_bias',
             'conv_ln_g', 'conv_ln_b', 'lru_conv_w', 'lru_conv_b', 'lru_w_a', 'lru_b_a', 'lru_w_x', 'lru_b_x',
             'lru_lambda', 'w_out', 'ffn2_norm', 'ffn2_w_gate', 'ffn2_w_up', 'ffn2_w_down', 'final_norm']
    env = dict(locals())
    W = {n: env[n] for n in names}
    M = {n: env['m_' + n] for n in names}
    V = {n: env['v_' + n] for n in names}

    xi, yi, ci = _here()
    chip = 2 * xi + yi
    cidx = ci.astype(jnp.int32).reshape(1)
    T, D = x.shape[-2], x.shape[-1]
    xs = x.reshape(T, D)
    tgt = loss_target.reshape(T, D)
    K, Cs = conv_dw.shape
    C = conv_dw_bias.shape[0]
    Wl = lru_conv_b.shape[0]
    K4 = lru_conv_w.shape[0]
    heads, dh, _ = lru_w_a.shape
    per = LANES // dh

    def row(v):
        return v.reshape(1, -1)

    tform = ('ffn1_w_gate', 'ffn1_w_up', 'ffn2_w_gate', 'ffn2_w_up')
    for n in tform:
        W[n], M[n], V[n] = W[n].T, M[n].T, V[n].T
    kp = -(-K // SUBLANES) * SUBLANES
    taps = jnp.concatenate([conv_dw, jnp.zeros((kp - K, Cs), F32), lru_conv_w,
                            jnp.zeros((2 * SUBLANES - K4, Cs), F32)], axis=0)
    idx = jnp.stack([ci, chip]).astype(jnp.int32)
    (wff1,) = _gather_weights([_place_cast([W['ffn1_w_gate'], W['ffn1_w_up'], ffn1_w_down], idx, BF16, "place_ffn1")])
    mixl = [_place_cast([w_in], idx, BF16, "place_w_in"), _place_cast([w_out], idx, BF16, "place_w_out"),
            _place_cast([taps], idx, F32, "place_taps")]
    msend, mrecv, mixl, mtok = _gather_start(mixl, wff1, "gather_mix_start")
    ff2l = _place_cast([W['ffn2_w_gate'], W['ffn2_w_up'], ffn2_w_down], idx, BF16, "place_ffn2")
    fsend, frecv, ff2l, ftok = _gather_start([ff2l], mtok, "gather_ffn2_start")
    wa_bd = _block_diag(lru_w_a, per).astype(BF16)
    wx_bd = _block_diag(lru_w_x, per).astype(BF16)

    x1, a1, b1 = _ffn_fwd(xs, row(ffn1_norm) + ftok[0:1, 0:1], wff1, "ffn1_fwd")
    win, wout, taps = _gather_wait(msend, mrecv, mixl, x1, "gather_mix_wait")
    win, wout, taps = win[0], wout.reshape(-1, D), taps[0]
    conv_w_full = taps[:, :K].transpose(1, 0, 2).reshape(K, N_CHIPS * Cs)
    lru_w4_full = taps[:, kp:kp + K4].transpose(1, 0, 2).reshape(K4, N_CHIPS * Cs)
    z = _mix_in_fwd(x1, row(mix_norm), win)
    u, u1 = _conv_fwd(z, conv_w_full, row(conv_dw_bias), row(conv_ln_g), row(conv_ln_b))
    yr, hs = _lru_fwd(z, 2 * C, lru_w4_full, row(lru_conv_b), wa_bd, row(lru_b_a), wx_bd, row(lru_b_x),
                      row(lru_lambda))
    x2 = _mix_out_fwd(x1, u, yr, wout)
    (wff2,) = _gather_wait(fsend, frecv, ff2l, x2, "gather_ffn2_wait")
    dx3, a2, b2, loss_blk, d_final = _ffn_fwd(x2, row(ffn2_norm), wff2, "ffn2_fwd", head=(row(final_norm), tgt))

    dx2, da2, db2, p2, hb2, dyh2, d_ffn2n = _ffn_bwd_tok(dx3, x2, row(ffn2_norm), a2, b2, wff2, "ffn2_bwd")
    dwg2, dwu2, dwd2 = _ffn_wgrad([([da2, db2], hb2), ([p2], dyh2)], ftok, "ffn2_wgrad")
    wsend, wrecv, f2g, f2o, wtok = _swap_start([dwg2, dwu2, dwd2], "swap_ffn2_start")
    dcat, dwout = _mix_out_bwd(dx2, u, yr, wout)
    dzc, cst = _conv_bwd(dcat, u1, z, conv_w_full, row(conv_ln_g) + wtok[0:1, 0:1], row(conv_ln_b))
    dzx, dzg, lst, dwa_bd, dwx_bd = _lru_bwd(dcat, C, hs, z, 2 * C, lru_w4_full, row(lru_conv_b), wa_bd,
                                              row(lru_b_a), wx_bd, row(lru_b_x), row(lru_lambda))
    dx1, dwin, d_mixn = _mix_in_bwd(dzc, dzx, dzg, x1, dx2, row(mix_norm), win)

    early_names = ['w_in', 'w_out', 'ffn2_w_gate', 'ffn2_w_up', 'ffn2_w_down']
    mixg = [dwin, dwout.reshape(N_CHIPS, -1, D)]
    mixo = _swap_halves_out(mixg, "swap_halves_mix")
    f2g, f2o = _swap_wait(wsend, wrecv, f2g, f2o, dwin, "swap_ffn2_wait")
    e_parts = [_add_cast(g, o, cidx, "add_cast_" + n)
               for g, o, n in zip(mixg + list(f2g), list(mixo) + list(f2o), early_names)]
    esend, erecv, e_parts, e_lands, etok = _exchange_start(e_parts, "exchange_early_start")

    dx0, da1, db1, p1, hb1, dyh1, d_ffn1n = _ffn_bwd_tok(dx1, xs, row(ffn1_norm) + etok[0:1, 0:1], a1, b1, wff1,
                                                         "ffn1_bwd")

    small_names = ['ffn1_norm', 'mix_norm', 'conv_dw', 'conv_dw_bias', 'conv_ln_g', 'conv_ln_b', 'lru_conv_w',
                   'lru_conv_b', 'lru_w_a', 'lru_b_a', 'lru_w_x', 'lru_b_x', 'lru_lambda', 'ffn2_norm',
                   'final_norm']
    small = {
        'ffn1_norm': d_ffn1n, 'mix_norm': d_mixn, 'conv_dw': cst[:K], 'conv_dw_bias': cst[K + 1],
        'conv_ln_g': cst[K + 2], 'conv_ln_b': cst[K + 3], 'lru_conv_w': lst[:K4], 'lru_conv_b': lst[K4],
        'lru_w_a': _block_diag_take(dwa_bd, per), 'lru_b_a': lst[K4 + 1],
        'lru_w_x': _block_diag_take(dwx_bd, per), 'lru_b_x': lst[K4 + 2], 'lru_lambda': lst[K4 + 3],
        'ffn2_norm': d_ffn2n, 'final_norm': d_final,
    }
    packed, rows = _pack([small[n] for n in small_names] + [loss_blk[0:1, 0:1]])
    ssend, srecv, packed, sslots, stok = _small_start(packed)

    gu_names, d_names = ['ffn1_w_gate', 'ffn1_w_up'], ['ffn1_w_down']
    gu = _ffn_wgrad([([da1, db1], hb1)], stok, "ffn1_wgrad_gu")
    gu_parts = [_add_cast(g, o, cidx, "add_cast_" + n)
                for g, o, n in zip(gu, _swap_halves_out(gu, "swap_halves_gu"), gu_names)]
    gsend, grecv, gu_parts, gu_lands, gtok = _exchange_start(gu_parts, "exchange_gu_start")
    dn = _ffn_wgrad([([p1], dyh1)], gtok, "ffn1_wgrad_d")
    dwd1 = dn[0]
    d_parts = [_add_cast(g, o, cidx, "add_cast_" + n)
               for g, o, n in zip(dn, _swap_halves_out(dn, "swap_halves_d"), d_names)]
    dsend, drecv, d_parts, d_lands, ltok = _exchange_start(d_parts, "exchange_d_start")
    e_parts, e_slots = _exchange_wait(esend, erecv, e_parts, e_lands, ltok, "exchange_early_wait")
    delta, new_m, new_v = {}, {}, {}

    def finish(group, parts, slots, tag):
        halves = [_sum_slots(p, b, idx, "sum_slots_" + n) for p, b, n in zip(parts, slots, group)]
        for n, g in zip(group, _share_halves(halves, "share_halves_" + tag)):
            G[n] = g
            delta[n], new_m[n], new_v[n] = _adamw(W[n], g, M[n], V[n], "adamw_" + n)

    G = {}
    finish(early_names, e_parts, e_slots, "early")

    full_shapes = [(K, C) if n == 'conv_dw' else (K4, Wl) if n == 'lru_conv_w' else W[n].shape for n in small_names]
    packed, sslots = _small_wait(ssend, srecv, packed, sslots, dwd1)
    summed = _sum_devices(packed, sslots, (4 * xi + 2 * yi + ci).astype(jnp.int32).reshape(1))
    *small_sums, loss_sum = _unpack(summed, rows, full_shapes + [(1, 1)])
    for n, gsum in zip(small_names, small_sums):
        if n == 'conv_dw':
            gsum = lax.dynamic_slice_in_dim(gsum, chip * Cs, Cs, axis=1)
        elif n == 'lru_conv_w':
            gsum = lax.dynamic_slice_in_dim(gsum, chip * lru_conv_w.shape[1], lru_conv_w.shape[1], axis=1)
        G[n] = gsum

    pw, prow = _pack([W[n] for n in small_names])
    pg, _ = _pack([G[n] for n in small_names])
    pm, _ = _pack([M[n] for n in small_names])
    pv, _ = _pack([V[n] for n in small_names])
    sd, sm, sv = _adamw(pw, pg, pm, pv, "adamw_small")
    shapes = [W[n].shape for n in small_names]
    for n, a, b, c_ in zip(small_names, _unpack(sd, prow, shapes), _unpack(sm, prow, shapes),
                           _unpack(sv, prow, shapes)):
        delta[n], new_m[n], new_v[n] = a, b, c_

    done = sd[0:SUBLANES] + delta[early_names[-1]][0:SUBLANES, 0:LANES]
    gu_parts, gu_slots = _exchange_wait(gsend, grecv, gu_parts, gu_lands, done, "exchange_gu_wait")
    d_parts, d_slots = _exchange_wait(dsend, drecv, d_parts, d_lands, gu_slots[0], "exchange_d_wait")
    finish(gu_names + d_names, list(gu_parts) + list(d_parts), list(gu_slots) + list(d_slots), "last")

    loss = loss_sum[0, 0]
    grad_x = dx0.reshape(x.shape)
    for n in tform:
        G[n], delta[n], new_m[n], new_v[n] = G[n].T, delta[n].T, new_m[n].T, new_v[n].T
    return (loss, grad_x, *[G[n] for n in names], *[delta[n] for n in names],
            *[new_m[n] for n in names], *[new_v[n] for n in names])
```

```python
import functools
import math

import jax
import jax.numpy as jnp
from jax import lax
from jax.experimental import pallas as pl
from jax.experimental.pallas import tpu as pltpu

F32 = jnp.float32
BF16 = jnp.bfloat16
MESH = pl.DeviceIdType.MESH

RMS_EPS = 1e-6
LN_EPS = 1e-5
LRU_C = 8.0
FFN_RES_SCALE = 0.5
ADAM_LR = 0.001
ADAM_B1 = 0.9
ADAM_B2 = 0.999
ADAM_EPS = 1e-08
ADAM_WD = 0.01
ADAM_STEP = 10

LANES = 128
SUBLANES = 8
CONV_HALO = 32
LRU_HALO = 8
ROW_CHUNK = 64
VMEM_LIMIT = 56 * 1024 * 1024
N_CHIPS = 4
N_DEV = 8
TOK_TILE = 1024
BWD_TILE = 512
FFN_BWD_TILE = 512
BWD_ROWS = 32
FFN_BWD_CHAIN = 256
CONV_TILE = 512
LRU_TILE = 2048
LRU_GROUPS = 8


def _dot(a, b):
    return jnp.dot(a, b, preferred_element_type=F32)


def _dot_nt(a, b):
    return lax.dot_general(a, b, (((1,), (1,)), ((), ())), preferred_element_type=F32)


def _dot_tn(a, b):
    return lax.dot_general(a, b, (((0,), (0,)), ((), ())), preferred_element_type=F32)


def _tile(n, pref, mult=SUBLANES):
    for t in range(min(pref, n), 0, -1):
        if n % t == 0 and t % mult == 0:
            return t
    return n


def _params(*sem):
    return pltpu.CompilerParams(dimension_semantics=sem, vmem_limit_bytes=VMEM_LIMIT)


def _rms_stats(x):
    r = lax.rsqrt(jnp.mean(x * x, axis=-1, keepdims=True) + RMS_EPS)
    return x * r, r


def _rms_bwd(dh, xh, r, g):
    dxh = dh * g
    return r * (dxh - xh * jnp.mean(dxh * xh, axis=-1, keepdims=True))


def _colsum(v):
    return jnp.sum(v, axis=0, keepdims=True)


def _ffn_fwd(x, g, wff, name, head=None):
    T, D = x.shape
    ns, fs = wff.shape[1], wff.shape[2]
    tm = _tile(T, TOK_TILE)
    mc = _tile(tm, FFN_BWD_CHAIN, 16)
    rc = _tile(tm, FFN_BWD_CHAIN)

    def body(*refs):
        x_ref, g_ref, wg_ref, wu_ref, wd_ref = refs[:5]
        if head is None:
            y_ref, a_ref, b_ref, hb_ref, acc_ref = refs[5:]
        else:
            gf_ref, t_ref, y_ref, a_ref, b_ref, loss_ref, dgf_ref, hb_ref, acc_ref = refs[5:]
        j = pl.program_id(1)

        @pl.when(j == 0)
        def _():
            xh, _ = _rms_stats(x_ref[...])
            hb_ref[...] = (xh * g_ref[...]).astype(BF16)
            acc_ref[...] = jnp.zeros_like(acc_ref)

        if head is not None:
            @pl.when((pl.program_id(0) == 0) & (j == 0))
            def _():
                loss_ref[...] = jnp.zeros_like(loss_ref)
                dgf_ref[...] = jnp.zeros_like(dgf_ref)

        for q0 in range(0, tm, mc):
            blk = pl.ds(q0, mc)
            hb = hb_ref[blk, :]
            a = _dot_nt(hb, wg_ref[...])
            b = _dot_nt(hb, wu_ref[...])
            a_ref[blk, :] = a.astype(BF16)
            b_ref[blk, :] = b.astype(BF16)
            p = (a * jax.nn.sigmoid(a) * b).astype(BF16)
            acc_ref[blk, :] += _dot(p, wd_ref[...])

        @pl.when(j == ns - 1)
        def _():
            if head is None:
                y_ref[...] = x_ref[...] + FFN_RES_SCALE * acc_ref[...]
                return
            gv = gf_ref[...]
            loss = jnp.zeros((), F32)
            dg = jnp.zeros((1, D), F32)
            for r0 in range(0, tm, rc):
                rows = pl.ds(r0, rc)
                xh, r = _rms_stats(x_ref[rows, :] + FFN_RES_SCALE * acc_ref[rows, :])
                e = xh * gv - t_ref[rows, :]
                loss = loss + 0.5 * jnp.sum(jnp.mean(e * e, axis=-1, keepdims=True))
                dy = e * (1.0 / D)
                dg = dg + _colsum(dy * xh)
                y_ref[rows, :] = _rms_bwd(dy, xh, r, gv)
            loss_ref[...] += loss
            dgf_ref[...] += dg

    def wspec(n):
        return pl.BlockSpec((None, None, fs, D), lambda i, j: (n, j, 0, 0))

    tok = pl.BlockSpec((tm, D), lambda i, j: (i, 0))
    vec = pl.BlockSpec((1, D), lambda i, j: (0, 0))
    mid = pl.BlockSpec((None, tm, fs), lambda i, j: (j, i, 0))
    in_specs = [tok, vec, wspec(0), wspec(1), wspec(2)]
    out_specs = [tok, mid, mid]
    out_shape = [jax.ShapeDtypeStruct((T, D), F32), jax.ShapeDtypeStruct((ns, T, fs), BF16),
                 jax.ShapeDtypeStruct((ns, T, fs), BF16)]
    args = [x, g, wff, wff, wff]
    if head is not None:
        in_specs += [vec, tok]
        out_specs += [pl.BlockSpec((SUBLANES, LANES), lambda i, j: (0, 0)), vec]
        out_shape += [jax.ShapeDtypeStruct((SUBLANES, LANES), F32), jax.ShapeDtypeStruct((1, D), F32)]
        args += list(head)
    return pl.pallas_call(
        body, grid=(T // tm, ns), in_specs=in_specs, out_specs=out_specs, out_shape=out_shape,
        scratch_shapes=[pltpu.VMEM((tm, D), BF16), pltpu.VMEM((tm, D), F32)],
        compiler_params=_params("arbitrary", "arbitrary"), name=name)(*args)


def _ffn_bwd_tok(dy, x, g, a, b, wff, name):
    T, D = x.shape
    ns, fs = wff.shape[1], wff.shape[2]
    tm = _tile(T, FFN_BWD_TILE)
    rc = _tile(tm, BWD_ROWS)
    mc = _tile(tm, FFN_BWD_CHAIN, rc)

    def body(dy_ref, x_ref, g_ref, a_ref, b_ref, w_ref,
             dx_ref, da_ref, db_ref, p_ref, hb_ref, dyh_ref, dg_ref, dh_ref, dp_ref):
        i, j = pl.program_id(0), pl.program_id(1)
        cur = dp_ref.at[j % 2]
        nxt = dp_ref.at[(j + 1) % 2]
        wg_ref, wu_ref = w_ref.at[0, j], w_ref.at[1, j]
        wd0_ref, wdn_ref = w_ref.at[2, 0], w_ref.at[2, jnp.minimum(j + 1, ns - 1)]

        @pl.when((i == 0) & (j == 0))
        def _():
            dg_ref[...] = jnp.zeros_like(dg_ref)

        @pl.when(j == 0)
        def _():
            for r0 in range(0, tm, rc):
                rows = pl.ds(r0, rc)
                xh, _ = _rms_stats(x_ref[rows, :])
                hb_ref[rows, :] = (xh * g_ref[...]).astype(BF16)
                dyh_ref[rows, :] = (FFN_RES_SCALE * dy_ref[rows, :]).astype(BF16)
            dh_ref[...] = jnp.zeros_like(dh_ref)
            cur[...] = _dot_nt(dyh_ref[...], wd0_ref[...])

        def chains(with_next):
            for q0 in range(0, tm, mc):
                blk = pl.ds(q0, mc)
                for r0 in range(q0, q0 + mc, rc):
                    rows = pl.ds(r0, rc)
                    av = a_ref[rows, :].astype(F32)
                    bv = b_ref[rows, :].astype(F32)
                    dp = cur[rows, :]
                    s = jax.nn.sigmoid(av)
                    sl = av * s
                    da_ref[rows, :] = (dp * bv * (s * (1.0 + av * (1.0 - s)))).astype(BF16)
                    db_ref[rows, :] = (dp * sl).astype(BF16)
                    p_ref[rows, :] = (sl * bv).astype(BF16)
                if with_next:
                    nxt[blk, :] = _dot_nt(dyh_ref[blk, :], wdn_ref[...])
                dh_ref[blk, :] += _dot(da_ref[blk, :], wg_ref[...]) + _dot(db_ref[blk, :], wu_ref[...])

        pl.when(j < ns - 1)(functools.partial(chains, True))
        pl.when(j == ns - 1)(functools.partial(chains, False))

        @pl.when(j == ns - 1)
        def _():
            gv = g_ref[...]
            dg = jnp.zeros((1, D), F32)
            for r0 in range(0, tm, rc):
                rows = pl.ds(r0, rc)
                xh, r = _rms_stats(x_ref[rows, :])
                dh = dh_ref[rows, :]
                dx_ref[rows, :] = dy_ref[rows, :] + _rms_bwd(dh, xh, r, gv)
                dg = dg + _colsum(dh * xh)
            dg_ref[...] += dg

    tok = pl.BlockSpec((tm, D), lambda i, j: (i, 0))
    mid = pl.BlockSpec((None, tm, fs), lambda i, j: (j, i, 0))
    vec = pl.BlockSpec((1, D), lambda i, j: (0, 0))
    return pl.pallas_call(
        body, grid=(T // tm, ns),
        in_specs=[tok, tok, vec, mid, mid,
                  pl.BlockSpec(wff.shape, lambda i, j: (0, 0, 0, 0), pipeline_mode=pl.Buffered(1))],
        out_specs=[tok, mid, mid, mid, tok, tok, vec],
        out_shape=[jax.ShapeDtypeStruct((T, D), F32),
                   jax.ShapeDtypeStruct((ns, T, fs), BF16), jax.ShapeDtypeStruct((ns, T, fs), BF16),
                   jax.ShapeDtypeStruct((ns, T, fs), BF16),
                   jax.ShapeDtypeStruct((T, D), BF16), jax.ShapeDtypeStruct((T, D), BF16),
                   jax.ShapeDtypeStruct((1, D), F32)],
        scratch_shapes=[pltpu.VMEM((tm, D), F32), pltpu.VMEM((2, tm, fs), F32)],
        compiler_params=_params("arbitrary", "arbitrary"), name=name)(dy, x, g, a, b, wff)


def _ffn_wgrad(groups, after, name, swap=False):
    flat = [(l, gi) for gi, (ls, _) in enumerate(groups) for l in ls]
    ng, n = len(groups), len(flat)
    T, D = groups[0][1].shape
    ns, _, fs = flat[0][0].shape
    tm = _tile(T, TOK_TILE)
    nI = T // tm
    rh = fs // 2

    def body(*refs):
        rhs_refs, lhs_refs, out_refs = refs[:ng], refs[ng:ng + n], refs[ng + n + 1:ng + 2 * n + 1]
        j, i = pl.program_id(0), pl.program_id(1)

        @pl.when(i == 0)
        def _():
            for o in out_refs:
                o[...] = jnp.zeros_like(o)

        rvs = [r[...] for r in rhs_refs]
        for l, o, (_, gi) in zip(lhs_refs, out_refs, flat):
            o[...] += _dot_tn(l[...], rvs[gi])

        if swap:
            land_refs = refs[ng + 2 * n + 1:ng + 3 * n + 1]
            send, recv, stage = refs[ng + 3 * n + 1:]
            x, y, c = _here()

            def copies(jj):
                return [pltpu.make_async_remote_copy(
                    src_ref=stage.at[jj % 2, k], dst_ref=land_refs[k].at[jj],
                    send_sem=send.at[k * ns + jj], recv_sem=recv.at[k * ns + jj],
                    device_id=(x, y, 1 - c), device_id_type=MESH) for k in range(n)]

            @pl.when(i == nI - 1)
            def _():
                theirs = pl.ds(pl.multiple_of((1 - c) * rh, SUBLANES), rh)
                for k in range(n):
                    stage[j % 2, k] = out_refs[k][theirs, :]
                for cp in copies(j):
                    cp.start()

            @pl.when((i == nI - 1) & (j > 0))
            def _():
                for cp in copies(j - 1):
                    cp.wait_send()

            @pl.when((i == nI - 1) & (j == ns - 1))
            def _():
                for cp in copies(j):
                    cp.wait_send()
                for jj in range(ns):
                    for cp in copies(jj):
                        cp.wait_recv()

    tok = pl.BlockSpec((tm, D), lambda j, i: (i, 0))
    mid = pl.BlockSpec((None, tm, fs), lambda j, i: (j, i, 0))
    wsp = pl.BlockSpec((None, fs, D), lambda j, i: (j, 0, 0))
    sds = jax.ShapeDtypeStruct((ns, fs, D), F32)
    out_specs, out_shape, scratch = [wsp] * n, [sds] * n, []
    if swap:
        out_specs += [ANY] * n
        out_shape += [jax.ShapeDtypeStruct((ns, rh, D), F32)] * n
        scratch = [pltpu.SemaphoreType.DMA((n * ns,)), pltpu.SemaphoreType.DMA((n * ns,)),
                   pltpu.VMEM((2, n, rh, D), F32)]
    return pl.pallas_call(
        body, grid=(ns, nI),
        in_specs=[tok] * ng + [mid] * n + [pl.BlockSpec((SUBLANES, LANES), lambda j, i: (0, 0))],
        out_specs=out_specs, out_shape=out_shape, scratch_shapes=scratch,
        compiler_params=_params("arbitrary", "arbitrary"), name=name)(
            *[r for _, r in groups], *[l for l, _ in flat], after)


def _mix_in_fwd(x, g, win):
    T, D = x.shape
    ns, ws = win.shape[0], win.shape[2]
    tm = _tile(T, TOK_TILE)

    def body(x_ref, g_ref, w_ref, z_ref):
        xh, _ = _rms_stats(x_ref[...])
        hb = (xh * g_ref[...]).astype(BF16)
        for j in range(ns):
            z_ref[:, pl.ds(j * ws, ws)] = _dot(hb, w_ref[j])

    return pl.pallas_call(
        body, grid=(T // tm,),
        in_specs=[pl.BlockSpec((tm, D), lambda i: (i, 0)), pl.BlockSpec((1, D), lambda i: (0, 0)),
                  pl.BlockSpec((ns, D, ws), lambda i: (0, 0, 0), pipeline_mode=pl.Buffered(1))],
        out_specs=pl.BlockSpec((tm, ns * ws), lambda i: (i, 0)),
        out_shape=jax.ShapeDtypeStruct((T, ns * ws), F32),
        compiler_params=_params("parallel"), name="mix_in_fwd")(x, g, win)


def _tap_sum(buf, w_ref, ntaps, first_row, r0, rows, flip):
    acc = None
    for k in range(ntaps):
        off = (ntaps - 1 - k) if flip else k
        t = buf[pl.ds(first_row + r0 + off, rows), :] * w_ref[pl.ds(k, 1), :]
        acc = t if acc is None else acc + t
    return acc


def _shift_copies(buf, sh, rows):
    for r in range(1, SUBLANES):
        sh[r - 1, pl.ds(0, rows), :] = buf[pl.ds(r, rows), :]


def _tap_rows(buf, sh, off, r0, rows):
    r = off % SUBLANES
    if r == 0:
        return buf[pl.ds(off + r0, rows), :]
    return sh[r - 1, pl.ds(off - r + r0, rows), :]


def _tap_sum_tiles(buf, sh, w_ref, ntaps, first_row, r0, rows, flip):
    acc = None
    for k in range(ntaps):
        off = first_row + ((ntaps - 1 - k) if flip else k)
        t = _tap_rows(buf, sh, off, r0, rows) * w_ref[pl.ds(k, 1), :]
        acc = t if acc is None else acc + t
    return acc


def _conv_fwd(z, w, bias, lng, lnb):
    T = z.shape[0]
    K, C = w.shape
    tm = _tile(T, CONV_TILE, ROW_CHUNK)
    rc = min(ROW_CHUNK, tm)
    srows = tm + CONV_HALO - SUBLANES

    def body(cv_ref, cg_ref, w_ref, b_ref, g_ref, bb_ref, u_ref, u1_ref, buf, sh):
        @pl.when(pl.program_id(0) == 0)
        def _():
            buf[pl.ds(0, CONV_HALO), :] = jnp.zeros((CONV_HALO, C), F32)

        buf[pl.ds(CONV_HALO, tm), :] = cv_ref[...] * jax.nn.sigmoid(cg_ref[...])
        _shift_copies(buf, sh, srows)
        for r0 in range(0, tm, rc):
            u1 = _tap_sum_tiles(buf, sh, w_ref, K, CONV_HALO - (K - 1), r0, rc, False) + b_ref[...]
            u1_ref[pl.ds(r0, rc), :] = u1
            xc = u1 - jnp.mean(u1, axis=-1, keepdims=True)
            xh = xc * lax.rsqrt(jnp.mean(xc * xc, axis=-1, keepdims=True) + LN_EPS)
            u2 = xh * g_ref[...] + bb_ref[...]
            u_ref[pl.ds(r0, rc), :] = (u2 * jax.nn.sigmoid(u2)).astype(BF16)
        buf[pl.ds(0, CONV_HALO), :] = buf[pl.ds(tm, CONV_HALO), :]

    vec = pl.BlockSpec((1, C), lambda i: (0, 0))
    return pl.pallas_call(
        body, grid=(T // tm,),
        in_specs=[pl.BlockSpec((tm, C), lambda i: (i, 0)), pl.BlockSpec((tm, C), lambda i: (i, 1)),
                  pl.BlockSpec((K, C), lambda i: (0, 0)), vec, vec, vec],
        out_specs=[pl.BlockSpec((tm, C), lambda i: (i, 0)), pl.BlockSpec((tm, C), lambda i: (i, 0))],
        out_shape=[jax.ShapeDtypeStruct((T, C), BF16), jax.ShapeDtypeStruct((T, C), F32)],
        scratch_shapes=[pltpu.VMEM((CONV_HALO + tm, C), F32), pltpu.VMEM((SUBLANES - 1, srows, C), F32)],
        compiler_params=_params("arbitrary"), name="conv_fwd")(z, z, w, bias, lng, lnb)


def _conv_bwd(dcat, u1, z, w, lng, lnb):
    T = z.shape[0]
    K, C = w.shape
    tm = _tile(T, CONV_TILE, ROW_CHUNK)
    rc = min(ROW_CHUNK, tm)
    nI = T // tm
    hb = tm // CONV_HALO
    srows = ((K + 4 + SUBLANES - 1) // SUBLANES) * SUBLANES
    shrows = tm + CONV_HALO - SUBLANES

    def body(du_ref, u1_ref, cv_ref, cg_ref, cvp_ref, cgp_ref, w_ref, g_ref, bb_ref,
             dz_ref, st_ref, u0buf, d1buf, ush, dsh):
        i = pl.program_id(0)
        ti = nI - 1 - i

        @pl.when(i == 0)
        def _():
            st_ref[...] = jnp.zeros_like(st_ref)
            d1buf[pl.ds(tm, CONV_HALO), :] = jnp.zeros((CONV_HALO, C), F32)

        prev = cvp_ref[...] * jax.nn.sigmoid(cgp_ref[...])
        u0buf[pl.ds(0, CONV_HALO), :] = jnp.where(ti == 0, 0.0, prev)
        u0buf[pl.ds(CONV_HALO, tm), :] = cv_ref[...] * jax.nn.sigmoid(cg_ref[...])

        gv = g_ref[...]
        dbias = jnp.zeros((1, C), F32)
        dgain = jnp.zeros((1, C), F32)
        dlnb = jnp.zeros((1, C), F32)
        for r0 in range(0, tm, rc):
            u1 = u1_ref[pl.ds(r0, rc), :]
            xc = u1 - jnp.mean(u1, axis=-1, keepdims=True)
            rstd = lax.rsqrt(jnp.mean(xc * xc, axis=-1, keepdims=True) + LN_EPS)
            xh = xc * rstd
            u2 = xh * gv + bb_ref[...]
            s = jax.nn.sigmoid(u2)
            du2 = du_ref[pl.ds(r0, rc), :] * (s * (1.0 + u2 * (1.0 - s)))
            dgain = dgain + _colsum(du2 * xh)
            dlnb = dlnb + _colsum(du2)
            dxh = du2 * gv
            du1 = rstd * (dxh - jnp.mean(dxh, axis=-1, keepdims=True)
                          - xh * jnp.mean(dxh * xh, axis=-1, keepdims=True))
            dbias = dbias + _colsum(du1)
            d1buf[pl.ds(r0, rc), :] = du1
        st_ref[pl.ds(K + 1, 1), :] += dbias
        st_ref[pl.ds(K + 2, 1), :] += dgain
        st_ref[pl.ds(K + 3, 1), :] += dlnb

        _shift_copies(u0buf, ush, shrows)
        _shift_copies(d1buf, dsh, shrows)
        for k in range(K):
            acc = jnp.zeros((SUBLANES, C), F32)
            for r0 in range(0, tm, rc):
                prod = d1buf[pl.ds(r0, rc), :] * _tap_rows(u0buf, ush, CONV_HALO - (K - 1) + k, r0, rc)
                acc = acc + jnp.sum(prod.reshape(rc // SUBLANES, SUBLANES, C), axis=0)
            st_ref[pl.ds(k, 1), :] += _colsum(acc)

        for r0 in range(0, tm, rc):
            du0 = _tap_sum_tiles(d1buf, dsh, w_ref, K, 0, r0, rc, True)
            cv = cv_ref[pl.ds(r0, rc), :]
            sg = jax.nn.sigmoid(cg_ref[pl.ds(r0, rc), :])
            dz_ref[pl.ds(r0, rc), pl.ds(0, C)] = (du0 * sg).astype(BF16)
            dz_ref[pl.ds(r0, rc), pl.ds(C, C)] = (du0 * cv * sg * (1.0 - sg)).astype(BF16)
        d1buf[pl.ds(tm, CONV_HALO), :] = d1buf[pl.ds(0, CONV_HALO), :]

    def rev(col):
        return lambda i: (nI - 1 - i, col)

    def rev_prev(col):
        return lambda i: (jnp.maximum((nI - 1 - i) * hb - 1, 0), col)

    vec = pl.BlockSpec((1, C), lambda i: (0, 0))
    return pl.pallas_call(
        body, grid=(nI,),
        in_specs=[pl.BlockSpec((tm, C), rev(0)), pl.BlockSpec((tm, C), rev(0)),
                  pl.BlockSpec((tm, C), rev(0)), pl.BlockSpec((tm, C), rev(1)),
                  pl.BlockSpec((CONV_HALO, C), rev_prev(0)), pl.BlockSpec((CONV_HALO, C), rev_prev(1)),
                  pl.BlockSpec((K, C), lambda i: (0, 0)), vec, vec],
        out_specs=[pl.BlockSpec((tm, 2 * C), rev(0)), pl.BlockSpec((srows, C), lambda i: (0, 0))],
        out_shape=[jax.ShapeDtypeStruct((T, 2 * C), BF16), jax.ShapeDtypeStruct((srows, C), F32)],
        scratch_shapes=[pltpu.VMEM((CONV_HALO + tm, C), F32), pltpu.VMEM((tm + CONV_HALO, C), F32),
                        pltpu.VMEM((SUBLANES - 1, shrows, C), F32), pltpu.VMEM((SUBLANES - 1, shrows, C), F32)],
        compiler_params=_params("arbitrary"), name="conv_bwd")(dcat, u1, z, z, z, z, w, lng, lnb)


def _softplus(v):
    return jnp.maximum(v, 0.0) + jnp.log(1.0 + jnp.exp(-jnp.abs(v)))


def _gelu(v):
    c = math.sqrt(2.0 / math.pi)
    t = jnp.tanh(c * (v + 0.044715 * v * v * v))
    gl = 0.5 * v * (1.0 + t)
    dgl = 0.5 * (1.0 + t) + 0.5 * v * (1.0 - t * t) * c * (1.0 + 3.0 * 0.044715 * v * v)
    return gl, dgl


def _lru_gates(xr, wa, ba, wx, bx, lam):
    xb = xr.astype(BF16)
    r = jax.nn.sigmoid(_dot(xb, wa) + ba)
    ig = jax.nn.sigmoid(_dot(xb, wx) + bx)
    sp = _softplus(-lam)
    log_a = -LRU_C * r * sp
    a = jnp.exp(log_a)
    y = 2.0 * log_a
    series = -(y * (1.0 + y * (0.5 + y * (1.0 / 6.0 + y * (1.0 / 24.0)))))
    mult = jnp.sqrt(jnp.where(y > -0.02, series, 1.0 - jnp.exp(y)))
    return a, mult, r, ig, sp


def _scan_tile(a_s, b_s, h_s, p_s, carry, seg, reverse):
    hl = [jnp.zeros((SUBLANES, LANES), F32)] * LRU_GROUPS
    pr = [jnp.ones((SUBLANES, LANES), F32)] * LRU_GROUPS
    for n in range(seg):
        for g in range(LRU_GROUPS):
            rows = pl.ds(g * SUBLANES * seg + ((seg - 1 - n) if reverse else n), SUBLANES, stride=seg)
            av = a_s[rows, :]
            hl[g] = av * hl[g] + b_s[rows, :]
            pr[g] = av * pr[g]
            h_s[rows, :] = hl[g]
            p_s[rows, :] = pr[g]
    nseg = SUBLANES * LRU_GROUPS
    cs = [None] * nseg
    c = carry
    for s in (range(nseg - 1, -1, -1) if reverse else range(nseg)):
        g, r = divmod(s, SUBLANES)
        cs[s] = c
        c = hl[g][r:r + 1, :] + pr[g][r:r + 1, :] * c
    return cs, c


def _lru_fwd(z, col0, w4, b4, wa, ba, wx, bx, lam):
    T = z.shape[0]
    K4, W = w4.shape
    nC = W // LANES
    tm = _tile(T, LRU_TILE, SUBLANES * SUBLANES * LRU_GROUPS)
    seg = tm // (SUBLANES * LRU_GROUPS)
    cx, cg = col0 // LANES, (col0 + W) // LANES

    def body(rx_ref, rg_ref, w4_ref, b4_ref, wa_ref, ba_ref, wx_ref, bx_ref, lam_ref,
             yr_ref, hs_ref, xbuf, a_s, b_s, h_s, p_s, hc):
        @pl.when(pl.program_id(1) == 0)
        def _():
            xbuf[pl.ds(0, LRU_HALO), :] = jnp.zeros((LRU_HALO, LANES), F32)
            hc[...] = jnp.zeros_like(hc)

        xbuf[pl.ds(LRU_HALO, tm), :] = rx_ref[...]
        xr = _tap_sum(xbuf, w4_ref, K4, LRU_HALO - (K4 - 1), 0, tm, False) + b4_ref[...]
        a, mult, _, ig, _ = _lru_gates(xr, wa_ref[...], ba_ref[...], wx_ref[...], bx_ref[...], lam_ref[...])
        a_s[...] = a
        b_s[...] = mult * ig * xr
        cs, cout = _scan_tile(a_s, b_s, h_s, p_s, hc[pl.ds(0, 1), :], seg, False)
        hc[pl.ds(0, 1), :] = cout
        for s in range(SUBLANES * LRU_GROUPS):
            rows = pl.ds(s * seg, seg)
            h = h_s[rows, :] + p_s[rows, :] * cs[s]
            hs_ref[rows, :] = h
            gl, _ = _gelu(rg_ref[rows, :])
            yr_ref[rows, :] = (h * gl).astype(BF16)
        xbuf[pl.ds(0, LRU_HALO), :] = xbuf[pl.ds(tm, LRU_HALO), :]

    vec = pl.BlockSpec((1, LANES), lambda c, i: (0, c))
    mat = pl.BlockSpec((None, LANES, LANES), lambda c, i: (c, 0, 0))
    return pl.pallas_call(
        body, grid=(nC, T // tm),
        in_specs=[pl.BlockSpec((tm, LANES), lambda c, i: (i, cx + c)),
                  pl.BlockSpec((tm, LANES), lambda c, i: (i, cg + c)),
                  pl.BlockSpec((K4, LANES), lambda c, i: (0, c)), vec, mat, vec, mat, vec, vec],
        out_specs=[pl.BlockSpec((tm, LANES), lambda c, i: (i, c)), pl.BlockSpec((tm, LANES), lambda c, i: (i, c))],
        out_shape=[jax.ShapeDtypeStruct((T, W), BF16), jax.ShapeDtypeStruct((T, W), F32)],
        scratch_shapes=[pltpu.VMEM((LRU_HALO + tm, LANES), F32)] + [pltpu.VMEM((tm, LANES), F32)] * 4
        + [pltpu.VMEM((SUBLANES, LANES), F32)],
        compiler_params=_params("parallel", "arbitrary"), name="lru_fwd")(z, z, w4, b4, wa, ba, wx, bx, lam)


def _lru_bwd(dcat, dcol0, hs, z, col0, w4, b4, wa, ba, wx, bx, lam):
    T = z.shape[0]
    K4, W = w4.shape
    assert K4 + 4 == SUBLANES
    nC = W // LANES
    tm = _tile(T, LRU_TILE, SUBLANES * SUBLANES * LRU_GROUPS)
    seg = tm // (SUBLANES * LRU_GROUPS)
    nI = T // tm
    hb = tm // LRU_HALO
    cx, cg, cd = col0 // LANES, (col0 + W) // LANES, dcol0 // LANES

    def body(dyr_ref, hs_ref, hsp_ref, rx_ref, rxp_ref, rg_ref, w4_ref, b4_ref, wa_ref, ba_ref, wx_ref, bx_ref,
             lam_ref, dzx_ref, dzg_ref, st_ref, dwa_ref, dwx_ref, xbuf, hbuf, abuf, a_s, b_s, h_s, p_s, dbuf, gc, anc):
        i = pl.program_id(1)
        ti = nI - 1 - i

        @pl.when(i == 0)
        def _():
            st_ref[...] = jnp.zeros_like(st_ref)
            dwa_ref[...] = jnp.zeros_like(dwa_ref)
            dwx_ref[...] = jnp.zeros_like(dwx_ref)
            gc[...] = jnp.zeros_like(gc)
            anc[...] = jnp.zeros_like(anc)
            dbuf[pl.ds(tm, LRU_HALO), :] = jnp.zeros((LRU_HALO, LANES), F32)

        xbuf[pl.ds(0, LRU_HALO), :] = jnp.where(ti == 0, 0.0, rxp_ref[...])
        xbuf[pl.ds(LRU_HALO, tm), :] = rx_ref[...]
        hbuf[pl.ds(0, LRU_HALO), :] = jnp.where(ti == 0, 0.0, hsp_ref[...])
        hbuf[pl.ds(LRU_HALO, tm), :] = hs_ref[...]

        wa, wx = wa_ref[...], wx_ref[...]
        lam_v = lam_ref[...]
        xr = _tap_sum(xbuf, w4_ref, K4, LRU_HALO - (K4 - 1), 0, tm, False) + b4_ref[...]
        a, mult, r, ig, sp = _lru_gates(xr, wa, ba_ref[...], wx, bx_ref[...], lam_v)

        dyr = dyr_ref[...]
        gl, dgl = _gelu(rg_ref[...])
        dzg_ref[...] = (dyr * hs_ref[...] * dgl).astype(BF16)

        abuf[pl.ds(0, tm), :] = a
        abuf[pl.ds(tm, LRU_HALO), :] = anc[...]
        a_s[...] = abuf[pl.ds(1, tm), :]
        b_s[...] = dyr * gl
        cs, cout = _scan_tile(a_s, b_s, h_s, p_s, gc[pl.ds(0, 1), :], seg, True)
        gc[pl.ds(0, 1), :] = cout
        anc[pl.ds(0, 1), :] = a[0:1, :]
        for s in range(SUBLANES * LRU_GROUPS):
            rows = pl.ds(s * seg, seg)
            b_s[rows, :] = h_s[rows, :] + p_s[rows, :] * cs[s]
        g = b_s[...]

        d_a = g * hbuf[pl.ds(LRU_HALO - 1, tm), :]
        gx_ = g * xr
        d_log_a = d_a * a - (gx_ * ig) * (a * a / mult)
        dga = (d_log_a * (-LRU_C * sp)) * r * (1.0 - r)
        dgx = (gx_ * mult) * ig * (1.0 - ig)
        dga_b, dgx_b = dga.astype(BF16), dgx.astype(BF16)
        dxr = g * mult * ig + _dot_nt(dga_b, wa) + _dot_nt(dgx_b, wx)
        xb = xr.astype(BF16)
        dwa_ref[...] += _dot_tn(xb, dga_b)
        dwx_ref[...] += _dot_tn(xb, dgx_b)
        st_ref[pl.ds(K4, 1), :] += _colsum(dxr)
        st_ref[pl.ds(K4 + 1, 1), :] += _colsum(dga)
        st_ref[pl.ds(K4 + 2, 1), :] += _colsum(dgx)
        st_ref[pl.ds(K4 + 3, 1), :] += _colsum(d_log_a * (-LRU_C * r)) * (-jax.nn.sigmoid(-lam_v))

        dbuf[pl.ds(0, tm), :] = dxr
        for k in range(K4):
            st_ref[pl.ds(k, 1), :] += _colsum(dxr * xbuf[pl.ds(LRU_HALO - (K4 - 1) + k, tm), :])
        dzx_ref[...] = _tap_sum(dbuf, w4_ref, K4, 0, 0, tm, True).astype(BF16)
        dbuf[pl.ds(tm, LRU_HALO), :] = dbuf[pl.ds(0, LRU_HALO), :]

    def rev(col):
        return lambda c, i: (nI - 1 - i, col + c)

    def rev_prev(col):
        return lambda c, i: (jnp.maximum((nI - 1 - i) * hb - 1, 0), col + c)

    vec = pl.BlockSpec((1, LANES), lambda c, i: (0, c))
    mat = pl.BlockSpec((None, LANES, LANES), lambda c, i: (c, 0, 0))
    big = pltpu.VMEM((tm, LANES), F32)
    halo = pltpu.VMEM((tm + LRU_HALO, LANES), F32)
    return pl.pallas_call(
        body, grid=(nC, nI),
        in_specs=[pl.BlockSpec((tm, LANES), rev(cd)),
                  pl.BlockSpec((tm, LANES), rev(0)), pl.BlockSpec((LRU_HALO, LANES), rev_prev(0)),
                  pl.BlockSpec((tm, LANES), rev(cx)), pl.BlockSpec((LRU_HALO, LANES), rev_prev(cx)),
                  pl.BlockSpec((tm, LANES), rev(cg)),
                  pl.BlockSpec((K4, LANES), lambda c, i: (0, c)), vec, mat, vec, mat, vec, vec],
        out_specs=[pl.BlockSpec((tm, LANES), rev(0)), pl.BlockSpec((tm, LANES), rev(0)),
                   pl.BlockSpec((SUBLANES, LANES), lambda c, i: (0, c)), mat, mat],
        out_shape=[jax.ShapeDtypeStruct((T, W), BF16), jax.ShapeDtypeStruct((T, W), BF16),
                   jax.ShapeDtypeStruct((SUBLANES, W), F32),
                   jax.ShapeDtypeStruct((nC, LANES, LANES), F32), jax.ShapeDtypeStruct((nC, LANES, LANES), F32)],
        scratch_shapes=[halo, halo, halo, big, big, big, big, halo,
                        pltpu.VMEM((SUBLANES, LANES), F32), pltpu.VMEM((SUBLANES, LANES), F32)],
        compiler_params=_params("parallel", "arbitrary"), name="lru_bwd")(
            dcat, hs, hs, z, z, z, w4, b4, wa, ba, wx, bx, lam)


def _mix_out_fwd(x, u, yr, wout):
    T, D = x.shape
    C, W = u.shape[1], yr.shape[1]
    tm = _tile(T, TOK_TILE)

    def body(x_ref, u_ref, yr_ref, w_ref, y_ref):
        y_ref[...] = (x_ref[...] + _dot(u_ref[...], w_ref[pl.ds(0, C), :])
                      + _dot(yr_ref[...], w_ref[pl.ds(C, W), :]))

    return pl.pallas_call(
        body, grid=(T // tm,),
        in_specs=[pl.BlockSpec((tm, D), lambda i: (i, 0)), pl.BlockSpec((tm, C), lambda i: (i, 0)),
                  pl.BlockSpec((tm, W), lambda i: (i, 0)),
                  pl.BlockSpec((C + W, D), lambda i: (0, 0), pipeline_mode=pl.Buffered(1))],
        out_specs=pl.BlockSpec((tm, D), lambda i: (i, 0)),
        out_shape=jax.ShapeDtypeStruct((T, D), F32),
        compiler_params=_params("parallel"), name="mix_out_fwd")(x, u, yr, wout)


def _mix_out_bwd(dy, u, yr, wout):
    T, D = dy.shape
    C, W = u.shape[1], yr.shape[1]
    tm = _tile(T, BWD_TILE)

    def body(dy_ref, u_ref, yr_ref, w_ref, dcat_ref, dw_ref):
        @pl.when(pl.program_id(0) == 0)
        def _():
            dw_ref[...] = jnp.zeros_like(dw_ref)

        dyb = dy_ref[...].astype(BF16)
        dcat_ref[...] = _dot_nt(dyb, w_ref[...])
        dw_ref[pl.ds(0, C), :] += _dot_tn(u_ref[...], dyb)
        dw_ref[pl.ds(C, W), :] += _dot_tn(yr_ref[...], dyb)

    return pl.pallas_call(
        body, grid=(T // tm,),
        in_specs=[pl.BlockSpec((tm, D), lambda i: (i, 0)), pl.BlockSpec((tm, C), lambda i: (i, 0)),
                  pl.BlockSpec((tm, W), lambda i: (i, 0)),
                  pl.BlockSpec((C + W, D), lambda i: (0, 0), pipeline_mode=pl.Buffered(1))],
        out_specs=[pl.BlockSpec((tm, C + W), lambda i: (i, 0)), pl.BlockSpec((C + W, D), lambda i: (0, 0))],
        out_shape=[jax.ShapeDtypeStruct((T, C + W), F32), jax.ShapeDtypeStruct((C + W, D), F32)],
        compiler_params=_params("arbitrary"), name="mix_out_bwd")(dy, u, yr, wout)


def _mix_in_bwd(dzc, dzx, dzg, x, dy, g, win):
    T, D = x.shape
    ns, ws = win.shape[0], win.shape[2]
    tm = _tile(T, BWD_TILE)
    parts = []
    for j in range(ns):
        lo = j * ws
        if lo < dzc.shape[1]:
            parts.append((0, lo))
        elif lo < dzc.shape[1] + dzx.shape[1]:
            parts.append((1, lo - dzc.shape[1]))
        else:
            parts.append((2, lo - dzc.shape[1] - dzx.shape[1]))

    def body(dzc_ref, dzx_ref, dzg_ref, x_ref, dy_ref, g_ref, w_ref, dx_ref, dw_ref, dg_ref):
        @pl.when(pl.program_id(0) == 0)
        def _():
            dw_ref[...] = jnp.zeros_like(dw_ref)
            dg_ref[...] = jnp.zeros_like(dg_ref)

        xh, r = _rms_stats(x_ref[...])
        gv = g_ref[...]
        hb = (xh * gv).astype(BF16)
        srcs = (dzc_ref, dzx_ref, dzg_ref)
        dh = jnp.zeros((tm, D), F32)
        for j, (si, off) in enumerate(parts):
            dzj = srcs[si][:, pl.ds(off, ws)]
            dh = dh + _dot_nt(dzj, w_ref[j])
            dw_ref[j] += _dot_tn(hb, dzj)
        dx_ref[...] = dy_ref[...] + _rms_bwd(dh, xh, r, gv)
        dg_ref[...] += _colsum(dh * xh)

    def tok(n):
        return pl.BlockSpec((tm, n), lambda i: (i, 0))

    vec = pl.BlockSpec((1, D), lambda i: (0, 0))
    return pl.pallas_call(
        body, grid=(T // tm,),
        in_specs=[tok(dzc.shape[1]), tok(dzx.shape[1]), tok(dzg.shape[1]), tok(D), tok(D), vec,
                  pl.BlockSpec((ns, D, ws), lambda i: (0, 0, 0), pipeline_mode=pl.Buffered(1))],
        out_specs=[tok(D), pl.BlockSpec((ns, D, ws), lambda i: (0, 0, 0)), vec],
        out_shape=[jax.ShapeDtypeStruct((T, D), F32), jax.ShapeDtypeStruct((ns, D, ws), F32),
                   jax.ShapeDtypeStruct((1, D), F32)],
        compiler_params=_params("arbitrary"), name="mix_in_bwd")(dzc, dzx, dzg, x, dy, g, win)


def _adamw(w, g, m, v, name):
    R, Cc = w.shape
    tr = _tile(R, max(SUBLANES, (1 << 19) // Cc))
    c1 = 1.0 - ADAM_B1 ** ADAM_STEP
    c2 = 1.0 - ADAM_B2 ** ADAM_STEP

    def body(w_ref, g_ref, m_ref, v_ref, d_ref, nm_ref, nv_ref):
        gv = g_ref[...]
        nm = ADAM_B1 * m_ref[...] + (1.0 - ADAM_B1) * gv
        nv = ADAM_B2 * v_ref[...] + (1.0 - ADAM_B2) * (gv * gv)
        nm_ref[...] = nm
        nv_ref[...] = nv
        d_ref[...] = -ADAM_LR * ((nm / c1) / (jnp.sqrt(nv / c2) + ADAM_EPS) + ADAM_WD * w_ref[...])

    blk = pl.BlockSpec((tr, Cc), lambda i: (i, 0))
    sds = jax.ShapeDtypeStruct((R, Cc), F32)
    return pl.pallas_call(
        body, grid=(R // tr,), in_specs=[blk] * 4, out_specs=[blk] * 3, out_shape=[sds] * 3,
        compiler_params=_params("parallel"), name=name)(w, g, m, v)


def _here():
    return lax.axis_index("x"), lax.axis_index("y"), lax.axis_index("c")


def _chip_at(x, y, m):
    return x ^ (m >> 1), y ^ (m & 1)


ANY = pl.BlockSpec(memory_space=pl.ANY)


def _place_cast(srcs, idx, dtype, name):
    n = len(srcs)
    R, Cc = srcs[0].shape
    tr = _tile(R, max(16, (1 << 18) // Cc), 16)

    def body(i_ref, *refs):
        o_ref = refs[n]
        for k in range(n):
            o_ref[k] = refs[k][...].astype(dtype)

    blk = pl.BlockSpec((tr, Cc), lambda i, s: (i, 0))
    return pl.pallas_call(
        body,
        grid_spec=pltpu.PrefetchScalarGridSpec(
            num_scalar_prefetch=1, grid=(R // tr,), in_specs=[blk] * n,
            out_specs=pl.BlockSpec((n, None, tr, Cc), lambda i, s: (0, s[1], i, 0))),
        out_shape=jax.ShapeDtypeStruct((n, N_CHIPS, R, Cc), dtype),
        compiler_params=_params("parallel"), name=name)(idx, *srcs)


def _gather_weights(lands):
    n = len(lands)

    def body(*refs):
        outs = refs[n:2 * n]
        send1, recv1, send2, recv2 = refs[2 * n:]
        x, y, c = _here()
        own = 2 * x + y

        def half(ref, chip, cc):
            rh = ref.shape[-2] // 2
            lead = (slice(None),) * (len(ref.shape) - 3)
            return ref.at[lead + (chip, pl.ds(cc * rh, rh), slice(None))]

        first = []
        for k in range(n):
            for m in (1, 2, 3):
                px, py = _chip_at(x, y, m)
                cp = pltpu.make_async_remote_copy(
                    src_ref=half(outs[k], own, c), dst_ref=half(outs[k], own, c),
                    send_sem=send1.at[k, m - 1], recv_sem=recv1.at[k, m - 1],
                    device_id=(px, py, c), device_id_type=MESH)
                cp.start()
                first.append(cp)

        passed = []
        for k in range(n):
            for m in (1, 2, 3):
                px, py = _chip_at(x, y, m)
                peer = 2 * px + py
                got = half(outs[k], peer, c)
                pltpu.make_async_remote_copy(
                    src_ref=got, dst_ref=got, send_sem=send1.at[k, m - 1], recv_sem=recv1.at[k, m - 1],
                    device_id=(px, py, c), device_id_type=MESH).wait_recv()
                cp = pltpu.make_async_remote_copy(
                    src_ref=got, dst_ref=got, send_sem=send2.at[k, m - 1], recv_sem=recv2.at[k, m - 1],
                    device_id=(x, y, 1 - c), device_id_type=MESH)
                cp.start()
                passed.append(cp)

        for k in range(n):
            for m in (1, 2, 3):
                px, py = _chip_at(x, y, m)
                other = half(outs[k], 2 * px + py, 1 - c)
                pltpu.make_async_remote_copy(
                    src_ref=other, dst_ref=other, send_sem=send2.at[k, m - 1], recv_sem=recv2.at[k, m - 1],
                    device_id=(x, y, 1 - c), device_id_type=MESH).wait_recv()
        for cp in first + passed:
            cp.wait_send()

    return pl.pallas_call(
        body, in_specs=[ANY] * n, out_specs=[ANY] * n,
        out_shape=[jax.ShapeDtypeStruct(a.shape, a.dtype) for a in lands],
        input_output_aliases={k: k for k in range(n)},
        scratch_shapes=[pltpu.SemaphoreType.DMA((n, 3)), pltpu.SemaphoreType.DMA((n, 3)),
                        pltpu.SemaphoreType.DMA((n, 3)), pltpu.SemaphoreType.DMA((n, 3))],
        name="gather_weights")(*lands)


HBM = pl.BlockSpec(memory_space=pltpu.HBM)
SEM = pl.BlockSpec(memory_space=pltpu.SEMAPHORE)
EFFECT = pltpu.SideEffectType.DATAFLOW_SIDE_EFFECTING


def _in_hbm(a):
    return pltpu.with_memory_space_constraint(a, pltpu.HBM)


def _gather_copies(land_refs, send, recv):
    x, y, c = _here()
    own = 2 * x + y
    cps = []
    for k in range(len(land_refs)):
        lead = (slice(None),) * (len(land_refs[k].shape) - 3)
        mine = land_refs[k].at[lead + (own,)]
        for m in (1, 2, 3):
            px, py = _chip_at(x, y, m)
            cps.append(pltpu.make_async_remote_copy(
                src_ref=mine, dst_ref=mine, send_sem=send.at[3 * k + m - 1], recv_sem=recv.at[3 * k + m - 1],
                device_id=(px, py, c), device_id_type=MESH))
    return cps


def _gather_start(lands, after, name):
    n = len(lands)

    def body(*refs):
        lz = refs[:n]
        send, recv = refs[n + 1], refs[n + 2]
        token = refs[-1]
        for cp in _gather_copies(lz, send, recv):
            cp.start()
        token[...] = jnp.zeros_like(token)

    hbm = [pltpu.HBM(a.shape, a.dtype) for a in lands]
    outs = pl.pallas_call(
        body, name=name,
        in_specs=[HBM] * n + [ANY],
        out_specs=[SEM, SEM] + [HBM] * n + [pl.BlockSpec(memory_space=pltpu.VMEM)],
        out_shape=[pltpu.SemaphoreType.DMA((3 * n,)), pltpu.SemaphoreType.DMA((3 * n,))] + hbm
        + [jax.ShapeDtypeStruct((SUBLANES, LANES), F32)],
        input_output_aliases={k: 2 + k for k in range(n)},
        compiler_params=pltpu.CompilerParams(has_side_effects=EFFECT),
    )(*[_in_hbm(a) for a in lands], after)
    return outs[0], outs[1], outs[2:2 + n], outs[-1]


def _gather_wait(send, recv, lands, after, name):
    n = len(lands)

    def body(*refs):
        lz = refs[:n]
        send_r, recv_r = refs[n], refs[n + 1]
        for cp in _gather_copies(lz, send_r, recv_r):
            cp.wait_send()
            cp.wait_recv()

    hbm = [pltpu.HBM(a.shape, a.dtype) for a in lands]
    return pl.pallas_call(
        body, name=name,
        in_specs=[HBM] * n + [SEM, SEM, ANY],
        out_specs=[HBM] * n, out_shape=hbm,
        input_output_aliases={k: k for k in range(n)},
        compiler_params=pltpu.CompilerParams(has_side_effects=EFFECT),
    )(*lands, send, recv, after)


def _exchange_copies(part_refs, slot_refs, send, recv):
    x, y, c = _here()
    cps = []
    for k in range(len(part_refs)):
        for m in (1, 2, 3):
            px, py = _chip_at(x, y, m)
            cps.append(pltpu.make_async_remote_copy(
                src_ref=part_refs[k].at[2 * px + py], dst_ref=slot_refs[k].at[m - 1],
                send_sem=send.at[3 * k + m - 1], recv_sem=recv.at[3 * k + m - 1],
                device_id=(px, py, c), device_id_type=MESH))
    return cps


def _exchange_start(parts, name):
    n = len(parts)
    lands = [lax.empty((N_CHIPS - 1,) + p.shape[1:], p.dtype) for p in parts]

    def body(*refs):
        ins, lz = refs[:n], refs[n:2 * n]
        send, recv = refs[2 * n], refs[2 * n + 1]
        token = refs[-1]
        for cp in _exchange_copies(ins, lz, send, recv):
            cp.start()
        token[...] = jnp.zeros_like(token)

    hbm = [pltpu.HBM(a.shape, a.dtype) for a in list(parts) + lands]
    outs = pl.pallas_call(
        body, name=name,
        in_specs=[HBM] * (2 * n),
        out_specs=[SEM, SEM] + [HBM] * (2 * n) + [pl.BlockSpec(memory_space=pltpu.VMEM)],
        out_shape=[pltpu.SemaphoreType.DMA((3 * n,)), pltpu.SemaphoreType.DMA((3 * n,))] + hbm
        + [jax.ShapeDtypeStruct((SUBLANES, LANES), F32)],
        input_output_aliases={k: 2 + k for k in range(2 * n)},
        compiler_params=pltpu.CompilerParams(has_side_effects=EFFECT),
    )(*[_in_hbm(a) for a in parts], *[_in_hbm(a) for a in lands])
    return outs[0], outs[1], outs[2:2 + n], outs[2 + n:2 + 2 * n], outs[-1]


def _exchange_wait(send, recv, parts, lands, after, name):
    n = len(parts)

    def body(*refs):
        ins, lz = refs[:n], refs[n:2 * n]
        send_r, recv_r = refs[2 * n], refs[2 * n + 1]
        for cp in _exchange_copies(ins, lz, send_r, recv_r):
            cp.wait_send()
            cp.wait_recv()

    hbm = [pltpu.HBM(a.shape, a.dtype) for a in list(parts) + list(lands)]
    outs = pl.pallas_call(
        body, name=name,
        in_specs=[HBM] * (2 * n) + [SEM, SEM, ANY],
        out_specs=[HBM] * (2 * n), out_shape=hbm,
        input_output_aliases={k: k for k in range(2 * n)},
        compiler_params=pltpu.CompilerParams(has_side_effects=EFFECT),
    )(*parts, *lands, send, recv, after)
    return outs[:n], outs[n:]


def _swap_halves_out(grads, name):
    n = len(grads)
    out_shapes = [jax.ShapeDtypeStruct((g.shape[0], g.shape[1] // 2, g.shape[2]), g.dtype) for g in grads]

    def body(*refs):
        ins, outs = refs[:n], refs[n:2 * n]
        send, recv = refs[2 * n:]
        x, y, c = _here()
        cps = []
        for k in range(n):
            rh = ins[k].shape[1] // 2
            cp = pltpu.make_async_remote_copy(
                src_ref=ins[k].at[:, pl.ds((1 - c) * rh, rh), :], dst_ref=outs[k],
                send_sem=send.at[k], recv_sem=recv.at[k], device_id=(x, y, 1 - c), device_id_type=MESH)
            cp.start()
            cps.append(cp)
        for cp in cps:
            cp.wait()

    return pl.pallas_call(
        body, in_specs=[ANY] * n, out_specs=[ANY] * n, out_shape=out_shapes,
        scratch_shapes=[pltpu.SemaphoreType.DMA((n,)), pltpu.SemaphoreType.DMA((n,))],
        name=name)(*grads)


def _swap_copies(grad_refs, land_refs, send, recv):
    x, y, c = _here()
    cps = []
    for k in range(len(grad_refs)):
        rh = grad_refs[k].shape[1] // 2
        cps.append(pltpu.make_async_remote_copy(
            src_ref=grad_refs[k].at[:, pl.ds((1 - c) * rh, rh), :], dst_ref=land_refs[k],
            send_sem=send.at[k], recv_sem=recv.at[k], device_id=(x, y, 1 - c), device_id_type=MESH))
    return cps


def _swap_start(grads, name):
    n = len(grads)
    lands = [lax.empty((g.shape[0], g.shape[1] // 2, g.shape[2]), g.dtype) for g in grads]

    def body(*refs):
        ins, lz = refs[:n], refs[n:2 * n]
        send, recv = refs[2 * n], refs[2 * n + 1]
        token = refs[-1]
        for cp in _swap_copies(ins, lz, send, recv):
            cp.start()
        token[...] = jnp.zeros_like(token)

    hbm = [pltpu.HBM(a.shape, a.dtype) for a in list(grads) + lands]
    outs = pl.pallas_call(
        body, name=name,
        in_specs=[HBM] * (2 * n),
        out_specs=[SEM, SEM] + [HBM] * (2 * n) + [pl.BlockSpec(memory_space=pltpu.VMEM)],
        out_shape=[pltpu.SemaphoreType.DMA((n,)), pltpu.SemaphoreType.DMA((n,))] + hbm
        + [jax.ShapeDtypeStruct((SUBLANES, LANES), F32)],
        input_output_aliases={k: 2 + k for k in range(2 * n)},
        compiler_params=pltpu.CompilerParams(has_side_effects=EFFECT),
    )(*[_in_hbm(a) for a in grads], *[_in_hbm(a) for a in lands])
    return outs[0], outs[1], outs[2:2 + n], outs[2 + n:2 + 2 * n], outs[-1]


def _swap_wait(send, recv, grads, lands, after, name):
    n = len(grads)

    def body(*refs):
        ins, lz = refs[:n], refs[n:2 * n]
        send_r, recv_r = refs[2 * n], refs[2 * n + 1]
        for cp in _swap_copies(ins, lz, send_r, recv_r):
            cp.wait_send()
            cp.wait_recv()

    hbm = [pltpu.HBM(a.shape, a.dtype) for a in list(grads) + list(lands)]
    outs = pl.pallas_call(
        body, name=name,
        in_specs=[HBM] * (2 * n) + [SEM, SEM, ANY],
        out_specs=[HBM] * (2 * n), out_shape=hbm,
        input_output_aliases={k: k for k in range(2 * n)},
        compiler_params=pltpu.CompilerParams(has_side_effects=EFFECT),
    )(*grads, *lands, send, recv, after)
    return outs[:n], outs[n:]


def _add_cast(g, other, cidx, name):
    ns, R, Cc = g.shape
    rh = R // 2
    tr = _tile(rh, max(16, (1 << 19) // Cc), 16)
    nb = rh // tr

    def body(c_ref, g_ref, o_ref, s_ref):
        s_ref[...] = (g_ref[...] + o_ref[...]).astype(BF16)

    return pl.pallas_call(
        body,
        grid_spec=pltpu.PrefetchScalarGridSpec(
            num_scalar_prefetch=1, grid=(ns, nb),
            in_specs=[pl.BlockSpec((None, tr, Cc), lambda k, i, c: (k, c[0] * nb + i, 0)),
                      pl.BlockSpec((None, tr, Cc), lambda k, i, c: (k, i, 0))],
            out_specs=pl.BlockSpec((None, tr, Cc), lambda k, i, c: (k, i, 0))),
        out_shape=jax.ShapeDtypeStruct((ns, rh, Cc), BF16),
        compiler_params=_params("parallel", "parallel"), name=name)(cidx, g, other)


def _sum_slots(part, got, idx, name):
    ns, rh, Cc = got.shape
    tr = _tile(rh, max(16, (1 << 18) // Cc), 16)
    nb = rh // tr

    def body(i_ref, p_ref, b_ref, o_ref):
        acc = p_ref[...].astype(F32)
        for m in range(ns):
            acc = acc + b_ref[m].astype(F32)
        o_ref[...] = acc

    return pl.pallas_call(
        body,
        grid_spec=pltpu.PrefetchScalarGridSpec(
            num_scalar_prefetch=1, grid=(nb,),
            in_specs=[pl.BlockSpec((None, tr, Cc), lambda i, s: (s[1], i, 0)),
                      pl.BlockSpec((ns, tr, Cc), lambda i, s: (0, i, 0))],
            out_specs=pl.BlockSpec((tr, Cc), lambda i, s: (s[0] * nb + i, 0))),
        out_shape=jax.ShapeDtypeStruct((2 * rh, Cc), F32),
        compiler_params=_params("parallel"), name=name)(idx, part, got)


def _share_halves(blocks, name):
    n = len(blocks)

    def body(*refs):
        ins, outs = refs[:n], refs[n:2 * n]
        send, recv = refs[2 * n:]
        x, y, c = _here()
        cps = []
        for k in range(n):
            rh = outs[k].shape[0] // 2
            mine = outs[k].at[pl.ds(c * rh, rh), :]
            cp = pltpu.make_async_remote_copy(
                src_ref=mine, dst_ref=mine, send_sem=send.at[k], recv_sem=recv.at[k],
                device_id=(x, y, 1 - c), device_id_type=MESH)
            cp.start()
            cps.append(cp)
        for cp in cps:
            cp.wait()

    return pl.pallas_call(
        body, in_specs=[ANY] * n, out_specs=[ANY] * n,
        out_shape=[jax.ShapeDtypeStruct(b.shape, b.dtype) for b in blocks],
        input_output_aliases={k: k for k in range(n)},
        scratch_shapes=[pltpu.SemaphoreType.DMA((n,)), pltpu.SemaphoreType.DMA((n,))],
        name=name)(*blocks)


def _small_copies(p_ref, slot_ref, send, recv):
    x, y, c = _here()
    mine = slot_ref.at[4 * x + 2 * y + c]
    cps = []
    for m in range(1, N_DEV):
        peer = (x ^ (m >> 2), y ^ ((m >> 1) & 1), c ^ (m & 1))
        cps.append(pltpu.make_async_remote_copy(
            src_ref=p_ref, dst_ref=mine, send_sem=send.at[m - 1], recv_sem=recv.at[m - 1],
            device_id=peer, device_id_type=MESH))
    return cps


def _small_start(packed):
    slots = lax.empty((N_DEV,) + packed.shape, packed.dtype)

    def body(p_ref, s_ref, send, recv, p_thru, s_thru, token):
        for cp in _small_copies(p_ref, s_ref, send, recv):
            cp.start()
        token[...] = jnp.zeros_like(token)

    return pl.pallas_call(
        body, name="small_start",
        in_specs=[HBM, HBM],
        out_specs=[SEM, SEM, HBM, HBM, pl.BlockSpec(memory_space=pltpu.VMEM)],
        out_shape=[pltpu.SemaphoreType.DMA((N_DEV - 1,)), pltpu.SemaphoreType.DMA((N_DEV - 1,)),
                   pltpu.HBM(packed.shape, packed.dtype), pltpu.HBM(slots.shape, slots.dtype),
                   jax.ShapeDtypeStruct((SUBLANES, LANES), F32)],
        input_output_aliases={0: 2, 1: 3},
        compiler_params=pltpu.CompilerParams(has_side_effects=EFFECT),
    )(_in_hbm(packed), _in_hbm(slots))


def _small_wait(send, recv, packed, slots, after):
    def body(p_ref, s_ref, send_r, recv_r, after_ref, p_out, s_out):
        for cp in _small_copies(p_ref, s_ref, send_r, recv_r):
            cp.wait_send()
            cp.wait_recv()

    return pl.pallas_call(
        body, name="small_wait",
        in_specs=[HBM, HBM, SEM, SEM, ANY], out_specs=[HBM, HBM],
        out_shape=[pltpu.HBM(packed.shape, packed.dtype), pltpu.HBM(slots.shape, slots.dtype)],
        input_output_aliases={0: 0, 1: 1},
        compiler_params=pltpu.CompilerParams(has_side_effects=EFFECT),
    )(packed, slots, send, recv, after)


def _sum_devices(packed, slots, me):
    n, R, _ = slots.shape
    tr = _tile(R, 1024)

    def body(m_ref, p_ref, s_ref, o_ref):
        own = p_ref[...]
        acc = None
        for d in range(n):
            term = jnp.where(m_ref[0] == d, own, s_ref[d])
            acc = term if acc is None else acc + term
        o_ref[...] = acc

    return pl.pallas_call(
        body,
        grid_spec=pltpu.PrefetchScalarGridSpec(
            num_scalar_prefetch=1, grid=(R // tr,),
            in_specs=[pl.BlockSpec((tr, LANES), lambda i, m: (i, 0)),
                      pl.BlockSpec((n, tr, LANES), lambda i, m: (0, i, 0))],
            out_specs=pl.BlockSpec((tr, LANES), lambda i, m: (i, 0))),
        out_shape=jax.ShapeDtypeStruct((R, LANES), F32),
        compiler_params=_params("parallel"), name="sum_devices")(me, packed, slots)


def _pack(arrs):
    rows, parts = [], []
    for a in arrs:
        flat = a.reshape(-1)
        r = -(-flat.shape[0] // (SUBLANES * LANES)) * SUBLANES
        parts.append(jnp.pad(flat, (0, r * LANES - flat.shape[0])).reshape(r, LANES))
        rows.append(r)
    return jnp.concatenate(parts, axis=0), rows


def _unpack(packed, rows, shapes):
    out, r0 = [], 0
    for r, shp in zip(rows, shapes):
        size = math.prod(shp)
        out.append(packed[r0:r0 + r].reshape(-1)[:size].reshape(shp))
        r0 += r
    return out


def _block_diag(w, per):
    H, dh, _ = w.shape
    w4 = w.reshape(H // per, per, dh, dh)
    eye = jnp.eye(per, dtype=w.dtype)
    return (w4[:, :, :, None, :] * eye[None, :, None, :, None]).reshape(H // per, per * dh, per * dh)


def _block_diag_take(d, per):
    n, s, _ = d.shape
    dh = s // per
    d5 = d.reshape(n, per, dh, per, dh)
    return jnp.stack([d5[:, h, :, h, :] for h in range(per)], axis=1).reshape(n * per, dh, dh)


def kernel(x, ffn1_norm, ffn1_w_gate, ffn1_w_up, ffn1_w_down, mix_norm, w_in, conv_dw, conv_dw_bias, conv_ln_g, conv_ln_b, lru_conv_w, lru_conv_b, lru_w_a, lru_b_a, lru_w_x, lru_b_x, lru_lambda, w_out, ffn2_norm, ffn2_w_gate, ffn2_w_up, ffn2_w_down, final_norm, loss_target, m_ffn1_norm, m_ffn1_w_gate, m_ffn1_w_up, m_ffn1_w_down, m_mix_norm, m_w_in, m_conv_dw, m_conv_dw_bias, m_conv_ln_g, m_conv_ln_b, m_lru_conv_w, m_lru_conv_b, m_lru_w_a, m_lru_b_a, m_lru_w_x, m_lru_b_x, m_lru_lambda, m_w_out, m_ffn2_norm, m_ffn2_w_gate, m_ffn2_w_up, m_ffn2_w_down, m_final_norm, v_ffn1_norm, v_ffn1_w_gate, v_ffn1_w_up, v_ffn1_w_down, v_mix_norm, v_w_in, v_conv_dw, v_conv_dw_bias, v_conv_ln_g, v_conv_ln_b, v_lru_conv_w, v_lru_conv_b, v_lru_w_a, v_lru_b_a, v_lru_w_x, v_lru_b_x, v_lru_lambda, v_w_out, v_ffn2_norm, v_ffn2_w_gate, v_ffn2_w_up, v_ffn2_w_down, v_final_norm):
    names = ['ffn1_norm', 'ffn1_w_gate', 'ffn1_w_up', 'ffn1_w_down', 'mix_norm', 'w_in', 'conv_dw', 'conv_dw_bias',
             'conv_ln_g', 'conv_ln_b', 'lru_conv_w', 'lru_conv_b', 'lru_w_a', 'lru_b_a', 'lru_w_x', 'lru_b_x',
             'lru_lambda', 'w_out', 'ffn2_norm', 'ffn2_w_gate', 'ffn2_w_up', 'ffn2_w_down', 'final_norm']
    env = dict(locals())
    W = {n: env[n] for n in names}
    M = {n: env['m_' + n] for n in names}
    V = {n: env['v_' + n] for n in names}

    xi, yi, ci = _here()
    chip = 2 * xi + yi
    cidx = ci.astype(jnp.int32).reshape(1)
    T, D = x.shape[-2], x.shape[-1]
    xs = x.reshape(T, D)
    tgt = loss_target.reshape(T, D)
    K, Cs = conv_dw.shape
    C = conv_dw_bias.shape[0]
    Wl = lru_conv_b.shape[0]
    K4 = lru_conv_w.shape[0]
    heads, dh, _ = lru_w_a.shape
    per = LANES // dh

    def row(v):
        return v.reshape(1, -1)

    tform = ('ffn1_w_gate', 'ffn1_w_up', 'ffn2_w_gate', 'ffn2_w_up')
    for n in tform:
        W[n], M[n], V[n] = W[n].T, M[n].T, V[n].T
    kp = -(-K // SUBLANES) * SUBLANES
    taps = jnp.concatenate([conv_dw, jnp.zeros((kp - K, Cs), F32), lru_conv_w,
                            jnp.zeros((2 * SUBLANES - K4, Cs), F32)], axis=0)
    idx = jnp.stack([ci, chip]).astype(jnp.int32)
    (wff1,) = _gather_weights([_place_cast([W['ffn1_w_gate'], W['ffn1_w_up'], ffn1_w_down], idx, BF16, "place_ffn1")])
    mixl = [_place_cast([w_in], idx, BF16, "place_w_in"), _place_cast([w_out], idx, BF16, "place_w_out"),
            _place_cast([taps], idx, F32, "place_taps")]
    msend, mrecv, mixl, mtok = _gather_start(mixl, wff1, "gather_mix_start")
    ff2l = _place_cast([W['ffn2_w_gate'], W['ffn2_w_up'], ffn2_w_down], idx, BF16, "place_ffn2")
    fsend, frecv, ff2l, ftok = _gather_start([ff2l], mtok, "gather_ffn2_start")
    wa_bd = _block_diag(lru_w_a, per).astype(BF16)
    wx_bd = _block_diag(lru_w_x, per).astype(BF16)

    x1, a1, b1 = _ffn_fwd(xs, row(ffn1_norm) + ftok[0:1, 0:1], wff1, "ffn1_fwd")
    win, wout, taps = _gather_wait(msend, mrecv, mixl, x1, "gather_mix_wait")
    win, wout, taps = win[0], wout.reshape(-1, D), taps[0]
    conv_w_full = taps[:, :K].transpose(1, 0, 2).reshape(K, N_CHIPS * Cs)
    lru_w4_full = taps[:, kp:kp + K4].transpose(1, 0, 2).reshape(K4, N_CHIPS * Cs)
    z = _mix_in_fwd(x1, row(mix_norm), win)
    u, u1 = _conv_fwd(z, conv_w_full, row(conv_dw_bias), row(conv_ln_g), row(conv_ln_b))
    yr, hs = _lru_fwd(z, 2 * C, lru_w4_full, row(lru_conv_b), wa_bd, row(lru_b_a), wx_bd, row(lru_b_x),
                      row(lru_lambda))
    x2 = _mix_out_fwd(x1, u, yr, wout)
    (wff2,) = _gather_wait(fsend, frecv, ff2l, x2, "gather_ffn2_wait")
    dx3, a2, b2, loss_blk, d_final = _ffn_fwd(x2, row(ffn2_norm), wff2, "ffn2_fwd", head=(row(final_norm), tgt))

    dx2, da2, db2, p2, hb2, dyh2, d_ffn2n = _ffn_bwd_tok(dx3, x2, row(ffn2_norm), a2, b2, wff2, "ffn2_bwd")
    dwg2, dwu2, dwd2 = _ffn_wgrad([([da2, db2], hb2), ([p2], dyh2)], ftok, "ffn2_wgrad")
    wsend, wrecv, f2g, f2o, wtok = _swap_start([dwg2, dwu2, dwd2], "swap_ffn2_start")
    dcat, dwout = _mix_out_bwd(dx2, u, yr, wout)
    dzc, cst = _conv_bwd(dcat, u1, z, conv_w_full, row(conv_ln_g) + wtok[0:1, 0:1], row(conv_ln_b))
    dzx, dzg, lst, dwa_bd, dwx_bd = _lru_bwd(dcat, C, hs, z, 2 * C, lru_w4_full, row(lru_conv_b), wa_bd,
                                              row(lru_b_a), wx_bd, row(lru_b_x), row(lru_lambda))
    dx1, dwin, d_mixn = _mix_in_bwd(dzc, dzx, dzg, x1, dx2, row(mix_norm), win)

    early_names = ['w_in', 'w_out', 'ffn2_w_gate', 'ffn2_w_up', 'ffn2_w_down']
    mixg = [dwin, dwout.reshape(N_CHIPS, -1, D)]
    mixo = _swap_halves_out(mixg, "swap_halves_mix")
    f2g, f2o = _swap_wait(wsend, wrecv, f2g, f2o, dwin, "swap_ffn2_wait")
    e_parts = [_add_cast(g, o, cidx, "add_cast_" + n)
               for g, o, n in zip(mixg + list(f2g), list(mixo) + list(f2o), early_names)]
    esend, erecv, e_parts, e_lands, etok = _exchange_start(e_parts, "exchange_early_start")

    dx0, da1, db1, p1, hb1, dyh1, d_ffn1n = _ffn_bwd_tok(dx1, xs, row(ffn1_norm) + etok[0:1, 0:1], a1, b1, wff1,
                                                         "ffn1_bwd")

    small_names = ['ffn1_norm', 'mix_norm', 'conv_dw', 'conv_dw_bias', 'conv_ln_g', 'conv_ln_b', 'lru_conv_w',
                   'lru_conv_b', 'lru_w_a', 'lru_b_a', 'lru_w_x', 'lru_b_x', 'lru_lambda', 'ffn2_norm',
                   'final_norm']
    small = {
        'ffn1_norm': d_ffn1n, 'mix_norm': d_mixn, 'conv_dw': cst[:K], 'conv_dw_bias': cst[K + 1],
        'conv_ln_g': cst[K + 2], 'conv_ln_b': cst[K + 3], 'lru_conv_w': lst[:K4], 'lru_conv_b': lst[K4],
        'lru_w_a': _block_diag_take(dwa_bd, per), 'lru_b_a': lst[K4 + 1],
        'lru_w_x': _block_diag_take(dwx_bd, per), 'lru_b_x': lst[K4 + 2], 'lru_lambda': lst[K4 + 3],
        'ffn2_norm': d_ffn2n, 'final_norm': d_final,
    }
    packed, rows = _pack([small[n] for n in small_names] + [loss_blk[0:1, 0:1]])
    ssend, srecv, packed, sslots, stok = _small_start(packed)

    gu_names, d_names = ['ffn1_w_gate', 'ffn1_w_up'], ['ffn1_w_down']
    gu = _ffn_wgrad([([da1, db1], hb1)], stok, "ffn1_wgrad_gu", swap=True)
    gu_parts = [_add_cast(g, o, cidx, "add_cast_" + n) for g, o, n in zip(gu[:2], gu[2:], gu_names)]
    gsend, grecv, gu_parts, gu_lands, gtok = _exchange_start(gu_parts, "exchange_gu_start")
    dn = _ffn_wgrad([([p1], dyh1)], gtok, "ffn1_wgrad_d", swap=True)
    dwd1 = dn[0]
    d_parts = [_add_cast(g, o, cidx, "add_cast_" + n) for g, o, n in zip(dn[:1], dn[1:], d_names)]
    dsend, drecv, d_parts, d_lands, ltok = _exchange_start(d_parts, "exchange_d_start")
    e_parts, e_slots = _exchange_wait(esend, erecv, e_parts, e_lands, ltok, "exchange_early_wait")
    delta, new_m, new_v = {}, {}, {}

    def finish(group, parts, slots, tag):
        halves = [_sum_slots(p, b, idx, "sum_slots_" + n) for p, b, n in zip(parts, slots, group)]
        for n, g in zip(group, _share_halves(halves, "share_halves_" + tag)):
            G[n] = g
            delta[n], new_m[n], new_v[n] = _adamw(W[n], g, M[n], V[n], "adamw_" + n)

    G = {}
    finish(early_names, e_parts, e_slots, "early")

    full_shapes = [(K, C) if n == 'conv_dw' else (K4, Wl) if n == 'lru_conv_w' else W[n].shape for n in small_names]
    packed, sslots = _small_wait(ssend, srecv, packed, sslots, dwd1)
    summed = _sum_devices(packed, sslots, (4 * xi + 2 * yi + ci).astype(jnp.int32).reshape(1))
    *small_sums, loss_sum = _unpack(summed, rows, full_shapes + [(1, 1)])
    for n, gsum in zip(small_names, small_sums):
        if n == 'conv_dw':
            gsum = lax.dynamic_slice_in_dim(gsum, chip * Cs, Cs, axis=1)
        elif n == 'lru_conv_w':
            gsum = lax.dynamic_slice_in_dim(gsum, chip * lru_conv_w.shape[1], lru_conv_w.shape[1], axis=1)
        G[n] = gsum

    pw, prow = _pack([W[n] for n in small_names])
    pg, _ = _pack([G[n] for n in small_names])
    pm, _ = _pack([M[n] for n in small_names])
    pv, _ = _pack([V[n] for n in small_names])
    sd, sm, sv = _adamw(pw, pg, pm, pv, "adamw_small")
    shapes = [W[n].shape for n in small_names]
    for n, a, b, c_ in zip(small_names, _unpack(sd, prow, shapes), _unpack(sm, prow, shapes),
                           _unpack(sv, prow, shapes)):
        delta[n], new_m[n], new_v[n] = a, b, c_

    done = sd[0:SUBLANES] + delta[early_names[-1]][0:SUBLANES, 0:LANES]
    gu_parts, gu_slots = _exchange_wait(gsend, grecv, gu_parts, gu_lands, done, "exchange_gu_wait")
    d_parts, d_slots = _exchange_wait(dsend, drecv, d_parts, d_lands, gu_slots[0], "exchange_d_wait")
    finish(gu_names + d_names, list(gu_parts) + list(d_parts), list(gu_slots) + list(d_slots), "last")

    loss = loss_sum[0, 0]
    grad_x = dx0.reshape(x.shape)
    for n in tform:
        G[n], delta[n], new_m[n], new_v[n] = G[n].T, delta[n].T, new_m[n].T, new_v[n].T
    return (loss, grad_x, *[G[n] for n in names], *[delta[n] for n in names],
            *[new_m[n] for n in names], *[new_v[n] for n in names])
```

```python
import functools
import math

import jax
import jax.numpy as jnp
from jax import lax
from jax.experimental import pallas as pl
from jax.experimental.pallas import tpu as pltpu

F32 = jnp.float32
BF16 = jnp.bfloat16
MESH = pl.DeviceIdType.MESH

RMS_EPS = 1e-6
LN_EPS = 1e-5
LRU_C = 8.0
FFN_RES_SCALE = 0.5
ADAM_LR = 0.001
ADAM_B1 = 0.9
ADAM_B2 = 0.999
ADAM_EPS = 1e-08
ADAM_WD = 0.01
ADAM_STEP = 10

LANES = 128
SUBLANES = 8
CONV_HALO = 32
LRU_HALO = 8
ROW_CHUNK = 64
VMEM_LIMIT = 56 * 1024 * 1024
N_CHIPS = 4
N_DEV = 8
TOK_TILE = 1024
BWD_TILE = 512
FFN_BWD_TILE = 512
BWD_ROWS = 32
FFN_BWD_CHAIN = 256
CONV_TILE = 512
LRU_TILE = 2048
LRU_GROUPS = 8


def _dot(a, b):
    return jnp.dot(a, b, preferred_element_type=F32)


def _dot_nt(a, b):
    return lax.dot_general(a, b, (((1,), (1,)), ((), ())), preferred_element_type=F32)


def _dot_tn(a, b):
    return lax.dot_general(a, b, (((0,), (0,)), ((), ())), preferred_element_type=F32)


def _tile(n, pref, mult=SUBLANES):
    for t in range(min(pref, n), 0, -1):
        if n % t == 0 and t % mult == 0:
            return t
    return n


def _params(*sem):
    return pltpu.CompilerParams(dimension_semantics=sem, vmem_limit_bytes=VMEM_LIMIT)


def _rms_stats(x):
    r = lax.rsqrt(jnp.mean(x * x, axis=-1, keepdims=True) + RMS_EPS)
    return x * r, r


def _rms_bwd(dh, xh, r, g):
    dxh = dh * g
    return r * (dxh - xh * jnp.mean(dxh * xh, axis=-1, keepdims=True))


def _colsum(v):
    return jnp.sum(v, axis=0, keepdims=True)


def _ffn_fwd(x, g, wff, name, head=None):
    T, D = x.shape
    ns, fs = wff.shape[1], wff.shape[2]
    tm = _tile(T, TOK_TILE)
    mc = _tile(tm, FFN_BWD_CHAIN, 16)
    rc = _tile(tm, FFN_BWD_CHAIN)

    def body(*refs):
        j = pl.program_id(1)
        if head is None:
            x_ref, g_ref, w_ref, y_ref, a_ref, b_ref, hb_ref, acc_ref = refs
            wg_ref, wu_ref, wd_ref = w_ref.at[0, j], w_ref.at[1, j], w_ref.at[2, j]
        else:
            (x_ref, g_ref, wg_ref, wu_ref, wd_ref, gf_ref, t_ref,
             y_ref, a_ref, b_ref, loss_ref, dgf_ref, hb_ref, acc_ref) = refs

        @pl.when(j == 0)
        def _():
            xh, _ = _rms_stats(x_ref[...])
            hb_ref[...] = (xh * g_ref[...]).astype(BF16)
            acc_ref[...] = jnp.zeros_like(acc_ref)

        if head is not None:
            @pl.when((pl.program_id(0) == 0) & (j == 0))
            def _():
                loss_ref[...] = jnp.zeros_like(loss_ref)
                dgf_ref[...] = jnp.zeros_like(dgf_ref)

        for q0 in range(0, tm, mc):
            blk = pl.ds(q0, mc)
            hb = hb_ref[blk, :]
            a = _dot_nt(hb, wg_ref[...])
            b = _dot_nt(hb, wu_ref[...])
            a_ref[blk, :] = a.astype(BF16)
            b_ref[blk, :] = b.astype(BF16)
            p = (a * jax.nn.sigmoid(a) * b).astype(BF16)
            acc_ref[blk, :] += _dot(p, wd_ref[...])

        @pl.when(j == ns - 1)
        def _():
            if head is None:
                y_ref[...] = x_ref[...] + FFN_RES_SCALE * acc_ref[...]
                return
            gv = gf_ref[...]
            loss = jnp.zeros((), F32)
            dg = jnp.zeros((1, D), F32)
            for r0 in range(0, tm, rc):
                rows = pl.ds(r0, rc)
                xh, r = _rms_stats(x_ref[rows, :] + FFN_RES_SCALE * acc_ref[rows, :])
                e = xh * gv - t_ref[rows, :]
                loss = loss + 0.5 * jnp.sum(jnp.mean(e * e, axis=-1, keepdims=True))
                dy = e * (1.0 / D)
                dg = dg + _colsum(dy * xh)
                y_ref[rows, :] = _rms_bwd(dy, xh, r, gv)
            loss_ref[...] += loss
            dgf_ref[...] += dg

    def wspec(n):
        return pl.BlockSpec((None, None, fs, D), lambda i, j: (n, j, 0, 0))

    tok = pl.BlockSpec((tm, D), lambda i, j: (i, 0))
    vec = pl.BlockSpec((1, D), lambda i, j: (0, 0))
    mid = pl.BlockSpec((None, tm, fs), lambda i, j: (j, i, 0))
    out_specs = [tok, mid, mid]
    out_shape = [jax.ShapeDtypeStruct((T, D), F32), jax.ShapeDtypeStruct((ns, T, fs), BF16),
                 jax.ShapeDtypeStruct((ns, T, fs), BF16)]
    if head is None:
        in_specs = [tok, vec, pl.BlockSpec(wff.shape, lambda i, j: (0, 0, 0, 0), pipeline_mode=pl.Buffered(1))]
        args = [x, g, wff]
    else:
        in_specs = [tok, vec, wspec(0), wspec(1), wspec(2), vec, tok]
        args = [x, g, wff, wff, wff]
        out_specs += [pl.BlockSpec((SUBLANES, LANES), lambda i, j: (0, 0)), vec]
        out_shape += [jax.ShapeDtypeStruct((SUBLANES, LANES), F32), jax.ShapeDtypeStruct((1, D), F32)]
        args += list(head)
    return pl.pallas_call(
        body, grid=(T // tm, ns), in_specs=in_specs, out_specs=out_specs, out_shape=out_shape,
        scratch_shapes=[pltpu.VMEM((tm, D), BF16), pltpu.VMEM((tm, D), F32)],
        compiler_params=_params("arbitrary", "arbitrary"), name=name)(*args)


def _ffn_bwd_tok(dy, x, g, a, b, wff, name):
    T, D = x.shape
    ns, fs = wff.shape[1], wff.shape[2]
    tm = _tile(T, FFN_BWD_TILE)
    rc = _tile(tm, BWD_ROWS)
    mc = _tile(tm, FFN_BWD_CHAIN, rc)

    def body(dy_ref, x_ref, g_ref, a_ref, b_ref, w_ref,
             dx_ref, da_ref, db_ref, p_ref, hb_ref, dyh_ref, dg_ref, dh_ref, dp_ref):
        i, j = pl.program_id(0), pl.program_id(1)
        cur = dp_ref.at[j % 2]
        nxt = dp_ref.at[(j + 1) % 2]
        wg_ref, wu_ref = w_ref.at[0, j], w_ref.at[1, j]
        wd0_ref, wdn_ref = w_ref.at[2, 0], w_ref.at[2, jnp.minimum(j + 1, ns - 1)]

        @pl.when((i == 0) & (j == 0))
        def _():
            dg_ref[...] = jnp.zeros_like(dg_ref)

        @pl.when(j == 0)
        def _():
            for r0 in range(0, tm, rc):
                rows = pl.ds(r0, rc)
                xh, _ = _rms_stats(x_ref[rows, :])
                hb_ref[rows, :] = (xh * g_ref[...]).astype(BF16)
                dyh_ref[rows, :] = (FFN_RES_SCALE * dy_ref[rows, :]).astype(BF16)
            dh_ref[...] = jnp.zeros_like(dh_ref)
            cur[...] = _dot_nt(dyh_ref[...], wd0_ref[...])

        def chains(with_next):
            for q0 in range(0, tm, mc):
                blk = pl.ds(q0, mc)
                for r0 in range(q0, q0 + mc, rc):
                    rows = pl.ds(r0, rc)
                    av = a_ref[rows, :].astype(F32)
                    bv = b_ref[rows, :].astype(F32)
                    dp = cur[rows, :]
                    s = jax.nn.sigmoid(av)
                    sl = av * s
                    da_ref[rows, :] = (dp * bv * (s * (1.0 + av * (1.0 - s)))).astype(BF16)
                    db_ref[rows, :] = (dp * sl).astype(BF16)
                    p_ref[rows, :] = (sl * bv).astype(BF16)
                if with_next:
                    nxt[blk, :] = _dot_nt(dyh_ref[blk, :], wdn_ref[...])
                dh_ref[blk, :] += _dot(da_ref[blk, :], wg_ref[...]) + _dot(db_ref[blk, :], wu_ref[...])

        pl.when(j < ns - 1)(functools.partial(chains, True))
        pl.when(j == ns - 1)(functools.partial(chains, False))

        @pl.when(j == ns - 1)
        def _():
            gv = g_ref[...]
            dg = jnp.zeros((1, D), F32)
            for r0 in range(0, tm, rc):
                rows = pl.ds(r0, rc)
                xh, r = _rms_stats(x_ref[rows, :])
                dh = dh_ref[rows, :]
                dx_ref[rows, :] = dy_ref[rows, :] + _rms_bwd(dh, xh, r, gv)
                dg = dg + _colsum(dh * xh)
            dg_ref[...] += dg

    tok = pl.BlockSpec((tm, D), lambda i, j: (i, 0))
    mid = pl.BlockSpec((None, tm, fs), lambda i, j: (j, i, 0))
    vec = pl.BlockSpec((1, D), lambda i, j: (0, 0))
    return pl.pallas_call(
        body, grid=(T // tm, ns),
        in_specs=[tok, tok, vec, mid, mid,
                  pl.BlockSpec(wff.shape, lambda i, j: (0, 0, 0, 0), pipeline_mode=pl.Buffered(1))],
        out_specs=[tok, mid, mid, mid, tok, tok, vec],
        out_shape=[jax.ShapeDtypeStruct((T, D), F32),
                   jax.ShapeDtypeStruct((ns, T, fs), BF16), jax.ShapeDtypeStruct((ns, T, fs), BF16),
                   jax.ShapeDtypeStruct((ns, T, fs), BF16),
                   jax.ShapeDtypeStruct((T, D), BF16), jax.ShapeDtypeStruct((T, D), BF16),
                   jax.ShapeDtypeStruct((1, D), F32)],
        scratch_shapes=[pltpu.VMEM((tm, D), F32), pltpu.VMEM((2, tm, fs), F32)],
        compiler_params=_params("arbitrary", "arbitrary"), name=name)(dy, x, g, a, b, wff)


def _ffn_wgrad(groups, after, name, swap=False):
    flat = [(l, gi) for gi, (ls, _) in enumerate(groups) for l in ls]
    ng, n = len(groups), len(flat)
    T, D = groups[0][1].shape
    ns, _, fs = flat[0][0].shape
    tm = _tile(T, TOK_TILE)
    nI = T // tm
    rh = fs // 2

    def body(*refs):
        rhs_refs, lhs_refs, out_refs = refs[:ng], refs[ng:ng + n], refs[ng + n + 1:ng + 2 * n + 1]
        j, i = pl.program_id(0), pl.program_id(1)

        @pl.when(i == 0)
        def _():
            for o in out_refs:
                o[...] = jnp.zeros_like(o)

        rvs = [r[...] for r in rhs_refs]
        for l, o, (_, gi) in zip(lhs_refs, out_refs, flat):
            o[...] += _dot_tn(l[...], rvs[gi])

        if swap:
            land_refs = refs[ng + 2 * n + 1:ng + 3 * n + 1]
            send, recv, stage = refs[ng + 3 * n + 1:]
            x, y, c = _here()

            def copies(jj):
                return [pltpu.make_async_remote_copy(
                    src_ref=stage.at[jj % 2, k], dst_ref=land_refs[k].at[jj],
                    send_sem=send.at[k * ns + jj], recv_sem=recv.at[k * ns + jj],
                    device_id=(x, y, 1 - c), device_id_type=MESH) for k in range(n)]

            @pl.when(i == nI - 1)
            def _():
                theirs = pl.ds(pl.multiple_of((1 - c) * rh, SUBLANES), rh)
                for k in range(n):
                    stage[j % 2, k] = out_refs[k][theirs, :]
                for cp in copies(j):
                    cp.start()

            @pl.when((i == nI - 1) & (j > 0))
            def _():
                for cp in copies(j - 1):
                    cp.wait_send()

            @pl.when((i == nI - 1) & (j == ns - 1))
            def _():
                for cp in copies(j):
                    cp.wait_send()
                for jj in range(ns):
                    for cp in copies(jj):
                        cp.wait_recv()

    tok = pl.BlockSpec((tm, D), lambda j, i: (i, 0))
    mid = pl.BlockSpec((None, tm, fs), lambda j, i: (j, i, 0))
    wsp = pl.BlockSpec((None, fs, D), lambda j, i: (j, 0, 0))
    sds = jax.ShapeDtypeStruct((ns, fs, D), F32)
    out_specs, out_shape, scratch = [wsp] * n, [sds] * n, []
    if swap:
        out_specs += [ANY] * n
        out_shape += [jax.ShapeDtypeStruct((ns, rh, D), F32)] * n
        scratch = [pltpu.SemaphoreType.DMA((n * ns,)), pltpu.SemaphoreType.DMA((n * ns,)),
                   pltpu.VMEM((2, n, rh, D), F32)]
    return pl.pallas_call(
        body, grid=(ns, nI),
        in_specs=[tok] * ng + [mid] * n + [pl.BlockSpec((SUBLANES, LANES), lambda j, i: (0, 0))],
        out_specs=out_specs, out_shape=out_shape, scratch_shapes=scratch,
        compiler_params=_params("arbitrary", "arbitrary"), name=name)(
            *[r for _, r in groups], *[l for l, _ in flat], after)


def _mix_in_fwd(x, g, win):
    T, D = x.shape
    ns, ws = win.shape[0], win.shape[2]
    tm = _tile(T, TOK_TILE)

    def body(x_ref, g_ref, w_ref, z_ref):
        xh, _ = _rms_stats(x_ref[...])
        hb = (xh * g_ref[...]).astype(BF16)
        for j in range(ns):
            z_ref[:, pl.ds(j * ws, ws)] = _dot(hb, w_ref[j])

    return pl.pallas_call(
        body, grid=(T // tm,),
        in_specs=[pl.BlockSpec((tm, D), lambda i: (i, 0)), pl.BlockSpec((1, D), lambda i: (0, 0)),
                  pl.BlockSpec((ns, D, ws), lambda i: (0, 0, 0), pipeline_mode=pl.Buffered(1))],
        out_specs=pl.BlockSpec((tm, ns * ws), lambda i: (i, 0)),
        out_shape=jax.ShapeDtypeStruct((T, ns * ws), F32),
        compiler_params=_params("parallel"), name="mix_in_fwd")(x, g, win)


def _tap_sum(buf, w_ref, ntaps, first_row, r0, rows, flip):
    acc = None
    for k in range(ntaps):
        off = (ntaps - 1 - k) if flip else k
        t = buf[pl.ds(first_row + r0 + off, rows), :] * w_ref[pl.ds(k, 1), :]
        acc = t if acc is None else acc + t
    return acc


def _shift_copies(buf, sh, rows):
    for r in range(1, SUBLANES):
        sh[r - 1, pl.ds(0, rows), :] = buf[pl.ds(r, rows), :]


def _tap_rows(buf, sh, off, r0, rows):
    r = off % SUBLANES
    if r == 0:
        return buf[pl.ds(off + r0, rows), :]
    return sh[r - 1, pl.ds(off - r + r0, rows), :]


def _tap_sum_tiles(buf, sh, w_ref, ntaps, first_row, r0, rows, flip):
    acc = None
    for k in range(ntaps):
        off = first_row + ((ntaps - 1 - k) if flip else k)
        t = _tap_rows(buf, sh, off, r0, rows) * w_ref[pl.ds(k, 1), :]
        acc = t if acc is None else acc + t
    return acc


def _conv_fwd(z, w, bias, lng, lnb):
    T = z.shape[0]
    K, C = w.shape
    tm = _tile(T, CONV_TILE, ROW_CHUNK)
    rc = min(ROW_CHUNK, tm)
    srows = tm + CONV_HALO - SUBLANES

    def body(cv_ref, cg_ref, w_ref, b_ref, g_ref, bb_ref, u_ref, u1_ref, buf, sh):
        @pl.when(pl.program_id(0) == 0)
        def _():
            buf[pl.ds(0, CONV_HALO), :] = jnp.zeros((CONV_HALO, C), F32)

        buf[pl.ds(CONV_HALO, tm), :] = cv_ref[...] * jax.nn.sigmoid(cg_ref[...])
        _shift_copies(buf, sh, srows)
        for r0 in range(0, tm, rc):
            u1 = _tap_sum_tiles(buf, sh, w_ref, K, CONV_HALO - (K - 1), r0, rc, False) + b_ref[...]
            u1_ref[pl.ds(r0, rc), :] = u1
            xc = u1 - jnp.mean(u1, axis=-1, keepdims=True)
            xh = xc * lax.rsqrt(jnp.mean(xc * xc, axis=-1, keepdims=True) + LN_EPS)
            u2 = xh * g_ref[...] + bb_ref[...]
            u_ref[pl.ds(r0, rc), :] = (u2 * jax.nn.sigmoid(u2)).astype(BF16)
        buf[pl.ds(0, CONV_HALO), :] = buf[pl.ds(tm, CONV_HALO), :]

    vec = pl.BlockSpec((1, C), lambda i: (0, 0))
    return pl.pallas_call(
        body, grid=(T // tm,),
        in_specs=[pl.BlockSpec((tm, C), lambda i: (i, 0)), pl.BlockSpec((tm, C), lambda i: (i, 1)),
                  pl.BlockSpec((K, C), lambda i: (0, 0)), vec, vec, vec],
        out_specs=[pl.BlockSpec((tm, C), lambda i: (i, 0)), pl.BlockSpec((tm, C), lambda i: (i, 0))],
        out_shape=[jax.ShapeDtypeStruct((T, C), BF16), jax.ShapeDtypeStruct((T, C), F32)],
        scratch_shapes=[pltpu.VMEM((CONV_HALO + tm, C), F32), pltpu.VMEM((SUBLANES - 1, srows, C), F32)],
        compiler_params=_params("arbitrary"), name="conv_fwd")(z, z, w, bias, lng, lnb)


def _conv_bwd(dcat, u1, z, w, lng, lnb):
    T = z.shape[0]
    K, C = w.shape
    tm = _tile(T, CONV_TILE, ROW_CHUNK)
    rc = min(ROW_CHUNK, tm)
    nI = T // tm
    hb = tm // CONV_HALO
    srows = ((K + 4 + SUBLANES - 1) // SUBLANES) * SUBLANES
    shrows = tm + CONV_HALO - SUBLANES

    def body(du_ref, u1_ref, cv_ref, cg_ref, cvp_ref, cgp_ref, w_ref, g_ref, bb_ref,
             dz_ref, st_ref, u0buf, d1buf, ush, dsh):
        i = pl.program_id(0)
        ti = nI - 1 - i

        @pl.when(i == 0)
        def _():
            st_ref[...] = jnp.zeros_like(st_ref)
            d1buf[pl.ds(tm, CONV_HALO), :] = jnp.zeros((CONV_HALO, C), F32)

        prev = cvp_ref[...] * jax.nn.sigmoid(cgp_ref[...])
        u0buf[pl.ds(0, CONV_HALO), :] = jnp.where(ti == 0, 0.0, prev)
        u0buf[pl.ds(CONV_HALO, tm), :] = cv_ref[...] * jax.nn.sigmoid(cg_ref[...])

        gv = g_ref[...]
        dbias = jnp.zeros((1, C), F32)
        dgain = jnp.zeros((1, C), F32)
        dlnb = jnp.zeros((1, C), F32)
        for r0 in range(0, tm, rc):
            u1 = u1_ref[pl.ds(r0, rc), :]
            xc = u1 - jnp.mean(u1, axis=-1, keepdims=True)
            rstd = lax.rsqrt(jnp.mean(xc * xc, axis=-1, keepdims=True) + LN_EPS)
            xh = xc * rstd
            u2 = xh * gv + bb_ref[...]
            s = jax.nn.sigmoid(u2)
            du2 = du_ref[pl.ds(r0, rc), :] * (s * (1.0 + u2 * (1.0 - s)))
            dgain = dgain + _colsum(du2 * xh)
            dlnb = dlnb + _colsum(du2)
            dxh = du2 * gv
            du1 = rstd * (dxh - jnp.mean(dxh, axis=-1, keepdims=True)
                          - xh * jnp.mean(dxh * xh, axis=-1, keepdims=True))
            dbias = dbias + _colsum(du1)
            d1buf[pl.ds(r0, rc), :] = du1
        st_ref[pl.ds(K + 1, 1), :] += dbias
        st_ref[pl.ds(K + 2, 1), :] += dgain
        st_ref[pl.ds(K + 3, 1), :] += dlnb

        _shift_copies(u0buf, ush, shrows)
        _shift_copies(d1buf, dsh, shrows)
        for k in range(K):
            acc = jnp.zeros((SUBLANES, C), F32)
            for r0 in range(0, tm, rc):
                prod = d1buf[pl.ds(r0, rc), :] * _tap_rows(u0buf, ush, CONV_HALO - (K - 1) + k, r0, rc)
                acc = acc + jnp.sum(prod.reshape(rc // SUBLANES, SUBLANES, C), axis=0)
            st_ref[pl.ds(k, 1), :] += _colsum(acc)

        for r0 in range(0, tm, rc):
            du0 = _tap_sum_tiles(d1buf, dsh, w_ref, K, 0, r0, rc, True)
            cv = cv_ref[pl.ds(r0, rc), :]
            sg = jax.nn.sigmoid(cg_ref[pl.ds(r0, rc), :])
            dz_ref[pl.ds(r0, rc), pl.ds(0, C)] = (du0 * sg).astype(BF16)
            dz_ref[pl.ds(r0, rc), pl.ds(C, C)] = (du0 * cv * sg * (1.0 - sg)).astype(BF16)
        d1buf[pl.ds(tm, CONV_HALO), :] = d1buf[pl.ds(0, CONV_HALO), :]

    def rev(col):
        return lambda i: (nI - 1 - i, col)

    def rev_prev(col):
        return lambda i: (jnp.maximum((nI - 1 - i) * hb - 1, 0), col)

    vec = pl.BlockSpec((1, C), lambda i: (0, 0))
    return pl.pallas_call(
        body, grid=(nI,),
        in_specs=[pl.BlockSpec((tm, C), rev(0)), pl.BlockSpec((tm, C), rev(0)),
                  pl.BlockSpec((tm, C), rev(0)), pl.BlockSpec((tm, C), rev(1)),
                  pl.BlockSpec((CONV_HALO, C), rev_prev(0)), pl.BlockSpec((CONV_HALO, C), rev_prev(1)),
                  pl.BlockSpec((K, C), lambda i: (0, 0)), vec, vec],
        out_specs=[pl.BlockSpec((tm, 2 * C), rev(0)), pl.BlockSpec((srows, C), lambda i: (0, 0))],
        out_shape=[jax.ShapeDtypeStruct((T, 2 * C), BF16), jax.ShapeDtypeStruct((srows, C), F32)],
        scratch_shapes=[pltpu.VMEM((CONV_HALO + tm, C), F32), pltpu.VMEM((tm + CONV_HALO, C), F32),
                        pltpu.VMEM((SUBLANES - 1, shrows, C), F32), pltpu.VMEM((SUBLANES - 1, shrows, C), F32)],
        compiler_params=_params("arbitrary"), name="conv_bwd")(dcat, u1, z, z, z, z, w, lng, lnb)


def _softplus(v):
    return jnp.maximum(v, 0.0) + jnp.log(1.0 + jnp.exp(-jnp.abs(v)))


def _gelu(v):
    c = math.sqrt(2.0 / math.pi)
    t = jnp.tanh(c * (v + 0.044715 * v * v * v))
    gl = 0.5 * v * (1.0 + t)
    dgl = 0.5 * (1.0 + t) + 0.5 * v * (1.0 - t * t) * c * (1.0 + 3.0 * 0.044715 * v * v)
    return gl, dgl


def _lru_gates(xr, wa, ba, wx, bx, lam):
    xb = xr.astype(BF16)
    r = jax.nn.sigmoid(_dot(xb, wa) + ba)
    ig = jax.nn.sigmoid(_dot(xb, wx) + bx)
    sp = _softplus(-lam)
    log_a = -LRU_C * r * sp
    a = jnp.exp(log_a)
    y = 2.0 * log_a
    series = -(y * (1.0 + y * (0.5 + y * (1.0 / 6.0 + y * (1.0 / 24.0)))))
    mult = jnp.sqrt(jnp.where(y > -0.02, series, 1.0 - jnp.exp(y)))
    return a, mult, r, ig, sp


def _scan_tile(a_s, b_s, h_s, p_s, carry, seg, reverse):
    hl = [jnp.zeros((SUBLANES, LANES), F32)] * LRU_GROUPS
    pr = [jnp.ones((SUBLANES, LANES), F32)] * LRU_GROUPS
    for n in range(seg):
        for g in range(LRU_GROUPS):
            rows = pl.ds(g * SUBLANES * seg + ((seg - 1 - n) if reverse else n), SUBLANES, stride=seg)
            av = a_s[rows, :]
            hl[g] = av * hl[g] + b_s[rows, :]
            pr[g] = av * pr[g]
            h_s[rows, :] = hl[g]
            p_s[rows, :] = pr[g]
    nseg = SUBLANES * LRU_GROUPS
    cs = [None] * nseg
    c = carry
    for s in (range(nseg - 1, -1, -1) if reverse else range(nseg)):
        g, r = divmod(s, SUBLANES)
        cs[s] = c
        c = hl[g][r:r + 1, :] + pr[g][r:r + 1, :] * c
    return cs, c


def _lru_fwd(z, col0, w4, b4, wa, ba, wx, bx, lam):
    T = z.shape[0]
    K4, W = w4.shape
    nC = W // LANES
    tm = _tile(T, LRU_TILE, SUBLANES * SUBLANES * LRU_GROUPS)
    seg = tm // (SUBLANES * LRU_GROUPS)
    cx, cg = col0 // LANES, (col0 + W) // LANES

    def body(rx_ref, rg_ref, w4_ref, b4_ref, wa_ref, ba_ref, wx_ref, bx_ref, lam_ref,
             yr_ref, hs_ref, xbuf, a_s, b_s, h_s, p_s, hc):
        @pl.when(pl.program_id(1) == 0)
        def _():
            xbuf[pl.ds(0, LRU_HALO), :] = jnp.zeros((LRU_HALO, LANES), F32)
            hc[...] = jnp.zeros_like(hc)

        xbuf[pl.ds(LRU_HALO, tm), :] = rx_ref[...]
        xr = _tap_sum(xbuf, w4_ref, K4, LRU_HALO - (K4 - 1), 0, tm, False) + b4_ref[...]
        a, mult, _, ig, _ = _lru_gates(xr, wa_ref[...], ba_ref[...], wx_ref[...], bx_ref[...], lam_ref[...])
        a_s[...] = a
        b_s[...] = mult * ig * xr
        cs, cout = _scan_tile(a_s, b_s, h_s, p_s, hc[pl.ds(0, 1), :], seg, False)
        hc[pl.ds(0, 1), :] = cout
        for s in range(SUBLANES * LRU_GROUPS):
            rows = pl.ds(s * seg, seg)
            h = h_s[rows, :] + p_s[rows, :] * cs[s]
            hs_ref[rows, :] = h
            gl, _ = _gelu(rg_ref[rows, :])
            yr_ref[rows, :] = (h * gl).astype(BF16)
        xbuf[pl.ds(0, LRU_HALO), :] = xbuf[pl.ds(tm, LRU_HALO), :]

    vec = pl.BlockSpec((1, LANES), lambda c, i: (0, c))
    mat = pl.BlockSpec((None, LANES, LANES), lambda c, i: (c, 0, 0))
    return pl.pallas_call(
        body, grid=(nC, T // tm),
        in_specs=[pl.BlockSpec((tm, LANES), lambda c, i: (i, cx + c)),
                  pl.BlockSpec((tm, LANES), lambda c, i: (i, cg + c)),
                  pl.BlockSpec((K4, LANES), lambda c, i: (0, c)), vec, mat, vec, mat, vec, vec],
        out_specs=[pl.BlockSpec((tm, LANES), lambda c, i: (i, c)), pl.BlockSpec((tm, LANES), lambda c, i: (i, c))],
        out_shape=[jax.ShapeDtypeStruct((T, W), BF16), jax.ShapeDtypeStruct((T, W), F32)],
        scratch_shapes=[pltpu.VMEM((LRU_HALO + tm, LANES), F32)] + [pltpu.VMEM((tm, LANES), F32)] * 4
        + [pltpu.VMEM((SUBLANES, LANES), F32)],
        compiler_params=_params("parallel", "arbitrary"), name="lru_fwd")(z, z, w4, b4, wa, ba, wx, bx, lam)


def _lru_bwd(dcat, dcol0, hs, z, col0, w4, b4, wa, ba, wx, bx, lam):
    T = z.shape[0]
    K4, W = w4.shape
    assert K4 + 4 == SUBLANES
    nC = W // LANES
    tm = _tile(T, LRU_TILE, SUBLANES * SUBLANES * LRU_GROUPS)
    seg = tm // (SUBLANES * LRU_GROUPS)
    nI = T // tm
    hb = tm // LRU_HALO
    cx, cg, cd = col0 // LANES, (col0 + W) // LANES, dcol0 // LANES

    def body(dyr_ref, hs_ref, hsp_ref, rx_ref, rxp_ref, rg_ref, w4_ref, b4_ref, wa_ref, ba_ref, wx_ref, bx_ref,
             lam_ref, dzx_ref, dzg_ref, st_ref, dwa_ref, dwx_ref, xbuf, hbuf, abuf, a_s, b_s, h_s, p_s, dbuf, gc, anc):
        i = pl.program_id(1)
        ti = nI - 1 - i

        @pl.when(i == 0)
        def _():
            st_ref[...] = jnp.zeros_like(st_ref)
            dwa_ref[...] = jnp.zeros_like(dwa_ref)
            dwx_ref[...] = jnp.zeros_like(dwx_ref)
            gc[...] = jnp.zeros_like(gc)
            anc[...] = jnp.zeros_like(anc)
            dbuf[pl.ds(tm, LRU_HALO), :] = jnp.zeros((LRU_HALO, LANES), F32)

        xbuf[pl.ds(0, LRU_HALO), :] = jnp.where(ti == 0, 0.0, rxp_ref[...])
        xbuf[pl.ds(LRU_HALO, tm), :] = rx_ref[...]
        hbuf[pl.ds(0, LRU_HALO), :] = jnp.where(ti == 0, 0.0, hsp_ref[...])
        hbuf[pl.ds(LRU_HALO, tm), :] = hs_ref[...]

        wa, wx = wa_ref[...], wx_ref[...]
        lam_v = lam_ref[...]
        xr = _tap_sum(xbuf, w4_ref, K4, LRU_HALO - (K4 - 1), 0, tm, False) + b4_ref[...]
        a, mult, r, ig, sp = _lru_gates(xr, wa, ba_ref[...], wx, bx_ref[...], lam_v)

        dyr = dyr_ref[...]
        gl, dgl = _gelu(rg_ref[...])
        dzg_ref[...] = (dyr * hs_ref[...] * dgl).astype(BF16)

        abuf[pl.ds(0, tm), :] = a
        abuf[pl.ds(tm, LRU_HALO), :] = anc[...]
        a_s[...] = abuf[pl.ds(1, tm), :]
        b_s[...] = dyr * gl
        cs, cout = _scan_tile(a_s, b_s, h_s, p_s, gc[pl.ds(0, 1), :], seg, True)
        gc[pl.ds(0, 1), :] = cout
        anc[pl.ds(0, 1), :] = a[0:1, :]
        for s in range(SUBLANES * LRU_GROUPS):
            rows = pl.ds(s * seg, seg)
            b_s[rows, :] = h_s[rows, :] + p_s[rows, :] * cs[s]
        g = b_s[...]

        d_a = g * hbuf[pl.ds(LRU_HALO - 1, tm), :]
        gx_ = g * xr
        d_log_a = d_a * a - (gx_ * ig) * (a * a / mult)
        dga = (d_log_a * (-LRU_C * sp)) * r * (1.0 - r)
        dgx = (gx_ * mult) * ig * (1.0 - ig)
        dga_b, dgx_b = dga.astype(BF16), dgx.astype(BF16)
        dxr = g * mult * ig + _dot_nt(dga_b, wa) + _dot_nt(dgx_b, wx)
        xb = xr.astype(BF16)
        dwa_ref[...] += _dot_tn(xb, dga_b)
        dwx_ref[...] += _dot_tn(xb, dgx_b)
        st_ref[pl.ds(K4, 1), :] += _colsum(dxr)
        st_ref[pl.ds(K4 + 1, 1), :] += _colsum(dga)
        st_ref[pl.ds(K4 + 2, 1), :] += _colsum(dgx)
        st_ref[pl.ds(K4 + 3, 1), :] += _colsum(d_log_a * (-LRU_C * r)) * (-jax.nn.sigmoid(-lam_v))

        dbuf[pl.ds(0, tm), :] = dxr
        for k in range(K4):
            st_ref[pl.ds(k, 1), :] += _colsum(dxr * xbuf[pl.ds(LRU_HALO - (K4 - 1) + k, tm), :])
        dzx_ref[...] = _tap_sum(dbuf, w4_ref, K4, 0, 0, tm, True).astype(BF16)
        dbuf[pl.ds(tm, LRU_HALO), :] = dbuf[pl.ds(0, LRU_HALO), :]

    def rev(col):
        return lambda c, i: (nI - 1 - i, col + c)

    def rev_prev(col):
        return lambda c, i: (jnp.maximum((nI - 1 - i) * hb - 1, 0), col + c)

    vec = pl.BlockSpec((1, LANES), lambda c, i: (0, c))
    mat = pl.BlockSpec((None, LANES, LANES), lambda c, i: (c, 0, 0))
    big = pltpu.VMEM((tm, LANES), F32)
    halo = pltpu.VMEM((tm + LRU_HALO, LANES), F32)
    return pl.pallas_call(
        body, grid=(nC, nI),
        in_specs=[pl.BlockSpec((tm, LANES), rev(cd)),
                  pl.BlockSpec((tm, LANES), rev(0)), pl.BlockSpec((LRU_HALO, LANES), rev_prev(0)),
                  pl.BlockSpec((tm, LANES), rev(cx)), pl.BlockSpec((LRU_HALO, LANES), rev_prev(cx)),
                  pl.BlockSpec((tm, LANES), rev(cg)),
                  pl.BlockSpec((K4, LANES), lambda c, i: (0, c)), vec, mat, vec, mat, vec, vec],
        out_specs=[pl.BlockSpec((tm, LANES), rev(0)), pl.BlockSpec((tm, LANES), rev(0)),
                   pl.BlockSpec((SUBLANES, LANES), lambda c, i: (0, c)), mat, mat],
        out_shape=[jax.ShapeDtypeStruct((T, W), BF16), jax.ShapeDtypeStruct((T, W), BF16),
                   jax.ShapeDtypeStruct((SUBLANES, W), F32),
                   jax.ShapeDtypeStruct((nC, LANES, LANES), F32), jax.ShapeDtypeStruct((nC, LANES, LANES), F32)],
        scratch_shapes=[halo, halo, halo, big, big, big, big, halo,
                        pltpu.VMEM((SUBLANES, LANES), F32), pltpu.VMEM((SUBLANES, LANES), F32)],
        compiler_params=_params("parallel", "arbitrary"), name="lru_bwd")(
            dcat, hs, hs, z, z, z, w4, b4, wa, ba, wx, bx, lam)


def _mix_out_fwd(x, u, yr, wout):
    T, D = x.shape
    C, W = u.shape[1], yr.shape[1]
    tm = _tile(T, TOK_TILE)

    def body(x_ref, u_ref, yr_ref, w_ref, y_ref):
        y_ref[...] = (x_ref[...] + _dot(u_ref[...], w_ref[pl.ds(0, C), :])
                      + _dot(yr_ref[...], w_ref[pl.ds(C, W), :]))

    return pl.pallas_call(
        body, grid=(T // tm,),
        in_specs=[pl.BlockSpec((tm, D), lambda i: (i, 0)), pl.BlockSpec((tm, C), lambda i: (i, 0)),
                  pl.BlockSpec((tm, W), lambda i: (i, 0)),
                  pl.BlockSpec((C + W, D), lambda i: (0, 0), pipeline_mode=pl.Buffered(1))],
        out_specs=pl.BlockSpec((tm, D), lambda i: (i, 0)),
        out_shape=jax.ShapeDtypeStruct((T, D), F32),
        compiler_params=_params("parallel"), name="mix_out_fwd")(x, u, yr, wout)


def _mix_out_bwd(dy, u, yr, wout):
    T, D = dy.shape
    C, W = u.shape[1], yr.shape[1]
    tm = _tile(T, TOK_TILE)

    def body(dy_ref, u_ref, yr_ref, w_ref, dcat_ref, dw_ref):
        @pl.when(pl.program_id(0) == 0)
        def _():
            dw_ref[...] = jnp.zeros_like(dw_ref)

        dyb = dy_ref[...].astype(BF16)
        dcat_ref[...] = _dot_nt(dyb, w_ref[...])
        dw_ref[pl.ds(0, C), :] += _dot_tn(u_ref[...], dyb)
        dw_ref[pl.ds(C, W), :] += _dot_tn(yr_ref[...], dyb)

    return pl.pallas_call(
        body, grid=(T // tm,),
        in_specs=[pl.BlockSpec((tm, D), lambda i: (i, 0)), pl.BlockSpec((tm, C), lambda i: (i, 0)),
                  pl.BlockSpec((tm, W), lambda i: (i, 0)),
                  pl.BlockSpec((C + W, D), lambda i: (0, 0), pipeline_mode=pl.Buffered(1))],
        out_specs=[pl.BlockSpec((tm, C + W), lambda i: (i, 0)), pl.BlockSpec((C + W, D), lambda i: (0, 0))],
        out_shape=[jax.ShapeDtypeStruct((T, C + W), F32), jax.ShapeDtypeStruct((C + W, D), F32)],
        compiler_params=_params("arbitrary"), name="mix_out_bwd")(dy, u, yr, wout)


def _mix_in_bwd(dzc, dzx, dzg, x, dy, g, win):
    T, D = x.shape
    ns, ws = win.shape[0], win.shape[2]
    tm = _tile(T, BWD_TILE)
    parts = []
    for j in range(ns):
        lo = j * ws
        if lo < dzc.shape[1]:
            parts.append((0, lo))
        elif lo < dzc.shape[1] + dzx.shape[1]:
            parts.append((1, lo - dzc.shape[1]))
        else:
            parts.append((2, lo - dzc.shape[1] - dzx.shape[1]))

    def body(dzc_ref, dzx_ref, dzg_ref, x_ref, dy_ref, g_ref, w_ref, dx_ref, dw_ref, dg_ref):
        @pl.when(pl.program_id(0) == 0)
        def _():
            dw_ref[...] = jnp.zeros_like(dw_ref)
            dg_ref[...] = jnp.zeros_like(dg_ref)

        xh, r = _rms_stats(x_ref[...])
        gv = g_ref[...]
        hb = (xh * gv).astype(BF16)
        srcs = (dzc_ref, dzx_ref, dzg_ref)
        dh = jnp.zeros((tm, D), F32)
        for j, (si, off) in enumerate(parts):
            dzj = srcs[si][:, pl.ds(off, ws)]
            dh = dh + _dot_nt(dzj, w_ref[j])
            dw_ref[j] += _dot_tn(hb, dzj)
        dx_ref[...] = dy_ref[...] + _rms_bwd(dh, xh, r, gv)
        dg_ref[...] += _colsum(dh * xh)

    def tok(n):
        return pl.BlockSpec((tm, n), lambda i: (i, 0))

    vec = pl.BlockSpec((1, D), lambda i: (0, 0))
    return pl.pallas_call(
        body, grid=(T // tm,),
        in_specs=[tok(dzc.shape[1]), tok(dzx.shape[1]), tok(dzg.shape[1]), tok(D), tok(D), vec,
                  pl.BlockSpec((ns, D, ws), lambda i: (0, 0, 0), pipeline_mode=pl.Buffered(1))],
        out_specs=[tok(D), pl.BlockSpec((ns, D, ws), lambda i: (0, 0, 0)), vec],
        out_shape=[jax.ShapeDtypeStruct((T, D), F32), jax.ShapeDtypeStruct((ns, D, ws), F32),
                   jax.ShapeDtypeStruct((1, D), F32)],
        compiler_params=_params("arbitrary"), name="mix_in_bwd")(dzc, dzx, dzg, x, dy, g, win)


def _adamw(w, g, m, v, name):
    R, Cc = w.shape
    tr = _tile(R, max(SUBLANES, (1 << 19) // Cc))
    c1 = 1.0 - ADAM_B1 ** ADAM_STEP
    c2 = 1.0 - ADAM_B2 ** ADAM_STEP

    def body(w_ref, g_ref, m_ref, v_ref, d_ref, nm_ref, nv_ref):
        gv = g_ref[...]
        nm = ADAM_B1 * m_ref[...] + (1.0 - ADAM_B1) * gv
        nv = ADAM_B2 * v_ref[...] + (1.0 - ADAM_B2) * (gv * gv)
        nm_ref[...] = nm
        nv_ref[...] = nv
        d_ref[...] = -ADAM_LR * ((nm / c1) / (jnp.sqrt(nv / c2) + ADAM_EPS) + ADAM_WD * w_ref[...])

    blk = pl.BlockSpec((tr, Cc), lambda i: (i, 0))
    sds = jax.ShapeDtypeStruct((R, Cc), F32)
    return pl.pallas_call(
        body, grid=(R // tr,), in_specs=[blk] * 4, out_specs=[blk] * 3, out_shape=[sds] * 3,
        compiler_params=_params("parallel"), name=name)(w, g, m, v)


def _here():
    return lax.axis_index("x"), lax.axis_index("y"), lax.axis_index("c")


def _chip_at(x, y, m):
    return x ^ (m >> 1), y ^ (m & 1)


ANY = pl.BlockSpec(memory_space=pl.ANY)


def _place_cast(srcs, idx, dtype, name):
    n = len(srcs)
    R, Cc = srcs[0].shape
    tr = _tile(R, max(16, (1 << 18) // Cc), 16)

    def body(i_ref, *refs):
        o_ref = refs[n]
        for k in range(n):
            o_ref[k] = refs[k][...].astype(dtype)

    blk = pl.BlockSpec((tr, Cc), lambda i, s: (i, 0))
    return pl.pallas_call(
        body,
        grid_spec=pltpu.PrefetchScalarGridSpec(
            num_scalar_prefetch=1, grid=(R // tr,), in_specs=[blk] * n,
            out_specs=pl.BlockSpec((n, None, tr, Cc), lambda i, s: (0, s[1], i, 0))),
        out_shape=jax.ShapeDtypeStruct((n, N_CHIPS, R, Cc), dtype),
        compiler_params=_params("parallel"), name=name)(idx, *srcs)


def _gather_weights(lands):
    n = len(lands)

    def body(*refs):
        outs = refs[n:2 * n]
        send1, recv1, send2, recv2 = refs[2 * n:]
        x, y, c = _here()
        own = 2 * x + y

        def half(ref, chip, cc):
            rh = ref.shape[-2] // 2
            lead = (slice(None),) * (len(ref.shape) - 3)
            return ref.at[lead + (chip, pl.ds(cc * rh, rh), slice(None))]

        first = []
        for k in range(n):
            for m in (1, 2, 3):
                px, py = _chip_at(x, y, m)
                cp = pltpu.make_async_remote_copy(
                    src_ref=half(outs[k], own, c), dst_ref=half(outs[k], own, c),
                    send_sem=send1.at[k, m - 1], recv_sem=recv1.at[k, m - 1],
                    device_id=(px, py, c), device_id_type=MESH)
                cp.start()
                first.append(cp)

        passed = []
        for k in range(n):
            for m in (1, 2, 3):
                px, py = _chip_at(x, y, m)
                peer = 2 * px + py
                got = half(outs[k], peer, c)
                pltpu.make_async_remote_copy(
                    src_ref=got, dst_ref=got, send_sem=send1.at[k, m - 1], recv_sem=recv1.at[k, m - 1],
                    device_id=(px, py, c), device_id_type=MESH).wait_recv()
                cp = pltpu.make_async_remote_copy(
                    src_ref=got, dst_ref=got, send_sem=send2.at[k, m - 1], recv_sem=recv2.at[k, m - 1],
                    device_id=(x, y, 1 - c), device_id_type=MESH)
                cp.start()
                passed.append(cp)

        for k in range(n):
            for m in (1, 2, 3):
                px, py = _chip_at(x, y, m)
                other = half(outs[k], 2 * px + py, 1 - c)
                pltpu.make_async_remote_copy(
                    src_ref=other, dst_ref=other, send_sem=send2.at[k, m - 1], recv_sem=recv2.at[k, m - 1],
                    device_id=(x, y, 1 - c), device_id_type=MESH).wait_recv()
        for cp in first + passed:
            cp.wait_send()

    return pl.pallas_call(
        body, in_specs=[ANY] * n, out_specs=[ANY] * n,
        out_shape=[jax.ShapeDtypeStruct(a.shape, a.dtype) for a in lands],
        input_output_aliases={k: k for k in range(n)},
        scratch_shapes=[pltpu.SemaphoreType.DMA((n, 3)), pltpu.SemaphoreType.DMA((n, 3)),
                        pltpu.SemaphoreType.DMA((n, 3)), pltpu.SemaphoreType.DMA((n, 3))],
        name="gather_weights")(*lands)


HBM = pl.BlockSpec(memory_space=pltpu.HBM)
SEM = pl.BlockSpec(memory_space=pltpu.SEMAPHORE)
EFFECT = pltpu.SideEffectType.DATAFLOW_SIDE_EFFECTING


def _in_hbm(a):
    return pltpu.with_memory_space_constraint(a, pltpu.HBM)


def _gather_copies(land_refs, send, recv):
    x, y, c = _here()
    own = 2 * x + y
    cps = []
    for k in range(len(land_refs)):
        lead = (slice(None),) * (len(land_refs[k].shape) - 3)
        mine = land_refs[k].at[lead + (own,)]
        for m in (1, 2, 3):
            px, py = _chip_at(x, y, m)
            cps.append(pltpu.make_async_remote_copy(
                src_ref=mine, dst_ref=mine, send_sem=send.at[3 * k + m - 1], recv_sem=recv.at[3 * k + m - 1],
                device_id=(px, py, c), device_id_type=MESH))
    return cps


def _gather_start(lands, after, name):
    n = len(lands)

    def body(*refs):
        lz = refs[:n]
        send, recv = refs[n + 1], refs[n + 2]
        token = refs[-1]
        for cp in _gather_copies(lz, send, recv):
            cp.start()
        token[...] = jnp.zeros_like(token)

    hbm = [pltpu.HBM(a.shape, a.dtype) for a in lands]
    outs = pl.pallas_call(
        body, name=name,
        in_specs=[HBM] * n + [ANY],
        out_specs=[SEM, SEM] + [HBM] * n + [pl.BlockSpec(memory_space=pltpu.VMEM)],
        out_shape=[pltpu.SemaphoreType.DMA((3 * n,)), pltpu.SemaphoreType.DMA((3 * n,))] + hbm
        + [jax.ShapeDtypeStruct((SUBLANES, LANES), F32)],
        input_output_aliases={k: 2 + k for k in range(n)},
        compiler_params=pltpu.CompilerParams(has_side_effects=EFFECT),
    )(*[_in_hbm(a) for a in lands], after)
    return outs[0], outs[1], outs[2:2 + n], outs[-1]


def _gather_wait(send, recv, lands, after, name):
    n = len(lands)

    def body(*refs):
        lz = refs[:n]
        send_r, recv_r = refs[n], refs[n + 1]
        for cp in _gather_copies(lz, send_r, recv_r):
            cp.wait_send()
            cp.wait_recv()

    hbm = [pltpu.HBM(a.shape, a.dtype) for a in lands]
    return pl.pallas_call(
        body, name=name,
        in_specs=[HBM] * n + [SEM, SEM, ANY],
        out_specs=[HBM] * n, out_shape=hbm,
        input_output_aliases={k: k for k in range(n)},
        compiler_params=pltpu.CompilerParams(has_side_effects=EFFECT),
    )(*lands, send, recv, after)


def _exchange_copies(part_refs, slot_refs, send, recv):
    x, y, c = _here()
    cps = []
    for k in range(len(part_refs)):
        for m in (1, 2, 3):
            px, py = _chip_at(x, y, m)
            cps.append(pltpu.make_async_remote_copy(
                src_ref=part_refs[k].at[2 * px + py], dst_ref=slot_refs[k].at[m - 1],
                send_sem=send.at[3 * k + m - 1], recv_sem=recv.at[3 * k + m - 1],
                device_id=(px, py, c), device_id_type=MESH))
    return cps


def _exchange_start(parts, name):
    n = len(parts)
    lands = [lax.empty((N_CHIPS - 1,) + p.shape[1:], p.dtype) for p in parts]

    def body(*refs):
        ins, lz = refs[:n], refs[n:2 * n]
        send, recv = refs[2 * n], refs[2 * n + 1]
        token = refs[-1]
        for cp in _exchange_copies(ins, lz, send, recv):
            cp.start()
        token[...] = jnp.zeros_like(token)

    hbm = [pltpu.HBM(a.shape, a.dtype) for a in list(parts) + lands]
    outs = pl.pallas_call(
        body, name=name,
        in_specs=[HBM] * (2 * n),
        out_specs=[SEM, SEM] + [HBM] * (2 * n) + [pl.BlockSpec(memory_space=pltpu.VMEM)],
        out_shape=[pltpu.SemaphoreType.DMA((3 * n,)), pltpu.SemaphoreType.DMA((3 * n,))] + hbm
        + [jax.ShapeDtypeStruct((SUBLANES, LANES), F32)],
        input_output_aliases={k: 2 + k for k in range(2 * n)},
        compiler_params=pltpu.CompilerParams(has_side_effects=EFFECT),
    )(*[_in_hbm(a) for a in parts], *[_in_hbm(a) for a in lands])
    return outs[0], outs[1], outs[2:2 + n], outs[2 + n:2 + 2 * n], outs[-1]


def _exchange_wait(send, recv, parts, lands, after, name):
    n = len(parts)

    def body(*refs):
        ins, lz = refs[:n], refs[n:2 * n]
        send_r, recv_r = refs[2 * n], refs[2 * n + 1]
        for cp in _exchange_copies(ins, lz, send_r, recv_r):
            cp.wait_send()
            cp.wait_recv()

    hbm = [pltpu.HBM(a.shape, a.dtype) for a in list(parts) + list(lands)]
    outs = pl.pallas_call(
        body, name=name,
        in_specs=[HBM] * (2 * n) + [SEM, SEM, ANY],
        out_specs=[HBM] * (2 * n), out_shape=hbm,
        input_output_aliases={k: k for k in range(2 * n)},
        compiler_params=pltpu.CompilerParams(has_side_effects=EFFECT),
    )(*parts, *lands, send, recv, after)
    return outs[:n], outs[n:]


def _swap_halves_out(grads, name):
    n = len(grads)
    out_shapes = [jax.ShapeDtypeStruct((g.shape[0], g.shape[1] // 2, g.shape[2]), g.dtype) for g in grads]

    def body(*refs):
        ins, outs = refs[:n], refs[n:2 * n]
        send, recv = refs[2 * n:]
        x, y, c = _here()
        cps = []
        for k in range(n):
            rh = ins[k].shape[1] // 2
            cp = pltpu.make_async_remote_copy(
                src_ref=ins[k].at[:, pl.ds((1 - c) * rh, rh), :], dst_ref=outs[k],
                send_sem=send.at[k], recv_sem=recv.at[k], device_id=(x, y, 1 - c), device_id_type=MESH)
            cp.start()
            cps.append(cp)
        for cp in cps:
            cp.wait()

    return pl.pallas_call(
        body, in_specs=[ANY] * n, out_specs=[ANY] * n, out_shape=out_shapes,
        scratch_shapes=[pltpu.SemaphoreType.DMA((n,)), pltpu.SemaphoreType.DMA((n,))],
        name=name)(*grads)


def _swap_copies(grad_refs, land_refs, send, recv):
    x, y, c = _here()
    cps = []
    for k in range(len(grad_refs)):
        rh = grad_refs[k].shape[1] // 2
        cps.append(pltpu.make_async_remote_copy(
            src_ref=grad_refs[k].at[:, pl.ds((1 - c) * rh, rh), :], dst_ref=land_refs[k],
            send_sem=send.at[k], recv_sem=recv.at[k], device_id=(x, y, 1 - c), device_id_type=MESH))
    return cps


def _swap_start(grads, name):
    n = len(grads)
    lands = [lax.empty((g.shape[0], g.shape[1] // 2, g.shape[2]), g.dtype) for g in grads]

    def body(*refs):
        ins, lz = refs[:n], refs[n:2 * n]
        send, recv = refs[2 * n], refs[2 * n + 1]
        token = refs[-1]
        for cp in _swap_copies(ins, lz, send, recv):
            cp.start()
        token[...] = jnp.zeros_like(token)

    hbm = [pltpu.HBM(a.shape, a.dtype) for a in list(grads) + lands]
    outs = pl.pallas_call(
        body, name=name,
        in_specs=[HBM] * (2 * n),
        out_specs=[SEM, SEM] + [HBM] * (2 * n) + [pl.BlockSpec(memory_space=pltpu.VMEM)],
        out_shape=[pltpu.SemaphoreType.DMA((n,)), pltpu.SemaphoreType.DMA((n,))] + hbm
        + [jax.ShapeDtypeStruct((SUBLANES, LANES), F32)],
        input_output_aliases={k: 2 + k for k in range(2 * n)},
        compiler_params=pltpu.CompilerParams(has_side_effects=EFFECT),
    )(*[_in_hbm(a) for a in grads], *[_in_hbm(a) for a in lands])
    return outs[0], outs[1], outs[2:2 + n], outs[2 + n:2 + 2 * n], outs[-1]


def _swap_wait(send, recv, grads, lands, after, name):
    n = len(grads)

    def body(*refs):
        ins, lz = refs[:n], refs[n:2 * n]
        send_r, recv_r = refs[2 * n], refs[2 * n + 1]
        for cp in _swap_copies(ins, lz, send_r, recv_r):
            cp.wait_send()
            cp.wait_recv()

    hbm = [pltpu.HBM(a.shape, a.dtype) for a in list(grads) + list(lands)]
    outs = pl.pallas_call(
        body, name=name,
        in_specs=[HBM] * (2 * n) + [SEM, SEM, ANY],
        out_specs=[HBM] * (2 * n), out_shape=hbm,
        input_output_aliases={k: k for k in range(2 * n)},
        compiler_params=pltpu.CompilerParams(has_side_effects=EFFECT),
    )(*grads, *lands, send, recv, after)
    return outs[:n], outs[n:]


def _add_cast(g, other, cidx, name):
    ns, R, Cc = g.shape
    rh = R // 2
    tr = _tile(rh, max(16, (1 << 19) // Cc), 16)
    nb = rh // tr

    def body(c_ref, g_ref, o_ref, s_ref):
        s_ref[...] = (g_ref[...] + o_ref[...]).astype(BF16)

    return pl.pallas_call(
        body,
        grid_spec=pltpu.PrefetchScalarGridSpec(
            num_scalar_prefetch=1, grid=(ns, nb),
            in_specs=[pl.BlockSpec((None, tr, Cc), lambda k, i, c: (k, c[0] * nb + i, 0)),
                      pl.BlockSpec((None, tr, Cc), lambda k, i, c: (k, i, 0))],
            out_specs=pl.BlockSpec((None, tr, Cc), lambda k, i, c: (k, i, 0))),
        out_shape=jax.ShapeDtypeStruct((ns, rh, Cc), BF16),
        compiler_params=_params("parallel", "parallel"), name=name)(cidx, g, other)


def _sum_slots(part, got, idx, name):
    ns, rh, Cc = got.shape
    tr = _tile(rh, max(16, (1 << 18) // Cc), 16)
    nb = rh // tr

    def body(i_ref, p_ref, b_ref, o_ref):
        acc = p_ref[...].astype(F32)
        for m in range(ns):
            acc = acc + b_ref[m].astype(F32)
        o_ref[...] = acc

    return pl.pallas_call(
        body,
        grid_spec=pltpu.PrefetchScalarGridSpec(
            num_scalar_prefetch=1, grid=(nb,),
            in_specs=[pl.BlockSpec((None, tr, Cc), lambda i, s: (s[1], i, 0)),
                      pl.BlockSpec((ns, tr, Cc), lambda i, s: (0, i, 0))],
            out_specs=pl.BlockSpec((tr, Cc), lambda i, s: (s[0] * nb + i, 0))),
        out_shape=jax.ShapeDtypeStruct((2 * rh, Cc), F32),
        compiler_params=_params("parallel"), name=name)(idx, part, got)


def _share_halves(blocks, name):
    n = len(blocks)

    def body(*refs):
        ins, outs = refs[:n], refs[n:2 * n]
        send, recv = refs[2 * n:]
        x, y, c = _here()
        cps = []
        for k in range(n):
            rh = outs[k].shape[0] // 2
            mine = outs[k].at[pl.ds(c * rh, rh), :]
            cp = pltpu.make_async_remote_copy(
                src_ref=mine, dst_ref=mine, send_sem=send.at[k], recv_sem=recv.at[k],
                device_id=(x, y, 1 - c), device_id_type=MESH)
            cp.start()
            cps.append(cp)
        for cp in cps:
            cp.wait()

    return pl.pallas_call(
        body, in_specs=[ANY] * n, out_specs=[ANY] * n,
        out_shape=[jax.ShapeDtypeStruct(b.shape, b.dtype) for b in blocks],
        input_output_aliases={k: k for k in range(n)},
        scratch_shapes=[pltpu.SemaphoreType.DMA((n,)), pltpu.SemaphoreType.DMA((n,))],
        name=name)(*blocks)


def _small_copies(p_ref, slot_ref, send, recv):
    x, y, c = _here()
    mine = slot_ref.at[4 * x + 2 * y + c]
    cps = []
    for m in range(1, N_DEV):
        peer = (x ^ (m >> 2), y ^ ((m >> 1) & 1), c ^ (m & 1))
        cps.append(pltpu.make_async_remote_copy(
            src_ref=p_ref, dst_ref=mine, send_sem=send.at[m - 1], recv_sem=recv.at[m - 1],
            device_id=peer, device_id_type=MESH))
    return cps


def _small_start(packed):
    slots = lax.empty((N_DEV,) + packed.shape, packed.dtype)

    def body(p_ref, s_ref, send, recv, p_thru, s_thru, token):
        for cp in _small_copies(p_ref, s_ref, send, recv):
            cp.start()
        token[...] = jnp.zeros_like(token)

    return pl.pallas_call(
        body, name="small_start",
        in_specs=[HBM, HBM],
        out_specs=[SEM, SEM, HBM, HBM, pl.BlockSpec(memory_space=pltpu.VMEM)],
        out_shape=[pltpu.SemaphoreType.DMA((N_DEV - 1,)), pltpu.SemaphoreType.DMA((N_DEV - 1,)),
                   pltpu.HBM(packed.shape, packed.dtype), pltpu.HBM(slots.shape, slots.dtype),
                   jax.ShapeDtypeStruct((SUBLANES, LANES), F32)],
        input_output_aliases={0: 2, 1: 3},
        compiler_params=pltpu.CompilerParams(has_side_effects=EFFECT),
    )(_in_hbm(packed), _in_hbm(slots))


def _small_wait(send, recv, packed, slots, after):
    def body(p_ref, s_ref, send_r, recv_r, after_ref, p_out, s_out):
        for cp in _small_copies(p_ref, s_ref, send_r, recv_r):
            cp.wait_send()
            cp.wait_recv()

    return pl.pallas_call(
        body, name="small_wait",
        in_specs=[HBM, HBM, SEM, SEM, ANY], out_specs=[HBM, HBM],
        out_shape=[pltpu.HBM(packed.shape, packed.dtype), pltpu.HBM(slots.shape, slots.dtype)],
        input_output_aliases={0: 0, 1: 1},
        compiler_params=pltpu.CompilerParams(has_side_effects=EFFECT),
    )(packed, slots, send, recv, after)


def _sum_devices(packed, slots, me):
    n, R, _ = slots.shape
    tr = _tile(R, 1024)

    def body(m_ref, p_ref, s_ref, o_ref):
        own = p_ref[...]
        acc = None
        for d in range(n):
            term = jnp.where(m_ref[0] == d, own, s_ref[d])
            acc = term if acc is None else acc + term
        o_ref[...] = acc

    return pl.pallas_call(
        body,
        grid_spec=pltpu.PrefetchScalarGridSpec(
            num_scalar_prefetch=1, grid=(R // tr,),
            in_specs=[pl.BlockSpec((tr, LANES), lambda i, m: (i, 0)),
                      pl.BlockSpec((n, tr, LANES), lambda i, m: (0, i, 0))],
            out_specs=pl.BlockSpec((tr, LANES), lambda i, m: (i, 0))),
        out_shape=jax.ShapeDtypeStruct((R, LANES), F32),
        compiler_params=_params("parallel"), name="sum_devices")(me, packed, slots)


def _pack(arrs):
    rows, parts = [], []
    for a in arrs:
        flat = a.reshape(-1)
        r = -(-flat.shape[0] // (SUBLANES * LANES)) * SUBLANES
        parts.append(jnp.pad(flat, (0, r * LANES - flat.shape[0])).reshape(r, LANES))
        rows.append(r)
    return jnp.concatenate(parts, axis=0), rows


def _unpack(packed, rows, shapes):
    out, r0 = [], 0
    for r, shp in zip(rows, shapes):
        size = math.prod(shp)
        out.append(packed[r0:r0 + r].reshape(-1)[:size].reshape(shp))
        r0 += r
    return out


def _block_diag(w, per):
    H, dh, _ = w.shape
    w4 = w.reshape(H // per, per, dh, dh)
    eye = jnp.eye(per, dtype=w.dtype)
    return (w4[:, :, :, None, :] * eye[None, :, None, :, None]).reshape(H // per, per * dh, per * dh)


def _block_diag_take(d, per):
    n, s, _ = d.shape
    dh = s // per
    d5 = d.reshape(n, per, dh, per, dh)
    return jnp.stack([d5[:, h, :, h, :] for h in range(per)], axis=1).reshape(n * per, dh, dh)


def kernel(x, ffn1_norm, ffn1_w_gate, ffn1_w_up, ffn1_w_down, mix_norm, w_in, conv_dw, conv_dw_bias, conv_ln_g, conv_ln_b, lru_conv_w, lru_conv_b, lru_w_a, lru_b_a, lru_w_x, lru_b_x, lru_lambda, w_out, ffn2_norm, ffn2_w_gate, ffn2_w_up, ffn2_w_down, final_norm, loss_target, m_ffn1_norm, m_ffn1_w_gate, m_ffn1_w_up, m_ffn1_w_down, m_mix_norm, m_w_in, m_conv_dw, m_conv_dw_bias, m_conv_ln_g, m_conv_ln_b, m_lru_conv_w, m_lru_conv_b, m_lru_w_a, m_lru_b_a, m_lru_w_x, m_lru_b_x, m_lru_lambda, m_w_out, m_ffn2_norm, m_ffn2_w_gate, m_ffn2_w_up, m_ffn2_w_down, m_final_norm, v_ffn1_norm, v_ffn1_w_gate, v_ffn1_w_up, v_ffn1_w_down, v_mix_norm, v_w_in, v_conv_dw, v_conv_dw_bias, v_conv_ln_g, v_conv_ln_b, v_lru_conv_w, v_lru_conv_b, v_lru_w_a, v_lru_b_a, v_lru_w_x, v_lru_b_x, v_lru_lambda, v_w_out, v_ffn2_norm, v_ffn2_w_gate, v_ffn2_w_up, v_ffn2_w_down, v_final_norm):
    names = ['ffn1_norm', 'ffn1_w_gate', 'ffn1_w_up', 'ffn1_w_down', 'mix_norm', 'w_in', 'conv_dw', 'conv_dw_bias',
             'conv_ln_g', 'conv_ln_b', 'lru_conv_w', 'lru_conv_b', 'lru_w_a', 'lru_b_a', 'lru_w_x', 'lru_b_x',
             'lru_lambda', 'w_out', 'ffn2_norm', 'ffn2_w_gate', 'ffn2_w_up', 'ffn2_w_down', 'final_norm']
    env = dict(locals())
    W = {n: env[n] for n in names}
    M = {n: env['m_' + n] for n in names}
    V = {n: env['v_' + n] for n in names}

    xi, yi, ci = _here()
    chip = 2 * xi + yi
    cidx = ci.astype(jnp.int32).reshape(1)
    T, D = x.shape[-2], x.shape[-1]
    xs = x.reshape(T, D)
    tgt = loss_target.reshape(T, D)
    K, Cs = conv_dw.shape
    C = conv_dw_bias.shape[0]
    Wl = lru_conv_b.shape[0]
    K4 = lru_conv_w.shape[0]
    heads, dh, _ = lru_w_a.shape
    per = LANES // dh

    def row(v):
        return v.reshape(1, -1)

    tform = ('ffn1_w_gate', 'ffn1_w_up', 'ffn2_w_gate', 'ffn2_w_up')
    for n in tform:
        W[n], M[n], V[n] = W[n].T, M[n].T, V[n].T
    kp = -(-K // SUBLANES) * SUBLANES
    taps = jnp.concatenate([conv_dw, jnp.zeros((kp - K, Cs), F32), lru_conv_w,
                            jnp.zeros((2 * SUBLANES - K4, Cs), F32)], axis=0)
    idx = jnp.stack([ci, chip]).astype(jnp.int32)
    (wff1,) = _gather_weights([_place_cast([W['ffn1_w_gate'], W['ffn1_w_up'], ffn1_w_down], idx, BF16, "place_ffn1")])
    mixl = [_place_cast([w_in], idx, BF16, "place_w_in"), _place_cast([w_out], idx, BF16, "place_w_out"),
            _place_cast([taps], idx, F32, "place_taps")]
    msend, mrecv, mixl, mtok = _gather_start(mixl, wff1, "gather_mix_start")
    ff2l = _place_cast([W['ffn2_w_gate'], W['ffn2_w_up'], ffn2_w_down], idx, BF16, "place_ffn2")
    fsend, frecv, ff2l, ftok = _gather_start([ff2l], mtok, "gather_ffn2_start")
    wa_bd = _block_diag(lru_w_a, per).astype(BF16)
    wx_bd = _block_diag(lru_w_x, per).astype(BF16)

    x1, a1, b1 = _ffn_fwd(xs, row(ffn1_norm) + ftok[0:1, 0:1], wff1, "ffn1_fwd")
    win, wout, taps = _gather_wait(msend, mrecv, mixl, x1, "gather_mix_wait")
    win, wout, taps = win[0], wout.reshape(-1, D), taps[0]
    conv_w_full = taps[:, :K].transpose(1, 0, 2).reshape(K, N_CHIPS * Cs)
    lru_w4_full = taps[:, kp:kp + K4].transpose(1, 0, 2).reshape(K4, N_CHIPS * Cs)
    z = _mix_in_fwd(x1, row(mix_norm), win)
    u, u1 = _conv_fwd(z, conv_w_full, row(conv_dw_bias), row(conv_ln_g), row(conv_ln_b))
    yr, hs = _lru_fwd(z, 2 * C, lru_w4_full, row(lru_conv_b), wa_bd, row(lru_b_a), wx_bd, row(lru_b_x),
                      row(lru_lambda))
    x2 = _mix_out_fwd(x1, u, yr, wout)
    (wff2,) = _gather_wait(fsend, frecv, ff2l, x2, "gather_ffn2_wait")
    dx3, a2, b2, loss_blk, d_final = _ffn_fwd(x2, row(ffn2_norm), wff2, "ffn2_fwd", head=(row(final_norm), tgt))

    dx2, da2, db2, p2, hb2, dyh2, d_ffn2n = _ffn_bwd_tok(dx3, x2, row(ffn2_norm), a2, b2, wff2, "ffn2_bwd")
    dwg2, dwu2, dwd2 = _ffn_wgrad([([da2, db2], hb2), ([p2], dyh2)], ftok, "ffn2_wgrad")
    wsend, wrecv, f2g, f2o, wtok = _swap_start([dwg2, dwu2, dwd2], "swap_ffn2_start")
    dcat, dwout = _mix_out_bwd(dx2, u, yr, wout)
    dzc, cst = _conv_bwd(dcat, u1, z, conv_w_full, row(conv_ln_g) + wtok[0:1, 0:1], row(conv_ln_b))
    dzx, dzg, lst, dwa_bd, dwx_bd = _lru_bwd(dcat, C, hs, z, 2 * C, lru_w4_full, row(lru_conv_b), wa_bd,
                                              row(lru_b_a), wx_bd, row(lru_b_x), row(lru_lambda))
    dx1, dwin, d_mixn = _mix_in_bwd(dzc, dzx, dzg, x1, dx2, row(mix_norm), win)

    early_names = ['w_in', 'w_out', 'ffn2_w_gate', 'ffn2_w_up', 'ffn2_w_down']
    mixg = [dwin, dwout.reshape(N_CHIPS, -1, D)]
    mixo = _swap_halves_out(mixg, "swap_halves_mix")
    f2g, f2o = _swap_wait(wsend, wrecv, f2g, f2o, dwin, "swap_ffn2_wait")
    e_parts = [_add_cast(g, o, cidx, "add_cast_" + n)
               for g, o, n in zip(mixg + list(f2g), list(mixo) + list(f2o), early_names)]
    esend, erecv, e_parts, e_lands, etok = _exchange_start(e_parts, "exchange_early_start")

    dx0, da1, db1, p1, hb1, dyh1, d_ffn1n = _ffn_bwd_tok(dx1, xs, row(ffn1_norm) + etok[0:1, 0:1], a1, b1, wff1,
                                                         "ffn1_bwd")

    small_names = ['ffn1_norm', 'mix_norm', 'conv_dw', 'conv_dw_bias', 'conv_ln_g', 'conv_ln_b', 'lru_conv_w',
                   'lru_conv_b', 'lru_w_a', 'lru_b_a', 'lru_w_x', 'lru_b_x', 'lru_lambda', 'ffn2_norm',
                   'final_norm']
    small = {
        'ffn1_norm': d_ffn1n, 'mix_norm': d_mixn, 'conv_dw': cst[:K], 'conv_dw_bias': cst[K + 1],
        'conv_ln_g': cst[K + 2], 'conv_ln_b': cst[K + 3], 'lru_conv_w': lst[:K4], 'lru_conv_b': lst[K4],
        'lru_w_a': _block_diag_take(dwa_bd, per), 'lru_b_a': lst[K4 + 1],
        'lru_w_x': _block_diag_take(dwx_bd, per), 'lru_b_x': lst[K4 + 2], 'lru_lambda': lst[K4 + 3],
        'ffn2_norm': d_ffn2n, 'final_norm': d_final,
    }
    packed, rows = _pack([small[n] for n in small_names] + [loss_blk[0:1, 0:1]])
    ssend, srecv, packed, sslots, stok = _small_start(packed)

    gu_names, d_names = ['ffn1_w_gate', 'ffn1_w_up'], ['ffn1_w_down']
    gu = _ffn_wgrad([([da1, db1], hb1)], stok, "ffn1_wgrad_gu", swap=True)
    gu_parts = [_add_cast(g, o, cidx, "add_cast_" + n) for g, o, n in zip(gu[:2], gu[2:], gu_names)]
    gsend, grecv, gu_parts, gu_lands, gtok = _exchange_start(gu_parts, "exchange_gu_start")
    dn = _ffn_wgrad([([p1], dyh1)], gtok, "ffn1_wgrad_d", swap=True)
    dwd1 = dn[0]
    d_parts = [_add_cast(g, o, cidx, "add_cast_" + n) for g, o, n in zip(dn[:1], dn[1:], d_names)]
    dsend, drecv, d_parts, d_lands, ltok = _exchange_start(d_parts, "exchange_d_start")
    e_parts, e_slots = _exchange_wait(esend, erecv, e_parts, e_lands, ltok, "exchange_early_wait")
    delta, new_m, new_v = {}, {}, {}

    def finish(group, parts, slots, tag):
        halves = [_sum_slots(p, b, idx, "sum_slots_" + n) for p, b, n in zip(parts, slots, group)]
        for n, g in zip(group, _share_halves(halves, "share_halves_" + tag)):
            G[n] = g
            delta[n], new_m[n], new_v[n] = _adamw(W[n], g, M[n], V[n], "adamw_" + n)

    G = {}
    finish(early_names, e_parts, e_slots, "early")

    full_shapes = [(K, C) if n == 'conv_dw' else (K4, Wl) if n == 'lru_conv_w' else W[n].shape for n in small_names]
    packed, sslots = _small_wait(ssend, srecv, packed, sslots, dwd1)
    summed = _sum_devices(packed, sslots, (4 * xi + 2 * yi + ci).astype(jnp.int32).reshape(1))
    *small_sums, loss_sum = _unpack(summed, rows, full_shapes + [(1, 1)])
    for n, gsum in zip(small_names, small_sums):
        if n == 'conv_dw':
            gsum = lax.dynamic_slice_in_dim(gsum, chip * Cs, Cs, axis=1)
        elif n == 'lru_conv_w':
            gsum = lax.dynamic_slice_in_dim(gsum, chip * lru_conv_w.shape[1], lru_conv_w.shape[1], axis=1)
        G[n] = gsum

    pw, prow = _pack([W[n] for n in small_names])
    pg, _ = _pack([G[n] for n in small_names])
    pm, _ = _pack([M[n] for n in small_names])
    pv, _ = _pack([V[n] for n in small_names])
    sd, sm, sv = _adamw(pw, pg, pm, pv, "adamw_small")
    shapes = [W[n].shape for n in small_names]
    for n, a, b, c_ in zip(small_names, _unpack(sd, prow, shapes), _unpack(sm, prow, shapes),
                           _unpack(sv, prow, shapes)):
        delta[n], new_m[n], new_v[n] = a, b, c_

    done = sd[0:SUBLANES] + delta[early_names[-1]][0:SUBLANES, 0:LANES]
    gu_parts, gu_slots = _exchange_wait(gsend, grecv, gu_parts, gu_lands, done, "exchange_gu_wait")
    d_parts, d_slots = _exchange_wait(dsend, drecv, d_parts, d_lands, gu_slots[0], "exchange_d_wait")
    finish(gu_names + d_names, list(gu_parts) + list(d_parts), list(gu_slots) + list(d_slots), "last")

    loss = loss_sum[0, 0]
    grad_x = dx0.reshape(x.shape)
    for n in tform:
        G[n], delta[n], new_m[n], new_v[n] = G[n].T, delta[n].T, new_m[n].T, new_v[n].T
    return (loss, grad_x, *[G[n] for n in names], *[delta[n] for n in names],
            *[new_m[n] for n in names], *[new_v[n] for n in names])
```

```python
import functools
import math

import jax
import jax.numpy as jnp
from jax import lax
from jax.experimental import pallas as pl
from jax.experimental.pallas import tpu as pltpu

F32 = jnp.float32
BF16 = jnp.bfloat16
MESH = pl.DeviceIdType.MESH

RMS_EPS = 1e-6
LN_EPS = 1e-5
LRU_C = 8.0
FFN_RES_SCALE = 0.5
ADAM_LR = 0.001
ADAM_B1 = 0.9
ADAM_B2 = 0.999
ADAM_EPS = 1e-08
ADAM_WD = 0.01
ADAM_STEP = 10

LANES = 128
SUBLANES = 8
CONV_HALO = 32
LRU_HALO = 8
ROW_CHUNK = 64
VMEM_LIMIT = 56 * 1024 * 1024
N_CHIPS = 4
N_DEV = 8
TOK_TILE = 1024
BWD_TILE = 512
FFN_BWD_TILE = 512
BWD_ROWS = 32
FFN_BWD_CHAIN = 256
CONV_TILE = 512
LRU_TILE = 2048
LRU_GROUPS = 8


def _dot(a, b):
    return jnp.dot(a, b, preferred_element_type=F32)


def _dot_nt(a, b):
    return lax.dot_general(a, b, (((1,), (1,)), ((), ())), preferred_element_type=F32)


def _dot_tn(a, b):
    return lax.dot_general(a, b, (((0,), (0,)), ((), ())), preferred_element_type=F32)


def _tile(n, pref, mult=SUBLANES):
    for t in range(min(pref, n), 0, -1):
        if n % t == 0 and t % mult == 0:
            return t
    return n


def _params(*sem):
    return pltpu.CompilerParams(dimension_semantics=sem, vmem_limit_bytes=VMEM_LIMIT)


def _rms_stats(x):
    r = lax.rsqrt(jnp.mean(x * x, axis=-1, keepdims=True) + RMS_EPS)
    return x * r, r


def _rms_bwd(dh, xh, r, g):
    dxh = dh * g
    return r * (dxh - xh * jnp.mean(dxh * xh, axis=-1, keepdims=True))


def _colsum(v):
    return jnp.sum(v, axis=0, keepdims=True)


def _ffn_fwd(x, g, wff, name, head=None):
    T, D = x.shape
    ns, fs = wff.shape[1], wff.shape[2]
    tm = _tile(T, TOK_TILE)
    mc = _tile(tm, FFN_BWD_CHAIN, 16)
    rc = _tile(tm, FFN_BWD_CHAIN)

    def body(*refs):
        x_ref, g_ref, wg_ref, wu_ref, wd_ref = refs[:5]
        if head is None:
            y_ref, a_ref, b_ref, hb_ref, acc_ref = refs[5:]
        else:
            gf_ref, t_ref, y_ref, a_ref, b_ref, loss_ref, dgf_ref, hb_ref, acc_ref = refs[5:]
        j = pl.program_id(1)

        @pl.when(j == 0)
        def _():
            xh, _ = _rms_stats(x_ref[...])
            hb_ref[...] = (xh * g_ref[...]).astype(BF16)
            acc_ref[...] = jnp.zeros_like(acc_ref)

        if head is not None:
            @pl.when((pl.program_id(0) == 0) & (j == 0))
            def _():
                loss_ref[...] = jnp.zeros_like(loss_ref)
                dgf_ref[...] = jnp.zeros_like(dgf_ref)

        for q0 in range(0, tm, mc):
            blk = pl.ds(q0, mc)
            hb = hb_ref[blk, :]
            a = _dot_nt(hb, wg_ref[...])
            b = _dot_nt(hb, wu_ref[...])
            a_ref[blk, :] = a.astype(BF16)
            b_ref[blk, :] = b.astype(BF16)
            p = (a * jax.nn.sigmoid(a) * b).astype(BF16)
            acc_ref[blk, :] += _dot(p, wd_ref[...])

        @pl.when(j == ns - 1)
        def _():
            if head is None:
                y_ref[...] = x_ref[...] + FFN_RES_SCALE * acc_ref[...]
                return
            gv = gf_ref[...]
            loss = jnp.zeros((), F32)
            dg = jnp.zeros((1, D), F32)
            for r0 in range(0, tm, rc):
                rows = pl.ds(r0, rc)
                xh, r = _rms_stats(x_ref[rows, :] + FFN_RES_SCALE * acc_ref[rows, :])
                e = xh * gv - t_ref[rows, :]
                loss = loss + 0.5 * jnp.sum(jnp.mean(e * e, axis=-1, keepdims=True))
                dy = e * (1.0 / D)
                dg = dg + _colsum(dy * xh)
                y_ref[rows, :] = _rms_bwd(dy, xh, r, gv)
            loss_ref[...] += loss
            dgf_ref[...] += dg

    def wspec(n):
        return pl.BlockSpec((None, None, fs, D), lambda i, j: (n, j, 0, 0))

    tok = pl.BlockSpec((tm, D), lambda i, j: (i, 0))
    vec = pl.BlockSpec((1, D), lambda i, j: (0, 0))
    mid = pl.BlockSpec((None, tm, fs), lambda i, j: (j, i, 0))
    in_specs = [tok, vec, wspec(0), wspec(1), wspec(2)]
    out_specs = [tok, mid, mid]
    out_shape = [jax.ShapeDtypeStruct((T, D), F32), jax.ShapeDtypeStruct((ns, T, fs), BF16),
                 jax.ShapeDtypeStruct((ns, T, fs), BF16)]
    args = [x, g, wff, wff, wff]
    if head is not None:
        in_specs += [vec, tok]
        out_specs += [pl.BlockSpec((SUBLANES, LANES), lambda i, j: (0, 0)), vec]
        out_shape += [jax.ShapeDtypeStruct((SUBLANES, LANES), F32), jax.ShapeDtypeStruct((1, D), F32)]
        args += list(head)
    return pl.pallas_call(
        body, grid=(T // tm, ns), in_specs=in_specs, out_specs=out_specs, out_shape=out_shape,
        scratch_shapes=[pltpu.VMEM((tm, D), BF16), pltpu.VMEM((tm, D), F32)],
        compiler_params=_params("arbitrary", "arbitrary"), name=name)(*args)


def _ffn_bwd_tok(dy, x, g, a, b, wff, name):
    T, D = x.shape
    ns, fs = wff.shape[1], wff.shape[2]
    tm = _tile(T, FFN_BWD_TILE)
    rc = _tile(tm, BWD_ROWS)
    mc = _tile(tm, FFN_BWD_CHAIN, rc)

    def body(dy_ref, x_ref, g_ref, a_ref, b_ref, w_ref,
             dx_ref, da_ref, db_ref, p_ref, hb_ref, dyh_ref, dg_ref, dh_ref, dp_ref):
        i, j = pl.program_id(0), pl.program_id(1)
        cur = dp_ref.at[j % 2]
        nxt = dp_ref.at[(j + 1) % 2]
        wg_ref, wu_ref = w_ref.at[0, j], w_ref.at[1, j]
        wd0_ref, wdn_ref = w_ref.at[2, 0], w_ref.at[2, jnp.minimum(j + 1, ns - 1)]

        @pl.when((i == 0) & (j == 0))
        def _():
            dg_ref[...] = jnp.zeros_like(dg_ref)

        @pl.when(j == 0)
        def _():
            for r0 in range(0, tm, rc):
                rows = pl.ds(r0, rc)
                xh, _ = _rms_stats(x_ref[rows, :])
                hb_ref[rows, :] = (xh * g_ref[...]).astype(BF16)
                dyh_ref[rows, :] = (FFN_RES_SCALE * dy_ref[rows, :]).astype(BF16)
            dh_ref[...] = jnp.zeros_like(dh_ref)
            cur[...] = _dot_nt(dyh_ref[...], wd0_ref[...])

        def chains(with_next):
            for q0 in range(0, tm, mc):
                blk = pl.ds(q0, mc)
                for r0 in range(q0, q0 + mc, rc):
                    rows = pl.ds(r0, rc)
                    av = a_ref[rows, :].astype(F32)
                    bv = b_ref[rows, :].astype(F32)
                    dp = cur[rows, :]
                    s = jax.nn.sigmoid(av)
                    sl = av * s
                    da_ref[rows, :] = (dp * bv * (s * (1.0 + av * (1.0 - s)))).astype(BF16)
                    db_ref[rows, :] = (dp * sl).astype(BF16)
                    p_ref[rows, :] = (sl * bv).astype(BF16)
                if with_next:
                    nxt[blk, :] = _dot_nt(dyh_ref[blk, :], wdn_ref[...])
                dh_ref[blk, :] += _dot(da_ref[blk, :], wg_ref[...]) + _dot(db_ref[blk, :], wu_ref[...])

        pl.when(j < ns - 1)(functools.partial(chains, True))
        pl.when(j == ns - 1)(functools.partial(chains, False))

        @pl.when(j == ns - 1)
        def _():
            gv = g_ref[...]
            dg = jnp.zeros((1, D), F32)
            for r0 in range(0, tm, rc):
                rows = pl.ds(r0, rc)
                xh, r = _rms_stats(x_ref[rows, :])
                dh = dh_ref[rows, :]
                dx_ref[rows, :] = dy_ref[rows, :] + _rms_bwd(dh, xh, r, gv)
                dg = dg + _colsum(dh * xh)
            dg_ref[...] += dg

    tok = pl.BlockSpec((tm, D), lambda i, j: (i, 0))
    mid = pl.BlockSpec((None, tm, fs), lambda i, j: (j, i, 0))
    vec = pl.BlockSpec((1, D), lambda i, j: (0, 0))
    return pl.pallas_call(
        body, grid=(T // tm, ns),
        in_specs=[tok, tok, vec, mid, mid,
                  pl.BlockSpec(wff.shape, lambda i, j: (0, 0, 0, 0), pipeline_mode=pl.Buffered(1))],
        out_specs=[tok, mid, mid, mid, tok, tok, vec],
        out_shape=[jax.ShapeDtypeStruct((T, D), F32),
                   jax.ShapeDtypeStruct((ns, T, fs), BF16), jax.ShapeDtypeStruct((ns, T, fs), BF16),
                   jax.ShapeDtypeStruct((ns, T, fs), BF16),
                   jax.ShapeDtypeStruct((T, D), BF16), jax.ShapeDtypeStruct((T, D), BF16),
                   jax.ShapeDtypeStruct((1, D), F32)],
        scratch_shapes=[pltpu.VMEM((tm, D), F32), pltpu.VMEM((2, tm, fs), F32)],
        compiler_params=_params("arbitrary", "arbitrary"), name=name)(dy, x, g, a, b, wff)


def _ffn_wgrad(groups, after, name, swap=False):
    flat = [(l, gi) for gi, (ls, _) in enumerate(groups) for l in ls]
    ng, n = len(groups), len(flat)
    T, D = groups[0][1].shape
    ns, _, fs = flat[0][0].shape
    tm = _tile(T, TOK_TILE)
    nI = T // tm
    rh = fs // 2

    def body(*refs):
        rhs_refs, lhs_refs, out_refs = refs[:ng], refs[ng:ng + n], refs[ng + n + 1:ng + 2 * n + 1]
        j, i = pl.program_id(0), pl.program_id(1)

        @pl.when(i == 0)
        def _():
            for o in out_refs:
                o[...] = jnp.zeros_like(o)

        rvs = [r[...] for r in rhs_refs]
        for l, o, (_, gi) in zip(lhs_refs, out_refs, flat):
            o[...] += _dot_tn(l[...], rvs[gi])

        if swap:
            land_refs = refs[ng + 2 * n + 1:ng + 3 * n + 1]
            send, recv, stage = refs[ng + 3 * n + 1:]
            x, y, c = _here()

            def copies(jj):
                return [pltpu.make_async_remote_copy(
                    src_ref=stage.at[jj % 2, k], dst_ref=land_refs[k].at[jj],
                    send_sem=send.at[k * ns + jj], recv_sem=recv.at[k * ns + jj],
                    device_id=(x, y, 1 - c), device_id_type=MESH) for k in range(n)]

            @pl.when(i == nI - 1)
            def _():
                theirs = pl.ds(pl.multiple_of((1 - c) * rh, SUBLANES), rh)
                for k in range(n):
                    stage[j % 2, k] = out_refs[k][theirs, :]
                for cp in copies(j):
                    cp.start()

            @pl.when((i == nI - 1) & (j > 0))
            def _():
                for cp in copies(j - 1):
                    cp.wait_send()

            @pl.when((i == nI - 1) & (j == ns - 1))
            def _():
                for cp in copies(j):
                    cp.wait_send()
                for jj in range(ns):
                    for cp in copies(jj):
                        cp.wait_recv()

    tok = pl.BlockSpec((tm, D), lambda j, i: (i, 0))
    mid = pl.BlockSpec((None, tm, fs), lambda j, i: (j, i, 0))
    wsp = pl.BlockSpec((None, fs, D), lambda j, i: (j, 0, 0))
    sds = jax.ShapeDtypeStruct((ns, fs, D), F32)
    out_specs, out_shape, scratch = [wsp] * n, [sds] * n, []
    if swap:
        out_specs += [ANY] * n
        out_shape += [jax.ShapeDtypeStruct((ns, rh, D), F32)] * n
        scratch = [pltpu.SemaphoreType.DMA((n * ns,)), pltpu.SemaphoreType.DMA((n * ns,)),
                   pltpu.VMEM((2, n, rh, D), F32)]
    return pl.pallas_call(
        body, grid=(ns, nI),
        in_specs=[tok] * ng + [mid] * n + [pl.BlockSpec((SUBLANES, LANES), lambda j, i: (0, 0))],
        out_specs=out_specs, out_shape=out_shape, scratch_shapes=scratch,
        compiler_params=_params("arbitrary", "arbitrary"), name=name)(
            *[r for _, r in groups], *[l for l, _ in flat], after)


def _mix_in_fwd(x, g, win):
    T, D = x.shape
    ns, ws = win.shape[0], win.shape[2]
    tm = _tile(T, TOK_TILE)

    def body(x_ref, g_ref, w_ref, z_ref):
        xh, _ = _rms_stats(x_ref[...])
        hb = (xh * g_ref[...]).astype(BF16)
        for j in range(ns):
            z_ref[:, pl.ds(j * ws, ws)] = _dot(hb, w_ref[j])

    return pl.pallas_call(
        body, grid=(T // tm,),
        in_specs=[pl.BlockSpec((tm, D), lambda i: (i, 0)), pl.BlockSpec((1, D), lambda i: (0, 0)),
                  pl.BlockSpec((ns, D, ws), lambda i: (0, 0, 0), pipeline_mode=pl.Buffered(1))],
        out_specs=pl.BlockSpec((tm, ns * ws), lambda i: (i, 0)),
        out_shape=jax.ShapeDtypeStruct((T, ns * ws), F32),
        compiler_params=_params("parallel"), name="mix_in_fwd")(x, g, win)


def _tap_sum(buf, w_ref, ntaps, first_row, r0, rows, flip):
    acc = None
    for k in range(ntaps):
        off = (ntaps - 1 - k) if flip else k
        t = buf[pl.ds(first_row + r0 + off, rows), :] * w_ref[pl.ds(k, 1), :]
        acc = t if acc is None else acc + t
    return acc


def _shift_copies(buf, sh, rows):
    for r in range(1, SUBLANES):
        sh[r - 1, pl.ds(0, rows), :] = buf[pl.ds(r, rows), :]


def _tap_rows(buf, sh, off, r0, rows):
    r = off % SUBLANES
    if r == 0:
        return buf[pl.ds(off + r0, rows), :]
    return sh[r - 1, pl.ds(off - r + r0, rows), :]


def _tap_sum_tiles(buf, sh, w_ref, ntaps, first_row, r0, rows, flip):
    acc = None
    for k in range(ntaps):
        off = first_row + ((ntaps - 1 - k) if flip else k)
        t = _tap_rows(buf, sh, off, r0, rows) * w_ref[pl.ds(k, 1), :]
        acc = t if acc is None else acc + t
    return acc


def _conv_fwd(z, w, bias, lng, lnb):
    T = z.shape[0]
    K, C = w.shape
    tm = _tile(T, CONV_TILE, ROW_CHUNK)
    rc = min(ROW_CHUNK, tm)
    srows = tm + CONV_HALO - SUBLANES

    def body(cv_ref, cg_ref, w_ref, b_ref, g_ref, bb_ref, u_ref, u1_ref, buf, sh):
        @pl.when(pl.program_id(0) == 0)
        def _():
            buf[pl.ds(0, CONV_HALO), :] = jnp.zeros((CONV_HALO, C), F32)

        buf[pl.ds(CONV_HALO, tm), :] = cv_ref[...] * jax.nn.sigmoid(cg_ref[...])
        _shift_copies(buf, sh, srows)
        for r0 in range(0, tm, rc):
            u1 = _tap_sum_tiles(buf, sh, w_ref, K, CONV_HALO - (K - 1), r0, rc, False) + b_ref[...]
            u1_ref[pl.ds(r0, rc), :] = u1
            xc = u1 - jnp.mean(u1, axis=-1, keepdims=True)
            xh = xc * lax.rsqrt(jnp.mean(xc * xc, axis=-1, keepdims=True) + LN_EPS)
            u2 = xh * g_ref[...] + bb_ref[...]
            u_ref[pl.ds(r0, rc), :] = (u2 * jax.nn.sigmoid(u2)).astype(BF16)
        buf[pl.ds(0, CONV_HALO), :] = buf[pl.ds(tm, CONV_HALO), :]

    vec = pl.BlockSpec((1, C), lambda i: (0, 0))
    return pl.pallas_call(
        body, grid=(T // tm,),
        in_specs=[pl.BlockSpec((tm, C), lambda i: (i, 0)), pl.BlockSpec((tm, C), lambda i: (i, 1)),
                  pl.BlockSpec((K, C), lambda i: (0, 0)), vec, vec, vec],
        out_specs=[pl.BlockSpec((tm, C), lambda i: (i, 0)), pl.BlockSpec((tm, C), lambda i: (i, 0))],
        out_shape=[jax.ShapeDtypeStruct((T, C), BF16), jax.ShapeDtypeStruct((T, C), F32)],
        scratch_shapes=[pltpu.VMEM((CONV_HALO + tm, C), F32), pltpu.VMEM((SUBLANES - 1, srows, C), F32)],
        compiler_params=_params("arbitrary"), name="conv_fwd")(z, z, w, bias, lng, lnb)


def _conv_bwd(dcat, u1, z, w, lng, lnb):
    T = z.shape[0]
    K, C = w.shape
    tm = _tile(T, CONV_TILE, ROW_CHUNK)
    rc = min(ROW_CHUNK, tm)
    nI = T // tm
    hb = tm // CONV_HALO
    srows = ((K + 4 + SUBLANES - 1) // SUBLANES) * SUBLANES
    shrows = tm + CONV_HALO - SUBLANES

    def body(du_ref, u1_ref, cv_ref, cg_ref, cvp_ref, cgp_ref, w_ref, g_ref, bb_ref,
             dz_ref, st_ref, u0buf, d1buf, ush, dsh):
        i = pl.program_id(0)
        ti = nI - 1 - i

        @pl.when(i == 0)
        def _():
            st_ref[...] = jnp.zeros_like(st_ref)
            d1buf[pl.ds(tm, CONV_HALO), :] = jnp.zeros((CONV_HALO, C), F32)

        prev = cvp_ref[...] * jax.nn.sigmoid(cgp_ref[...])
        u0buf[pl.ds(0, CONV_HALO), :] = jnp.where(ti == 0, 0.0, prev)
        u0buf[pl.ds(CONV_HALO, tm), :] = cv_ref[...] * jax.nn.sigmoid(cg_ref[...])

        gv = g_ref[...]
        dbias = jnp.zeros((1, C), F32)
        dgain = jnp.zeros((1, C), F32)
        dlnb = jnp.zeros((1, C), F32)
        for r0 in range(0, tm, rc):
            u1 = u1_ref[pl.ds(r0, rc), :]
            xc = u1 - jnp.mean(u1, axis=-1, keepdims=True)
            rstd = lax.rsqrt(jnp.mean(xc * xc, axis=-1, keepdims=True) + LN_EPS)
            xh = xc * rstd
            u2 = xh * gv + bb_ref[...]
            s = jax.nn.sigmoid(u2)
            du2 = du_ref[pl.ds(r0, rc), :] * (s * (1.0 + u2 * (1.0 - s)))
            dgain = dgain + _colsum(du2 * xh)
            dlnb = dlnb + _colsum(du2)
            dxh = du2 * gv
            du1 = rstd * (dxh - jnp.mean(dxh, axis=-1, keepdims=True)
                          - xh * jnp.mean(dxh * xh, axis=-1, keepdims=True))
            dbias = dbias + _colsum(du1)
            d1buf[pl.ds(r0, rc), :] = du1
        st_ref[pl.ds(K + 1, 1), :] += dbias
        st_ref[pl.ds(K + 2, 1), :] += dgain
        st_ref[pl.ds(K + 3, 1), :] += dlnb

        _shift_copies(u0buf, ush, shrows)
        _shift_copies(d1buf, dsh, shrows)
        for k in range(K):
            acc = jnp.zeros((SUBLANES, C), F32)
            for r0 in range(0, tm, rc):
                prod = d1buf[pl.ds(r0, rc), :] * _tap_rows(u0buf, ush, CONV_HALO - (K - 1) + k, r0, rc)
                acc = acc + jnp.sum(prod.reshape(rc // SUBLANES, SUBLANES, C), axis=0)
            st_ref[pl.ds(k, 1), :] += _colsum(acc)

        for r0 in range(0, tm, rc):
            du0 = _tap_sum_tiles(d1buf, dsh, w_ref, K, 0, r0, rc, True)
            cv = cv_ref[pl.ds(r0, rc), :]
            sg = jax.nn.sigmoid(cg_ref[pl.ds(r0, rc), :])
            dz_ref[pl.ds(r0, rc), pl.ds(0, C)] = (du0 * sg).astype(BF16)
            dz_ref[pl.ds(r0, rc), pl.ds(C, C)] = (du0 * cv * sg * (1.0 - sg)).astype(BF16)
        d1buf[pl.ds(tm, CONV_HALO), :] = d1buf[pl.ds(0, CONV_HALO), :]

    def rev(col):
        return lambda i: (nI - 1 - i, col)

    def rev_prev(col):
        return lambda i: (jnp.maximum((nI - 1 - i) * hb - 1, 0), col)

    vec = pl.BlockSpec((1, C), lambda i: (0, 0))
    return pl.pallas_call(
        body, grid=(nI,),
        in_specs=[pl.BlockSpec((tm, C), rev(0)), pl.BlockSpec((tm, C), rev(0)),
                  pl.BlockSpec((tm, C), rev(0)), pl.BlockSpec((tm, C), rev(1)),
                  pl.BlockSpec((CONV_HALO, C), rev_prev(0)), pl.BlockSpec((CONV_HALO, C), rev_prev(1)),
                  pl.BlockSpec((K, C), lambda i: (0, 0)), vec, vec],
        out_specs=[pl.BlockSpec((tm, 2 * C), rev(0)), pl.BlockSpec((srows, C), lambda i: (0, 0))],
        out_shape=[jax.ShapeDtypeStruct((T, 2 * C), BF16), jax.ShapeDtypeStruct((srows, C), F32)],
        scratch_shapes=[pltpu.VMEM((CONV_HALO + tm, C), F32), pltpu.VMEM((tm + CONV_HALO, C), F32),
                        pltpu.VMEM((SUBLANES - 1, shrows, C), F32), pltpu.VMEM((SUBLANES - 1, shrows, C), F32)],
        compiler_params=_params("arbitrary"), name="conv_bwd")(dcat, u1, z, z, z, z, w, lng, lnb)


def _softplus(v):
    return jnp.maximum(v, 0.0) + jnp.log(1.0 + jnp.exp(-jnp.abs(v)))


def _gelu(v):
    c = math.sqrt(2.0 / math.pi)
    t = jnp.tanh(c * (v + 0.044715 * v * v * v))
    gl = 0.5 * v * (1.0 + t)
    dgl = 0.5 * (1.0 + t) + 0.5 * v * (1.0 - t * t) * c * (1.0 + 3.0 * 0.044715 * v * v)
    return gl, dgl


def _lru_gates(xr, wa, ba, wx, bx, lam):
    xb = xr.astype(BF16)
    r = jax.nn.sigmoid(_dot(xb, wa) + ba)
    ig = jax.nn.sigmoid(_dot(xb, wx) + bx)
    sp = _softplus(-lam)
    log_a = -LRU_C * r * sp
    a = jnp.exp(log_a)
    y = 2.0 * log_a
    series = -(y * (1.0 + y * (0.5 + y * (1.0 / 6.0 + y * (1.0 / 24.0)))))
    mult = jnp.sqrt(jnp.where(y > -0.02, series, 1.0 - jnp.exp(y)))
    return a, mult, r, ig, sp


def _scan_tile(a_s, b_s, h_s, p_s, carry, seg, reverse):
    hl = [jnp.zeros((SUBLANES, LANES), F32)] * LRU_GROUPS
    pr = [jnp.ones((SUBLANES, LANES), F32)] * LRU_GROUPS
    for n in range(seg):
        for g in range(LRU_GROUPS):
            rows = pl.ds(g * SUBLANES * seg + ((seg - 1 - n) if reverse else n), SUBLANES, stride=seg)
            av = a_s[rows, :]
            hl[g] = av * hl[g] + b_s[rows, :]
            pr[g] = av * pr[g]
            h_s[rows, :] = hl[g]
            p_s[rows, :] = pr[g]
    nseg = SUBLANES * LRU_GROUPS
    cs = [None] * nseg
    c = carry
    for s in (range(nseg - 1, -1, -1) if reverse else range(nseg)):
        g, r = divmod(s, SUBLANES)
        cs[s] = c
        c = hl[g][r:r + 1, :] + pr[g][r:r + 1, :] * c
    return cs, c


def _lru_fwd(z, col0, w4, b4, wa, ba, wx, bx, lam):
    T = z.shape[0]
    K4, W = w4.shape
    nC = W // LANES
    tm = _tile(T, LRU_TILE, SUBLANES * SUBLANES * LRU_GROUPS)
    seg = tm // (SUBLANES * LRU_GROUPS)
    cx, cg = col0 // LANES, (col0 + W) // LANES

    def body(rx_ref, rg_ref, w4_ref, b4_ref, wa_ref, ba_ref, wx_ref, bx_ref, lam_ref,
             yr_ref, hs_ref, xbuf, a_s, b_s, h_s, p_s, hc):
        @pl.when(pl.program_id(1) == 0)
        def _():
            xbuf[pl.ds(0, LRU_HALO), :] = jnp.zeros((LRU_HALO, LANES), F32)
            hc[...] = jnp.zeros_like(hc)

        xbuf[pl.ds(LRU_HALO, tm), :] = rx_ref[...]
        xr = _tap_sum(xbuf, w4_ref, K4, LRU_HALO - (K4 - 1), 0, tm, False) + b4_ref[...]
        a, mult, _, ig, _ = _lru_gates(xr, wa_ref[...], ba_ref[...], wx_ref[...], bx_ref[...], lam_ref[...])
        a_s[...] = a
        b_s[...] = mult * ig * xr
        cs, cout = _scan_tile(a_s, b_s, h_s, p_s, hc[pl.ds(0, 1), :], seg, False)
        hc[pl.ds(0, 1), :] = cout
        for s in range(SUBLANES * LRU_GROUPS):
            rows = pl.ds(s * seg, seg)
            h = h_s[rows, :] + p_s[rows, :] * cs[s]
            hs_ref[rows, :] = h
            gl, _ = _gelu(rg_ref[rows, :])
            yr_ref[rows, :] = (h * gl).astype(BF16)
        xbuf[pl.ds(0, LRU_HALO), :] = xbuf[pl.ds(tm, LRU_HALO), :]

    vec = pl.BlockSpec((1, LANES), lambda c, i: (0, c))
    mat = pl.BlockSpec((None, LANES, LANES), lambda c, i: (c, 0, 0))
    return pl.pallas_call(
        body, grid=(nC, T // tm),
        in_specs=[pl.BlockSpec((tm, LANES), lambda c, i: (i, cx + c)),
                  pl.BlockSpec((tm, LANES), lambda c, i: (i, cg + c)),
                  pl.BlockSpec((K4, LANES), lambda c, i: (0, c)), vec, mat, vec, mat, vec, vec],
        out_specs=[pl.BlockSpec((tm, LANES), lambda c, i: (i, c)), pl.BlockSpec((tm, LANES), lambda c, i: (i, c))],
        out_shape=[jax.ShapeDtypeStruct((T, W), BF16), jax.ShapeDtypeStruct((T, W), F32)],
        scratch_shapes=[pltpu.VMEM((LRU_HALO + tm, LANES), F32)] + [pltpu.VMEM((tm, LANES), F32)] * 4
        + [pltpu.VMEM((SUBLANES, LANES), F32)],
        compiler_params=_params("parallel", "arbitrary"), name="lru_fwd")(z, z, w4, b4, wa, ba, wx, bx, lam)


def _lru_bwd(dcat, dcol0, hs, z, col0, w4, b4, wa, ba, wx, bx, lam):
    T = z.shape[0]
    K4, W = w4.shape
    assert K4 + 4 == SUBLANES
    nC = W // LANES
    tm = _tile(T, LRU_TILE, SUBLANES * SUBLANES * LRU_GROUPS)
    seg = tm // (SUBLANES * LRU_GROUPS)
    nI = T // tm
    hb = tm // LRU_HALO
    cx, cg, cd = col0 // LANES, (col0 + W) // LANES, dcol0 // LANES

    def body(dyr_ref, hs_ref, hsp_ref, rx_ref, rxp_ref, rg_ref, w4_ref, b4_ref, wa_ref, ba_ref, wx_ref, bx_ref,
             lam_ref, dzx_ref, dzg_ref, st_ref, dwa_ref, dwx_ref, xbuf, hbuf, abuf, a_s, b_s, h_s, p_s, dbuf, gc, anc):
        i = pl.program_id(1)
        ti = nI - 1 - i

        @pl.when(i == 0)
        def _():
            st_ref[...] = jnp.zeros_like(st_ref)
            dwa_ref[...] = jnp.zeros_like(dwa_ref)
            dwx_ref[...] = jnp.zeros_like(dwx_ref)
            gc[...] = jnp.zeros_like(gc)
            anc[...] = jnp.zeros_like(anc)
            dbuf[pl.ds(tm, LRU_HALO), :] = jnp.zeros((LRU_HALO, LANES), F32)

        xbuf[pl.ds(0, LRU_HALO), :] = jnp.where(ti == 0, 0.0, rxp_ref[...])
        xbuf[pl.ds(LRU_HALO, tm), :] = rx_ref[...]
        hbuf[pl.ds(0, LRU_HALO), :] = jnp.where(ti == 0, 0.0, hsp_ref[...])
        hbuf[pl.ds(LRU_HALO, tm), :] = hs_ref[...]

        wa, wx = wa_ref[...], wx_ref[...]
        lam_v = lam_ref[...]
        xr = _tap_sum(xbuf, w4_ref, K4, LRU_HALO - (K4 - 1), 0, tm, False) + b4_ref[...]
        a, mult, r, ig, sp = _lru_gates(xr, wa, ba_ref[...], wx, bx_ref[...], lam_v)

        dyr = dyr_ref[...]
        gl, dgl = _gelu(rg_ref[...])
        dzg_ref[...] = (dyr * hs_ref[...] * dgl).astype(BF16)

        abuf[pl.ds(0, tm), :] = a
        abuf[pl.ds(tm, LRU_HALO), :] = anc[...]
        a_s[...] = abuf[pl.ds(1, tm), :]
        b_s[...] = dyr * gl
        cs, cout = _scan_tile(a_s, b_s, h_s, p_s, gc[pl.ds(0, 1), :], seg, True)
        gc[pl.ds(0, 1), :] = cout
        anc[pl.ds(0, 1), :] = a[0:1, :]
        for s in range(SUBLANES * LRU_GROUPS):
            rows = pl.ds(s * seg, seg)
            b_s[rows, :] = h_s[rows, :] + p_s[rows, :] * cs[s]
        g = b_s[...]

        d_a = g * hbuf[pl.ds(LRU_HALO - 1, tm), :]
        gx_ = g * xr
        d_log_a = d_a * a - (gx_ * ig) * (a * a / mult)
        dga = (d_log_a * (-LRU_C * sp)) * r * (1.0 - r)
        dgx = (gx_ * mult) * ig * (1.0 - ig)
        dga_b, dgx_b = dga.astype(BF16), dgx.astype(BF16)
        dxr = g * mult * ig + _dot_nt(dga_b, wa) + _dot_nt(dgx_b, wx)
        xb = xr.astype(BF16)
        dwa_ref[...] += _dot_tn(xb, dga_b)
        dwx_ref[...] += _dot_tn(xb, dgx_b)
        st_ref[pl.ds(K4, 1), :] += _colsum(dxr)
        st_ref[pl.ds(K4 + 1, 1), :] += _colsum(dga)
        st_ref[pl.ds(K4 + 2, 1), :] += _colsum(dgx)
        st_ref[pl.ds(K4 + 3, 1), :] += _colsum(d_log_a * (-LRU_C * r)) * (-jax.nn.sigmoid(-lam_v))

        dbuf[pl.ds(0, tm), :] = dxr
        for k in range(K4):
            st_ref[pl.ds(k, 1), :] += _colsum(dxr * xbuf[pl.ds(LRU_HALO - (K4 - 1) + k, tm), :])
        dzx_ref[...] = _tap_sum(dbuf, w4_ref, K4, 0, 0, tm, True).astype(BF16)
        dbuf[pl.ds(tm, LRU_HALO), :] = dbuf[pl.ds(0, LRU_HALO), :]

    def rev(col):
        return lambda c, i: (nI - 1 - i, col + c)

    def rev_prev(col):
        return lambda c, i: (jnp.maximum((nI - 1 - i) * hb - 1, 0), col + c)

    vec = pl.BlockSpec((1, LANES), lambda c, i: (0, c))
    mat = pl.BlockSpec((None, LANES, LANES), lambda c, i: (c, 0, 0))
    big = pltpu.VMEM((tm, LANES), F32)
    halo = pltpu.VMEM((tm + LRU_HALO, LANES), F32)
    return pl.pallas_call(
        body, grid=(nC, nI),
        in_specs=[pl.BlockSpec((tm, LANES), rev(cd)),
                  pl.BlockSpec((tm, LANES), rev(0)), pl.BlockSpec((LRU_HALO, LANES), rev_prev(0)),
                  pl.BlockSpec((tm, LANES), rev(cx)), pl.BlockSpec((LRU_HALO, LANES), rev_prev(cx)),
                  pl.BlockSpec((tm, LANES), rev(cg)),
                  pl.BlockSpec((K4, LANES), lambda c, i: (0, c)), vec, mat, vec, mat, vec, vec],
        out_specs=[pl.BlockSpec((tm, LANES), rev(0)), pl.BlockSpec((tm, LANES), rev(0)),
                   pl.BlockSpec((SUBLANES, LANES), lambda c, i: (0, c)), mat, mat],
        out_shape=[jax.ShapeDtypeStruct((T, W), BF16), jax.ShapeDtypeStruct((T, W), BF16),
                   jax.ShapeDtypeStruct((SUBLANES, W), F32),
                   jax.ShapeDtypeStruct((nC, LANES, LANES), F32), jax.ShapeDtypeStruct((nC, LANES, LANES), F32)],
        scratch_shapes=[halo, halo, halo, big, big, big, big, halo,
                        pltpu.VMEM((SUBLANES, LANES), F32), pltpu.VMEM((SUBLANES, LANES), F32)],
        compiler_params=_params("parallel", "arbitrary"), name="lru_bwd")(
            dcat, hs, hs, z, z, z, w4, b4, wa, ba, wx, bx, lam)


def _mix_out_fwd(x, u, yr, wout):
    T, D = x.shape
    C, W = u.shape[1], yr.shape[1]
    tm = _tile(T, TOK_TILE)

    def body(x_ref, u_ref, yr_ref, w_ref, y_ref):
        y_ref[...] = (x_ref[...] + _dot(u_ref[...], w_ref[pl.ds(0, C), :])
                      + _dot(yr_ref[...], w_ref[pl.ds(C, W), :]))

    return pl.pallas_call(
        body, grid=(T // tm,),
        in_specs=[pl.BlockSpec((tm, D), lambda i: (i, 0)), pl.BlockSpec((tm, C), lambda i: (i, 0)),
                  pl.BlockSpec((tm, W), lambda i: (i, 0)),
                  pl.BlockSpec((C + W, D), lambda i: (0, 0), pipeline_mode=pl.Buffered(1))],
        out_specs=pl.BlockSpec((tm, D), lambda i: (i, 0)),
        out_shape=jax.ShapeDtypeStruct((T, D), F32),
        compiler_params=_params("parallel"), name="mix_out_fwd")(x, u, yr, wout)


def _mix_out_bwd(dy, u, yr, wout):
    T, D = dy.shape
    C, W = u.shape[1], yr.shape[1]
    tm = _tile(T, BWD_TILE)

    def body(dy_ref, u_ref, yr_ref, w_ref, dcat_ref, dw_ref):
        @pl.when(pl.program_id(0) == 0)
        def _():
            dw_ref[...] = jnp.zeros_like(dw_ref)

        dyb = dy_ref[...].astype(BF16)
        dcat_ref[...] = _dot_nt(dyb, w_ref[...])
        dw_ref[pl.ds(0, C), :] += _dot_tn(u_ref[...], dyb)
        dw_ref[pl.ds(C, W), :] += _dot_tn(yr_ref[...], dyb)

    return pl.pallas_call(
        body, grid=(T // tm,),
        in_specs=[pl.BlockSpec((tm, D), lambda i: (i, 0)), pl.BlockSpec((tm, C), lambda i: (i, 0)),
                  pl.BlockSpec((tm, W), lambda i: (i, 0)),
                  pl.BlockSpec((C + W, D), lambda i: (0, 0), pipeline_mode=pl.Buffered(1))],
        out_specs=[pl.BlockSpec((tm, C + W), lambda i: (i, 0)), pl.BlockSpec((C + W, D), lambda i: (0, 0))],
        out_shape=[jax.ShapeDtypeStruct((T, C + W), F32), jax.ShapeDtypeStruct((C + W, D), F32)],
        compiler_params=_params("arbitrary"), name="mix_out_bwd")(dy, u, yr, wout)


def _mix_in_bwd(dzc, dzx, dzg, x, dy, g, win):
    T, D = x.shape
    ns, ws = win.shape[0], win.shape[2]
    tm = _tile(T, BWD_TILE)
    parts = []
    for j in range(ns):
        lo = j * ws
        if lo < dzc.shape[1]:
            parts.append((0, lo))
        elif lo < dzc.shape[1] + dzx.shape[1]:
            parts.append((1, lo - dzc.shape[1]))
        else:
            parts.append((2, lo - dzc.shape[1] - dzx.shape[1]))

    def body(dzc_ref, dzx_ref, dzg_ref, x_ref, dy_ref, g_ref, w_ref, dx_ref, dw_ref, dg_ref):
        @pl.when(pl.program_id(0) == 0)
        def _():
            dw_ref[...] = jnp.zeros_like(dw_ref)
            dg_ref[...] = jnp.zeros_like(dg_ref)

        xh, r = _rms_stats(x_ref[...])
        gv = g_ref[...]
        hb = (xh * gv).astype(BF16)
        srcs = (dzc_ref, dzx_ref, dzg_ref)
        dh = jnp.zeros((tm, D), F32)
        for j, (si, off) in enumerate(parts):
            dzj = srcs[si][:, pl.ds(off, ws)]
            dh = dh + _dot_nt(dzj, w_ref[j])
            dw_ref[j] += _dot_tn(hb, dzj)
        dx_ref[...] = dy_ref[...] + _rms_bwd(dh, xh, r, gv)
        dg_ref[...] += _colsum(dh * xh)

    def tok(n):
        return pl.BlockSpec((tm, n), lambda i: (i, 0))

    vec = pl.BlockSpec((1, D), lambda i: (0, 0))
    return pl.pallas_call(
        body, grid=(T // tm,),
        in_specs=[tok(dzc.shape[1]), tok(dzx.shape[1]), tok(dzg.shape[1]), tok(D), tok(D), vec,
                  pl.BlockSpec((ns, D, ws), lambda i: (0, 0, 0), pipeline_mode=pl.Buffered(1))],
        out_specs=[tok(D), pl.BlockSpec((ns, D, ws), lambda i: (0, 0, 0)), vec],
        out_shape=[jax.ShapeDtypeStruct((T, D), F32), jax.ShapeDtypeStruct((ns, D, ws), F32),
                   jax.ShapeDtypeStruct((1, D), F32)],
        compiler_params=_params("arbitrary"), name="mix_in_bwd")(dzc, dzx, dzg, x, dy, g, win)


def _adamw(w, g, m, v, name):
    R, Cc = w.shape
    tr = _tile(R, max(SUBLANES, (1 << 19) // Cc))
    c1 = 1.0 - ADAM_B1 ** ADAM_STEP
    c2 = 1.0 - ADAM_B2 ** ADAM_STEP

    def body(w_ref, g_ref, m_ref, v_ref, d_ref, nm_ref, nv_ref):
        gv = g_ref[...]
        nm = ADAM_B1 * m_ref[...] + (1.0 - ADAM_B1) * gv
        nv = ADAM_B2 * v_ref[...] + (1.0 - ADAM_B2) * (gv * gv)
        nm_ref[...] = nm
        nv_ref[...] = nv
        d_ref[...] = -ADAM_LR * ((nm / c1) / (jnp.sqrt(nv / c2) + ADAM_EPS) + ADAM_WD * w_ref[...])

    blk = pl.BlockSpec((tr, Cc), lambda i: (i, 0))
    sds = jax.ShapeDtypeStruct((R, Cc), F32)
    return pl.pallas_call(
        body, grid=(R // tr,), in_specs=[blk] * 4, out_specs=[blk] * 3, out_shape=[sds] * 3,
        compiler_params=_params("parallel"), name=name)(w, g, m, v)


def _here():
    return lax.axis_index("x"), lax.axis_index("y"), lax.axis_index("c")


def _chip_at(x, y, m):
    return x ^ (m >> 1), y ^ (m & 1)


ANY = pl.BlockSpec(memory_space=pl.ANY)


def _place_cast(srcs, idx, dtype, name):
    n = len(srcs)
    R, Cc = srcs[0].shape
    tr = _tile(R, max(16, (1 << 18) // Cc), 16)

    def body(i_ref, *refs):
        o_ref = refs[n]
        for k in range(n):
            o_ref[k] = refs[k][...].astype(dtype)

    blk = pl.BlockSpec((tr, Cc), lambda i, s: (i, 0))
    return pl.pallas_call(
        body,
        grid_spec=pltpu.PrefetchScalarGridSpec(
            num_scalar_prefetch=1, grid=(R // tr,), in_specs=[blk] * n,
            out_specs=pl.BlockSpec((n, None, tr, Cc), lambda i, s: (0, s[1], i, 0))),
        out_shape=jax.ShapeDtypeStruct((n, N_CHIPS, R, Cc), dtype),
        compiler_params=_params("parallel"), name=name)(idx, *srcs)


def _gather_weights(lands):
    n = len(lands)

    def body(*refs):
        outs = refs[n:2 * n]
        send1, recv1, send2, recv2 = refs[2 * n:]
        x, y, c = _here()
        own = 2 * x + y

        def half(ref, chip, cc):
            rh = ref.shape[-2] // 2
            lead = (slice(None),) * (len(ref.shape) - 3)
            return ref.at[lead + (chip, pl.ds(cc * rh, rh), slice(None))]

        first = []
        for k in range(n):
            for m in (1, 2, 3):
                px, py = _chip_at(x, y, m)
                cp = pltpu.make_async_remote_copy(
                    src_ref=half(outs[k], own, c), dst_ref=half(outs[k], own, c),
                    send_sem=send1.at[k, m - 1], recv_sem=recv1.at[k, m - 1],
                    device_id=(px, py, c), device_id_type=MESH)
                cp.start()
                first.append(cp)

        passed = []
        for k in range(n):
            for m in (1, 2, 3):
                px, py = _chip_at(x, y, m)
                peer = 2 * px + py
                got = half(outs[k], peer, c)
                pltpu.make_async_remote_copy(
                    src_ref=got, dst_ref=got, send_sem=send1.at[k, m - 1], recv_sem=recv1.at[k, m - 1],
                    device_id=(px, py, c), device_id_type=MESH).wait_recv()
                cp = pltpu.make_async_remote_copy(
                    src_ref=got, dst_ref=got, send_sem=send2.at[k, m - 1], recv_sem=recv2.at[k, m - 1],
                    device_id=(x, y, 1 - c), device_id_type=MESH)
                cp.start()
                passed.append(cp)

        for k in range(n):
            for m in (1, 2, 3):
                px, py = _chip_at(x, y, m)
                other = half(outs[k], 2 * px + py, 1 - c)
                pltpu.make_async_remote_copy(
                    src_ref=other, dst_ref=other, send_sem=send2.at[k, m - 1], recv_sem=recv2.at[k, m - 1],
                    device_id=(x, y, 1 - c), device_id_type=MESH).wait_recv()
        for cp in first + passed:
            cp.wait_send()

    return pl.pallas_call(
        body, in_specs=[ANY] * n, out_specs=[ANY] * n,
        out_shape=[jax.ShapeDtypeStruct(a.shape, a.dtype) for a in lands],
        input_output_aliases={k: k for k in range(n)},
        scratch_shapes=[pltpu.SemaphoreType.DMA((n, 3)), pltpu.SemaphoreType.DMA((n, 3)),
                        pltpu.SemaphoreType.DMA((n, 3)), pltpu.SemaphoreType.DMA((n, 3))],
        name="gather_weights")(*lands)


HBM = pl.BlockSpec(memory_space=pltpu.HBM)
SEM = pl.BlockSpec(memory_space=pltpu.SEMAPHORE)
EFFECT = pltpu.SideEffectType.DATAFLOW_SIDE_EFFECTING


def _in_hbm(a):
    return pltpu.with_memory_space_constraint(a, pltpu.HBM)


def _gather_copies(land_refs, send, recv):
    x, y, c = _here()
    own = 2 * x + y
    cps = []
    for k in range(len(land_refs)):
        lead = (slice(None),) * (len(land_refs[k].shape) - 3)
        mine = land_refs[k].at[lead + (own,)]
        for m in (1, 2, 3):
            px, py = _chip_at(x, y, m)
            cps.append(pltpu.make_async_remote_copy(
                src_ref=mine, dst_ref=mine, send_sem=send.at[3 * k + m - 1], recv_sem=recv.at[3 * k + m - 1],
                device_id=(px, py, c), device_id_type=MESH))
    return cps


def _gather_start(lands, after, name):
    n = len(lands)

    def body(*refs):
        lz = refs[:n]
        send, recv = refs[n + 1], refs[n + 2]
        token = refs[-1]
        for cp in _gather_copies(lz, send, recv):
            cp.start()
        token[...] = jnp.zeros_like(token)

    hbm = [pltpu.HBM(a.shape, a.dtype) for a in lands]
    outs = pl.pallas_call(
        body, name=name,
        in_specs=[HBM] * n + [ANY],
        out_specs=[SEM, SEM] + [HBM] * n + [pl.BlockSpec(memory_space=pltpu.VMEM)],
        out_shape=[pltpu.SemaphoreType.DMA((3 * n,)), pltpu.SemaphoreType.DMA((3 * n,))] + hbm
        + [jax.ShapeDtypeStruct((SUBLANES, LANES), F32)],
        input_output_aliases={k: 2 + k for k in range(n)},
        compiler_params=pltpu.CompilerParams(has_side_effects=EFFECT),
    )(*[_in_hbm(a) for a in lands], after)
    return outs[0], outs[1], outs[2:2 + n], outs[-1]


def _gather_wait(send, recv, lands, after, name):
    n = len(lands)

    def body(*refs):
        lz = refs[:n]
        send_r, recv_r = refs[n], refs[n + 1]
        for cp in _gather_copies(lz, send_r, recv_r):
            cp.wait_send()
            cp.wait_recv()

    hbm = [pltpu.HBM(a.shape, a.dtype) for a in lands]
    return pl.pallas_call(
        body, name=name,
        in_specs=[HBM] * n + [SEM, SEM, ANY],
        out_specs=[HBM] * n, out_shape=hbm,
        input_output_aliases={k: k for k in range(n)},
        compiler_params=pltpu.CompilerParams(has_side_effects=EFFECT),
    )(*lands, send, recv, after)


def _exchange_copies(part_refs, slot_refs, send, recv):
    x, y, c = _here()
    cps = []
    for k in range(len(part_refs)):
        for m in (1, 2, 3):
            px, py = _chip_at(x, y, m)
            cps.append(pltpu.make_async_remote_copy(
                src_ref=part_refs[k].at[2 * px + py], dst_ref=slot_refs[k].at[m - 1],
                send_sem=send.at[3 * k + m - 1], recv_sem=recv.at[3 * k + m - 1],
                device_id=(px, py, c), device_id_type=MESH))
    return cps


def _exchange_start(parts, name):
    n = len(parts)
    lands = [lax.empty((N_CHIPS - 1,) + p.shape[1:], p.dtype) for p in parts]

    def body(*refs):
        ins, lz = refs[:n], refs[n:2 * n]
        send, recv = refs[2 * n], refs[2 * n + 1]
        token = refs[-1]
        for cp in _exchange_copies(ins, lz, send, recv):
            cp.start()
        token[...] = jnp.zeros_like(token)

    hbm = [pltpu.HBM(a.shape, a.dtype) for a in list(parts) + lands]
    outs = pl.pallas_call(
        body, name=name,
        in_specs=[HBM] * (2 * n),
        out_specs=[SEM, SEM] + [HBM] * (2 * n) + [pl.BlockSpec(memory_space=pltpu.VMEM)],
        out_shape=[pltpu.SemaphoreType.DMA((3 * n,)), pltpu.SemaphoreType.DMA((3 * n,))] + hbm
        + [jax.ShapeDtypeStruct((SUBLANES, LANES), F32)],
        input_output_aliases={k: 2 + k for k in range(2 * n)},
        compiler_params=pltpu.CompilerParams(has_side_effects=EFFECT),
    )(*[_in_hbm(a) for a in parts], *[_in_hbm(a) for a in lands])
    return outs[0], outs[1], outs[2:2 + n], outs[2 + n:2 + 2 * n], outs[-1]


def _exchange_wait(send, recv, parts, lands, after, name):
    n = len(parts)

    def body(*refs):
        ins, lz = refs[:n], refs[n:2 * n]
        send_r, recv_r = refs[2 * n], refs[2 * n + 1]
        for cp in _exchange_copies(ins, lz, send_r, recv_r):
            cp.wait_send()
            cp.wait_recv()

    hbm = [pltpu.HBM(a.shape, a.dtype) for a in list(parts) + list(lands)]
    outs = pl.pallas_call(
        body, name=name,
        in_specs=[HBM] * (2 * n) + [SEM, SEM, ANY],
        out_specs=[HBM] * (2 * n), out_shape=hbm,
        input_output_aliases={k: k for k in range(2 * n)},
        compiler_params=pltpu.CompilerParams(has_side_effects=EFFECT),
    )(*parts, *lands, send, recv, after)
    return outs[:n], outs[n:]


def _swap_halves_out(grads, name):
    n = len(grads)
    out_shapes = [jax.ShapeDtypeStruct((g.shape[0], g.shape[1] // 2, g.shape[2]), g.dtype) for g in grads]

    def body(*refs):
        ins, outs = refs[:n], refs[n:2 * n]
        send, recv = refs[2 * n:]
        x, y, c = _here()
        cps = []
        for k in range(n):
            rh = ins[k].shape[1] // 2
            cp = pltpu.make_async_remote_copy(
                src_ref=ins[k].at[:, pl.ds((1 - c) * rh, rh), :], dst_ref=outs[k],
                send_sem=send.at[k], recv_sem=recv.at[k], device_id=(x, y, 1 - c), device_id_type=MESH)
            cp.start()
            cps.append(cp)
        for cp in cps:
            cp.wait()

    return pl.pallas_call(
        body, in_specs=[ANY] * n, out_specs=[ANY] * n, out_shape=out_shapes,
        scratch_shapes=[pltpu.SemaphoreType.DMA((n,)), pltpu.SemaphoreType.DMA((n,))],
        name=name)(*grads)


def _swap_copies(grad_refs, land_refs, send, recv):
    x, y, c = _here()
    cps = []
    for k in range(len(grad_refs)):
        rh = grad_refs[k].shape[1] // 2
        cps.append(pltpu.make_async_remote_copy(
            src_ref=grad_refs[k].at[:, pl.ds((1 - c) * rh, rh), :], dst_ref=land_refs[k],
            send_sem=send.at[k], recv_sem=recv.at[k], device_id=(x, y, 1 - c), device_id_type=MESH))
    return cps


def _swap_start(grads, name):
    n = len(grads)
    lands = [lax.empty((g.shape[0], g.shape[1] // 2, g.shape[2]), g.dtype) for g in grads]

    def body(*refs):
        ins, lz = refs[:n], refs[n:2 * n]
        send, recv = refs[2 * n], refs[2 * n + 1]
        token = refs[-1]
        for cp in _swap_copies(ins, lz, send, recv):
            cp.start()
        token[...] = jnp.zeros_like(token)

    hbm = [pltpu.HBM(a.shape, a.dtype) for a in list(grads) + lands]
    outs = pl.pallas_call(
        body, name=name,
        in_specs=[HBM] * (2 * n),
        out_specs=[SEM, SEM] + [HBM] * (2 * n) + [pl.BlockSpec(memory_space=pltpu.VMEM)],
        out_shape=[pltpu.SemaphoreType.DMA((n,)), pltpu.SemaphoreType.DMA((n,))] + hbm
        + [jax.ShapeDtypeStruct((SUBLANES, LANES), F32)],
        input_output_aliases={k: 2 + k for k in range(2 * n)},
        compiler_params=pltpu.CompilerParams(has_side_effects=EFFECT),
    )(*[_in_hbm(a) for a in grads], *[_in_hbm(a) for a in lands])
    return outs[0], outs[1], outs[2:2 + n], outs[2 + n:2 + 2 * n], outs[-1]


def _swap_wait(send, recv, grads, lands, after, name):
    n = len(grads)

    def body(*refs):
        ins, lz = refs[:n], refs[n:2 * n]
        send_r, recv_r = refs[2 * n], refs[2 * n + 1]
        for cp in _swap_copies(ins, lz, send_r, recv_r):
            cp.wait_send()
            cp.wait_recv()

    hbm = [pltpu.HBM(a.shape, a.dtype) for a in list(grads) + list(lands)]
    outs = pl.pallas_call(
        body, name=name,
        in_specs=[HBM] * (2 * n) + [SEM, SEM, ANY],
        out_specs=[HBM] * (2 * n), out_shape=hbm,
        input_output_aliases={k: k for k in range(2 * n)},
        compiler_params=pltpu.CompilerParams(has_side_effects=EFFECT),
    )(*grads, *lands, send, recv, after)
    return outs[:n], outs[n:]


def _add_cast(g, other, cidx, name):
    ns, R, Cc = g.shape
    rh = R // 2
    tr = _tile(rh, max(16, (1 << 19) // Cc), 16)
    nb = rh // tr

    def body(c_ref, g_ref, o_ref, s_ref):
        s_ref[...] = (g_ref[...] + o_ref[...]).astype(BF16)

    return pl.pallas_call(
        body,
        grid_spec=pltpu.PrefetchScalarGridSpec(
            num_scalar_prefetch=1, grid=(ns, nb),
            in_specs=[pl.BlockSpec((None, tr, Cc), lambda k, i, c: (k, c[0] * nb + i, 0)),
                      pl.BlockSpec((None, tr, Cc), lambda k, i, c: (k, i, 0))],
            out_specs=pl.BlockSpec((None, tr, Cc), lambda k, i, c: (k, i, 0))),
        out_shape=jax.ShapeDtypeStruct((ns, rh, Cc), BF16),
        compiler_params=_params("parallel", "parallel"), name=name)(cidx, g, other)


def _sum_slots(part, got, idx, name):
    ns, rh, Cc = got.shape
    tr = _tile(rh, max(16, (1 << 18) // Cc), 16)
    nb = rh // tr

    def body(i_ref, p_ref, b_ref, o_ref):
        acc = p_ref[...].astype(F32)
        for m in range(ns):
            acc = acc + b_ref[m].astype(F32)
        o_ref[...] = acc

    return pl.pallas_call(
        body,
        grid_spec=pltpu.PrefetchScalarGridSpec(
            num_scalar_prefetch=1, grid=(nb,),
            in_specs=[pl.BlockSpec((None, tr, Cc), lambda i, s: (s[1], i, 0)),
                      pl.BlockSpec((ns, tr, Cc), lambda i, s: (0, i, 0))],
            out_specs=pl.BlockSpec((tr, Cc), lambda i, s: (s[0] * nb + i, 0))),
        out_shape=jax.ShapeDtypeStruct((2 * rh, Cc), F32),
        compiler_params=_params("parallel"), name=name)(idx, part, got)


def _share_copies(block_refs, send, recv):
    x, y, c = _here()
    cps = []
    for k, ref in enumerate(block_refs):
        rh = ref.shape[0] // 2
        mine = ref.at[pl.ds(c * rh, rh), :]
        cps.append(pltpu.make_async_remote_copy(
            src_ref=mine, dst_ref=mine, send_sem=send.at[k], recv_sem=recv.at[k],
            device_id=(x, y, 1 - c), device_id_type=MESH))
    return cps


def _share_start(blocks, name):
    n = len(blocks)

    def body(*refs):
        send, recv = refs[n], refs[n + 1]
        token = refs[-1]
        for cp in _share_copies(refs[:n], send, recv):
            cp.start()
        token[...] = jnp.zeros_like(token)

    hbm = [pltpu.HBM(a.shape, a.dtype) for a in blocks]
    outs = pl.pallas_call(
        body, name=name,
        in_specs=[HBM] * n,
        out_specs=[SEM, SEM] + [HBM] * n + [pl.BlockSpec(memory_space=pltpu.VMEM)],
        out_shape=[pltpu.SemaphoreType.DMA((n,)), pltpu.SemaphoreType.DMA((n,))] + hbm
        + [jax.ShapeDtypeStruct((SUBLANES, LANES), F32)],
        input_output_aliases={k: 2 + k for k in range(n)},
        compiler_params=pltpu.CompilerParams(has_side_effects=EFFECT),
    )(*[_in_hbm(a) for a in blocks])
    return outs[0], outs[1], outs[2:2 + n], outs[-1]


def _share_wait(send, recv, blocks, after, name):
    n = len(blocks)

    def body(*refs):
        for cp in _share_copies(refs[:n], refs[n], refs[n + 1]):
            cp.wait_send()
            cp.wait_recv()

    return pl.pallas_call(
        body, name=name,
        in_specs=[HBM] * n + [SEM, SEM, ANY],
        out_specs=[HBM] * n, out_shape=[pltpu.HBM(a.shape, a.dtype) for a in blocks],
        input_output_aliases={k: k for k in range(n)},
        compiler_params=pltpu.CompilerParams(has_side_effects=EFFECT),
    )(*blocks, send, recv, after)


def _share_halves(blocks, name):
    n = len(blocks)

    def body(*refs):
        outs = refs[n:2 * n]
        send, recv = refs[2 * n:]
        cps = _share_copies(outs, send, recv)
        for cp in cps:
            cp.start()
        for cp in cps:
            cp.wait()

    return pl.pallas_call(
        body, in_specs=[ANY] * n, out_specs=[ANY] * n,
        out_shape=[jax.ShapeDtypeStruct(b.shape, b.dtype) for b in blocks],
        input_output_aliases={k: k for k in range(n)},
        scratch_shapes=[pltpu.SemaphoreType.DMA((n,)), pltpu.SemaphoreType.DMA((n,))],
        name=name)(*blocks)


def _small_copies(p_ref, slot_ref, send, recv):
    x, y, c = _here()
    mine = slot_ref.at[4 * x + 2 * y + c]
    cps = []
    for m in range(1, N_DEV):
        peer = (x ^ (m >> 2), y ^ ((m >> 1) & 1), c ^ (m & 1))
        cps.append(pltpu.make_async_remote_copy(
            src_ref=p_ref, dst_ref=mine, send_sem=send.at[m - 1], recv_sem=recv.at[m - 1],
            device_id=peer, device_id_type=MESH))
    return cps


def _small_start(packed):
    slots = lax.empty((N_DEV,) + packed.shape, packed.dtype)

    def body(p_ref, s_ref, send, recv, p_thru, s_thru, token):
        for cp in _small_copies(p_ref, s_ref, send, recv):
            cp.start()
        token[...] = jnp.zeros_like(token)

    return pl.pallas_call(
        body, name="small_start",
        in_specs=[HBM, HBM],
        out_specs=[SEM, SEM, HBM, HBM, pl.BlockSpec(memory_space=pltpu.VMEM)],
        out_shape=[pltpu.SemaphoreType.DMA((N_DEV - 1,)), pltpu.SemaphoreType.DMA((N_DEV - 1,)),
                   pltpu.HBM(packed.shape, packed.dtype), pltpu.HBM(slots.shape, slots.dtype),
                   jax.ShapeDtypeStruct((SUBLANES, LANES), F32)],
        input_output_aliases={0: 2, 1: 3},
        compiler_params=pltpu.CompilerParams(has_side_effects=EFFECT),
    )(_in_hbm(packed), _in_hbm(slots))


def _small_wait(send, recv, packed, slots, after):
    def body(p_ref, s_ref, send_r, recv_r, after_ref, p_out, s_out):
        for cp in _small_copies(p_ref, s_ref, send_r, recv_r):
            cp.wait_send()
            cp.wait_recv()

    return pl.pallas_call(
        body, name="small_wait",
        in_specs=[HBM, HBM, SEM, SEM, ANY], out_specs=[HBM, HBM],
        out_shape=[pltpu.HBM(packed.shape, packed.dtype), pltpu.HBM(slots.shape, slots.dtype)],
        input_output_aliases={0: 0, 1: 1},
        compiler_params=pltpu.CompilerParams(has_side_effects=EFFECT),
    )(packed, slots, send, recv, after)


def _sum_devices(packed, slots, me):
    n, R, _ = slots.shape
    tr = _tile(R, 1024)

    def body(m_ref, p_ref, s_ref, o_ref):
        own = p_ref[...]
        acc = None
        for d in range(n):
            term = jnp.where(m_ref[0] == d, own, s_ref[d])
            acc = term if acc is None else acc + term
        o_ref[...] = acc

    return pl.pallas_call(
        body,
        grid_spec=pltpu.PrefetchScalarGridSpec(
            num_scalar_prefetch=1, grid=(R // tr,),
            in_specs=[pl.BlockSpec((tr, LANES), lambda i, m: (i, 0)),
                      pl.BlockSpec((n, tr, LANES), lambda i, m: (0, i, 0))],
            out_specs=pl.BlockSpec((tr, LANES), lambda i, m: (i, 0))),
        out_shape=jax.ShapeDtypeStruct((R, LANES), F32),
        compiler_params=_params("parallel"), name="sum_devices")(me, packed, slots)


def _pack(arrs):
    rows, parts = [], []
    for a in arrs:
        flat = a.reshape(-1)
        r = -(-flat.shape[0] // (SUBLANES * LANES)) * SUBLANES
        parts.append(jnp.pad(flat, (0, r * LANES - flat.shape[0])).reshape(r, LANES))
        rows.append(r)
    return jnp.concatenate(parts, axis=0), rows


def _unpack(packed, rows, shapes):
    out, r0 = [], 0
    for r, shp in zip(rows, shapes):
        size = math.prod(shp)
        out.append(packed[r0:r0 + r].reshape(-1)[:size].reshape(shp))
        r0 += r
    return out


def _block_diag(w, per):
    H, dh, _ = w.shape
    w4 = w.reshape(H // per, per, dh, dh)
    eye = jnp.eye(per, dtype=w.dtype)
    return (w4[:, :, :, None, :] * eye[None, :, None, :, None]).reshape(H // per, per * dh, per * dh)


def _block_diag_take(d, per):
    n, s, _ = d.shape
    dh = s // per
    d5 = d.reshape(n, per, dh, per, dh)
    return jnp.stack([d5[:, h, :, h, :] for h in range(per)], axis=1).reshape(n * per, dh, dh)


def kernel(x, ffn1_norm, ffn1_w_gate, ffn1_w_up, ffn1_w_down, mix_norm, w_in, conv_dw, conv_dw_bias, conv_ln_g, conv_ln_b, lru_conv_w, lru_conv_b, lru_w_a, lru_b_a, lru_w_x, lru_b_x, lru_lambda, w_out, ffn2_norm, ffn2_w_gate, ffn2_w_up, ffn2_w_down, final_norm, loss_target, m_ffn1_norm, m_ffn1_w_gate, m_ffn1_w_up, m_ffn1_w_down, m_mix_norm, m_w_in, m_conv_dw, m_conv_dw_bias, m_conv_ln_g, m_conv_ln_b, m_lru_conv_w, m_lru_conv_b, m_lru_w_a, m_lru_b_a, m_lru_w_x, m_lru_b_x, m_lru_lambda, m_w_out, m_ffn2_norm, m_ffn2_w_gate, m_ffn2_w_up, m_ffn2_w_down, m_final_norm, v_ffn1_norm, v_ffn1_w_gate, v_ffn1_w_up, v_ffn1_w_down, v_mix_norm, v_w_in, v_conv_dw, v_conv_dw_bias, v_conv_ln_g, v_conv_ln_b, v_lru_conv_w, v_lru_conv_b, v_lru_w_a, v_lru_b_a, v_lru_w_x, v_lru_b_x, v_lru_lambda, v_w_out, v_ffn2_norm, v_ffn2_w_gate, v_ffn2_w_up, v_ffn2_w_down, v_final_norm):
    names = ['ffn1_norm', 'ffn1_w_gate', 'ffn1_w_up', 'ffn1_w_down', 'mix_norm', 'w_in', 'conv_dw', 'conv_dw_bias',
             'conv_ln_g', 'conv_ln_b', 'lru_conv_w', 'lru_conv_b', 'lru_w_a', 'lru_b_a', 'lru_w_x', 'lru_b_x',
             'lru_lambda', 'w_out', 'ffn2_norm', 'ffn2_w_gate', 'ffn2_w_up', 'ffn2_w_down', 'final_norm']
    env = dict(locals())
    W = {n: env[n] for n in names}
    M = {n: env['m_' + n] for n in names}
    V = {n: env['v_' + n] for n in names}

    xi, yi, ci = _here()
    chip = 2 * xi + yi
    cidx = ci.astype(jnp.int32).reshape(1)
    T, D = x.shape[-2], x.shape[-1]
    xs = x.reshape(T, D)
    tgt = loss_target.reshape(T, D)
    K, Cs = conv_dw.shape
    C = conv_dw_bias.shape[0]
    Wl = lru_conv_b.shape[0]
    K4 = lru_conv_w.shape[0]
    heads, dh, _ = lru_w_a.shape
    per = LANES // dh

    def row(v):
        return v.reshape(1, -1)

    tform = ('ffn1_w_gate', 'ffn1_w_up', 'ffn2_w_gate', 'ffn2_w_up')
    for n in tform:
        W[n], M[n], V[n] = W[n].T, M[n].T, V[n].T
    kp = -(-K // SUBLANES) * SUBLANES
    taps = jnp.concatenate([conv_dw, jnp.zeros((kp - K, Cs), F32), lru_conv_w,
                            jnp.zeros((2 * SUBLANES - K4, Cs), F32)], axis=0)
    idx = jnp.stack([ci, chip]).astype(jnp.int32)
    (wff1,) = _gather_weights([_place_cast([W['ffn1_w_gate'], W['ffn1_w_up'], ffn1_w_down], idx, BF16, "place_ffn1")])
    mixl = [_place_cast([w_in], idx, BF16, "place_w_in"), _place_cast([w_out], idx, BF16, "place_w_out"),
            _place_cast([taps], idx, F32, "place_taps")]
    msend, mrecv, mixl, mtok = _gather_start(mixl, wff1, "gather_mix_start")
    ff2l = _place_cast([W['ffn2_w_gate'], W['ffn2_w_up'], ffn2_w_down], idx, BF16, "place_ffn2")
    fsend, frecv, ff2l, ftok = _gather_start([ff2l], mtok, "gather_ffn2_start")
    wa_bd = _block_diag(lru_w_a, per).astype(BF16)
    wx_bd = _block_diag(lru_w_x, per).astype(BF16)

    x1, a1, b1 = _ffn_fwd(xs, row(ffn1_norm) + ftok[0:1, 0:1], wff1, "ffn1_fwd")
    win, wout, taps = _gather_wait(msend, mrecv, mixl, x1, "gather_mix_wait")
    win, wout, taps = win[0], wout.reshape(-1, D), taps[0]
    conv_w_full = taps[:, :K].transpose(1, 0, 2).reshape(K, N_CHIPS * Cs)
    lru_w4_full = taps[:, kp:kp + K4].transpose(1, 0, 2).reshape(K4, N_CHIPS * Cs)
    z = _mix_in_fwd(x1, row(mix_norm), win)
    u, u1 = _conv_fwd(z, conv_w_full, row(conv_dw_bias), row(conv_ln_g), row(conv_ln_b))
    yr, hs = _lru_fwd(z, 2 * C, lru_w4_full, row(lru_conv_b), wa_bd, row(lru_b_a), wx_bd, row(lru_b_x),
                      row(lru_lambda))
    x2 = _mix_out_fwd(x1, u, yr, wout)
    (wff2,) = _gather_wait(fsend, frecv, ff2l, x2, "gather_ffn2_wait")
    dx3, a2, b2, loss_blk, d_final = _ffn_fwd(x2, row(ffn2_norm), wff2, "ffn2_fwd", head=(row(final_norm), tgt))

    dx2, da2, db2, p2, hb2, dyh2, d_ffn2n = _ffn_bwd_tok(dx3, x2, row(ffn2_norm), a2, b2, wff2, "ffn2_bwd")
    dwg2, dwu2, dwd2 = _ffn_wgrad([([da2, db2], hb2), ([p2], dyh2)], ftok, "ffn2_wgrad")
    wsend, wrecv, f2g, f2o, wtok = _swap_start([dwg2, dwu2, dwd2], "swap_ffn2_start")
    dcat, dwout = _mix_out_bwd(dx2, u, yr, wout)
    dzc, cst = _conv_bwd(dcat, u1, z, conv_w_full, row(conv_ln_g) + wtok[0:1, 0:1], row(conv_ln_b))
    dzx, dzg, lst, dwa_bd, dwx_bd = _lru_bwd(dcat, C, hs, z, 2 * C, lru_w4_full, row(lru_conv_b), wa_bd,
                                              row(lru_b_a), wx_bd, row(lru_b_x), row(lru_lambda))
    dx1, dwin, d_mixn = _mix_in_bwd(dzc, dzx, dzg, x1, dx2, row(mix_norm), win)

    early_names = ['w_in', 'w_out', 'ffn2_w_gate', 'ffn2_w_up', 'ffn2_w_down']
    mixg = [dwin, dwout.reshape(N_CHIPS, -1, D)]
    mixo = _swap_halves_out(mixg, "swap_halves_mix")
    f2g, f2o = _swap_wait(wsend, wrecv, f2g, f2o, dwin, "swap_ffn2_wait")
    e_parts = [_add_cast(g, o, cidx, "add_cast_" + n)
               for g, o, n in zip(mixg + list(f2g), list(mixo) + list(f2o), early_names)]
    esend, erecv, e_parts, e_lands, etok = _exchange_start(e_parts, "exchange_early_start")

    dx0, da1, db1, p1, hb1, dyh1, d_ffn1n = _ffn_bwd_tok(dx1, xs, row(ffn1_norm) + etok[0:1, 0:1], a1, b1, wff1,
                                                         "ffn1_bwd")

    small_names = ['ffn1_norm', 'mix_norm', 'conv_dw', 'conv_dw_bias', 'conv_ln_g', 'conv_ln_b', 'lru_conv_w',
                   'lru_conv_b', 'lru_w_a', 'lru_b_a', 'lru_w_x', 'lru_b_x', 'lru_lambda', 'ffn2_norm',
                   'final_norm']
    small = {
        'ffn1_norm': d_ffn1n, 'mix_norm': d_mixn, 'conv_dw': cst[:K], 'conv_dw_bias': cst[K + 1],
        'conv_ln_g': cst[K + 2], 'conv_ln_b': cst[K + 3], 'lru_conv_w': lst[:K4], 'lru_conv_b': lst[K4],
        'lru_w_a': _block_diag_take(dwa_bd, per), 'lru_b_a': lst[K4 + 1],
        'lru_w_x': _block_diag_take(dwx_bd, per), 'lru_b_x': lst[K4 + 2], 'lru_lambda': lst[K4 + 3],
        'ffn2_norm': d_ffn2n, 'final_norm': d_final,
    }
    packed, rows = _pack([small[n] for n in small_names] + [loss_blk[0:1, 0:1]])
    ssend, srecv, packed, sslots, stok = _small_start(packed)

    gu_names, d_names = ['ffn1_w_gate', 'ffn1_w_up'], ['ffn1_w_down']
    gu = _ffn_wgrad([([da1, db1], hb1)], stok, "ffn1_wgrad_gu", swap=True)
    gu_parts = [_add_cast(g, o, cidx, "add_cast_" + n) for g, o, n in zip(gu[:2], gu[2:], gu_names)]
    gsend, grecv, gu_parts, gu_lands, gtok = _exchange_start(gu_parts, "exchange_gu_start")
    dn = _ffn_wgrad([([p1], dyh1)], gtok, "ffn1_wgrad_d", swap=True)
    d_parts = [_add_cast(g, o, cidx, "add_cast_" + n) for g, o, n in zip(dn[:1], dn[1:], d_names)]
    dsend, drecv, d_parts, d_lands, ltok = _exchange_start(d_parts, "exchange_d_start")
    e_parts, e_slots = _exchange_wait(esend, erecv, e_parts, e_lands, ltok, "exchange_early_wait")
    delta, new_m, new_v = {}, {}, {}

    def finish(group, parts, slots, tag):
        halves = [_sum_slots(p, b, idx, "sum_slots_" + n) for p, b, n in zip(parts, slots, group)]
        for n, g in zip(group, _share_halves(halves, "share_halves_" + tag)):
            G[n] = g
            delta[n], new_m[n], new_v[n] = _adamw(W[n], g, M[n], V[n], "adamw_" + n)

    G = {}
    e_halves = [_sum_slots(p, b, idx, "sum_slots_" + n) for p, b, n in zip(e_parts, e_slots, early_names)]
    hsend, hrecv, e_halves, htok = _share_start(e_halves, "share_early_start")

    full_shapes = [(K, C) if n == 'conv_dw' else (K4, Wl) if n == 'lru_conv_w' else W[n].shape for n in small_names]
    packed, sslots = _small_wait(ssend, srecv, packed, sslots, htok)
    summed = _sum_devices(packed, sslots, (4 * xi + 2 * yi + ci).astype(jnp.int32).reshape(1))
    *small_sums, loss_sum = _unpack(summed, rows, full_shapes + [(1, 1)])
    for n, gsum in zip(small_names, small_sums):
        if n == 'conv_dw':
            gsum = lax.dynamic_slice_in_dim(gsum, chip * Cs, Cs, axis=1)
        elif n == 'lru_conv_w':
            gsum = lax.dynamic_slice_in_dim(gsum, chip * lru_conv_w.shape[1], lru_conv_w.shape[1], axis=1)
        G[n] = gsum

    pw, prow = _pack([W[n] for n in small_names])
    pg, _ = _pack([G[n] for n in small_names])
    pm, _ = _pack([M[n] for n in small_names])
    pv, _ = _pack([V[n] for n in small_names])
    sd, sm, sv = _adamw(pw, pg, pm, pv, "adamw_small")
    shapes = [W[n].shape for n in small_names]
    for n, a, b, c_ in zip(small_names, _unpack(sd, prow, shapes), _unpack(sm, prow, shapes),
                           _unpack(sv, prow, shapes)):
        delta[n], new_m[n], new_v[n] = a, b, c_

    for n, g in zip(early_names, _share_wait(hsend, hrecv, e_halves, sd, "share_early_wait")):
        G[n] = g
        delta[n], new_m[n], new_v[n] = _adamw(W[n], g, M[n], V[n], "adamw_" + n)
    done = sd[0:SUBLANES] + delta[early_names[-1]][0:SUBLANES, 0:LANES]
    gu_parts, gu_slots = _exchange_wait(gsend, grecv, gu_parts, gu_lands, done, "exchange_gu_wait")
    d_parts, d_slots = _exchange_wait(dsend, drecv, d_parts, d_lands, gu_slots[0], "exchange_d_wait")
    finish(gu_names + d_names, list(gu_parts) + list(d_parts), list(gu_slots) + list(d_slots), "last")

    loss = loss_sum[0, 0]
    grad_x = dx0.reshape(x.shape)
    for n in tform:
        G[n], delta[n], new_m[n], new_v[n] = G[n].T, delta[n].T, new_m[n].T, new_v[n].T
    return (loss, grad_x, *[G[n] for n in names], *[delta[n] for n in names],
            *[new_m[n] for n in names], *[new_v[n] for n in names])
```

```python
import functools
import math

import jax
import jax.numpy as jnp
from jax import lax
from jax.experimental import pallas as pl
from jax.experimental.pallas import tpu as pltpu

F32 = jnp.float32
BF16 = jnp.bfloat16
MESH = pl.DeviceIdType.MESH

RMS_EPS = 1e-6
LN_EPS = 1e-5
LRU_C = 8.0
FFN_RES_SCALE = 0.5
ADAM_LR = 0.001
ADAM_B1 = 0.9
ADAM_B2 = 0.999
ADAM_EPS = 1e-08
ADAM_WD = 0.01
ADAM_STEP = 10

LANES = 128
SUBLANES = 8
CONV_HALO = 32
LRU_HALO = 8
ROW_CHUNK = 64
VMEM_LIMIT = 56 * 1024 * 1024
N_CHIPS = 4
N_DEV = 8
TOK_TILE = 1024
BWD_TILE = 512
FFN_BWD_TILE = 512
BWD_ROWS = 32
FFN_BWD_CHAIN = 256
CONV_TILE = 512
LRU_TILE = 2048
LRU_GROUPS = 8


def _dot(a, b):
    return jnp.dot(a, b, preferred_element_type=F32)


def _dot_nt(a, b):
    return lax.dot_general(a, b, (((1,), (1,)), ((), ())), preferred_element_type=F32)


def _dot_tn(a, b):
    return lax.dot_general(a, b, (((0,), (0,)), ((), ())), preferred_element_type=F32)


def _tile(n, pref, mult=SUBLANES):
    for t in range(min(pref, n), 0, -1):
        if n % t == 0 and t % mult == 0:
            return t
    return n


def _params(*sem):
    return pltpu.CompilerParams(dimension_semantics=sem, vmem_limit_bytes=VMEM_LIMIT)


def _rms_stats(x):
    r = lax.rsqrt(jnp.mean(x * x, axis=-1, keepdims=True) + RMS_EPS)
    return x * r, r


def _rms_bwd(dh, xh, r, g):
    dxh = dh * g
    return r * (dxh - xh * jnp.mean(dxh * xh, axis=-1, keepdims=True))


def _colsum(v):
    return jnp.sum(v, axis=0, keepdims=True)


def _ffn_fwd(x, g, wff, name, head=None):
    T, D = x.shape
    ns, fs = wff.shape[1], wff.shape[2]
    tm = _tile(T, TOK_TILE)
    mc = _tile(tm, FFN_BWD_CHAIN, 16)
    rc = _tile(tm, FFN_BWD_CHAIN)

    def body(*refs):
        x_ref, g_ref, wg_ref, wu_ref, wd_ref = refs[:5]
        if head is None:
            y_ref, a_ref, b_ref, hb_ref, acc_ref = refs[5:]
        else:
            gf_ref, t_ref, y_ref, a_ref, b_ref, loss_ref, dgf_ref, hb_ref, acc_ref = refs[5:]
        j = pl.program_id(1)

        @pl.when(j == 0)
        def _():
            xh, _ = _rms_stats(x_ref[...])
            hb_ref[...] = (xh * g_ref[...]).astype(BF16)
            acc_ref[...] = jnp.zeros_like(acc_ref)

        if head is not None:
            @pl.when((pl.program_id(0) == 0) & (j == 0))
            def _():
                loss_ref[...] = jnp.zeros_like(loss_ref)
                dgf_ref[...] = jnp.zeros_like(dgf_ref)

        for q0 in range(0, tm, mc):
            blk = pl.ds(q0, mc)
            hb = hb_ref[blk, :]
            a = _dot_nt(hb, wg_ref[...])
            b = _dot_nt(hb, wu_ref[...])
            a_ref[blk, :] = a.astype(BF16)
            b_ref[blk, :] = b.astype(BF16)
            p = (a * jax.nn.sigmoid(a) * b).astype(BF16)
            acc_ref[blk, :] += _dot(p, wd_ref[...])

        @pl.when(j == ns - 1)
        def _():
            if head is None:
                y_ref[...] = x_ref[...] + FFN_RES_SCALE * acc_ref[...]
                return
            gv = gf_ref[...]
            loss = jnp.zeros((), F32)
            dg = jnp.zeros((1, D), F32)
            for r0 in range(0, tm, rc):
                rows = pl.ds(r0, rc)
                xh, r = _rms_stats(x_ref[rows, :] + FFN_RES_SCALE * acc_ref[rows, :])
                e = xh * gv - t_ref[rows, :]
                loss = loss + 0.5 * jnp.sum(jnp.mean(e * e, axis=-1, keepdims=True))
                dy = e * (1.0 / D)
                dg = dg + _colsum(dy * xh)
                y_ref[rows, :] = _rms_bwd(dy, xh, r, gv)
            loss_ref[...] += loss
            dgf_ref[...] += dg

    def wspec(n):
        return pl.BlockSpec((None, None, fs, D), lambda i, j: (n, j, 0, 0))

    tok = pl.BlockSpec((tm, D), lambda i, j: (i, 0))
    vec = pl.BlockSpec((1, D), lambda i, j: (0, 0))
    mid = pl.BlockSpec((None, tm, fs), lambda i, j: (j, i, 0))
    in_specs = [tok, vec, wspec(0), wspec(1), wspec(2)]
    out_specs = [tok, mid, mid]
    out_shape = [jax.ShapeDtypeStruct((T, D), F32), jax.ShapeDtypeStruct((ns, T, fs), BF16),
                 jax.ShapeDtypeStruct((ns, T, fs), BF16)]
    args = [x, g, wff, wff, wff]
    if head is not None:
        in_specs += [vec, tok]
        out_specs += [pl.BlockSpec((SUBLANES, LANES), lambda i, j: (0, 0)), vec]
        out_shape += [jax.ShapeDtypeStruct((SUBLANES, LANES), F32), jax.ShapeDtypeStruct((1, D), F32)]
        args += list(head)
    return pl.pallas_call(
        body, grid=(T // tm, ns), in_specs=in_specs, out_specs=out_specs, out_shape=out_shape,
        scratch_shapes=[pltpu.VMEM((tm, D), BF16), pltpu.VMEM((tm, D), F32)],
        compiler_params=_params("arbitrary", "arbitrary"), name=name)(*args)


def _ffn_bwd_tok(dy, x, g, a, b, wff, name):
    T, D = x.shape
    ns, fs = wff.shape[1], wff.shape[2]
    tm = _tile(T, FFN_BWD_TILE)
    rc = _tile(tm, BWD_ROWS)
    mc = _tile(tm, FFN_BWD_CHAIN, rc)

    def body(dy_ref, x_ref, g_ref, a_ref, b_ref, w_ref,
             dx_ref, da_ref, db_ref, p_ref, hb_ref, dyh_ref, dg_ref, dh_ref, dp_ref):
        i, j = pl.program_id(0), pl.program_id(1)
        cur = dp_ref.at[j % 2]
        nxt = dp_ref.at[(j + 1) % 2]
        wg_ref, wu_ref = w_ref.at[0, j], w_ref.at[1, j]
        wd0_ref, wdn_ref = w_ref.at[2, 0], w_ref.at[2, jnp.minimum(j + 1, ns - 1)]

        @pl.when((i == 0) & (j == 0))
        def _():
            dg_ref[...] = jnp.zeros_like(dg_ref)

        @pl.when(j == 0)
        def _():
            for r0 in range(0, tm, rc):
                rows = pl.ds(r0, rc)
                xh, _ = _rms_stats(x_ref[rows, :])
                hb_ref[rows, :] = (xh * g_ref[...]).astype(BF16)
                dyh_ref[rows, :] = (FFN_RES_SCALE * dy_ref[rows, :]).astype(BF16)
            dh_ref[...] = jnp.zeros_like(dh_ref)
            cur[...] = _dot_nt(dyh_ref[...], wd0_ref[...])

        def chains(with_next):
            for q0 in range(0, tm, mc):
                blk = pl.ds(q0, mc)
                for r0 in range(q0, q0 + mc, rc):
                    rows = pl.ds(r0, rc)
                    av = a_ref[rows, :].astype(F32)
                    bv = b_ref[rows, :].astype(F32)
                    dp = cur[rows, :]
                    s = jax.nn.sigmoid(av)
                    sl = av * s
                    da_ref[rows, :] = (dp * bv * (s * (1.0 + av * (1.0 - s)))).astype(BF16)
                    db_ref[rows, :] = (dp * sl).astype(BF16)
                    p_ref[rows, :] = (sl * bv).astype(BF16)
                if with_next:
                    nxt[blk, :] = _dot_nt(dyh_ref[blk, :], wdn_ref[...])
                dh_ref[blk, :] += _dot(da_ref[blk, :], wg_ref[...]) + _dot(db_ref[blk, :], wu_ref[...])

        pl.when(j < ns - 1)(functools.partial(chains, True))
        pl.when(j == ns - 1)(functools.partial(chains, False))

        @pl.when(j == ns - 1)
        def _():
            gv = g_ref[...]
            dg = jnp.zeros((1, D), F32)
            for r0 in range(0, tm, rc):
                rows = pl.ds(r0, rc)
                xh, r = _rms_stats(x_ref[rows, :])
                dh = dh_ref[rows, :]
                dx_ref[rows, :] = dy_ref[rows, :] + _rms_bwd(dh, xh, r, gv)
                dg = dg + _colsum(dh * xh)
            dg_ref[...] += dg

    tok = pl.BlockSpec((tm, D), lambda i, j: (i, 0))
    mid = pl.BlockSpec((None, tm, fs), lambda i, j: (j, i, 0))
    vec = pl.BlockSpec((1, D), lambda i, j: (0, 0))
    return pl.pallas_call(
        body, grid=(T // tm, ns),
        in_specs=[tok, tok, vec, mid, mid,
                  pl.BlockSpec(wff.shape, lambda i, j: (0, 0, 0, 0), pipeline_mode=pl.Buffered(1))],
        out_specs=[tok, mid, mid, mid, tok, tok, vec],
        out_shape=[jax.ShapeDtypeStruct((T, D), F32),
                   jax.ShapeDtypeStruct((ns, T, fs), BF16), jax.ShapeDtypeStruct((ns, T, fs), BF16),
                   jax.ShapeDtypeStruct((ns, T, fs), BF16),
                   jax.ShapeDtypeStruct((T, D), BF16), jax.ShapeDtypeStruct((T, D), BF16),
                   jax.ShapeDtypeStruct((1, D), F32)],
        scratch_shapes=[pltpu.VMEM((tm, D), F32), pltpu.VMEM((2, tm, fs), F32)],
        compiler_params=_params("arbitrary", "arbitrary"), name=name)(dy, x, g, a, b, wff)


def _ffn_wgrad(groups, after, name, swap=False):
    flat = [(l, gi) for gi, (ls, _) in enumerate(groups) for l in ls]
    ng, n = len(groups), len(flat)
    T, D = groups[0][1].shape
    ns, _, fs = flat[0][0].shape
    tm = _tile(T, TOK_TILE)
    nI = T // tm
    rh = fs // 2

    def body(*refs):
        rhs_refs, lhs_refs, out_refs = refs[:ng], refs[ng:ng + n], refs[ng + n + 1:ng + 2 * n + 1]
        j, i = pl.program_id(0), pl.program_id(1)

        @pl.when(i == 0)
        def _():
            for o in out_refs:
                o[...] = jnp.zeros_like(o)

        rvs = [r[...] for r in rhs_refs]
        for l, o, (_, gi) in zip(lhs_refs, out_refs, flat):
            o[...] += _dot_tn(l[...], rvs[gi])

        if swap:
            land_refs = refs[ng + 2 * n + 1:ng + 3 * n + 1]
            send, recv, stage = refs[ng + 3 * n + 1:]
            x, y, c = _here()

            def copies(jj):
                return [pltpu.make_async_remote_copy(
                    src_ref=stage.at[jj % 2, k], dst_ref=land_refs[k].at[jj],
                    send_sem=send.at[k * ns + jj], recv_sem=recv.at[k * ns + jj],
                    device_id=(x, y, 1 - c), device_id_type=MESH) for k in range(n)]

            @pl.when(i == nI - 1)
            def _():
                theirs = pl.ds(pl.multiple_of((1 - c) * rh, SUBLANES), rh)
                for k in range(n):
                    stage[j % 2, k] = out_refs[k][theirs, :]
                for cp in copies(j):
                    cp.start()

            @pl.when((i == nI - 1) & (j > 0))
            def _():
                for cp in copies(j - 1):
                    cp.wait_send()

            @pl.when((i == nI - 1) & (j == ns - 1))
            def _():
                for cp in copies(j):
                    cp.wait_send()
                for jj in range(ns):
                    for cp in copies(jj):
                        cp.wait_recv()

    tok = pl.BlockSpec((tm, D), lambda j, i: (i, 0))
    mid = pl.BlockSpec((None, tm, fs), lambda j, i: (j, i, 0))
    wsp = pl.BlockSpec((None, fs, D), lambda j, i: (j, 0, 0))
    sds = jax.ShapeDtypeStruct((ns, fs, D), F32)
    out_specs, out_shape, scratch = [wsp] * n, [sds] * n, []
    if swap:
        out_specs += [ANY] * n
        out_shape += [jax.ShapeDtypeStruct((ns, rh, D), F32)] * n
        scratch = [pltpu.SemaphoreType.DMA((n * ns,)), pltpu.SemaphoreType.DMA((n * ns,)),
                   pltpu.VMEM((2, n, rh, D), F32)]
    return pl.pallas_call(
        body, grid=(ns, nI),
        in_specs=[tok] * ng + [mid] * n + [pl.BlockSpec((SUBLANES, LANES), lambda j, i: (0, 0))],
        out_specs=out_specs, out_shape=out_shape, scratch_shapes=scratch,
        compiler_params=_params("arbitrary", "arbitrary"), name=name)(
            *[r for _, r in groups], *[l for l, _ in flat], after)


def _mix_in_fwd(x, g, win):
    T, D = x.shape
    ns, ws = win.shape[0], win.shape[2]
    tm = _tile(T, TOK_TILE)

    def body(x_ref, g_ref, w_ref, z_ref):
        xh, _ = _rms_stats(x_ref[...])
        hb = (xh * g_ref[...]).astype(BF16)
        for j in range(ns):
            z_ref[:, pl.ds(j * ws, ws)] = _dot(hb, w_ref[j])

    return pl.pallas_call(
        body, grid=(T // tm,),
        in_specs=[pl.BlockSpec((tm, D), lambda i: (i, 0)), pl.BlockSpec((1, D), lambda i: (0, 0)),
                  pl.BlockSpec((ns, D, ws), lambda i: (0, 0, 0), pipeline_mode=pl.Buffered(1))],
        out_specs=pl.BlockSpec((tm, ns * ws), lambda i: (i, 0)),
        out_shape=jax.ShapeDtypeStruct((T, ns * ws), F32),
        compiler_params=_params("parallel"), name="mix_in_fwd")(x, g, win)


def _tap_sum(buf, w_ref, ntaps, first_row, r0, rows, flip):
    acc = None
    for k in range(ntaps):
        off = (ntaps - 1 - k) if flip else k
        t = buf[pl.ds(first_row + r0 + off, rows), :] * w_ref[pl.ds(k, 1), :]
        acc = t if acc is None else acc + t
    return acc


def _shift_copies(buf, sh, rows):
    for r in range(1, SUBLANES):
        sh[r - 1, pl.ds(0, rows), :] = buf[pl.ds(r, rows), :]


def _tap_rows(buf, sh, off, r0, rows):
    r = off % SUBLANES
    if r == 0:
        return buf[pl.ds(off + r0, rows), :]
    return sh[r - 1, pl.ds(off - r + r0, rows), :]


def _tap_sum_tiles(buf, sh, w_ref, ntaps, first_row, r0, rows, flip):
    acc = None
    for k in range(ntaps):
        off = first_row + ((ntaps - 1 - k) if flip else k)
        t = _tap_rows(buf, sh, off, r0, rows) * w_ref[pl.ds(k, 1), :]
        acc = t if acc is None else acc + t
    return acc


def _conv_fwd(z, w, bias, lng, lnb):
    T = z.shape[0]
    K, C = w.shape
    tm = _tile(T, CONV_TILE, ROW_CHUNK)
    rc = min(ROW_CHUNK, tm)
    srows = tm + CONV_HALO - SUBLANES

    def body(cv_ref, cg_ref, w_ref, b_ref, g_ref, bb_ref, u_ref, u1_ref, buf, sh):
        @pl.when(pl.program_id(0) == 0)
        def _():
            buf[pl.ds(0, CONV_HALO), :] = jnp.zeros((CONV_HALO, C), F32)

        buf[pl.ds(CONV_HALO, tm), :] = cv_ref[...] * jax.nn.sigmoid(cg_ref[...])
        _shift_copies(buf, sh, srows)
        for r0 in range(0, tm, rc):
            u1 = _tap_sum_tiles(buf, sh, w_ref, K, CONV_HALO - (K - 1), r0, rc, False) + b_ref[...]
            u1_ref[pl.ds(r0, rc), :] = u1
            xc = u1 - jnp.mean(u1, axis=-1, keepdims=True)
            xh = xc * lax.rsqrt(jnp.mean(xc * xc, axis=-1, keepdims=True) + LN_EPS)
            u2 = xh * g_ref[...] + bb_ref[...]
            u_ref[pl.ds(r0, rc), :] = (u2 * jax.nn.sigmoid(u2)).astype(BF16)
        buf[pl.ds(0, CONV_HALO), :] = buf[pl.ds(tm, CONV_HALO), :]

    vec = pl.BlockSpec((1, C), lambda i: (0, 0))
    return pl.pallas_call(
        body, grid=(T // tm,),
        in_specs=[pl.BlockSpec((tm, C), lambda i: (i, 0)), pl.BlockSpec((tm, C), lambda i: (i, 1)),
                  pl.BlockSpec((K, C), lambda i: (0, 0)), vec, vec, vec],
        out_specs=[pl.BlockSpec((tm, C), lambda i: (i, 0)), pl.BlockSpec((tm, C), lambda i: (i, 0))],
        out_shape=[jax.ShapeDtypeStruct((T, C), BF16), jax.ShapeDtypeStruct((T, C), F32)],
        scratch_shapes=[pltpu.VMEM((CONV_HALO + tm, C), F32), pltpu.VMEM((SUBLANES - 1, srows, C), F32)],
        compiler_params=_params("arbitrary"), name="conv_fwd")(z, z, w, bias, lng, lnb)


def _conv_bwd(dcat, u1, z, w, lng, lnb):
    T = z.shape[0]
    K, C = w.shape
    tm = _tile(T, CONV_TILE, ROW_CHUNK)
    rc = min(ROW_CHUNK, tm)
    nI = T // tm
    hb = tm // CONV_HALO
    srows = ((K + 4 + SUBLANES - 1) // SUBLANES) * SUBLANES
    shrows = tm + CONV_HALO - SUBLANES

    def body(du_ref, u1_ref, cv_ref, cg_ref, cvp_ref, cgp_ref, w_ref, g_ref, bb_ref,
             dz_ref, st_ref, u0buf, d1buf, ush, dsh):
        i = pl.program_id(0)
        ti = nI - 1 - i

        @pl.when(i == 0)
        def _():
            st_ref[...] = jnp.zeros_like(st_ref)
            d1buf[pl.ds(tm, CONV_HALO), :] = jnp.zeros((CONV_HALO, C), F32)

        prev = cvp_ref[...] * jax.nn.sigmoid(cgp_ref[...])
        u0buf[pl.ds(0, CONV_HALO), :] = jnp.where(ti == 0, 0.0, prev)
        u0buf[pl.ds(CONV_HALO, tm), :] = cv_ref[...] * jax.nn.sigmoid(cg_ref[...])

        gv = g_ref[...]
        dbias = jnp.zeros((1, C), F32)
        dgain = jnp.zeros((1, C), F32)
        dlnb = jnp.zeros((1, C), F32)
        for r0 in range(0, tm, rc):
            u1 = u1_ref[pl.ds(r0, rc), :]
            xc = u1 - jnp.mean(u1, axis=-1, keepdims=True)
            rstd = lax.rsqrt(jnp.mean(xc * xc, axis=-1, keepdims=True) + LN_EPS)
            xh = xc * rstd
            u2 = xh * gv + bb_ref[...]
            s = jax.nn.sigmoid(u2)
            du2 = du_ref[pl.ds(r0, rc), :] * (s * (1.0 + u2 * (1.0 - s)))
            dgain = dgain + _colsum(du2 * xh)
            dlnb = dlnb + _colsum(du2)
            dxh = du2 * gv
            du1 = rstd * (dxh - jnp.mean(dxh, axis=-1, keepdims=True)
                          - xh * jnp.mean(dxh * xh, axis=-1, keepdims=True))
            dbias = dbias + _colsum(du1)
            d1buf[pl.ds(r0, rc), :] = du1
        st_ref[pl.ds(K + 1, 1), :] += dbias
        st_ref[pl.ds(K + 2, 1), :] += dgain
        st_ref[pl.ds(K + 3, 1), :] += dlnb

        _shift_copies(u0buf, ush, shrows)
        _shift_copies(d1buf, dsh, shrows)
        for k in range(K):
            acc = jnp.zeros((SUBLANES, C), F32)
            for r0 in range(0, tm, rc):
                prod = d1buf[pl.ds(r0, rc), :] * _tap_rows(u0buf, ush, CONV_HALO - (K - 1) + k, r0, rc)
                acc = acc + jnp.sum(prod.reshape(rc // SUBLANES, SUBLANES, C), axis=0)
            st_ref[pl.ds(k, 1), :] += _colsum(acc)

        for r0 in range(0, tm, rc):
            du0 = _tap_sum_tiles(d1buf, dsh, w_ref, K, 0, r0, rc, True)
            cv = cv_ref[pl.ds(r0, rc), :]
            sg = jax.nn.sigmoid(cg_ref[pl.ds(r0, rc), :])
            dz_ref[pl.ds(r0, rc), pl.ds(0, C)] = (du0 * sg).astype(BF16)
            dz_ref[pl.ds(r0, rc), pl.ds(C, C)] = (du0 * cv * sg * (1.0 - sg)).astype(BF16)
        d1buf[pl.ds(tm, CONV_HALO), :] = d1buf[pl.ds(0, CONV_HALO), :]

    def rev(col):
        return lambda i: (nI - 1 - i, col)

    def rev_prev(col):
        return lambda i: (jnp.maximum((nI - 1 - i) * hb - 1, 0), col)

    vec = pl.BlockSpec((1, C), lambda i: (0, 0))
    return pl.pallas_call(
        body, grid=(nI,),
        in_specs=[pl.BlockSpec((tm, C), rev(0)), pl.BlockSpec((tm, C), rev(0)),
                  pl.BlockSpec((tm, C), rev(0)), pl.BlockSpec((tm, C), rev(1)),
                  pl.BlockSpec((CONV_HALO, C), rev_prev(0)), pl.BlockSpec((CONV_HALO, C), rev_prev(1)),
                  pl.BlockSpec((K, C), lambda i: (0, 0)), vec, vec],
        out_specs=[pl.BlockSpec((tm, 2 * C), rev(0)), pl.BlockSpec((srows, C), lambda i: (0, 0))],
        out_shape=[jax.ShapeDtypeStruct((T, 2 * C), BF16), jax.ShapeDtypeStruct((srows, C), F32)],
        scratch_shapes=[pltpu.VMEM((CONV_HALO + tm, C), F32), pltpu.VMEM((tm + CONV_HALO, C), F32),
                        pltpu.VMEM((SUBLANES - 1, shrows, C), F32), pltpu.VMEM((SUBLANES - 1, shrows, C), F32)],
        compiler_params=_params("arbitrary"), name="conv_bwd")(dcat, u1, z, z, z, z, w, lng, lnb)


def _softplus(v):
    return jnp.maximum(v, 0.0) + jnp.log(1.0 + jnp.exp(-jnp.abs(v)))


def _gelu(v):
    c = math.sqrt(2.0 / math.pi)
    t = jnp.tanh(c * (v + 0.044715 * v * v * v))
    gl = 0.5 * v * (1.0 + t)
    dgl = 0.5 * (1.0 + t) + 0.5 * v * (1.0 - t * t) * c * (1.0 + 3.0 * 0.044715 * v * v)
    return gl, dgl


def _lru_gates(xr, wa, ba, wx, bx, lam):
    xb = xr.astype(BF16)
    r = jax.nn.sigmoid(_dot(xb, wa) + ba)
    ig = jax.nn.sigmoid(_dot(xb, wx) + bx)
    sp = _softplus(-lam)
    log_a = -LRU_C * r * sp
    a = jnp.exp(log_a)
    y = 2.0 * log_a
    series = -(y * (1.0 + y * (0.5 + y * (1.0 / 6.0 + y * (1.0 / 24.0)))))
    mult = jnp.sqrt(jnp.where(y > -0.02, series, 1.0 - jnp.exp(y)))
    return a, mult, r, ig, sp


def _scan_tile(a_s, b_s, h_s, p_s, carry, seg, reverse):
    hl = [jnp.zeros((SUBLANES, LANES), F32)] * LRU_GROUPS
    pr = [jnp.ones((SUBLANES, LANES), F32)] * LRU_GROUPS
    for n in range(seg):
        for g in range(LRU_GROUPS):
            rows = pl.ds(g * SUBLANES * seg + ((seg - 1 - n) if reverse else n), SUBLANES, stride=seg)
            av = a_s[rows, :]
            hl[g] = av * hl[g] + b_s[rows, :]
            pr[g] = av * pr[g]
            h_s[rows, :] = hl[g]
            p_s[rows, :] = pr[g]
    nseg = SUBLANES * LRU_GROUPS
    cs = [None] * nseg
    c = carry
    for s in (range(nseg - 1, -1, -1) if reverse else range(nseg)):
        g, r = divmod(s, SUBLANES)
        cs[s] = c
        c = hl[g][r:r + 1, :] + pr[g][r:r + 1, :] * c
    return cs, c


def _lru_fwd(z, col0, w4, b4, wa, ba, wx, bx, lam):
    T = z.shape[0]
    K4, W = w4.shape
    nC = W // LANES
    tm = _tile(T, LRU_TILE, SUBLANES * SUBLANES * LRU_GROUPS)
    seg = tm // (SUBLANES * LRU_GROUPS)
    cx, cg = col0 // LANES, (col0 + W) // LANES

    def body(rx_ref, rg_ref, w4_ref, b4_ref, wa_ref, ba_ref, wx_ref, bx_ref, lam_ref,
             yr_ref, hs_ref, xbuf, a_s, b_s, h_s, p_s, hc):
        @pl.when(pl.program_id(1) == 0)
        def _():
            xbuf[pl.ds(0, LRU_HALO), :] = jnp.zeros((LRU_HALO, LANES), F32)
            hc[...] = jnp.zeros_like(hc)

        xbuf[pl.ds(LRU_HALO, tm), :] = rx_ref[...]
        xr = _tap_sum(xbuf, w4_ref, K4, LRU_HALO - (K4 - 1), 0, tm, False) + b4_ref[...]
        a, mult, _, ig, _ = _lru_gates(xr, wa_ref[...], ba_ref[...], wx_ref[...], bx_ref[...], lam_ref[...])
        a_s[...] = a
        b_s[...] = mult * ig * xr
        cs, cout = _scan_tile(a_s, b_s, h_s, p_s, hc[pl.ds(0, 1), :], seg, False)
        hc[pl.ds(0, 1), :] = cout
        for s in range(SUBLANES * LRU_GROUPS):
            rows = pl.ds(s * seg, seg)
            h = h_s[rows, :] + p_s[rows, :] * cs[s]
            hs_ref[rows, :] = h
            gl, _ = _gelu(rg_ref[rows, :])
            yr_ref[rows, :] = (h * gl).astype(BF16)
        xbuf[pl.ds(0, LRU_HALO), :] = xbuf[pl.ds(tm, LRU_HALO), :]

    vec = pl.BlockSpec((1, LANES), lambda c, i: (0, c))
    mat = pl.BlockSpec((None, LANES, LANES), lambda c, i: (c, 0, 0))
    return pl.pallas_call(
        body, grid=(nC, T // tm),
        in_specs=[pl.BlockSpec((tm, LANES), lambda c, i: (i, cx + c)),
                  pl.BlockSpec((tm, LANES), lambda c, i: (i, cg + c)),
                  pl.BlockSpec((K4, LANES), lambda c, i: (0, c)), vec, mat, vec, mat, vec, vec],
        out_specs=[pl.BlockSpec((tm, LANES), lambda c, i: (i, c)), pl.BlockSpec((tm, LANES), lambda c, i: (i, c))],
        out_shape=[jax.ShapeDtypeStruct((T, W), BF16), jax.ShapeDtypeStruct((T, W), F32)],
        scratch_shapes=[pltpu.VMEM((LRU_HALO + tm, LANES), F32)] + [pltpu.VMEM((tm, LANES), F32)] * 4
        + [pltpu.VMEM((SUBLANES, LANES), F32)],
        compiler_params=_params("parallel", "arbitrary"), name="lru_fwd")(z, z, w4, b4, wa, ba, wx, bx, lam)


def _lru_bwd(dcat, dcol0, hs, z, col0, w4, b4, wa, ba, wx, bx, lam):
    T = z.shape[0]
    K4, W = w4.shape
    assert K4 + 4 == SUBLANES
    nC = W // LANES
    tm = _tile(T, LRU_TILE, SUBLANES * SUBLANES * LRU_GROUPS)
    seg = tm // (SUBLANES * LRU_GROUPS)
    nI = T // tm
    hb = tm // LRU_HALO
    cx, cg, cd = col0 // LANES, (col0 + W) // LANES, dcol0 // LANES

    def body(dyr_ref, hs_ref, hsp_ref, rx_ref, rxp_ref, rg_ref, w4_ref, b4_ref, wa_ref, ba_ref, wx_ref, bx_ref,
             lam_ref, dzx_ref, dzg_ref, st_ref, dwa_ref, dwx_ref, xbuf, hbuf, abuf, a_s, b_s, h_s, p_s, dbuf, gc, anc):
        i = pl.program_id(1)
        ti = nI - 1 - i

        @pl.when(i == 0)
        def _():
            st_ref[...] = jnp.zeros_like(st_ref)
            dwa_ref[...] = jnp.zeros_like(dwa_ref)
            dwx_ref[...] = jnp.zeros_like(dwx_ref)
            gc[...] = jnp.zeros_like(gc)
            anc[...] = jnp.zeros_like(anc)
            dbuf[pl.ds(tm, LRU_HALO), :] = jnp.zeros((LRU_HALO, LANES), F32)

        xbuf[pl.ds(0, LRU_HALO), :] = jnp.where(ti == 0, 0.0, rxp_ref[...])
        xbuf[pl.ds(LRU_HALO, tm), :] = rx_ref[...]
        hbuf[pl.ds(0, LRU_HALO), :] = jnp.where(ti == 0, 0.0, hsp_ref[...])
        hbuf[pl.ds(LRU_HALO, tm), :] = hs_ref[...]

        wa, wx = wa_ref[...], wx_ref[...]
        lam_v = lam_ref[...]
        xr = _tap_sum(xbuf, w4_ref, K4, LRU_HALO - (K4 - 1), 0, tm, False) + b4_ref[...]
        a, mult, r, ig, sp = _lru_gates(xr, wa, ba_ref[...], wx, bx_ref[...], lam_v)

        dyr = dyr_ref[...]
        gl, dgl = _gelu(rg_ref[...])
        dzg_ref[...] = (dyr * hs_ref[...] * dgl).astype(BF16)

        abuf[pl.ds(0, tm), :] = a
        abuf[pl.ds(tm, LRU_HALO), :] = anc[...]
        a_s[...] = abuf[pl.ds(1, tm), :]
        b_s[...] = dyr * gl
        cs, cout = _scan_tile(a_s, b_s, h_s, p_s, gc[pl.ds(0, 1), :], seg, True)
        gc[pl.ds(0, 1), :] = cout
        anc[pl.ds(0, 1), :] = a[0:1, :]
        for s in range(SUBLANES * LRU_GROUPS):
            rows = pl.ds(s * seg, seg)
            b_s[rows, :] = h_s[rows, :] + p_s[rows, :] * cs[s]
        g = b_s[...]

        d_a = g * hbuf[pl.ds(LRU_HALO - 1, tm), :]
        gx_ = g * xr
        d_log_a = d_a * a - (gx_ * ig) * (a * a / mult)
        dga = (d_log_a * (-LRU_C * sp)) * r * (1.0 - r)
        dgx = (gx_ * mult) * ig * (1.0 - ig)
        dga_b, dgx_b = dga.astype(BF16), dgx.astype(BF16)
        dxr = g * mult * ig + _dot_nt(dga_b, wa) + _dot_nt(dgx_b, wx)
        xb = xr.astype(BF16)
        dwa_ref[...] += _dot_tn(xb, dga_b)
        dwx_ref[...] += _dot_tn(xb, dgx_b)
        st_ref[pl.ds(K4, 1), :] += _colsum(dxr)
        st_ref[pl.ds(K4 + 1, 1), :] += _colsum(dga)
        st_ref[pl.ds(K4 + 2, 1), :] += _colsum(dgx)
        st_ref[pl.ds(K4 + 3, 1), :] += _colsum(d_log_a * (-LRU_C * r)) * (-jax.nn.sigmoid(-lam_v))

        dbuf[pl.ds(0, tm), :] = dxr
        for k in range(K4):
            st_ref[pl.ds(k, 1), :] += _colsum(dxr * xbuf[pl.ds(LRU_HALO - (K4 - 1) + k, tm), :])
        dzx_ref[...] = _tap_sum(dbuf, w4_ref, K4, 0, 0, tm, True).astype(BF16)
        dbuf[pl.ds(tm, LRU_HALO), :] = dbuf[pl.ds(0, LRU_HALO), :]

    def rev(col):
        return lambda c, i: (nI - 1 - i, col + c)

    def rev_prev(col):
        return lambda c, i: (jnp.maximum((nI - 1 - i) * hb - 1, 0), col + c)

    vec = pl.BlockSpec((1, LANES), lambda c, i: (0, c))
    mat = pl.BlockSpec((None, LANES, LANES), lambda c, i: (c, 0, 0))
    big = pltpu.VMEM((tm, LANES), F32)
    halo = pltpu.VMEM((tm + LRU_HALO, LANES), F32)
    return pl.pallas_call(
        body, grid=(nC, nI),
        in_specs=[pl.BlockSpec((tm, LANES), rev(cd)),
                  pl.BlockSpec((tm, LANES), rev(0)), pl.BlockSpec((LRU_HALO, LANES), rev_prev(0)),
                  pl.BlockSpec((tm, LANES), rev(cx)), pl.BlockSpec((LRU_HALO, LANES), rev_prev(cx)),
                  pl.BlockSpec((tm, LANES), rev(cg)),
                  pl.BlockSpec((K4, LANES), lambda c, i: (0, c)), vec, mat, vec, mat, vec, vec],
        out_specs=[pl.BlockSpec((tm, LANES), rev(0)), pl.BlockSpec((tm, LANES), rev(0)),
                   pl.BlockSpec((SUBLANES, LANES), lambda c, i: (0, c)), mat, mat],
        out_shape=[jax.ShapeDtypeStruct((T, W), BF16), jax.ShapeDtypeStruct((T, W), BF16),
                   jax.ShapeDtypeStruct((SUBLANES, W), F32),
                   jax.ShapeDtypeStruct((nC, LANES, LANES), F32), jax.ShapeDtypeStruct((nC, LANES, LANES), F32)],
        scratch_shapes=[halo, halo, halo, big, big, big, big, halo,
                        pltpu.VMEM((SUBLANES, LANES), F32), pltpu.VMEM((SUBLANES, LANES), F32)],
        compiler_params=_params("parallel", "arbitrary"), name="lru_bwd")(
            dcat, hs, hs, z, z, z, w4, b4, wa, ba, wx, bx, lam)


def _mix_out_fwd(x, u, yr, wout):
    T, D = x.shape
    C, W = u.shape[1], yr.shape[1]
    tm = _tile(T, TOK_TILE)

    def body(x_ref, u_ref, yr_ref, w_ref, y_ref):
        y_ref[...] = (x_ref[...] + _dot(u_ref[...], w_ref[pl.ds(0, C), :])
                      + _dot(yr_ref[...], w_ref[pl.ds(C, W), :]))

    return pl.pallas_call(
        body, grid=(T // tm,),
        in_specs=[pl.BlockSpec((tm, D), lambda i: (i, 0)), pl.BlockSpec((tm, C), lambda i: (i, 0)),
                  pl.BlockSpec((tm, W), lambda i: (i, 0)),
                  pl.BlockSpec((C + W, D), lambda i: (0, 0), pipeline_mode=pl.Buffered(1))],
        out_specs=pl.BlockSpec((tm, D), lambda i: (i, 0)),
        out_shape=jax.ShapeDtypeStruct((T, D), F32),
        compiler_params=_params("parallel"), name="mix_out_fwd")(x, u, yr, wout)


def _mix_out_bwd(dy, u, yr, wout):
    T, D = dy.shape
    C, W = u.shape[1], yr.shape[1]
    tm = _tile(T, BWD_TILE)

    def body(dy_ref, u_ref, yr_ref, w_ref, dcat_ref, dw_ref):
        @pl.when(pl.program_id(0) == 0)
        def _():
            dw_ref[...] = jnp.zeros_like(dw_ref)

        dyb = dy_ref[...].astype(BF16)
        dcat_ref[...] = _dot_nt(dyb, w_ref[...])
        dw_ref[pl.ds(0, C), :] += _dot_tn(u_ref[...], dyb)
        dw_ref[pl.ds(C, W), :] += _dot_tn(yr_ref[...], dyb)

    return pl.pallas_call(
        body, grid=(T // tm,),
        in_specs=[pl.BlockSpec((tm, D), lambda i: (i, 0)), pl.BlockSpec((tm, C), lambda i: (i, 0)),
                  pl.BlockSpec((tm, W), lambda i: (i, 0)),
                  pl.BlockSpec((C + W, D), lambda i: (0, 0), pipeline_mode=pl.Buffered(1))],
        out_specs=[pl.BlockSpec((tm, C + W), lambda i: (i, 0)), pl.BlockSpec((C + W, D), lambda i: (0, 0))],
        out_shape=[jax.ShapeDtypeStruct((T, C + W), F32), jax.ShapeDtypeStruct((C + W, D), F32)],
        compiler_params=_params("arbitrary"), name="mix_out_bwd")(dy, u, yr, wout)


def _mix_in_bwd(dzc, dzx, dzg, x, dy, g, win):
    T, D = x.shape
    ns, ws = win.shape[0], win.shape[2]
    tm = _tile(T, BWD_TILE)
    parts = []
    for j in range(ns):
        lo = j * ws
        if lo < dzc.shape[1]:
            parts.append((0, lo))
        elif lo < dzc.shape[1] + dzx.shape[1]:
            parts.append((1, lo - dzc.shape[1]))
        else:
            parts.append((2, lo - dzc.shape[1] - dzx.shape[1]))

    def body(dzc_ref, dzx_ref, dzg_ref, x_ref, dy_ref, g_ref, w_ref, dx_ref, dw_ref, dg_ref):
        @pl.when(pl.program_id(0) == 0)
        def _():
            dw_ref[...] = jnp.zeros_like(dw_ref)
            dg_ref[...] = jnp.zeros_like(dg_ref)

        xh, r = _rms_stats(x_ref[...])
        gv = g_ref[...]
        hb = (xh * gv).astype(BF16)
        srcs = (dzc_ref, dzx_ref, dzg_ref)
        dh = jnp.zeros((tm, D), F32)
        for j, (si, off) in enumerate(parts):
            dzj = srcs[si][:, pl.ds(off, ws)]
            dh = dh + _dot_nt(dzj, w_ref[j])
            dw_ref[j] += _dot_tn(hb, dzj)
        dx_ref[...] = dy_ref[...] + _rms_bwd(dh, xh, r, gv)
        dg_ref[...] += _colsum(dh * xh)

    def tok(n):
        return pl.BlockSpec((tm, n), lambda i: (i, 0))

    vec = pl.BlockSpec((1, D), lambda i: (0, 0))
    return pl.pallas_call(
        body, grid=(T // tm,),
        in_specs=[tok(dzc.shape[1]), tok(dzx.shape[1]), tok(dzg.shape[1]), tok(D), tok(D), vec,
                  pl.BlockSpec((ns, D, ws), lambda i: (0, 0, 0), pipeline_mode=pl.Buffered(1))],
        out_specs=[tok(D), pl.BlockSpec((ns, D, ws), lambda i: (0, 0, 0)), vec],
        out_shape=[jax.ShapeDtypeStruct((T, D), F32), jax.ShapeDtypeStruct((ns, D, ws), F32),
                   jax.ShapeDtypeStruct((1, D), F32)],
        compiler_params=_params("arbitrary"), name="mix_in_bwd")(dzc, dzx, dzg, x, dy, g, win)


def _adamw(w, g, m, v, name):
    R, Cc = w.shape
    tr = _tile(R, max(SUBLANES, (1 << 19) // Cc))
    c1 = 1.0 - ADAM_B1 ** ADAM_STEP
    c2 = 1.0 - ADAM_B2 ** ADAM_STEP

    def body(w_ref, g_ref, m_ref, v_ref, d_ref, nm_ref, nv_ref):
        gv = g_ref[...]
        nm = ADAM_B1 * m_ref[...] + (1.0 - ADAM_B1) * gv
        nv = ADAM_B2 * v_ref[...] + (1.0 - ADAM_B2) * (gv * gv)
        nm_ref[...] = nm
        nv_ref[...] = nv
        d_ref[...] = -ADAM_LR * ((nm / c1) / (jnp.sqrt(nv / c2) + ADAM_EPS) + ADAM_WD * w_ref[...])

    blk = pl.BlockSpec((tr, Cc), lambda i: (i, 0))
    sds = jax.ShapeDtypeStruct((R, Cc), F32)
    return pl.pallas_call(
        body, grid=(R // tr,), in_specs=[blk] * 4, out_specs=[blk] * 3, out_shape=[sds] * 3,
        compiler_params=_params("parallel"), name=name)(w, g, m, v)


def _here():
    return lax.axis_index("x"), lax.axis_index("y"), lax.axis_index("c")


def _chip_at(x, y, m):
    return x ^ (m >> 1), y ^ (m & 1)


ANY = pl.BlockSpec(memory_space=pl.ANY)


def _place_cast(srcs, idx, dtype, name):
    n = len(srcs)
    R, Cc = srcs[0].shape
    tr = _tile(R, max(16, (1 << 18) // Cc), 16)

    def body(i_ref, *refs):
        o_ref = refs[n]
        for k in range(n):
            o_ref[k] = refs[k][...].astype(dtype)

    blk = pl.BlockSpec((tr, Cc), lambda i, s: (i, 0))
    return pl.pallas_call(
        body,
        grid_spec=pltpu.PrefetchScalarGridSpec(
            num_scalar_prefetch=1, grid=(R // tr,), in_specs=[blk] * n,
            out_specs=pl.BlockSpec((n, None, tr, Cc), lambda i, s: (0, s[1], i, 0))),
        out_shape=jax.ShapeDtypeStruct((n, N_CHIPS, R, Cc), dtype),
        compiler_params=_params("parallel"), name=name)(idx, *srcs)


def _gather_weights(lands):
    n = len(lands)

    def body(*refs):
        outs = refs[n:2 * n]
        send1, recv1, send2, recv2 = refs[2 * n:]
        x, y, c = _here()
        own = 2 * x + y

        def half(ref, chip, cc):
            rh = ref.shape[-2] // 2
            lead = (slice(None),) * (len(ref.shape) - 3)
            return ref.at[lead + (chip, pl.ds(cc * rh, rh), slice(None))]

        first = []
        for k in range(n):
            for m in (1, 2, 3):
                px, py = _chip_at(x, y, m)
                cp = pltpu.make_async_remote_copy(
                    src_ref=half(outs[k], own, c), dst_ref=half(outs[k], own, c),
                    send_sem=send1.at[k, m - 1], recv_sem=recv1.at[k, m - 1],
                    device_id=(px, py, c), device_id_type=MESH)
                cp.start()
                first.append(cp)

        passed = []
        for k in range(n):
            for m in (1, 2, 3):
                px, py = _chip_at(x, y, m)
                peer = 2 * px + py
                got = half(outs[k], peer, c)
                pltpu.make_async_remote_copy(
                    src_ref=got, dst_ref=got, send_sem=send1.at[k, m - 1], recv_sem=recv1.at[k, m - 1],
                    device_id=(px, py, c), device_id_type=MESH).wait_recv()
                cp = pltpu.make_async_remote_copy(
                    src_ref=got, dst_ref=got, send_sem=send2.at[k, m - 1], recv_sem=recv2.at[k, m - 1],
                    device_id=(x, y, 1 - c), device_id_type=MESH)
                cp.start()
                passed.append(cp)

        for k in range(n):
            for m in (1, 2, 3):
                px, py = _chip_at(x, y, m)
                other = half(outs[k], 2 * px + py, 1 - c)
                pltpu.make_async_remote_copy(
                    src_ref=other, dst_ref=other, send_sem=send2.at[k, m - 1], recv_sem=recv2.at[k, m - 1],
                    device_id=(x, y, 1 - c), device_id_type=MESH).wait_recv()
        for cp in first + passed:
            cp.wait_send()

    return pl.pallas_call(
        body, in_specs=[ANY] * n, out_specs=[ANY] * n,
        out_shape=[jax.ShapeDtypeStruct(a.shape, a.dtype) for a in lands],
        input_output_aliases={k: k for k in range(n)},
        scratch_shapes=[pltpu.SemaphoreType.DMA((n, 3)), pltpu.SemaphoreType.DMA((n, 3)),
                        pltpu.SemaphoreType.DMA((n, 3)), pltpu.SemaphoreType.DMA((n, 3))],
        name="gather_weights")(*lands)


HBM = pl.BlockSpec(memory_space=pltpu.HBM)
SEM = pl.BlockSpec(memory_space=pltpu.SEMAPHORE)
EFFECT = pltpu.SideEffectType.DATAFLOW_SIDE_EFFECTING


def _in_hbm(a):
    return pltpu.with_memory_space_constraint(a, pltpu.HBM)


def _gather_copies(land_refs, send, recv):
    x, y, c = _here()
    own = 2 * x + y
    cps = []
    for k in range(len(land_refs)):
        lead = (slice(None),) * (len(land_refs[k].shape) - 3)
        mine = land_refs[k].at[lead + (own,)]
        for m in (1, 2, 3):
            px, py = _chip_at(x, y, m)
            cps.append(pltpu.make_async_remote_copy(
                src_ref=mine, dst_ref=mine, send_sem=send.at[3 * k + m - 1], recv_sem=recv.at[3 * k + m - 1],
                device_id=(px, py, c), device_id_type=MESH))
    return cps


def _gather_start(lands, after, name):
    n = len(lands)

    def body(*refs):
        lz = refs[:n]
        send, recv = refs[n + 1], refs[n + 2]
        token = refs[-1]
        for cp in _gather_copies(lz, send, recv):
            cp.start()
        token[...] = jnp.zeros_like(token)

    hbm = [pltpu.HBM(a.shape, a.dtype) for a in lands]
    outs = pl.pallas_call(
        body, name=name,
        in_specs=[HBM] * n + [ANY],
        out_specs=[SEM, SEM] + [HBM] * n + [pl.BlockSpec(memory_space=pltpu.VMEM)],
        out_shape=[pltpu.SemaphoreType.DMA((3 * n,)), pltpu.SemaphoreType.DMA((3 * n,))] + hbm
        + [jax.ShapeDtypeStruct((SUBLANES, LANES), F32)],
        input_output_aliases={k: 2 + k for k in range(n)},
        compiler_params=pltpu.CompilerParams(has_side_effects=EFFECT),
    )(*[_in_hbm(a) for a in lands], after)
    return outs[0], outs[1], outs[2:2 + n], outs[-1]


def _gather_wait(send, recv, lands, after, name):
    n = len(lands)

    def body(*refs):
        lz = refs[:n]
        send_r, recv_r = refs[n], refs[n + 1]
        for cp in _gather_copies(lz, send_r, recv_r):
            cp.wait_send()
            cp.wait_recv()

    hbm = [pltpu.HBM(a.shape, a.dtype) for a in lands]
    return pl.pallas_call(
        body, name=name,
        in_specs=[HBM] * n + [SEM, SEM, ANY],
        out_specs=[HBM] * n, out_shape=hbm,
        input_output_aliases={k: k for k in range(n)},
        compiler_params=pltpu.CompilerParams(has_side_effects=EFFECT),
    )(*lands, send, recv, after)


def _exchange_copies(part_refs, slot_refs, send, recv):
    x, y, c = _here()
    cps = []
    for k in range(len(part_refs)):
        for m in (1, 2, 3):
            px, py = _chip_at(x, y, m)
            cps.append(pltpu.make_async_remote_copy(
                src_ref=part_refs[k].at[2 * px + py], dst_ref=slot_refs[k].at[m - 1],
                send_sem=send.at[3 * k + m - 1], recv_sem=recv.at[3 * k + m - 1],
                device_id=(px, py, c), device_id_type=MESH))
    return cps


def _exchange_start(parts, name):
    n = len(parts)
    lands = [lax.empty((N_CHIPS - 1,) + p.shape[1:], p.dtype) for p in parts]

    def body(*refs):
        ins, lz = refs[:n], refs[n:2 * n]
        send, recv = refs[2 * n], refs[2 * n + 1]
        token = refs[-1]
        for cp in _exchange_copies(ins, lz, send, recv):
            cp.start()
        token[...] = jnp.zeros_like(token)

    hbm = [pltpu.HBM(a.shape, a.dtype) for a in list(parts) + lands]
    outs = pl.pallas_call(
        body, name=name,
        in_specs=[HBM] * (2 * n),
        out_specs=[SEM, SEM] + [HBM] * (2 * n) + [pl.BlockSpec(memory_space=pltpu.VMEM)],
        out_shape=[pltpu.SemaphoreType.DMA((3 * n,)), pltpu.SemaphoreType.DMA((3 * n,))] + hbm
        + [jax.ShapeDtypeStruct((SUBLANES, LANES), F32)],
        input_output_aliases={k: 2 + k for k in range(2 * n)},
        compiler_params=pltpu.CompilerParams(has_side_effects=EFFECT),
    )(*[_in_hbm(a) for a in parts], *[_in_hbm(a) for a in lands])
    return outs[0], outs[1], outs[2:2 + n], outs[2 + n:2 + 2 * n], outs[-1]


def _exchange_wait(send, recv, parts, lands, after, name):
    n = len(parts)

    def body(*refs):
        ins, lz = refs[:n], refs[n:2 * n]
        send_r, recv_r = refs[2 * n], refs[2 * n + 1]
        for cp in _exchange_copies(ins, lz, send_r, recv_r):
            cp.wait_send()
            cp.wait_recv()

    hbm = [pltpu.HBM(a.shape, a.dtype) for a in list(parts) + list(lands)]
    outs = pl.pallas_call(
        body, name=name,
        in_specs=[HBM] * (2 * n) + [SEM, SEM, ANY],
        out_specs=[HBM] * (2 * n), out_shape=hbm,
        input_output_aliases={k: k for k in range(2 * n)},
        compiler_params=pltpu.CompilerParams(has_side_effects=EFFECT),
    )(*parts, *lands, send, recv, after)
    return outs[:n], outs[n:]


def _swap_halves_out(grads, name):
    n = len(grads)
    out_shapes = [jax.ShapeDtypeStruct((g.shape[0], g.shape[1] // 2, g.shape[2]), g.dtype) for g in grads]

    def body(*refs):
        ins, outs = refs[:n], refs[n:2 * n]
        send, recv = refs[2 * n:]
        x, y, c = _here()
        cps = []
        for k in range(n):
            rh = ins[k].shape[1] // 2
            cp = pltpu.make_async_remote_copy(
                src_ref=ins[k].at[:, pl.ds((1 - c) * rh, rh), :], dst_ref=outs[k],
                send_sem=send.at[k], recv_sem=recv.at[k], device_id=(x, y, 1 - c), device_id_type=MESH)
            cp.start()
            cps.append(cp)
        for cp in cps:
            cp.wait()

    return pl.pallas_call(
        body, in_specs=[ANY] * n, out_specs=[ANY] * n, out_shape=out_shapes,
        scratch_shapes=[pltpu.SemaphoreType.DMA((n,)), pltpu.SemaphoreType.DMA((n,))],
        name=name)(*grads)


def _swap_copies(grad_refs, land_refs, send, recv):
    x, y, c = _here()
    cps = []
    for k in range(len(grad_refs)):
        rh = grad_refs[k].shape[1] // 2
        cps.append(pltpu.make_async_remote_copy(
            src_ref=grad_refs[k].at[:, pl.ds((1 - c) * rh, rh), :], dst_ref=land_refs[k],
            send_sem=send.at[k], recv_sem=recv.at[k], device_id=(x, y, 1 - c), device_id_type=MESH))
    return cps


def _swap_start(grads, name):
    n = len(grads)
    lands = [lax.empty((g.shape[0], g.shape[1] // 2, g.shape[2]), g.dtype) for g in grads]

    def body(*refs):
        ins, lz = refs[:n], refs[n:2 * n]
        send, recv = refs[2 * n], refs[2 * n + 1]
        token = refs[-1]
        for cp in _swap_copies(ins, lz, send, recv):
            cp.start()
        token[...] = jnp.zeros_like(token)

    hbm = [pltpu.HBM(a.shape, a.dtype) for a in list(grads) + lands]
    outs = pl.pallas_call(
        body, name=name,
        in_specs=[HBM] * (2 * n),
        out_specs=[SEM, SEM] + [HBM] * (2 * n) + [pl.BlockSpec(memory_space=pltpu.VMEM)],
        out_shape=[pltpu.SemaphoreType.DMA((n,)), pltpu.SemaphoreType.DMA((n,))] + hbm
        + [jax.ShapeDtypeStruct((SUBLANES, LANES), F32)],
        input_output_aliases={k: 2 + k for k in range(2 * n)},
        compiler_params=pltpu.CompilerParams(has_side_effects=EFFECT),
    )(*[_in_hbm(a) for a in grads], *[_in_hbm(a) for a in lands])
    return outs[0], outs[1], outs[2:2 + n], outs[2 + n:2 + 2 * n], outs[-1]


def _swap_wait(send, recv, grads, lands, after, name):
    n = len(grads)

    def body(*refs):
        ins, lz = refs[:n], refs[n:2 * n]
        send_r, recv_r = refs[2 * n], refs[2 * n + 1]
        for cp in _swap_copies(ins, lz, send_r, recv_r):
            cp.wait_send()
            cp.wait_recv()

    hbm = [pltpu.HBM(a.shape, a.dtype) for a in list(grads) + list(lands)]
    outs = pl.pallas_call(
        body, name=name,
        in_specs=[HBM] * (2 * n) + [SEM, SEM, ANY],
        out_specs=[HBM] * (2 * n), out_shape=hbm,
        input_output_aliases={k: k for k in range(2 * n)},
        compiler_params=pltpu.CompilerParams(has_side_effects=EFFECT),
    )(*grads, *lands, send, recv, after)
    return outs[:n], outs[n:]


def _add_cast(g, other, cidx, name):
    ns, R, Cc = g.shape
    rh = R // 2
    tr = _tile(rh, max(16, (1 << 19) // Cc), 16)
    nb = rh // tr

    def body(c_ref, g_ref, o_ref, s_ref):
        s_ref[...] = (g_ref[...] + o_ref[...]).astype(BF16)

    return pl.pallas_call(
        body,
        grid_spec=pltpu.PrefetchScalarGridSpec(
            num_scalar_prefetch=1, grid=(ns, nb),
            in_specs=[pl.BlockSpec((None, tr, Cc), lambda k, i, c: (k, c[0] * nb + i, 0)),
                      pl.BlockSpec((None, tr, Cc), lambda k, i, c: (k, i, 0))],
            out_specs=pl.BlockSpec((None, tr, Cc), lambda k, i, c: (k, i, 0))),
        out_shape=jax.ShapeDtypeStruct((ns, rh, Cc), BF16),
        compiler_params=_params("parallel", "parallel"), name=name)(cidx, g, other)


def _sum_slots(part, got, idx, name):
    ns, rh, Cc = got.shape
    tr = _tile(rh, max(16, (1 << 18) // Cc), 16)
    nb = rh // tr

    def body(i_ref, p_ref, b_ref, o_ref):
        acc = p_ref[...].astype(F32)
        for m in range(ns):
            acc = acc + b_ref[m].astype(F32)
        o_ref[...] = acc

    return pl.pallas_call(
        body,
        grid_spec=pltpu.PrefetchScalarGridSpec(
            num_scalar_prefetch=1, grid=(nb,),
            in_specs=[pl.BlockSpec((None, tr, Cc), lambda i, s: (s[1], i, 0)),
                      pl.BlockSpec((ns, tr, Cc), lambda i, s: (0, i, 0))],
            out_specs=pl.BlockSpec((tr, Cc), lambda i, s: (s[0] * nb + i, 0))),
        out_shape=jax.ShapeDtypeStruct((2 * rh, Cc), F32),
        compiler_params=_params("parallel"), name=name)(idx, part, got)


def _share_copies(block_refs, send, recv):
    x, y, c = _here()
    cps = []
    for k, ref in enumerate(block_refs):
        rh = ref.shape[0] // 2
        mine = ref.at[pl.ds(c * rh, rh), :]
        cps.append(pltpu.make_async_remote_copy(
            src_ref=mine, dst_ref=mine, send_sem=send.at[k], recv_sem=recv.at[k],
            device_id=(x, y, 1 - c), device_id_type=MESH))
    return cps


def _share_start(blocks, name):
    n = len(blocks)

    def body(*refs):
        send, recv = refs[n], refs[n + 1]
        token = refs[-1]
        for cp in _share_copies(refs[:n], send, recv):
            cp.start()
        token[...] = jnp.zeros_like(token)

    hbm = [pltpu.HBM(a.shape, a.dtype) for a in blocks]
    outs = pl.pallas_call(
        body, name=name,
        in_specs=[HBM] * n,
        out_specs=[SEM, SEM] + [HBM] * n + [pl.BlockSpec(memory_space=pltpu.VMEM)],
        out_shape=[pltpu.SemaphoreType.DMA((n,)), pltpu.SemaphoreType.DMA((n,))] + hbm
        + [jax.ShapeDtypeStruct((SUBLANES, LANES), F32)],
        input_output_aliases={k: 2 + k for k in range(n)},
        compiler_params=pltpu.CompilerParams(has_side_effects=EFFECT),
    )(*[_in_hbm(a) for a in blocks])
    return outs[0], outs[1], outs[2:2 + n], outs[-1]


def _share_wait(send, recv, blocks, after, name):
    n = len(blocks)

    def body(*refs):
        for cp in _share_copies(refs[:n], refs[n], refs[n + 1]):
            cp.wait_send()
            cp.wait_recv()

    return pl.pallas_call(
        body, name=name,
        in_specs=[HBM] * n + [SEM, SEM, ANY],
        out_specs=[HBM] * n, out_shape=[pltpu.HBM(a.shape, a.dtype) for a in blocks],
        input_output_aliases={k: k for k in range(n)},
        compiler_params=pltpu.CompilerParams(has_side_effects=EFFECT),
    )(*blocks, send, recv, after)


def _share_halves(blocks, name):
    n = len(blocks)

    def body(*refs):
        outs = refs[n:2 * n]
        send, recv = refs[2 * n:]
        cps = _share_copies(outs, send, recv)
        for cp in cps:
            cp.start()
        for cp in cps:
            cp.wait()

    return pl.pallas_call(
        body, in_specs=[ANY] * n, out_specs=[ANY] * n,
        out_shape=[jax.ShapeDtypeStruct(b.shape, b.dtype) for b in blocks],
        input_output_aliases={k: k for k in range(n)},
        scratch_shapes=[pltpu.SemaphoreType.DMA((n,)), pltpu.SemaphoreType.DMA((n,))],
        name=name)(*blocks)


def _small_copies(p_ref, slot_ref, send, recv):
    x, y, c = _here()
    mine = slot_ref.at[4 * x + 2 * y + c]
    cps = []
    for m in range(1, N_DEV):
        peer = (x ^ (m >> 2), y ^ ((m >> 1) & 1), c ^ (m & 1))
        cps.append(pltpu.make_async_remote_copy(
            src_ref=p_ref, dst_ref=mine, send_sem=send.at[m - 1], recv_sem=recv.at[m - 1],
            device_id=peer, device_id_type=MESH))
    return cps


def _small_start(packed):
    slots = lax.empty((N_DEV,) + packed.shape, packed.dtype)

    def body(p_ref, s_ref, send, recv, p_thru, s_thru, token):
        for cp in _small_copies(p_ref, s_ref, send, recv):
            cp.start()
        token[...] = jnp.zeros_like(token)

    return pl.pallas_call(
        body, name="small_start",
        in_specs=[HBM, HBM],
        out_specs=[SEM, SEM, HBM, HBM, pl.BlockSpec(memory_space=pltpu.VMEM)],
        out_shape=[pltpu.SemaphoreType.DMA((N_DEV - 1,)), pltpu.SemaphoreType.DMA((N_DEV - 1,)),
                   pltpu.HBM(packed.shape, packed.dtype), pltpu.HBM(slots.shape, slots.dtype),
                   jax.ShapeDtypeStruct((SUBLANES, LANES), F32)],
        input_output_aliases={0: 2, 1: 3},
        compiler_params=pltpu.CompilerParams(has_side_effects=EFFECT),
    )(_in_hbm(packed), _in_hbm(slots))


def _small_wait(send, recv, packed, slots, after):
    def body(p_ref, s_ref, send_r, recv_r, after_ref, p_out, s_out):
        for cp in _small_copies(p_ref, s_ref, send_r, recv_r):
            cp.wait_send()
            cp.wait_recv()

    return pl.pallas_call(
        body, name="small_wait",
        in_specs=[HBM, HBM, SEM, SEM, ANY], out_specs=[HBM, HBM],
        out_shape=[pltpu.HBM(packed.shape, packed.dtype), pltpu.HBM(slots.shape, slots.dtype)],
        input_output_aliases={0: 0, 1: 1},
        compiler_params=pltpu.CompilerParams(has_side_effects=EFFECT),
    )(packed, slots, send, recv, after)


def _sum_devices(packed, slots, me):
    n, R, _ = slots.shape
    tr = _tile(R, 1024)

    def body(m_ref, p_ref, s_ref, o_ref):
        own = p_ref[...]
        acc = None
        for d in range(n):
            term = jnp.where(m_ref[0] == d, own, s_ref[d])
            acc = term if acc is None else acc + term
        o_ref[...] = acc

    return pl.pallas_call(
        body,
        grid_spec=pltpu.PrefetchScalarGridSpec(
            num_scalar_prefetch=1, grid=(R // tr,),
            in_specs=[pl.BlockSpec((tr, LANES), lambda i, m: (i, 0)),
                      pl.BlockSpec((n, tr, LANES), lambda i, m: (0, i, 0))],
            out_specs=pl.BlockSpec((tr, LANES), lambda i, m: (i, 0))),
        out_shape=jax.ShapeDtypeStruct((R, LANES), F32),
        compiler_params=_params("parallel"), name="sum_devices")(me, packed, slots)


def _pack(arrs):
    rows, parts = [], []
    for a in arrs:
        flat = a.reshape(-1)
        r = -(-flat.shape[0] // (SUBLANES * LANES)) * SUBLANES
        parts.append(jnp.pad(flat, (0, r * LANES - flat.shape[0])).reshape(r, LANES))
        rows.append(r)
    return jnp.concatenate(parts, axis=0), rows


def _unpack(packed, rows, shapes):
    out, r0 = [], 0
    for r, shp in zip(rows, shapes):
        size = math.prod(shp)
        out.append(packed[r0:r0 + r].reshape(-1)[:size].reshape(shp))
        r0 += r
    return out


def _block_diag(w, per):
    H, dh, _ = w.shape
    w4 = w.reshape(H // per, per, dh, dh)
    eye = jnp.eye(per, dtype=w.dtype)
    return (w4[:, :, :, None, :] * eye[None, :, None, :, None]).reshape(H // per, per * dh, per * dh)


def _block_diag_take(d, per):
    n, s, _ = d.shape
    dh = s // per
    d5 = d.reshape(n, per, dh, per, dh)
    return jnp.stack([d5[:, h, :, h, :] for h in range(per)], axis=1).reshape(n * per, dh, dh)


def kernel(x, ffn1_norm, ffn1_w_gate, ffn1_w_up, ffn1_w_down, mix_norm, w_in, conv_dw, conv_dw_bias, conv_ln_g, conv_ln_b, lru_conv_w, lru_conv_b, lru_w_a, lru_b_a, lru_w_x, lru_b_x, lru_lambda, w_out, ffn2_norm, ffn2_w_gate, ffn2_w_up, ffn2_w_down, final_norm, loss_target, m_ffn1_norm, m_ffn1_w_gate, m_ffn1_w_up, m_ffn1_w_down, m_mix_norm, m_w_in, m_conv_dw, m_conv_dw_bias, m_conv_ln_g, m_conv_ln_b, m_lru_conv_w, m_lru_conv_b, m_lru_w_a, m_lru_b_a, m_lru_w_x, m_lru_b_x, m_lru_lambda, m_w_out, m_ffn2_norm, m_ffn2_w_gate, m_ffn2_w_up, m_ffn2_w_down, m_final_norm, v_ffn1_norm, v_ffn1_w_gate, v_ffn1_w_up, v_ffn1_w_down, v_mix_norm, v_w_in, v_conv_dw, v_conv_dw_bias, v_conv_ln_g, v_conv_ln_b, v_lru_conv_w, v_lru_conv_b, v_lru_w_a, v_lru_b_a, v_lru_w_x, v_lru_b_x, v_lru_lambda, v_w_out, v_ffn2_norm, v_ffn2_w_gate, v_ffn2_w_up, v_ffn2_w_down, v_final_norm):
    names = ['ffn1_norm', 'ffn1_w_gate', 'ffn1_w_up', 'ffn1_w_down', 'mix_norm', 'w_in', 'conv_dw', 'conv_dw_bias',
             'conv_ln_g', 'conv_ln_b', 'lru_conv_w', 'lru_conv_b', 'lru_w_a', 'lru_b_a', 'lru_w_x', 'lru_b_x',
             'lru_lambda', 'w_out', 'ffn2_norm', 'ffn2_w_gate', 'ffn2_w_up', 'ffn2_w_down', 'final_norm']
    env = dict(locals())
    W = {n: env[n] for n in names}
    M = {n: env['m_' + n] for n in names}
    V = {n: env['v_' + n] for n in names}

    xi, yi, ci = _here()
    chip = 2 * xi + yi
    cidx = ci.astype(jnp.int32).reshape(1)
    T, D = x.shape[-2], x.shape[-1]
    xs = x.reshape(T, D)
    tgt = loss_target.reshape(T, D)
    K, Cs = conv_dw.shape
    C = conv_dw_bias.shape[0]
    Wl = lru_conv_b.shape[0]
    K4 = lru_conv_w.shape[0]
    heads, dh, _ = lru_w_a.shape
    per = LANES // dh

    def row(v):
        return v.reshape(1, -1)

    tform = ('ffn1_w_gate', 'ffn1_w_up', 'ffn2_w_gate', 'ffn2_w_up')
    for n in tform:
        W[n], M[n], V[n] = W[n].T, M[n].T, V[n].T
    kp = -(-K // SUBLANES) * SUBLANES
    taps = jnp.concatenate([conv_dw, jnp.zeros((kp - K, Cs), F32), lru_conv_w,
                            jnp.zeros((2 * SUBLANES - K4, Cs), F32)], axis=0)
    idx = jnp.stack([ci, chip]).astype(jnp.int32)
    (wff1,) = _gather_weights([_place_cast([W['ffn1_w_gate'], W['ffn1_w_up'], ffn1_w_down], idx, BF16, "place_ffn1")])
    mixl = [_place_cast([w_in], idx, BF16, "place_w_in"), _place_cast([w_out], idx, BF16, "place_w_out"),
            _place_cast([taps], idx, F32, "place_taps")]
    msend, mrecv, mixl, mtok = _gather_start(mixl, wff1, "gather_mix_start")
    ff2l = _place_cast([W['ffn2_w_gate'], W['ffn2_w_up'], ffn2_w_down], idx, BF16, "place_ffn2")
    fsend, frecv, ff2l, ftok = _gather_start([ff2l], mtok, "gather_ffn2_start")
    wa_bd = _block_diag(lru_w_a, per).astype(BF16)
    wx_bd = _block_diag(lru_w_x, per).astype(BF16)

    x1, a1, b1 = _ffn_fwd(xs, row(ffn1_norm) + ftok[0:1, 0:1], wff1, "ffn1_fwd")
    win, wout, taps = _gather_wait(msend, mrecv, mixl, x1, "gather_mix_wait")
    win, wout, taps = win[0], wout.reshape(-1, D), taps[0]
    conv_w_full = taps[:, :K].transpose(1, 0, 2).reshape(K, N_CHIPS * Cs)
    lru_w4_full = taps[:, kp:kp + K4].transpose(1, 0, 2).reshape(K4, N_CHIPS * Cs)
    z = _mix_in_fwd(x1, row(mix_norm), win)
    u, u1 = _conv_fwd(z, conv_w_full, row(conv_dw_bias), row(conv_ln_g), row(conv_ln_b))
    yr, hs = _lru_fwd(z, 2 * C, lru_w4_full, row(lru_conv_b), wa_bd, row(lru_b_a), wx_bd, row(lru_b_x),
                      row(lru_lambda))
    x2 = _mix_out_fwd(x1, u, yr, wout)
    (wff2,) = _gather_wait(fsend, frecv, ff2l, x2, "gather_ffn2_wait")
    dx3, a2, b2, loss_blk, d_final = _ffn_fwd(x2, row(ffn2_norm), wff2, "ffn2_fwd", head=(row(final_norm), tgt))

    dx2, da2, db2, p2, hb2, dyh2, d_ffn2n = _ffn_bwd_tok(dx3, x2, row(ffn2_norm), a2, b2, wff2, "ffn2_bwd")
    dwg2, dwu2, dwd2 = _ffn_wgrad([([da2, db2], hb2), ([p2], dyh2)], ftok, "ffn2_wgrad")
    wsend, wrecv, f2g, f2o, wtok = _swap_start([dwg2, dwu2, dwd2], "swap_ffn2_start")
    dcat, dwout = _mix_out_bwd(dx2, u, yr, wout)
    dzc, cst = _conv_bwd(dcat, u1, z, conv_w_full, row(conv_ln_g) + wtok[0:1, 0:1], row(conv_ln_b))
    dzx, dzg, lst, dwa_bd, dwx_bd = _lru_bwd(dcat, C, hs, z, 2 * C, lru_w4_full, row(lru_conv_b), wa_bd,
                                              row(lru_b_a), wx_bd, row(lru_b_x), row(lru_lambda))
    dx1, dwin, d_mixn = _mix_in_bwd(dzc, dzx, dzg, x1, dx2, row(mix_norm), win)

    early_names = ['w_in', 'w_out', 'ffn2_w_gate', 'ffn2_w_up', 'ffn2_w_down']
    xsend, xrecv, mixg, mixo, xtok = _swap_start([dwin, dwout.reshape(N_CHIPS, -1, D)], "swap_mix_start")
    f2g, f2o = _swap_wait(wsend, wrecv, f2g, f2o, xtok, "swap_ffn2_wait")
    f2_parts = [_add_cast(g, o, cidx, "add_cast_" + n) for g, o, n in zip(f2g, f2o, early_names[2:])]
    mixg, mixo = _swap_wait(xsend, xrecv, mixg, mixo, f2_parts[-1], "swap_mix_wait")
    e_parts = [_add_cast(g, o, cidx, "add_cast_" + n) for g, o, n in zip(mixg, mixo, early_names[:2])] + f2_parts
    esend, erecv, e_parts, e_lands, etok = _exchange_start(e_parts, "exchange_early_start")

    dx0, da1, db1, p1, hb1, dyh1, d_ffn1n = _ffn_bwd_tok(dx1, xs, row(ffn1_norm) + etok[0:1, 0:1], a1, b1, wff1,
                                                         "ffn1_bwd")

    small_names = ['ffn1_norm', 'mix_norm', 'conv_dw', 'conv_dw_bias', 'conv_ln_g', 'conv_ln_b', 'lru_conv_w',
                   'lru_conv_b', 'lru_w_a', 'lru_b_a', 'lru_w_x', 'lru_b_x', 'lru_lambda', 'ffn2_norm',
                   'final_norm']
    small = {
        'ffn1_norm': d_ffn1n, 'mix_norm': d_mixn, 'conv_dw': cst[:K], 'conv_dw_bias': cst[K + 1],
        'conv_ln_g': cst[K + 2], 'conv_ln_b': cst[K + 3], 'lru_conv_w': lst[:K4], 'lru_conv_b': lst[K4],
        'lru_w_a': _block_diag_take(dwa_bd, per), 'lru_b_a': lst[K4 + 1],
        'lru_w_x': _block_diag_take(dwx_bd, per), 'lru_b_x': lst[K4 + 2], 'lru_lambda': lst[K4 + 3],
        'ffn2_norm': d_ffn2n, 'final_norm': d_final,
    }
    packed, rows = _pack([small[n] for n in small_names] + [loss_blk[0:1, 0:1]])
    ssend, srecv, packed, sslots, stok = _small_start(packed)

    gu_names, d_names = ['ffn1_w_gate', 'ffn1_w_up'], ['ffn1_w_down']
    gu = _ffn_wgrad([([da1, db1], hb1)], stok, "ffn1_wgrad_gu", swap=True)
    gu_parts = [_add_cast(g, o, cidx, "add_cast_" + n) for g, o, n in zip(gu[:2], gu[2:], gu_names)]
    gsend, grecv, gu_parts, gu_lands, gtok = _exchange_start(gu_parts, "exchange_gu_start")
    dn = _ffn_wgrad([([p1], dyh1)], gtok, "ffn1_wgrad_d", swap=True)
    d_parts = [_add_cast(g, o, cidx, "add_cast_" + n) for g, o, n in zip(dn[:1], dn[1:], d_names)]
    dsend, drecv, d_parts, d_lands, ltok = _exchange_start(d_parts, "exchange_d_start")
    e_parts, e_slots = _exchange_wait(esend, erecv, e_parts, e_lands, ltok, "exchange_early_wait")
    delta, new_m, new_v = {}, {}, {}

    def finish(group, parts, slots, tag):
        halves = [_sum_slots(p, b, idx, "sum_slots_" + n) for p, b, n in zip(parts, slots, group)]
        for n, g in zip(group, _share_halves(halves, "share_halves_" + tag)):
            G[n] = g
            delta[n], new_m[n], new_v[n] = _adamw(W[n], g, M[n], V[n], "adamw_" + n)

    G = {}
    e_halves = [_sum_slots(p, b, idx, "sum_slots_" + n) for p, b, n in zip(e_parts, e_slots, early_names)]
    hsend, hrecv, e_halves, htok = _share_start(e_halves, "share_early_start")

    full_shapes = [(K, C) if n == 'conv_dw' else (K4, Wl) if n == 'lru_conv_w' else W[n].shape for n in small_names]
    packed, sslots = _small_wait(ssend, srecv, packed, sslots, htok)
    summed = _sum_devices(packed, sslots, (4 * xi + 2 * yi + ci).astype(jnp.int32).reshape(1))
    *small_sums, loss_sum = _unpack(summed, rows, full_shapes + [(1, 1)])
    for n, gsum in zip(small_names, small_sums):
        if n == 'conv_dw':
            gsum = lax.dynamic_slice_in_dim(gsum, chip * Cs, Cs, axis=1)
        elif n == 'lru_conv_w':
            gsum = lax.dynamic_slice_in_dim(gsum, chip * lru_conv_w.shape[1], lru_conv_w.shape[1], axis=1)
        G[n] = gsum

    pw, prow = _pack([W[n] for n in small_names])
    pg, _ = _pack([G[n] for n in small_names])
    pm, _ = _pack([M[n] for n in small_names])
    pv, _ = _pack([V[n] for n in small_names])
    sd, sm, sv = _adamw(pw, pg, pm, pv, "adamw_small")
    shapes = [W[n].shape for n in small_names]
    for n, a, b, c_ in zip(small_names, _unpack(sd, prow, shapes), _unpack(sm, prow, shapes),
                           _unpack(sv, prow, shapes)):
        delta[n], new_m[n], new_v[n] = a, b, c_

    for n, g in zip(early_names, _share_wait(hsend, hrecv, e_halves, sd, "share_early_wait")):
        G[n] = g
        delta[n], new_m[n], new_v[n] = _adamw(W[n], g, M[n], V[n], "adamw_" + n)
    done = sd[0:SUBLANES] + delta[early_names[-1]][0:SUBLANES, 0:LANES]
    gu_parts, gu_slots = _exchange_wait(gsend, grecv, gu_parts, gu_lands, done, "exchange_gu_wait")
    d_parts, d_slots = _exchange_wait(dsend, drecv, d_parts, d_lands, gu_slots[0], "exchange_d_wait")
    finish(gu_names + d_names, list(gu_parts) + list(d_parts), list(gu_slots) + list(d_slots), "last")

    loss = loss_sum[0, 0]
    grad_x = dx0.reshape(x.shape)
    for n in tform:
        G[n], delta[n], new_m[n], new_v[n] = G[n].T, delta[n].T, new_m[n].T, new_v[n].T
    return (loss, grad_x, *[G[n] for n in names], *[delta[n] for n in names],
            *[new_m[n] for n in names], *[new_v[n] for n in names])
```

```python
import functools
import math

import jax
import jax.numpy as jnp
from jax import lax
from jax.experimental import pallas as pl
from jax.experimental.pallas import tpu as pltpu

F32 = jnp.float32
BF16 = jnp.bfloat16
MESH = pl.DeviceIdType.MESH

RMS_EPS = 1e-6
LN_EPS = 1e-5
LRU_C = 8.0
FFN_RES_SCALE = 0.5
ADAM_LR = 0.001
ADAM_B1 = 0.9
ADAM_B2 = 0.999
ADAM_EPS = 1e-08
ADAM_WD = 0.01
ADAM_STEP = 10

LANES = 128
SUBLANES = 8
CONV_HALO = 32
LRU_HALO = 8
ROW_CHUNK = 64
VMEM_LIMIT = 56 * 1024 * 1024
N_CHIPS = 4
N_DEV = 8
TOK_TILE = 1024
BWD_TILE = 512
FFN_BWD_TILE = 512
BWD_ROWS = 32
FFN_BWD_CHAIN = 256
CONV_TILE = 512
LRU_TILE = 4096
LRU_GROUPS = 16


def _dot(a, b):
    return jnp.dot(a, b, preferred_element_type=F32)


def _dot_nt(a, b):
    return lax.dot_general(a, b, (((1,), (1,)), ((), ())), preferred_element_type=F32)


def _dot_tn(a, b):
    return lax.dot_general(a, b, (((0,), (0,)), ((), ())), preferred_element_type=F32)


def _tile(n, pref, mult=SUBLANES):
    for t in range(min(pref, n), 0, -1):
        if n % t == 0 and t % mult == 0:
            return t
    return n


def _params(*sem):
    return pltpu.CompilerParams(dimension_semantics=sem, vmem_limit_bytes=VMEM_LIMIT)


def _rms_stats(x):
    r = lax.rsqrt(jnp.mean(x * x, axis=-1, keepdims=True) + RMS_EPS)
    return x * r, r


def _rms_bwd(dh, xh, r, g):
    dxh = dh * g
    return r * (dxh - xh * jnp.mean(dxh * xh, axis=-1, keepdims=True))


def _colsum(v):
    return jnp.sum(v, axis=0, keepdims=True)


def _ffn_fwd(x, g, wff, name, head=None):
    T, D = x.shape
    ns, fs = wff.shape[1], wff.shape[2]
    tm = _tile(T, TOK_TILE)
    mc = _tile(tm, FFN_BWD_CHAIN, 16)
    rc = _tile(tm, FFN_BWD_CHAIN)

    def body(*refs):
        x_ref, g_ref, wg_ref, wu_ref, wd_ref = refs[:5]
        if head is None:
            y_ref, a_ref, b_ref, hb_ref, acc_ref = refs[5:]
        else:
            gf_ref, t_ref, y_ref, a_ref, b_ref, loss_ref, dgf_ref, hb_ref, acc_ref = refs[5:]
        j = pl.program_id(1)

        @pl.when(j == 0)
        def _():
            xh, _ = _rms_stats(x_ref[...])
            hb_ref[...] = (xh * g_ref[...]).astype(BF16)
            acc_ref[...] = jnp.zeros_like(acc_ref)

        if head is not None:
            @pl.when((pl.program_id(0) == 0) & (j == 0))
            def _():
                loss_ref[...] = jnp.zeros_like(loss_ref)
                dgf_ref[...] = jnp.zeros_like(dgf_ref)

        for q0 in range(0, tm, mc):
            blk = pl.ds(q0, mc)
            hb = hb_ref[blk, :]
            a = _dot_nt(hb, wg_ref[...])
            b = _dot_nt(hb, wu_ref[...])
            a_ref[blk, :] = a.astype(BF16)
            b_ref[blk, :] = b.astype(BF16)
            p = (a * jax.nn.sigmoid(a) * b).astype(BF16)
            acc_ref[blk, :] += _dot(p, wd_ref[...])

        @pl.when(j == ns - 1)
        def _():
            if head is None:
                y_ref[...] = x_ref[...] + FFN_RES_SCALE * acc_ref[...]
                return
            gv = gf_ref[...]
            loss = jnp.zeros((), F32)
            dg = jnp.zeros((1, D), F32)
            for r0 in range(0, tm, rc):
                rows = pl.ds(r0, rc)
                xh, r = _rms_stats(x_ref[rows, :] + FFN_RES_SCALE * acc_ref[rows, :])
                e = xh * gv - t_ref[rows, :]
                loss = loss + 0.5 * jnp.sum(jnp.mean(e * e, axis=-1, keepdims=True))
                dy = e * (1.0 / D)
                dg = dg + _colsum(dy * xh)
                y_ref[rows, :] = _rms_bwd(dy, xh, r, gv)
            loss_ref[...] += loss
            dgf_ref[...] += dg

    def wspec(n):
        return pl.BlockSpec((None, None, fs, D), lambda i, j: (n, j, 0, 0))

    tok = pl.BlockSpec((tm, D), lambda i, j: (i, 0))
    vec = pl.BlockSpec((1, D), lambda i, j: (0, 0))
    mid = pl.BlockSpec((None, tm, fs), lambda i, j: (j, i, 0))
    in_specs = [tok, vec, wspec(0), wspec(1), wspec(2)]
    out_specs = [tok, mid, mid]
    out_shape = [jax.ShapeDtypeStruct((T, D), F32), jax.ShapeDtypeStruct((ns, T, fs), BF16),
                 jax.ShapeDtypeStruct((ns, T, fs), BF16)]
    args = [x, g, wff, wff, wff]
    if head is not None:
        in_specs += [vec, tok]
        out_specs += [pl.BlockSpec((SUBLANES, LANES), lambda i, j: (0, 0)), vec]
        out_shape += [jax.ShapeDtypeStruct((SUBLANES, LANES), F32), jax.ShapeDtypeStruct((1, D), F32)]
        args += list(head)
    return pl.pallas_call(
        body, grid=(T // tm, ns), in_specs=in_specs, out_specs=out_specs, out_shape=out_shape,
        scratch_shapes=[pltpu.VMEM((tm, D), BF16), pltpu.VMEM((tm, D), F32)],
        compiler_params=_params("arbitrary", "arbitrary"), name=name)(*args)


def _ffn_bwd_tok(dy, x, g, a, b, wff, name):
    T, D = x.shape
    ns, fs = wff.shape[1], wff.shape[2]
    tm = _tile(T, FFN_BWD_TILE)
    rc = _tile(tm, BWD_ROWS)
    mc = _tile(tm, FFN_BWD_CHAIN, rc)

    def body(dy_ref, x_ref, g_ref, a_ref, b_ref, w_ref,
             dx_ref, da_ref, db_ref, p_ref, hb_ref, dyh_ref, dg_ref, dh_ref, dp_ref):
        i, j = pl.program_id(0), pl.program_id(1)
        cur = dp_ref.at[j % 2]
        nxt = dp_ref.at[(j + 1) % 2]
        wg_ref, wu_ref = w_ref.at[0, j], w_ref.at[1, j]
        wd0_ref, wdn_ref = w_ref.at[2, 0], w_ref.at[2, jnp.minimum(j + 1, ns - 1)]

        @pl.when((i == 0) & (j == 0))
        def _():
            dg_ref[...] = jnp.zeros_like(dg_ref)

        @pl.when(j == 0)
        def _():
            for r0 in range(0, tm, rc):
                rows = pl.ds(r0, rc)
                xh, _ = _rms_stats(x_ref[rows, :])
                hb_ref[rows, :] = (xh * g_ref[...]).astype(BF16)
                dyh_ref[rows, :] = (FFN_RES_SCALE * dy_ref[rows, :]).astype(BF16)
            dh_ref[...] = jnp.zeros_like(dh_ref)
            cur[...] = _dot_nt(dyh_ref[...], wd0_ref[...])

        def chains(with_next):
            for q0 in range(0, tm, mc):
                blk = pl.ds(q0, mc)
                for r0 in range(q0, q0 + mc, rc):
                    rows = pl.ds(r0, rc)
                    av = a_ref[rows, :].astype(F32)
                    bv = b_ref[rows, :].astype(F32)
                    dp = cur[rows, :]
                    s = jax.nn.sigmoid(av)
                    sl = av * s
                    da_ref[rows, :] = (dp * bv * (s * (1.0 + av * (1.0 - s)))).astype(BF16)
                    db_ref[rows, :] = (dp * sl).astype(BF16)
                    p_ref[rows, :] = (sl * bv).astype(BF16)
                if with_next:
                    nxt[blk, :] = _dot_nt(dyh_ref[blk, :], wdn_ref[...])
                dh_ref[blk, :] += _dot(da_ref[blk, :], wg_ref[...]) + _dot(db_ref[blk, :], wu_ref[...])

        pl.when(j < ns - 1)(functools.partial(chains, True))
        pl.when(j == ns - 1)(functools.partial(chains, False))

        @pl.when(j == ns - 1)
        def _():
            gv = g_ref[...]
            dg = jnp.zeros((1, D), F32)
            for r0 in range(0, tm, rc):
                rows = pl.ds(r0, rc)
                xh, r = _rms_stats(x_ref[rows, :])
                dh = dh_ref[rows, :]
                dx_ref[rows, :] = dy_ref[rows, :] + _rms_bwd(dh, xh, r, gv)
                dg = dg + _colsum(dh * xh)
            dg_ref[...] += dg

    tok = pl.BlockSpec((tm, D), lambda i, j: (i, 0))
    mid = pl.BlockSpec((None, tm, fs), lambda i, j: (j, i, 0))
    vec = pl.BlockSpec((1, D), lambda i, j: (0, 0))
    return pl.pallas_call(
        body, grid=(T // tm, ns),
        in_specs=[tok, tok, vec, mid, mid,
                  pl.BlockSpec(wff.shape, lambda i, j: (0, 0, 0, 0), pipeline_mode=pl.Buffered(1))],
        out_specs=[tok, mid, mid, mid, tok, tok, vec],
        out_shape=[jax.ShapeDtypeStruct((T, D), F32),
                   jax.ShapeDtypeStruct((ns, T, fs), BF16), jax.ShapeDtypeStruct((ns, T, fs), BF16),
                   jax.ShapeDtypeStruct((ns, T, fs), BF16),
                   jax.ShapeDtypeStruct((T, D), BF16), jax.ShapeDtypeStruct((T, D), BF16),
                   jax.ShapeDtypeStruct((1, D), F32)],
        scratch_shapes=[pltpu.VMEM((tm, D), F32), pltpu.VMEM((2, tm, fs), F32)],
        compiler_params=_params("arbitrary", "arbitrary"), name=name)(dy, x, g, a, b, wff)


def _ffn_wgrad(groups, after, name, swap=False):
    flat = [(l, gi) for gi, (ls, _) in enumerate(groups) for l in ls]
    ng, n = len(groups), len(flat)
    T, D = groups[0][1].shape
    ns, _, fs = flat[0][0].shape
    tm = _tile(T, TOK_TILE)
    nI = T // tm
    rh = fs // 2

    def body(*refs):
        rhs_refs, lhs_refs, out_refs = refs[:ng], refs[ng:ng + n], refs[ng + n + 1:ng + 2 * n + 1]
        j, i = pl.program_id(0), pl.program_id(1)

        @pl.when(i == 0)
        def _():
            for o in out_refs:
                o[...] = jnp.zeros_like(o)

        rvs = [r[...] for r in rhs_refs]
        for l, o, (_, gi) in zip(lhs_refs, out_refs, flat):
            o[...] += _dot_tn(l[...], rvs[gi])

        if swap:
            land_refs = refs[ng + 2 * n + 1:ng + 3 * n + 1]
            send, recv, stage = refs[ng + 3 * n + 1:]
            x, y, c = _here()

            def copies(jj):
                return [pltpu.make_async_remote_copy(
                    src_ref=stage.at[jj % 2, k], dst_ref=land_refs[k].at[jj],
                    send_sem=send.at[k * ns + jj], recv_sem=recv.at[k * ns + jj],
                    device_id=(x, y, 1 - c), device_id_type=MESH) for k in range(n)]

            @pl.when(i == nI - 1)
            def _():
                theirs = pl.ds(pl.multiple_of((1 - c) * rh, SUBLANES), rh)
                for k in range(n):
                    stage[j % 2, k] = out_refs[k][theirs, :]
                for cp in copies(j):
                    cp.start()

            @pl.when((i == nI - 1) & (j > 0))
            def _():
                for cp in copies(j - 1):
                    cp.wait_send()

            @pl.when((i == nI - 1) & (j == ns - 1))
            def _():
                for cp in copies(j):
                    cp.wait_send()
                for jj in range(ns):
                    for cp in copies(jj):
                        cp.wait_recv()

    tok = pl.BlockSpec((tm, D), lambda j, i: (i, 0))
    mid = pl.BlockSpec((None, tm, fs), lambda j, i: (j, i, 0))
    wsp = pl.BlockSpec((None, fs, D), lambda j, i: (j, 0, 0))
    sds = jax.ShapeDtypeStruct((ns, fs, D), F32)
    out_specs, out_shape, scratch = [wsp] * n, [sds] * n, []
    if swap:
        out_specs += [ANY] * n
        out_shape += [jax.ShapeDtypeStruct((ns, rh, D), F32)] * n
        scratch = [pltpu.SemaphoreType.DMA((n * ns,)), pltpu.SemaphoreType.DMA((n * ns,)),
                   pltpu.VMEM((2, n, rh, D), F32)]
    return pl.pallas_call(
        body, grid=(ns, nI),
        in_specs=[tok] * ng + [mid] * n + [pl.BlockSpec((SUBLANES, LANES), lambda j, i: (0, 0))],
        out_specs=out_specs, out_shape=out_shape, scratch_shapes=scratch,
        compiler_params=_params("arbitrary", "arbitrary"), name=name)(
            *[r for _, r in groups], *[l for l, _ in flat], after)


def _mix_in_fwd(x, g, win):
    T, D = x.shape
    ns, ws = win.shape[0], win.shape[2]
    tm = _tile(T, TOK_TILE)

    def body(x_ref, g_ref, w_ref, z_ref):
        xh, _ = _rms_stats(x_ref[...])
        hb = (xh * g_ref[...]).astype(BF16)
        for j in range(ns):
            z_ref[:, pl.ds(j * ws, ws)] = _dot(hb, w_ref[j])

    return pl.pallas_call(
        body, grid=(T // tm,),
        in_specs=[pl.BlockSpec((tm, D), lambda i: (i, 0)), pl.BlockSpec((1, D), lambda i: (0, 0)),
                  pl.BlockSpec((ns, D, ws), lambda i: (0, 0, 0), pipeline_mode=pl.Buffered(1))],
        out_specs=pl.BlockSpec((tm, ns * ws), lambda i: (i, 0)),
        out_shape=jax.ShapeDtypeStruct((T, ns * ws), F32),
        compiler_params=_params("parallel"), name="mix_in_fwd")(x, g, win)


def _tap_sum(buf, w_ref, ntaps, first_row, r0, rows, flip):
    acc = None
    for k in range(ntaps):
        off = (ntaps - 1 - k) if flip else k
        t = buf[pl.ds(first_row + r0 + off, rows), :] * w_ref[pl.ds(k, 1), :]
        acc = t if acc is None else acc + t
    return acc


def _shift_copies(buf, sh, rows):
    for r in range(1, SUBLANES):
        sh[r - 1, pl.ds(0, rows), :] = buf[pl.ds(r, rows), :]


def _tap_rows(buf, sh, off, r0, rows):
    r = off % SUBLANES
    if r == 0:
        return buf[pl.ds(off + r0, rows), :]
    return sh[r - 1, pl.ds(off - r + r0, rows), :]


def _tap_sum_tiles(buf, sh, w_ref, ntaps, first_row, r0, rows, flip):
    acc = None
    for k in range(ntaps):
        off = first_row + ((ntaps - 1 - k) if flip else k)
        t = _tap_rows(buf, sh, off, r0, rows) * w_ref[pl.ds(k, 1), :]
        acc = t if acc is None else acc + t
    return acc


def _conv_fwd(z, w, bias, lng, lnb):
    T = z.shape[0]
    K, C = w.shape
    tm = _tile(T, CONV_TILE, ROW_CHUNK)
    rc = min(ROW_CHUNK, tm)
    srows = tm + CONV_HALO - SUBLANES

    def body(cv_ref, cg_ref, w_ref, b_ref, g_ref, bb_ref, u_ref, u1_ref, buf, sh):
        @pl.when(pl.program_id(0) == 0)
        def _():
            buf[pl.ds(0, CONV_HALO), :] = jnp.zeros((CONV_HALO, C), F32)

        buf[pl.ds(CONV_HALO, tm), :] = cv_ref[...] * jax.nn.sigmoid(cg_ref[...])
        _shift_copies(buf, sh, srows)
        for r0 in range(0, tm, rc):
            u1 = _tap_sum_tiles(buf, sh, w_ref, K, CONV_HALO - (K - 1), r0, rc, False) + b_ref[...]
            u1_ref[pl.ds(r0, rc), :] = u1
            xc = u1 - jnp.mean(u1, axis=-1, keepdims=True)
            xh = xc * lax.rsqrt(jnp.mean(xc * xc, axis=-1, keepdims=True) + LN_EPS)
            u2 = xh * g_ref[...] + bb_ref[...]
            u_ref[pl.ds(r0, rc), :] = (u2 * jax.nn.sigmoid(u2)).astype(BF16)
        buf[pl.ds(0, CONV_HALO), :] = buf[pl.ds(tm, CONV_HALO), :]

    vec = pl.BlockSpec((1, C), lambda i: (0, 0))
    return pl.pallas_call(
        body, grid=(T // tm,),
        in_specs=[pl.BlockSpec((tm, C), lambda i: (i, 0)), pl.BlockSpec((tm, C), lambda i: (i, 1)),
                  pl.BlockSpec((K, C), lambda i: (0, 0)), vec, vec, vec],
        out_specs=[pl.BlockSpec((tm, C), lambda i: (i, 0)), pl.BlockSpec((tm, C), lambda i: (i, 0))],
        out_shape=[jax.ShapeDtypeStruct((T, C), BF16), jax.ShapeDtypeStruct((T, C), F32)],
        scratch_shapes=[pltpu.VMEM((CONV_HALO + tm, C), F32), pltpu.VMEM((SUBLANES - 1, srows, C), F32)],
        compiler_params=_params("arbitrary"), name="conv_fwd")(z, z, w, bias, lng, lnb)


def _conv_bwd(dcat, u1, z, w, lng, lnb):
    T = z.shape[0]
    K, C = w.shape
    tm = _tile(T, CONV_TILE, ROW_CHUNK)
    rc = min(ROW_CHUNK, tm)
    nI = T // tm
    hb = tm // CONV_HALO
    srows = ((K + 4 + SUBLANES - 1) // SUBLANES) * SUBLANES
    shrows = tm + CONV_HALO - SUBLANES

    def body(du_ref, u1_ref, cv_ref, cg_ref, cvp_ref, cgp_ref, w_ref, g_ref, bb_ref,
             dz_ref, st_ref, u0buf, d1buf, ush, dsh):
        i = pl.program_id(0)
        ti = nI - 1 - i

        @pl.when(i == 0)
        def _():
            st_ref[...] = jnp.zeros_like(st_ref)
            d1buf[pl.ds(tm, CONV_HALO), :] = jnp.zeros((CONV_HALO, C), F32)

        prev = cvp_ref[...] * jax.nn.sigmoid(cgp_ref[...])
        u0buf[pl.ds(0, CONV_HALO), :] = jnp.where(ti == 0, 0.0, prev)
        u0buf[pl.ds(CONV_HALO, tm), :] = cv_ref[...] * jax.nn.sigmoid(cg_ref[...])

        gv = g_ref[...]
        dbias = jnp.zeros((1, C), F32)
        dgain = jnp.zeros((1, C), F32)
        dlnb = jnp.zeros((1, C), F32)
        for r0 in range(0, tm, rc):
            u1 = u1_ref[pl.ds(r0, rc), :]
            xc = u1 - jnp.mean(u1, axis=-1, keepdims=True)
            rstd = lax.rsqrt(jnp.mean(xc * xc, axis=-1, keepdims=True) + LN_EPS)
            xh = xc * rstd
            u2 = xh * gv + bb_ref[...]
            s = jax.nn.sigmoid(u2)
            du2 = du_ref[pl.ds(r0, rc), :] * (s * (1.0 + u2 * (1.0 - s)))
            dgain = dgain + _colsum(du2 * xh)
            dlnb = dlnb + _colsum(du2)
            dxh = du2 * gv
            du1 = rstd * (dxh - jnp.mean(dxh, axis=-1, keepdims=True)
                          - xh * jnp.mean(dxh * xh, axis=-1, keepdims=True))
            dbias = dbias + _colsum(du1)
            d1buf[pl.ds(r0, rc), :] = du1
        st_ref[pl.ds(K + 1, 1), :] += dbias
        st_ref[pl.ds(K + 2, 1), :] += dgain
        st_ref[pl.ds(K + 3, 1), :] += dlnb

        _shift_copies(u0buf, ush, shrows)
        _shift_copies(d1buf, dsh, shrows)
        for k in range(K):
            acc = jnp.zeros((SUBLANES, C), F32)
            for r0 in range(0, tm, rc):
                prod = d1buf[pl.ds(r0, rc), :] * _tap_rows(u0buf, ush, CONV_HALO - (K - 1) + k, r0, rc)
                acc = acc + jnp.sum(prod.reshape(rc // SUBLANES, SUBLANES, C), axis=0)
            st_ref[pl.ds(k, 1), :] += _colsum(acc)

        for r0 in range(0, tm, rc):
            du0 = _tap_sum_tiles(d1buf, dsh, w_ref, K, 0, r0, rc, True)
            cv = cv_ref[pl.ds(r0, rc), :]
            sg = jax.nn.sigmoid(cg_ref[pl.ds(r0, rc), :])
            dz_ref[pl.ds(r0, rc), pl.ds(0, C)] = (du0 * sg).astype(BF16)
            dz_ref[pl.ds(r0, rc), pl.ds(C, C)] = (du0 * cv * sg * (1.0 - sg)).astype(BF16)
        d1buf[pl.ds(tm, CONV_HALO), :] = d1buf[pl.ds(0, CONV_HALO), :]

    def rev(col):
        return lambda i: (nI - 1 - i, col)

    def rev_prev(col):
        return lambda i: (jnp.maximum((nI - 1 - i) * hb - 1, 0), col)

    vec = pl.BlockSpec((1, C), lambda i: (0, 0))
    return pl.pallas_call(
        body, grid=(nI,),
        in_specs=[pl.BlockSpec((tm, C), rev(0)), pl.BlockSpec((tm, C), rev(0)),
                  pl.BlockSpec((tm, C), rev(0)), pl.BlockSpec((tm, C), rev(1)),
                  pl.BlockSpec((CONV_HALO, C), rev_prev(0)), pl.BlockSpec((CONV_HALO, C), rev_prev(1)),
                  pl.BlockSpec((K, C), lambda i: (0, 0)), vec, vec],
        out_specs=[pl.BlockSpec((tm, 2 * C), rev(0)), pl.BlockSpec((srows, C), lambda i: (0, 0))],
        out_shape=[jax.ShapeDtypeStruct((T, 2 * C), BF16), jax.ShapeDtypeStruct((srows, C), F32)],
        scratch_shapes=[pltpu.VMEM((CONV_HALO + tm, C), F32), pltpu.VMEM((tm + CONV_HALO, C), F32),
                        pltpu.VMEM((SUBLANES - 1, shrows, C), F32), pltpu.VMEM((SUBLANES - 1, shrows, C), F32)],
        compiler_params=_params("arbitrary"), name="conv_bwd")(dcat, u1, z, z, z, z, w, lng, lnb)


def _softplus(v):
    return jnp.maximum(v, 0.0) + jnp.log(1.0 + jnp.exp(-jnp.abs(v)))


def _gelu(v):
    c = math.sqrt(2.0 / math.pi)
    t = jnp.tanh(c * (v + 0.044715 * v * v * v))
    gl = 0.5 * v * (1.0 + t)
    dgl = 0.5 * (1.0 + t) + 0.5 * v * (1.0 - t * t) * c * (1.0 + 3.0 * 0.044715 * v * v)
    return gl, dgl


def _lru_gates(xr, wa, ba, wx, bx, lam):
    xb = xr.astype(BF16)
    r = jax.nn.sigmoid(_dot(xb, wa) + ba)
    ig = jax.nn.sigmoid(_dot(xb, wx) + bx)
    sp = _softplus(-lam)
    log_a = -LRU_C * r * sp
    a = jnp.exp(log_a)
    y = 2.0 * log_a
    series = -(y * (1.0 + y * (0.5 + y * (1.0 / 6.0 + y * (1.0 / 24.0)))))
    mult = jnp.sqrt(jnp.where(y > -0.02, series, 1.0 - jnp.exp(y)))
    return a, mult, r, ig, sp


def _scan_tile(a_s, b_s, h_s, p_s, carry, seg, reverse):
    hl = [jnp.zeros((SUBLANES, LANES), F32)] * LRU_GROUPS
    pr = [jnp.ones((SUBLANES, LANES), F32)] * LRU_GROUPS
    for n in range(seg):
        for g in range(LRU_GROUPS):
            rows = pl.ds(g * SUBLANES * seg + ((seg - 1 - n) if reverse else n), SUBLANES, stride=seg)
            av = a_s[rows, :]
            hl[g] = av * hl[g] + b_s[rows, :]
            pr[g] = av * pr[g]
            h_s[rows, :] = hl[g]
            p_s[rows, :] = pr[g]
    nseg = SUBLANES * LRU_GROUPS
    cs = [None] * nseg
    c = carry
    for s in (range(nseg - 1, -1, -1) if reverse else range(nseg)):
        g, r = divmod(s, SUBLANES)
        cs[s] = c
        c = hl[g][r:r + 1, :] + pr[g][r:r + 1, :] * c
    return cs, c


def _lru_fwd(z, col0, w4, b4, wa, ba, wx, bx, lam):
    T = z.shape[0]
    K4, W = w4.shape
    nC = W // LANES
    tm = _tile(T, LRU_TILE, SUBLANES * SUBLANES * LRU_GROUPS)
    seg = tm // (SUBLANES * LRU_GROUPS)
    cx, cg = col0 // LANES, (col0 + W) // LANES

    def body(rx_ref, rg_ref, w4_ref, b4_ref, wa_ref, ba_ref, wx_ref, bx_ref, lam_ref,
             yr_ref, hs_ref, xbuf, a_s, b_s, h_s, p_s, hc):
        @pl.when(pl.program_id(1) == 0)
        def _():
            xbuf[pl.ds(0, LRU_HALO), :] = jnp.zeros((LRU_HALO, LANES), F32)
            hc[...] = jnp.zeros_like(hc)

        xbuf[pl.ds(LRU_HALO, tm), :] = rx_ref[...]
        xr = _tap_sum(xbuf, w4_ref, K4, LRU_HALO - (K4 - 1), 0, tm, False) + b4_ref[...]
        a, mult, _, ig, _ = _lru_gates(xr, wa_ref[...], ba_ref[...], wx_ref[...], bx_ref[...], lam_ref[...])
        a_s[...] = a
        b_s[...] = mult * ig * xr
        cs, cout = _scan_tile(a_s, b_s, h_s, p_s, hc[pl.ds(0, 1), :], seg, False)
        hc[pl.ds(0, 1), :] = cout
        for s in range(SUBLANES * LRU_GROUPS):
            rows = pl.ds(s * seg, seg)
            h = h_s[rows, :] + p_s[rows, :] * cs[s]
            hs_ref[rows, :] = h
            gl, _ = _gelu(rg_ref[rows, :])
            yr_ref[rows, :] = (h * gl).astype(BF16)
        xbuf[pl.ds(0, LRU_HALO), :] = xbuf[pl.ds(tm, LRU_HALO), :]

    vec = pl.BlockSpec((1, LANES), lambda c, i: (0, c))
    mat = pl.BlockSpec((None, LANES, LANES), lambda c, i: (c, 0, 0))
    return pl.pallas_call(
        body, grid=(nC, T // tm),
        in_specs=[pl.BlockSpec((tm, LANES), lambda c, i: (i, cx + c)),
                  pl.BlockSpec((tm, LANES), lambda c, i: (i, cg + c)),
                  pl.BlockSpec((K4, LANES), lambda c, i: (0, c)), vec, mat, vec, mat, vec, vec],
        out_specs=[pl.BlockSpec((tm, LANES), lambda c, i: (i, c)), pl.BlockSpec((tm, LANES), lambda c, i: (i, c))],
        out_shape=[jax.ShapeDtypeStruct((T, W), BF16), jax.ShapeDtypeStruct((T, W), F32)],
        scratch_shapes=[pltpu.VMEM((LRU_HALO + tm, LANES), F32)] + [pltpu.VMEM((tm, LANES), F32)] * 4
        + [pltpu.VMEM((SUBLANES, LANES), F32)],
        compiler_params=_params("parallel", "arbitrary"), name="lru_fwd")(z, z, w4, b4, wa, ba, wx, bx, lam)


def _lru_bwd(dcat, dcol0, hs, z, col0, w4, b4, wa, ba, wx, bx, lam):
    T = z.shape[0]
    K4, W = w4.shape
    assert K4 + 4 == SUBLANES
    nC = W // LANES
    tm = _tile(T, LRU_TILE, SUBLANES * SUBLANES * LRU_GROUPS)
    seg = tm // (SUBLANES * LRU_GROUPS)
    nI = T // tm
    hb = tm // LRU_HALO
    cx, cg, cd = col0 // LANES, (col0 + W) // LANES, dcol0 // LANES

    def body(dyr_ref, hs_ref, hsp_ref, rx_ref, rxp_ref, rg_ref, w4_ref, b4_ref, wa_ref, ba_ref, wx_ref, bx_ref,
             lam_ref, dzx_ref, dzg_ref, st_ref, dwa_ref, dwx_ref, xbuf, hbuf, abuf, a_s, b_s, h_s, p_s, dbuf, gc, anc):
        i = pl.program_id(1)
        ti = nI - 1 - i

        @pl.when(i == 0)
        def _():
            st_ref[...] = jnp.zeros_like(st_ref)
            dwa_ref[...] = jnp.zeros_like(dwa_ref)
            dwx_ref[...] = jnp.zeros_like(dwx_ref)
            gc[...] = jnp.zeros_like(gc)
            anc[...] = jnp.zeros_like(anc)
            dbuf[pl.ds(tm, LRU_HALO), :] = jnp.zeros((LRU_HALO, LANES), F32)

        xbuf[pl.ds(0, LRU_HALO), :] = jnp.where(ti == 0, 0.0, rxp_ref[...])
        xbuf[pl.ds(LRU_HALO, tm), :] = rx_ref[...]
        hbuf[pl.ds(0, LRU_HALO), :] = jnp.where(ti == 0, 0.0, hsp_ref[...])
        hbuf[pl.ds(LRU_HALO, tm), :] = hs_ref[...]

        wa, wx = wa_ref[...], wx_ref[...]
        lam_v = lam_ref[...]
        xr = _tap_sum(xbuf, w4_ref, K4, LRU_HALO - (K4 - 1), 0, tm, False) + b4_ref[...]
        a, mult, r, ig, sp = _lru_gates(xr, wa, ba_ref[...], wx, bx_ref[...], lam_v)

        dyr = dyr_ref[...]
        gl, dgl = _gelu(rg_ref[...])
        dzg_ref[...] = (dyr * hs_ref[...] * dgl).astype(BF16)

        abuf[pl.ds(0, tm), :] = a
        abuf[pl.ds(tm, LRU_HALO), :] = anc[...]
        a_s[...] = abuf[pl.ds(1, tm), :]
        b_s[...] = dyr * gl
        cs, cout = _scan_tile(a_s, b_s, h_s, p_s, gc[pl.ds(0, 1), :], seg, True)
        gc[pl.ds(0, 1), :] = cout
        anc[pl.ds(0, 1), :] = a[0:1, :]
        for s in range(SUBLANES * LRU_GROUPS):
            rows = pl.ds(s * seg, seg)
            b_s[rows, :] = h_s[rows, :] + p_s[rows, :] * cs[s]
        g = b_s[...]

        d_a = g * hbuf[pl.ds(LRU_HALO - 1, tm), :]
        gx_ = g * xr
        d_log_a = d_a * a - (gx_ * ig) * (a * a / mult)
        dga = (d_log_a * (-LRU_C * sp)) * r * (1.0 - r)
        dgx = (gx_ * mult) * ig * (1.0 - ig)
        dga_b, dgx_b = dga.astype(BF16), dgx.astype(BF16)
        dxr = g * mult * ig + _dot_nt(dga_b, wa) + _dot_nt(dgx_b, wx)
        xb = xr.astype(BF16)
        dwa_ref[...] += _dot_tn(xb, dga_b)
        dwx_ref[...] += _dot_tn(xb, dgx_b)
        st_ref[pl.ds(K4, 1), :] += _colsum(dxr)
        st_ref[pl.ds(K4 + 1, 1), :] += _colsum(dga)
        st_ref[pl.ds(K4 + 2, 1), :] += _colsum(dgx)
        st_ref[pl.ds(K4 + 3, 1), :] += _colsum(d_log_a * (-LRU_C * r)) * (-jax.nn.sigmoid(-lam_v))

        dbuf[pl.ds(0, tm), :] = dxr
        for k in range(K4):
            st_ref[pl.ds(k, 1), :] += _colsum(dxr * xbuf[pl.ds(LRU_HALO - (K4 - 1) + k, tm), :])
        dzx_ref[...] = _tap_sum(dbuf, w4_ref, K4, 0, 0, tm, True).astype(BF16)
        dbuf[pl.ds(tm, LRU_HALO), :] = dbuf[pl.ds(0, LRU_HALO), :]

    def rev(col):
        return lambda c, i: (nI - 1 - i, col + c)

    def rev_prev(col):
        return lambda c, i: (jnp.maximum((nI - 1 - i) * hb - 1, 0), col + c)

    vec = pl.BlockSpec((1, LANES), lambda c, i: (0, c))
    mat = pl.BlockSpec((None, LANES, LANES), lambda c, i: (c, 0, 0))
    big = pltpu.VMEM((tm, LANES), F32)
    halo = pltpu.VMEM((tm + LRU_HALO, LANES), F32)
    return pl.pallas_call(
        body, grid=(nC, nI),
        in_specs=[pl.BlockSpec((tm, LANES), rev(cd)),
                  pl.BlockSpec((tm, LANES), rev(0)), pl.BlockSpec((LRU_HALO, LANES), rev_prev(0)),
                  pl.BlockSpec((tm, LANES), rev(cx)), pl.BlockSpec((LRU_HALO, LANES), rev_prev(cx)),
                  pl.BlockSpec((tm, LANES), rev(cg)),
                  pl.BlockSpec((K4, LANES), lambda c, i: (0, c)), vec, mat, vec, mat, vec, vec],
        out_specs=[pl.BlockSpec((tm, LANES), rev(0)), pl.BlockSpec((tm, LANES), rev(0)),
                   pl.BlockSpec((SUBLANES, LANES), lambda c, i: (0, c)), mat, mat],
        out_shape=[jax.ShapeDtypeStruct((T, W), BF16), jax.ShapeDtypeStruct((T, W), BF16),
                   jax.ShapeDtypeStruct((SUBLANES, W), F32),
                   jax.ShapeDtypeStruct((nC, LANES, LANES), F32), jax.ShapeDtypeStruct((nC, LANES, LANES), F32)],
        scratch_shapes=[halo, halo, halo, big, big, big, big, halo,
                        pltpu.VMEM((SUBLANES, LANES), F32), pltpu.VMEM((SUBLANES, LANES), F32)],
        compiler_params=_params("parallel", "arbitrary"), name="lru_bwd")(
            dcat, hs, hs, z, z, z, w4, b4, wa, ba, wx, bx, lam)


def _mix_out_fwd(x, u, yr, wout):
    T, D = x.shape
    C, W = u.shape[1], yr.shape[1]
    tm = _tile(T, TOK_TILE)

    def body(x_ref, u_ref, yr_ref, w_ref, y_ref):
        y_ref[...] = (x_ref[...] + _dot(u_ref[...], w_ref[pl.ds(0, C), :])
                      + _dot(yr_ref[...], w_ref[pl.ds(C, W), :]))

    return pl.pallas_call(
        body, grid=(T // tm,),
        in_specs=[pl.BlockSpec((tm, D), lambda i: (i, 0)), pl.BlockSpec((tm, C), lambda i: (i, 0)),
                  pl.BlockSpec((tm, W), lambda i: (i, 0)),
                  pl.BlockSpec((C + W, D), lambda i: (0, 0), pipeline_mode=pl.Buffered(1))],
        out_specs=pl.BlockSpec((tm, D), lambda i: (i, 0)),
        out_shape=jax.ShapeDtypeStruct((T, D), F32),
        compiler_params=_params("parallel"), name="mix_out_fwd")(x, u, yr, wout)


def _mix_out_bwd(dy, u, yr, wout):
    T, D = dy.shape
    C, W = u.shape[1], yr.shape[1]
    tm = _tile(T, BWD_TILE)

    def body(dy_ref, u_ref, yr_ref, w_ref, dcat_ref, dw_ref):
        @pl.when(pl.program_id(0) == 0)
        def _():
            dw_ref[...] = jnp.zeros_like(dw_ref)

        dyb = dy_ref[...].astype(BF16)
        dcat_ref[...] = _dot_nt(dyb, w_ref[...])
        dw_ref[pl.ds(0, C), :] += _dot_tn(u_ref[...], dyb)
        dw_ref[pl.ds(C, W), :] += _dot_tn(yr_ref[...], dyb)

    return pl.pallas_call(
        body, grid=(T // tm,),
        in_specs=[pl.BlockSpec((tm, D), lambda i: (i, 0)), pl.BlockSpec((tm, C), lambda i: (i, 0)),
                  pl.BlockSpec((tm, W), lambda i: (i, 0)),
                  pl.BlockSpec((C + W, D), lambda i: (0, 0), pipeline_mode=pl.Buffered(1))],
        out_specs=[pl.BlockSpec((tm, C + W), lambda i: (i, 0)), pl.BlockSpec((C + W, D), lambda i: (0, 0))],
        out_shape=[jax.ShapeDtypeStruct((T, C + W), F32), jax.ShapeDtypeStruct((C + W, D), F32)],
        compiler_params=_params("arbitrary"), name="mix_out_bwd")(dy, u, yr, wout)


def _mix_in_bwd(dzc, dzx, dzg, x, dy, g, win):
    T, D = x.shape
    ns, ws = win.shape[0], win.shape[2]
    tm = _tile(T, BWD_TILE)
    parts = []
    for j in range(ns):
        lo = j * ws
        if lo < dzc.shape[1]:
            parts.append((0, lo))
        elif lo < dzc.shape[1] + dzx.shape[1]:
            parts.append((1, lo - dzc.shape[1]))
        else:
            parts.append((2, lo - dzc.shape[1] - dzx.shape[1]))

    def body(dzc_ref, dzx_ref, dzg_ref, x_ref, dy_ref, g_ref, w_ref, dx_ref, dw_ref, dg_ref):
        @pl.when(pl.program_id(0) == 0)
        def _():
            dw_ref[...] = jnp.zeros_like(dw_ref)
            dg_ref[...] = jnp.zeros_like(dg_ref)

        xh, r = _rms_stats(x_ref[...])
        gv = g_ref[...]
        hb = (xh * gv).astype(BF16)
        srcs = (dzc_ref, dzx_ref, dzg_ref)
        dh = jnp.zeros((tm, D), F32)
        for j, (si, off) in enumerate(parts):
            dzj = srcs[si][:, pl.ds(off, ws)]
            dh = dh + _dot_nt(dzj, w_ref[j])
            dw_ref[j] += _dot_tn(hb, dzj)
        dx_ref[...] = dy_ref[...] + _rms_bwd(dh, xh, r, gv)
        dg_ref[...] += _colsum(dh * xh)

    def tok(n):
        return pl.BlockSpec((tm, n), lambda i: (i, 0))

    vec = pl.BlockSpec((1, D), lambda i: (0, 0))
    return pl.pallas_call(
        body, grid=(T // tm,),
        in_specs=[tok(dzc.shape[1]), tok(dzx.shape[1]), tok(dzg.shape[1]), tok(D), tok(D), vec,
                  pl.BlockSpec((ns, D, ws), lambda i: (0, 0, 0), pipeline_mode=pl.Buffered(1))],
        out_specs=[tok(D), pl.BlockSpec((ns, D, ws), lambda i: (0, 0, 0)), vec],
        out_shape=[jax.ShapeDtypeStruct((T, D), F32), jax.ShapeDtypeStruct((ns, D, ws), F32),
                   jax.ShapeDtypeStruct((1, D), F32)],
        compiler_params=_params("arbitrary"), name="mix_in_bwd")(dzc, dzx, dzg, x, dy, g, win)


def _adamw(w, g, m, v, name):
    R, Cc = w.shape
    tr = _tile(R, max(SUBLANES, (1 << 19) // Cc))
    c1 = 1.0 - ADAM_B1 ** ADAM_STEP
    c2 = 1.0 - ADAM_B2 ** ADAM_STEP

    def body(w_ref, g_ref, m_ref, v_ref, d_ref, nm_ref, nv_ref):
        gv = g_ref[...]
        nm = ADAM_B1 * m_ref[...] + (1.0 - ADAM_B1) * gv
        nv = ADAM_B2 * v_ref[...] + (1.0 - ADAM_B2) * (gv * gv)
        nm_ref[...] = nm
        nv_ref[...] = nv
        d_ref[...] = -ADAM_LR * ((nm / c1) / (jnp.sqrt(nv / c2) + ADAM_EPS) + ADAM_WD * w_ref[...])

    blk = pl.BlockSpec((tr, Cc), lambda i: (i, 0))
    sds = jax.ShapeDtypeStruct((R, Cc), F32)
    return pl.pallas_call(
        body, grid=(R // tr,), in_specs=[blk] * 4, out_specs=[blk] * 3, out_shape=[sds] * 3,
        compiler_params=_params("parallel"), name=name)(w, g, m, v)


def _here():
    return lax.axis_index("x"), lax.axis_index("y"), lax.axis_index("c")


def _chip_at(x, y, m):
    return x ^ (m >> 1), y ^ (m & 1)


ANY = pl.BlockSpec(memory_space=pl.ANY)


def _place_cast(srcs, idx, dtype, name):
    n = len(srcs)
    R, Cc = srcs[0].shape
    tr = _tile(R, max(16, (1 << 18) // Cc), 16)

    def body(i_ref, *refs):
        o_ref = refs[n]
        for k in range(n):
            o_ref[k] = refs[k][...].astype(dtype)

    blk = pl.BlockSpec((tr, Cc), lambda i, s: (i, 0))
    return pl.pallas_call(
        body,
        grid_spec=pltpu.PrefetchScalarGridSpec(
            num_scalar_prefetch=1, grid=(R // tr,), in_specs=[blk] * n,
            out_specs=pl.BlockSpec((n, None, tr, Cc), lambda i, s: (0, s[1], i, 0))),
        out_shape=jax.ShapeDtypeStruct((n, N_CHIPS, R, Cc), dtype),
        compiler_params=_params("parallel"), name=name)(idx, *srcs)


def _gather_weights(lands):
    n = len(lands)

    def body(*refs):
        outs = refs[n:2 * n]
        send1, recv1, send2, recv2 = refs[2 * n:]
        x, y, c = _here()
        own = 2 * x + y

        def half(ref, chip, cc):
            rh = ref.shape[-2] // 2
            lead = (slice(None),) * (len(ref.shape) - 3)
            return ref.at[lead + (chip, pl.ds(cc * rh, rh), slice(None))]

        first = []
        for k in range(n):
            for m in (1, 2, 3):
                px, py = _chip_at(x, y, m)
                cp = pltpu.make_async_remote_copy(
                    src_ref=half(outs[k], own, c), dst_ref=half(outs[k], own, c),
                    send_sem=send1.at[k, m - 1], recv_sem=recv1.at[k, m - 1],
                    device_id=(px, py, c), device_id_type=MESH)
                cp.start()
                first.append(cp)

        passed = []
        for k in range(n):
            for m in (1, 2, 3):
                px, py = _chip_at(x, y, m)
                peer = 2 * px + py
                got = half(outs[k], peer, c)
                pltpu.make_async_remote_copy(
                    src_ref=got, dst_ref=got, send_sem=send1.at[k, m - 1], recv_sem=recv1.at[k, m - 1],
                    device_id=(px, py, c), device_id_type=MESH).wait_recv()
                cp = pltpu.make_async_remote_copy(
                    src_ref=got, dst_ref=got, send_sem=send2.at[k, m - 1], recv_sem=recv2.at[k, m - 1],
                    device_id=(x, y, 1 - c), device_id_type=MESH)
                cp.start()
                passed.append(cp)

        for k in range(n):
            for m in (1, 2, 3):
                px, py = _chip_at(x, y, m)
                other = half(outs[k], 2 * px + py, 1 - c)
                pltpu.make_async_remote_copy(
                    src_ref=other, dst_ref=other, send_sem=send2.at[k, m - 1], recv_sem=recv2.at[k, m - 1],
                    device_id=(x, y, 1 - c), device_id_type=MESH).wait_recv()
        for cp in first + passed:
            cp.wait_send()

    return pl.pallas_call(
        body, in_specs=[ANY] * n, out_specs=[ANY] * n,
        out_shape=[jax.ShapeDtypeStruct(a.shape, a.dtype) for a in lands],
        input_output_aliases={k: k for k in range(n)},
        scratch_shapes=[pltpu.SemaphoreType.DMA((n, 3)), pltpu.SemaphoreType.DMA((n, 3)),
                        pltpu.SemaphoreType.DMA((n, 3)), pltpu.SemaphoreType.DMA((n, 3))],
        name="gather_weights")(*lands)


HBM = pl.BlockSpec(memory_space=pltpu.HBM)
SEM = pl.BlockSpec(memory_space=pltpu.SEMAPHORE)
EFFECT = pltpu.SideEffectType.DATAFLOW_SIDE_EFFECTING


def _in_hbm(a):
    return pltpu.with_memory_space_constraint(a, pltpu.HBM)


def _gather_copies(land_refs, send, recv):
    x, y, c = _here()
    own = 2 * x + y
    cps = []
    for k in range(len(land_refs)):
        lead = (slice(None),) * (len(land_refs[k].shape) - 3)
        mine = land_refs[k].at[lead + (own,)]
        for m in (1, 2, 3):
            px, py = _chip_at(x, y, m)
            cps.append(pltpu.make_async_remote_copy(
                src_ref=mine, dst_ref=mine, send_sem=send.at[3 * k + m - 1], recv_sem=recv.at[3 * k + m - 1],
                device_id=(px, py, c), device_id_type=MESH))
    return cps


def _gather_start(lands, after, name):
    n = len(lands)

    def body(*refs):
        lz = refs[:n]
        send, recv = refs[n + 1], refs[n + 2]
        token = refs[-1]
        for cp in _gather_copies(lz, send, recv):
            cp.start()
        token[...] = jnp.zeros_like(token)

    hbm = [pltpu.HBM(a.shape, a.dtype) for a in lands]
    outs = pl.pallas_call(
        body, name=name,
        in_specs=[HBM] * n + [ANY],
        out_specs=[SEM, SEM] + [HBM] * n + [pl.BlockSpec(memory_space=pltpu.VMEM)],
        out_shape=[pltpu.SemaphoreType.DMA((3 * n,)), pltpu.SemaphoreType.DMA((3 * n,))] + hbm
        + [jax.ShapeDtypeStruct((SUBLANES, LANES), F32)],
        input_output_aliases={k: 2 + k for k in range(n)},
        compiler_params=pltpu.CompilerParams(has_side_effects=EFFECT),
    )(*[_in_hbm(a) for a in lands], after)
    return outs[0], outs[1], outs[2:2 + n], outs[-1]


def _gather_wait(send, recv, lands, after, name):
    n = len(lands)

    def body(*refs):
        lz = refs[:n]
        send_r, recv_r = refs[n], refs[n + 1]
        for cp in _gather_copies(lz, send_r, recv_r):
            cp.wait_send()
            cp.wait_recv()

    hbm = [pltpu.HBM(a.shape, a.dtype) for a in lands]
    return pl.pallas_call(
        body, name=name,
        in_specs=[HBM] * n + [SEM, SEM, ANY],
        out_specs=[HBM] * n, out_shape=hbm,
        input_output_aliases={k: k for k in range(n)},
        compiler_params=pltpu.CompilerParams(has_side_effects=EFFECT),
    )(*lands, send, recv, after)


def _exchange_copies(part_refs, slot_refs, send, recv):
    x, y, c = _here()
    cps = []
    for k in range(len(part_refs)):
        for m in (1, 2, 3):
            px, py = _chip_at(x, y, m)
            cps.append(pltpu.make_async_remote_copy(
                src_ref=part_refs[k].at[2 * px + py], dst_ref=slot_refs[k].at[m - 1],
                send_sem=send.at[3 * k + m - 1], recv_sem=recv.at[3 * k + m - 1],
                device_id=(px, py, c), device_id_type=MESH))
    return cps


def _exchange_start(parts, name):
    n = len(parts)
    lands = [lax.empty((N_CHIPS - 1,) + p.shape[1:], p.dtype) for p in parts]

    def body(*refs):
        ins, lz = refs[:n], refs[n:2 * n]
        send, recv = refs[2 * n], refs[2 * n + 1]
        token = refs[-1]
        for cp in _exchange_copies(ins, lz, send, recv):
            cp.start()
        token[...] = jnp.zeros_like(token)

    hbm = [pltpu.HBM(a.shape, a.dtype) for a in list(parts) + lands]
    outs = pl.pallas_call(
        body, name=name,
        in_specs=[HBM] * (2 * n),
        out_specs=[SEM, SEM] + [HBM] * (2 * n) + [pl.BlockSpec(memory_space=pltpu.VMEM)],
        out_shape=[pltpu.SemaphoreType.DMA((3 * n,)), pltpu.SemaphoreType.DMA((3 * n,))] + hbm
        + [jax.ShapeDtypeStruct((SUBLANES, LANES), F32)],
        input_output_aliases={k: 2 + k for k in range(2 * n)},
        compiler_params=pltpu.CompilerParams(has_side_effects=EFFECT),
    )(*[_in_hbm(a) for a in parts], *[_in_hbm(a) for a in lands])
    return outs[0], outs[1], outs[2:2 + n], outs[2 + n:2 + 2 * n], outs[-1]


def _exchange_wait(send, recv, parts, lands, after, name):
    n = len(parts)

    def body(*refs):
        ins, lz = refs[:n], refs[n:2 * n]
        send_r, recv_r = refs[2 * n], refs[2 * n + 1]
        for cp in _exchange_copies(ins, lz, send_r, recv_r):
            cp.wait_send()
            cp.wait_recv()

    hbm = [pltpu.HBM(a.shape, a.dtype) for a in list(parts) + list(lands)]
    outs = pl.pallas_call(
        body, name=name,
        in_specs=[HBM] * (2 * n) + [SEM, SEM, ANY],
        out_specs=[HBM] * (2 * n), out_shape=hbm,
        input_output_aliases={k: k for k in range(2 * n)},
        compiler_params=pltpu.CompilerParams(has_side_effects=EFFECT),
    )(*parts, *lands, send, recv, after)
    return outs[:n], outs[n:]


def _swap_halves_out(grads, name):
    n = len(grads)
    out_shapes = [jax.ShapeDtypeStruct((g.shape[0], g.shape[1] // 2, g.shape[2]), g.dtype) for g in grads]

    def body(*refs):
        ins, outs = refs[:n], refs[n:2 * n]
        send, recv = refs[2 * n:]
        x, y, c = _here()
        cps = []
        for k in range(n):
            rh = ins[k].shape[1] // 2
            cp = pltpu.make_async_remote_copy(
                src_ref=ins[k].at[:, pl.ds((1 - c) * rh, rh), :], dst_ref=outs[k],
                send_sem=send.at[k], recv_sem=recv.at[k], device_id=(x, y, 1 - c), device_id_type=MESH)
            cp.start()
            cps.append(cp)
        for cp in cps:
            cp.wait()

    return pl.pallas_call(
        body, in_specs=[ANY] * n, out_specs=[ANY] * n, out_shape=out_shapes,
        scratch_shapes=[pltpu.SemaphoreType.DMA((n,)), pltpu.SemaphoreType.DMA((n,))],
        name=name)(*grads)


def _swap_copies(grad_refs, land_refs, send, recv):
    x, y, c = _here()
    cps = []
    for k in range(len(grad_refs)):
        rh = grad_refs[k].shape[1] // 2
        cps.append(pltpu.make_async_remote_copy(
            src_ref=grad_refs[k].at[:, pl.ds((1 - c) * rh, rh), :], dst_ref=land_refs[k],
            send_sem=send.at[k], recv_sem=recv.at[k], device_id=(x, y, 1 - c), device_id_type=MESH))
    return cps


def _swap_start(grads, name):
    n = len(grads)
    lands = [lax.empty((g.shape[0], g.shape[1] // 2, g.shape[2]), g.dtype) for g in grads]

    def body(*refs):
        ins, lz = refs[:n], refs[n:2 * n]
        send, recv = refs[2 * n], refs[2 * n + 1]
        token = refs[-1]
        for cp in _swap_copies(ins, lz, send, recv):
            cp.start()
        token[...] = jnp.zeros_like(token)

    hbm = [pltpu.HBM(a.shape, a.dtype) for a in list(grads) + lands]
    outs = pl.pallas_call(
        body, name=name,
        in_specs=[HBM] * (2 * n),
        out_specs=[SEM, SEM] + [HBM] * (2 * n) + [pl.BlockSpec(memory_space=pltpu.VMEM)],
        out_shape=[pltpu.SemaphoreType.DMA((n,)), pltpu.SemaphoreType.DMA((n,))] + hbm
        + [jax.ShapeDtypeStruct((SUBLANES, LANES), F32)],
        input_output_aliases={k: 2 + k for k in range(2 * n)},
        compiler_params=pltpu.CompilerParams(has_side_effects=EFFECT),
    )(*[_in_hbm(a) for a in grads], *[_in_hbm(a) for a in lands])
    return outs[0], outs[1], outs[2:2 + n], outs[2 + n:2 + 2 * n], outs[-1]


def _swap_wait(send, recv, grads, lands, after, name):
    n = len(grads)

    def body(*refs):
        ins, lz = refs[:n], refs[n:2 * n]
        send_r, recv_r = refs[2 * n], refs[2 * n + 1]
        for cp in _swap_copies(ins, lz, send_r, recv_r):
            cp.wait_send()
            cp.wait_recv()

    hbm = [pltpu.HBM(a.shape, a.dtype) for a in list(grads) + list(lands)]
    outs = pl.pallas_call(
        body, name=name,
        in_specs=[HBM] * (2 * n) + [SEM, SEM, ANY],
        out_specs=[HBM] * (2 * n), out_shape=hbm,
        input_output_aliases={k: k for k in range(2 * n)},
        compiler_params=pltpu.CompilerParams(has_side_effects=EFFECT),
    )(*grads, *lands, send, recv, after)
    return outs[:n], outs[n:]


def _add_cast(g, other, cidx, name):
    ns, R, Cc = g.shape
    rh = R // 2
    tr = _tile(rh, max(16, (1 << 19) // Cc), 16)
    nb = rh // tr

    def body(c_ref, g_ref, o_ref, s_ref):
        s_ref[...] = (g_ref[...] + o_ref[...]).astype(BF16)

    return pl.pallas_call(
        body,
        grid_spec=pltpu.PrefetchScalarGridSpec(
            num_scalar_prefetch=1, grid=(ns, nb),
            in_specs=[pl.BlockSpec((None, tr, Cc), lambda k, i, c: (k, c[0] * nb + i, 0)),
                      pl.BlockSpec((None, tr, Cc), lambda k, i, c: (k, i, 0))],
            out_specs=pl.BlockSpec((None, tr, Cc), lambda k, i, c: (k, i, 0))),
        out_shape=jax.ShapeDtypeStruct((ns, rh, Cc), BF16),
        compiler_params=_params("parallel", "parallel"), name=name)(cidx, g, other)


def _sum_slots(part, got, idx, name):
    ns, rh, Cc = got.shape
    tr = _tile(rh, max(16, (1 << 18) // Cc), 16)
    nb = rh // tr

    def body(i_ref, p_ref, b_ref, o_ref):
        acc = p_ref[...].astype(F32)
        for m in range(ns):
            acc = acc + b_ref[m].astype(F32)
        o_ref[...] = acc

    return pl.pallas_call(
        body,
        grid_spec=pltpu.PrefetchScalarGridSpec(
            num_scalar_prefetch=1, grid=(nb,),
            in_specs=[pl.BlockSpec((None, tr, Cc), lambda i, s: (s[1], i, 0)),
                      pl.BlockSpec((ns, tr, Cc), lambda i, s: (0, i, 0))],
            out_specs=pl.BlockSpec((tr, Cc), lambda i, s: (s[0] * nb + i, 0))),
        out_shape=jax.ShapeDtypeStruct((2 * rh, Cc), F32),
        compiler_params=_params("parallel"), name=name)(idx, part, got)


def _share_copies(block_refs, send, recv):
    x, y, c = _here()
    cps = []
    for k, ref in enumerate(block_refs):
        rh = ref.shape[0] // 2
        mine = ref.at[pl.ds(c * rh, rh), :]
        cps.append(pltpu.make_async_remote_copy(
            src_ref=mine, dst_ref=mine, send_sem=send.at[k], recv_sem=recv.at[k],
            device_id=(x, y, 1 - c), device_id_type=MESH))
    return cps


def _share_start(blocks, name):
    n = len(blocks)

    def body(*refs):
        send, recv = refs[n], refs[n + 1]
        token = refs[-1]
        for cp in _share_copies(refs[:n], send, recv):
            cp.start()
        token[...] = jnp.zeros_like(token)

    hbm = [pltpu.HBM(a.shape, a.dtype) for a in blocks]
    outs = pl.pallas_call(
        body, name=name,
        in_specs=[HBM] * n,
        out_specs=[SEM, SEM] + [HBM] * n + [pl.BlockSpec(memory_space=pltpu.VMEM)],
        out_shape=[pltpu.SemaphoreType.DMA((n,)), pltpu.SemaphoreType.DMA((n,))] + hbm
        + [jax.ShapeDtypeStruct((SUBLANES, LANES), F32)],
        input_output_aliases={k: 2 + k for k in range(n)},
        compiler_params=pltpu.CompilerParams(has_side_effects=EFFECT),
    )(*[_in_hbm(a) for a in blocks])
    return outs[0], outs[1], outs[2:2 + n], outs[-1]


def _share_wait(send, recv, blocks, after, name):
    n = len(blocks)

    def body(*refs):
        for cp in _share_copies(refs[:n], refs[n], refs[n + 1]):
            cp.wait_send()
            cp.wait_recv()

    return pl.pallas_call(
        body, name=name,
        in_specs=[HBM] * n + [SEM, SEM, ANY],
        out_specs=[HBM] * n, out_shape=[pltpu.HBM(a.shape, a.dtype) for a in blocks],
        input_output_aliases={k: k for k in range(n)},
        compiler_params=pltpu.CompilerParams(has_side_effects=EFFECT),
    )(*blocks, send, recv, after)


def _share_halves(blocks, name):
    n = len(blocks)

    def body(*refs):
        outs = refs[n:2 * n]
        send, recv = refs[2 * n:]
        cps = _share_copies(outs, send, recv)
        for cp in cps:
            cp.start()
        for cp in cps:
            cp.wait()

    return pl.pallas_call(
        body, in_specs=[ANY] * n, out_specs=[ANY] * n,
        out_shape=[jax.ShapeDtypeStruct(b.shape, b.dtype) for b in blocks],
        input_output_aliases={k: k for k in range(n)},
        scratch_shapes=[pltpu.SemaphoreType.DMA((n,)), pltpu.SemaphoreType.DMA((n,))],
        name=name)(*blocks)


def _small_copies(p_ref, slot_ref, send, recv):
    x, y, c = _here()
    mine = slot_ref.at[4 * x + 2 * y + c]
    cps = []
    for m in range(1, N_DEV):
        peer = (x ^ (m >> 2), y ^ ((m >> 1) & 1), c ^ (m & 1))
        cps.append(pltpu.make_async_remote_copy(
            src_ref=p_ref, dst_ref=mine, send_sem=send.at[m - 1], recv_sem=recv.at[m - 1],
            device_id=peer, device_id_type=MESH))
    return cps


def _small_start(packed):
    slots = lax.empty((N_DEV,) + packed.shape, packed.dtype)

    def body(p_ref, s_ref, send, recv, p_thru, s_thru, token):
        for cp in _small_copies(p_ref, s_ref, send, recv):
            cp.start()
        token[...] = jnp.zeros_like(token)

    return pl.pallas_call(
        body, name="small_start",
        in_specs=[HBM, HBM],
        out_specs=[SEM, SEM, HBM, HBM, pl.BlockSpec(memory_space=pltpu.VMEM)],
        out_shape=[pltpu.SemaphoreType.DMA((N_DEV - 1,)), pltpu.SemaphoreType.DMA((N_DEV - 1,)),
                   pltpu.HBM(packed.shape, packed.dtype), pltpu.HBM(slots.shape, slots.dtype),
                   jax.ShapeDtypeStruct((SUBLANES, LANES), F32)],
        input_output_aliases={0: 2, 1: 3},
        compiler_params=pltpu.CompilerParams(has_side_effects=EFFECT),
    )(_in_hbm(packed), _in_hbm(slots))


def _small_wait(send, recv, packed, slots, after):
    def body(p_ref, s_ref, send_r, recv_r, after_ref, p_out, s_out):
        for cp in _small_copies(p_ref, s_ref, send_r, recv_r):
            cp.wait_send()
            cp.wait_recv()

    return pl.pallas_call(
        body, name="small_wait",
        in_specs=[HBM, HBM, SEM, SEM, ANY], out_specs=[HBM, HBM],
        out_shape=[pltpu.HBM(packed.shape, packed.dtype), pltpu.HBM(slots.shape, slots.dtype)],
        input_output_aliases={0: 0, 1: 1},
        compiler_params=pltpu.CompilerParams(has_side_effects=EFFECT),
    )(packed, slots, send, recv, after)


def _sum_devices(packed, slots, me):
    n, R, _ = slots.shape
    tr = _tile(R, 1024)

    def body(m_ref, p_ref, s_ref, o_ref):
        own = p_ref[...]
        acc = None
        for d in range(n):
            term = jnp.where(m_ref[0] == d, own, s_ref[d])
            acc = term if acc is None else acc + term
        o_ref[...] = acc

    return pl.pallas_call(
        body,
        grid_spec=pltpu.PrefetchScalarGridSpec(
            num_scalar_prefetch=1, grid=(R // tr,),
            in_specs=[pl.BlockSpec((tr, LANES), lambda i, m: (i, 0)),
                      pl.BlockSpec((n, tr, LANES), lambda i, m: (0, i, 0))],
            out_specs=pl.BlockSpec((tr, LANES), lambda i, m: (i, 0))),
        out_shape=jax.ShapeDtypeStruct((R, LANES), F32),
        compiler_params=_params("parallel"), name="sum_devices")(me, packed, slots)


def _pack(arrs):
    rows, parts = [], []
    for a in arrs:
        flat = a.reshape(-1)
        r = -(-flat.shape[0] // (SUBLANES * LANES)) * SUBLANES
        parts.append(jnp.pad(flat, (0, r * LANES - flat.shape[0])).reshape(r, LANES))
        rows.append(r)
    return jnp.concatenate(parts, axis=0), rows


def _unpack(packed, rows, shapes):
    out, r0 = [], 0
    for r, shp in zip(rows, shapes):
        size = math.prod(shp)
        out.append(packed[r0:r0 + r].reshape(-1)[:size].reshape(shp))
        r0 += r
    return out


def _block_diag(w, per):
    H, dh, _ = w.shape
    w4 = w.reshape(H // per, per, dh, dh)
    eye = jnp.eye(per, dtype=w.dtype)
    return (w4[:, :, :, None, :] * eye[None, :, None, :, None]).reshape(H // per, per * dh, per * dh)


def _block_diag_take(d, per):
    n, s, _ = d.shape
    dh = s // per
    d5 = d.reshape(n, per, dh, per, dh)
    return jnp.stack([d5[:, h, :, h, :] for h in range(per)], axis=1).reshape(n * per, dh, dh)


def kernel(x, ffn1_norm, ffn1_w_gate, ffn1_w_up, ffn1_w_down, mix_norm, w_in, conv_dw, conv_dw_bias, conv_ln_g, conv_ln_b, lru_conv_w, lru_conv_b, lru_w_a, lru_b_a, lru_w_x, lru_b_x, lru_lambda, w_out, ffn2_norm, ffn2_w_gate, ffn2_w_up, ffn2_w_down, final_norm, loss_target, m_ffn1_norm, m_ffn1_w_gate, m_ffn1_w_up, m_ffn1_w_down, m_mix_norm, m_w_in, m_conv_dw, m_conv_dw_bias, m_conv_ln_g, m_conv_ln_b, m_lru_conv_w, m_lru_conv_b, m_lru_w_a, m_lru_b_a, m_lru_w_x, m_lru_b_x, m_lru_lambda, m_w_out, m_ffn2_norm, m_ffn2_w_gate, m_ffn2_w_up, m_ffn2_w_down, m_final_norm, v_ffn1_norm, v_ffn1_w_gate, v_ffn1_w_up, v_ffn1_w_down, v_mix_norm, v_w_in, v_conv_dw, v_conv_dw_bias, v_conv_ln_g, v_conv_ln_b, v_lru_conv_w, v_lru_conv_b, v_lru_w_a, v_lru_b_a, v_lru_w_x, v_lru_b_x, v_lru_lambda, v_w_out, v_ffn2_norm, v_ffn2_w_gate, v_ffn2_w_up, v_ffn2_w_down, v_final_norm):
    names = ['ffn1_norm', 'ffn1_w_gate', 'ffn1_w_up', 'ffn1_w_down', 'mix_norm', 'w_in', 'conv_dw', 'conv_dw_bias',
             'conv_ln_g', 'conv_ln_b', 'lru_conv_w', 'lru_conv_b', 'lru_w_a', 'lru_b_a', 'lru_w_x', 'lru_b_x',
             'lru_lambda', 'w_out', 'ffn2_norm', 'ffn2_w_gate', 'ffn2_w_up', 'ffn2_w_down', 'final_norm']
    env = dict(locals())
    W = {n: env[n] for n in names}
    M = {n: env['m_' + n] for n in names}
    V = {n: env['v_' + n] for n in names}

    xi, yi, ci = _here()
    chip = 2 * xi + yi
    cidx = ci.astype(jnp.int32).reshape(1)
    T, D = x.shape[-2], x.shape[-1]
    xs = x.reshape(T, D)
    tgt = loss_target.reshape(T, D)
    K, Cs = conv_dw.shape
    C = conv_dw_bias.shape[0]
    Wl = lru_conv_b.shape[0]
    K4 = lru_conv_w.shape[0]
    heads, dh, _ = lru_w_a.shape
    per = LANES // dh

    def row(v):
        return v.reshape(1, -1)

    tform = ('ffn1_w_gate', 'ffn1_w_up', 'ffn2_w_gate', 'ffn2_w_up')
    for n in tform:
        W[n], M[n], V[n] = W[n].T, M[n].T, V[n].T
    kp = -(-K // SUBLANES) * SUBLANES
    taps = jnp.concatenate([conv_dw, jnp.zeros((kp - K, Cs), F32), lru_conv_w,
                            jnp.zeros((2 * SUBLANES - K4, Cs), F32)], axis=0)
    idx = jnp.stack([ci, chip]).astype(jnp.int32)
    (wff1,) = _gather_weights([_place_cast([W['ffn1_w_gate'], W['ffn1_w_up'], ffn1_w_down], idx, BF16, "place_ffn1")])
    mixl = [_place_cast([w_in], idx, BF16, "place_w_in"), _place_cast([w_out], idx, BF16, "place_w_out"),
            _place_cast([taps], idx, F32, "place_taps")]
    msend, mrecv, mixl, mtok = _gather_start(mixl, wff1, "gather_mix_start")
    ff2l = _place_cast([W['ffn2_w_gate'], W['ffn2_w_up'], ffn2_w_down], idx, BF16, "place_ffn2")
    fsend, frecv, ff2l, ftok = _gather_start([ff2l], mtok, "gather_ffn2_start")
    wa_bd = _block_diag(lru_w_a, per).astype(BF16)
    wx_bd = _block_diag(lru_w_x, per).astype(BF16)

    x1, a1, b1 = _ffn_fwd(xs, row(ffn1_norm) + ftok[0:1, 0:1], wff1, "ffn1_fwd")
    win, wout, taps = _gather_wait(msend, mrecv, mixl, x1, "gather_mix_wait")
    win, wout, taps = win[0], wout.reshape(-1, D), taps[0]
    conv_w_full = taps[:, :K].transpose(1, 0, 2).reshape(K, N_CHIPS * Cs)
    lru_w4_full = taps[:, kp:kp + K4].transpose(1, 0, 2).reshape(K4, N_CHIPS * Cs)
    z = _mix_in_fwd(x1, row(mix_norm), win)
    u, u1 = _conv_fwd(z, conv_w_full, row(conv_dw_bias), row(conv_ln_g), row(conv_ln_b))
    yr, hs = _lru_fwd(z, 2 * C, lru_w4_full, row(lru_conv_b), wa_bd, row(lru_b_a), wx_bd, row(lru_b_x),
                      row(lru_lambda))
    x2 = _mix_out_fwd(x1, u, yr, wout)
    (wff2,) = _gather_wait(fsend, frecv, ff2l, x2, "gather_ffn2_wait")
    dx3, a2, b2, loss_blk, d_final = _ffn_fwd(x2, row(ffn2_norm), wff2, "ffn2_fwd", head=(row(final_norm), tgt))

    dx2, da2, db2, p2, hb2, dyh2, d_ffn2n = _ffn_bwd_tok(dx3, x2, row(ffn2_norm), a2, b2, wff2, "ffn2_bwd")
    dwg2, dwu2, dwd2 = _ffn_wgrad([([da2, db2], hb2), ([p2], dyh2)], ftok, "ffn2_wgrad")
    wsend, wrecv, f2g, f2o, wtok = _swap_start([dwg2, dwu2, dwd2], "swap_ffn2_start")
    dcat, dwout = _mix_out_bwd(dx2, u, yr, wout)
    dzc, cst = _conv_bwd(dcat, u1, z, conv_w_full, row(conv_ln_g) + wtok[0:1, 0:1], row(conv_ln_b))
    dzx, dzg, lst, dwa_bd, dwx_bd = _lru_bwd(dcat, C, hs, z, 2 * C, lru_w4_full, row(lru_conv_b), wa_bd,
                                              row(lru_b_a), wx_bd, row(lru_b_x), row(lru_lambda))
    dx1, dwin, d_mixn = _mix_in_bwd(dzc, dzx, dzg, x1, dx2, row(mix_norm), win)

    early_names = ['w_in', 'w_out', 'ffn2_w_gate', 'ffn2_w_up', 'ffn2_w_down']
    xsend, xrecv, mixg, mixo, xtok = _swap_start([dwin, dwout.reshape(N_CHIPS, -1, D)], "swap_mix_start")
    f2g, f2o = _swap_wait(wsend, wrecv, f2g, f2o, xtok, "swap_ffn2_wait")
    f2_parts = [_add_cast(g, o, cidx, "add_cast_" + n) for g, o, n in zip(f2g, f2o, early_names[2:])]
    mixg, mixo = _swap_wait(xsend, xrecv, mixg, mixo, f2_parts[-1], "swap_mix_wait")
    e_parts = [_add_cast(g, o, cidx, "add_cast_" + n) for g, o, n in zip(mixg, mixo, early_names[:2])] + f2_parts
    esend, erecv, e_parts, e_lands, etok = _exchange_start(e_parts, "exchange_early_start")

    dx0, da1, db1, p1, hb1, dyh1, d_ffn1n = _ffn_bwd_tok(dx1, xs, row(ffn1_norm) + etok[0:1, 0:1], a1, b1, wff1,
                                                         "ffn1_bwd")

    small_names = ['ffn1_norm', 'mix_norm', 'conv_dw', 'conv_dw_bias', 'conv_ln_g', 'conv_ln_b', 'lru_conv_w',
                   'lru_conv_b', 'lru_w_a', 'lru_b_a', 'lru_w_x', 'lru_b_x', 'lru_lambda', 'ffn2_norm',
                   'final_norm']
    small = {
        'ffn1_norm': d_ffn1n, 'mix_norm': d_mixn, 'conv_dw': cst[:K], 'conv_dw_bias': cst[K + 1],
        'conv_ln_g': cst[K + 2], 'conv_ln_b': cst[K + 3], 'lru_conv_w': lst[:K4], 'lru_conv_b': lst[K4],
        'lru_w_a': _block_diag_take(dwa_bd, per), 'lru_b_a': lst[K4 + 1],
        'lru_w_x': _block_diag_take(dwx_bd, per), 'lru_b_x': lst[K4 + 2], 'lru_lambda': lst[K4 + 3],
        'ffn2_norm': d_ffn2n, 'final_norm': d_final,
    }
    packed, rows = _pack([small[n] for n in small_names] + [loss_blk[0:1, 0:1]])
    ssend, srecv, packed, sslots, stok = _small_start(packed)

    gu_names, d_names = ['ffn1_w_gate', 'ffn1_w_up'], ['ffn1_w_down']
    gu = _ffn_wgrad([([da1, db1], hb1)], stok, "ffn1_wgrad_gu", swap=True)
    gu_parts = [_add_cast(g, o, cidx, "add_cast_" + n) for g, o, n in zip(gu[:2], gu[2:], gu_names)]
    gsend, grecv, gu_parts, gu_lands, gtok = _exchange_start(gu_parts, "exchange_gu_start")
    dn = _ffn_wgrad([([p1], dyh1)], gtok, "ffn1_wgrad_d", swap=True)
    d_parts = [_add_cast(g, o, cidx, "add_cast_" + n) for g, o, n in zip(dn[:1], dn[1:], d_names)]
    dsend, drecv, d_parts, d_lands, ltok = _exchange_start(d_parts, "exchange_d_start")
    e_parts, e_slots = _exchange_wait(esend, erecv, e_parts, e_lands, ltok, "exchange_early_wait")
    delta, new_m, new_v = {}, {}, {}

    def finish(group, parts, slots, tag):
        halves = [_sum_slots(p, b, idx, "sum_slots_" + n) for p, b, n in zip(parts, slots, group)]
        for n, g in zip(group, _share_halves(halves, "share_halves_" + tag)):
            G[n] = g
            delta[n], new_m[n], new_v[n] = _adamw(W[n], g, M[n], V[n], "adamw_" + n)

    G = {}
    e_halves = [_sum_slots(p, b, idx, "sum_slots_" + n) for p, b, n in zip(e_parts, e_slots, early_names)]
    hsend, hrecv, e_halves, htok = _share_start(e_halves, "share_early_start")

    full_shapes = [(K, C) if n == 'conv_dw' else (K4, Wl) if n == 'lru_conv_w' else W[n].shape for n in small_names]
    packed, sslots = _small_wait(ssend, srecv, packed, sslots, htok)
    summed = _sum_devices(packed, sslots, (4 * xi + 2 * yi + ci).astype(jnp.int32).reshape(1))
    *small_sums, loss_sum = _unpack(summed, rows, full_shapes + [(1, 1)])
    for n, gsum in zip(small_names, small_sums):
        if n == 'conv_dw':
            gsum = lax.dynamic_slice_in_dim(gsum, chip * Cs, Cs, axis=1)
        elif n == 'lru_conv_w':
            gsum = lax.dynamic_slice_in_dim(gsum, chip * lru_conv_w.shape[1], lru_conv_w.shape[1], axis=1)
        G[n] = gsum

    pw, prow = _pack([W[n] for n in small_names])
    pg, _ = _pack([G[n] for n in small_names])
    pm, _ = _pack([M[n] for n in small_names])
    pv, _ = _pack([V[n] for n in small_names])
    sd, sm, sv = _adamw(pw, pg, pm, pv, "adamw_small")
    shapes = [W[n].shape for n in small_names]
    for n, a, b, c_ in zip(small_names, _unpack(sd, prow, shapes), _unpack(sm, prow, shapes),
                           _unpack(sv, prow, shapes)):
        delta[n], new_m[n], new_v[n] = a, b, c_

    for n, g in zip(early_names, _share_wait(hsend, hrecv, e_halves, sd, "share_early_wait")):
        G[n] = g
        delta[n], new_m[n], new_v[n] = _adamw(W[n], g, M[n], V[n], "adamw_" + n)
    done = sd[0:SUBLANES] + delta[early_names[-1]][0:SUBLANES, 0:LANES]
    gu_parts, gu_slots = _exchange_wait(gsend, grecv, gu_parts, gu_lands, done, "exchange_gu_wait")
    d_parts, d_slots = _exchange_wait(dsend, drecv, d_parts, d_lands, gu_slots[0], "exchange_d_wait")
    finish(gu_names + d_names, list(gu_parts) + list(d_parts), list(gu_slots) + list(d_slots), "last")

    loss = loss_sum[0, 0]
    grad_x = dx0.reshape(x.shape)
    for n in tform:
        G[n], delta[n], new_m[n], new_v[n] = G[n].T, delta[n].T, new_m[n].T, new_v[n].T
    return (loss, grad_x, *[G[n] for n in names], *[delta[n] for n in names],
            *[new_m[n] for n in names], *[new_v[n] for n in names])
```
